```python
import math
import jax
import jax.numpy as jnp
from jax import lax
import numpy as np

D_MODEL = 1024
BATCH = 16
SEQ = 2048
DEPTH = 2

CTX_LEN = 256
GRID_W = 64
HEAD_DIM = 64
MIX_WIDTH = D_MODEL
S5_WIDTH = MIX_WIDTH // 4
S5_GROUP = 16
S5_GROUPS = S5_WIDTH // S5_GROUP
S5_STATE = 64
ATT_WIDTH = MIX_WIDTH // 2
ATT_HEADS = ATT_WIDTH // HEAD_DIM
ATT_KV_HEADS = ATT_HEADS // 4
ATT_REP = ATT_HEADS // ATT_KV_HEADS
KV_WIDTH = ATT_KV_HEADS * HEAD_DIM
WINDOW = 128
ATT_BLOCK = 128
RET_WIDTH = MIX_WIDTH - S5_WIDTH - ATT_WIDTH
RET_HEADS = RET_WIDTH // HEAD_DIM
RET_CHUNK = 128
IN_SIZES = (S5_WIDTH, ATT_WIDTH, KV_WIDTH, KV_WIDTH, RET_WIDTH, RET_WIDTH, RET_WIDTH, RET_WIDTH)
IN_WIDTH = sum(IN_SIZES)
D_FF = -(-8 * D_MODEL // (3 * 128)) * 128
N_EXPERTS = 8
TOP_K = 2
D_FF_EXPERT = 7 * D_MODEL // 2
N_DENSE_LAYERS = (DEPTH + 1) // 2
N_MOE_LAYERS = DEPTH // 2
ALPHA = (2 * DEPTH) ** 0.25
BETA = (8 * DEPTH) ** -0.25
LN_EPS = 1e-5
ROPE_BASE = 10000.0
NEG_INF = -1e30

kernel_name = 'hybrid_s5_swa_retention_moe_dit'


def layer_norm(x):
    xf = x.astype(jnp.float32)
    mu = jnp.mean(xf, axis=-1, keepdims=True)
    var = jnp.mean(jnp.square(xf - mu), axis=-1, keepdims=True)
    return ((xf - mu) * lax.rsqrt(var + LN_EPS)).astype(x.dtype)


def modulate(x, shift, scale):
    return layer_norm(x) * (1.0 + scale) + shift


def post_norm(x, update, gain, bias):
    return layer_norm(ALPHA * x + update) * gain + bias


def rope(x, pos):
    d = x.shape[-1]
    inv_freq = ROPE_BASE ** (-jnp.arange(0, d, 2, dtype=jnp.float32) / d)
    ang = pos[:, None] * inv_freq[None, :]
    cos = jnp.cos(ang)[None, :, None, :]
    sin = jnp.sin(ang)[None, :, None, :]
    x1, x2 = jnp.split(x.astype(jnp.float32), 2, axis=-1)
    return jnp.concatenate([x1 * cos - x2 * sin, x1 * sin + x2 * cos], axis=-1).astype(x.dtype)


def axial_rope(x, rows, cols):
    xr, xc = jnp.split(x, 2, axis=-1)
    return jnp.concatenate([rope(xr, rows), rope(xc, cols)], axis=-1)


def diag_linear_scan(lam_bar, bu):
    a = jnp.broadcast_to(lam_bar, bu.shape)

    def combine(e1, e2):
        a1, b1 = e1
        a2, b2 = e2
        return a1 * a2, a2 * b1 + b2

    return lax.associative_scan(combine, (a, bu), axis=1)[1]


def s5_direction(ux, uz, lam_re, lam_im, log_step, b_re, b_im, c_re, c_im, reverse, need_ctx):
    f32 = jnp.float32
    lam = lax.complex(lam_re.astype(f32), lam_im.astype(f32))
    lam_dt = lam * jnp.exp(log_step.astype(f32))[:, None]
    lam_bar = jnp.exp(lam_dt)
    b_bar = lax.complex(b_re.astype(f32), b_im.astype(f32)) * ((lam_bar - 1.0) / lam)[:, :, None]
    c_mat = lax.complex(c_re.astype(f32), c_im.astype(f32))
    if reverse:
        ux, uz = jnp.flip(ux, axis=1), jnp.flip(uz, axis=1)

    def drive(u):
        return lax.complex(jnp.einsum('btgc,gnc->btgn', u, b_bar.real),
                           jnp.einsum('btgc,gnc->btgn', u, b_bar.imag))

    def read(h):
        return jnp.einsum('btgn,gcn->btgc', h, c_mat).real

    h_z = diag_linear_scan(lam_bar, drive(uz))
    steps = jnp.arange(1, ux.shape[1] + 1, dtype=f32)[:, None, None]
    h_x = diag_linear_scan(lam_bar, drive(ux)) + jnp.exp(steps * lam_dt) * h_z[:, -1:]
    y_x = read(h_x)
    y_z = read(h_z) if need_ctx else None
    if reverse:
        y_x = jnp.flip(y_x, axis=1)
        y_z = jnp.flip(y_z, axis=1) if need_ctx else None
    return y_x, y_z


def s5_mixer(u, L, lam_re, lam_im, log_step, b_re, b_im, c_re, c_im, d_skip, w_glu, b_glu, need_ctx):
    B, n_tok, _ = u.shape
    ug = u.astype(jnp.float32).reshape(B, n_tok, S5_GROUPS, S5_GROUP)
    uz, ux = ug[:, :L], ug[:, L:]
    yf_x, yf_z = s5_direction(ux, uz, lam_re[0], lam_im[0], log_step[0], b_re[0], b_im[0],
                              c_re[0], c_im[0], False, need_ctx)
    yb_x, yb_z = s5_direction(ux, uz, lam_re[1], lam_im[1], log_step[1], b_re[1], b_im[1],
                              c_re[1], c_im[1], True, need_ctx)
    y = yf_x + yb_x
    u_sel = ux
    if need_ctx:
        y = jnp.concatenate([yf_z + yb_z, y], axis=1)
        u_sel = ug
    y = y + d_skip.astype(jnp.float32).reshape(S5_GROUPS, S5_GROUP) * u_sel
    g = jax.nn.gelu(y.reshape(B, -1, S5_WIDTH)).astype(u.dtype)
    return g * jax.nn.sigmoid(g @ w_glu + b_glu)


def window_attention_latent(q, k, v, kc, vc, sink):
    B, T, _, _ = q.shape
    nb = T // ATT_BLOCK
    nk = 3 * ATT_BLOCK
    n_ctx = kc.shape[1]
    scale = HEAD_DIM ** -0.5
    qb = q.reshape(B, nb, ATT_BLOCK, ATT_KV_HEADS, ATT_REP, HEAD_DIM)
    pad = ((0, 0), (ATT_BLOCK, ATT_BLOCK), (0, 0), (0, 0))

    def band(t):
        tp = jnp.pad(t, pad).reshape(B, nb + 2, ATT_BLOCK, ATT_KV_HEADS, HEAD_DIM)
        return jnp.concatenate([tp[:, :-2], tp[:, 1:-1], tp[:, 2:]], axis=2)

    kw, vw = band(k), band(v)
    blk = jnp.arange(nb)[:, None, None] * ATT_BLOCK
    q_pos = blk + jnp.arange(ATT_BLOCK)[None, :, None]
    k_pos = blk - ATT_BLOCK + jnp.arange(nk)[None, None, :]
    valid = (jnp.abs(k_pos - q_pos) <= WINDOW) & (k_pos >= 0) & (k_pos < T)
    s_loc = jnp.einsum('bnqgrd,bnkgd->bgrnqk', qb, kw).astype(jnp.float32) * scale
    s_loc = jnp.where(valid, s_loc, NEG_INF)
    s_ctx = jnp.einsum('bnqgrd,bcgd->bgrnqc', qb, kc).astype(jnp.float32) * scale
    s_sink = jnp.broadcast_to(sink.astype(jnp.float32).reshape(1, ATT_KV_HEADS, ATT_REP, 1, 1, 1),
                              s_loc.shape[:-1] + (1,))
    p = jax.nn.softmax(jnp.concatenate([s_loc, s_ctx, s_sink], axis=-1), axis=-1).astype(v.dtype)
    o = (jnp.einsum('bgrnqk,bnkgd->bnqgrd', p[..., :nk], vw)
         + jnp.einsum('bgrnqc,bcgd->bnqgrd', p[..., nk:nk + n_ctx], vc))
    return o.reshape(B, T, ATT_WIDTH)


def context_attention(q, k, v, sink):
    B, L, _, _ = q.shape
    qg = q.reshape(B, L, ATT_KV_HEADS, ATT_REP, HEAD_DIM)
    s = jnp.einsum('blgrd,bcgd->bgrlc', qg, k).astype(jnp.float32) * HEAD_DIM ** -0.5
    s_sink = jnp.broadcast_to(sink.astype(jnp.float32).reshape(1, ATT_KV_HEADS, ATT_REP, 1, 1),
                              s.shape[:-1] + (1,))
    p = jax.nn.softmax(jnp.concatenate([s, s_sink], axis=-1), axis=-1).astype(v.dtype)
    o = jnp.einsum('bgrlc,bcgd->blgrd', p[..., :k.shape[1]], v)
    return o.reshape(B, L, ATT_WIDTH)


def window_attention_mixer(q, k, v, L, rows, cols, sink, need_ctx):
    B, n_tok, _ = q.shape
    q = q.reshape(B, n_tok, ATT_HEADS, HEAD_DIM)
    k = k.reshape(B, n_tok, ATT_KV_HEADS, HEAD_DIM)
    v = v.reshape(B, n_tok, ATT_KV_HEADS, HEAD_DIM)
    kz, vz = k[:, :L], v[:, :L]
    qx = axial_rope(q[:, L:], rows, cols)
    kx = axial_rope(k[:, L:], rows, cols)
    ox = window_attention_latent(qx, kx, v[:, L:], kz, vz, sink)
    if need_ctx:
        return jnp.concatenate([context_attention(q[:, :L], kz, vz, sink), ox], axis=1)
    return ox


def retention_chunkwise(q, k, v, log_gamma, s0, include_diag):
    B, T, H, _ = q.shape
    n = T // RET_CHUNK
    qc, kc, vc = (t.reshape(B, n, RET_CHUNK, H, -1) for t in (q, k, v))
    idx = jnp.arange(RET_CHUNK, dtype=jnp.float32)
    diff = idx[:, None] - idx[None, :]
    mask = (diff >= 0) if include_diag else (diff > 0)
    decay = jnp.where(mask, jnp.exp(log_gamma[:, None, None] * jnp.maximum(diff, 0.0)), 0.0)
    scores = jnp.einsum('bnihd,bnjhd->bnhij', qc, kc) * decay
    o_intra = jnp.einsum('bnhij,bnjhe->bnihe', scores, vc)
    k_w = jnp.exp(log_gamma[None, :] * (RET_CHUNK - 1 - idx)[:, None])
    contrib = jnp.einsum('bnjhd,jh,bnjhe->bnhde', kc, k_w, vc)
    chunk_decay = jnp.exp(log_gamma * RET_CHUNK)[None, :, None, None]

    def step(s, cb):
        return chunk_decay * s + cb, s

    s_final, s_before = lax.scan(step, s0, jnp.moveaxis(contrib, 1, 0))
    q_w = jnp.exp(log_gamma[None, :] * (idx + 1.0)[:, None])
    o_cross = jnp.einsum('bnihd,ih,nbhde->bnihe', qc, q_w, s_before)
    return (o_intra + o_cross).reshape(B, T, H, -1), s_final


def retention_final_state(k, v, log_gamma):
    T = k.shape[1]
    w = jnp.exp(log_gamma[None, :] * (T - 1 - jnp.arange(T, dtype=jnp.float32))[:, None])
    return jnp.einsum('bthd,th,bthe->bhde', k, w, v)


def retention_mixer(q, k, v, g, L, pos, log_gamma, need_ctx):
    B, n_tok, _ = q.shape
    f32 = jnp.float32

    def heads(t):
        return t.astype(f32).reshape(B, n_tok, RET_HEADS, HEAD_DIM)

    q, k, v = heads(q), heads(k) * HEAD_DIM ** -0.5, heads(v)
    qz, qx = q[:, :L], rope(q[:, L:], pos)
    kz, kx = k[:, :L], rope(k[:, L:], pos)
    vz, vx = v[:, :L], v[:, L:]
    lg_f, lg_b = log_gamma[0].astype(f32), log_gamma[1].astype(f32)

    def flip(t):
        return jnp.flip(t, axis=1)

    if need_ctx:
        zeros = jnp.zeros((B, RET_HEADS, HEAD_DIM, HEAD_DIM), f32)
        oz_f, s_f = retention_chunkwise(qz, kz, vz, lg_f, zeros, True)
        oz_b, s_b = retention_chunkwise(flip(qz), flip(kz), flip(vz), lg_b, zeros, False)
    else:
        s_f = retention_final_state(kz, vz, lg_f)
        s_b = retention_final_state(flip(kz), flip(vz), lg_b)
    ox_f, _ = retention_chunkwise(qx, kx, vx, lg_f, s_f, True)
    ox_b, _ = retention_chunkwise(flip(qx), flip(kx), flip(vx), lg_b, s_b, False)
    o = ox_f + flip(ox_b)
    g_sel = g[:, L:]
    if need_ctx:
        o = jnp.concatenate([oz_f + flip(oz_b), o], axis=1)
        g_sel = g
    o = layer_norm(o).reshape(B, -1, RET_WIDTH).astype(g.dtype)
    return o * jax.nn.silu(g_sel)


def hybrid_mixer(hx, hz, rows, cols, pos, w_in, lam_re, lam_im, log_step, b_re, b_im, c_re, c_im,
                 d_skip, w_glu, b_glu, sink, log_gamma, w_out, need_ctx):
    L = hz.shape[1]
    p = jnp.concatenate([hz, hx], axis=1) @ w_in
    splits = [int(i) for i in np.cumsum(IN_SIZES)[:-1]]
    u, qa, ka, va, qr, kr, vr, gr = jnp.split(p, splits, axis=-1)
    y_s5 = s5_mixer(u, L, lam_re, lam_im, log_step, b_re, b_im, c_re, c_im, d_skip, w_glu, b_glu, need_ctx)
    y_att = window_attention_mixer(qa, ka, va, L, rows, cols, sink, need_ctx)
    y_ret = retention_mixer(qr, kr, vr, gr, L, pos, log_gamma, need_ctx)
    out = jnp.concatenate([y_s5, y_att, y_ret], axis=-1) @ w_out
    if need_ctx:
        return out[:, L:], out[:, :L]
    return out, None


def swiglu(h, w1, w3, w2):
    return (jax.nn.silu(h @ w1) * (h @ w3)) @ w2


def moe_swiglu(h, router, w1, w3, w2):
    logits = (h @ router).astype(jnp.float32)
    top_val, top_idx = lax.top_k(logits, TOP_K)
    top_w = jax.nn.softmax(top_val, axis=-1)
    gates = jnp.sum(jax.nn.one_hot(top_idx, N_EXPERTS, dtype=jnp.float32) * top_w[..., None], axis=-2)
    gates = gates.astype(h.dtype)
    out = gates[..., 0:1] * swiglu(h, w1[0], w3[0], w2[0])
    for e in range(1, N_EXPERTS):
        out = out + gates[..., e:e + 1] * swiglu(h, w1[e], w3[e], w2[e])
    return out


def channel_mixer(h, l, ffn_w1, ffn_w3, ffn_w2, moe_router, moe_w1, moe_w3, moe_w2):
    i = l // 2
    if l % 2 == 0:
        return swiglu(h, ffn_w1[i], ffn_w3[i], ffn_w2[i])
    return moe_swiglu(h, moe_router[i], moe_w1[i], moe_w3[i], moe_w2[i])


def setup_inputs(seed: int = 0) -> dict:
    key = jax.random.key(seed)
    ks = jax.random.split(key, 32)
    f32 = jnp.float32

    def nrm(k, shape, scale):
        return jax.random.normal(k, shape, f32) * scale

    s5_shape = (DEPTH, 2, S5_GROUPS, S5_STATE)
    n_idx = jnp.arange(S5_STATE, dtype=f32)
    ret_base = jnp.log(1.0 - 2.0 ** (-5.0 - jnp.arange(RET_HEADS, dtype=f32)))
    return {
        'x': nrm(ks[0], (BATCH, SEQ, D_MODEL), 1.0),
        'c': nrm(ks[1], (BATCH, D_MODEL), 1.0),
        'ctx': nrm(ks[2], (BATCH, CTX_LEN, D_MODEL), 1.0),
        'c_ctx': nrm(ks[3], (D_MODEL,), 1.0),
        'w_mod': nrm(ks[4], (DEPTH, D_MODEL, 6 * D_MODEL), 0.5 * D_MODEL ** -0.5),
        'b_mod': nrm(ks[5], (DEPTH, 6 * D_MODEL), 0.02),
        'w_in': nrm(ks[6], (DEPTH, D_MODEL, IN_WIDTH), D_MODEL ** -0.5),
        's5_lam_re': -0.5 + nrm(ks[7], s5_shape, 0.01),
        's5_lam_im': math.pi * n_idx + nrm(ks[8], s5_shape, 0.01),
        's5_log_step': jax.random.uniform(ks[9], (DEPTH, 2, S5_GROUPS), f32, math.log(0.001), math.log(0.1)),
        's5_b_re': nrm(ks[10], (DEPTH, 2, S5_GROUPS, S5_STATE, S5_GROUP), (2 * S5_GROUP) ** -0.5),
        's5_b_im': nrm(ks[11], (DEPTH, 2, S5_GROUPS, S5_STATE, S5_GROUP), (2 * S5_GROUP) ** -0.5),
        's5_c_re': nrm(ks[12], (DEPTH, 2, S5_GROUPS, S5_GROUP, S5_STATE), (2 * S5_STATE) ** -0.5),
        's5_c_im': nrm(ks[13], (DEPTH, 2, S5_GROUPS, S5_GROUP, S5_STATE), (2 * S5_STATE) ** -0.5),
        's5_d': nrm(ks[14], (DEPTH, S5_WIDTH), 1.0),
        's5_w_glu': nrm(ks[15], (DEPTH, S5_WIDTH, S5_WIDTH), S5_WIDTH ** -0.5),
        's5_b_glu': nrm(ks[16], (DEPTH, S5_WIDTH), 0.02),
        'attn_sink': nrm(ks[17], (DEPTH, ATT_HEADS), 0.5),
        'ret_log_gamma': ret_base * (1.0 + nrm(ks[18], (DEPTH, 2, RET_HEADS), 0.05)),
        'w_out': nrm(ks[19], (DEPTH, MIX_WIDTH, D_MODEL), BETA * MIX_WIDTH ** -0.5),
        'ln1_g': 1.0 + nrm(ks[20], (DEPTH, D_MODEL), 0.02),
        'ln1_b': nrm(ks[21], (DEPTH, D_MODEL), 0.02),
        'ln2_g': 1.0 + nrm(ks[22], (DEPTH, D_MODEL), 0.02),
        'ln2_b': nrm(ks[23], (DEPTH, D_MODEL), 0.02),
        'ffn_w1': nrm(ks[24], (N_DENSE_LAYERS, D_MODEL, D_FF), D_MODEL ** -0.5),
        'ffn_w3': nrm(ks[25], (N_DENSE_LAYERS, D_MODEL, D_FF), D_MODEL ** -0.5),
        'ffn_w2': nrm(ks[26], (N_DENSE_LAYERS, D_FF, D_MODEL), BETA * D_FF ** -0.5),
        'moe_router': nrm(ks[27], (N_MOE_LAYERS, D_MODEL, N_EXPERTS), D_MODEL ** -0.5),
        'moe_w1': nrm(ks[28], (N_MOE_LAYERS, N_EXPERTS, D_MODEL, D_FF_EXPERT), D_MODEL ** -0.5),
        'moe_w3': nrm(ks[29], (N_MOE_LAYERS, N_EXPERTS, D_MODEL, D_FF_EXPERT), D_MODEL ** -0.5),
        'moe_w2': nrm(ks[30], (N_MOE_LAYERS, N_EXPERTS, D_FF_EXPERT, D_MODEL), BETA * D_FF_EXPERT ** -0.5),
    }


def reference(x, c, ctx, c_ctx, w_mod, b_mod, w_in, s5_lam_re, s5_lam_im, s5_log_step, s5_b_re, s5_b_im,
              s5_c_re, s5_c_im, s5_d, s5_w_glu, s5_b_glu, attn_sink, ret_log_gamma, w_out,
              ln1_g, ln1_b, ln2_g, ln2_b, ffn_w1, ffn_w3, ffn_w2, moe_router, moe_w1, moe_w3, moe_w2):
    B, T, D = x.shape
    L = ctx.shape[1]
    ROWS = T // GRID_W
    t = jnp.arange(ROWS * GRID_W)
    rows = (t // GRID_W).astype(jnp.float32)
    cols = (t % GRID_W).astype(jnp.float32)
    pos = t.astype(jnp.float32)
    z = ctx
    for l in range(DEPTH):
        need_ctx = l < DEPTH - 1
        mod_x = (jax.nn.silu(c) @ w_mod[l] + b_mod[l]).reshape(B, 6, 1, D)
        mod_z = (jax.nn.silu(c_ctx) @ w_mod[l] + b_mod[l]).reshape(6, 1, 1, D)
        hx = modulate(x, mod_x[:, 0], mod_x[:, 1])
        hz = modulate(z, mod_z[0], mod_z[1])
        mix_x, mix_z = hybrid_mixer(hx, hz, rows, cols, pos, w_in[l], s5_lam_re[l], s5_lam_im[l],
                                    s5_log_step[l], s5_b_re[l], s5_b_im[l], s5_c_re[l], s5_c_im[l],
                                    s5_d[l], s5_w_glu[l], s5_b_glu[l], attn_sink[l], ret_log_gamma[l],
                                    w_out[l], need_ctx)
        x = post_norm(x, mod_x[:, 2] * mix_x, ln1_g[l], ln1_b[l])
        fx = modulate(x, mod_x[:, 3], mod_x[:, 4])
        if need_ctx:
            z = post_norm(z, mod_z[2] * mix_z, ln1_g[l], ln1_b[l])
            fz = modulate(z, mod_z[3], mod_z[4])
            f = channel_mixer(jnp.concatenate([fz, fx], axis=1), l, ffn_w1, ffn_w3, ffn_w2,
                              moe_router, moe_w1, moe_w3, moe_w2)
            z = post_norm(z, mod_z[5] * f[:, :L], ln2_g[l], ln2_b[l])
            fx_out = f[:, L:]
        else:
            fx_out = channel_mixer(fx, l, ffn_w1, ffn_w3, ffn_w2, moe_router, moe_w1, moe_w3, moe_w2)
        x = post_norm(x, mod_x[:, 5] * fx_out, ln2_g[l], ln2_b[l])
    return x
```

```python
import functools
import math

import jax
import jax.numpy as jnp
import numpy as np
from jax import lax
from jax.experimental import pallas as pl
from jax.experimental.pallas import tpu as pltpu

F32 = jnp.float32
BF16 = jnp.bfloat16

GRID_W = 64
HEAD_DIM = 64
S5_GROUP = 16
ATT_REP = 4
WINDOW = 128
ATT_BLOCK = 128
RET_CHUNK = 128
TOP_K = 2
LN_EPS = 1e-5
ROPE_BASE = 10000.0
NEG_INF = -1e30

LANES = 128
S5_CHUNK = 64
TOKEN_BLOCK = 256
VMEM_LIMIT = 56 * 1024 * 1024


def _cparams(sem):
    return pltpu.CompilerParams(dimension_semantics=sem, vmem_limit_bytes=VMEM_LIMIT)


def _dot(a, b):
    return jnp.dot(a, b, preferred_element_type=F32)


def _dot_nt(a, b):
    return lax.dot_general(a, b, (((1,), (1,)), ((), ())), preferred_element_type=F32)


def _dot_tn(a, b):
    return lax.dot_general(a, b, (((0,), (0,)), ((), ())), preferred_element_type=F32)


def _split_bf16(a):
    hi = a.astype(BF16)
    lo = (a - hi.astype(F32)).astype(BF16)
    return hi, lo


def _dot3(a, b):
    ah, al = _split_bf16(a)
    bh, bl = _split_bf16(b)
    return _dot(ah, bh) + (_dot(ah, bl) + _dot(al, bh))


def _ln(x):
    mu = jnp.mean(x, axis=-1, keepdims=True)
    xc = x - mu
    var = jnp.mean(xc * xc, axis=-1, keepdims=True)
    return xc * lax.rsqrt(var + LN_EPS)


def _silu(x):
    return x * (1.0 / (1.0 + jnp.exp(-x)))


def _sigmoid(x):
    return 1.0 / (1.0 + jnp.exp(-x))


def _gelu_tanh(x):
    c = math.sqrt(2.0 / math.pi)
    return 0.5 * x * (1.0 + jnp.tanh(c * (x + 0.044715 * (x * x * x))))


def _mod_kernel(c_ref, w_ref, b_ref, o_ref):
    o_ref[...] = _dot3(_silu(c_ref[...]), w_ref[...]) + b_ref[...]


def _modulation(cvec, w_mod, b_mod):
    depth, d, n = w_mod.shape
    rows = cvec.shape[0]
    tn = 1536
    return pl.pallas_call(
        _mod_kernel,
        grid=(depth, n // tn),
        in_specs=[pl.BlockSpec((rows, d), lambda l, j: (0, 0)),
                  pl.BlockSpec((None, d, tn), lambda l, j: (l, 0, j)),
                  pl.BlockSpec((None, 1, tn), lambda l, j: (l, 0, j))],
        out_specs=pl.BlockSpec((None, rows, tn), lambda l, j: (l, 0, j)),
        out_shape=jax.ShapeDtypeStruct((depth, rows, n), F32),
        compiler_params=_cparams(("parallel", "parallel")),
        name="modulation",
    )(cvec, w_mod, b_mod.reshape(depth, 1, n))


def _rope_slab(xs, cos, sa, sb, half):
    return xs * cos + pltpu.roll(xs, LANES - half, 1) * sa + pltpu.roll(xs, half, 1) * sb


def _inproj_kernel(x_ref, modx_ref, modz_ref, w_ref, ac_ref, asa_ref, asb_ref, rc_ref, rsa_ref, rsb_ref,
                   u_ref, qa_ref, ka_ref, va_ref, qr_ref, kr_ref, vr_ref, gr_ref, *, nzb, sizes):
    is_ctx = pl.program_id(1) < nzb
    mod = jnp.where(is_ctx, modz_ref[...], modx_ref[...])
    h = _ln(x_ref[...]) * (1.0 + mod[1:2]) + mod[0:1]
    p = _dot(h.astype(BF16), w_ref[...])
    ac = jnp.where(is_ctx, 1.0, ac_ref[...])
    asa = jnp.where(is_ctx, 0.0, asa_ref[...])
    asb = jnp.where(is_ctx, 0.0, asb_ref[...])
    rc = jnp.where(is_ctx, 1.0, rc_ref[...])
    rsa = jnp.where(is_ctx, 0.0, rsa_ref[...])
    rsb = jnp.where(is_ctx, 0.0, rsb_ref[...])
    offs = np.concatenate([[0], np.cumsum(sizes)])
    scale = HEAD_DIM ** -0.5

    def cols(i):
        return p[:, int(offs[i]):int(offs[i + 1])]

    def rope_cols(i, cos, sa, sb, half, mul):
        blk = cols(i)
        slabs = [_rope_slab(blk[:, s:s + LANES], cos, sa, sb, half) for s in range(0, blk.shape[1], LANES)]
        out = slabs[0] if len(slabs) == 1 else jnp.concatenate(slabs, axis=1)
        return out * mul if mul != 1.0 else out

    u_ref[...] = cols(0)
    qa_ref[...] = rope_cols(1, ac, asa, asb, HEAD_DIM // 4, scale)
    ka_ref[...] = rope_cols(2, ac, asa, asb, HEAD_DIM // 4, 1.0)
    va_ref[...] = cols(3)
    qr_ref[...] = rope_cols(4, rc, rsa, rsb, HEAD_DIM // 2, 1.0)
    kr_ref[...] = rope_cols(5, rc, rsa, rsb, HEAD_DIM // 2, scale)
    vr_ref[...] = cols(6)
    gr_ref[...] = cols(7)


def _inproj(xz, mod, w_in, tabs, sizes, n_ctx):
    bsz, s, d = xz.shape
    tm = TOKEN_BLOCK
    nzb = n_ctx // tm
    nrow = mod.shape[0]
    n_in = w_in.shape[1]
    tok = lambda width: pl.BlockSpec((None, tm, width), lambda b, j: (b, j, 0))
    tab = pl.BlockSpec((tm, LANES), lambda b, j: (jnp.maximum(j - nzb, 0), 0))
    return pl.pallas_call(
        functools.partial(_inproj_kernel, nzb=nzb, sizes=sizes),
        grid=(bsz, s // tm),
        in_specs=[tok(d),
                  pl.BlockSpec((None, 6, d), lambda b, j: (b, 0, 0)),
                  pl.BlockSpec((None, 6, d), lambda b, j: (nrow - 1, 0, 0)),
                  pl.BlockSpec((d, n_in), lambda b, j: (0, 0)),
                  tab, tab, tab, tab, tab, tab],
        out_specs=[tok(w) for w in sizes],
        out_shape=[jax.ShapeDtypeStruct((bsz, s, w), F32) for w in sizes],
        compiler_params=_cparams(("parallel", "parallel")),
        name="inproj",
    )(xz, mod, mod, w_in, *tabs)


def _rope_tables(t_len):
    t = jnp.arange(t_len)
    rows = (t // GRID_W).astype(F32)
    cols = (t % GRID_W).astype(F32)
    pos = t.astype(F32)

    def angles(p, dim):
        inv_freq = ROPE_BASE ** (-jnp.arange(0, dim, 2, dtype=F32) / dim)
        return p[:, None] * inv_freq[None, :]

    def head_tables(angs):
        cos = jnp.concatenate([jnp.concatenate([jnp.cos(a), jnp.cos(a)], -1) for a in angs], -1)
        sa = jnp.concatenate([jnp.concatenate([-jnp.sin(a), jnp.zeros_like(a)], -1) for a in angs], -1)
        sb = jnp.concatenate([jnp.concatenate([jnp.zeros_like(a), jnp.sin(a)], -1) for a in angs], -1)
        rep = LANES // HEAD_DIM
        return tuple(jnp.tile(x, (1, rep)) for x in (cos, sa, sb))

    att = head_tables([angles(rows, HEAD_DIM // 2), angles(cols, HEAD_DIM // 2)])
    ret = head_tables([angles(pos, HEAD_DIM)])
    return att + ret


def _s5_kernel(u_ref, m_ref, wsf_ref, wsb_ref, wrf_ref, wrb_ref, lam_ref, y_ref,
               cf_ref, cb_ref, pf_ref, pb_ref, *, bsz, n_chunks, nz_chunks):
    ub = u_ref[...].astype(BF16)
    cf_ref[...] = _dot(ub, wsf_ref[...])
    cb_ref[...] = _dot(ub, wsb_ref[...])
    lam = lam_ref[...]
    nst = lam.shape[1] // 2

    def advance(s, a, bc, c):
        return s * a + pltpu.roll(s, nst, 1) * bc + c

    def sweep(order, c_ref, p_ref, a, bc):
        s = jnp.zeros((bsz, lam.shape[1]), F32)
        for n in order:
            rows = slice(n * bsz, (n + 1) * bsz)
            p_ref[rows, :] = s
            s = advance(s, a, bc, c_ref[rows, :])

    sweep(list(range(n_chunks)), cf_ref, pf_ref, lam[0:1], lam[1:2])
    order_b = list(range(nz_chunks - 1, -1, -1)) + list(range(n_chunks - 1, nz_chunks - 1, -1))
    sweep(order_b, cb_ref, pb_ref, lam[2:3], lam[3:4])
    y_ref[...] = (_dot(ub, m_ref[...])
                  + _dot(pf_ref[...].astype(BF16), wrf_ref[...])
                  + _dot(pb_ref[...].astype(BF16), wrb_ref[...]))


def _s5_weights(lam_re, lam_im, log_step, b_re, b_im, c_re, c_im, chunk):
    hp = lax.Precision.HIGHEST
    lam = lax.complex(lam_re.astype(F32), lam_im.astype(F32))
    lam_dt = lam * jnp.exp(log_step.astype(F32))[..., None]
    lam_bar = jnp.exp(lam_dt)
    b_bar = lax.complex(b_re.astype(F32), b_im.astype(F32)) * ((lam_bar - 1.0) / lam)[..., None]
    c_mat = lax.complex(c_re.astype(F32), c_im.astype(F32))
    g, n = lam.shape[1], lam.shape[2]
    ch = b_bar.shape[-1]
    steps = jnp.arange(chunk + 1, dtype=F32)
    pw = jnp.exp(steps[None, :, None, None] * lam_dt[:, None])
    kern = jnp.einsum('zgon,zdgn,zgni->zgdoi', c_mat, pw[:, :chunk], b_bar, precision=hp).real
    zero_lag = kern[0, :, 0] + kern[1, :, 0]
    lag_table = jnp.concatenate([kern[1, :, :0:-1], zero_lag[:, None], kern[0, :, 1:]], axis=1)
    t_idx = jnp.arange(chunk)
    lag = t_idx[None, :] - t_idx[:, None] + (chunk - 1)
    m = lag_table[:, lag]
    m = m.transpose(0, 1, 4, 2, 3).reshape(g, chunk * ch, chunk * ch)

    def state_in(pw_s, b_dir):
        w = pw_s[:, :, :, None] * b_dir[None]
        w = jnp.concatenate([w.real, w.imag], axis=2)
        return w.transpose(1, 0, 3, 2).reshape(g, chunk * ch, 2 * n)

    def state_out(pw_t, c_dir):
        w = c_dir[None] * pw_t[:, :, None, :]
        w = jnp.concatenate([w.real, -w.imag], axis=3)
        return w.transpose(1, 3, 0, 2).reshape(g, 2 * n, chunk * ch)

    wsf = state_in(pw[0, chunk - 1::-1][:chunk], b_bar[0])
    wsb = state_in(pw[1, :chunk], b_bar[1])
    wrf = state_out(pw[0, 1:chunk + 1], c_mat[0])
    wrb = state_out(pw[1, chunk:0:-1], c_mat[1])
    lam_c = pw[:, chunk]
    rows = []
    for z in range(2):
        rows.append(jnp.concatenate([lam_c[z].real, lam_c[z].real], -1))
        rows.append(jnp.concatenate([-lam_c[z].imag, lam_c[z].imag], -1))
    lam_rows = jnp.stack(rows + [jnp.zeros_like(rows[0])] * 4, axis=1)
    return m.astype(BF16), wsf.astype(BF16), wsb.astype(BF16), wrf.astype(BF16), wrb.astype(BF16), lam_rows


def _s5_scan(u, weights, n_ctx):
    bsz, s, width = u.shape
    m, wsf, wsb, wrf, wrb, lam_rows = weights
    g = m.shape[0]
    ch = width // g
    chunk = S5_CHUNK
    n_chunks = s // chunk
    k = chunk * ch
    nst2 = wsf.shape[2]
    rows = n_chunks * bsz
    ug = u.reshape(bsz, n_chunks, chunk, g, ch).transpose(3, 1, 0, 2, 4).reshape(g, rows, k)
    per_g = lambda a, b: pl.BlockSpec((None, a, b), lambda i: (i, 0, 0))
    y = pl.pallas_call(
        functools.partial(_s5_kernel, bsz=bsz, n_chunks=n_chunks, nz_chunks=n_ctx // chunk),
        grid=(g,),
        in_specs=[per_g(rows, k), per_g(k, k), per_g(k, nst2), per_g(k, nst2), per_g(nst2, k), per_g(nst2, k),
                  per_g(8, nst2)],
        out_specs=per_g(rows, k),
        out_shape=jax.ShapeDtypeStruct((g, rows, k), F32),
        scratch_shapes=[pltpu.VMEM((rows, nst2), F32)] * 4,
        compiler_params=_cparams(("parallel",)),
        name="s5_scan",
    )(ug, m, wsf, wsb, wrf, wrb, lam_rows)
    return y.reshape(g, n_chunks, bsz, chunk, ch).transpose(2, 1, 3, 0, 4).reshape(bsz, s, width)


def _attn_kernel(sink_ref, q_ref, k_ref, v_ref, o_ref, *, n_ctx, t_len, q_off, kv_heads):
    qi = pl.program_id(1) + q_off
    nzb = n_ctx // ATT_BLOCK
    band = 3 * ATT_BLOCK
    q = q_ref[...]
    kc = k_ref[0:n_ctx, :].astype(BF16)
    vc = v_ref[0:n_ctx, :].astype(BF16)

    def head_cols(a, i):
        return a[:, i * HEAD_DIM:(i + 1) * HEAD_DIM]

    def put(h, o):
        o_ref[:, h * HEAD_DIM:(h + 1) * HEAD_DIM] = o

    @pl.when(qi < nzb)
    def _():
        for h in range(kv_heads * ATT_REP):
            g = h // ATT_REP
            sink = sink_ref[h]
            s = _dot_nt(head_cols(q, h).astype(BF16), head_cols(kc, g))
            m = jnp.maximum(jnp.max(s, axis=1, keepdims=True), sink)
            e = jnp.exp(s - m)
            den = jnp.sum(e, axis=1, keepdims=True) + jnp.exp(sink - m)
            put(h, _dot(e.astype(BF16), head_cols(vc, g)) / den)

    @pl.when(qi >= nzb)
    def _():
        n = qi - nzb
        start = jnp.clip((n - 1) * ATT_BLOCK, 0, t_len - band)
        kl = k_ref[pl.ds(pl.multiple_of(n_ctx + start, ATT_BLOCK), band), :].astype(BF16)
        vl = v_ref[pl.ds(pl.multiple_of(n_ctx + start, ATT_BLOCK), band), :].astype(BF16)
        q_pos = n * ATT_BLOCK + lax.broadcasted_iota(jnp.int32, (ATT_BLOCK, band), 0)
        k_pos = start + lax.broadcasted_iota(jnp.int32, (ATT_BLOCK, band), 1)
        valid = jnp.abs(k_pos - q_pos) <= WINDOW
        for h in range(kv_heads * ATT_REP):
            g = h // ATT_REP
            sink = sink_ref[h]
            qh = head_cols(q, h).astype(BF16)
            s_loc = jnp.where(valid, _dot_nt(qh, head_cols(kl, g)), NEG_INF)
            s_ctx = _dot_nt(qh, head_cols(kc, g))
            m = jnp.maximum(jnp.maximum(jnp.max(s_loc, axis=1, keepdims=True),
                                        jnp.max(s_ctx, axis=1, keepdims=True)), sink)
            e_loc = jnp.exp(s_loc - m)
            e_ctx = jnp.exp(s_ctx - m)
            den = (jnp.sum(e_loc, axis=1, keepdims=True) + jnp.sum(e_ctx, axis=1, keepdims=True)
                   + jnp.exp(sink - m))
            o = _dot(e_loc.astype(BF16), head_cols(vl, g)) + _dot(e_ctx.astype(BF16), head_cols(vc, g))
            put(h, o / den)


def _attention(qa, ka, va, sink, n_ctx, need_ctx):
    bsz, s, qw = qa.shape
    kvw = ka.shape[2]
    t_len = s - n_ctx
    q_off = 0 if need_ctx else n_ctx // ATT_BLOCK
    nq = s // ATT_BLOCK - q_off
    return pl.pallas_call(
        functools.partial(_attn_kernel, n_ctx=n_ctx, t_len=t_len, q_off=q_off, kv_heads=kvw // HEAD_DIM),
        grid=(bsz, nq),
        in_specs=[pl.BlockSpec(memory_space=pltpu.SMEM),
                  pl.BlockSpec((None, ATT_BLOCK, qw), lambda b, j: (b, j + q_off, 0)),
                  pl.BlockSpec((None, s, kvw), lambda b, j: (b, 0, 0)),
                  pl.BlockSpec((None, s, kvw), lambda b, j: (b, 0, 0))],
        out_specs=pl.BlockSpec((None, ATT_BLOCK, qw), lambda b, j: (b, j, 0)),
        out_shape=jax.ShapeDtypeStruct((bsz, nq * ATT_BLOCK, qw), F32),
        compiler_params=_cparams(("parallel", "arbitrary")),
        name="window_attention",
    )(sink.astype(F32), qa, ka, va)


def _ret_kernel(lg_ref, q_ref, k_ref, v_ref, g_ref, o_ref, acc_ref, dec_ref, sf_ref, sb_ref,
                *, n_chunks, nz_chunks, heads):
    c = RET_CHUNK
    ii = lax.broadcasted_iota(jnp.int32, (c, c), 0)
    jj = lax.broadcasted_iota(jnp.int32, (c, c), 1)
    diff = (ii - jj).astype(F32)
    idx = lax.broadcasted_iota(jnp.int32, (c, 1), 0).astype(F32)
    for h in range(heads):
        lgf, lgb = lg_ref[0, h], lg_ref[1, h]
        dec_ref[h] = jnp.where(diff >= 0, jnp.exp(lgf * jnp.maximum(diff, 0.0)),
                               jnp.exp(lgb * jnp.maximum(-diff, 0.0)))
    sf_ref[...] = jnp.zeros_like(sf_ref)
    sb_ref[...] = jnp.zeros_like(sb_ref)

    def hcols(a, h):
        return a[:, h * HEAD_DIM:(h + 1) * HEAD_DIM]

    chunk_len = jnp.full((1, HEAD_DIM), float(c), F32)

    def fwd(n, carry):
        rows = pl.ds(pl.multiple_of(n * c, c), c)
        q, k, v = q_ref[rows, :], k_ref[rows, :], v_ref[rows, :]
        for h in range(heads):
            lgf = lg_ref[0, h]
            cols = slice(h * HEAD_DIM, (h + 1) * HEAD_DIM)
            qh, kh, vh = hcols(q, h), hcols(k, h), hcols(v, h).astype(BF16)
            scores = _dot_nt(qh.astype(BF16), kh.astype(BF16)) * dec_ref[h]
            s_prev = sf_ref[h]
            o = _dot(scores.astype(BF16), vh)
            o = o + _dot((qh * jnp.exp(lgf * (idx + 1.0))).astype(BF16), s_prev.astype(BF16))
            kw = kh * jnp.exp(lgf * (c - 1.0 - idx))
            sf_ref[h] = jnp.exp(lgf * chunk_len) * s_prev + _dot_tn(kw.astype(BF16), vh)
            acc_ref[rows, cols] = o
        return carry

    lax.fori_loop(0, n_chunks, fwd, 0)

    def bwd(i, carry):
        n = jnp.where(i < nz_chunks, nz_chunks - 1 - i, n_chunks - 1 - i + nz_chunks)
        rows = pl.ds(pl.multiple_of(n * c, c), c)
        q, k, v = q_ref[rows, :], k_ref[rows, :], v_ref[rows, :]
        acc = acc_ref[rows, :]
        gate = _silu(g_ref[rows, :])
        for h in range(heads):
            lgb = lg_ref[1, h]
            cols = slice(h * HEAD_DIM, (h + 1) * HEAD_DIM)
            qh, kh, vh = hcols(q, h), hcols(k, h), hcols(v, h).astype(BF16)
            s_prev = sb_ref[h]
            o = hcols(acc, h) + _dot((qh * jnp.exp(lgb * (c - idx))).astype(BF16), s_prev.astype(BF16))
            kw = kh * jnp.exp(lgb * idx)
            sb_ref[h] = jnp.exp(lgb * chunk_len) * s_prev + _dot_tn(kw.astype(BF16), vh)
            o_ref[rows, cols] = _ln(o) * hcols(gate, h)
        return carry

    lax.fori_loop(0, n_chunks, bwd, 0)


def _retention(qr, kr, vr, gr, log_gamma, n_ctx):
    bsz, s, w = qr.shape
    heads = w // HEAD_DIM
    n_chunks = s // RET_CHUNK
    tok = pl.BlockSpec((None, s, w), lambda b: (b, 0, 0))
    return pl.pallas_call(
        functools.partial(_ret_kernel, n_chunks=n_chunks, nz_chunks=n_ctx // RET_CHUNK, heads=heads),
        grid=(bsz,),
        in_specs=[pl.BlockSpec(memory_space=pltpu.SMEM), tok, tok, tok, tok],
        out_specs=tok,
        out_shape=jax.ShapeDtypeStruct((bsz, s, w), F32),
        scratch_shapes=[pltpu.VMEM((s, w), F32),
                        pltpu.VMEM((heads, RET_CHUNK, RET_CHUNK), F32),
                        pltpu.VMEM((heads, HEAD_DIM, HEAD_DIM), F32),
                        pltpu.VMEM((heads, HEAD_DIM, HEAD_DIM), F32)],
        compiler_params=_cparams(("parallel",)),
        name="retention",
    )(log_gamma.astype(F32), qr, kr, vr, gr)


def _outproj_kernel(*refs, nzb, alpha, with_router, w_s5, w_att, n_exp):
    if with_router:
        (y_ref, u_ref, a_ref, r_ref, x_ref, modx_ref, modz_ref, d_ref, wg_ref, bg_ref, wo_ref,
         g1_ref, b1_ref, rt_ref, x1_ref, fx_ref, gate_ref) = refs
    else:
        (y_ref, u_ref, a_ref, r_ref, x_ref, modx_ref, modz_ref, d_ref, wg_ref, bg_ref, wo_ref,
         g1_ref, b1_ref, x1_ref, fx_ref) = refs
    is_ctx = pl.program_id(1) < nzb
    mod = jnp.where(is_ctx, modz_ref[...], modx_ref[...])
    g = _gelu_tanh(y_ref[...] + d_ref[...] * u_ref[...])
    gb = g.astype(BF16)
    s5 = g * _sigmoid(_dot(gb, wg_ref[...]) + bg_ref[...])
    mix = (_dot(s5.astype(BF16), wo_ref[0:w_s5, :])
           + _dot(a_ref[...].astype(BF16), wo_ref[w_s5:w_s5 + w_att, :])
           + _dot(r_ref[...].astype(BF16), wo_ref[w_s5 + w_att:, :]))
    x1 = _ln(alpha * x_ref[...] + mod[2:3] * mix) * g1_ref[...] + b1_ref[...]
    x1_ref[...] = x1
    fx = _ln(x1) * (1.0 + mod[4:5]) + mod[3:4]
    fx_ref[...] = fx.astype(BF16)
    if with_router:
        lane = lax.broadcasted_iota(jnp.int32, (fx.shape[0], LANES), 1)
        logits = jnp.where(lane < n_exp, _dot3(fx, rt_ref[...]), -jnp.inf)
        m1 = jnp.max(logits, axis=1, keepdims=True)
        i1 = jnp.min(jnp.where(logits == m1, lane, LANES), axis=1, keepdims=True)
        rest = jnp.where(lane == i1, -jnp.inf, logits)
        m2 = jnp.max(rest, axis=1, keepdims=True)
        i2 = jnp.min(jnp.where(rest == m2, lane, LANES), axis=1, keepdims=True)
        e2 = jnp.exp(m2 - m1)
        den = 1.0 + e2
        gate_ref[...] = jnp.where(lane == i1, 1.0 / den, jnp.where(lane == i2, e2 / den, 0.0))


def _outproj(y_s5, u, o_att, o_ret, xz, mod, s5_d, w_glu, b_glu, w_out, ln_g, ln_b, router, n_ctx, need_ctx,
             alpha):
    bsz, s, d = xz.shape
    tm = TOKEN_BLOCK
    nzb = n_ctx // tm
    off = 0 if need_ctx else nzb
    nblk = s // tm - off
    s_out = nblk * tm
    nrow = mod.shape[0]
    w_s5, w_att = y_s5.shape[2], o_att.shape[2]
    att_off = off if o_att.shape[1] == s else 0

    def tok(width, shift):
        return pl.BlockSpec((None, tm, width), lambda b, j: (b, j + shift, 0))

    def full(a):
        return pl.BlockSpec(a.shape, lambda b, j: (0,) * a.ndim)

    vec = lambda a: a.reshape(1, -1).astype(F32)
    consts = [vec(s5_d), w_glu.astype(BF16), vec(b_glu), w_out.astype(BF16), vec(ln_g), vec(ln_b)]
    in_specs = [tok(w_s5, off), tok(w_s5, off), tok(w_att, att_off), tok(o_ret.shape[2], off), tok(d, off),
                pl.BlockSpec((None, 6, d), lambda b, j: (b, 0, 0)),
                pl.BlockSpec((None, 6, d), lambda b, j: (nrow - 1, 0, 0))] + [full(a) for a in consts]
    out_specs = [tok(d, 0), tok(d, 0)]
    out_shape = [jax.ShapeDtypeStruct((bsz, s_out, d), F32), jax.ShapeDtypeStruct((bsz, s_out, d), BF16)]
    args = [y_s5, u, o_att, o_ret, xz, mod, mod] + consts
    with_router = router is not None
    n_exp = 0
    if with_router:
        n_exp = router.shape[1]
        router_pad = jnp.pad(router.astype(F32), ((0, 0), (0, LANES - n_exp)))
        args.append(router_pad)
        in_specs.append(full(router_pad))
        out_specs.append(tok(LANES, 0))
        out_shape.append(jax.ShapeDtypeStruct((bsz, s_out, LANES), F32))
    return pl.pallas_call(
        functools.partial(_outproj_kernel, nzb=nzb - off, alpha=alpha, with_router=with_router,
                          w_s5=w_s5, w_att=w_att, n_exp=n_exp),
        grid=(bsz, nblk),
        in_specs=in_specs,
        out_specs=out_specs,
        out_shape=out_shape,
        compiler_params=_cparams(("parallel", "parallel")),
        name="outproj",
    )(*args)


def _ffn_kernel(fx_ref, x1_ref, gate_ref, modx_ref, modz_ref, w1_ref, w3_ref, w2_ref, g2_ref, b2_ref,
                o_ref, acc_ref, *, nzb, alpha, fc, gated):
    e = pl.program_id(2)
    f = pl.program_id(3)

    @pl.when((e == 0) & (f == 0))
    def _():
        acc_ref[...] = jnp.zeros_like(acc_ref)

    fx = fx_ref[...]
    if gated:
        gates = gate_ref[...]
        lane = lax.broadcasted_iota(jnp.int32, gates.shape, 1)
        gate = jnp.sum(jnp.where(lane == e, gates, 0.0), axis=1, keepdims=True)
    for s in range(0, w1_ref.shape[1], fc):
        h1 = _dot(fx, w1_ref[:, s:s + fc])
        h3 = _dot(fx, w3_ref[:, s:s + fc])
        a = _silu(h1) * h3
        if gated:
            a = a * gate
        acc_ref[...] += _dot(a.astype(BF16), w2_ref[s:s + fc, :])

    @pl.when((e == pl.num_programs(2) - 1) & (f == pl.num_programs(3) - 1))
    def _():
        is_ctx = pl.program_id(1) < nzb
        mod = jnp.where(is_ctx, modz_ref[...], modx_ref[...])
        o_ref[...] = _ln(alpha * x1_ref[...] + mod[5:6] * acc_ref[...]) * g2_ref[...] + b2_ref[...]


def _ffn(fx, x1, gates, mod, w1, w3, w2, ln_g, ln_b, n_ctx_tokens, alpha, tm, nf, fc):
    bsz, s, d = x1.shape
    n_exp, _, ff = w1.shape
    nzb = n_ctx_tokens // tm
    nrow = mod.shape[0]
    tf = ff // nf
    gated = gates is not None
    if not gated:
        gates = jnp.ones((bsz, s, 1), F32)
    tok = lambda width: pl.BlockSpec((None, tm, width), lambda b, j, e, f: (b, j, 0))
    vec = lambda a: a.reshape(1, -1).astype(F32)
    row = pl.BlockSpec((1, d), lambda b, j, e, f: (0, 0))
    return pl.pallas_call(
        functools.partial(_ffn_kernel, nzb=nzb, alpha=alpha, fc=fc, gated=gated),
        grid=(bsz, s // tm, n_exp, nf),
        in_specs=[tok(d), tok(d), tok(gates.shape[2]),
                  pl.BlockSpec((None, 6, d), lambda b, j, e, f: (b, 0, 0)),
                  pl.BlockSpec((None, 6, d), lambda b, j, e, f: (nrow - 1, 0, 0)),
                  pl.BlockSpec((None, d, tf), lambda b, j, e, f: (e, 0, f)),
                  pl.BlockSpec((None, d, tf), lambda b, j, e, f: (e, 0, f)),
                  pl.BlockSpec((None, tf, d), lambda b, j, e, f: (e, f, 0)),
                  row, row],
        out_specs=tok(d),
        out_shape=jax.ShapeDtypeStruct((bsz, s, d), F32),
        scratch_shapes=[pltpu.VMEM((tm, d), F32)],
        compiler_params=_cparams(("parallel", "parallel", "arbitrary", "arbitrary")),
        name="moe_ffn" if gated else "dense_ffn",
    )(fx, x1, gates, mod, mod, w1, w3, w2, vec(ln_g), vec(ln_b))


def _pick_chunk(total, target):
    best = LANES
    for c in range(LANES, target + 1, LANES):
        if total % c == 0:
            best = c
    return best


def kernel(x, c, ctx, c_ctx, w_mod, b_mod, w_in, s5_lam_re, s5_lam_im, s5_log_step, s5_b_re, s5_b_im,
           s5_c_re, s5_c_im, s5_d, s5_w_glu, s5_b_glu, attn_sink, ret_log_gamma, w_out,
           ln1_g, ln1_b, ln2_g, ln2_b, ffn_w1, ffn_w3, ffn_w2, moe_router, moe_w1, moe_w3, moe_w2):
    bsz, t_len, d = x.shape
    n_ctx = ctx.shape[1]
    depth = w_in.shape[0]
    alpha = (2 * depth) ** 0.25
    s5_w = s5_d.shape[1]
    att_w = attn_sink.shape[1] * HEAD_DIM
    kv_w = att_w // ATT_REP
    ret_w = ret_log_gamma.shape[2] * HEAD_DIM
    sizes = (s5_w, att_w, kv_w, kv_w, ret_w, ret_w, ret_w, ret_w)
    assert sum(sizes) == w_in.shape[2] and s5_w + att_w + ret_w == w_out.shape[1]
    assert n_ctx % TOKEN_BLOCK == 0 and t_len % TOKEN_BLOCK == 0 and t_len >= 3 * ATT_BLOCK

    pad = (-(bsz + 1)) % 8
    cvec = jnp.concatenate([jnp.zeros((pad, d), F32), c_ctx[None].astype(F32)], axis=0)
    cvec = jnp.concatenate([c.astype(F32), cvec], axis=0)
    mod_all = _modulation(cvec, w_mod.astype(F32), b_mod.astype(F32)).reshape(depth, bsz + pad + 1, 6, d)

    tabs = _rope_tables(t_len)
    xz = jnp.concatenate([ctx, x], axis=1).astype(F32)
    for l in range(depth):
        need_ctx = l < depth - 1
        mod = mod_all[l]
        u, qa, ka, va, qr, kr, vr, gr = _inproj(xz, mod, w_in[l].astype(BF16), tabs, sizes, n_ctx)
        s5w = _s5_weights(s5_lam_re[l], s5_lam_im[l], s5_log_step[l], s5_b_re[l], s5_b_im[l],
                          s5_c_re[l], s5_c_im[l], S5_CHUNK)
        y_s5 = _s5_scan(u, s5w, n_ctx)
        o_att = _attention(qa, ka, va, attn_sink[l], n_ctx, need_ctx)
        o_ret = _retention(qr, kr, vr, gr, ret_log_gamma[l], n_ctx)
        i = l // 2
        router = None if l % 2 == 0 else moe_router[i]
        outs = _outproj(y_s5, u, o_att, o_ret, xz, mod, s5_d[l], s5_w_glu[l], s5_b_glu[l], w_out[l],
                        ln1_g[l], ln1_b[l], router, n_ctx, need_ctx, alpha)
        ctx_tokens = n_ctx if need_ctx else 0
        if l % 2 == 0:
            x1, fx = outs
            ff = ffn_w1.shape[2]
            xz_next = _ffn(fx, x1, None, mod, ffn_w1[i][None].astype(BF16), ffn_w3[i][None].astype(BF16),
                           ffn_w2[i][None].astype(BF16), ln2_g[l], ln2_b[l], ctx_tokens, alpha,
                           tm=TOKEN_BLOCK, nf=1, fc=_pick_chunk(ff, 512))
        else:
            x1, fx, gates = outs
            ff = moe_w1.shape[3]
            nf = 2 if ff % (2 * LANES) == 0 else 1
            tm = TOKEN_BLOCK if need_ctx else _pick_chunk(t_len, 512)
            xz_next = _ffn(fx, x1, gates, mod, moe_w1[i].astype(BF16), moe_w3[i].astype(BF16),
                           moe_w2[i].astype(BF16), ln2_g[l], ln2_b[l], ctx_tokens, alpha,
                           tm=tm, nf=nf, fc=_pick_chunk(ff // nf, 512))
        xz = xz_next
    return xz if xz.shape[1] == t_len else xz[:, n_ctx:]
```

```python
import functools
import math

import jax
import jax.numpy as jnp
import numpy as np
from jax import lax
from jax.experimental import pallas as pl
from jax.experimental.pallas import tpu as pltpu

F32 = jnp.float32
BF16 = jnp.bfloat16

GRID_W = 64
HEAD_DIM = 64
S5_GROUP = 16
ATT_REP = 4
WINDOW = 128
ATT_BLOCK = 128
RET_CHUNK = 128
TOP_K = 2
LN_EPS = 1e-5
ROPE_BASE = 10000.0
NEG_INF = -1e30

LANES = 128
S5_CHUNK = 64
TOKEN_BLOCK = 256
VMEM_LIMIT = 56 * 1024 * 1024


def _cparams(sem):
    return pltpu.CompilerParams(dimension_semantics=sem, vmem_limit_bytes=VMEM_LIMIT)


def _dot(a, b):
    return jnp.dot(a, b, preferred_element_type=F32)


def _dot_nt(a, b):
    return lax.dot_general(a, b, (((1,), (1,)), ((), ())), preferred_element_type=F32)


def _dot_tn(a, b):
    return lax.dot_general(a, b, (((0,), (0,)), ((), ())), preferred_element_type=F32)


def _split_bf16(a):
    hi = a.astype(BF16)
    lo = (a - hi.astype(F32)).astype(BF16)
    return hi, lo


def _dot3(a, b):
    ah, al = _split_bf16(a)
    bh, bl = _split_bf16(b)
    return _dot(ah, bh) + (_dot(ah, bl) + _dot(al, bh))


def _ln(x):
    mu = jnp.mean(x, axis=-1, keepdims=True)
    xc = x - mu
    var = jnp.mean(xc * xc, axis=-1, keepdims=True)
    return xc * lax.rsqrt(var + LN_EPS)


def _silu(x):
    return x * (1.0 / (1.0 + jnp.exp(-x)))


def _sigmoid(x):
    return 1.0 / (1.0 + jnp.exp(-x))


def _gelu_tanh(x):
    c = math.sqrt(2.0 / math.pi)
    return 0.5 * x * (1.0 + jnp.tanh(c * (x + 0.044715 * (x * x * x))))


def _mod_kernel(c_ref, w_ref, b_ref, o_ref):
    o_ref[...] = _dot3(_silu(c_ref[...]), w_ref[...]) + b_ref[...]


def _modulation(cvec, w_mod, b_mod):
    depth, d, n = w_mod.shape
    rows = cvec.shape[0]
    tn = 1536
    return pl.pallas_call(
        _mod_kernel,
        grid=(depth, n // tn),
        in_specs=[pl.BlockSpec((rows, d), lambda l, j: (0, 0)),
                  pl.BlockSpec((None, d, tn), lambda l, j: (l, 0, j)),
                  pl.BlockSpec((None, 1, tn), lambda l, j: (l, 0, j))],
        out_specs=pl.BlockSpec((None, rows, tn), lambda l, j: (l, 0, j)),
        out_shape=jax.ShapeDtypeStruct((depth, rows, n), F32),
        compiler_params=_cparams(("parallel", "parallel")),
        name="modulation",
    )(cvec, w_mod, b_mod.reshape(depth, 1, n))


def _rope_slab(xs, cos, sa, sb, half):
    return xs * cos + pltpu.roll(xs, LANES - half, 1) * sa + pltpu.roll(xs, half, 1) * sb


def _inproj_kernel(x_ref, modx_ref, modz_ref, w_ref, ac_ref, asa_ref, asb_ref, rc_ref, rsa_ref, rsb_ref,
                   u_ref, qa_ref, ka_ref, va_ref, qr_ref, kr_ref, vr_ref, gr_ref, *, nzb, sizes):
    is_ctx = pl.program_id(1) < nzb
    mod = jnp.where(is_ctx, modz_ref[...], modx_ref[...])
    h = _ln(x_ref[...]) * (1.0 + mod[1:2]) + mod[0:1]
    p = _dot(h.astype(BF16), w_ref[...])
    ac = jnp.where(is_ctx, 1.0, ac_ref[...])
    asa = jnp.where(is_ctx, 0.0, asa_ref[...])
    asb = jnp.where(is_ctx, 0.0, asb_ref[...])
    rc = jnp.where(is_ctx, 1.0, rc_ref[...])
    rsa = jnp.where(is_ctx, 0.0, rsa_ref[...])
    rsb = jnp.where(is_ctx, 0.0, rsb_ref[...])
    offs = np.concatenate([[0], np.cumsum(sizes)])
    scale = HEAD_DIM ** -0.5

    def cols(i):
        return p[:, int(offs[i]):int(offs[i + 1])]

    def rope_cols(i, cos, sa, sb, half, mul):
        blk = cols(i)
        slabs = [_rope_slab(blk[:, s:s + LANES], cos, sa, sb, half) for s in range(0, blk.shape[1], LANES)]
        out = slabs[0] if len(slabs) == 1 else jnp.concatenate(slabs, axis=1)
        return out * mul if mul != 1.0 else out

    u_ref[...] = cols(0)
    qa_ref[...] = rope_cols(1, ac, asa, asb, HEAD_DIM // 4, scale)
    ka_ref[...] = rope_cols(2, ac, asa, asb, HEAD_DIM // 4, 1.0)
    va_ref[...] = cols(3)
    qr_ref[...] = rope_cols(4, rc, rsa, rsb, HEAD_DIM // 2, 1.0)
    kr_ref[...] = rope_cols(5, rc, rsa, rsb, HEAD_DIM // 2, scale)
    vr_ref[...] = cols(6)
    gr_ref[...] = cols(7)


def _inproj(xz, mod, w_in, tabs, sizes, n_ctx):
    bsz, s, d = xz.shape
    tm = TOKEN_BLOCK
    nzb = n_ctx // tm
    nrow = mod.shape[0]
    n_in = w_in.shape[1]
    tok = lambda width: pl.BlockSpec((None, tm, width), lambda b, j: (b, j, 0))
    tab = pl.BlockSpec((tm, LANES), lambda b, j: (jnp.maximum(j - nzb, 0), 0))
    return pl.pallas_call(
        functools.partial(_inproj_kernel, nzb=nzb, sizes=sizes),
        grid=(bsz, s // tm),
        in_specs=[tok(d),
                  pl.BlockSpec((None, 6, d), lambda b, j: (b, 0, 0)),
                  pl.BlockSpec((None, 6, d), lambda b, j: (nrow - 1, 0, 0)),
                  pl.BlockSpec((d, n_in), lambda b, j: (0, 0)),
                  tab, tab, tab, tab, tab, tab],
        out_specs=[tok(w) for w in sizes],
        out_shape=[jax.ShapeDtypeStruct((bsz, s, w), F32) for w in sizes],
        compiler_params=_cparams(("parallel", "parallel")),
        name="inproj",
    )(xz, mod, mod, w_in, *tabs)


def _rope_tables(t_len):
    t = jnp.arange(t_len)
    rows = (t // GRID_W).astype(F32)
    cols = (t % GRID_W).astype(F32)
    pos = t.astype(F32)

    def angles(p, dim):
        inv_freq = ROPE_BASE ** (-jnp.arange(0, dim, 2, dtype=F32) / dim)
        return p[:, None] * inv_freq[None, :]

    def head_tables(angs):
        cos = jnp.concatenate([jnp.concatenate([jnp.cos(a), jnp.cos(a)], -1) for a in angs], -1)
        sa = jnp.concatenate([jnp.concatenate([-jnp.sin(a), jnp.zeros_like(a)], -1) for a in angs], -1)
        sb = jnp.concatenate([jnp.concatenate([jnp.zeros_like(a), jnp.sin(a)], -1) for a in angs], -1)
        rep = LANES // HEAD_DIM
        return tuple(jnp.tile(x, (1, rep)) for x in (cos, sa, sb))

    att = head_tables([angles(rows, HEAD_DIM // 2), angles(cols, HEAD_DIM // 2)])
    ret = head_tables([angles(pos, HEAD_DIM)])
    return att + ret


def _s5_kernel(u_ref, m_ref, wsf_ref, wsb_ref, wrf_ref, wrb_ref, lam_ref, y_ref,
               cf_ref, cb_ref, pf_ref, pb_ref, *, bsz, n_chunks, nz_chunks):
    ub = u_ref[...].astype(BF16)
    cf_ref[...] = _dot(ub, wsf_ref[...])
    cb_ref[...] = _dot(ub, wsb_ref[...])
    lam = lam_ref[...]
    nst = lam.shape[1] // 2

    def advance(s, a, bc, c):
        return s * a + pltpu.roll(s, nst, 1) * bc + c

    def sweep(order, c_ref, p_ref, a, bc):
        s = jnp.zeros((bsz, lam.shape[1]), F32)
        for n in order:
            rows = slice(n * bsz, (n + 1) * bsz)
            p_ref[rows, :] = s
            s = advance(s, a, bc, c_ref[rows, :])

    sweep(list(range(n_chunks)), cf_ref, pf_ref, lam[0:1], lam[1:2])
    order_b = list(range(nz_chunks - 1, -1, -1)) + list(range(n_chunks - 1, nz_chunks - 1, -1))
    sweep(order_b, cb_ref, pb_ref, lam[2:3], lam[3:4])
    y_ref[...] = (_dot(ub, m_ref[...])
                  + _dot(pf_ref[...].astype(BF16), wrf_ref[...])
                  + _dot(pb_ref[...].astype(BF16), wrb_ref[...]))


def _s5_weights(lam_re, lam_im, log_step, b_re, b_im, c_re, c_im, chunk):
    hp = lax.Precision.HIGHEST
    lam = lax.complex(lam_re.astype(F32), lam_im.astype(F32))
    lam_dt = lam * jnp.exp(log_step.astype(F32))[..., None]
    lam_bar = jnp.exp(lam_dt)
    b_bar = lax.complex(b_re.astype(F32), b_im.astype(F32)) * ((lam_bar - 1.0) / lam)[..., None]
    c_mat = lax.complex(c_re.astype(F32), c_im.astype(F32))
    g, n = lam.shape[1], lam.shape[2]
    ch = b_bar.shape[-1]
    steps = jnp.arange(chunk + 1, dtype=F32)
    pw = jnp.exp(steps[None, :, None, None] * lam_dt[:, None])
    kern = jnp.einsum('zgon,zdgn,zgni->zgdoi', c_mat, pw[:, :chunk], b_bar, precision=hp).real
    zero_lag = kern[0, :, 0] + kern[1, :, 0]
    lag_table = jnp.concatenate([kern[1, :, :0:-1], zero_lag[:, None], kern[0, :, 1:]], axis=1)
    t_idx = jnp.arange(chunk)
    lag = t_idx[None, :] - t_idx[:, None] + (chunk - 1)
    m = lag_table[:, lag]
    m = m.transpose(0, 1, 4, 2, 3).reshape(g, chunk * ch, chunk * ch)

    def state_in(pw_s, b_dir):
        w = pw_s[:, :, :, None] * b_dir[None]
        w = jnp.concatenate([w.real, w.imag], axis=2)
        return w.transpose(1, 0, 3, 2).reshape(g, chunk * ch, 2 * n)

    def state_out(pw_t, c_dir):
        w = c_dir[None] * pw_t[:, :, None, :]
        w = jnp.concatenate([w.real, -w.imag], axis=3)
        return w.transpose(1, 3, 0, 2).reshape(g, 2 * n, chunk * ch)

    wsf = state_in(pw[0, chunk - 1::-1][:chunk], b_bar[0])
    wsb = state_in(pw[1, :chunk], b_bar[1])
    wrf = state_out(pw[0, 1:chunk + 1], c_mat[0])
    wrb = state_out(pw[1, chunk:0:-1], c_mat[1])
    lam_c = pw[:, chunk]
    rows = []
    for z in range(2):
        rows.append(jnp.concatenate([lam_c[z].real, lam_c[z].real], -1))
        rows.append(jnp.concatenate([-lam_c[z].imag, lam_c[z].imag], -1))
    lam_rows = jnp.stack(rows + [jnp.zeros_like(rows[0])] * 4, axis=1)
    return m.astype(BF16), wsf.astype(BF16), wsb.astype(BF16), wrf.astype(BF16), wrb.astype(BF16), lam_rows


def _s5_scan(u, weights, n_ctx):
    bsz, s, width = u.shape
    m, wsf, wsb, wrf, wrb, lam_rows = weights
    g = m.shape[0]
    ch = width // g
    chunk = S5_CHUNK
    n_chunks = s // chunk
    k = chunk * ch
    nst2 = wsf.shape[2]
    rows = n_chunks * bsz
    ug = u.reshape(bsz, n_chunks, chunk, g, ch).transpose(3, 1, 0, 2, 4).reshape(g, rows, k)
    per_g = lambda a, b: pl.BlockSpec((None, a, b), lambda i: (i, 0, 0))
    y = pl.pallas_call(
        functools.partial(_s5_kernel, bsz=bsz, n_chunks=n_chunks, nz_chunks=n_ctx // chunk),
        grid=(g,),
        in_specs=[per_g(rows, k), per_g(k, k), per_g(k, nst2), per_g(k, nst2), per_g(nst2, k), per_g(nst2, k),
                  per_g(8, nst2)],
        out_specs=per_g(rows, k),
        out_shape=jax.ShapeDtypeStruct((g, rows, k), F32),
        scratch_shapes=[pltpu.VMEM((rows, nst2), F32)] * 4,
        compiler_params=_cparams(("parallel",)),
        name="s5_scan",
    )(ug, m, wsf, wsb, wrf, wrb, lam_rows)
    return y.reshape(g, n_chunks, bsz, chunk, ch).transpose(2, 1, 3, 0, 4).reshape(bsz, s, width)


def _attn_kernel(sink_ref, q_ref, k_ref, v_ref, o_ref, *, n_ctx, t_len, q_off, kv_heads):
    qi = pl.program_id(1) + q_off
    nzb = n_ctx // ATT_BLOCK
    band = 3 * ATT_BLOCK
    q = q_ref[...]
    kc = k_ref[0:n_ctx, :].astype(BF16)
    vc = v_ref[0:n_ctx, :].astype(BF16)

    def head_cols(a, i):
        return a[:, i * HEAD_DIM:(i + 1) * HEAD_DIM]

    def put(h, o):
        o_ref[:, h * HEAD_DIM:(h + 1) * HEAD_DIM] = o

    @pl.when(qi < nzb)
    def _():
        for h in range(kv_heads * ATT_REP):
            g = h // ATT_REP
            sink = sink_ref[h]
            s = _dot_nt(head_cols(q, h).astype(BF16), head_cols(kc, g))
            m = jnp.maximum(jnp.max(s, axis=1, keepdims=True), sink)
            e = jnp.exp(s - m)
            den = jnp.sum(e, axis=1, keepdims=True) + jnp.exp(sink - m)
            put(h, _dot(e.astype(BF16), head_cols(vc, g)) / den)

    @pl.when(qi >= nzb)
    def _():
        n = qi - nzb
        start = jnp.clip((n - 1) * ATT_BLOCK, 0, t_len - band)
        kl = k_ref[pl.ds(pl.multiple_of(n_ctx + start, ATT_BLOCK), band), :].astype(BF16)
        vl = v_ref[pl.ds(pl.multiple_of(n_ctx + start, ATT_BLOCK), band), :].astype(BF16)
        q_pos = n * ATT_BLOCK + lax.broadcasted_iota(jnp.int32, (ATT_BLOCK, band), 0)
        k_pos = start + lax.broadcasted_iota(jnp.int32, (ATT_BLOCK, band), 1)
        valid = jnp.abs(k_pos - q_pos) <= WINDOW
        for h in range(kv_heads * ATT_REP):
            g = h // ATT_REP
            sink = sink_ref[h]
            qh = head_cols(q, h).astype(BF16)
            s_loc = jnp.where(valid, _dot_nt(qh, head_cols(kl, g)), NEG_INF)
            s_ctx = _dot_nt(qh, head_cols(kc, g))
            m = jnp.maximum(jnp.maximum(jnp.max(s_loc, axis=1, keepdims=True),
                                        jnp.max(s_ctx, axis=1, keepdims=True)), sink)
            e_loc = jnp.exp(s_loc - m)
            e_ctx = jnp.exp(s_ctx - m)
            den = (jnp.sum(e_loc, axis=1, keepdims=True) + jnp.sum(e_ctx, axis=1, keepdims=True)
                   + jnp.exp(sink - m))
            o = _dot(e_loc.astype(BF16), head_cols(vl, g)) + _dot(e_ctx.astype(BF16), head_cols(vc, g))
            put(h, o / den)


def _attention(qa, ka, va, sink, n_ctx, need_ctx):
    bsz, s, qw = qa.shape
    kvw = ka.shape[2]
    t_len = s - n_ctx
    q_off = 0 if need_ctx else n_ctx // ATT_BLOCK
    nq = s // ATT_BLOCK - q_off
    return pl.pallas_call(
        functools.partial(_attn_kernel, n_ctx=n_ctx, t_len=t_len, q_off=q_off, kv_heads=kvw // HEAD_DIM),
        grid=(bsz, nq),
        in_specs=[pl.BlockSpec(memory_space=pltpu.SMEM),
                  pl.BlockSpec((None, ATT_BLOCK, qw), lambda b, j: (b, j + q_off, 0)),
                  pl.BlockSpec((None, s, kvw), lambda b, j: (b, 0, 0)),
                  pl.BlockSpec((None, s, kvw), lambda b, j: (b, 0, 0))],
        out_specs=pl.BlockSpec((None, ATT_BLOCK, qw), lambda b, j: (b, j, 0)),
        out_shape=jax.ShapeDtypeStruct((bsz, nq * ATT_BLOCK, qw), F32),
        compiler_params=_cparams(("parallel", "arbitrary")),
        name="window_attention",
    )(sink.astype(F32), qa, ka, va)


def _ret_kernel(lg_ref, q_ref, k_ref, v_ref, g_ref, o_ref, acc_ref, dec_ref, sf_ref, sb_ref,
                *, n_chunks, nz_chunks, heads):
    c = RET_CHUNK
    ii = lax.broadcasted_iota(jnp.int32, (c, c), 0)
    jj = lax.broadcasted_iota(jnp.int32, (c, c), 1)
    diff = (ii - jj).astype(F32)
    idx = lax.broadcasted_iota(jnp.int32, (c, 1), 0).astype(F32)
    for h in range(heads):
        lgf, lgb = lg_ref[0, h], lg_ref[1, h]
        dec_ref[h] = jnp.where(diff >= 0, jnp.exp(lgf * jnp.maximum(diff, 0.0)),
                               jnp.exp(lgb * jnp.maximum(-diff, 0.0)))
    sf_ref[...] = jnp.zeros_like(sf_ref)
    sb_ref[...] = jnp.zeros_like(sb_ref)

    def hcols(a, h):
        return a[:, h * HEAD_DIM:(h + 1) * HEAD_DIM]

    chunk_len = jnp.full((1, HEAD_DIM), float(c), F32)

    def fwd(n, carry):
        rows = pl.ds(pl.multiple_of(n * c, c), c)
        q, k, v = q_ref[rows, :], k_ref[rows, :], v_ref[rows, :]
        for h in range(heads):
            lgf = lg_ref[0, h]
            cols = slice(h * HEAD_DIM, (h + 1) * HEAD_DIM)
            qh, kh, vh = hcols(q, h), hcols(k, h), hcols(v, h).astype(BF16)
            scores = _dot_nt(qh.astype(BF16), kh.astype(BF16)) * dec_ref[h]
            s_prev = sf_ref[h]
            o = _dot(scores.astype(BF16), vh)
            o = o + _dot((qh * jnp.exp(lgf * (idx + 1.0))).astype(BF16), s_prev.astype(BF16))
            kw = kh * jnp.exp(lgf * (c - 1.0 - idx))
            sf_ref[h] = jnp.exp(lgf * chunk_len) * s_prev + _dot_tn(kw.astype(BF16), vh)
            acc_ref[rows, cols] = o
        return carry

    lax.fori_loop(0, n_chunks, fwd, 0)

    def bwd(i, carry):
        n = jnp.where(i < nz_chunks, nz_chunks - 1 - i, n_chunks - 1 - i + nz_chunks)
        rows = pl.ds(pl.multiple_of(n * c, c), c)
        q, k, v = q_ref[rows, :], k_ref[rows, :], v_ref[rows, :]
        acc = acc_ref[rows, :]
        gate = _silu(g_ref[rows, :])
        for h in range(heads):
            lgb = lg_ref[1, h]
            cols = slice(h * HEAD_DIM, (h + 1) * HEAD_DIM)
            qh, kh, vh = hcols(q, h), hcols(k, h), hcols(v, h).astype(BF16)
            s_prev = sb_ref[h]
            o = hcols(acc, h) + _dot((qh * jnp.exp(lgb * (c - idx))).astype(BF16), s_prev.astype(BF16))
            kw = kh * jnp.exp(lgb * idx)
            sb_ref[h] = jnp.exp(lgb * chunk_len) * s_prev + _dot_tn(kw.astype(BF16), vh)
            o_ref[rows, cols] = _ln(o) * hcols(gate, h)
        return carry

    lax.fori_loop(0, n_chunks, bwd, 0)


def _retention(qr, kr, vr, gr, log_gamma, n_ctx):
    bsz, s, w = qr.shape
    heads = w // HEAD_DIM
    n_chunks = s // RET_CHUNK
    tok = pl.BlockSpec((None, s, w), lambda b: (b, 0, 0))
    return pl.pallas_call(
        functools.partial(_ret_kernel, n_chunks=n_chunks, nz_chunks=n_ctx // RET_CHUNK, heads=heads),
        grid=(bsz,),
        in_specs=[pl.BlockSpec(memory_space=pltpu.SMEM), tok, tok, tok, tok],
        out_specs=tok,
        out_shape=jax.ShapeDtypeStruct((bsz, s, w), F32),
        scratch_shapes=[pltpu.VMEM((s, w), F32),
                        pltpu.VMEM((heads, RET_CHUNK, RET_CHUNK), F32),
                        pltpu.VMEM((heads, HEAD_DIM, HEAD_DIM), F32),
                        pltpu.VMEM((heads, HEAD_DIM, HEAD_DIM), F32)],
        compiler_params=_cparams(("parallel",)),
        name="retention",
    )(log_gamma.astype(F32), qr, kr, vr, gr)


def _outproj_kernel(*refs, nzb, alpha, with_router, w_s5, w_att, n_exp):
    if with_router:
        (y_ref, u_ref, a_ref, r_ref, x_ref, modx_ref, modz_ref, d_ref, wg_ref, bg_ref, wo_ref,
         g1_ref, b1_ref, rt_ref, x1_ref, fx_ref, gate_ref) = refs
    else:
        (y_ref, u_ref, a_ref, r_ref, x_ref, modx_ref, modz_ref, d_ref, wg_ref, bg_ref, wo_ref,
         g1_ref, b1_ref, x1_ref, fx_ref) = refs
    is_ctx = pl.program_id(1) < nzb
    mod = jnp.where(is_ctx, modz_ref[...], modx_ref[...])
    g = _gelu_tanh(y_ref[...] + d_ref[...] * u_ref[...])
    gb = g.astype(BF16)
    s5 = g * _sigmoid(_dot(gb, wg_ref[...]) + bg_ref[...])
    mix = (_dot(s5.astype(BF16), wo_ref[0:w_s5, :])
           + _dot(a_ref[...].astype(BF16), wo_ref[w_s5:w_s5 + w_att, :])
           + _dot(r_ref[...].astype(BF16), wo_ref[w_s5 + w_att:, :]))
    x1 = _ln(alpha * x_ref[...] + mod[2:3] * mix) * g1_ref[...] + b1_ref[...]
    x1_ref[...] = x1
    fx = _ln(x1) * (1.0 + mod[4:5]) + mod[3:4]
    fx_ref[...] = fx.astype(fx_ref.dtype)
    if with_router:
        lane = lax.broadcasted_iota(jnp.int32, (fx.shape[0], LANES), 1)
        logits = jnp.where(lane < n_exp, _dot3(fx, rt_ref[...]), -jnp.inf)
        m1 = jnp.max(logits, axis=1, keepdims=True)
        i1 = jnp.min(jnp.where(logits == m1, lane, LANES), axis=1, keepdims=True)
        rest = jnp.where(lane == i1, -jnp.inf, logits)
        m2 = jnp.max(rest, axis=1, keepdims=True)
        i2 = jnp.min(jnp.where(rest == m2, lane, LANES), axis=1, keepdims=True)
        e2 = jnp.exp(m2 - m1)
        den = 1.0 + e2
        route = jnp.where(lane == 0, 1.0 / den, jnp.where(lane == 1, e2 / den, 0.0))
        route = jnp.where(lane == 2, i1.astype(F32), jnp.where(lane == 3, i2.astype(F32), route))
        gate_ref[...] = route


def _outproj(y_s5, u, o_att, o_ret, xz, mod, s5_d, w_glu, b_glu, w_out, ln_g, ln_b, router, n_ctx, need_ctx,
             alpha):
    bsz, s, d = xz.shape
    tm = TOKEN_BLOCK
    nzb = n_ctx // tm
    off = 0 if need_ctx else nzb
    nblk = s // tm - off
    s_out = nblk * tm
    nrow = mod.shape[0]
    w_s5, w_att = y_s5.shape[2], o_att.shape[2]
    att_off = off if o_att.shape[1] == s else 0

    def tok(width, shift):
        return pl.BlockSpec((None, tm, width), lambda b, j: (b, j + shift, 0))

    def full(a):
        return pl.BlockSpec(a.shape, lambda b, j: (0,) * a.ndim)

    vec = lambda a: a.reshape(1, -1).astype(F32)
    consts = [vec(s5_d), w_glu.astype(BF16), vec(b_glu), w_out.astype(BF16), vec(ln_g), vec(ln_b)]
    in_specs = [tok(w_s5, off), tok(w_s5, off), tok(w_att, att_off), tok(o_ret.shape[2], off), tok(d, off),
                pl.BlockSpec((None, 6, d), lambda b, j: (b, 0, 0)),
                pl.BlockSpec((None, 6, d), lambda b, j: (nrow - 1, 0, 0))] + [full(a) for a in consts]
    out_specs = [tok(d, 0), tok(d, 0)]
    fx_dtype = BF16 if router is None else F32
    out_shape = [jax.ShapeDtypeStruct((bsz, s_out, d), F32), jax.ShapeDtypeStruct((bsz, s_out, d), fx_dtype)]
    args = [y_s5, u, o_att, o_ret, xz, mod, mod] + consts
    with_router = router is not None
    n_exp = 0
    if with_router:
        n_exp = router.shape[1]
        router_pad = jnp.pad(router.astype(F32), ((0, 0), (0, LANES - n_exp)))
        args.append(router_pad)
        in_specs.append(full(router_pad))
        out_specs.append(tok(LANES, 0))
        out_shape.append(jax.ShapeDtypeStruct((bsz, s_out, LANES), F32))
    return pl.pallas_call(
        functools.partial(_outproj_kernel, nzb=nzb - off, alpha=alpha, with_router=with_router,
                          w_s5=w_s5, w_att=w_att, n_exp=n_exp),
        grid=(bsz, nblk),
        in_specs=in_specs,
        out_specs=out_specs,
        out_shape=out_shape,
        compiler_params=_cparams(("parallel", "parallel")),
        name="outproj",
    )(*args)


def _swiglu_into(x_bf16, w1_ref, w3_ref, w2_ref, acc_ref, fc):
    for s in range(0, w1_ref.shape[1], fc):
        h1 = _dot(x_bf16, w1_ref[:, s:s + fc])
        h3 = _dot(x_bf16, w3_ref[:, s:s + fc])
        acc_ref[...] += _dot((_silu(h1) * h3).astype(BF16), w2_ref[s:s + fc, :])


def _ffn_kernel(fx_ref, x1_ref, modx_ref, modz_ref, w1_ref, w3_ref, w2_ref, g2_ref, b2_ref,
                o_ref, acc_ref, *, nzb, alpha, fc):
    acc_ref[...] = jnp.zeros_like(acc_ref)
    _swiglu_into(fx_ref[...], w1_ref, w3_ref, w2_ref, acc_ref, fc)
    is_ctx = pl.program_id(1) < nzb
    mod = jnp.where(is_ctx, modz_ref[...], modx_ref[...])
    o_ref[...] = _ln(alpha * x1_ref[...] + mod[5:6] * acc_ref[...]) * g2_ref[...] + b2_ref[...]


def _ffn(fx, x1, mod, w1, w3, w2, ln_g, ln_b, n_ctx_tokens, alpha):
    bsz, s, d = x1.shape
    ff = w1.shape[1]
    tm = TOKEN_BLOCK
    nrow = mod.shape[0]
    tok = pl.BlockSpec((None, tm, d), lambda b, j: (b, j, 0))
    full = lambda a: pl.BlockSpec(a.shape, lambda b, j: (0,) * a.ndim)
    vec = lambda a: a.reshape(1, -1).astype(F32)
    return pl.pallas_call(
        functools.partial(_ffn_kernel, nzb=n_ctx_tokens // tm, alpha=alpha, fc=_pick_chunk(ff, 512)),
        grid=(bsz, s // tm),
        in_specs=[tok, tok,
                  pl.BlockSpec((None, 6, d), lambda b, j: (b, 0, 0)),
                  pl.BlockSpec((None, 6, d), lambda b, j: (nrow - 1, 0, 0)),
                  full(w1), full(w3), full(w2),
                  pl.BlockSpec((1, d), lambda b, j: (0, 0)), pl.BlockSpec((1, d), lambda b, j: (0, 0))],
        out_specs=tok,
        out_shape=jax.ShapeDtypeStruct((bsz, s, d), F32),
        scratch_shapes=[pltpu.VMEM((tm, d), F32)],
        compiler_params=_cparams(("parallel", "parallel")),
        name="dense_ffn",
    )(fx, x1, mod, mod, w1, w3, w2, vec(ln_g), vec(ln_b))


MOE_TILE = 1024


def _route_plan(e1, e2, n_exp, tile):
    n = e1.shape[0]
    pair_e = jnp.stack([e1, e2], axis=1).reshape(-1)
    onehot = (pair_e[:, None] == jnp.arange(n_exp, dtype=jnp.int32)[None, :]).astype(jnp.int32)
    before = jnp.cumsum(onehot, axis=0) - onehot
    rank = jnp.sum(before * onehot, axis=1)
    counts = jnp.sum(onehot, axis=0)
    padded = (counts + tile - 1) // tile * tile
    ends = jnp.cumsum(padded)
    starts = ends - padded
    dest = starts[pair_e] + rank
    n_rows = (2 * n + n_exp * (tile - 1)) // tile * tile
    n_tiles = n_rows // tile
    row_token = jnp.zeros((n_rows,), jnp.int32).at[dest].set(jnp.arange(2 * n, dtype=jnp.int32) // 2)
    tile_start = jnp.arange(n_tiles, dtype=jnp.int32) * tile
    tile_expert = jnp.minimum(jnp.sum((tile_start[:, None] >= ends[None, :]).astype(jnp.int32), axis=1),
                              n_exp - 1)
    n_used = (ends[-1] // tile).astype(jnp.int32).reshape(1)
    return row_token, dest.reshape(n, 2), tile_expert, n_used


def _moe_gather_kernel(tok_ref, x_hbm, o_hbm, sem, *, tile):
    base = pl.program_id(0) * tile

    def row_copy(r):
        return pltpu.make_async_copy(x_hbm.at[pl.ds(tok_ref[0, r], 1)], o_hbm.at[pl.ds(base + r, 1)], sem)

    def issue(r, carry):
        row_copy(r).start()
        return carry

    def drain(r, carry):
        row_copy(r).wait()
        return carry

    lax.fori_loop(0, tile, issue, 0)
    lax.fori_loop(0, tile, drain, 0)


def _moe_gather(x_flat, row_token, tile):
    n_rows = row_token.shape[0]
    n_tiles = n_rows // tile
    d = x_flat.shape[1]
    return pl.pallas_call(
        functools.partial(_moe_gather_kernel, tile=tile),
        grid=(n_tiles,),
        in_specs=[pl.BlockSpec((None, 1, tile), lambda i: (i, 0, 0), memory_space=pltpu.SMEM),
                  pl.BlockSpec(memory_space=pl.ANY)],
        out_specs=pl.BlockSpec(memory_space=pl.ANY),
        out_shape=jax.ShapeDtypeStruct((n_rows, d), x_flat.dtype),
        scratch_shapes=[pltpu.SemaphoreType.DMA(())],
        compiler_params=_cparams(("arbitrary",)),
        name="moe_gather",
    )(row_token.reshape(n_tiles, 1, tile), x_flat)


def _moe_gemm_kernel(te_ref, nused_ref, x_ref, w1_ref, w3_ref, w2_ref, y_ref, *, fc):
    @pl.when(pl.program_id(1) == 0)
    def _():
        y_ref[...] = jnp.zeros_like(y_ref)

    @pl.when(pl.program_id(0) < nused_ref[0])
    def _():
        _swiglu_into(x_ref[...].astype(BF16), w1_ref, w3_ref, w2_ref, y_ref, fc)


def _moe_gemm(xs, tile_expert, n_used, w1, w3, w2, tile):
    n_rows, d = xs.shape
    ff = w1.shape[2]
    nf = 2 if ff % (2 * LANES) == 0 else 1
    tf = ff // nf
    last = lambda i, nu: jnp.minimum(i, nu[0] - 1)
    fsel = lambda i, f, nu: jnp.where(i < nu[0], f, nf - 1)
    return pl.pallas_call(
        functools.partial(_moe_gemm_kernel, fc=_pick_chunk(tf, 256)),
        grid_spec=pltpu.PrefetchScalarGridSpec(
            num_scalar_prefetch=2,
            grid=(n_rows // tile, nf),
            in_specs=[pl.BlockSpec((tile, d), lambda i, f, te, nu: (last(i, nu), 0)),
                      pl.BlockSpec((None, d, tf), lambda i, f, te, nu: (te[last(i, nu)], 0, fsel(i, f, nu))),
                      pl.BlockSpec((None, d, tf), lambda i, f, te, nu: (te[last(i, nu)], 0, fsel(i, f, nu))),
                      pl.BlockSpec((None, tf, d), lambda i, f, te, nu: (te[last(i, nu)], fsel(i, f, nu), 0))],
            out_specs=pl.BlockSpec((tile, d), lambda i, f, te, nu: (i, 0))),
        out_shape=jax.ShapeDtypeStruct((n_rows, d), F32),
        compiler_params=_cparams(("arbitrary", "arbitrary")),
        name="moe_gemm",
    )(tile_expert, n_used, xs, w1, w3, w2)


def _moe_combine_kernel(pos_ref, posn_ref, route_ref, x1_ref, modx_ref, modz_ref, g2_ref, b2_ref, y_hbm,
                        o_ref, buf_ref, sem, *, nzb, nblk, alpha, tm):
    i = pl.program_id(0)
    slot = i % 2

    def row_copy(p_ref, r, k, s):
        return pltpu.make_async_copy(y_hbm.at[pl.ds(p_ref[0, k * tm + r], 1)],
                                     buf_ref.at[s, k, pl.ds(r, 1)], sem.at[s])

    def issue(p_ref, s):
        def body(r, carry):
            row_copy(p_ref, r, 0, s).start()
            row_copy(p_ref, r, 1, s).start()
            return carry
        lax.fori_loop(0, tm, body, 0)

    @pl.when(i == 0)
    def _():
        issue(pos_ref, 0)

    @pl.when(i + 1 < pl.num_programs(0))
    def _():
        issue(posn_ref, 1 - slot)

    def drain(r, carry):
        row_copy(pos_ref, r, 0, slot).wait()
        row_copy(pos_ref, r, 1, slot).wait()
        return carry

    lax.fori_loop(0, tm, drain, 0)
    route = route_ref[...]
    f = route[:, 0:1] * buf_ref[slot, 0] + route[:, 1:2] * buf_ref[slot, 1]
    is_ctx = (i % nblk) < nzb
    mod = jnp.where(is_ctx, modz_ref[...], modx_ref[...])
    o_ref[...] = _ln(alpha * x1_ref[...] + mod[5:6] * f) * g2_ref[...] + b2_ref[...]


def _moe_combine(ys, pos, route, x1, mod, ln_g, ln_b, n_ctx_tokens, alpha):
    bsz, s, d = x1.shape
    tm = TOKEN_BLOCK
    nblk = s // tm
    n_steps = bsz * nblk
    nrow = mod.shape[0]
    pos_steps = pos.reshape(n_steps, tm, 2).transpose(0, 2, 1).reshape(n_steps, 1, 2 * tm)
    tok = lambda width: pl.BlockSpec((tm, width), lambda i: (i, 0))
    smem = lambda imap: pl.BlockSpec((None, 1, 2 * tm), imap, memory_space=pltpu.SMEM)
    vec = lambda a: a.reshape(1, -1).astype(F32)
    out = pl.pallas_call(
        functools.partial(_moe_combine_kernel, nzb=n_ctx_tokens // tm, nblk=nblk, alpha=alpha, tm=tm),
        grid=(n_steps,),
        in_specs=[smem(lambda i: (i, 0, 0)),
                  smem(lambda i: (jnp.minimum(i + 1, n_steps - 1), 0, 0)),
                  tok(LANES), tok(d),
                  pl.BlockSpec((None, 6, d), lambda i: (i // nblk, 0, 0)),
                  pl.BlockSpec((None, 6, d), lambda i: (nrow - 1, 0, 0)),
                  pl.BlockSpec((1, d), lambda i: (0, 0)), pl.BlockSpec((1, d), lambda i: (0, 0)),
                  pl.BlockSpec(memory_space=pl.ANY)],
        out_specs=tok(d),
        out_shape=jax.ShapeDtypeStruct((bsz * s, d), F32),
        scratch_shapes=[pltpu.VMEM((2, 2, tm, d), F32), pltpu.SemaphoreType.DMA((2,))],
        compiler_params=_cparams(("arbitrary",)),
        name="moe_combine",
    )(pos_steps, pos_steps, route.reshape(bsz * s, LANES), x1.reshape(bsz * s, d), mod, mod,
      vec(ln_g), vec(ln_b), ys)
    return out.reshape(bsz, s, d)


def _moe(fx, x1, route, mod, w1, w3, w2, ln_g, ln_b, n_ctx_tokens, alpha):
    bsz, s, d = x1.shape
    n_exp = w1.shape[0]
    idx = route.reshape(bsz * s, LANES)[:, 2:4].astype(jnp.int32)
    row_token, pos, tile_expert, n_used = _route_plan(idx[:, 0], idx[:, 1], n_exp, MOE_TILE)
    xs = _moe_gather(fx.reshape(bsz * s, d), row_token, MOE_TILE)
    ys = _moe_gemm(xs, tile_expert, n_used, w1, w3, w2, MOE_TILE)
    return _moe_combine(ys, pos, route, x1, mod, ln_g, ln_b, n_ctx_tokens, alpha)


def _pick_chunk(total, target):
    best = LANES
    for c in range(LANES, target + 1, LANES):
        if total % c == 0:
            best = c
    return best


def kernel(x, c, ctx, c_ctx, w_mod, b_mod, w_in, s5_lam_re, s5_lam_im, s5_log_step, s5_b_re, s5_b_im,
           s5_c_re, s5_c_im, s5_d, s5_w_glu, s5_b_glu, attn_sink, ret_log_gamma, w_out,
           ln1_g, ln1_b, ln2_g, ln2_b, ffn_w1, ffn_w3, ffn_w2, moe_router, moe_w1, moe_w3, moe_w2):
    bsz, t_len, d = x.shape
    n_ctx = ctx.shape[1]
    depth = w_in.shape[0]
    alpha = (2 * depth) ** 0.25
    s5_w = s5_d.shape[1]
    att_w = attn_sink.shape[1] * HEAD_DIM
    kv_w = att_w // ATT_REP
    ret_w = ret_log_gamma.shape[2] * HEAD_DIM
    sizes = (s5_w, att_w, kv_w, kv_w, ret_w, ret_w, ret_w, ret_w)
    assert sum(sizes) == w_in.shape[2] and s5_w + att_w + ret_w == w_out.shape[1]
    assert n_ctx % TOKEN_BLOCK == 0 and t_len % TOKEN_BLOCK == 0 and t_len >= 3 * ATT_BLOCK

    pad = (-(bsz + 1)) % 8
    cvec = jnp.concatenate([jnp.zeros((pad, d), F32), c_ctx[None].astype(F32)], axis=0)
    cvec = jnp.concatenate([c.astype(F32), cvec], axis=0)
    mod_all = _modulation(cvec, w_mod.astype(F32), b_mod.astype(F32)).reshape(depth, bsz + pad + 1, 6, d)

    tabs = _rope_tables(t_len)
    xz = jnp.concatenate([ctx, x], axis=1).astype(F32)
    for l in range(depth):
        need_ctx = l < depth - 1
        mod = mod_all[l]
        u, qa, ka, va, qr, kr, vr, gr = _inproj(xz, mod, w_in[l].astype(BF16), tabs, sizes, n_ctx)
        s5w = _s5_weights(s5_lam_re[l], s5_lam_im[l], s5_log_step[l], s5_b_re[l], s5_b_im[l],
                          s5_c_re[l], s5_c_im[l], S5_CHUNK)
        y_s5 = _s5_scan(u, s5w, n_ctx)
        o_att = _attention(qa, ka, va, attn_sink[l], n_ctx, need_ctx)
        o_ret = _retention(qr, kr, vr, gr, ret_log_gamma[l], n_ctx)
        i = l // 2
        router = None if l % 2 == 0 else moe_router[i]
        outs = _outproj(y_s5, u, o_att, o_ret, xz, mod, s5_d[l], s5_w_glu[l], s5_b_glu[l], w_out[l],
                        ln1_g[l], ln1_b[l], router, n_ctx, need_ctx, alpha)
        ctx_tokens = n_ctx if need_ctx else 0
        if l % 2 == 0:
            x1, fx = outs
            xz = _ffn(fx, x1, mod, ffn_w1[i].astype(BF16), ffn_w3[i].astype(BF16), ffn_w2[i].astype(BF16),
                      ln2_g[l], ln2_b[l], ctx_tokens, alpha)
        else:
            x1, fx, route = outs
            xz = _moe(fx, x1, route, mod, moe_w1[i].astype(BF16), moe_w3[i].astype(BF16),
                      moe_w2[i].astype(BF16), ln2_g[l], ln2_b[l], ctx_tokens, alpha)
    return xz if xz.shape[1] == t_len else xz[:, n_ctx:]
```

```python
import functools
import math

import jax
import jax.numpy as jnp
import numpy as np
from jax import lax
from jax.experimental import pallas as pl
from jax.experimental.pallas import tpu as pltpu

F32 = jnp.float32
BF16 = jnp.bfloat16

GRID_W = 64
HEAD_DIM = 64
S5_GROUP = 16
ATT_REP = 4
WINDOW = 128
ATT_BLOCK = 128
RET_CHUNK = 128
TOP_K = 2
LN_EPS = 1e-5
ROPE_BASE = 10000.0
NEG_INF = -1e30

LANES = 128
S5_CHUNK = 64
TOKEN_BLOCK = 256
VMEM_LIMIT = 56 * 1024 * 1024


def _cparams(sem):
    return pltpu.CompilerParams(dimension_semantics=sem, vmem_limit_bytes=VMEM_LIMIT)


def _dot(a, b):
    return jnp.dot(a, b, preferred_element_type=F32)


def _dot_nt(a, b):
    return lax.dot_general(a, b, (((1,), (1,)), ((), ())), preferred_element_type=F32)


def _dot_tn(a, b):
    return lax.dot_general(a, b, (((0,), (0,)), ((), ())), preferred_element_type=F32)


def _split_bf16(a):
    hi = a.astype(BF16)
    lo = (a - hi.astype(F32)).astype(BF16)
    return hi, lo


def _dot3(a, b):
    ah, al = _split_bf16(a)
    bh, bl = _split_bf16(b)
    return _dot(ah, bh) + (_dot(ah, bl) + _dot(al, bh))


def _ln(x):
    mu = jnp.mean(x, axis=-1, keepdims=True)
    xc = x - mu
    var = jnp.mean(xc * xc, axis=-1, keepdims=True)
    return xc * lax.rsqrt(var + LN_EPS)


def _silu(x):
    return x * (1.0 / (1.0 + jnp.exp(-x)))


def _sigmoid(x):
    return 1.0 / (1.0 + jnp.exp(-x))


def _gelu_tanh(x):
    c = math.sqrt(2.0 / math.pi)
    return 0.5 * x * (1.0 + jnp.tanh(c * (x + 0.044715 * (x * x * x))))


def _mod_kernel(c_ref, w_ref, b_ref, o_ref):
    o_ref[...] = _dot3(_silu(c_ref[...]), w_ref[...]) + b_ref[...]


def _modulation(cvec, w_mod, b_mod):
    depth, d, n = w_mod.shape
    rows = cvec.shape[0]
    tn = 1536
    return pl.pallas_call(
        _mod_kernel,
        grid=(depth, n // tn),
        in_specs=[pl.BlockSpec((rows, d), lambda l, j: (0, 0)),
                  pl.BlockSpec((None, d, tn), lambda l, j: (l, 0, j)),
                  pl.BlockSpec((None, 1, tn), lambda l, j: (l, 0, j))],
        out_specs=pl.BlockSpec((None, rows, tn), lambda l, j: (l, 0, j)),
        out_shape=jax.ShapeDtypeStruct((depth, rows, n), F32),
        compiler_params=_cparams(("parallel", "parallel")),
        name="modulation",
    )(cvec, w_mod, b_mod.reshape(depth, 1, n))


def _rope_slab(xs, cos, sa, sb, half):
    return xs * cos + pltpu.roll(xs, LANES - half, 1) * sa + pltpu.roll(xs, half, 1) * sb


def _inproj_kernel(x_ref, modx_ref, modz_ref, w_ref, ac_ref, asa_ref, asb_ref, rc_ref, rsa_ref, rsb_ref,
                   u_ref, qa_ref, ka_ref, va_ref, qr_ref, kr_ref, vr_ref, gr_ref, *, nzb, sizes):
    is_ctx = pl.program_id(1) < nzb
    mod = jnp.where(is_ctx, modz_ref[...], modx_ref[...])
    h = _ln(x_ref[...]) * (1.0 + mod[1:2]) + mod[0:1]
    p = _dot(h.astype(BF16), w_ref[...])
    ac = jnp.where(is_ctx, 1.0, ac_ref[...])
    asa = jnp.where(is_ctx, 0.0, asa_ref[...])
    asb = jnp.where(is_ctx, 0.0, asb_ref[...])
    rc = jnp.where(is_ctx, 1.0, rc_ref[...])
    rsa = jnp.where(is_ctx, 0.0, rsa_ref[...])
    rsb = jnp.where(is_ctx, 0.0, rsb_ref[...])
    offs = np.concatenate([[0], np.cumsum(sizes)])
    scale = HEAD_DIM ** -0.5

    def cols(i):
        return p[:, int(offs[i]):int(offs[i + 1])]

    def rope_cols(i, cos, sa, sb, half, mul):
        blk = cols(i)
        slabs = [_rope_slab(blk[:, s:s + LANES], cos, sa, sb, half) for s in range(0, blk.shape[1], LANES)]
        out = slabs[0] if len(slabs) == 1 else jnp.concatenate(slabs, axis=1)
        return out * mul if mul != 1.0 else out

    u_ref[...] = cols(0)
    qa_ref[...] = rope_cols(1, ac, asa, asb, HEAD_DIM // 4, scale)
    ka_ref[...] = rope_cols(2, ac, asa, asb, HEAD_DIM // 4, 1.0)
    va_ref[...] = cols(3)
    qr_ref[...] = rope_cols(4, rc, rsa, rsb, HEAD_DIM // 2, 1.0)
    kr_ref[...] = rope_cols(5, rc, rsa, rsb, HEAD_DIM // 2, scale)
    vr_ref[...] = cols(6)
    gr_ref[...] = cols(7)


def _inproj(xz, mod, w_in, tabs, sizes, n_ctx):
    bsz, s, d = xz.shape
    tm = TOKEN_BLOCK
    nzb = n_ctx // tm
    nrow = mod.shape[0]
    n_in = w_in.shape[1]
    tok = lambda width: pl.BlockSpec((None, tm, width), lambda b, j: (b, j, 0))
    tab = pl.BlockSpec((tm, LANES), lambda b, j: (jnp.maximum(j - nzb, 0), 0))
    return pl.pallas_call(
        functools.partial(_inproj_kernel, nzb=nzb, sizes=sizes),
        grid=(bsz, s // tm),
        in_specs=[tok(d),
                  pl.BlockSpec((None, 6, d), lambda b, j: (b, 0, 0)),
                  pl.BlockSpec((None, 6, d), lambda b, j: (nrow - 1, 0, 0)),
                  pl.BlockSpec((d, n_in), lambda b, j: (0, 0)),
                  tab, tab, tab, tab, tab, tab],
        out_specs=[tok(w) for w in sizes],
        out_shape=[jax.ShapeDtypeStruct((bsz, s, w), F32) for w in sizes],
        compiler_params=_cparams(("parallel", "parallel")),
        name="inproj",
    )(xz, mod, mod, w_in, *tabs)


def _rope_tables(t_len):
    t = jnp.arange(t_len)
    rows = (t // GRID_W).astype(F32)
    cols = (t % GRID_W).astype(F32)
    pos = t.astype(F32)

    def angles(p, dim):
        inv_freq = ROPE_BASE ** (-jnp.arange(0, dim, 2, dtype=F32) / dim)
        return p[:, None] * inv_freq[None, :]

    def head_tables(angs):
        cos = jnp.concatenate([jnp.concatenate([jnp.cos(a), jnp.cos(a)], -1) for a in angs], -1)
        sa = jnp.concatenate([jnp.concatenate([-jnp.sin(a), jnp.zeros_like(a)], -1) for a in angs], -1)
        sb = jnp.concatenate([jnp.concatenate([jnp.zeros_like(a), jnp.sin(a)], -1) for a in angs], -1)
        rep = LANES // HEAD_DIM
        return tuple(jnp.tile(x, (1, rep)) for x in (cos, sa, sb))

    att = head_tables([angles(rows, HEAD_DIM // 2), angles(cols, HEAD_DIM // 2)])
    ret = head_tables([angles(pos, HEAD_DIM)])
    return att + ret


def _s5_kernel(u_ref, m_ref, wsf_ref, wsb_ref, wrf_ref, wrb_ref, lam_ref, y_ref,
               cf_ref, cb_ref, pf_ref, pb_ref, *, bsz, n_chunks, nz_chunks):
    ub = u_ref[...].astype(BF16)
    cf_ref[...] = _dot(ub, wsf_ref[...])
    cb_ref[...] = _dot(ub, wsb_ref[...])
    lam = lam_ref[...]
    nst = lam.shape[1] // 2

    def advance(s, a, bc, c):
        return s * a + pltpu.roll(s, nst, 1) * bc + c

    def sweep(order, c_ref, p_ref, a, bc):
        s = jnp.zeros((bsz, lam.shape[1]), F32)
        for n in order:
            rows = slice(n * bsz, (n + 1) * bsz)
            p_ref[rows, :] = s
            s = advance(s, a, bc, c_ref[rows, :])

    sweep(list(range(n_chunks)), cf_ref, pf_ref, lam[0:1], lam[1:2])
    order_b = list(range(nz_chunks - 1, -1, -1)) + list(range(n_chunks - 1, nz_chunks - 1, -1))
    sweep(order_b, cb_ref, pb_ref, lam[2:3], lam[3:4])
    y_ref[...] = (_dot(ub, m_ref[...])
                  + _dot(pf_ref[...].astype(BF16), wrf_ref[...])
                  + _dot(pb_ref[...].astype(BF16), wrb_ref[...]))


def _s5_weights(lam_re, lam_im, log_step, b_re, b_im, c_re, c_im, chunk):
    hp = lax.Precision.HIGHEST
    lam = lax.complex(lam_re.astype(F32), lam_im.astype(F32))
    lam_dt = lam * jnp.exp(log_step.astype(F32))[..., None]
    lam_bar = jnp.exp(lam_dt)
    b_bar = lax.complex(b_re.astype(F32), b_im.astype(F32)) * ((lam_bar - 1.0) / lam)[..., None]
    c_mat = lax.complex(c_re.astype(F32), c_im.astype(F32))
    g, n = lam.shape[1], lam.shape[2]
    ch = b_bar.shape[-1]
    steps = jnp.arange(chunk + 1, dtype=F32)
    pw = jnp.exp(steps[None, :, None, None] * lam_dt[:, None])
    kern = jnp.einsum('zgon,zdgn,zgni->zgdoi', c_mat, pw[:, :chunk], b_bar, precision=hp).real
    zero_lag = kern[0, :, 0] + kern[1, :, 0]
    lag_table = jnp.concatenate([kern[1, :, :0:-1], zero_lag[:, None], kern[0, :, 1:]], axis=1)
    t_idx = jnp.arange(chunk)
    lag = t_idx[None, :] - t_idx[:, None] + (chunk - 1)
    m = lag_table[:, lag]
    m = m.transpose(0, 1, 4, 2, 3).reshape(g, chunk * ch, chunk * ch)

    def state_in(pw_s, b_dir):
        w = pw_s[:, :, :, None] * b_dir[None]
        w = jnp.concatenate([w.real, w.imag], axis=2)
        return w.transpose(1, 0, 3, 2).reshape(g, chunk * ch, 2 * n)

    def state_out(pw_t, c_dir):
        w = c_dir[None] * pw_t[:, :, None, :]
        w = jnp.concatenate([w.real, -w.imag], axis=3)
        return w.transpose(1, 3, 0, 2).reshape(g, 2 * n, chunk * ch)

    wsf = state_in(pw[0, chunk - 1::-1][:chunk], b_bar[0])
    wsb = state_in(pw[1, :chunk], b_bar[1])
    wrf = state_out(pw[0, 1:chunk + 1], c_mat[0])
    wrb = state_out(pw[1, chunk:0:-1], c_mat[1])
    lam_c = pw[:, chunk]
    rows = []
    for z in range(2):
        rows.append(jnp.concatenate([lam_c[z].real, lam_c[z].real], -1))
        rows.append(jnp.concatenate([-lam_c[z].imag, lam_c[z].imag], -1))
    lam_rows = jnp.stack(rows + [jnp.zeros_like(rows[0])] * 4, axis=1)
    return m.astype(BF16), wsf.astype(BF16), wsb.astype(BF16), wrf.astype(BF16), wrb.astype(BF16), lam_rows


def _s5_scan(u, weights, n_ctx):
    bsz, s, width = u.shape
    m, wsf, wsb, wrf, wrb, lam_rows = weights
    g = m.shape[0]
    ch = width // g
    chunk = S5_CHUNK
    n_chunks = s // chunk
    k = chunk * ch
    nst2 = wsf.shape[2]
    rows = n_chunks * bsz
    ug = u.reshape(bsz, n_chunks, chunk, g, ch).transpose(3, 1, 0, 2, 4).reshape(g, rows, k)
    per_g = lambda a, b: pl.BlockSpec((None, a, b), lambda i: (i, 0, 0))
    y = pl.pallas_call(
        functools.partial(_s5_kernel, bsz=bsz, n_chunks=n_chunks, nz_chunks=n_ctx // chunk),
        grid=(g,),
        in_specs=[per_g(rows, k), per_g(k, k), per_g(k, nst2), per_g(k, nst2), per_g(nst2, k), per_g(nst2, k),
                  per_g(8, nst2)],
        out_specs=per_g(rows, k),
        out_shape=jax.ShapeDtypeStruct((g, rows, k), F32),
        scratch_shapes=[pltpu.VMEM((rows, nst2), F32)] * 4,
        compiler_params=_cparams(("parallel",)),
        name="s5_scan",
    )(ug, m, wsf, wsb, wrf, wrb, lam_rows)
    return y.reshape(g, n_chunks, bsz, chunk, ch).transpose(2, 1, 3, 0, 4).reshape(bsz, s, width)


def _attn_kernel(sink_ref, q_ref, k_ref, v_ref, o_ref, *, n_ctx, t_len, q_off, kv_heads):
    qi = pl.program_id(1) + q_off
    nzb = n_ctx // ATT_BLOCK
    band = 3 * ATT_BLOCK
    q = q_ref[...]
    kc = k_ref[0:n_ctx, :].astype(BF16)
    vc = v_ref[0:n_ctx, :].astype(BF16)

    def head_cols(a, i):
        return a[:, i * HEAD_DIM:(i + 1) * HEAD_DIM]

    def put(h, o):
        o_ref[:, h * HEAD_DIM:(h + 1) * HEAD_DIM] = o

    @pl.when(qi < nzb)
    def _():
        for h in range(kv_heads * ATT_REP):
            g = h // ATT_REP
            sink = sink_ref[h]
            s = _dot_nt(head_cols(q, h).astype(BF16), head_cols(kc, g))
            m = jnp.maximum(jnp.max(s, axis=1, keepdims=True), sink)
            e = jnp.exp(s - m)
            den = jnp.sum(e, axis=1, keepdims=True) + jnp.exp(sink - m)
            put(h, _dot(e.astype(BF16), head_cols(vc, g)) / den)

    @pl.when(qi >= nzb)
    def _():
        n = qi - nzb
        start = jnp.clip((n - 1) * ATT_BLOCK, 0, t_len - band)
        kl = k_ref[pl.ds(pl.multiple_of(n_ctx + start, ATT_BLOCK), band), :].astype(BF16)
        vl = v_ref[pl.ds(pl.multiple_of(n_ctx + start, ATT_BLOCK), band), :].astype(BF16)
        q_pos = n * ATT_BLOCK + lax.broadcasted_iota(jnp.int32, (ATT_BLOCK, band), 0)
        k_pos = start + lax.broadcasted_iota(jnp.int32, (ATT_BLOCK, band), 1)
        valid = jnp.abs(k_pos - q_pos) <= WINDOW
        for h in range(kv_heads * ATT_REP):
            g = h // ATT_REP
            sink = sink_ref[h]
            qh = head_cols(q, h).astype(BF16)
            s_loc = jnp.where(valid, _dot_nt(qh, head_cols(kl, g)), NEG_INF)
            s_ctx = _dot_nt(qh, head_cols(kc, g))
            m = jnp.maximum(jnp.maximum(jnp.max(s_loc, axis=1, keepdims=True),
                                        jnp.max(s_ctx, axis=1, keepdims=True)), sink)
            e_loc = jnp.exp(s_loc - m)
            e_ctx = jnp.exp(s_ctx - m)
            den = (jnp.sum(e_loc, axis=1, keepdims=True) + jnp.sum(e_ctx, axis=1, keepdims=True)
                   + jnp.exp(sink - m))
            o = _dot(e_loc.astype(BF16), head_cols(vl, g)) + _dot(e_ctx.astype(BF16), head_cols(vc, g))
            put(h, o / den)


def _attention(qa, ka, va, sink, n_ctx, need_ctx):
    bsz, s, qw = qa.shape
    kvw = ka.shape[2]
    t_len = s - n_ctx
    q_off = 0 if need_ctx else n_ctx // ATT_BLOCK
    nq = s // ATT_BLOCK - q_off
    return pl.pallas_call(
        functools.partial(_attn_kernel, n_ctx=n_ctx, t_len=t_len, q_off=q_off, kv_heads=kvw // HEAD_DIM),
        grid=(bsz, nq),
        in_specs=[pl.BlockSpec(memory_space=pltpu.SMEM),
                  pl.BlockSpec((None, ATT_BLOCK, qw), lambda b, j: (b, j + q_off, 0)),
                  pl.BlockSpec((None, s, kvw), lambda b, j: (b, 0, 0)),
                  pl.BlockSpec((None, s, kvw), lambda b, j: (b, 0, 0))],
        out_specs=pl.BlockSpec((None, ATT_BLOCK, qw), lambda b, j: (b, j, 0)),
        out_shape=jax.ShapeDtypeStruct((bsz, nq * ATT_BLOCK, qw), F32),
        compiler_params=_cparams(("parallel", "arbitrary")),
        name="window_attention",
    )(sink.astype(F32), qa, ka, va)


def _ret_kernel(lg_ref, q_ref, k_ref, v_ref, g_ref, o_ref, acc_ref, dec_ref, sf_ref, sb_ref,
                *, n_chunks, nz_chunks, heads):
    c = RET_CHUNK
    ii = lax.broadcasted_iota(jnp.int32, (c, c), 0)
    jj = lax.broadcasted_iota(jnp.int32, (c, c), 1)
    diff = (ii - jj).astype(F32)
    idx = lax.broadcasted_iota(jnp.int32, (c, 1), 0).astype(F32)
    for h in range(heads):
        lgf, lgb = lg_ref[0, h], lg_ref[1, h]
        dec_ref[h] = jnp.where(diff >= 0, jnp.exp(lgf * jnp.maximum(diff, 0.0)),
                               jnp.exp(lgb * jnp.maximum(-diff, 0.0)))
    sf_ref[...] = jnp.zeros_like(sf_ref)
    sb_ref[...] = jnp.zeros_like(sb_ref)

    def hcols(a, h):
        return a[:, h * HEAD_DIM:(h + 1) * HEAD_DIM]

    chunk_len = jnp.full((1, HEAD_DIM), float(c), F32)

    def fwd(n, carry):
        rows = pl.ds(pl.multiple_of(n * c, c), c)
        q, k, v = q_ref[rows, :], k_ref[rows, :], v_ref[rows, :]
        for h in range(heads):
            lgf = lg_ref[0, h]
            cols = slice(h * HEAD_DIM, (h + 1) * HEAD_DIM)
            qh, kh, vh = hcols(q, h), hcols(k, h), hcols(v, h).astype(BF16)
            scores = _dot_nt(qh.astype(BF16), kh.astype(BF16)) * dec_ref[h]
            s_prev = sf_ref[h]
            o = _dot(scores.astype(BF16), vh)
            o = o + _dot((qh * jnp.exp(lgf * (idx + 1.0))).astype(BF16), s_prev.astype(BF16))
            kw = kh * jnp.exp(lgf * (c - 1.0 - idx))
            sf_ref[h] = jnp.exp(lgf * chunk_len) * s_prev + _dot_tn(kw.astype(BF16), vh)
            acc_ref[rows, cols] = o
        return carry

    lax.fori_loop(0, n_chunks, fwd, 0)

    def bwd(i, carry):
        n = jnp.where(i < nz_chunks, nz_chunks - 1 - i, n_chunks - 1 - i + nz_chunks)
        rows = pl.ds(pl.multiple_of(n * c, c), c)
        q, k, v = q_ref[rows, :], k_ref[rows, :], v_ref[rows, :]
        acc = acc_ref[rows, :]
        gate = _silu(g_ref[rows, :])
        for h in range(heads):
            lgb = lg_ref[1, h]
            cols = slice(h * HEAD_DIM, (h + 1) * HEAD_DIM)
            qh, kh, vh = hcols(q, h), hcols(k, h), hcols(v, h).astype(BF16)
            s_prev = sb_ref[h]
            o = hcols(acc, h) + _dot((qh * jnp.exp(lgb * (c - idx))).astype(BF16), s_prev.astype(BF16))
            kw = kh * jnp.exp(lgb * idx)
            sb_ref[h] = jnp.exp(lgb * chunk_len) * s_prev + _dot_tn(kw.astype(BF16), vh)
            o_ref[rows, cols] = _ln(o) * hcols(gate, h)
        return carry

    lax.fori_loop(0, n_chunks, bwd, 0)


def _retention(qr, kr, vr, gr, log_gamma, n_ctx):
    bsz, s, w = qr.shape
    heads = w // HEAD_DIM
    n_chunks = s // RET_CHUNK
    tok = pl.BlockSpec((None, s, w), lambda b: (b, 0, 0))
    return pl.pallas_call(
        functools.partial(_ret_kernel, n_chunks=n_chunks, nz_chunks=n_ctx // RET_CHUNK, heads=heads),
        grid=(bsz,),
        in_specs=[pl.BlockSpec(memory_space=pltpu.SMEM), tok, tok, tok, tok],
        out_specs=tok,
        out_shape=jax.ShapeDtypeStruct((bsz, s, w), F32),
        scratch_shapes=[pltpu.VMEM((s, w), F32),
                        pltpu.VMEM((heads, RET_CHUNK, RET_CHUNK), F32),
                        pltpu.VMEM((heads, HEAD_DIM, HEAD_DIM), F32),
                        pltpu.VMEM((heads, HEAD_DIM, HEAD_DIM), F32)],
        compiler_params=_cparams(("parallel",)),
        name="retention",
    )(log_gamma.astype(F32), qr, kr, vr, gr)


def _outproj_kernel(*refs, nzb, alpha, with_router, w_s5, w_att, n_exp):
    if with_router:
        (y_ref, u_ref, a_ref, r_ref, x_ref, modx_ref, modz_ref, d_ref, wg_ref, bg_ref, wo_ref,
         g1_ref, b1_ref, rt_ref, x1_ref, fx_ref, gate_ref) = refs
    else:
        (y_ref, u_ref, a_ref, r_ref, x_ref, modx_ref, modz_ref, d_ref, wg_ref, bg_ref, wo_ref,
         g1_ref, b1_ref, x1_ref, fx_ref) = refs
    is_ctx = pl.program_id(1) < nzb
    mod = jnp.where(is_ctx, modz_ref[...], modx_ref[...])
    g = _gelu_tanh(y_ref[...] + d_ref[...] * u_ref[...])
    gb = g.astype(BF16)
    s5 = g * _sigmoid(_dot(gb, wg_ref[...]) + bg_ref[...])
    mix = (_dot(s5.astype(BF16), wo_ref[0:w_s5, :])
           + _dot(a_ref[...].astype(BF16), wo_ref[w_s5:w_s5 + w_att, :])
           + _dot(r_ref[...].astype(BF16), wo_ref[w_s5 + w_att:, :]))
    x1 = _ln(alpha * x_ref[...] + mod[2:3] * mix) * g1_ref[...] + b1_ref[...]
    x1_ref[...] = x1
    fx = _ln(x1) * (1.0 + mod[4:5]) + mod[3:4]
    fx_ref[...] = fx.astype(fx_ref.dtype)
    if with_router:
        lane = lax.broadcasted_iota(jnp.int32, (fx.shape[0], LANES), 1)
        logits = jnp.where(lane < n_exp, _dot3(fx, rt_ref[...]), -jnp.inf)
        m1 = jnp.max(logits, axis=1, keepdims=True)
        i1 = jnp.min(jnp.where(logits == m1, lane, LANES), axis=1, keepdims=True)
        rest = jnp.where(lane == i1, -jnp.inf, logits)
        m2 = jnp.max(rest, axis=1, keepdims=True)
        i2 = jnp.min(jnp.where(rest == m2, lane, LANES), axis=1, keepdims=True)
        e2 = jnp.exp(m2 - m1)
        den = 1.0 + e2
        route = jnp.where(lane == 0, 1.0 / den, jnp.where(lane == 1, e2 / den, 0.0))
        route = jnp.where(lane == 2, i1.astype(F32), jnp.where(lane == 3, i2.astype(F32), route))
        gate_ref[...] = route


def _outproj(y_s5, u, o_att, o_ret, xz, mod, s5_d, w_glu, b_glu, w_out, ln_g, ln_b, router, n_ctx, need_ctx,
             alpha):
    bsz, s, d = xz.shape
    tm = TOKEN_BLOCK
    nzb = n_ctx // tm
    off = 0 if need_ctx else nzb
    nblk = s // tm - off
    s_out = nblk * tm
    nrow = mod.shape[0]
    w_s5, w_att = y_s5.shape[2], o_att.shape[2]
    att_off = off if o_att.shape[1] == s else 0

    def tok(width, shift):
        return pl.BlockSpec((None, tm, width), lambda b, j: (b, j + shift, 0))

    def full(a):
        return pl.BlockSpec(a.shape, lambda b, j: (0,) * a.ndim)

    vec = lambda a: a.reshape(1, -1).astype(F32)
    consts = [vec(s5_d), w_glu.astype(BF16), vec(b_glu), w_out.astype(BF16), vec(ln_g), vec(ln_b)]
    in_specs = [tok(w_s5, off), tok(w_s5, off), tok(w_att, att_off), tok(o_ret.shape[2], off), tok(d, off),
                pl.BlockSpec((None, 6, d), lambda b, j: (b, 0, 0)),
                pl.BlockSpec((None, 6, d), lambda b, j: (nrow - 1, 0, 0))] + [full(a) for a in consts]
    out_specs = [tok(d, 0), tok(d, 0)]
    fx_dtype = BF16 if router is None else F32
    out_shape = [jax.ShapeDtypeStruct((bsz, s_out, d), F32), jax.ShapeDtypeStruct((bsz, s_out, d), fx_dtype)]
    args = [y_s5, u, o_att, o_ret, xz, mod, mod] + consts
    with_router = router is not None
    n_exp = 0
    if with_router:
        n_exp = router.shape[1]
        router_pad = jnp.pad(router.astype(F32), ((0, 0), (0, LANES - n_exp)))
        args.append(router_pad)
        in_specs.append(full(router_pad))
        out_specs.append(tok(LANES, 0))
        out_shape.append(jax.ShapeDtypeStruct((bsz, s_out, LANES), F32))
    return pl.pallas_call(
        functools.partial(_outproj_kernel, nzb=nzb - off, alpha=alpha, with_router=with_router,
                          w_s5=w_s5, w_att=w_att, n_exp=n_exp),
        grid=(bsz, nblk),
        in_specs=in_specs,
        out_specs=out_specs,
        out_shape=out_shape,
        compiler_params=_cparams(("parallel", "parallel")),
        name="outproj",
    )(*args)


def _swiglu_into(x_bf16, w1_ref, w3_ref, w2_ref, acc_ref, fc):
    for s in range(0, w1_ref.shape[1], fc):
        h1 = _dot(x_bf16, w1_ref[:, s:s + fc])
        h3 = _dot(x_bf16, w3_ref[:, s:s + fc])
        acc_ref[...] += _dot((_silu(h1) * h3).astype(BF16), w2_ref[s:s + fc, :])


def _ffn_kernel(fx_ref, x1_ref, modx_ref, modz_ref, w1_ref, w3_ref, w2_ref, g2_ref, b2_ref,
                o_ref, acc_ref, *, nzb, alpha, fc):
    acc_ref[...] = jnp.zeros_like(acc_ref)
    _swiglu_into(fx_ref[...], w1_ref, w3_ref, w2_ref, acc_ref, fc)
    is_ctx = pl.program_id(1) < nzb
    mod = jnp.where(is_ctx, modz_ref[...], modx_ref[...])
    o_ref[...] = _ln(alpha * x1_ref[...] + mod[5:6] * acc_ref[...]) * g2_ref[...] + b2_ref[...]


def _ffn(fx, x1, mod, w1, w3, w2, ln_g, ln_b, n_ctx_tokens, alpha):
    bsz, s, d = x1.shape
    ff = w1.shape[1]
    tm = TOKEN_BLOCK
    nrow = mod.shape[0]
    tok = pl.BlockSpec((None, tm, d), lambda b, j: (b, j, 0))
    full = lambda a: pl.BlockSpec(a.shape, lambda b, j: (0,) * a.ndim)
    vec = lambda a: a.reshape(1, -1).astype(F32)
    return pl.pallas_call(
        functools.partial(_ffn_kernel, nzb=n_ctx_tokens // tm, alpha=alpha, fc=_pick_chunk(ff, 512)),
        grid=(bsz, s // tm),
        in_specs=[tok, tok,
                  pl.BlockSpec((None, 6, d), lambda b, j: (b, 0, 0)),
                  pl.BlockSpec((None, 6, d), lambda b, j: (nrow - 1, 0, 0)),
                  full(w1), full(w3), full(w2),
                  pl.BlockSpec((1, d), lambda b, j: (0, 0)), pl.BlockSpec((1, d), lambda b, j: (0, 0))],
        out_specs=tok,
        out_shape=jax.ShapeDtypeStruct((bsz, s, d), F32),
        scratch_shapes=[pltpu.VMEM((tm, d), F32)],
        compiler_params=_cparams(("parallel", "parallel")),
        name="dense_ffn",
    )(fx, x1, mod, mod, w1, w3, w2, vec(ln_g), vec(ln_b))


MOE_TILE = 1024


def _route_plan(e1, e2, n_exp, tile):
    n = e1.shape[0]
    pair_e = jnp.stack([e1, e2], axis=1).reshape(-1)
    onehot = (pair_e[:, None] == jnp.arange(n_exp, dtype=jnp.int32)[None, :]).astype(jnp.int32)
    before = jnp.cumsum(onehot, axis=0) - onehot
    rank = jnp.sum(before * onehot, axis=1)
    counts = jnp.sum(onehot, axis=0)
    padded = (counts + tile - 1) // tile * tile
    ends = jnp.cumsum(padded)
    starts = ends - padded
    dest = starts[pair_e] + rank
    n_rows = (2 * n + n_exp * (tile - 1)) // tile * tile
    n_tiles = n_rows // tile
    row_token = jnp.zeros((n_rows,), jnp.int32).at[dest].set(jnp.arange(2 * n, dtype=jnp.int32) // 2)
    tile_start = jnp.arange(n_tiles, dtype=jnp.int32) * tile
    tile_expert = jnp.minimum(jnp.sum((tile_start[:, None] >= ends[None, :]).astype(jnp.int32), axis=1),
                              n_exp - 1)
    n_used = (ends[-1] // tile).astype(jnp.int32).reshape(1)
    return row_token, dest.reshape(n, 2), tile_expert, n_used


def _moe_gather_kernel(tok_ref, x_hbm, o_ref, sem, *, tile):
    def row_copy(r):
        return pltpu.make_async_copy(x_hbm.at[pl.ds(tok_ref[0, r], 1)], o_ref.at[pl.ds(r, 1)], sem)

    def issue(r, carry):
        row_copy(r).start()
        return carry

    def drain(r, carry):
        row_copy(r).wait()
        return carry

    lax.fori_loop(0, tile, issue, 0)
    lax.fori_loop(0, tile, drain, 0)


def _moe_gather(x_flat, row_token, tile):
    n_rows = row_token.shape[0]
    n_tiles = n_rows // tile
    d = x_flat.shape[1]
    return pl.pallas_call(
        functools.partial(_moe_gather_kernel, tile=tile),
        grid=(n_tiles,),
        in_specs=[pl.BlockSpec((None, 1, tile), lambda i: (i, 0, 0), memory_space=pltpu.SMEM),
                  pl.BlockSpec(memory_space=pl.ANY)],
        out_specs=pl.BlockSpec((tile, d), lambda i: (i, 0)),
        out_shape=jax.ShapeDtypeStruct((n_rows, d), x_flat.dtype),
        scratch_shapes=[pltpu.SemaphoreType.DMA(())],
        compiler_params=_cparams(("arbitrary",)),
        name="moe_gather",
    )(row_token.reshape(n_tiles, 1, tile), x_flat)


def _moe_gemm_kernel(te_ref, nused_ref, x_ref, w1_ref, w3_ref, w2_ref, y_ref, *, fc):
    @pl.when(pl.program_id(1) == 0)
    def _():
        y_ref[...] = jnp.zeros_like(y_ref)

    @pl.when(pl.program_id(0) < nused_ref[0])
    def _():
        _swiglu_into(x_ref[...].astype(BF16), w1_ref, w3_ref, w2_ref, y_ref, fc)


def _moe_gemm(xs, tile_expert, n_used, w1, w3, w2, tile):
    n_rows, d = xs.shape
    ff = w1.shape[2]
    nf = 2 if ff % (2 * LANES) == 0 else 1
    tf = ff // nf
    last = lambda i, nu: jnp.minimum(i, nu[0] - 1)
    fsel = lambda i, f, nu: jnp.where(i < nu[0], f, nf - 1)
    return pl.pallas_call(
        functools.partial(_moe_gemm_kernel, fc=_pick_chunk(tf, 256)),
        grid_spec=pltpu.PrefetchScalarGridSpec(
            num_scalar_prefetch=2,
            grid=(n_rows // tile, nf),
            in_specs=[pl.BlockSpec((tile, d), lambda i, f, te, nu: (last(i, nu), 0)),
                      pl.BlockSpec((None, d, tf), lambda i, f, te, nu: (te[last(i, nu)], 0, fsel(i, f, nu))),
                      pl.BlockSpec((None, d, tf), lambda i, f, te, nu: (te[last(i, nu)], 0, fsel(i, f, nu))),
                      pl.BlockSpec((None, tf, d), lambda i, f, te, nu: (te[last(i, nu)], fsel(i, f, nu), 0))],
            out_specs=pl.BlockSpec((tile, d), lambda i, f, te, nu: (i, 0))),
        out_shape=jax.ShapeDtypeStruct((n_rows, d), F32),
        compiler_params=_cparams(("arbitrary", "arbitrary")),
        name="moe_gemm",
    )(tile_expert, n_used, xs, w1, w3, w2)


def _moe_combine_kernel(pos_ref, posn_ref, route_ref, x1_ref, modx_ref, modz_ref, g2_ref, b2_ref, y_hbm,
                        o_ref, buf_ref, sem, *, nzb, nblk, alpha, tm):
    i = pl.program_id(0)
    slot = i % 2

    def row_copy(p_ref, r, k, s):
        return pltpu.make_async_copy(y_hbm.at[pl.ds(p_ref[0, k * tm + r], 1)],
                                     buf_ref.at[s, k, pl.ds(r, 1)], sem.at[s])

    def issue(p_ref, s):
        def body(r, carry):
            row_copy(p_ref, r, 0, s).start()
            row_copy(p_ref, r, 1, s).start()
            return carry
        lax.fori_loop(0, tm, body, 0)

    @pl.when(i == 0)
    def _():
        issue(pos_ref, 0)

    @pl.when(i + 1 < pl.num_programs(0))
    def _():
        issue(posn_ref, 1 - slot)

    def drain(r, carry):
        row_copy(pos_ref, r, 0, slot).wait()
        row_copy(pos_ref, r, 1, slot).wait()
        return carry

    lax.fori_loop(0, tm, drain, 0)
    route = route_ref[...]
    f = route[:, 0:1] * buf_ref[slot, 0] + route[:, 1:2] * buf_ref[slot, 1]
    is_ctx = (i % nblk) < nzb
    mod = jnp.where(is_ctx, modz_ref[...], modx_ref[...])
    o_ref[...] = _ln(alpha * x1_ref[...] + mod[5:6] * f) * g2_ref[...] + b2_ref[...]


def _moe_combine(ys, pos, route, x1, mod, ln_g, ln_b, n_ctx_tokens, alpha):
    bsz, s, d = x1.shape
    tm = TOKEN_BLOCK
    nblk = s // tm
    n_steps = bsz * nblk
    nrow = mod.shape[0]
    pos_steps = pos.reshape(n_steps, tm, 2).transpose(0, 2, 1).reshape(n_steps, 1, 2 * tm)
    tok = lambda width: pl.BlockSpec((tm, width), lambda i: (i, 0))
    smem = lambda imap: pl.BlockSpec((None, 1, 2 * tm), imap, memory_space=pltpu.SMEM)
    vec = lambda a: a.reshape(1, -1).astype(F32)
    out = pl.pallas_call(
        functools.partial(_moe_combine_kernel, nzb=n_ctx_tokens // tm, nblk=nblk, alpha=alpha, tm=tm),
        grid=(n_steps,),
        in_specs=[smem(lambda i: (i, 0, 0)),
                  smem(lambda i: (jnp.minimum(i + 1, n_steps - 1), 0, 0)),
                  tok(LANES), tok(d),
                  pl.BlockSpec((None, 6, d), lambda i: (i // nblk, 0, 0)),
                  pl.BlockSpec((None, 6, d), lambda i: (nrow - 1, 0, 0)),
                  pl.BlockSpec((1, d), lambda i: (0, 0)), pl.BlockSpec((1, d), lambda i: (0, 0)),
                  pl.BlockSpec(memory_space=pl.ANY)],
        out_specs=tok(d),
        out_shape=jax.ShapeDtypeStruct((bsz * s, d), F32),
        scratch_shapes=[pltpu.VMEM((2, 2, tm, d), F32), pltpu.SemaphoreType.DMA((2,))],
        compiler_params=_cparams(("arbitrary",)),
        name="moe_combine",
    )(pos_steps, pos_steps, route.reshape(bsz * s, LANES), x1.reshape(bsz * s, d), mod, mod,
      vec(ln_g), vec(ln_b), ys)
    return out.reshape(bsz, s, d)


def _moe(fx, x1, route, mod, w1, w3, w2, ln_g, ln_b, n_ctx_tokens, alpha):
    bsz, s, d = x1.shape
    n_exp = w1.shape[0]
    idx = route.reshape(bsz * s, LANES)[:, 2:4].astype(jnp.int32)
    row_token, pos, tile_expert, n_used = _route_plan(idx[:, 0], idx[:, 1], n_exp, MOE_TILE)
    xs = _moe_gather(fx.reshape(bsz * s, d), row_token, MOE_TILE)
    ys = _moe_gemm(xs, tile_expert, n_used, w1, w3, w2, MOE_TILE)
    return _moe_combine(ys, pos, route, x1, mod, ln_g, ln_b, n_ctx_tokens, alpha)


def _pick_chunk(total, target):
    best = LANES
    for c in range(LANES, target + 1, LANES):
        if total % c == 0:
            best = c
    return best


def kernel(x, c, ctx, c_ctx, w_mod, b_mod, w_in, s5_lam_re, s5_lam_im, s5_log_step, s5_b_re, s5_b_im,
           s5_c_re, s5_c_im, s5_d, s5_w_glu, s5_b_glu, attn_sink, ret_log_gamma, w_out,
           ln1_g, ln1_b, ln2_g, ln2_b, ffn_w1, ffn_w3, ffn_w2, moe_router, moe_w1, moe_w3, moe_w2):
    bsz, t_len, d = x.shape
    n_ctx = ctx.shape[1]
    depth = w_in.shape[0]
    alpha = (2 * depth) ** 0.25
    s5_w = s5_d.shape[1]
    att_w = attn_sink.shape[1] * HEAD_DIM
    kv_w = att_w // ATT_REP
    ret_w = ret_log_gamma.shape[2] * HEAD_DIM
    sizes = (s5_w, att_w, kv_w, kv_w, ret_w, ret_w, ret_w, ret_w)
    assert sum(sizes) == w_in.shape[2] and s5_w + att_w + ret_w == w_out.shape[1]
    assert n_ctx % TOKEN_BLOCK == 0 and t_len % TOKEN_BLOCK == 0 and t_len >= 3 * ATT_BLOCK

    pad = (-(bsz + 1)) % 8
    cvec = jnp.concatenate([jnp.zeros((pad, d), F32), c_ctx[None].astype(F32)], axis=0)
    cvec = jnp.concatenate([c.astype(F32), cvec], axis=0)
    mod_all = _modulation(cvec, w_mod.astype(F32), b_mod.astype(F32)).reshape(depth, bsz + pad + 1, 6, d)

    tabs = _rope_tables(t_len)
    xz = jnp.concatenate([ctx, x], axis=1).astype(F32)
    for l in range(depth):
        need_ctx = l < depth - 1
        mod = mod_all[l]
        u, qa, ka, va, qr, kr, vr, gr = _inproj(xz, mod, w_in[l].astype(BF16), tabs, sizes, n_ctx)
        s5w = _s5_weights(s5_lam_re[l], s5_lam_im[l], s5_log_step[l], s5_b_re[l], s5_b_im[l],
                          s5_c_re[l], s5_c_im[l], S5_CHUNK)
        y_s5 = _s5_scan(u, s5w, n_ctx)
        o_att = _attention(qa, ka, va, attn_sink[l], n_ctx, need_ctx)
        o_ret = _retention(qr, kr, vr, gr, ret_log_gamma[l], n_ctx)
        i = l // 2
        router = None if l % 2 == 0 else moe_router[i]
        outs = _outproj(y_s5, u, o_att, o_ret, xz, mod, s5_d[l], s5_w_glu[l], s5_b_glu[l], w_out[l],
                        ln1_g[l], ln1_b[l], router, n_ctx, need_ctx, alpha)
        ctx_tokens = n_ctx if need_ctx else 0
        if l % 2 == 0:
            x1, fx = outs
            xz = _ffn(fx, x1, mod, ffn_w1[i].astype(BF16), ffn_w3[i].astype(BF16), ffn_w2[i].astype(BF16),
                      ln2_g[l], ln2_b[l], ctx_tokens, alpha)
        else:
            x1, fx, route = outs
            xz = _moe(fx, x1, route, mod, moe_w1[i].astype(BF16), moe_w3[i].astype(BF16),
                      moe_w2[i].astype(BF16), ln2_g[l], ln2_b[l], ctx_tokens, alpha)
    return xz if xz.shape[1] == t_len else xz[:, n_ctx:]
```

```python
import functools
import math

import jax
import jax.numpy as jnp
import numpy as np
from jax import lax
from jax.experimental import pallas as pl
from jax.experimental.pallas import tpu as pltpu

F32 = jnp.float32
BF16 = jnp.bfloat16

GRID_W = 64
HEAD_DIM = 64
S5_GROUP = 16
ATT_REP = 4
WINDOW = 128
ATT_BLOCK = 128
RET_CHUNK = 128
TOP_K = 2
LN_EPS = 1e-5
ROPE_BASE = 10000.0
NEG_INF = -1e30

LANES = 128
S5_CHUNK = LANES
TOKEN_BLOCK = 256
VMEM_LIMIT = 56 * 1024 * 1024


def _cparams(sem):
    return pltpu.CompilerParams(dimension_semantics=sem, vmem_limit_bytes=VMEM_LIMIT)


def _dot(a, b):
    return jnp.dot(a, b, preferred_element_type=F32)


def _dot_nt(a, b):
    return lax.dot_general(a, b, (((1,), (1,)), ((), ())), preferred_element_type=F32)


def _dot_tn(a, b):
    return lax.dot_general(a, b, (((0,), (0,)), ((), ())), preferred_element_type=F32)


def _split_bf16(a):
    hi = a.astype(BF16)
    lo = (a - hi.astype(F32)).astype(BF16)
    return hi, lo


def _dot3(a, b):
    ah, al = _split_bf16(a)
    bh, bl = _split_bf16(b)
    return _dot(ah, bh) + (_dot(ah, bl) + _dot(al, bh))


def _ln(x):
    mu = jnp.mean(x, axis=-1, keepdims=True)
    xc = x - mu
    var = jnp.mean(xc * xc, axis=-1, keepdims=True)
    return xc * lax.rsqrt(var + LN_EPS)


def _silu(x):
    return x * (1.0 / (1.0 + jnp.exp(-x)))


def _sigmoid(x):
    return 1.0 / (1.0 + jnp.exp(-x))


def _gelu_tanh(x):
    c = math.sqrt(2.0 / math.pi)
    return 0.5 * x * (1.0 + jnp.tanh(c * (x + 0.044715 * (x * x * x))))


def _mod_kernel(c_ref, w_ref, b_ref, o_ref):
    o_ref[...] = _dot3(_silu(c_ref[...]), w_ref[...]) + b_ref[...]


def _modulation(cvec, w_mod, b_mod):
    depth, d, n = w_mod.shape
    rows = cvec.shape[0]
    tn = 1536
    return pl.pallas_call(
        _mod_kernel,
        grid=(depth, n // tn),
        in_specs=[pl.BlockSpec((rows, d), lambda l, j: (0, 0)),
                  pl.BlockSpec((None, d, tn), lambda l, j: (l, 0, j)),
                  pl.BlockSpec((None, 1, tn), lambda l, j: (l, 0, j))],
        out_specs=pl.BlockSpec((None, rows, tn), lambda l, j: (l, 0, j)),
        out_shape=jax.ShapeDtypeStruct((depth, rows, n), F32),
        compiler_params=_cparams(("parallel", "parallel")),
        name="modulation",
    )(cvec, w_mod, b_mod.reshape(depth, 1, n))


def _rope_slab(xs, cos, sa, sb, half):
    return xs * cos + pltpu.roll(xs, LANES - half, 1) * sa + pltpu.roll(xs, half, 1) * sb


def _inproj_kernel(x_ref, modx_ref, modz_ref, wu_ref, w_ref, ac_ref, asa_ref, asb_ref, rc_ref, rsa_ref, rsb_ref,
                   ut_ref, qa_ref, ka_ref, va_ref, qr_ref, kr_ref, vr_ref, gr_ref, *, nzb, sizes):
    is_ctx = pl.program_id(1) < nzb
    mod = jnp.where(is_ctx, modz_ref[...], modx_ref[...])
    h = (_ln(x_ref[...]) * (1.0 + mod[1:2]) + mod[0:1]).astype(BF16)
    ut_ref[...] = _dot_nt(wu_ref[...], h)
    p = _dot(h, w_ref[...])
    ac = jnp.where(is_ctx, 1.0, ac_ref[...])
    asa = jnp.where(is_ctx, 0.0, asa_ref[...])
    asb = jnp.where(is_ctx, 0.0, asb_ref[...])
    rc = jnp.where(is_ctx, 1.0, rc_ref[...])
    rsa = jnp.where(is_ctx, 0.0, rsa_ref[...])
    rsb = jnp.where(is_ctx, 0.0, rsb_ref[...])
    offs = np.concatenate([[0], np.cumsum(sizes)])
    scale = HEAD_DIM ** -0.5

    def cols(i):
        return p[:, int(offs[i]):int(offs[i + 1])]

    def rope_cols(i, cos, sa, sb, half, mul):
        blk = cols(i)
        slabs = [_rope_slab(blk[:, s:s + LANES], cos, sa, sb, half) for s in range(0, blk.shape[1], LANES)]
        out = slabs[0] if len(slabs) == 1 else jnp.concatenate(slabs, axis=1)
        return out * mul if mul != 1.0 else out

    qa_ref[...] = rope_cols(0, ac, asa, asb, HEAD_DIM // 4, scale)
    ka_ref[...] = rope_cols(1, ac, asa, asb, HEAD_DIM // 4, 1.0)
    va_ref[...] = cols(2)
    qr_ref[...] = rope_cols(3, rc, rsa, rsb, HEAD_DIM // 2, 1.0)
    kr_ref[...] = rope_cols(4, rc, rsa, rsb, HEAD_DIM // 2, scale)
    vr_ref[...] = cols(5)
    gr_ref[...] = cols(6)


def _inproj(xz, mod, w_in, tabs, sizes, n_ctx):
    bsz, s, d = xz.shape
    tm = TOKEN_BLOCK
    nzb = n_ctx // tm
    nrow = mod.shape[0]
    s5_w, rest = sizes[0], sizes[1:]
    w_u_t = w_in[:, :s5_w].T
    w_rest = w_in[:, s5_w:]
    tok = lambda width: pl.BlockSpec((None, tm, width), lambda b, j: (b, j, 0))
    tab = pl.BlockSpec((tm, LANES), lambda b, j: (jnp.maximum(j - nzb, 0), 0))
    return pl.pallas_call(
        functools.partial(_inproj_kernel, nzb=nzb, sizes=rest),
        grid=(bsz, s // tm),
        in_specs=[tok(d),
                  pl.BlockSpec((None, 6, d), lambda b, j: (b, 0, 0)),
                  pl.BlockSpec((None, 6, d), lambda b, j: (nrow - 1, 0, 0)),
                  pl.BlockSpec((s5_w, d), lambda b, j: (0, 0)),
                  pl.BlockSpec((d, sum(rest)), lambda b, j: (0, 0)),
                  tab, tab, tab, tab, tab, tab],
        out_specs=[pl.BlockSpec((None, None, s5_w, tm), lambda b, j: (j, b, 0, 0))] + [tok(w) for w in rest],
        out_shape=[jax.ShapeDtypeStruct((s // tm, bsz, s5_w, tm), F32)]
                  + [jax.ShapeDtypeStruct((bsz, s, w), F32) for w in rest],
        compiler_params=_cparams(("parallel", "parallel")),
        name="inproj",
    )(xz, mod, mod, w_u_t, w_rest, *tabs)


def _rope_tables(t_len):
    t = jnp.arange(t_len)
    rows = (t // GRID_W).astype(F32)
    cols = (t % GRID_W).astype(F32)
    pos = t.astype(F32)

    def angles(p, dim):
        inv_freq = ROPE_BASE ** (-jnp.arange(0, dim, 2, dtype=F32) / dim)
        return p[:, None] * inv_freq[None, :]

    def head_tables(angs):
        cos = jnp.concatenate([jnp.concatenate([jnp.cos(a), jnp.cos(a)], -1) for a in angs], -1)
        sa = jnp.concatenate([jnp.concatenate([-jnp.sin(a), jnp.zeros_like(a)], -1) for a in angs], -1)
        sb = jnp.concatenate([jnp.concatenate([jnp.zeros_like(a), jnp.sin(a)], -1) for a in angs], -1)
        rep = LANES // HEAD_DIM
        return tuple(jnp.tile(x, (1, rep)) for x in (cos, sa, sb))

    att = head_tables([angles(rows, HEAD_DIM // 2), angles(cols, HEAD_DIM // 2)])
    ret = head_tables([angles(pos, HEAD_DIM)])
    return att + ret


def _s5_kernel(u_ref, tab_ref, wsf_ref, wsb_ref, wrf_ref, wrb_ref, lam_ref, y_ref,
               lhs_ref, m_ref, acc_ref, cf_ref, cb_ref, pf_ref, pb_ref, *, bsz, nz_chunks):
    r_blk, ch, tm = u_ref.shape
    c = S5_CHUNK
    kpb = tm // c
    n_chunks = (r_blk // bsz) * kpb

    for i in range(ch):
        x = u_ref[:, i, :].astype(BF16)
        p, half = divmod(i, 2)
        for k in range(kpb):
            lhs_ref[p, k * r_blk:(k + 1) * r_blk, half * c:(half + 1) * c] = x[:, k * c:(k + 1) * c]
    acc_ref[...] = jnp.zeros_like(acc_ref)
    cf_ref[...] = jnp.zeros_like(cf_ref)
    cb_ref[...] = jnp.zeros_like(cb_ref)

    def pair(p, carry):
        for ih in range(2):
            for o in range(ch):
                tab = tab_ref[2 * p + ih, o]
                for r2 in range(c // 16):
                    lo = c - 16 * r2
                    rows = jnp.concatenate([tab[:, lo:lo + c], tab[:, lo - 8:lo - 8 + c]], axis=0)
                    m_ref[ih * c + 16 * r2:ih * c + 16 * r2 + 16, o * c:(o + 1) * c] = rows.astype(BF16)
        lhs = lhs_ref[p]
        acc_ref[...] += _dot(lhs, m_ref[...])
        cf_ref[...] += _dot(lhs, wsf_ref[p])
        cb_ref[...] += _dot(lhs, wsb_ref[p])
        return carry

    lax.fori_loop(0, ch // 2, pair, 0)
    lam = lam_ref[...]
    nst = lam.shape[1] // 2

    def advance(s, a, bc, drive):
        return s * a + pltpu.roll(s, nst, 1) * bc + drive

    def sweep(order, c_ref, p_ref, a, bc):
        s = jnp.zeros((bsz, lam.shape[1]), F32)
        for n in order:
            j, k = divmod(n, kpb)
            rows = slice(k * r_blk + j * bsz, k * r_blk + (j + 1) * bsz)
            p_ref[rows, :] = s
            s = advance(s, a, bc, c_ref[rows, :])

    sweep(list(range(n_chunks)), cf_ref, pf_ref, lam[0:1], lam[1:2])
    order_b = list(range(nz_chunks - 1, -1, -1)) + list(range(n_chunks - 1, nz_chunks - 1, -1))
    sweep(order_b, cb_ref, pb_ref, lam[2:3], lam[3:4])
    acc_ref[...] += (_dot(pf_ref[...].astype(BF16), wrf_ref[...])
                     + _dot(pb_ref[...].astype(BF16), wrb_ref[...]))
    for o in range(ch):
        for k in range(kpb):
            y_ref[:, o, k * c:(k + 1) * c] = acc_ref[k * r_blk:(k + 1) * r_blk, o * c:(o + 1) * c]


def _s5_weights(lam_re, lam_im, log_step, b_re, b_im, c_re, c_im, chunk):
    hp = lax.Precision.HIGHEST
    lam = lax.complex(lam_re.astype(F32), lam_im.astype(F32))
    lam_dt = lam * jnp.exp(log_step.astype(F32))[..., None]
    lam_bar = jnp.exp(lam_dt)
    b_bar = lax.complex(b_re.astype(F32), b_im.astype(F32)) * ((lam_bar - 1.0) / lam)[..., None]
    c_mat = lax.complex(c_re.astype(F32), c_im.astype(F32))
    g, n = lam.shape[1], lam.shape[2]
    ch = b_bar.shape[-1]
    steps = jnp.arange(chunk + 1, dtype=F32)
    pw = jnp.exp(steps[None, :, None, None] * lam_dt[:, None])
    kern = jnp.einsum('zgon,zdgn,zgni->zgdoi', c_mat, pw[:, :chunk], b_bar, precision=hp).real
    zero_lag = kern[0, :, 0] + kern[1, :, 0]
    lag_table = jnp.concatenate([jnp.zeros_like(zero_lag)[:, None], kern[1, :, :0:-1], zero_lag[:, None],
                                 kern[0, :, 1:]], axis=1).transpose(0, 3, 2, 1)
    tab = jnp.stack([jnp.roll(lag_table, q, axis=-1) for q in range(8)], axis=3)

    def state_in(pw_s, b_dir):
        w = pw_s[:, :, :, None] * b_dir[None]
        w = jnp.concatenate([w.real, w.imag], axis=2)
        return w.transpose(1, 3, 0, 2).reshape(g, ch // 2, 2 * chunk, 2 * n)

    def state_out(pw_t, c_dir):
        w = c_dir[None] * pw_t[:, :, None, :]
        w = jnp.concatenate([w.real, -w.imag], axis=3)
        return w.transpose(1, 3, 2, 0).reshape(g, 2 * n, ch * chunk)

    wsf = state_in(pw[0, chunk - 1::-1][:chunk], b_bar[0])
    wsb = state_in(pw[1, :chunk], b_bar[1])
    wrf = state_out(pw[0, 1:chunk + 1], c_mat[0])
    wrb = state_out(pw[1, chunk:0:-1], c_mat[1])
    lam_c = pw[:, chunk]
    rows = []
    for z in range(2):
        rows.append(jnp.concatenate([lam_c[z].real, lam_c[z].real], -1))
        rows.append(jnp.concatenate([-lam_c[z].imag, lam_c[z].imag], -1))
    lam_rows = jnp.stack(rows + [jnp.zeros_like(rows[0])] * 4, axis=1)
    return tab, wsf.astype(BF16), wsb.astype(BF16), wrf.astype(BF16), wrb.astype(BF16), lam_rows


def _s5_scan(ut, weights, n_ctx):
    nblk, bsz, width, tm = ut.shape
    tab, wsf, wsb, wrf, wrb, lam_rows = weights
    g, ch = tab.shape[0], tab.shape[1]
    c = S5_CHUNK
    nst2 = wsf.shape[3]
    r_blk = nblk * bsz
    rows = r_blk * (tm // c)
    grp = lambda *shape: pl.BlockSpec((None,) + shape, lambda i: (i,) + (0,) * len(shape))
    tok = pl.BlockSpec((r_blk, ch, tm), lambda i: (0, i, 0))
    y = pl.pallas_call(
        functools.partial(_s5_kernel, bsz=bsz, nz_chunks=n_ctx // c),
        grid=(g,),
        in_specs=[tok, grp(ch, ch, 8, 2 * c), grp(ch // 2, 2 * c, nst2), grp(ch // 2, 2 * c, nst2),
                  grp(nst2, ch * c), grp(nst2, ch * c), grp(8, nst2)],
        out_specs=tok,
        out_shape=jax.ShapeDtypeStruct((r_blk, width, tm), F32),
        scratch_shapes=[pltpu.VMEM((ch // 2, rows, 2 * c), BF16),
                        pltpu.VMEM((2 * c, ch * c), BF16),
                        pltpu.VMEM((rows, ch * c), F32)] + [pltpu.VMEM((rows, nst2), F32)] * 4,
        compiler_params=_cparams(("parallel",)),
        name="s5_scan",
    )(ut.reshape(r_blk, width, tm), tab, wsf, wsb, wrf, wrb, lam_rows)
    return y.reshape(nblk, bsz, width, tm)


def _attn_kernel(sink_ref, q_ref, k_ref, v_ref, o_ref, *, n_ctx, t_len, q_off, kv_heads):
    qi = pl.program_id(1) + q_off
    nzb = n_ctx // ATT_BLOCK
    band = 3 * ATT_BLOCK
    q = q_ref[...]
    kc = k_ref[0:n_ctx, :].astype(BF16)
    vc = v_ref[0:n_ctx, :].astype(BF16)

    def head_cols(a, i):
        return a[:, i * HEAD_DIM:(i + 1) * HEAD_DIM]

    def put(h, o):
        o_ref[:, h * HEAD_DIM:(h + 1) * HEAD_DIM] = o

    @pl.when(qi < nzb)
    def _():
        for h in range(kv_heads * ATT_REP):
            g = h // ATT_REP
            sink = sink_ref[h]
            s = _dot_nt(head_cols(q, h).astype(BF16), head_cols(kc, g))
            m = jnp.maximum(jnp.max(s, axis=1, keepdims=True), sink)
            e = jnp.exp(s - m)
            den = jnp.sum(e, axis=1, keepdims=True) + jnp.exp(sink - m)
            put(h, _dot(e.astype(BF16), head_cols(vc, g)) / den)

    @pl.when(qi >= nzb)
    def _():
        n = qi - nzb
        start = jnp.clip((n - 1) * ATT_BLOCK, 0, t_len - band)
        kl = k_ref[pl.ds(pl.multiple_of(n_ctx + start, ATT_BLOCK), band), :].astype(BF16)
        vl = v_ref[pl.ds(pl.multiple_of(n_ctx + start, ATT_BLOCK), band), :].astype(BF16)
        q_pos = n * ATT_BLOCK + lax.broadcasted_iota(jnp.int32, (ATT_BLOCK, band), 0)
        k_pos = start + lax.broadcasted_iota(jnp.int32, (ATT_BLOCK, band), 1)
        valid = jnp.abs(k_pos - q_pos) <= WINDOW
        for h in range(kv_heads * ATT_REP):
            g = h // ATT_REP
            sink = sink_ref[h]
            qh = head_cols(q, h).astype(BF16)
            s_loc = jnp.where(valid, _dot_nt(qh, head_cols(kl, g)), NEG_INF)
            s_ctx = _dot_nt(qh, head_cols(kc, g))
            m = jnp.maximum(jnp.maximum(jnp.max(s_loc, axis=1, keepdims=True),
                                        jnp.max(s_ctx, axis=1, keepdims=True)), sink)
            e_loc = jnp.exp(s_loc - m)
            e_ctx = jnp.exp(s_ctx - m)
            den = (jnp.sum(e_loc, axis=1, keepdims=True) + jnp.sum(e_ctx, axis=1, keepdims=True)
                   + jnp.exp(sink - m))
            o = _dot(e_loc.astype(BF16), head_cols(vl, g)) + _dot(e_ctx.astype(BF16), head_cols(vc, g))
            put(h, o / den)


def _attention(qa, ka, va, sink, n_ctx, need_ctx):
    bsz, s, qw = qa.shape
    kvw = ka.shape[2]
    t_len = s - n_ctx
    q_off = 0 if need_ctx else n_ctx // ATT_BLOCK
    nq = s // ATT_BLOCK - q_off
    return pl.pallas_call(
        functools.partial(_attn_kernel, n_ctx=n_ctx, t_len=t_len, q_off=q_off, kv_heads=kvw // HEAD_DIM),
        grid=(bsz, nq),
        in_specs=[pl.BlockSpec(memory_space=pltpu.SMEM),
                  pl.BlockSpec((None, ATT_BLOCK, qw), lambda b, j: (b, j + q_off, 0)),
                  pl.BlockSpec((None, s, kvw), lambda b, j: (b, 0, 0)),
                  pl.BlockSpec((None, s, kvw), lambda b, j: (b, 0, 0))],
        out_specs=pl.BlockSpec((None, ATT_BLOCK, qw), lambda b, j: (b, j, 0)),
        out_shape=jax.ShapeDtypeStruct((bsz, nq * ATT_BLOCK, qw), F32),
        compiler_params=_cparams(("parallel", "arbitrary")),
        name="window_attention",
    )(sink.astype(F32), qa, ka, va)


def _ret_kernel(lg_ref, q_ref, k_ref, v_ref, g_ref, o_ref, acc_ref, dec_ref, sf_ref, sb_ref,
                *, n_chunks, nz_chunks, heads):
    c = RET_CHUNK
    ii = lax.broadcasted_iota(jnp.int32, (c, c), 0)
    jj = lax.broadcasted_iota(jnp.int32, (c, c), 1)
    diff = (ii - jj).astype(F32)
    idx = lax.broadcasted_iota(jnp.int32, (c, 1), 0).astype(F32)
    for h in range(heads):
        lgf, lgb = lg_ref[0, h], lg_ref[1, h]
        dec_ref[h] = jnp.where(diff >= 0, jnp.exp(lgf * jnp.maximum(diff, 0.0)),
                               jnp.exp(lgb * jnp.maximum(-diff, 0.0)))
    sf_ref[...] = jnp.zeros_like(sf_ref)
    sb_ref[...] = jnp.zeros_like(sb_ref)

    def hcols(a, h):
        return a[:, h * HEAD_DIM:(h + 1) * HEAD_DIM]

    chunk_len = jnp.full((1, HEAD_DIM), float(c), F32)

    def fwd(n, carry):
        rows = pl.ds(pl.multiple_of(n * c, c), c)
        q, k, v = q_ref[rows, :], k_ref[rows, :], v_ref[rows, :]
        for h in range(heads):
            lgf = lg_ref[0, h]
            cols = slice(h * HEAD_DIM, (h + 1) * HEAD_DIM)
            qh, kh, vh = hcols(q, h), hcols(k, h), hcols(v, h).astype(BF16)
            scores = _dot_nt(qh.astype(BF16), kh.astype(BF16)) * dec_ref[h]
            s_prev = sf_ref[h]
            o = _dot(scores.astype(BF16), vh)
            o = o + _dot((qh * jnp.exp(lgf * (idx + 1.0))).astype(BF16), s_prev.astype(BF16))
            kw = kh * jnp.exp(lgf * (c - 1.0 - idx))
            sf_ref[h] = jnp.exp(lgf * chunk_len) * s_prev + _dot_tn(kw.astype(BF16), vh)
            acc_ref[rows, cols] = o
        return carry

    lax.fori_loop(0, n_chunks, fwd, 0)

    def bwd(i, carry):
        n = jnp.where(i < nz_chunks, nz_chunks - 1 - i, n_chunks - 1 - i + nz_chunks)
        rows = pl.ds(pl.multiple_of(n * c, c), c)
        q, k, v = q_ref[rows, :], k_ref[rows, :], v_ref[rows, :]
        acc = acc_ref[rows, :]
        gate = _silu(g_ref[rows, :])
        for h in range(heads):
            lgb = lg_ref[1, h]
            cols = slice(h * HEAD_DIM, (h + 1) * HEAD_DIM)
            qh, kh, vh = hcols(q, h), hcols(k, h), hcols(v, h).astype(BF16)
            s_prev = sb_ref[h]
            o = hcols(acc, h) + _dot((qh * jnp.exp(lgb * (c - idx))).astype(BF16), s_prev.astype(BF16))
            kw = kh * jnp.exp(lgb * idx)
            sb_ref[h] = jnp.exp(lgb * chunk_len) * s_prev + _dot_tn(kw.astype(BF16), vh)
            o_ref[rows, cols] = _ln(o) * hcols(gate, h)
        return carry

    lax.fori_loop(0, n_chunks, bwd, 0)


def _retention(qr, kr, vr, gr, log_gamma, n_ctx):
    bsz, s, w = qr.shape
    heads = w // HEAD_DIM
    n_chunks = s // RET_CHUNK
    tok = pl.BlockSpec((None, s, w), lambda b: (b, 0, 0))
    return pl.pallas_call(
        functools.partial(_ret_kernel, n_chunks=n_chunks, nz_chunks=n_ctx // RET_CHUNK, heads=heads),
        grid=(bsz,),
        in_specs=[pl.BlockSpec(memory_space=pltpu.SMEM), tok, tok, tok, tok],
        out_specs=tok,
        out_shape=jax.ShapeDtypeStruct((bsz, s, w), F32),
        scratch_shapes=[pltpu.VMEM((s, w), F32),
                        pltpu.VMEM((heads, RET_CHUNK, RET_CHUNK), F32),
                        pltpu.VMEM((heads, HEAD_DIM, HEAD_DIM), F32),
                        pltpu.VMEM((heads, HEAD_DIM, HEAD_DIM), F32)],
        compiler_params=_cparams(("parallel",)),
        name="retention",
    )(log_gamma.astype(F32), qr, kr, vr, gr)


def _outproj_kernel(*refs, nzb, alpha, with_router, w_s5, w_att, n_exp):
    if with_router:
        (y_ref, u_ref, a_ref, r_ref, x_ref, modx_ref, modz_ref, d_ref, wg_ref, bg_ref, wo_ref,
         g1_ref, b1_ref, rt_ref, x1_ref, fx_ref, gate_ref) = refs
    else:
        (y_ref, u_ref, a_ref, r_ref, x_ref, modx_ref, modz_ref, d_ref, wg_ref, bg_ref, wo_ref,
         g1_ref, b1_ref, x1_ref, fx_ref) = refs
    is_ctx = pl.program_id(1) < nzb
    mod = jnp.where(is_ctx, modz_ref[...], modx_ref[...])
    g = _gelu_tanh(y_ref[...] + d_ref[...] * u_ref[...])
    s5 = g * _sigmoid(_dot(wg_ref[...], g.astype(BF16)) + bg_ref[...])
    mix = (_dot_tn(s5.astype(BF16), wo_ref[0:w_s5, :])
           + _dot(a_ref[...].astype(BF16), wo_ref[w_s5:w_s5 + w_att, :])
           + _dot(r_ref[...].astype(BF16), wo_ref[w_s5 + w_att:, :]))
    x1 = _ln(alpha * x_ref[...] + mod[2:3] * mix) * g1_ref[...] + b1_ref[...]
    x1_ref[...] = x1
    fx = _ln(x1) * (1.0 + mod[4:5]) + mod[3:4]
    fx_ref[...] = fx.astype(fx_ref.dtype)
    if with_router:
        lane = lax.broadcasted_iota(jnp.int32, (fx.shape[0], LANES), 1)
        logits = jnp.where(lane < n_exp, _dot3(fx, rt_ref[...]), -jnp.inf)
        m1 = jnp.max(logits, axis=1, keepdims=True)
        i1 = jnp.min(jnp.where(logits == m1, lane, LANES), axis=1, keepdims=True)
        rest = jnp.where(lane == i1, -jnp.inf, logits)
        m2 = jnp.max(rest, axis=1, keepdims=True)
        i2 = jnp.min(jnp.where(rest == m2, lane, LANES), axis=1, keepdims=True)
        e2 = jnp.exp(m2 - m1)
        den = 1.0 + e2
        route = jnp.where(lane == 0, 1.0 / den, jnp.where(lane == 1, e2 / den, 0.0))
        route = jnp.where(lane == 2, i1.astype(F32), jnp.where(lane == 3, i2.astype(F32), route))
        gate_ref[...] = route


def _outproj(y_s5, u, o_att, o_ret, xz, mod, s5_d, w_glu, b_glu, w_out, ln_g, ln_b, router, n_ctx, need_ctx,
             alpha):
    bsz, s, d = xz.shape
    tm = TOKEN_BLOCK
    nzb = n_ctx // tm
    off = 0 if need_ctx else nzb
    nblk = s // tm - off
    s_out = nblk * tm
    nrow = mod.shape[0]
    w_s5, w_att = y_s5.shape[2], o_att.shape[2]
    att_off = off if o_att.shape[1] == s else 0

    def tok(width, shift):
        return pl.BlockSpec((None, tm, width), lambda b, j: (b, j + shift, 0))

    def full(a):
        return pl.BlockSpec(a.shape, lambda b, j: (0,) * a.ndim)

    chan = pl.BlockSpec((None, None, w_s5, tm), lambda b, j: (j + off, b, 0, 0))
    vec = lambda a: a.reshape(1, -1).astype(F32)
    col = lambda a: a.reshape(-1, 1).astype(F32)
    consts = [col(s5_d), w_glu.T.astype(BF16), col(b_glu), w_out.astype(BF16), vec(ln_g), vec(ln_b)]
    in_specs = [chan, chan, tok(w_att, att_off), tok(o_ret.shape[2], off), tok(d, off),
                pl.BlockSpec((None, 6, d), lambda b, j: (b, 0, 0)),
                pl.BlockSpec((None, 6, d), lambda b, j: (nrow - 1, 0, 0))] + [full(a) for a in consts]
    out_specs = [tok(d, 0), tok(d, 0)]
    fx_dtype = BF16 if router is None else F32
    out_shape = [jax.ShapeDtypeStruct((bsz, s_out, d), F32), jax.ShapeDtypeStruct((bsz, s_out, d), fx_dtype)]
    args = [y_s5, u, o_att, o_ret, xz, mod, mod] + consts
    with_router = router is not None
    n_exp = 0
    if with_router:
        n_exp = router.shape[1]
        router_pad = jnp.pad(router.astype(F32), ((0, 0), (0, LANES - n_exp)))
        args.append(router_pad)
        in_specs.append(full(router_pad))
        out_specs.append(tok(LANES, 0))
        out_shape.append(jax.ShapeDtypeStruct((bsz, s_out, LANES), F32))
    return pl.pallas_call(
        functools.partial(_outproj_kernel, nzb=nzb - off, alpha=alpha, with_router=with_router,
                          w_s5=w_s5, w_att=w_att, n_exp=n_exp),
        grid=(bsz, nblk),
        in_specs=in_specs,
        out_specs=out_specs,
        out_shape=out_shape,
        compiler_params=_cparams(("parallel", "parallel")),
        name="outproj",
    )(*args)


def _swiglu_into(x_bf16, w1_ref, w3_ref, w2_ref, acc_ref, fc):
    for s in range(0, w1_ref.shape[1], fc):
        h1 = _dot(x_bf16, w1_ref[:, s:s + fc])
        h3 = _dot(x_bf16, w3_ref[:, s:s + fc])
        acc_ref[...] += _dot((_silu(h1) * h3).astype(BF16), w2_ref[s:s + fc, :])


def _ffn_kernel(fx_ref, x1_ref, modx_ref, modz_ref, w1_ref, w3_ref, w2_ref, g2_ref, b2_ref,
                o_ref, acc_ref, *, nzb, alpha, fc):
    acc_ref[...] = jnp.zeros_like(acc_ref)
    _swiglu_into(fx_ref[...], w1_ref, w3_ref, w2_ref, acc_ref, fc)
    is_ctx = pl.program_id(1) < nzb
    mod = jnp.where(is_ctx, modz_ref[...], modx_ref[...])
    o_ref[...] = _ln(alpha * x1_ref[...] + mod[5:6] * acc_ref[...]) * g2_ref[...] + b2_ref[...]


def _ffn(fx, x1, mod, w1, w3, w2, ln_g, ln_b, n_ctx_tokens, alpha):
    bsz, s, d = x1.shape
    ff = w1.shape[1]
    tm = TOKEN_BLOCK
    nrow = mod.shape[0]
    tok = pl.BlockSpec((None, tm, d), lambda b, j: (b, j, 0))
    full = lambda a: pl.BlockSpec(a.shape, lambda b, j: (0,) * a.ndim)
    vec = lambda a: a.reshape(1, -1).astype(F32)
    return pl.pallas_call(
        functools.partial(_ffn_kernel, nzb=n_ctx_tokens // tm, alpha=alpha, fc=_pick_chunk(ff, 512)),
        grid=(bsz, s // tm),
        in_specs=[tok, tok,
                  pl.BlockSpec((None, 6, d), lambda b, j: (b, 0, 0)),
                  pl.BlockSpec((None, 6, d), lambda b, j: (nrow - 1, 0, 0)),
                  full(w1), full(w3), full(w2),
                  pl.BlockSpec((1, d), lambda b, j: (0, 0)), pl.BlockSpec((1, d), lambda b, j: (0, 0))],
        out_specs=tok,
        out_shape=jax.ShapeDtypeStruct((bsz, s, d), F32),
        scratch_shapes=[pltpu.VMEM((tm, d), F32)],
        compiler_params=_cparams(("parallel", "parallel")),
        name="dense_ffn",
    )(fx, x1, mod, mod, w1, w3, w2, vec(ln_g), vec(ln_b))


MOE_TILE = 1024


def _route_plan(e1, e2, n_exp, tile):
    n = e1.shape[0]
    pair_e = jnp.stack([e1, e2], axis=1).reshape(-1)
    onehot = (pair_e[:, None] == jnp.arange(n_exp, dtype=jnp.int32)[None, :]).astype(jnp.int32)
    before = jnp.cumsum(onehot, axis=0) - onehot
    rank = jnp.sum(before * onehot, axis=1)
    counts = jnp.sum(onehot, axis=0)
    padded = (counts + tile - 1) // tile * tile
    ends = jnp.cumsum(padded)
    starts = ends - padded
    dest = starts[pair_e] + rank
    n_rows = (2 * n + n_exp * (tile - 1)) // tile * tile
    n_tiles = n_rows // tile
    row_token = jnp.zeros((n_rows,), jnp.int32).at[dest].set(jnp.arange(2 * n, dtype=jnp.int32) // 2)
    tile_start = jnp.arange(n_tiles, dtype=jnp.int32) * tile
    tile_expert = jnp.minimum(jnp.sum((tile_start[:, None] >= ends[None, :]).astype(jnp.int32), axis=1),
                              n_exp - 1)
    n_used = (ends[-1] // tile).astype(jnp.int32).reshape(1)
    return row_token, dest.reshape(n, 2), tile_expert, n_used


def _moe_gather_kernel(tok_ref, x_hbm, o_ref, sem, *, tile):
    def row_copy(r):
        return pltpu.make_async_copy(x_hbm.at[pl.ds(tok_ref[0, r], 1)], o_ref.at[pl.ds(r, 1)], sem)

    def issue(r, carry):
        row_copy(r).start()
        return carry

    def drain(r, carry):
        row_copy(r).wait()
        return carry

    lax.fori_loop(0, tile, issue, 0)
    lax.fori_loop(0, tile, drain, 0)


def _moe_gather(x_flat, row_token, tile):
    n_rows = row_token.shape[0]
    n_tiles = n_rows // tile
    d = x_flat.shape[1]
    return pl.pallas_call(
        functools.partial(_moe_gather_kernel, tile=tile),
        grid=(n_tiles,),
        in_specs=[pl.BlockSpec((None, 1, tile), lambda i: (i, 0, 0), memory_space=pltpu.SMEM),
                  pl.BlockSpec(memory_space=pl.ANY)],
        out_specs=pl.BlockSpec((tile, d), lambda i: (i, 0)),
        out_shape=jax.ShapeDtypeStruct((n_rows, d), x_flat.dtype),
        scratch_shapes=[pltpu.SemaphoreType.DMA(())],
        compiler_params=_cparams(("arbitrary",)),
        name="moe_gather",
    )(row_token.reshape(n_tiles, 1, tile), x_flat)


def _moe_gemm_kernel(te_ref, nused_ref, x_ref, w1_ref, w3_ref, w2_ref, y_ref, *, fc):
    @pl.when(pl.program_id(1) == 0)
    def _():
        y_ref[...] = jnp.zeros_like(y_ref)

    @pl.when(pl.program_id(0) < nused_ref[0])
    def _():
        _swiglu_into(x_ref[...].astype(BF16), w1_ref, w3_ref, w2_ref, y_ref, fc)


def _moe_gemm(xs, tile_expert, n_used, w1, w3, w2, tile):
    n_rows, d = xs.shape
    ff = w1.shape[2]
    nf = 2 if ff % (2 * LANES) == 0 else 1
    tf = ff // nf
    last = lambda i, nu: jnp.minimum(i, nu[0] - 1)
    fsel = lambda i, f, nu: jnp.where(i < nu[0], f, nf - 1)
    return pl.pallas_call(
        functools.partial(_moe_gemm_kernel, fc=_pick_chunk(tf, 256)),
        grid_spec=pltpu.PrefetchScalarGridSpec(
            num_scalar_prefetch=2,
            grid=(n_rows // tile, nf),
            in_specs=[pl.BlockSpec((tile, d), lambda i, f, te, nu: (last(i, nu), 0)),
                      pl.BlockSpec((None, d, tf), lambda i, f, te, nu: (te[last(i, nu)], 0, fsel(i, f, nu))),
                      pl.BlockSpec((None, d, tf), lambda i, f, te, nu: (te[last(i, nu)], 0, fsel(i, f, nu))),
                      pl.BlockSpec((None, tf, d), lambda i, f, te, nu: (te[last(i, nu)], fsel(i, f, nu), 0))],
            out_specs=pl.BlockSpec((tile, d), lambda i, f, te, nu: (i, 0))),
        out_shape=jax.ShapeDtypeStruct((n_rows, d), F32),
        compiler_params=_cparams(("arbitrary", "arbitrary")),
        name="moe_gemm",
    )(tile_expert, n_used, xs, w1, w3, w2)


def _moe_combine_kernel(pos_ref, posn_ref, route_ref, x1_ref, modx_ref, modz_ref, g2_ref, b2_ref, y_hbm,
                        o_ref, buf_ref, sem, *, nzb, nblk, alpha, tm):
    i = pl.program_id(0)
    slot = i % 2

    def row_copy(p_ref, r, k, s):
        return pltpu.make_async_copy(y_hbm.at[pl.ds(p_ref[0, k * tm + r], 1)],
                                     buf_ref.at[s, k, pl.ds(r, 1)], sem.at[s])

    def issue(p_ref, s):
        def body(r, carry):
            row_copy(p_ref, r, 0, s).start()
            row_copy(p_ref, r, 1, s).start()
            return carry
        lax.fori_loop(0, tm, body, 0)

    @pl.when(i == 0)
    def _():
        issue(pos_ref, 0)

    @pl.when(i + 1 < pl.num_programs(0))
    def _():
        issue(posn_ref, 1 - slot)

    def drain(r, carry):
        row_copy(pos_ref, r, 0, slot).wait()
        row_copy(pos_ref, r, 1, slot).wait()
        return carry

    lax.fori_loop(0, tm, drain, 0)
    route = route_ref[...]
    f = route[:, 0:1] * buf_ref[slot, 0] + route[:, 1:2] * buf_ref[slot, 1]
    is_ctx = (i % nblk) < nzb
    mod = jnp.where(is_ctx, modz_ref[...], modx_ref[...])
    o_ref[...] = _ln(alpha * x1_ref[...] + mod[5:6] * f) * g2_ref[...] + b2_ref[...]


def _moe_combine(ys, pos, route, x1, mod, ln_g, ln_b, n_ctx_tokens, alpha):
    bsz, s, d = x1.shape
    tm = TOKEN_BLOCK
    nblk = s // tm
    n_steps = bsz * nblk
    nrow = mod.shape[0]
    pos_steps = pos.reshape(n_steps, tm, 2).transpose(0, 2, 1).reshape(n_steps, 1, 2 * tm)
    tok = lambda width: pl.BlockSpec((tm, width), lambda i: (i, 0))
    smem = lambda imap: pl.BlockSpec((None, 1, 2 * tm), imap, memory_space=pltpu.SMEM)
    vec = lambda a: a.reshape(1, -1).astype(F32)
    out = pl.pallas_call(
        functools.partial(_moe_combine_kernel, nzb=n_ctx_tokens // tm, nblk=nblk, alpha=alpha, tm=tm),
        grid=(n_steps,),
        in_specs=[smem(lambda i: (i, 0, 0)),
                  smem(lambda i: (jnp.minimum(i + 1, n_steps - 1), 0, 0)),
                  tok(LANES), tok(d),
                  pl.BlockSpec((None, 6, d), lambda i: (i // nblk, 0, 0)),
                  pl.BlockSpec((None, 6, d), lambda i: (nrow - 1, 0, 0)),
                  pl.BlockSpec((1, d), lambda i: (0, 0)), pl.BlockSpec((1, d), lambda i: (0, 0)),
                  pl.BlockSpec(memory_space=pl.ANY)],
        out_specs=tok(d),
        out_shape=jax.ShapeDtypeStruct((bsz * s, d), F32),
        scratch_shapes=[pltpu.VMEM((2, 2, tm, d), F32), pltpu.SemaphoreType.DMA((2,))],
        compiler_params=_cparams(("arbitrary",)),
        name="moe_combine",
    )(pos_steps, pos_steps, route.reshape(bsz * s, LANES), x1.reshape(bsz * s, d), mod, mod,
      vec(ln_g), vec(ln_b), ys)
    return out.reshape(bsz, s, d)


def _moe(fx, x1, route, mod, w1, w3, w2, ln_g, ln_b, n_ctx_tokens, alpha):
    bsz, s, d = x1.shape
    n_exp = w1.shape[0]
    idx = route.reshape(bsz * s, LANES)[:, 2:4].astype(jnp.int32)
    row_token, pos, tile_expert, n_used = _route_plan(idx[:, 0], idx[:, 1], n_exp, MOE_TILE)
    xs = _moe_gather(fx.reshape(bsz * s, d), row_token, MOE_TILE)
    ys = _moe_gemm(xs, tile_expert, n_used, w1, w3, w2, MOE_TILE)
    return _moe_combine(ys, pos, route, x1, mod, ln_g, ln_b, n_ctx_tokens, alpha)


def _pick_chunk(total, target):
    best = LANES
    for c in range(LANES, target + 1, LANES):
        if total % c == 0:
            best = c
    return best


def kernel(x, c, ctx, c_ctx, w_mod, b_mod, w_in, s5_lam_re, s5_lam_im, s5_log_step, s5_b_re, s5_b_im,
           s5_c_re, s5_c_im, s5_d, s5_w_glu, s5_b_glu, attn_sink, ret_log_gamma, w_out,
           ln1_g, ln1_b, ln2_g, ln2_b, ffn_w1, ffn_w3, ffn_w2, moe_router, moe_w1, moe_w3, moe_w2):
    bsz, t_len, d = x.shape
    n_ctx = ctx.shape[1]
    depth = w_in.shape[0]
    alpha = (2 * depth) ** 0.25
    s5_w = s5_d.shape[1]
    att_w = attn_sink.shape[1] * HEAD_DIM
    kv_w = att_w // ATT_REP
    ret_w = ret_log_gamma.shape[2] * HEAD_DIM
    sizes = (s5_w, att_w, kv_w, kv_w, ret_w, ret_w, ret_w, ret_w)
    assert sum(sizes) == w_in.shape[2] and s5_w + att_w + ret_w == w_out.shape[1]
    assert n_ctx % TOKEN_BLOCK == 0 and t_len % TOKEN_BLOCK == 0 and t_len >= 3 * ATT_BLOCK

    pad = (-(bsz + 1)) % 8
    cvec = jnp.concatenate([jnp.zeros((pad, d), F32), c_ctx[None].astype(F32)], axis=0)
    cvec = jnp.concatenate([c.astype(F32), cvec], axis=0)
    mod_all = _modulation(cvec, w_mod.astype(F32), b_mod.astype(F32)).reshape(depth, bsz + pad + 1, 6, d)

    tabs = _rope_tables(t_len)
    xz = jnp.concatenate([ctx, x], axis=1).astype(F32)
    for l in range(depth):
        need_ctx = l < depth - 1
        mod = mod_all[l]
        u, qa, ka, va, qr, kr, vr, gr = _inproj(xz, mod, w_in[l].astype(BF16), tabs, sizes, n_ctx)
        s5w = _s5_weights(s5_lam_re[l], s5_lam_im[l], s5_log_step[l], s5_b_re[l], s5_b_im[l],
                          s5_c_re[l], s5_c_im[l], S5_CHUNK)
        y_s5 = _s5_scan(u, s5w, n_ctx)
        o_att = _attention(qa, ka, va, attn_sink[l], n_ctx, need_ctx)
        o_ret = _retention(qr, kr, vr, gr, ret_log_gamma[l], n_ctx)
        i = l // 2
        router = None if l % 2 == 0 else moe_router[i]
        outs = _outproj(y_s5, u, o_att, o_ret, xz, mod, s5_d[l], s5_w_glu[l], s5_b_glu[l], w_out[l],
                        ln1_g[l], ln1_b[l], router, n_ctx, need_ctx, alpha)
        ctx_tokens = n_ctx if need_ctx else 0
        if l % 2 == 0:
            x1, fx = outs
            xz = _ffn(fx, x1, mod, ffn_w1[i].astype(BF16), ffn_w3[i].astype(BF16), ffn_w2[i].astype(BF16),
                      ln2_g[l], ln2_b[l], ctx_tokens, alpha)
        else:
            x1, fx, route = outs
            xz = _moe(fx, x1, route, mod, moe_w1[i].astype(BF16), moe_w3[i].astype(BF16),
                      moe_w2[i].astype(BF16), ln2_g[l], ln2_b[l], ctx_tokens, alpha)
    return xz if xz.shape[1] == t_len else xz[:, n_ctx:]
```

```python
import functools
import math

import jax
import jax.numpy as jnp
import numpy as np
from jax import lax
from jax.experimental import pallas as pl
from jax.experimental.pallas import tpu as pltpu

F32 = jnp.float32
BF16 = jnp.bfloat16

GRID_W = 64
HEAD_DIM = 64
S5_GROUP = 16
ATT_REP = 4
WINDOW = 128
ATT_BLOCK = 128
RET_CHUNK = 128
TOP_K = 2
LN_EPS = 1e-5
ROPE_BASE = 10000.0
NEG_INF = -1e30

LANES = 128
S5_CHUNK = LANES
TOKEN_BLOCK = 256
FFN_BLOCK = 768
VMEM_LIMIT = 56 * 1024 * 1024


def _cparams(sem):
    return pltpu.CompilerParams(dimension_semantics=sem, vmem_limit_bytes=VMEM_LIMIT)


def _dot(a, b):
    return jnp.dot(a, b, preferred_element_type=F32)


def _dot_nt(a, b):
    return lax.dot_general(a, b, (((1,), (1,)), ((), ())), preferred_element_type=F32)


def _dot_tn(a, b):
    return lax.dot_general(a, b, (((0,), (0,)), ((), ())), preferred_element_type=F32)


def _split_bf16(a):
    hi = a.astype(BF16)
    lo = (a - hi.astype(F32)).astype(BF16)
    return hi, lo


def _dot3(a, b):
    ah, al = _split_bf16(a)
    bh, bl = _split_bf16(b)
    return _dot(ah, bh) + (_dot(ah, bl) + _dot(al, bh))


def _ln(x):
    mu = jnp.mean(x, axis=-1, keepdims=True)
    xc = x - mu
    var = jnp.mean(xc * xc, axis=-1, keepdims=True)
    return xc * lax.rsqrt(var + LN_EPS)


def _silu(x):
    return x * (1.0 / (1.0 + jnp.exp(-x)))


def _sigmoid(x):
    return 1.0 / (1.0 + jnp.exp(-x))


def _gelu_tanh(x):
    c = math.sqrt(2.0 / math.pi)
    return 0.5 * x * (1.0 + jnp.tanh(c * (x + 0.044715 * (x * x * x))))


def _mod_kernel(c_ref, w_ref, b_ref, o_ref):
    o_ref[...] = _dot3(_silu(c_ref[...]), w_ref[...]) + b_ref[...]


def _modulation(cvec, w_mod, b_mod):
    depth, d, n = w_mod.shape
    rows = cvec.shape[0]
    tn = 1536
    return pl.pallas_call(
        _mod_kernel,
        grid=(depth, n // tn),
        in_specs=[pl.BlockSpec((rows, d), lambda l, j: (0, 0)),
                  pl.BlockSpec((None, d, tn), lambda l, j: (l, 0, j)),
                  pl.BlockSpec((None, 1, tn), lambda l, j: (l, 0, j))],
        out_specs=pl.BlockSpec((None, rows, tn), lambda l, j: (l, 0, j)),
        out_shape=jax.ShapeDtypeStruct((depth, rows, n), F32),
        compiler_params=_cparams(("parallel", "parallel")),
        name="modulation",
    )(cvec, w_mod, b_mod.reshape(depth, 1, n))


def _rope_slab(xs, cos, sa, sb, half):
    return xs * cos + pltpu.roll(xs, LANES - half, 1) * sa + pltpu.roll(xs, half, 1) * sb


def _inproj_kernel(x_ref, modx_ref, modz_ref, wu_ref, w_ref, ac_ref, asa_ref, asb_ref, rc_ref, rsa_ref, rsb_ref,
                   ut_ref, qa_ref, ka_ref, va_ref, qr_ref, kr_ref, vr_ref, gr_ref, *, nzb, sizes):
    is_ctx = pl.program_id(1) < nzb
    mod = jnp.where(is_ctx, modz_ref[...], modx_ref[...])
    h = (_ln(x_ref[...]) * (1.0 + mod[1:2]) + mod[0:1]).astype(BF16)
    ut_ref[...] = _dot_nt(wu_ref[...], h)
    p = _dot(h, w_ref[...])
    ac = jnp.where(is_ctx, 1.0, ac_ref[...])
    asa = jnp.where(is_ctx, 0.0, asa_ref[...])
    asb = jnp.where(is_ctx, 0.0, asb_ref[...])
    rc = jnp.where(is_ctx, 1.0, rc_ref[...])
    rsa = jnp.where(is_ctx, 0.0, rsa_ref[...])
    rsb = jnp.where(is_ctx, 0.0, rsb_ref[...])
    offs = np.concatenate([[0], np.cumsum(sizes)])
    scale = HEAD_DIM ** -0.5

    def cols(i):
        return p[:, int(offs[i]):int(offs[i + 1])]

    def rope_cols(i, cos, sa, sb, half, mul):
        blk = cols(i)
        slabs = [_rope_slab(blk[:, s:s + LANES], cos, sa, sb, half) for s in range(0, blk.shape[1], LANES)]
        out = slabs[0] if len(slabs) == 1 else jnp.concatenate(slabs, axis=1)
        return out * mul if mul != 1.0 else out

    qa_ref[...] = rope_cols(0, ac, asa, asb, HEAD_DIM // 4, scale)
    ka_ref[...] = rope_cols(1, ac, asa, asb, HEAD_DIM // 4, 1.0)
    va_ref[...] = cols(2)
    qr_ref[...] = rope_cols(3, rc, rsa, rsb, HEAD_DIM // 2, 1.0)
    kr_ref[...] = rope_cols(4, rc, rsa, rsb, HEAD_DIM // 2, scale)
    vr_ref[...] = cols(5)
    gr_ref[...] = cols(6)


def _inproj(xz, mod, w_in, tabs, sizes, n_ctx):
    bsz, s, d = xz.shape
    tm = TOKEN_BLOCK
    nzb = n_ctx // tm
    nrow = mod.shape[0]
    s5_w, rest = sizes[0], sizes[1:]
    w_u_t = w_in[:, :s5_w].T
    w_rest = w_in[:, s5_w:]
    tok = lambda width: pl.BlockSpec((None, tm, width), lambda b, j: (b, j, 0))
    tab = pl.BlockSpec((tm, LANES), lambda b, j: (jnp.maximum(j - nzb, 0), 0))
    return pl.pallas_call(
        functools.partial(_inproj_kernel, nzb=nzb, sizes=rest),
        grid=(bsz, s // tm),
        in_specs=[tok(d),
                  pl.BlockSpec((None, 6, d), lambda b, j: (b, 0, 0)),
                  pl.BlockSpec((None, 6, d), lambda b, j: (nrow - 1, 0, 0)),
                  pl.BlockSpec((s5_w, d), lambda b, j: (0, 0)),
                  pl.BlockSpec((d, sum(rest)), lambda b, j: (0, 0)),
                  tab, tab, tab, tab, tab, tab],
        out_specs=[pl.BlockSpec((None, None, s5_w, tm), lambda b, j: (j, b, 0, 0))] + [tok(w) for w in rest],
        out_shape=[jax.ShapeDtypeStruct((s // tm, bsz, s5_w, tm), F32)]
                  + [jax.ShapeDtypeStruct((bsz, s, w), F32) for w in rest],
        compiler_params=_cparams(("parallel", "parallel")),
        name="inproj",
    )(xz, mod, mod, w_u_t, w_rest, *tabs)


def _rope_tables(t_len):
    t = jnp.arange(t_len)
    rows = (t // GRID_W).astype(F32)
    cols = (t % GRID_W).astype(F32)
    pos = t.astype(F32)

    def angles(p, dim):
        inv_freq = ROPE_BASE ** (-jnp.arange(0, dim, 2, dtype=F32) / dim)
        return p[:, None] * inv_freq[None, :]

    def head_tables(angs):
        cos = jnp.concatenate([jnp.concatenate([jnp.cos(a), jnp.cos(a)], -1) for a in angs], -1)
        sa = jnp.concatenate([jnp.concatenate([-jnp.sin(a), jnp.zeros_like(a)], -1) for a in angs], -1)
        sb = jnp.concatenate([jnp.concatenate([jnp.zeros_like(a), jnp.sin(a)], -1) for a in angs], -1)
        rep = LANES // HEAD_DIM
        return tuple(jnp.tile(x, (1, rep)) for x in (cos, sa, sb))

    att = head_tables([angles(rows, HEAD_DIM // 2), angles(cols, HEAD_DIM // 2)])
    ret = head_tables([angles(pos, HEAD_DIM)])
    return att + ret


def _s5_kernel(u_ref, tab_ref, wsf_ref, wsb_ref, wrf_ref, wrb_ref, lam_ref, y_ref,
               lhs_ref, m_ref, acc_ref, cf_ref, cb_ref, pf_ref, pb_ref, *, bsz, nz_chunks):
    r_blk, ch, tm = u_ref.shape
    c = S5_CHUNK
    kpb = tm // c
    n_chunks = (r_blk // bsz) * kpb

    for i in range(ch):
        x = u_ref[:, i, :].astype(BF16)
        p, half = divmod(i, 2)
        for k in range(kpb):
            lhs_ref[p, k * r_blk:(k + 1) * r_blk, half * c:(half + 1) * c] = x[:, k * c:(k + 1) * c]
    acc_ref[...] = jnp.zeros_like(acc_ref)
    cf_ref[...] = jnp.zeros_like(cf_ref)
    cb_ref[...] = jnp.zeros_like(cb_ref)

    def pair(p, carry):
        for ih in range(2):
            for o in range(ch):
                tab = tab_ref[2 * p + ih, o]
                for r2 in range(c // 16):
                    lo = c - 16 * r2
                    rows = jnp.concatenate([tab[:, lo:lo + c], tab[:, lo - 8:lo - 8 + c]], axis=0)
                    m_ref[ih * c + 16 * r2:ih * c + 16 * r2 + 16, o * c:(o + 1) * c] = rows.astype(BF16)
        lhs = lhs_ref[p]
        acc_ref[...] += _dot(lhs, m_ref[...])
        cf_ref[...] += _dot(lhs, wsf_ref[p])
        cb_ref[...] += _dot(lhs, wsb_ref[p])
        return carry

    lax.fori_loop(0, ch // 2, pair, 0)
    lam = lam_ref[...]
    nst = lam.shape[1] // 2

    def advance(s, a, bc, drive):
        return s * a + pltpu.roll(s, nst, 1) * bc + drive

    def sweep(order, c_ref, p_ref, a, bc):
        s = jnp.zeros((bsz, lam.shape[1]), F32)
        for n in order:
            j, k = divmod(n, kpb)
            rows = slice(k * r_blk + j * bsz, k * r_blk + (j + 1) * bsz)
            p_ref[rows, :] = s
            s = advance(s, a, bc, c_ref[rows, :])

    sweep(list(range(n_chunks)), cf_ref, pf_ref, lam[0:1], lam[1:2])
    order_b = list(range(nz_chunks - 1, -1, -1)) + list(range(n_chunks - 1, nz_chunks - 1, -1))
    sweep(order_b, cb_ref, pb_ref, lam[2:3], lam[3:4])
    acc_ref[...] += (_dot(pf_ref[...].astype(BF16), wrf_ref[...])
                     + _dot(pb_ref[...].astype(BF16), wrb_ref[...]))
    for o in range(ch):
        for k in range(kpb):
            y_ref[:, o, k * c:(k + 1) * c] = acc_ref[k * r_blk:(k + 1) * r_blk, o * c:(o + 1) * c]


def _s5_weights(lam_re, lam_im, log_step, b_re, b_im, c_re, c_im, chunk):
    hp = lax.Precision.HIGHEST
    lam = lax.complex(lam_re.astype(F32), lam_im.astype(F32))
    lam_dt = lam * jnp.exp(log_step.astype(F32))[..., None]
    lam_bar = jnp.exp(lam_dt)
    b_bar = lax.complex(b_re.astype(F32), b_im.astype(F32)) * ((lam_bar - 1.0) / lam)[..., None]
    c_mat = lax.complex(c_re.astype(F32), c_im.astype(F32))
    g, n = lam.shape[1], lam.shape[2]
    ch = b_bar.shape[-1]
    steps = jnp.arange(chunk + 1, dtype=F32)
    pw = jnp.exp(steps[None, :, None, None] * lam_dt[:, None])
    kern = jnp.einsum('zgon,zdgn,zgni->zgdoi', c_mat, pw[:, :chunk], b_bar, precision=hp).real
    zero_lag = kern[0, :, 0] + kern[1, :, 0]
    lag_table = jnp.concatenate([jnp.zeros_like(zero_lag)[:, None], kern[1, :, :0:-1], zero_lag[:, None],
                                 kern[0, :, 1:]], axis=1).transpose(0, 3, 2, 1)
    tab = jnp.stack([jnp.roll(lag_table, q, axis=-1) for q in range(8)], axis=3)

    def state_in(pw_s, b_dir):
        w = pw_s[:, :, :, None] * b_dir[None]
        w = jnp.concatenate([w.real, w.imag], axis=2)
        return w.transpose(1, 3, 0, 2).reshape(g, ch // 2, 2 * chunk, 2 * n)

    def state_out(pw_t, c_dir):
        w = c_dir[None] * pw_t[:, :, None, :]
        w = jnp.concatenate([w.real, -w.imag], axis=3)
        return w.transpose(1, 3, 2, 0).reshape(g, 2 * n, ch * chunk)

    wsf = state_in(pw[0, chunk - 1::-1][:chunk], b_bar[0])
    wsb = state_in(pw[1, :chunk], b_bar[1])
    wrf = state_out(pw[0, 1:chunk + 1], c_mat[0])
    wrb = state_out(pw[1, chunk:0:-1], c_mat[1])
    lam_c = pw[:, chunk]
    rows = []
    for z in range(2):
        rows.append(jnp.concatenate([lam_c[z].real, lam_c[z].real], -1))
        rows.append(jnp.concatenate([-lam_c[z].imag, lam_c[z].imag], -1))
    lam_rows = jnp.stack(rows + [jnp.zeros_like(rows[0])] * 4, axis=1)
    return tab, wsf.astype(BF16), wsb.astype(BF16), wrf.astype(BF16), wrb.astype(BF16), lam_rows


def _s5_scan(ut, weights, n_ctx):
    nblk, bsz, width, tm = ut.shape
    tab, wsf, wsb, wrf, wrb, lam_rows = weights
    g, ch = tab.shape[0], tab.shape[1]
    c = S5_CHUNK
    nst2 = wsf.shape[3]
    r_blk = nblk * bsz
    rows = r_blk * (tm // c)
    grp = lambda *shape: pl.BlockSpec((None,) + shape, lambda i: (i,) + (0,) * len(shape))
    tok = pl.BlockSpec((r_blk, ch, tm), lambda i: (0, i, 0))
    y = pl.pallas_call(
        functools.partial(_s5_kernel, bsz=bsz, nz_chunks=n_ctx // c),
        grid=(g,),
        in_specs=[tok, grp(ch, ch, 8, 2 * c), grp(ch // 2, 2 * c, nst2), grp(ch // 2, 2 * c, nst2),
                  grp(nst2, ch * c), grp(nst2, ch * c), grp(8, nst2)],
        out_specs=tok,
        out_shape=jax.ShapeDtypeStruct((r_blk, width, tm), F32),
        scratch_shapes=[pltpu.VMEM((ch // 2, rows, 2 * c), BF16),
                        pltpu.VMEM((2 * c, ch * c), BF16),
                        pltpu.VMEM((rows, ch * c), F32)] + [pltpu.VMEM((rows, nst2), F32)] * 4,
        compiler_params=_cparams(("parallel",)),
        name="s5_scan",
    )(ut.reshape(r_blk, width, tm), tab, wsf, wsb, wrf, wrb, lam_rows)
    return y.reshape(nblk, bsz, width, tm)


def _attn_kernel(sink_ref, q_ref, k_ref, v_ref, o_ref, *, n_ctx, t_len, q_off, kv_heads):
    qi = pl.program_id(1) + q_off
    nzb = n_ctx // ATT_BLOCK
    band = 3 * ATT_BLOCK
    q = q_ref[...]
    kc = k_ref[0:n_ctx, :].astype(BF16)
    vc = v_ref[0:n_ctx, :].astype(BF16)

    def head_cols(a, i):
        return a[:, i * HEAD_DIM:(i + 1) * HEAD_DIM]

    def put(h, o):
        o_ref[:, h * HEAD_DIM:(h + 1) * HEAD_DIM] = o

    @pl.when(qi < nzb)
    def _():
        for h in range(kv_heads * ATT_REP):
            g = h // ATT_REP
            sink = sink_ref[h]
            s = _dot_nt(head_cols(q, h).astype(BF16), head_cols(kc, g))
            m = jnp.maximum(jnp.max(s, axis=1, keepdims=True), sink)
            e = jnp.exp(s - m)
            den = jnp.sum(e, axis=1, keepdims=True) + jnp.exp(sink - m)
            put(h, _dot(e.astype(BF16), head_cols(vc, g)) / den)

    @pl.when(qi >= nzb)
    def _():
        n = qi - nzb
        start = jnp.clip((n - 1) * ATT_BLOCK, 0, t_len - band)
        kl = k_ref[pl.ds(pl.multiple_of(n_ctx + start, ATT_BLOCK), band), :].astype(BF16)
        vl = v_ref[pl.ds(pl.multiple_of(n_ctx + start, ATT_BLOCK), band), :].astype(BF16)
        q_pos = n * ATT_BLOCK + lax.broadcasted_iota(jnp.int32, (ATT_BLOCK, band), 0)
        k_pos = start + lax.broadcasted_iota(jnp.int32, (ATT_BLOCK, band), 1)
        valid = jnp.abs(k_pos - q_pos) <= WINDOW
        for h in range(kv_heads * ATT_REP):
            g = h // ATT_REP
            sink = sink_ref[h]
            qh = head_cols(q, h).astype(BF16)
            s_loc = jnp.where(valid, _dot_nt(qh, head_cols(kl, g)), NEG_INF)
            s_ctx = _dot_nt(qh, head_cols(kc, g))
            m = jnp.maximum(jnp.maximum(jnp.max(s_loc, axis=1, keepdims=True),
                                        jnp.max(s_ctx, axis=1, keepdims=True)), sink)
            e_loc = jnp.exp(s_loc - m)
            e_ctx = jnp.exp(s_ctx - m)
            den = (jnp.sum(e_loc, axis=1, keepdims=True) + jnp.sum(e_ctx, axis=1, keepdims=True)
                   + jnp.exp(sink - m))
            o = _dot(e_loc.astype(BF16), head_cols(vl, g)) + _dot(e_ctx.astype(BF16), head_cols(vc, g))
            put(h, o / den)


def _attention(qa, ka, va, sink, n_ctx, need_ctx):
    bsz, s, qw = qa.shape
    kvw = ka.shape[2]
    t_len = s - n_ctx
    q_off = 0 if need_ctx else n_ctx // ATT_BLOCK
    nq = s // ATT_BLOCK - q_off
    return pl.pallas_call(
        functools.partial(_attn_kernel, n_ctx=n_ctx, t_len=t_len, q_off=q_off, kv_heads=kvw // HEAD_DIM),
        grid=(bsz, nq),
        in_specs=[pl.BlockSpec(memory_space=pltpu.SMEM),
                  pl.BlockSpec((None, ATT_BLOCK, qw), lambda b, j: (b, j + q_off, 0)),
                  pl.BlockSpec((None, s, kvw), lambda b, j: (b, 0, 0)),
                  pl.BlockSpec((None, s, kvw), lambda b, j: (b, 0, 0))],
        out_specs=pl.BlockSpec((None, ATT_BLOCK, qw), lambda b, j: (b, j, 0)),
        out_shape=jax.ShapeDtypeStruct((bsz, nq * ATT_BLOCK, qw), F32),
        compiler_params=_cparams(("parallel", "arbitrary")),
        name="window_attention",
    )(sink.astype(F32), qa, ka, va)


def _ret_kernel(lg_ref, q_ref, k_ref, v_ref, g_ref, o_ref, acc_ref, dec_ref, sf_ref, sb_ref,
                *, n_chunks, nz_chunks, heads):
    c = RET_CHUNK
    ii = lax.broadcasted_iota(jnp.int32, (c, c), 0)
    jj = lax.broadcasted_iota(jnp.int32, (c, c), 1)
    diff = (ii - jj).astype(F32)
    idx = lax.broadcasted_iota(jnp.int32, (c, 1), 0).astype(F32)
    for h in range(heads):
        lgf, lgb = lg_ref[0, h], lg_ref[1, h]
        dec_ref[h] = jnp.where(diff >= 0, jnp.exp(lgf * jnp.maximum(diff, 0.0)),
                               jnp.exp(lgb * jnp.maximum(-diff, 0.0)))
    sf_ref[...] = jnp.zeros_like(sf_ref)
    sb_ref[...] = jnp.zeros_like(sb_ref)

    def hcols(a, h):
        return a[:, h * HEAD_DIM:(h + 1) * HEAD_DIM]

    chunk_len = jnp.full((1, HEAD_DIM), float(c), F32)

    def fwd(n, carry):
        rows = pl.ds(pl.multiple_of(n * c, c), c)
        q, k, v = q_ref[rows, :], k_ref[rows, :], v_ref[rows, :]
        for h in range(heads):
            lgf = lg_ref[0, h]
            cols = slice(h * HEAD_DIM, (h + 1) * HEAD_DIM)
            qh, kh, vh = hcols(q, h), hcols(k, h), hcols(v, h).astype(BF16)
            scores = _dot_nt(qh.astype(BF16), kh.astype(BF16)) * dec_ref[h]
            s_prev = sf_ref[h]
            o = _dot(scores.astype(BF16), vh)
            o = o + _dot((qh * jnp.exp(lgf * (idx + 1.0))).astype(BF16), s_prev.astype(BF16))
            kw = kh * jnp.exp(lgf * (c - 1.0 - idx))
            sf_ref[h] = jnp.exp(lgf * chunk_len) * s_prev + _dot_tn(kw.astype(BF16), vh)
            acc_ref[rows, cols] = o
        return carry

    lax.fori_loop(0, n_chunks, fwd, 0)

    def bwd(i, carry):
        n = jnp.where(i < nz_chunks, nz_chunks - 1 - i, n_chunks - 1 - i + nz_chunks)
        rows = pl.ds(pl.multiple_of(n * c, c), c)
        q, k, v = q_ref[rows, :], k_ref[rows, :], v_ref[rows, :]
        acc = acc_ref[rows, :]
        gate = _silu(g_ref[rows, :])
        for h in range(heads):
            lgb = lg_ref[1, h]
            cols = slice(h * HEAD_DIM, (h + 1) * HEAD_DIM)
            qh, kh, vh = hcols(q, h), hcols(k, h), hcols(v, h).astype(BF16)
            s_prev = sb_ref[h]
            o = hcols(acc, h) + _dot((qh * jnp.exp(lgb * (c - idx))).astype(BF16), s_prev.astype(BF16))
            kw = kh * jnp.exp(lgb * idx)
            sb_ref[h] = jnp.exp(lgb * chunk_len) * s_prev + _dot_tn(kw.astype(BF16), vh)
            o_ref[rows, cols] = _ln(o) * hcols(gate, h)
        return carry

    lax.fori_loop(0, n_chunks, bwd, 0)


def _retention(qr, kr, vr, gr, log_gamma, n_ctx):
    bsz, s, w = qr.shape
    heads = w // HEAD_DIM
    n_chunks = s // RET_CHUNK
    tok = pl.BlockSpec((None, s, w), lambda b: (b, 0, 0))
    return pl.pallas_call(
        functools.partial(_ret_kernel, n_chunks=n_chunks, nz_chunks=n_ctx // RET_CHUNK, heads=heads),
        grid=(bsz,),
        in_specs=[pl.BlockSpec(memory_space=pltpu.SMEM), tok, tok, tok, tok],
        out_specs=tok,
        out_shape=jax.ShapeDtypeStruct((bsz, s, w), F32),
        scratch_shapes=[pltpu.VMEM((s, w), F32),
                        pltpu.VMEM((heads, RET_CHUNK, RET_CHUNK), F32),
                        pltpu.VMEM((heads, HEAD_DIM, HEAD_DIM), F32),
                        pltpu.VMEM((heads, HEAD_DIM, HEAD_DIM), F32)],
        compiler_params=_cparams(("parallel",)),
        name="retention",
    )(log_gamma.astype(F32), qr, kr, vr, gr)


def _outproj_kernel(*refs, nzb, alpha, with_router, w_s5, w_att, n_exp):
    if with_router:
        (y_ref, u_ref, a_ref, r_ref, x_ref, modx_ref, modz_ref, d_ref, wg_ref, bg_ref, wo_ref,
         g1_ref, b1_ref, rt_ref, x1_ref, fx_ref, gate_ref) = refs
    else:
        (y_ref, u_ref, a_ref, r_ref, x_ref, modx_ref, modz_ref, d_ref, wg_ref, bg_ref, wo_ref,
         g1_ref, b1_ref, x1_ref, fx_ref) = refs
    is_ctx = pl.program_id(1) < nzb
    mod = jnp.where(is_ctx, modz_ref[...], modx_ref[...])
    g = _gelu_tanh(y_ref[...] + d_ref[...] * u_ref[...])
    s5 = g * _sigmoid(_dot(wg_ref[...], g.astype(BF16)) + bg_ref[...])
    mix = (_dot_tn(s5.astype(BF16), wo_ref[0:w_s5, :])
           + _dot(a_ref[...].astype(BF16), wo_ref[w_s5:w_s5 + w_att, :])
           + _dot(r_ref[...].astype(BF16), wo_ref[w_s5 + w_att:, :]))
    x1 = _ln(alpha * x_ref[...] + mod[2:3] * mix) * g1_ref[...] + b1_ref[...]
    x1_ref[...] = x1
    fx = _ln(x1) * (1.0 + mod[4:5]) + mod[3:4]
    fx_ref[...] = fx.astype(fx_ref.dtype)
    if with_router:
        lane = lax.broadcasted_iota(jnp.int32, (fx.shape[0], LANES), 1)
        logits = jnp.where(lane < n_exp, _dot3(fx, rt_ref[...]), -jnp.inf)
        m1 = jnp.max(logits, axis=1, keepdims=True)
        i1 = jnp.min(jnp.where(logits == m1, lane, LANES), axis=1, keepdims=True)
        rest = jnp.where(lane == i1, -jnp.inf, logits)
        m2 = jnp.max(rest, axis=1, keepdims=True)
        i2 = jnp.min(jnp.where(rest == m2, lane, LANES), axis=1, keepdims=True)
        e2 = jnp.exp(m2 - m1)
        den = 1.0 + e2
        route = jnp.where(lane == 0, 1.0 / den, jnp.where(lane == 1, e2 / den, 0.0))
        route = jnp.where(lane == 2, i1.astype(F32), jnp.where(lane == 3, i2.astype(F32), route))
        gate_ref[...] = route


def _outproj(y_s5, u, o_att, o_ret, xz, mod, s5_d, w_glu, b_glu, w_out, ln_g, ln_b, router, n_ctx, need_ctx,
             alpha):
    bsz, s, d = xz.shape
    tm = TOKEN_BLOCK
    nzb = n_ctx // tm
    off = 0 if need_ctx else nzb
    nblk = s // tm - off
    s_out = nblk * tm
    nrow = mod.shape[0]
    w_s5, w_att = y_s5.shape[2], o_att.shape[2]
    att_off = off if o_att.shape[1] == s else 0

    def tok(width, shift):
        return pl.BlockSpec((None, tm, width), lambda b, j: (b, j + shift, 0))

    def full(a):
        return pl.BlockSpec(a.shape, lambda b, j: (0,) * a.ndim)

    chan = pl.BlockSpec((None, None, w_s5, tm), lambda b, j: (j + off, b, 0, 0))
    vec = lambda a: a.reshape(1, -1).astype(F32)
    col = lambda a: a.reshape(-1, 1).astype(F32)
    consts = [col(s5_d), w_glu.T.astype(BF16), col(b_glu), w_out.astype(BF16), vec(ln_g), vec(ln_b)]
    in_specs = [chan, chan, tok(w_att, att_off), tok(o_ret.shape[2], off), tok(d, off),
                pl.BlockSpec((None, 6, d), lambda b, j: (b, 0, 0)),
                pl.BlockSpec((None, 6, d), lambda b, j: (nrow - 1, 0, 0))] + [full(a) for a in consts]
    out_specs = [tok(d, 0), tok(d, 0)]
    fx_dtype = BF16 if router is None else F32
    out_shape = [jax.ShapeDtypeStruct((bsz, s_out, d), F32), jax.ShapeDtypeStruct((bsz, s_out, d), fx_dtype)]
    args = [y_s5, u, o_att, o_ret, xz, mod, mod] + consts
    with_router = router is not None
    n_exp = 0
    if with_router:
        n_exp = router.shape[1]
        router_pad = jnp.pad(router.astype(F32), ((0, 0), (0, LANES - n_exp)))
        args.append(router_pad)
        in_specs.append(full(router_pad))
        out_specs.append(tok(LANES, 0))
        out_shape.append(jax.ShapeDtypeStruct((bsz, s_out, LANES), F32))
    return pl.pallas_call(
        functools.partial(_outproj_kernel, nzb=nzb - off, alpha=alpha, with_router=with_router,
                          w_s5=w_s5, w_att=w_att, n_exp=n_exp),
        grid=(bsz, nblk),
        in_specs=in_specs,
        out_specs=out_specs,
        out_shape=out_shape,
        compiler_params=_cparams(("parallel", "parallel")),
        name="outproj",
    )(*args)


def _swiglu_into(x_bf16, w1_ref, w3_ref, w2_ref, acc_ref, fc):
    for s in range(0, w1_ref.shape[1], fc):
        h1 = _dot(x_bf16, w1_ref[:, s:s + fc])
        h3 = _dot(x_bf16, w3_ref[:, s:s + fc])
        acc_ref[...] += _dot((_silu(h1) * h3).astype(BF16), w2_ref[s:s + fc, :])


def _ffn_kernel(fx_ref, x1_ref, modx_ref, modz_ref, w1_ref, w3_ref, w2_ref, g2_ref, b2_ref,
                o_ref, acc_ref, *, n_ctx_tokens, alpha, fc):
    tm = acc_ref.shape[0]
    acc_ref[...] = jnp.zeros_like(acc_ref)
    _swiglu_into(fx_ref[...], w1_ref, w3_ref, w2_ref, acc_ref, fc)
    row = pl.program_id(1) * tm + lax.broadcasted_iota(jnp.int32, (tm, 1), 0)
    gate = jnp.where(row < n_ctx_tokens, modz_ref[5:6, :], modx_ref[5:6, :])
    o_ref[...] = _ln(alpha * x1_ref[...] + gate * acc_ref[...]) * g2_ref[...] + b2_ref[...]


def _ffn(fx, x1, mod, w1, w3, w2, ln_g, ln_b, n_ctx_tokens, alpha):
    bsz, s, d = x1.shape
    ff = w1.shape[1]
    tm = _pick_chunk(s, FFN_BLOCK)
    nrow = mod.shape[0]
    tok = pl.BlockSpec((None, tm, d), lambda b, j: (b, j, 0))
    full = lambda a: pl.BlockSpec(a.shape, lambda b, j: (0,) * a.ndim, pipeline_mode=pl.Buffered(1))
    vec = lambda a: a.reshape(1, -1).astype(F32)
    return pl.pallas_call(
        functools.partial(_ffn_kernel, n_ctx_tokens=n_ctx_tokens, alpha=alpha, fc=_pick_chunk(ff, 512)),
        grid=(bsz, s // tm),
        in_specs=[tok, tok,
                  pl.BlockSpec((None, 6, d), lambda b, j: (b, 0, 0)),
                  pl.BlockSpec((None, 6, d), lambda b, j: (nrow - 1, 0, 0)),
                  full(w1), full(w3), full(w2),
                  pl.BlockSpec((1, d), lambda b, j: (0, 0)), pl.BlockSpec((1, d), lambda b, j: (0, 0))],
        out_specs=tok,
        out_shape=jax.ShapeDtypeStruct((bsz, s, d), F32),
        scratch_shapes=[pltpu.VMEM((tm, d), F32)],
        compiler_params=_cparams(("parallel", "parallel")),
        name="dense_ffn",
    )(fx, x1, mod, mod, w1, w3, w2, vec(ln_g), vec(ln_b))


MOE_TILE = 1024


def _route_plan(e1, e2, n_exp, tile):
    n = e1.shape[0]
    pair_e = jnp.stack([e1, e2], axis=1).reshape(-1)
    onehot = (pair_e[:, None] == jnp.arange(n_exp, dtype=jnp.int32)[None, :]).astype(jnp.int32)
    before = jnp.cumsum(onehot, axis=0) - onehot
    rank = jnp.sum(before * onehot, axis=1)
    counts = jnp.sum(onehot, axis=0)
    padded = (counts + tile - 1) // tile * tile
    ends = jnp.cumsum(padded)
    starts = ends - padded
    dest = starts[pair_e] + rank
    n_rows = (2 * n + n_exp * (tile - 1)) // tile * tile
    n_tiles = n_rows // tile
    row_token = jnp.zeros((n_rows,), jnp.int32).at[dest].set(jnp.arange(2 * n, dtype=jnp.int32) // 2)
    tile_start = jnp.arange(n_tiles, dtype=jnp.int32) * tile
    tile_expert = jnp.minimum(jnp.sum((tile_start[:, None] >= ends[None, :]).astype(jnp.int32), axis=1),
                              n_exp - 1)
    n_used = (ends[-1] // tile).astype(jnp.int32).reshape(1)
    return row_token, dest.reshape(n, 2), tile_expert, n_used


def _moe_gemm_kernel(te_ref, nused_ref, tok_ref, tokn_ref, x_hbm, w1_ref, w3_ref, w2_ref, y_ref,
                     xbuf_ref, sem, *, fc, tile):
    i = pl.program_id(0)
    f = pl.program_id(1)
    slot = i % 2
    n_used = nused_ref[0]

    def row_copy(t_ref, r, s):
        return pltpu.make_async_copy(x_hbm.at[pl.ds(t_ref[0, r], 1)], xbuf_ref.at[s, pl.ds(r, 1)], sem.at[s])

    def issue(t_ref, s):
        def body(r, carry):
            row_copy(t_ref, r, s).start()
            return carry
        lax.fori_loop(0, tile, body, 0)

    @pl.when(f == 0)
    def _():
        y_ref[...] = jnp.zeros_like(y_ref)

        @pl.when(i == 0)
        def _():
            issue(tok_ref, 0)

        @pl.when(i + 1 < n_used)
        def _():
            issue(tokn_ref, 1 - slot)

        @pl.when(i < n_used)
        def _():
            def drain(r, carry):
                row_copy(tok_ref, r, slot).wait()
                return carry
            lax.fori_loop(0, tile, drain, 0)

    @pl.when(i < n_used)
    def _():
        _swiglu_into(xbuf_ref[slot].astype(BF16), w1_ref, w3_ref, w2_ref, y_ref, fc)


def _moe_gemm(x_flat, row_token, tile_expert, n_used, w1, w3, w2, tile):
    n_rows = row_token.shape[0]
    n_tiles = n_rows // tile
    d = x_flat.shape[1]
    ff = w1.shape[2]
    nf = 2 if ff % (2 * LANES) == 0 else 1
    tf = ff // nf
    last = lambda i, nu: jnp.minimum(i, nu[0] - 1)
    fsel = lambda i, f, nu: jnp.where(i < nu[0], f, nf - 1)
    tok = lambda imap: pl.BlockSpec((None, 1, tile), imap, memory_space=pltpu.SMEM)
    tokens = row_token.reshape(n_tiles, 1, tile)
    return pl.pallas_call(
        functools.partial(_moe_gemm_kernel, fc=_pick_chunk(tf, 256), tile=tile),
        grid_spec=pltpu.PrefetchScalarGridSpec(
            num_scalar_prefetch=2,
            grid=(n_tiles, nf),
            in_specs=[tok(lambda i, f, te, nu: (i, 0, 0)),
                      tok(lambda i, f, te, nu: (jnp.minimum(i + 1, n_tiles - 1), 0, 0)),
                      pl.BlockSpec(memory_space=pl.ANY),
                      pl.BlockSpec((None, d, tf), lambda i, f, te, nu: (te[last(i, nu)], 0, fsel(i, f, nu))),
                      pl.BlockSpec((None, d, tf), lambda i, f, te, nu: (te[last(i, nu)], 0, fsel(i, f, nu))),
                      pl.BlockSpec((None, tf, d), lambda i, f, te, nu: (te[last(i, nu)], fsel(i, f, nu), 0))],
            out_specs=pl.BlockSpec((tile, d), lambda i, f, te, nu: (i, 0)),
            scratch_shapes=[pltpu.VMEM((2, tile, d), F32), pltpu.SemaphoreType.DMA((2,))]),
        out_shape=jax.ShapeDtypeStruct((n_rows, d), F32),
        compiler_params=_cparams(("arbitrary", "arbitrary")),
        name="moe_gemm",
    )(tile_expert, n_used, tokens, tokens, x_flat, w1, w3, w2)


def _moe_combine_kernel(pos_ref, posn_ref, route_ref, x1_ref, modx_ref, modz_ref, g2_ref, b2_ref, y_hbm,
                        o_ref, buf_ref, sem, *, nzb, nblk, alpha, tm):
    i = pl.program_id(0)
    slot = i % 2

    def row_copy(p_ref, r, k, s):
        return pltpu.make_async_copy(y_hbm.at[pl.ds(p_ref[0, k * tm + r], 1)],
                                     buf_ref.at[s, k, pl.ds(r, 1)], sem.at[s])

    def issue(p_ref, s):
        def body(r, carry):
            row_copy(p_ref, r, 0, s).start()
            row_copy(p_ref, r, 1, s).start()
            return carry
        lax.fori_loop(0, tm, body, 0)

    @pl.when(i == 0)
    def _():
        issue(pos_ref, 0)

    @pl.when(i + 1 < pl.num_programs(0))
    def _():
        issue(posn_ref, 1 - slot)

    def drain(r, carry):
        row_copy(pos_ref, r, 0, slot).wait()
        row_copy(pos_ref, r, 1, slot).wait()
        return carry

    lax.fori_loop(0, tm, drain, 0)
    route = route_ref[...]
    f = route[:, 0:1] * buf_ref[slot, 0] + route[:, 1:2] * buf_ref[slot, 1]
    is_ctx = (i % nblk) < nzb
    mod = jnp.where(is_ctx, modz_ref[...], modx_ref[...])
    o_ref[...] = _ln(alpha * x1_ref[...] + mod[5:6] * f) * g2_ref[...] + b2_ref[...]


def _moe_combine(ys, pos, route, x1, mod, ln_g, ln_b, n_ctx_tokens, alpha):
    bsz, s, d = x1.shape
    tm = TOKEN_BLOCK
    nblk = s // tm
    n_steps = bsz * nblk
    nrow = mod.shape[0]
    pos_steps = pos.reshape(n_steps, tm, 2).transpose(0, 2, 1).reshape(n_steps, 1, 2 * tm)
    tok = lambda width: pl.BlockSpec((tm, width), lambda i: (i, 0))
    smem = lambda imap: pl.BlockSpec((None, 1, 2 * tm), imap, memory_space=pltpu.SMEM)
    vec = lambda a: a.reshape(1, -1).astype(F32)
    out = pl.pallas_call(
        functools.partial(_moe_combine_kernel, nzb=n_ctx_tokens // tm, nblk=nblk, alpha=alpha, tm=tm),
        grid=(n_steps,),
        in_specs=[smem(lambda i: (i, 0, 0)),
                  smem(lambda i: (jnp.minimum(i + 1, n_steps - 1), 0, 0)),
                  tok(LANES), tok(d),
                  pl.BlockSpec((None, 6, d), lambda i: (i // nblk, 0, 0)),
                  pl.BlockSpec((None, 6, d), lambda i: (nrow - 1, 0, 0)),
                  pl.BlockSpec((1, d), lambda i: (0, 0)), pl.BlockSpec((1, d), lambda i: (0, 0)),
                  pl.BlockSpec(memory_space=pl.ANY)],
        out_specs=tok(d),
        out_shape=jax.ShapeDtypeStruct((bsz * s, d), F32),
        scratch_shapes=[pltpu.VMEM((2, 2, tm, d), F32), pltpu.SemaphoreType.DMA((2,))],
        compiler_params=_cparams(("arbitrary",)),
        name="moe_combine",
    )(pos_steps, pos_steps, route.reshape(bsz * s, LANES), x1.reshape(bsz * s, d), mod, mod,
      vec(ln_g), vec(ln_b), ys)
    return out.reshape(bsz, s, d)


def _moe(fx, x1, route, mod, w1, w3, w2, ln_g, ln_b, n_ctx_tokens, alpha):
    bsz, s, d = x1.shape
    n_exp = w1.shape[0]
    idx = route.reshape(bsz * s, LANES)[:, 2:4].astype(jnp.int32)
    row_token, pos, tile_expert, n_used = _route_plan(idx[:, 0], idx[:, 1], n_exp, MOE_TILE)
    ys = _moe_gemm(fx.reshape(bsz * s, d), row_token, tile_expert, n_used, w1, w3, w2, MOE_TILE)
    return _moe_combine(ys, pos, route, x1, mod, ln_g, ln_b, n_ctx_tokens, alpha)


def _pick_chunk(total, target):
    best = LANES
    for c in range(LANES, target + 1, LANES):
        if total % c == 0:
            best = c
    return best


def kernel(x, c, ctx, c_ctx, w_mod, b_mod, w_in, s5_lam_re, s5_lam_im, s5_log_step, s5_b_re, s5_b_im,
           s5_c_re, s5_c_im, s5_d, s5_w_glu, s5_b_glu, attn_sink, ret_log_gamma, w_out,
           ln1_g, ln1_b, ln2_g, ln2_b, ffn_w1, ffn_w3, ffn_w2, moe_router, moe_w1, moe_w3, moe_w2):
    bsz, t_len, d = x.shape
    n_ctx = ctx.shape[1]
    depth = w_in.shape[0]
    alpha = (2 * depth) ** 0.25
    s5_w = s5_d.shape[1]
    att_w = attn_sink.shape[1] * HEAD_DIM
    kv_w = att_w // ATT_REP
    ret_w = ret_log_gamma.shape[2] * HEAD_DIM
    sizes = (s5_w, att_w, kv_w, kv_w, ret_w, ret_w, ret_w, ret_w)
    assert sum(sizes) == w_in.shape[2] and s5_w + att_w + ret_w == w_out.shape[1]
    assert n_ctx % TOKEN_BLOCK == 0 and t_len % TOKEN_BLOCK == 0 and t_len >= 3 * ATT_BLOCK

    pad = (-(bsz + 1)) % 8
    cvec = jnp.concatenate([jnp.zeros((pad, d), F32), c_ctx[None].astype(F32)], axis=0)
    cvec = jnp.concatenate([c.astype(F32), cvec], axis=0)
    mod_all = _modulation(cvec, w_mod.astype(F32), b_mod.astype(F32)).reshape(depth, bsz + pad + 1, 6, d)

    tabs = _rope_tables(t_len)
    xz = jnp.concatenate([ctx, x], axis=1).astype(F32)
    for l in range(depth):
        need_ctx = l < depth - 1
        mod = mod_all[l]
        u, qa, ka, va, qr, kr, vr, gr = _inproj(xz, mod, w_in[l].astype(BF16), tabs, sizes, n_ctx)
        s5w = _s5_weights(s5_lam_re[l], s5_lam_im[l], s5_log_step[l], s5_b_re[l], s5_b_im[l],
                          s5_c_re[l], s5_c_im[l], S5_CHUNK)
        y_s5 = _s5_scan(u, s5w, n_ctx)
        o_att = _attention(qa, ka, va, attn_sink[l], n_ctx, need_ctx)
        o_ret = _retention(qr, kr, vr, gr, ret_log_gamma[l], n_ctx)
        i = l // 2
        router = None if l % 2 == 0 else moe_router[i]
        outs = _outproj(y_s5, u, o_att, o_ret, xz, mod, s5_d[l], s5_w_glu[l], s5_b_glu[l], w_out[l],
                        ln1_g[l], ln1_b[l], router, n_ctx, need_ctx, alpha)
        ctx_tokens = n_ctx if need_ctx else 0
        if l % 2 == 0:
            x1, fx = outs
            xz = _ffn(fx, x1, mod, ffn_w1[i].astype(BF16), ffn_w3[i].astype(BF16), ffn_w2[i].astype(BF16),
                      ln2_g[l], ln2_b[l], ctx_tokens, alpha)
        else:
            x1, fx, route = outs
            xz = _moe(fx, x1, route, mod, moe_w1[i].astype(BF16), moe_w3[i].astype(BF16),
                      moe_w2[i].astype(BF16), ln2_g[l], ln2_b[l], ctx_tokens, alpha)
    return xz if xz.shape[1] == t_len else xz[:, n_ctx:]
```

```python
import functools
import math

import jax
import jax.numpy as jnp
import numpy as np
from jax import lax
from jax.experimental import pallas as pl
from jax.experimental.pallas import tpu as pltpu

F32 = jnp.float32
BF16 = jnp.bfloat16

GRID_W = 64
HEAD_DIM = 64
S5_GROUP = 16
ATT_REP = 4
WINDOW = 128
ATT_BLOCK = 128
RET_CHUNK = 128
TOP_K = 2
LN_EPS = 1e-5
ROPE_BASE = 10000.0
NEG_INF = -1e30

LANES = 128
S5_CHUNK = LANES
TOKEN_BLOCK = 256
FFN_BLOCK = 768
VMEM_LIMIT = 56 * 1024 * 1024


def _cparams(sem):
    return pltpu.CompilerParams(dimension_semantics=sem, vmem_limit_bytes=VMEM_LIMIT)


def _dot(a, b):
    return jnp.dot(a, b, preferred_element_type=F32)


def _dot_nt(a, b):
    return lax.dot_general(a, b, (((1,), (1,)), ((), ())), preferred_element_type=F32)


def _dot_tn(a, b):
    return lax.dot_general(a, b, (((0,), (0,)), ((), ())), preferred_element_type=F32)


def _split_bf16(a):
    hi = a.astype(BF16)
    lo = (a - hi.astype(F32)).astype(BF16)
    return hi, lo


def _dot3(a, b):
    ah, al = _split_bf16(a)
    bh, bl = _split_bf16(b)
    return _dot(ah, bh) + (_dot(ah, bl) + _dot(al, bh))


def _ln(x):
    mu = jnp.mean(x, axis=-1, keepdims=True)
    xc = x - mu
    var = jnp.mean(xc * xc, axis=-1, keepdims=True)
    return xc * lax.rsqrt(var + LN_EPS)


def _silu(x):
    return x * (1.0 / (1.0 + jnp.exp(-x)))


def _sigmoid(x):
    return 1.0 / (1.0 + jnp.exp(-x))


def _gelu_tanh(x):
    c = math.sqrt(2.0 / math.pi)
    return 0.5 * x * (1.0 + jnp.tanh(c * (x + 0.044715 * (x * x * x))))


def _mod_kernel(c_ref, w_ref, b_ref, o_ref):
    o_ref[...] = _dot3(_silu(c_ref[...]), w_ref[...]) + b_ref[...]


def _modulation(cvec, w_mod, b_mod):
    depth, d, n = w_mod.shape
    rows = cvec.shape[0]
    tn = 1536
    return pl.pallas_call(
        _mod_kernel,
        grid=(depth, n // tn),
        in_specs=[pl.BlockSpec((rows, d), lambda l, j: (0, 0)),
                  pl.BlockSpec((None, d, tn), lambda l, j: (l, 0, j)),
                  pl.BlockSpec((None, 1, tn), lambda l, j: (l, 0, j))],
        out_specs=pl.BlockSpec((None, rows, tn), lambda l, j: (l, 0, j)),
        out_shape=jax.ShapeDtypeStruct((depth, rows, n), F32),
        compiler_params=_cparams(("parallel", "parallel")),
        name="modulation",
    )(cvec, w_mod, b_mod.reshape(depth, 1, n))


def _rope_slab(xs, cos, sa, sb, half):
    return xs * cos + pltpu.roll(xs, LANES - half, 1) * sa + pltpu.roll(xs, half, 1) * sb


def _inproj_kernel(x_ref, modx_ref, modz_ref, wu_ref, w_ref, ac_ref, asa_ref, asb_ref, rc_ref, rsa_ref, rsb_ref,
                   ut_ref, qa_ref, ka_ref, va_ref, qr_ref, kr_ref, vr_ref, gr_ref, *, nzb, sizes):
    is_ctx = pl.program_id(1) < nzb
    mod = jnp.where(is_ctx, modz_ref[...], modx_ref[...])
    h = (_ln(x_ref[...]) * (1.0 + mod[1:2]) + mod[0:1]).astype(BF16)
    ut_ref[...] = _dot_nt(wu_ref[...], h)
    p = _dot(h, w_ref[...])
    ac = jnp.where(is_ctx, 1.0, ac_ref[...])
    asa = jnp.where(is_ctx, 0.0, asa_ref[...])
    asb = jnp.where(is_ctx, 0.0, asb_ref[...])
    rc = jnp.where(is_ctx, 1.0, rc_ref[...])
    rsa = jnp.where(is_ctx, 0.0, rsa_ref[...])
    rsb = jnp.where(is_ctx, 0.0, rsb_ref[...])
    offs = np.concatenate([[0], np.cumsum(sizes)])
    scale = HEAD_DIM ** -0.5

    def cols(i):
        return p[:, int(offs[i]):int(offs[i + 1])]

    def rope_cols(i, cos, sa, sb, half, mul):
        blk = cols(i)
        slabs = [_rope_slab(blk[:, s:s + LANES], cos, sa, sb, half) for s in range(0, blk.shape[1], LANES)]
        out = slabs[0] if len(slabs) == 1 else jnp.concatenate(slabs, axis=1)
        return out * mul if mul != 1.0 else out

    qa_ref[...] = rope_cols(0, ac, asa, asb, HEAD_DIM // 4, scale)
    ka_ref[...] = rope_cols(1, ac, asa, asb, HEAD_DIM // 4, 1.0)
    va_ref[...] = cols(2)
    qr_ref[...] = rope_cols(3, rc, rsa, rsb, HEAD_DIM // 2, 1.0)
    kr_ref[...] = rope_cols(4, rc, rsa, rsb, HEAD_DIM // 2, scale)
    vr_ref[...] = cols(5)
    gr_ref[...] = cols(6)


def _inproj(xz, mod, w_in, tabs, sizes, n_ctx):
    bsz, s, d = xz.shape
    tm = TOKEN_BLOCK
    nzb = n_ctx // tm
    nrow = mod.shape[0]
    s5_w, rest = sizes[0], sizes[1:]
    w_u_t = w_in[:, :s5_w].T
    w_rest = w_in[:, s5_w:]
    tok = lambda width: pl.BlockSpec((None, tm, width), lambda b, j: (b, j, 0))
    tab = pl.BlockSpec((tm, LANES), lambda b, j: (jnp.maximum(j - nzb, 0), 0))
    return pl.pallas_call(
        functools.partial(_inproj_kernel, nzb=nzb, sizes=rest),
        grid=(bsz, s // tm),
        in_specs=[tok(d),
                  pl.BlockSpec((None, 6, d), lambda b, j: (b, 0, 0)),
                  pl.BlockSpec((None, 6, d), lambda b, j: (nrow - 1, 0, 0)),
                  pl.BlockSpec((s5_w, d), lambda b, j: (0, 0)),
                  pl.BlockSpec((d, sum(rest)), lambda b, j: (0, 0)),
                  tab, tab, tab, tab, tab, tab],
        out_specs=[pl.BlockSpec((None, None, s5_w, tm), lambda b, j: (j, b, 0, 0))] + [tok(w) for w in rest],
        out_shape=[jax.ShapeDtypeStruct((s // tm, bsz, s5_w, tm), F32)]
                  + [jax.ShapeDtypeStruct((bsz, s, w), F32) for w in rest],
        compiler_params=_cparams(("parallel", "parallel")),
        name="inproj",
    )(xz, mod, mod, w_u_t, w_rest, *tabs)


def _rope_tables(t_len):
    t = jnp.arange(t_len)
    rows = (t // GRID_W).astype(F32)
    cols = (t % GRID_W).astype(F32)
    pos = t.astype(F32)

    def angles(p, dim):
        inv_freq = ROPE_BASE ** (-jnp.arange(0, dim, 2, dtype=F32) / dim)
        return p[:, None] * inv_freq[None, :]

    def head_tables(angs):
        cos = jnp.concatenate([jnp.concatenate([jnp.cos(a), jnp.cos(a)], -1) for a in angs], -1)
        sa = jnp.concatenate([jnp.concatenate([-jnp.sin(a), jnp.zeros_like(a)], -1) for a in angs], -1)
        sb = jnp.concatenate([jnp.concatenate([jnp.zeros_like(a), jnp.sin(a)], -1) for a in angs], -1)
        rep = LANES // HEAD_DIM
        return tuple(jnp.tile(x, (1, rep)) for x in (cos, sa, sb))

    att = head_tables([angles(rows, HEAD_DIM // 2), angles(cols, HEAD_DIM // 2)])
    ret = head_tables([angles(pos, HEAD_DIM)])
    return att + ret


def _s5_kernel(u_ref, tab_ref, wsf_ref, wsb_ref, wrf_ref, wrb_ref, lam_ref, y_ref,
               lhs_ref, m_ref, acc_ref, cf_ref, cb_ref, pf_ref, pb_ref, *, bsz, nz_chunks):
    r_blk, ch, tm = u_ref.shape
    c = S5_CHUNK
    kpb = tm // c
    n_chunks = (r_blk // bsz) * kpb

    for i in range(ch):
        x = u_ref[:, i, :].astype(BF16)
        p, half = divmod(i, 2)
        for k in range(kpb):
            lhs_ref[p, k * r_blk:(k + 1) * r_blk, half * c:(half + 1) * c] = x[:, k * c:(k + 1) * c]
    acc_ref[...] = jnp.zeros_like(acc_ref)
    cf_ref[...] = jnp.zeros_like(cf_ref)
    cb_ref[...] = jnp.zeros_like(cb_ref)

    def pair(p, carry):
        for ih in range(2):
            for o in range(ch):
                tab = tab_ref[2 * p + ih, o]
                for r2 in range(c // 16):
                    lo = c - 16 * r2
                    rows = jnp.concatenate([tab[:, lo:lo + c], tab[:, lo - 8:lo - 8 + c]], axis=0)
                    m_ref[ih * c + 16 * r2:ih * c + 16 * r2 + 16, o * c:(o + 1) * c] = rows.astype(BF16)
        lhs = lhs_ref[p]
        acc_ref[...] += _dot(lhs, m_ref[...])
        cf_ref[...] += _dot(lhs, wsf_ref[p])
        cb_ref[...] += _dot(lhs, wsb_ref[p])
        return carry

    lax.fori_loop(0, ch // 2, pair, 0)
    lam = lam_ref[...]
    nst = lam.shape[1] // 2

    def advance(s, a, bc, drive):
        return s * a + pltpu.roll(s, nst, 1) * bc + drive

    def sweep(order, c_ref, p_ref, a, bc):
        s = jnp.zeros((bsz, lam.shape[1]), F32)
        for n in order:
            j, k = divmod(n, kpb)
            rows = slice(k * r_blk + j * bsz, k * r_blk + (j + 1) * bsz)
            p_ref[rows, :] = s
            s = advance(s, a, bc, c_ref[rows, :])

    sweep(list(range(n_chunks)), cf_ref, pf_ref, lam[0:1], lam[1:2])
    order_b = list(range(nz_chunks - 1, -1, -1)) + list(range(n_chunks - 1, nz_chunks - 1, -1))
    sweep(order_b, cb_ref, pb_ref, lam[2:3], lam[3:4])
    acc_ref[...] += (_dot(pf_ref[...].astype(BF16), wrf_ref[...])
                     + _dot(pb_ref[...].astype(BF16), wrb_ref[...]))
    for o in range(ch):
        for k in range(kpb):
            y_ref[:, o, k * c:(k + 1) * c] = acc_ref[k * r_blk:(k + 1) * r_blk, o * c:(o + 1) * c]


def _s5_weights(lam_re, lam_im, log_step, b_re, b_im, c_re, c_im, chunk):
    hp = lax.Precision.HIGHEST
    lam = lax.complex(lam_re.astype(F32), lam_im.astype(F32))
    lam_dt = lam * jnp.exp(log_step.astype(F32))[..., None]
    lam_bar = jnp.exp(lam_dt)
    b_bar = lax.complex(b_re.astype(F32), b_im.astype(F32)) * ((lam_bar - 1.0) / lam)[..., None]
    c_mat = lax.complex(c_re.astype(F32), c_im.astype(F32))
    g, n = lam.shape[1], lam.shape[2]
    ch = b_bar.shape[-1]
    steps = jnp.arange(chunk + 1, dtype=F32)
    pw = jnp.exp(steps[None, :, None, None] * lam_dt[:, None])
    kern = jnp.einsum('zgon,zdgn,zgni->zgdoi', c_mat, pw[:, :chunk], b_bar, precision=hp).real
    zero_lag = kern[0, :, 0] + kern[1, :, 0]
    lag_table = jnp.concatenate([jnp.zeros_like(zero_lag)[:, None], kern[1, :, :0:-1], zero_lag[:, None],
                                 kern[0, :, 1:]], axis=1).transpose(0, 3, 2, 1)
    tab = jnp.stack([jnp.roll(lag_table, q, axis=-1) for q in range(8)], axis=3)

    def state_in(pw_s, b_dir):
        w = pw_s[:, :, :, None] * b_dir[None]
        w = jnp.concatenate([w.real, w.imag], axis=2)
        return w.transpose(1, 3, 0, 2).reshape(g, ch // 2, 2 * chunk, 2 * n)

    def state_out(pw_t, c_dir):
        w = c_dir[None] * pw_t[:, :, None, :]
        w = jnp.concatenate([w.real, -w.imag], axis=3)
        return w.transpose(1, 3, 2, 0).reshape(g, 2 * n, ch * chunk)

    wsf = state_in(pw[0, chunk - 1::-1][:chunk], b_bar[0])
    wsb = state_in(pw[1, :chunk], b_bar[1])
    wrf = state_out(pw[0, 1:chunk + 1], c_mat[0])
    wrb = state_out(pw[1, chunk:0:-1], c_mat[1])
    lam_c = pw[:, chunk]
    rows = []
    for z in range(2):
        rows.append(jnp.concatenate([lam_c[z].real, lam_c[z].real], -1))
        rows.append(jnp.concatenate([-lam_c[z].imag, lam_c[z].imag], -1))
    lam_rows = jnp.stack(rows + [jnp.zeros_like(rows[0])] * 4, axis=1)
    return tab, wsf.astype(BF16), wsb.astype(BF16), wrf.astype(BF16), wrb.astype(BF16), lam_rows


def _s5_scan(ut, weights, n_ctx):
    nblk, bsz, width, tm = ut.shape
    tab, wsf, wsb, wrf, wrb, lam_rows = weights
    g, ch = tab.shape[0], tab.shape[1]
    c = S5_CHUNK
    nst2 = wsf.shape[3]
    r_blk = nblk * bsz
    rows = r_blk * (tm // c)
    grp = lambda *shape: pl.BlockSpec((None,) + shape, lambda i: (i,) + (0,) * len(shape))
    tok = pl.BlockSpec((r_blk, ch, tm), lambda i: (0, i, 0))
    y = pl.pallas_call(
        functools.partial(_s5_kernel, bsz=bsz, nz_chunks=n_ctx // c),
        grid=(g,),
        in_specs=[tok, grp(ch, ch, 8, 2 * c), grp(ch // 2, 2 * c, nst2), grp(ch // 2, 2 * c, nst2),
                  grp(nst2, ch * c), grp(nst2, ch * c), grp(8, nst2)],
        out_specs=tok,
        out_shape=jax.ShapeDtypeStruct((r_blk, width, tm), F32),
        scratch_shapes=[pltpu.VMEM((ch // 2, rows, 2 * c), BF16),
                        pltpu.VMEM((2 * c, ch * c), BF16),
                        pltpu.VMEM((rows, ch * c), F32)] + [pltpu.VMEM((rows, nst2), F32)] * 4,
        compiler_params=_cparams(("parallel",)),
        name="s5_scan",
    )(ut.reshape(r_blk, width, tm), tab, wsf, wsb, wrf, wrb, lam_rows)
    return y.reshape(nblk, bsz, width, tm)


def _attn_kernel(sink_ref, q_ref, k_ref, v_ref, o_ref, *, n_ctx, t_len, q_off, kv_heads):
    qi = pl.program_id(1) + q_off
    nzb = n_ctx // ATT_BLOCK
    band = 3 * ATT_BLOCK
    q = q_ref[...]
    kc = k_ref[0:n_ctx, :].astype(BF16)
    vc = v_ref[0:n_ctx, :].astype(BF16)

    def head_cols(a, i):
        return a[:, i * HEAD_DIM:(i + 1) * HEAD_DIM]

    def put(h, o):
        o_ref[:, h * HEAD_DIM:(h + 1) * HEAD_DIM] = o

    @pl.when(qi < nzb)
    def _():
        for h in range(kv_heads * ATT_REP):
            g = h // ATT_REP
            sink = sink_ref[h]
            s = _dot_nt(head_cols(q, h).astype(BF16), head_cols(kc, g))
            m = jnp.maximum(jnp.max(s, axis=1, keepdims=True), sink)
            e = jnp.exp(s - m)
            den = jnp.sum(e, axis=1, keepdims=True) + jnp.exp(sink - m)
            put(h, _dot(e.astype(BF16), head_cols(vc, g)) / den)

    @pl.when(qi >= nzb)
    def _():
        n = qi - nzb
        start = jnp.clip((n - 1) * ATT_BLOCK, 0, t_len - band)
        kl = k_ref[pl.ds(pl.multiple_of(n_ctx + start, ATT_BLOCK), band), :].astype(BF16)
        vl = v_ref[pl.ds(pl.multiple_of(n_ctx + start, ATT_BLOCK), band), :].astype(BF16)
        q_pos = n * ATT_BLOCK + lax.broadcasted_iota(jnp.int32, (ATT_BLOCK, band), 0)
        k_pos = start + lax.broadcasted_iota(jnp.int32, (ATT_BLOCK, band), 1)
        valid = jnp.abs(k_pos - q_pos) <= WINDOW
        for h in range(kv_heads * ATT_REP):
            g = h // ATT_REP
            sink = sink_ref[h]
            qh = head_cols(q, h).astype(BF16)
            s_loc = jnp.where(valid, _dot_nt(qh, head_cols(kl, g)), NEG_INF)
            s_ctx = _dot_nt(qh, head_cols(kc, g))
            m = jnp.maximum(jnp.maximum(jnp.max(s_loc, axis=1, keepdims=True),
                                        jnp.max(s_ctx, axis=1, keepdims=True)), sink)
            e_loc = jnp.exp(s_loc - m)
            e_ctx = jnp.exp(s_ctx - m)
            den = (jnp.sum(e_loc, axis=1, keepdims=True) + jnp.sum(e_ctx, axis=1, keepdims=True)
                   + jnp.exp(sink - m))
            o = _dot(e_loc.astype(BF16), head_cols(vl, g)) + _dot(e_ctx.astype(BF16), head_cols(vc, g))
            put(h, o / den)


def _attention(qa, ka, va, sink, n_ctx, need_ctx):
    bsz, s, qw = qa.shape
    kvw = ka.shape[2]
    t_len = s - n_ctx
    q_off = 0 if need_ctx else n_ctx // ATT_BLOCK
    nq = s // ATT_BLOCK - q_off
    return pl.pallas_call(
        functools.partial(_attn_kernel, n_ctx=n_ctx, t_len=t_len, q_off=q_off, kv_heads=kvw // HEAD_DIM),
        grid=(bsz, nq),
        in_specs=[pl.BlockSpec(memory_space=pltpu.SMEM),
                  pl.BlockSpec((None, ATT_BLOCK, qw), lambda b, j: (b, j + q_off, 0)),
                  pl.BlockSpec((None, s, kvw), lambda b, j: (b, 0, 0)),
                  pl.BlockSpec((None, s, kvw), lambda b, j: (b, 0, 0))],
        out_specs=pl.BlockSpec((None, ATT_BLOCK, qw), lambda b, j: (b, j, 0)),
        out_shape=jax.ShapeDtypeStruct((bsz, nq * ATT_BLOCK, qw), F32),
        compiler_params=_cparams(("parallel", "arbitrary")),
        name="window_attention",
    )(sink.astype(F32), qa, ka, va)


def _ret_kernel(lg_ref, q_ref, k_ref, v_ref, g_ref, o_ref, acc_ref, dec_ref, sf_ref, sb_ref,
                *, n_chunks, nz_chunks, heads):
    c = RET_CHUNK
    ii = lax.broadcasted_iota(jnp.int32, (c, c), 0)
    jj = lax.broadcasted_iota(jnp.int32, (c, c), 1)
    diff = (ii - jj).astype(F32)
    idx = lax.broadcasted_iota(jnp.int32, (c, 1), 0).astype(F32)
    for h in range(heads):
        lgf, lgb = lg_ref[0, h], lg_ref[1, h]
        dec_ref[h] = jnp.where(diff >= 0, jnp.exp(lgf * jnp.maximum(diff, 0.0)),
                               jnp.exp(lgb * jnp.maximum(-diff, 0.0)))
    sf_ref[...] = jnp.zeros_like(sf_ref)
    sb_ref[...] = jnp.zeros_like(sb_ref)

    def hcols(a, h):
        return a[:, h * HEAD_DIM:(h + 1) * HEAD_DIM]

    chunk_len = jnp.full((1, HEAD_DIM), float(c), F32)

    def fwd(n, carry):
        rows = pl.ds(pl.multiple_of(n * c, c), c)
        q, k, v = q_ref[rows, :], k_ref[rows, :], v_ref[rows, :]
        for h in range(heads):
            lgf = lg_ref[0, h]
            cols = slice(h * HEAD_DIM, (h + 1) * HEAD_DIM)
            qh, kh, vh = hcols(q, h), hcols(k, h), hcols(v, h).astype(BF16)
            scores = _dot_nt(qh.astype(BF16), kh.astype(BF16)) * dec_ref[h]
            s_prev = sf_ref[h]
            o = _dot(scores.astype(BF16), vh)
            o = o + _dot((qh * jnp.exp(lgf * (idx + 1.0))).astype(BF16), s_prev.astype(BF16))
            kw = kh * jnp.exp(lgf * (c - 1.0 - idx))
            sf_ref[h] = jnp.exp(lgf * chunk_len) * s_prev + _dot_tn(kw.astype(BF16), vh)
            acc_ref[rows, cols] = o
        return carry

    lax.fori_loop(0, n_chunks, fwd, 0)

    def bwd(i, carry):
        n = jnp.where(i < nz_chunks, nz_chunks - 1 - i, n_chunks - 1 - i + nz_chunks)
        rows = pl.ds(pl.multiple_of(n * c, c), c)
        q, k, v = q_ref[rows, :], k_ref[rows, :], v_ref[rows, :]
        acc = acc_ref[rows, :]
        gate = _silu(g_ref[rows, :])
        for h in range(heads):
            lgb = lg_ref[1, h]
            cols = slice(h * HEAD_DIM, (h + 1) * HEAD_DIM)
            qh, kh, vh = hcols(q, h), hcols(k, h), hcols(v, h).astype(BF16)
            s_prev = sb_ref[h]
            o = hcols(acc, h) + _dot((qh * jnp.exp(lgb * (c - idx))).astype(BF16), s_prev.astype(BF16))
            kw = kh * jnp.exp(lgb * idx)
            sb_ref[h] = jnp.exp(lgb * chunk_len) * s_prev + _dot_tn(kw.astype(BF16), vh)
            o_ref[rows, cols] = _ln(o) * hcols(gate, h)
        return carry

    lax.fori_loop(0, n_chunks, bwd, 0)


def _retention(qr, kr, vr, gr, log_gamma, n_ctx):
    bsz, s, w = qr.shape
    heads = w // HEAD_DIM
    n_chunks = s // RET_CHUNK
    tok = pl.BlockSpec((None, s, w), lambda b: (b, 0, 0))
    return pl.pallas_call(
        functools.partial(_ret_kernel, n_chunks=n_chunks, nz_chunks=n_ctx // RET_CHUNK, heads=heads),
        grid=(bsz,),
        in_specs=[pl.BlockSpec(memory_space=pltpu.SMEM), tok, tok, tok, tok],
        out_specs=tok,
        out_shape=jax.ShapeDtypeStruct((bsz, s, w), F32),
        scratch_shapes=[pltpu.VMEM((s, w), F32),
                        pltpu.VMEM((heads, RET_CHUNK, RET_CHUNK), F32),
                        pltpu.VMEM((heads, HEAD_DIM, HEAD_DIM), F32),
                        pltpu.VMEM((heads, HEAD_DIM, HEAD_DIM), F32)],
        compiler_params=_cparams(("parallel",)),
        name="retention",
    )(log_gamma.astype(F32), qr, kr, vr, gr)


def _outproj_kernel(*refs, nzb, alpha, with_router, w_s5, w_att, n_exp):
    if with_router:
        (y_ref, u_ref, a_ref, r_ref, x_ref, modx_ref, modz_ref, d_ref, wg_ref, bg_ref, wo_ref,
         g1_ref, b1_ref, rt_ref, x1_ref, fx_ref, gate_ref) = refs
    else:
        (y_ref, u_ref, a_ref, r_ref, x_ref, modx_ref, modz_ref, d_ref, wg_ref, bg_ref, wo_ref,
         g1_ref, b1_ref, x1_ref, fx_ref) = refs
    is_ctx = pl.program_id(1) < nzb
    mod = jnp.where(is_ctx, modz_ref[...], modx_ref[...])
    g = _gelu_tanh(y_ref[...] + d_ref[...] * u_ref[...])
    s5 = g * _sigmoid(_dot(wg_ref[...], g.astype(BF16)) + bg_ref[...])
    mix = (_dot_tn(s5.astype(BF16), wo_ref[0:w_s5, :])
           + _dot(a_ref[...].astype(BF16), wo_ref[w_s5:w_s5 + w_att, :])
           + _dot(r_ref[...].astype(BF16), wo_ref[w_s5 + w_att:, :]))
    x1 = _ln(alpha * x_ref[...] + mod[2:3] * mix) * g1_ref[...] + b1_ref[...]
    x1_ref[...] = x1
    fx = _ln(x1) * (1.0 + mod[4:5]) + mod[3:4]
    fx_ref[...] = fx.astype(fx_ref.dtype)
    if with_router:
        lane = lax.broadcasted_iota(jnp.int32, (fx.shape[0], LANES), 1)
        logits = jnp.where(lane < n_exp, _dot3(fx, rt_ref[...]), -jnp.inf)
        m1 = jnp.max(logits, axis=1, keepdims=True)
        i1 = jnp.min(jnp.where(logits == m1, lane, LANES), axis=1, keepdims=True)
        rest = jnp.where(lane == i1, -jnp.inf, logits)
        m2 = jnp.max(rest, axis=1, keepdims=True)
        i2 = jnp.min(jnp.where(rest == m2, lane, LANES), axis=1, keepdims=True)
        e2 = jnp.exp(m2 - m1)
        den = 1.0 + e2
        route = jnp.where(lane == 0, 1.0 / den, jnp.where(lane == 1, e2 / den, 0.0))
        route = jnp.where(lane == 2, i1.astype(F32), jnp.where(lane == 3, i2.astype(F32), route))
        gate_ref[...] = route


def _outproj(y_s5, u, o_att, o_ret, xz, mod, s5_d, w_glu, b_glu, w_out, ln_g, ln_b, router, n_ctx, need_ctx,
             alpha):
    bsz, s, d = xz.shape
    tm = TOKEN_BLOCK
    nzb = n_ctx // tm
    off = 0 if need_ctx else nzb
    nblk = s // tm - off
    s_out = nblk * tm
    nrow = mod.shape[0]
    w_s5, w_att = y_s5.shape[2], o_att.shape[2]
    att_off = off if o_att.shape[1] == s else 0

    def tok(width, shift):
        return pl.BlockSpec((None, tm, width), lambda b, j: (b, j + shift, 0))

    def full(a):
        return pl.BlockSpec(a.shape, lambda b, j: (0,) * a.ndim)

    chan = pl.BlockSpec((None, None, w_s5, tm), lambda b, j: (j + off, b, 0, 0))
    vec = lambda a: a.reshape(1, -1).astype(F32)
    col = lambda a: a.reshape(-1, 1).astype(F32)
    consts = [col(s5_d), w_glu.T.astype(BF16), col(b_glu), w_out.astype(BF16), vec(ln_g), vec(ln_b)]
    in_specs = [chan, chan, tok(w_att, att_off), tok(o_ret.shape[2], off), tok(d, off),
                pl.BlockSpec((None, 6, d), lambda b, j: (b, 0, 0)),
                pl.BlockSpec((None, 6, d), lambda b, j: (nrow - 1, 0, 0))] + [full(a) for a in consts]
    out_specs = [tok(d, 0), tok(d, 0)]
    fx_dtype = BF16 if router is None else F32
    out_shape = [jax.ShapeDtypeStruct((bsz, s_out, d), F32), jax.ShapeDtypeStruct((bsz, s_out, d), fx_dtype)]
    args = [y_s5, u, o_att, o_ret, xz, mod, mod] + consts
    with_router = router is not None
    n_exp = 0
    if with_router:
        n_exp = router.shape[1]
        router_pad = jnp.pad(router.astype(F32), ((0, 0), (0, LANES - n_exp)))
        args.append(router_pad)
        in_specs.append(full(router_pad))
        out_specs.append(tok(LANES, 0))
        out_shape.append(jax.ShapeDtypeStruct((bsz, s_out, LANES), F32))
    return pl.pallas_call(
        functools.partial(_outproj_kernel, nzb=nzb - off, alpha=alpha, with_router=with_router,
                          w_s5=w_s5, w_att=w_att, n_exp=n_exp),
        grid=(bsz, nblk),
        in_specs=in_specs,
        out_specs=out_specs,
        out_shape=out_shape,
        compiler_params=_cparams(("parallel", "parallel")),
        name="outproj",
    )(*args)


def _swiglu_into(x_bf16, w1_ref, w3_ref, w2_ref, acc_ref, fc):
    for s in range(0, w1_ref.shape[1], fc):
        h1 = _dot(x_bf16, w1_ref[:, s:s + fc])
        h3 = _dot(x_bf16, w3_ref[:, s:s + fc])
        acc_ref[...] += _dot((_silu(h1) * h3).astype(BF16), w2_ref[s:s + fc, :])


def _ffn_kernel(fx_ref, x1_ref, modx_ref, modz_ref, w1_ref, w3_ref, w2_ref, g2_ref, b2_ref,
                o_ref, acc_ref, *, n_ctx_tokens, alpha, fc):
    tm = acc_ref.shape[0]
    acc_ref[...] = jnp.zeros_like(acc_ref)
    _swiglu_into(fx_ref[...], w1_ref, w3_ref, w2_ref, acc_ref, fc)
    row = pl.program_id(1) * tm + lax.broadcasted_iota(jnp.int32, (tm, 1), 0)
    gate = jnp.where(row < n_ctx_tokens, modz_ref[5:6, :], modx_ref[5:6, :])
    o_ref[...] = _ln(alpha * x1_ref[...] + gate * acc_ref[...]) * g2_ref[...] + b2_ref[...]


def _ffn(fx, x1, mod, w1, w3, w2, ln_g, ln_b, n_ctx_tokens, alpha):
    bsz, s, d = x1.shape
    ff = w1.shape[1]
    tm = _pick_chunk(s, FFN_BLOCK)
    nrow = mod.shape[0]
    tok = pl.BlockSpec((None, tm, d), lambda b, j: (b, j, 0))
    full = lambda a: pl.BlockSpec(a.shape, lambda b, j: (0,) * a.ndim, pipeline_mode=pl.Buffered(1))
    vec = lambda a: a.reshape(1, -1).astype(F32)
    return pl.pallas_call(
        functools.partial(_ffn_kernel, n_ctx_tokens=n_ctx_tokens, alpha=alpha, fc=_pick_chunk(ff, 512)),
        grid=(bsz, s // tm),
        in_specs=[tok, tok,
                  pl.BlockSpec((None, 6, d), lambda b, j: (b, 0, 0)),
                  pl.BlockSpec((None, 6, d), lambda b, j: (nrow - 1, 0, 0)),
                  full(w1), full(w3), full(w2),
                  pl.BlockSpec((1, d), lambda b, j: (0, 0)), pl.BlockSpec((1, d), lambda b, j: (0, 0))],
        out_specs=tok,
        out_shape=jax.ShapeDtypeStruct((bsz, s, d), F32),
        scratch_shapes=[pltpu.VMEM((tm, d), F32)],
        compiler_params=_cparams(("parallel", "parallel")),
        name="dense_ffn",
    )(fx, x1, mod, mod, w1, w3, w2, vec(ln_g), vec(ln_b))


MOE_TILE = 1024


def _route_plan(e1, e2, n_exp, tile):
    n = e1.shape[0]
    pair_e = jnp.stack([e1, e2], axis=1).reshape(-1)
    onehot = (pair_e[:, None] == jnp.arange(n_exp, dtype=jnp.int32)[None, :]).astype(jnp.int32)
    before = jnp.cumsum(onehot, axis=0) - onehot
    rank = jnp.sum(before * onehot, axis=1)
    counts = jnp.sum(onehot, axis=0)
    padded = (counts + tile - 1) // tile * tile
    ends = jnp.cumsum(padded)
    starts = ends - padded
    dest = starts[pair_e] + rank
    n_rows = (2 * n + n_exp * (tile - 1)) // tile * tile
    n_tiles = n_rows // tile
    row_token = jnp.zeros((n_rows,), jnp.int32).at[dest].set(jnp.arange(2 * n, dtype=jnp.int32) // 2)
    tile_start = jnp.arange(n_tiles, dtype=jnp.int32) * tile
    tile_expert = jnp.minimum(jnp.sum((tile_start[:, None] >= ends[None, :]).astype(jnp.int32), axis=1),
                              n_exp - 1)
    n_used = (ends[-1] // tile).astype(jnp.int32).reshape(1)
    return row_token, dest.reshape(n, 2), tile_expert, n_used


def _moe_gemm_kernel(te_ref, nused_ref, tok_ref, tokn_ref, x_hbm, w1_ref, w3_ref, w2_ref, y_ref,
                     xbuf_ref, sem, *, fc, tile):
    i = pl.program_id(0)
    f = pl.program_id(1)
    slot = i % 2
    n_used = nused_ref[0]

    def row_copy(t_ref, r, s):
        return pltpu.make_async_copy(x_hbm.at[pl.ds(t_ref[0, r], 1)], xbuf_ref.at[s, pl.ds(r, 1)], sem.at[s])

    def issue(t_ref, s):
        def body(r, carry):
            row_copy(t_ref, r, s).start()
            return carry
        lax.fori_loop(0, tile, body, 0, unroll=8)

    @pl.when(f == 0)
    def _():
        y_ref[...] = jnp.zeros_like(y_ref)

        @pl.when(i == 0)
        def _():
            issue(tok_ref, 0)

        @pl.when(i + 1 < n_used)
        def _():
            issue(tokn_ref, 1 - slot)

        @pl.when(i < n_used)
        def _():
            pltpu.make_async_copy(x_hbm.at[pl.ds(0, tile)], xbuf_ref.at[slot], sem.at[slot]).wait()

    @pl.when(i < n_used)
    def _():
        _swiglu_into(xbuf_ref[slot].astype(BF16), w1_ref, w3_ref, w2_ref, y_ref, fc)


def _moe_gemm(x_flat, row_token, tile_expert, n_used, w1, w3, w2, tile):
    n_rows = row_token.shape[0]
    n_tiles = n_rows // tile
    d = x_flat.shape[1]
    ff = w1.shape[2]
    nf = 2 if ff % (2 * LANES) == 0 else 1
    tf = ff // nf
    last = lambda i, nu: jnp.minimum(i, nu[0] - 1)
    fsel = lambda i, f, nu: jnp.where(i < nu[0], f, nf - 1)
    tok = lambda imap: pl.BlockSpec((None, 1, tile), imap, memory_space=pltpu.SMEM)
    tokens = row_token.reshape(n_tiles, 1, tile)
    return pl.pallas_call(
        functools.partial(_moe_gemm_kernel, fc=_pick_chunk(tf, 256), tile=tile),
        grid_spec=pltpu.PrefetchScalarGridSpec(
            num_scalar_prefetch=2,
            grid=(n_tiles, nf),
            in_specs=[tok(lambda i, f, te, nu: (i, 0, 0)),
                      tok(lambda i, f, te, nu: (jnp.minimum(i + 1, n_tiles - 1), 0, 0)),
                      pl.BlockSpec(memory_space=pl.ANY),
                      pl.BlockSpec((None, d, tf), lambda i, f, te, nu: (te[last(i, nu)], 0, fsel(i, f, nu))),
                      pl.BlockSpec((None, d, tf), lambda i, f, te, nu: (te[last(i, nu)], 0, fsel(i, f, nu))),
                      pl.BlockSpec((None, tf, d), lambda i, f, te, nu: (te[last(i, nu)], fsel(i, f, nu), 0))],
            out_specs=pl.BlockSpec((tile, d), lambda i, f, te, nu: (i, 0)),
            scratch_shapes=[pltpu.VMEM((2, tile, d), F32), pltpu.SemaphoreType.DMA((2,))]),
        out_shape=jax.ShapeDtypeStruct((n_rows, d), F32),
        compiler_params=_cparams(("arbitrary", "arbitrary")),
        name="moe_gemm",
    )(tile_expert, n_used, tokens, tokens, x_flat, w1, w3, w2)


def _moe_combine_kernel(pos_ref, posn_ref, route_ref, x1_ref, modx_ref, modz_ref, g2_ref, b2_ref, y_hbm,
                        o_ref, buf_ref, sem, *, nzb, nblk, alpha, tm):
    i = pl.program_id(0)
    slot = i % 2

    def row_copy(p_ref, r, k, s):
        return pltpu.make_async_copy(y_hbm.at[pl.ds(p_ref[0, k * tm + r], 1)],
                                     buf_ref.at[s, pl.ds(k * tm + r, 1)], sem.at[s])

    def issue(p_ref, s):
        def body(r, carry):
            row_copy(p_ref, r, 0, s).start()
            row_copy(p_ref, r, 1, s).start()
            return carry
        lax.fori_loop(0, tm, body, 0, unroll=8)

    @pl.when(i == 0)
    def _():
        issue(pos_ref, 0)

    @pl.when(i + 1 < pl.num_programs(0))
    def _():
        issue(posn_ref, 1 - slot)

    pltpu.make_async_copy(y_hbm.at[pl.ds(0, 2 * tm)], buf_ref.at[slot], sem.at[slot]).wait()
    route = route_ref[...]
    f = route[:, 0:1] * buf_ref[slot, 0:tm] + route[:, 1:2] * buf_ref[slot, tm:2 * tm]
    is_ctx = (i % nblk) < nzb
    mod = jnp.where(is_ctx, modz_ref[...], modx_ref[...])
    o_ref[...] = _ln(alpha * x1_ref[...] + mod[5:6] * f) * g2_ref[...] + b2_ref[...]


def _moe_combine(ys, pos, route, x1, mod, ln_g, ln_b, n_ctx_tokens, alpha):
    bsz, s, d = x1.shape
    tm = TOKEN_BLOCK
    nblk = s // tm
    n_steps = bsz * nblk
    nrow = mod.shape[0]
    pos_steps = pos.reshape(n_steps, tm, 2).transpose(0, 2, 1).reshape(n_steps, 1, 2 * tm)
    tok = lambda width: pl.BlockSpec((tm, width), lambda i: (i, 0))
    smem = lambda imap: pl.BlockSpec((None, 1, 2 * tm), imap, memory_space=pltpu.SMEM)
    vec = lambda a: a.reshape(1, -1).astype(F32)
    out = pl.pallas_call(
        functools.partial(_moe_combine_kernel, nzb=n_ctx_tokens // tm, nblk=nblk, alpha=alpha, tm=tm),
        grid=(n_steps,),
        in_specs=[smem(lambda i: (i, 0, 0)),
                  smem(lambda i: (jnp.minimum(i + 1, n_steps - 1), 0, 0)),
                  tok(LANES), tok(d),
                  pl.BlockSpec((None, 6, d), lambda i: (i // nblk, 0, 0)),
                  pl.BlockSpec((None, 6, d), lambda i: (nrow - 1, 0, 0)),
                  pl.BlockSpec((1, d), lambda i: (0, 0)), pl.BlockSpec((1, d), lambda i: (0, 0)),
                  pl.BlockSpec(memory_space=pl.ANY)],
        out_specs=tok(d),
        out_shape=jax.ShapeDtypeStruct((bsz * s, d), F32),
        scratch_shapes=[pltpu.VMEM((2, 2 * tm, d), F32), pltpu.SemaphoreType.DMA((2,))],
        compiler_params=_cparams(("arbitrary",)),
        name="moe_combine",
    )(pos_steps, pos_steps, route.reshape(bsz * s, LANES), x1.reshape(bsz * s, d), mod, mod,
      vec(ln_g), vec(ln_b), ys)
    return out.reshape(bsz, s, d)


def _moe(fx, x1, route, mod, w1, w3, w2, ln_g, ln_b, n_ctx_tokens, alpha):
    bsz, s, d = x1.shape
    n_exp = w1.shape[0]
    idx = route.reshape(bsz * s, LANES)[:, 2:4].astype(jnp.int32)
    row_token, pos, tile_expert, n_used = _route_plan(idx[:, 0], idx[:, 1], n_exp, MOE_TILE)
    ys = _moe_gemm(fx.reshape(bsz * s, d), row_token, tile_expert, n_used, w1, w3, w2, MOE_TILE)
    return _moe_combine(ys, pos, route, x1, mod, ln_g, ln_b, n_ctx_tokens, alpha)


def _pick_chunk(total, target):
    best = LANES
    for c in range(LANES, target + 1, LANES):
        if total % c == 0:
            best = c
    return best


def kernel(x, c, ctx, c_ctx, w_mod, b_mod, w_in, s5_lam_re, s5_lam_im, s5_log_step, s5_b_re, s5_b_im,
           s5_c_re, s5_c_im, s5_d, s5_w_glu, s5_b_glu, attn_sink, ret_log_gamma, w_out,
           ln1_g, ln1_b, ln2_g, ln2_b, ffn_w1, ffn_w3, ffn_w2, moe_router, moe_w1, moe_w3, moe_w2):
    bsz, t_len, d = x.shape
    n_ctx = ctx.shape[1]
    depth = w_in.shape[0]
    alpha = (2 * depth) ** 0.25
    s5_w = s5_d.shape[1]
    att_w = attn_sink.shape[1] * HEAD_DIM
    kv_w = att_w // ATT_REP
    ret_w = ret_log_gamma.shape[2] * HEAD_DIM
    sizes = (s5_w, att_w, kv_w, kv_w, ret_w, ret_w, ret_w, ret_w)
    assert sum(sizes) == w_in.shape[2] and s5_w + att_w + ret_w == w_out.shape[1]
    assert n_ctx % TOKEN_BLOCK == 0 and t_len % TOKEN_BLOCK == 0 and t_len >= 3 * ATT_BLOCK

    pad = (-(bsz + 1)) % 8
    cvec = jnp.concatenate([jnp.zeros((pad, d), F32), c_ctx[None].astype(F32)], axis=0)
    cvec = jnp.concatenate([c.astype(F32), cvec], axis=0)
    mod_all = _modulation(cvec, w_mod.astype(F32), b_mod.astype(F32)).reshape(depth, bsz + pad + 1, 6, d)

    tabs = _rope_tables(t_len)
    xz = jnp.concatenate([ctx, x], axis=1).astype(F32)
    for l in range(depth):
        need_ctx = l < depth - 1
        mod = mod_all[l]
        u, qa, ka, va, qr, kr, vr, gr = _inproj(xz, mod, w_in[l].astype(BF16), tabs, sizes, n_ctx)
        s5w = _s5_weights(s5_lam_re[l], s5_lam_im[l], s5_log_step[l], s5_b_re[l], s5_b_im[l],
                          s5_c_re[l], s5_c_im[l], S5_CHUNK)
        y_s5 = _s5_scan(u, s5w, n_ctx)
        o_att = _attention(qa, ka, va, attn_sink[l], n_ctx, need_ctx)
        o_ret = _retention(qr, kr, vr, gr, ret_log_gamma[l], n_ctx)
        i = l // 2
        router = None if l % 2 == 0 else moe_router[i]
        outs = _outproj(y_s5, u, o_att, o_ret, xz, mod, s5_d[l], s5_w_glu[l], s5_b_glu[l], w_out[l],
                        ln1_g[l], ln1_b[l], router, n_ctx, need_ctx, alpha)
        ctx_tokens = n_ctx if need_ctx else 0
        if l % 2 == 0:
            x1, fx = outs
            xz = _ffn(fx, x1, mod, ffn_w1[i].astype(BF16), ffn_w3[i].astype(BF16), ffn_w2[i].astype(BF16),
                      ln2_g[l], ln2_b[l], ctx_tokens, alpha)
        else:
            x1, fx, route = outs
            xz = _moe(fx, x1, route, mod, moe_w1[i].astype(BF16), moe_w3[i].astype(BF16),
                      moe_w2[i].astype(BF16), ln2_g[l], ln2_b[l], ctx_tokens, alpha)
    return xz if xz.shape[1] == t_len else xz[:, n_ctx:]
```

```python
import functools
import math

import jax
import jax.numpy as jnp
import numpy as np
from jax import lax
from jax.experimental import pallas as pl
from jax.experimental.pallas import tpu as pltpu

F32 = jnp.float32
BF16 = jnp.bfloat16

GRID_W = 64
HEAD_DIM = 64
S5_GROUP = 16
ATT_REP = 4
WINDOW = 128
ATT_BLOCK = 128
RET_CHUNK = 128
TOP_K = 2
LN_EPS = 1e-5
ROPE_BASE = 10000.0
NEG_INF = -1e30

LANES = 128
S5_CHUNK = LANES
TOKEN_BLOCK = 256
FFN_BLOCK = 768
VMEM_LIMIT = 56 * 1024 * 1024


def _cparams(sem):
    return pltpu.CompilerParams(dimension_semantics=sem, vmem_limit_bytes=VMEM_LIMIT)


def _dot(a, b):
    return jnp.dot(a, b, preferred_element_type=F32)


def _dot_nt(a, b):
    return lax.dot_general(a, b, (((1,), (1,)), ((), ())), preferred_element_type=F32)


def _dot_tn(a, b):
    return lax.dot_general(a, b, (((0,), (0,)), ((), ())), preferred_element_type=F32)


def _split_bf16(a):
    hi = a.astype(BF16)
    lo = (a - hi.astype(F32)).astype(BF16)
    return hi, lo


def _dot3(a, b):
    ah, al = _split_bf16(a)
    bh, bl = _split_bf16(b)
    return _dot(ah, bh) + (_dot(ah, bl) + _dot(al, bh))


def _ln(x):
    mu = jnp.mean(x, axis=-1, keepdims=True)
    xc = x - mu
    var = jnp.mean(xc * xc, axis=-1, keepdims=True)
    return xc * lax.rsqrt(var + LN_EPS)


def _silu(x):
    return x * (1.0 / (1.0 + jnp.exp(-x)))


def _sigmoid(x):
    return 1.0 / (1.0 + jnp.exp(-x))


def _gelu_tanh(x):
    c = math.sqrt(2.0 / math.pi)
    return 0.5 * x * (1.0 + jnp.tanh(c * (x + 0.044715 * (x * x * x))))


def _mod_kernel(c_ref, w_ref, b_ref, o_ref):
    o_ref[...] = _dot3(_silu(c_ref[...]), w_ref[...]) + b_ref[...]


def _modulation(cvec, w_mod, b_mod):
    depth, d, n = w_mod.shape
    rows = cvec.shape[0]
    tn = 1536
    return pl.pallas_call(
        _mod_kernel,
        grid=(depth, n // tn),
        in_specs=[pl.BlockSpec((rows, d), lambda l, j: (0, 0)),
                  pl.BlockSpec((None, d, tn), lambda l, j: (l, 0, j)),
                  pl.BlockSpec((None, 1, tn), lambda l, j: (l, 0, j))],
        out_specs=pl.BlockSpec((None, rows, tn), lambda l, j: (l, 0, j)),
        out_shape=jax.ShapeDtypeStruct((depth, rows, n), F32),
        compiler_params=_cparams(("parallel", "parallel")),
        name="modulation",
    )(cvec, w_mod, b_mod.reshape(depth, 1, n))


def _rope_slab(xs, cos, sa, sb, half):
    return xs * cos + pltpu.roll(xs, LANES - half, 1) * sa + pltpu.roll(xs, half, 1) * sb


def _inproj_kernel(x_ref, modx_ref, modz_ref, wc_ref, w_ref, act_ref, ast_ref,
                   ac_ref, asa_ref, asb_ref, rc_ref, rsa_ref, rsb_ref,
                   ut_ref, qat_ref, vat_ref, ka_ref, qr_ref, kr_ref, vr_ref, gr_ref, *, nzb, c_sizes, sizes):
    is_ctx = pl.program_id(1) < nzb
    mod = jnp.where(is_ctx, modz_ref[...], modx_ref[...])
    h = (_ln(x_ref[...]) * (1.0 + mod[1:2]) + mod[0:1]).astype(BF16)
    scale = HEAD_DIM ** -0.5

    ct = _dot_nt(wc_ref[...], h)
    c_offs = np.concatenate([[0], np.cumsum(c_sizes)])
    ut_ref[...] = ct[int(c_offs[0]):int(c_offs[1])]
    cos_t = jnp.where(is_ctx, 1.0, act_ref[...])
    sin_t = jnp.where(is_ctx, 0.0, ast_ref[...])
    quarter = HEAD_DIM // 4
    heads = []
    for hd in range(c_sizes[1] // HEAD_DIM):
        xh = ct[int(c_offs[1]) + hd * HEAD_DIM:int(c_offs[1]) + (hd + 1) * HEAD_DIM]
        swap = jnp.concatenate([xh[quarter:2 * quarter], xh[0:quarter], xh[3 * quarter:], xh[2 * quarter:3 * quarter]],
                               axis=0)
        heads.append((xh * cos_t + swap * sin_t) * scale)
    qat = jnp.concatenate(heads, axis=0)
    vat = ct[int(c_offs[2]):int(c_offs[3])]
    for k in range(qat_ref.shape[0]):
        qat_ref[k] = qat[:, k * ATT_BLOCK:(k + 1) * ATT_BLOCK]
        vat_ref[k] = vat[:, k * ATT_BLOCK:(k + 1) * ATT_BLOCK]

    p = _dot(h, w_ref[...])
    ac = jnp.where(is_ctx, 1.0, ac_ref[...])
    asa = jnp.where(is_ctx, 0.0, asa_ref[...])
    asb = jnp.where(is_ctx, 0.0, asb_ref[...])
    rc = jnp.where(is_ctx, 1.0, rc_ref[...])
    rsa = jnp.where(is_ctx, 0.0, rsa_ref[...])
    rsb = jnp.where(is_ctx, 0.0, rsb_ref[...])
    offs = np.concatenate([[0], np.cumsum(sizes)])

    def cols(i):
        return p[:, int(offs[i]):int(offs[i + 1])]

    def rope_cols(i, cos, sa, sb, half, mul):
        blk = cols(i)
        slabs = [_rope_slab(blk[:, s:s + LANES], cos, sa, sb, half) for s in range(0, blk.shape[1], LANES)]
        out = slabs[0] if len(slabs) == 1 else jnp.concatenate(slabs, axis=1)
        return out * mul if mul != 1.0 else out

    ka_ref[...] = rope_cols(0, ac, asa, asb, HEAD_DIM // 4, 1.0)
    qr_ref[...] = rope_cols(1, rc, rsa, rsb, HEAD_DIM // 2, 1.0)
    kr_ref[...] = rope_cols(2, rc, rsa, rsb, HEAD_DIM // 2, scale)
    vr_ref[...] = cols(3)
    gr_ref[...] = cols(4)


def _inproj(xz, mod, w_in, tabs, sizes, n_ctx):
    bsz, s, d = xz.shape
    tm = TOKEN_BLOCK
    nzb = n_ctx // tm
    nrow = mod.shape[0]
    offs = np.concatenate([[0], np.cumsum(sizes)])
    col = lambda i: w_in[:, int(offs[i]):int(offs[i + 1])]
    c_idx, r_idx = (0, 1, 3), (2, 4, 5, 6, 7)
    c_sizes = tuple(sizes[i] for i in c_idx)
    r_sizes = tuple(sizes[i] for i in r_idx)
    w_c_t = jnp.concatenate([col(i) for i in c_idx], axis=1).T
    w_rest = jnp.concatenate([col(i) for i in r_idx], axis=1)
    sub = tm // ATT_BLOCK
    tok = lambda width: pl.BlockSpec((None, tm, width), lambda b, j: (b, j, 0))
    chan = lambda width: pl.BlockSpec((sub, None, width, ATT_BLOCK), lambda b, j: (j, b, 0, 0))
    tab = pl.BlockSpec((tm, LANES), lambda b, j: (jnp.maximum(j - nzb, 0), 0))
    tab_t = pl.BlockSpec((HEAD_DIM, tm), lambda b, j: (0, jnp.maximum(j - nzb, 0)))
    return pl.pallas_call(
        functools.partial(_inproj_kernel, nzb=nzb, c_sizes=c_sizes, sizes=r_sizes),
        grid=(bsz, s // tm),
        in_specs=[tok(d),
                  pl.BlockSpec((None, 6, d), lambda b, j: (b, 0, 0)),
                  pl.BlockSpec((None, 6, d), lambda b, j: (nrow - 1, 0, 0)),
                  pl.BlockSpec((sum(c_sizes), d), lambda b, j: (0, 0)),
                  pl.BlockSpec((d, sum(r_sizes)), lambda b, j: (0, 0)),
                  tab_t, tab_t, tab, tab, tab, tab, tab, tab],
        out_specs=[pl.BlockSpec((None, None, c_sizes[0], tm), lambda b, j: (j, b, 0, 0)),
                   chan(c_sizes[1]), chan(c_sizes[2])] + [tok(w) for w in r_sizes],
        out_shape=[jax.ShapeDtypeStruct((s // tm, bsz, c_sizes[0], tm), F32),
                   jax.ShapeDtypeStruct((s // ATT_BLOCK, bsz, c_sizes[1], ATT_BLOCK), F32),
                   jax.ShapeDtypeStruct((s // ATT_BLOCK, bsz, c_sizes[2], ATT_BLOCK), F32)]
                  + [jax.ShapeDtypeStruct((bsz, s, w), F32) for w in r_sizes],
        compiler_params=_cparams(("parallel", "parallel")),
        name="inproj",
    )(xz, mod, mod, w_c_t, w_rest, *tabs)


def _rope_tables(t_len):
    t = jnp.arange(t_len)
    rows = (t // GRID_W).astype(F32)
    cols = (t % GRID_W).astype(F32)
    pos = t.astype(F32)

    def angles(p, dim):
        inv_freq = ROPE_BASE ** (-jnp.arange(0, dim, 2, dtype=F32) / dim)
        return p[:, None] * inv_freq[None, :]

    def head_tables(angs):
        cos = jnp.concatenate([jnp.concatenate([jnp.cos(a), jnp.cos(a)], -1) for a in angs], -1)
        sa = jnp.concatenate([jnp.concatenate([-jnp.sin(a), jnp.zeros_like(a)], -1) for a in angs], -1)
        sb = jnp.concatenate([jnp.concatenate([jnp.zeros_like(a), jnp.sin(a)], -1) for a in angs], -1)
        rep = LANES // HEAD_DIM
        return tuple(jnp.tile(x, (1, rep)) for x in (cos, sa, sb))

    att = head_tables([angles(rows, HEAD_DIM // 2), angles(cols, HEAD_DIM // 2)])
    ret = head_tables([angles(pos, HEAD_DIM)])
    att_t = (att[0][:, :HEAD_DIM].T, (att[1] + att[2])[:, :HEAD_DIM].T)
    return att_t + att + ret


def _s5_kernel(u_ref, tab_ref, wsf_ref, wsb_ref, wrf_ref, wrb_ref, lam_ref, y_ref,
               lhs_ref, m_ref, acc_ref, cf_ref, cb_ref, pf_ref, pb_ref, *, bsz, nz_chunks):
    r_blk, ch, tm = u_ref.shape
    c = S5_CHUNK
    kpb = tm // c
    n_chunks = (r_blk // bsz) * kpb

    for i in range(ch):
        x = u_ref[:, i, :].astype(BF16)
        p, half = divmod(i, 2)
        for k in range(kpb):
            lhs_ref[p, k * r_blk:(k + 1) * r_blk, half * c:(half + 1) * c] = x[:, k * c:(k + 1) * c]
    acc_ref[...] = jnp.zeros_like(acc_ref)
    cf_ref[...] = jnp.zeros_like(cf_ref)
    cb_ref[...] = jnp.zeros_like(cb_ref)

    def pair(p, carry):
        for ih in range(2):
            for o in range(ch):
                tab = tab_ref[2 * p + ih, o]
                for r2 in range(c // 16):
                    lo = c - 16 * r2
                    rows = jnp.concatenate([tab[:, lo:lo + c], tab[:, lo - 8:lo - 8 + c]], axis=0)
                    m_ref[ih * c + 16 * r2:ih * c + 16 * r2 + 16, o * c:(o + 1) * c] = rows.astype(BF16)
        lhs = lhs_ref[p]
        acc_ref[...] += _dot(lhs, m_ref[...])
        cf_ref[...] += _dot(lhs, wsf_ref[p])
        cb_ref[...] += _dot(lhs, wsb_ref[p])
        return carry

    lax.fori_loop(0, ch // 2, pair, 0)
    lam = lam_ref[...]
    nst = lam.shape[1] // 2

    def advance(s, a, bc, drive):
        return s * a + pltpu.roll(s, nst, 1) * bc + drive

    def sweep(order, c_ref, p_ref, a, bc):
        s = jnp.zeros((bsz, lam.shape[1]), F32)
        for n in order:
            j, k = divmod(n, kpb)
            rows = slice(k * r_blk + j * bsz, k * r_blk + (j + 1) * bsz)
            p_ref[rows, :] = s
            s = advance(s, a, bc, c_ref[rows, :])

    sweep(list(range(n_chunks)), cf_ref, pf_ref, lam[0:1], lam[1:2])
    order_b = list(range(nz_chunks - 1, -1, -1)) + list(range(n_chunks - 1, nz_chunks - 1, -1))
    sweep(order_b, cb_ref, pb_ref, lam[2:3], lam[3:4])
    acc_ref[...] += (_dot(pf_ref[...].astype(BF16), wrf_ref[...])
                     + _dot(pb_ref[...].astype(BF16), wrb_ref[...]))
    for o in range(ch):
        for k in range(kpb):
            y_ref[:, o, k * c:(k + 1) * c] = acc_ref[k * r_blk:(k + 1) * r_blk, o * c:(o + 1) * c]


def _s5_weights(lam_re, lam_im, log_step, b_re, b_im, c_re, c_im, chunk):
    hp = lax.Precision.HIGHEST
    lam = lax.complex(lam_re.astype(F32), lam_im.astype(F32))
    lam_dt = lam * jnp.exp(log_step.astype(F32))[..., None]
    lam_bar = jnp.exp(lam_dt)
    b_bar = lax.complex(b_re.astype(F32), b_im.astype(F32)) * ((lam_bar - 1.0) / lam)[..., None]
    c_mat = lax.complex(c_re.astype(F32), c_im.astype(F32))
    g, n = lam.shape[1], lam.shape[2]
    ch = b_bar.shape[-1]
    steps = jnp.arange(chunk + 1, dtype=F32)
    pw = jnp.exp(steps[None, :, None, None] * lam_dt[:, None])
    kern = jnp.einsum('zgon,zdgn,zgni->zgdoi', c_mat, pw[:, :chunk], b_bar, precision=hp).real
    zero_lag = kern[0, :, 0] + kern[1, :, 0]
    lag_table = jnp.concatenate([jnp.zeros_like(zero_lag)[:, None], kern[1, :, :0:-1], zero_lag[:, None],
                                 kern[0, :, 1:]], axis=1).transpose(0, 3, 2, 1)
    tab = jnp.stack([jnp.roll(lag_table, q, axis=-1) for q in range(8)], axis=3)

    def state_in(pw_s, b_dir):
        w = pw_s[:, :, :, None] * b_dir[None]
        w = jnp.concatenate([w.real, w.imag], axis=2)
        return w.transpose(1, 3, 0, 2).reshape(g, ch // 2, 2 * chunk, 2 * n)

    def state_out(pw_t, c_dir):
        w = c_dir[None] * pw_t[:, :, None, :]
        w = jnp.concatenate([w.real, -w.imag], axis=3)
        return w.transpose(1, 3, 2, 0).reshape(g, 2 * n, ch * chunk)

    wsf = state_in(pw[0, chunk - 1::-1][:chunk], b_bar[0])
    wsb = state_in(pw[1, :chunk], b_bar[1])
    wrf = state_out(pw[0, 1:chunk + 1], c_mat[0])
    wrb = state_out(pw[1, chunk:0:-1], c_mat[1])
    lam_c = pw[:, chunk]
    rows = []
    for z in range(2):
        rows.append(jnp.concatenate([lam_c[z].real, lam_c[z].real], -1))
        rows.append(jnp.concatenate([-lam_c[z].imag, lam_c[z].imag], -1))
    lam_rows = jnp.stack(rows + [jnp.zeros_like(rows[0])] * 4, axis=1)
    return tab, wsf.astype(BF16), wsb.astype(BF16), wrf.astype(BF16), wrb.astype(BF16), lam_rows


def _s5_scan(ut, weights, n_ctx):
    nblk, bsz, width, tm = ut.shape
    tab, wsf, wsb, wrf, wrb, lam_rows = weights
    g, ch = tab.shape[0], tab.shape[1]
    c = S5_CHUNK
    nst2 = wsf.shape[3]
    r_blk = nblk * bsz
    rows = r_blk * (tm // c)
    grp = lambda *shape: pl.BlockSpec((None,) + shape, lambda i: (i,) + (0,) * len(shape))
    tok = pl.BlockSpec((r_blk, ch, tm), lambda i: (0, i, 0))
    y = pl.pallas_call(
        functools.partial(_s5_kernel, bsz=bsz, nz_chunks=n_ctx // c),
        grid=(g,),
        in_specs=[tok, grp(ch, ch, 8, 2 * c), grp(ch // 2, 2 * c, nst2), grp(ch // 2, 2 * c, nst2),
                  grp(nst2, ch * c), grp(nst2, ch * c), grp(8, nst2)],
        out_specs=tok,
        out_shape=jax.ShapeDtypeStruct((r_blk, width, tm), F32),
        scratch_shapes=[pltpu.VMEM((ch // 2, rows, 2 * c), BF16),
                        pltpu.VMEM((2 * c, ch * c), BF16),
                        pltpu.VMEM((rows, ch * c), F32)] + [pltpu.VMEM((rows, nst2), F32)] * 4,
        compiler_params=_cparams(("parallel",)),
        name="s5_scan",
    )(ut.reshape(r_blk, width, tm), tab, wsf, wsb, wrf, wrb, lam_rows)
    return y.reshape(nblk, bsz, width, tm)


def _attn_kernel(sink_ref, q_ref, k_ref, v_ref, o_ref, *, n_ctx, t_len, q_off, kv_heads):
    qi = pl.program_id(1) + q_off
    nzb = n_ctx // ATT_BLOCK
    band = 3 * ATT_BLOCK
    wide = ATT_REP * ATT_BLOCK
    q = q_ref[...].astype(BF16)
    kc = k_ref[0:n_ctx, :].astype(BF16)
    vc = jnp.concatenate([v_ref[p] for p in range(nzb)], axis=1).astype(BF16)

    def rows(a, i):
        return a[i * HEAD_DIM:(i + 1) * HEAD_DIM]

    def q_group(g):
        qg = jnp.concatenate([rows(q, g * ATT_REP + r) for r in range(ATT_REP)], axis=1)
        zero = jnp.zeros_like(qg)
        return jnp.concatenate([qg if gg == g else zero for gg in range(kv_heads)], axis=0)

    def finish(g, o, den):
        o = o / den
        for r in range(ATT_REP):
            h = g * ATT_REP + r
            o_ref[h * HEAD_DIM:(h + 1) * HEAD_DIM, :] = o[:, r * ATT_BLOCK:(r + 1) * ATT_BLOCK]

    @pl.when(qi < nzb)
    def _():
        for g in range(kv_heads):
            sink = sink_ref[g:g + 1, :]
            s = _dot(kc, q_group(g))
            m = jnp.maximum(jnp.max(s, axis=0, keepdims=True), sink)
            e = jnp.exp(s - m)
            den = jnp.sum(e, axis=0, keepdims=True) + jnp.exp(sink - m)
            finish(g, _dot(rows(vc, g), e.astype(BF16)), den)

    @pl.when(qi >= nzb)
    def _():
        n = qi - nzb
        start = jnp.clip((n - 1) * ATT_BLOCK, 0, t_len - band)
        kl = k_ref[pl.ds(pl.multiple_of(n_ctx + start, ATT_BLOCK), band), :].astype(BF16)
        p0 = (n_ctx + start) // ATT_BLOCK
        vl = jnp.concatenate([v_ref[p0 + t] for t in range(3)], axis=1).astype(BF16)
        k_pos = start + lax.broadcasted_iota(jnp.int32, (band, wide), 0)
        q_pos = n * ATT_BLOCK + (lax.broadcasted_iota(jnp.int32, (band, wide), 1) & (ATT_BLOCK - 1))
        valid = jnp.abs(k_pos - q_pos) <= WINDOW
        for g in range(kv_heads):
            sink = sink_ref[g:g + 1, :]
            qg = q_group(g)
            s_loc = jnp.where(valid, _dot(kl, qg), NEG_INF)
            s_ctx = _dot(kc, qg)
            m = jnp.maximum(jnp.maximum(jnp.max(s_loc, axis=0, keepdims=True),
                                        jnp.max(s_ctx, axis=0, keepdims=True)), sink)
            e_loc = jnp.exp(s_loc - m)
            e_ctx = jnp.exp(s_ctx - m)
            den = (jnp.sum(e_loc, axis=0, keepdims=True) + jnp.sum(e_ctx, axis=0, keepdims=True)
                   + jnp.exp(sink - m))
            o = _dot(rows(vl, g), e_loc.astype(BF16)) + _dot(rows(vc, g), e_ctx.astype(BF16))
            finish(g, o, den)


def _attention(qat, ka, vat, sink, n_ctx, need_ctx):
    nblk, bsz, qw, _ = qat.shape
    s, kvw = ka.shape[1], ka.shape[2]
    t_len = s - n_ctx
    kv_heads = kvw // HEAD_DIM
    q_off = 0 if need_ctx else n_ctx // ATT_BLOCK
    nq = nblk - q_off
    sink_rows = jnp.repeat(sink.astype(F32).reshape(kv_heads, ATT_REP), ATT_BLOCK, axis=1)
    return pl.pallas_call(
        functools.partial(_attn_kernel, n_ctx=n_ctx, t_len=t_len, q_off=q_off, kv_heads=kv_heads),
        grid=(bsz, nq),
        in_specs=[pl.BlockSpec(sink_rows.shape, lambda b, j: (0, 0)),
                  pl.BlockSpec((None, None, qw, ATT_BLOCK), lambda b, j: (j + q_off, b, 0, 0)),
                  pl.BlockSpec((None, s, kvw), lambda b, j: (b, 0, 0)),
                  pl.BlockSpec((nblk, None, kvw, ATT_BLOCK), lambda b, j: (0, b, 0, 0))],
        out_specs=pl.BlockSpec((None, None, qw, ATT_BLOCK), lambda b, j: (j, b, 0, 0)),
        out_shape=jax.ShapeDtypeStruct((nq, bsz, qw, ATT_BLOCK), F32),
        compiler_params=_cparams(("parallel", "arbitrary")),
        name="window_attention",
    )(sink_rows, qat, ka, vat)


def _ret_kernel(lg_ref, q_ref, k_ref, v_ref, g_ref, o_ref, acc_ref, dec_ref, sf_ref, sb_ref,
                *, n_chunks, nz_chunks, heads):
    c = RET_CHUNK
    ii = lax.broadcasted_iota(jnp.int32, (c, c), 0)
    jj = lax.broadcasted_iota(jnp.int32, (c, c), 1)
    diff = (ii - jj).astype(F32)
    idx = lax.broadcasted_iota(jnp.int32, (c, 1), 0).astype(F32)
    for h in range(heads):
        lgf, lgb = lg_ref[0, h], lg_ref[1, h]
        dec_ref[h] = jnp.where(diff >= 0, jnp.exp(lgf * jnp.maximum(diff, 0.0)),
                               jnp.exp(lgb * jnp.maximum(-diff, 0.0)))
    sf_ref[...] = jnp.zeros_like(sf_ref)
    sb_ref[...] = jnp.zeros_like(sb_ref)

    def hcols(a, h):
        return a[:, h * HEAD_DIM:(h + 1) * HEAD_DIM]

    chunk_len = jnp.full((1, HEAD_DIM), float(c), F32)

    def fwd(n, carry):
        rows = pl.ds(pl.multiple_of(n * c, c), c)
        q, k, v = q_ref[rows, :], k_ref[rows, :], v_ref[rows, :]
        for h in range(heads):
            lgf = lg_ref[0, h]
            cols = slice(h * HEAD_DIM, (h + 1) * HEAD_DIM)
            qh, kh, vh = hcols(q, h), hcols(k, h), hcols(v, h).astype(BF16)
            scores = _dot_nt(qh.astype(BF16), kh.astype(BF16)) * dec_ref[h]
            s_prev = sf_ref[h]
            o = _dot(scores.astype(BF16), vh)
            o = o + _dot((qh * jnp.exp(lgf * (idx + 1.0))).astype(BF16), s_prev.astype(BF16))
            kw = kh * jnp.exp(lgf * (c - 1.0 - idx))
            sf_ref[h] = jnp.exp(lgf * chunk_len) * s_prev + _dot_tn(kw.astype(BF16), vh)
            acc_ref[rows, cols] = o
        return carry

    lax.fori_loop(0, n_chunks, fwd, 0)

    def bwd(i, carry):
        n = jnp.where(i < nz_chunks, nz_chunks - 1 - i, n_chunks - 1 - i + nz_chunks)
        rows = pl.ds(pl.multiple_of(n * c, c), c)
        q, k, v = q_ref[rows, :], k_ref[rows, :], v_ref[rows, :]
        acc = acc_ref[rows, :]
        gate = _silu(g_ref[rows, :])
        for h in range(heads):
            lgb = lg_ref[1, h]
            cols = slice(h * HEAD_DIM, (h + 1) * HEAD_DIM)
            qh, kh, vh = hcols(q, h), hcols(k, h), hcols(v, h).astype(BF16)
            s_prev = sb_ref[h]
            o = hcols(acc, h) + _dot((qh * jnp.exp(lgb * (c - idx))).astype(BF16), s_prev.astype(BF16))
            kw = kh * jnp.exp(lgb * idx)
            sb_ref[h] = jnp.exp(lgb * chunk_len) * s_prev + _dot_tn(kw.astype(BF16), vh)
            o_ref[rows, cols] = _ln(o) * hcols(gate, h)
        return carry

    lax.fori_loop(0, n_chunks, bwd, 0)


def _retention(qr, kr, vr, gr, log_gamma, n_ctx):
    bsz, s, w = qr.shape
    heads = w // HEAD_DIM
    n_chunks = s // RET_CHUNK
    tok = pl.BlockSpec((None, s, w), lambda b: (b, 0, 0))
    return pl.pallas_call(
        functools.partial(_ret_kernel, n_chunks=n_chunks, nz_chunks=n_ctx // RET_CHUNK, heads=heads),
        grid=(bsz,),
        in_specs=[pl.BlockSpec(memory_space=pltpu.SMEM), tok, tok, tok, tok],
        out_specs=tok,
        out_shape=jax.ShapeDtypeStruct((bsz, s, w), F32),
        scratch_shapes=[pltpu.VMEM((s, w), F32),
                        pltpu.VMEM((heads, RET_CHUNK, RET_CHUNK), F32),
                        pltpu.VMEM((heads, HEAD_DIM, HEAD_DIM), F32),
                        pltpu.VMEM((heads, HEAD_DIM, HEAD_DIM), F32)],
        compiler_params=_cparams(("parallel",)),
        name="retention",
    )(log_gamma.astype(F32), qr, kr, vr, gr)


def _outproj_kernel(*refs, nzb, alpha, with_router, w_s5, w_att, n_exp):
    if with_router:
        (y_ref, u_ref, a_ref, r_ref, x_ref, modx_ref, modz_ref, d_ref, wg_ref, bg_ref, wo_ref,
         g1_ref, b1_ref, rt_ref, x1_ref, fx_ref, gate_ref) = refs
    else:
        (y_ref, u_ref, a_ref, r_ref, x_ref, modx_ref, modz_ref, d_ref, wg_ref, bg_ref, wo_ref,
         g1_ref, b1_ref, x1_ref, fx_ref) = refs
    is_ctx = pl.program_id(1) < nzb
    mod = jnp.where(is_ctx, modz_ref[...], modx_ref[...])
    g = _gelu_tanh(y_ref[...] + d_ref[...] * u_ref[...])
    s5 = g * _sigmoid(_dot(wg_ref[...], g.astype(BF16)) + bg_ref[...])
    att = jnp.concatenate([_dot_tn(a_ref[k].astype(BF16), wo_ref[w_s5:w_s5 + w_att, :])
                           for k in range(a_ref.shape[0])], axis=0)
    mix = (_dot_tn(s5.astype(BF16), wo_ref[0:w_s5, :]) + att
           + _dot(r_ref[...].astype(BF16), wo_ref[w_s5 + w_att:, :]))
    x1 = _ln(alpha * x_ref[...] + mod[2:3] * mix) * g1_ref[...] + b1_ref[...]
    x1_ref[...] = x1
    fx = _ln(x1) * (1.0 + mod[4:5]) + mod[3:4]
    fx_ref[...] = fx.astype(fx_ref.dtype)
    if with_router:
        lane = lax.broadcasted_iota(jnp.int32, (fx.shape[0], LANES), 1)
        logits = jnp.where(lane < n_exp, _dot3(fx, rt_ref[...]), -jnp.inf)
        m1 = jnp.max(logits, axis=1, keepdims=True)
        i1 = jnp.min(jnp.where(logits == m1, lane, LANES), axis=1, keepdims=True)
        rest = jnp.where(lane == i1, -jnp.inf, logits)
        m2 = jnp.max(rest, axis=1, keepdims=True)
        i2 = jnp.min(jnp.where(rest == m2, lane, LANES), axis=1, keepdims=True)
        e2 = jnp.exp(m2 - m1)
        den = 1.0 + e2
        route = jnp.where(lane == 0, 1.0 / den, jnp.where(lane == 1, e2 / den, 0.0))
        route = jnp.where(lane == 2, i1.astype(F32), jnp.where(lane == 3, i2.astype(F32), route))
        gate_ref[...] = route


def _outproj(y_s5, u, o_att, o_ret, xz, mod, s5_d, w_glu, b_glu, w_out, ln_g, ln_b, router, n_ctx, need_ctx,
             alpha):
    bsz, s, d = xz.shape
    tm = TOKEN_BLOCK
    nzb = n_ctx // tm
    off = 0 if need_ctx else nzb
    nblk = s // tm - off
    s_out = nblk * tm
    nrow = mod.shape[0]
    w_s5, w_att = y_s5.shape[2], o_att.shape[2]
    sub = tm // ATT_BLOCK
    att_off = off if o_att.shape[0] == s // ATT_BLOCK else 0
    att = pl.BlockSpec((sub, None, w_att, ATT_BLOCK), lambda b, j: (j + att_off, b, 0, 0))

    def tok(width, shift):
        return pl.BlockSpec((None, tm, width), lambda b, j: (b, j + shift, 0))

    def full(a):
        return pl.BlockSpec(a.shape, lambda b, j: (0,) * a.ndim)

    chan = pl.BlockSpec((None, None, w_s5, tm), lambda b, j: (j + off, b, 0, 0))
    vec = lambda a: a.reshape(1, -1).astype(F32)
    col = lambda a: a.reshape(-1, 1).astype(F32)
    consts = [col(s5_d), w_glu.T.astype(BF16), col(b_glu), w_out.astype(BF16), vec(ln_g), vec(ln_b)]
    in_specs = [chan, chan, att, tok(o_ret.shape[2], off), tok(d, off),
                pl.BlockSpec((None, 6, d), lambda b, j: (b, 0, 0)),
                pl.BlockSpec((None, 6, d), lambda b, j: (nrow - 1, 0, 0))] + [full(a) for a in consts]
    out_specs = [tok(d, 0), tok(d, 0)]
    fx_dtype = BF16 if router is None else F32
    out_shape = [jax.ShapeDtypeStruct((bsz, s_out, d), F32), jax.ShapeDtypeStruct((bsz, s_out, d), fx_dtype)]
    args = [y_s5, u, o_att, o_ret, xz, mod, mod] + consts
    with_router = router is not None
    n_exp = 0
    if with_router:
        n_exp = router.shape[1]
        router_pad = jnp.pad(router.astype(F32), ((0, 0), (0, LANES - n_exp)))
        args.append(router_pad)
        in_specs.append(full(router_pad))
        out_specs.append(tok(LANES, 0))
        out_shape.append(jax.ShapeDtypeStruct((bsz, s_out, LANES), F32))
    return pl.pallas_call(
        functools.partial(_outproj_kernel, nzb=nzb - off, alpha=alpha, with_router=with_router,
                          w_s5=w_s5, w_att=w_att, n_exp=n_exp),
        grid=(bsz, nblk),
        in_specs=in_specs,
        out_specs=out_specs,
        out_shape=out_shape,
        compiler_params=_cparams(("parallel", "parallel")),
        name="outproj",
    )(*args)


def _swiglu_into(x_bf16, w1_ref, w3_ref, w2_ref, acc_ref, fc):
    for s in range(0, w1_ref.shape[1], fc):
        h1 = _dot(x_bf16, w1_ref[:, s:s + fc])
        h3 = _dot(x_bf16, w3_ref[:, s:s + fc])
        acc_ref[...] += _dot((_silu(h1) * h3).astype(BF16), w2_ref[s:s + fc, :])


def _ffn_kernel(fx_ref, x1_ref, modx_ref, modz_ref, w1_ref, w3_ref, w2_ref, g2_ref, b2_ref,
                o_ref, acc_ref, *, n_ctx_tokens, alpha, fc):
    tm = acc_ref.shape[0]
    acc_ref[...] = jnp.zeros_like(acc_ref)
    _swiglu_into(fx_ref[...], w1_ref, w3_ref, w2_ref, acc_ref, fc)
    row = pl.program_id(1) * tm + lax.broadcasted_iota(jnp.int32, (tm, 1), 0)
    gate = jnp.where(row < n_ctx_tokens, modz_ref[5:6, :], modx_ref[5:6, :])
    o_ref[...] = _ln(alpha * x1_ref[...] + gate * acc_ref[...]) * g2_ref[...] + b2_ref[...]


def _ffn(fx, x1, mod, w1, w3, w2, ln_g, ln_b, n_ctx_tokens, alpha):
    bsz, s, d = x1.shape
    ff = w1.shape[1]
    tm = _pick_chunk(s, FFN_BLOCK)
    nrow = mod.shape[0]
    tok = pl.BlockSpec((None, tm, d), lambda b, j: (b, j, 0))
    full = lambda a: pl.BlockSpec(a.shape, lambda b, j: (0,) * a.ndim, pipeline_mode=pl.Buffered(1))
    vec = lambda a: a.reshape(1, -1).astype(F32)
    return pl.pallas_call(
        functools.partial(_ffn_kernel, n_ctx_tokens=n_ctx_tokens, alpha=alpha, fc=_pick_chunk(ff, 512)),
        grid=(bsz, s // tm),
        in_specs=[tok, tok,
                  pl.BlockSpec((None, 6, d), lambda b, j: (b, 0, 0)),
                  pl.BlockSpec((None, 6, d), lambda b, j: (nrow - 1, 0, 0)),
                  full(w1), full(w3), full(w2),
                  pl.BlockSpec((1, d), lambda b, j: (0, 0)), pl.BlockSpec((1, d), lambda b, j: (0, 0))],
        out_specs=tok,
        out_shape=jax.ShapeDtypeStruct((bsz, s, d), F32),
        scratch_shapes=[pltpu.VMEM((tm, d), F32)],
        compiler_params=_cparams(("parallel", "parallel")),
        name="dense_ffn",
    )(fx, x1, mod, mod, w1, w3, w2, vec(ln_g), vec(ln_b))


MOE_TILE = 1024


def _route_plan(e1, e2, n_exp, tile):
    n = e1.shape[0]
    pair_e = jnp.stack([e1, e2], axis=1).reshape(-1)
    onehot = (pair_e[:, None] == jnp.arange(n_exp, dtype=jnp.int32)[None, :]).astype(jnp.int32)
    before = jnp.cumsum(onehot, axis=0) - onehot
    rank = jnp.sum(before * onehot, axis=1)
    counts = jnp.sum(onehot, axis=0)
    padded = (counts + tile - 1) // tile * tile
    ends = jnp.cumsum(padded)
    starts = ends - padded
    dest = starts[pair_e] + rank
    n_rows = (2 * n + n_exp * (tile - 1)) // tile * tile
    n_tiles = n_rows // tile
    row_token = jnp.zeros((n_rows,), jnp.int32).at[dest].set(jnp.arange(2 * n, dtype=jnp.int32) // 2)
    tile_start = jnp.arange(n_tiles, dtype=jnp.int32) * tile
    tile_expert = jnp.minimum(jnp.sum((tile_start[:, None] >= ends[None, :]).astype(jnp.int32), axis=1),
                              n_exp - 1)
    n_used = (ends[-1] // tile).astype(jnp.int32).reshape(1)
    return row_token, dest.reshape(n, 2), tile_expert, n_used


def _moe_gemm_kernel(te_ref, nused_ref, tok_ref, tokn_ref, x_hbm, w1_ref, w3_ref, w2_ref, y_ref,
                     xbuf_ref, sem, *, fc, tile):
    i = pl.program_id(0)
    f = pl.program_id(1)
    slot = i % 2
    n_used = nused_ref[0]

    def row_copy(t_ref, r, s):
        return pltpu.make_async_copy(x_hbm.at[pl.ds(t_ref[0, r], 1)], xbuf_ref.at[s, pl.ds(r, 1)], sem.at[s])

    def issue(t_ref, s):
        def body(r, carry):
            row_copy(t_ref, r, s).start()
            return carry
        lax.fori_loop(0, tile, body, 0, unroll=8)

    @pl.when(f == 0)
    def _():
        y_ref[...] = jnp.zeros_like(y_ref)

        @pl.when(i == 0)
        def _():
            issue(tok_ref, 0)

        @pl.when(i + 1 < n_used)
        def _():
            issue(tokn_ref, 1 - slot)

        @pl.when(i < n_used)
        def _():
            pltpu.make_async_copy(x_hbm.at[pl.ds(0, tile)], xbuf_ref.at[slot], sem.at[slot]).wait()

    @pl.when(i < n_used)
    def _():
        _swiglu_into(xbuf_ref[slot].astype(BF16), w1_ref, w3_ref, w2_ref, y_ref, fc)


def _moe_gemm(x_flat, row_token, tile_expert, n_used, w1, w3, w2, tile):
    n_rows = row_token.shape[0]
    n_tiles = n_rows // tile
    d = x_flat.shape[1]
    ff = w1.shape[2]
    nf = 2 if ff % (2 * LANES) == 0 else 1
    tf = ff // nf
    last = lambda i, nu: jnp.minimum(i, nu[0] - 1)
    fsel = lambda i, f, nu: jnp.where(i < nu[0], f, nf - 1)
    tok = lambda imap: pl.BlockSpec((None, 1, tile), imap, memory_space=pltpu.SMEM)
    tokens = row_token.reshape(n_tiles, 1, tile)
    return pl.pallas_call(
        functools.partial(_moe_gemm_kernel, fc=_pick_chunk(tf, 256), tile=tile),
        grid_spec=pltpu.PrefetchScalarGridSpec(
            num_scalar_prefetch=2,
            grid=(n_tiles, nf),
            in_specs=[tok(lambda i, f, te, nu: (i, 0, 0)),
                      tok(lambda i, f, te, nu: (jnp.minimum(i + 1, n_tiles - 1), 0, 0)),
                      pl.BlockSpec(memory_space=pl.ANY),
                      pl.BlockSpec((None, d, tf), lambda i, f, te, nu: (te[last(i, nu)], 0, fsel(i, f, nu))),
                      pl.BlockSpec((None, d, tf), lambda i, f, te, nu: (te[last(i, nu)], 0, fsel(i, f, nu))),
                      pl.BlockSpec((None, tf, d), lambda i, f, te, nu: (te[last(i, nu)], fsel(i, f, nu), 0))],
            out_specs=pl.BlockSpec((tile, d), lambda i, f, te, nu: (i, 0)),
            scratch_shapes=[pltpu.VMEM((2, tile, d), F32), pltpu.SemaphoreType.DMA((2,))]),
        out_shape=jax.ShapeDtypeStruct((n_rows, d), F32),
        compiler_params=_cparams(("arbitrary", "arbitrary")),
        name="moe_gemm",
    )(tile_expert, n_used, tokens, tokens, x_flat, w1, w3, w2)


def _moe_combine_kernel(pos_ref, posn_ref, route_ref, x1_ref, modx_ref, modz_ref, g2_ref, b2_ref, y_hbm,
                        o_ref, buf_ref, sem, *, nzb, nblk, alpha, tm):
    i = pl.program_id(0)
    slot = i % 2

    def row_copy(p_ref, r, k, s):
        return pltpu.make_async_copy(y_hbm.at[pl.ds(p_ref[0, k * tm + r], 1)],
                                     buf_ref.at[s, pl.ds(k * tm + r, 1)], sem.at[s])

    def issue(p_ref, s):
        def body(r, carry):
            row_copy(p_ref, r, 0, s).start()
            row_copy(p_ref, r, 1, s).start()
            return carry
        lax.fori_loop(0, tm, body, 0, unroll=8)

    @pl.when(i == 0)
    def _():
        issue(pos_ref, 0)

    @pl.when(i + 1 < pl.num_programs(0))
    def _():
        issue(posn_ref, 1 - slot)

    pltpu.make_async_copy(y_hbm.at[pl.ds(0, 2 * tm)], buf_ref.at[slot], sem.at[slot]).wait()
    route = route_ref[...]
    f = route[:, 0:1] * buf_ref[slot, 0:tm] + route[:, 1:2] * buf_ref[slot, tm:2 * tm]
    is_ctx = (i % nblk) < nzb
    mod = jnp.where(is_ctx, modz_ref[...], modx_ref[...])
    o_ref[...] = _ln(alpha * x1_ref[...] + mod[5:6] * f) * g2_ref[...] + b2_ref[...]


def _moe_combine(ys, pos, route, x1, mod, ln_g, ln_b, n_ctx_tokens, alpha):
    bsz, s, d = x1.shape
    tm = TOKEN_BLOCK
    nblk = s // tm
    n_steps = bsz * nblk
    nrow = mod.shape[0]
    pos_steps = pos.reshape(n_steps, tm, 2).transpose(0, 2, 1).reshape(n_steps, 1, 2 * tm)
    tok = lambda width: pl.BlockSpec((tm, width), lambda i: (i, 0))
    smem = lambda imap: pl.BlockSpec((None, 1, 2 * tm), imap, memory_space=pltpu.SMEM)
    vec = lambda a: a.reshape(1, -1).astype(F32)
    out = pl.pallas_call(
        functools.partial(_moe_combine_kernel, nzb=n_ctx_tokens // tm, nblk=nblk, alpha=alpha, tm=tm),
        grid=(n_steps,),
        in_specs=[smem(lambda i: (i, 0, 0)),
                  smem(lambda i: (jnp.minimum(i + 1, n_steps - 1), 0, 0)),
                  tok(LANES), tok(d),
                  pl.BlockSpec((None, 6, d), lambda i: (i // nblk, 0, 0)),
                  pl.BlockSpec((None, 6, d), lambda i: (nrow - 1, 0, 0)),
                  pl.BlockSpec((1, d), lambda i: (0, 0)), pl.BlockSpec((1, d), lambda i: (0, 0)),
                  pl.BlockSpec(memory_space=pl.ANY)],
        out_specs=tok(d),
        out_shape=jax.ShapeDtypeStruct((bsz * s, d), F32),
        scratch_shapes=[pltpu.VMEM((2, 2 * tm, d), F32), pltpu.SemaphoreType.DMA((2,))],
        compiler_params=_cparams(("arbitrary",)),
        name="moe_combine",
    )(pos_steps, pos_steps, route.reshape(bsz * s, LANES), x1.reshape(bsz * s, d), mod, mod,
      vec(ln_g), vec(ln_b), ys)
    return out.reshape(bsz, s, d)


def _moe(fx, x1, route, mod, w1, w3, w2, ln_g, ln_b, n_ctx_tokens, alpha):
    bsz, s, d = x1.shape
    n_exp = w1.shape[0]
    idx = route.reshape(bsz * s, LANES)[:, 2:4].astype(jnp.int32)
    row_token, pos, tile_expert, n_used = _route_plan(idx[:, 0], idx[:, 1], n_exp, MOE_TILE)
    ys = _moe_gemm(fx.reshape(bsz * s, d), row_token, tile_expert, n_used, w1, w3, w2, MOE_TILE)
    return _moe_combine(ys, pos, route, x1, mod, ln_g, ln_b, n_ctx_tokens, alpha)


def _pick_chunk(total, target):
    best = LANES
    for c in range(LANES, target + 1, LANES):
        if total % c == 0:
            best = c
    return best


def kernel(x, c, ctx, c_ctx, w_mod, b_mod, w_in, s5_lam_re, s5_lam_im, s5_log_step, s5_b_re, s5_b_im,
           s5_c_re, s5_c_im, s5_d, s5_w_glu, s5_b_glu, attn_sink, ret_log_gamma, w_out,
           ln1_g, ln1_b, ln2_g, ln2_b, ffn_w1, ffn_w3, ffn_w2, moe_router, moe_w1, moe_w3, moe_w2):
    bsz, t_len, d = x.shape
    n_ctx = ctx.shape[1]
    depth = w_in.shape[0]
    alpha = (2 * depth) ** 0.25
    s5_w = s5_d.shape[1]
    att_w = attn_sink.shape[1] * HEAD_DIM
    kv_w = att_w // ATT_REP
    ret_w = ret_log_gamma.shape[2] * HEAD_DIM
    sizes = (s5_w, att_w, kv_w, kv_w, ret_w, ret_w, ret_w, ret_w)
    assert sum(sizes) == w_in.shape[2] and s5_w + att_w + ret_w == w_out.shape[1]
    assert n_ctx % TOKEN_BLOCK == 0 and t_len % TOKEN_BLOCK == 0 and t_len >= 3 * ATT_BLOCK

    pad = (-(bsz + 1)) % 8
    cvec = jnp.concatenate([jnp.zeros((pad, d), F32), c_ctx[None].astype(F32)], axis=0)
    cvec = jnp.concatenate([c.astype(F32), cvec], axis=0)
    mod_all = _modulation(cvec, w_mod.astype(F32), b_mod.astype(F32)).reshape(depth, bsz + pad + 1, 6, d)

    tabs = _rope_tables(t_len)
    xz = jnp.concatenate([ctx, x], axis=1).astype(F32)
    for l in range(depth):
        need_ctx = l < depth - 1
        mod = mod_all[l]
        u, qa, va, ka, qr, kr, vr, gr = _inproj(xz, mod, w_in[l].astype(BF16), tabs, sizes, n_ctx)
        s5w = _s5_weights(s5_lam_re[l], s5_lam_im[l], s5_log_step[l], s5_b_re[l], s5_b_im[l],
                          s5_c_re[l], s5_c_im[l], S5_CHUNK)
        y_s5 = _s5_scan(u, s5w, n_ctx)
        o_att = _attention(qa, ka, va, attn_sink[l], n_ctx, need_ctx)
        o_ret = _retention(qr, kr, vr, gr, ret_log_gamma[l], n_ctx)
        i = l // 2
        router = None if l % 2 == 0 else moe_router[i]
        outs = _outproj(y_s5, u, o_att, o_ret, xz, mod, s5_d[l], s5_w_glu[l], s5_b_glu[l], w_out[l],
                        ln1_g[l], ln1_b[l], router, n_ctx, need_ctx, alpha)
        ctx_tokens = n_ctx if need_ctx else 0
        if l % 2 == 0:
            x1, fx = outs
            xz = _ffn(fx, x1, mod, ffn_w1[i].astype(BF16), ffn_w3[i].astype(BF16), ffn_w2[i].astype(BF16),
                      ln2_g[l], ln2_b[l], ctx_tokens, alpha)
        else:
            x1, fx, route = outs
            xz = _moe(fx, x1, route, mod, moe_w1[i].astype(BF16), moe_w3[i].astype(BF16),
                      moe_w2[i].astype(BF16), ln2_g[l], ln2_b[l], ctx_tokens, alpha)
    return xz if xz.shape[1] == t_len else xz[:, n_ctx:]
```

```python
import functools
import math

import jax
import jax.numpy as jnp
import numpy as np
from jax import lax
from jax.experimental import pallas as pl
from jax.experimental.pallas import tpu as pltpu

F32 = jnp.float32
BF16 = jnp.bfloat16

GRID_W = 64
HEAD_DIM = 64
S5_GROUP = 16
ATT_REP = 4
WINDOW = 128
ATT_BLOCK = 128
RET_CHUNK = 128
TOP_K = 2
LN_EPS = 1e-5
ROPE_BASE = 10000.0
NEG_INF = -1e30

LANES = 128
S5_CHUNK = LANES
TOKEN_BLOCK = 256
FFN_BLOCK = 768
VMEM_LIMIT = 56 * 1024 * 1024


def _cparams(sem):
    return pltpu.CompilerParams(dimension_semantics=sem, vmem_limit_bytes=VMEM_LIMIT)


def _dot(a, b):
    return jnp.dot(a, b, preferred_element_type=F32)


def _dot_nt(a, b):
    return lax.dot_general(a, b, (((1,), (1,)), ((), ())), preferred_element_type=F32)


def _dot_tn(a, b):
    return lax.dot_general(a, b, (((0,), (0,)), ((), ())), preferred_element_type=F32)


def _split_bf16(a):
    hi = a.astype(BF16)
    lo = (a - hi.astype(F32)).astype(BF16)
    return hi, lo


def _dot3(a, b):
    ah, al = _split_bf16(a)
    bh, bl = _split_bf16(b)
    return _dot(ah, bh) + (_dot(ah, bl) + _dot(al, bh))


def _ln(x):
    mu = jnp.mean(x, axis=-1, keepdims=True)
    xc = x - mu
    var = jnp.mean(xc * xc, axis=-1, keepdims=True)
    return xc * lax.rsqrt(var + LN_EPS)


def _silu(x):
    return x * (1.0 / (1.0 + jnp.exp(-x)))


def _sigmoid(x):
    return 1.0 / (1.0 + jnp.exp(-x))


def _gelu_tanh(x):
    c = math.sqrt(2.0 / math.pi)
    return 0.5 * x * (1.0 + jnp.tanh(c * (x + 0.044715 * (x * x * x))))


def _mod_kernel(c_ref, w_ref, b_ref, o_ref):
    o_ref[...] = _dot3(_silu(c_ref[...]), w_ref[...]) + b_ref[...]


def _modulation(cvec, w_mod, b_mod):
    depth, d, n = w_mod.shape
    rows = cvec.shape[0]
    tn = 1536
    return pl.pallas_call(
        _mod_kernel,
        grid=(depth, n // tn),
        in_specs=[pl.BlockSpec((rows, d), lambda l, j: (0, 0)),
                  pl.BlockSpec((None, d, tn), lambda l, j: (l, 0, j)),
                  pl.BlockSpec((None, 1, tn), lambda l, j: (l, 0, j))],
        out_specs=pl.BlockSpec((None, rows, tn), lambda l, j: (l, 0, j)),
        out_shape=jax.ShapeDtypeStruct((depth, rows, n), F32),
        compiler_params=_cparams(("parallel", "parallel")),
        name="modulation",
    )(cvec, w_mod, b_mod.reshape(depth, 1, n))


def _rope_slab(xs, cos, sa, sb, half):
    return xs * cos + pltpu.roll(xs, LANES - half, 1) * sa + pltpu.roll(xs, half, 1) * sb


def _inproj_kernel(x_ref, modx_ref, modz_ref, wc_ref, w_ref, act_ref, ast_ref, rct_ref, rst_ref,
                   ac_ref, asa_ref, asb_ref, rc_ref, rsa_ref, rsb_ref,
                   ut_ref, qat_ref, vat_ref, qrt_ref, vrt_ref, grt_ref, ka_ref, kr_ref, *, nzb, c_sizes, sizes):
    is_ctx = pl.program_id(1) < nzb
    mod = jnp.where(is_ctx, modz_ref[...], modx_ref[...])
    h = (_ln(x_ref[...]) * (1.0 + mod[1:2]) + mod[0:1]).astype(BF16)
    scale = HEAD_DIM ** -0.5

    ct = _dot_nt(wc_ref[...], h)
    c_offs = [int(o) for o in np.concatenate([[0], np.cumsum(c_sizes)])]

    def chan(i):
        return ct[c_offs[i]:c_offs[i + 1]]

    def rope_t(x, cos_ref, sin_ref, part, mul):
        cos_t = jnp.where(is_ctx, 1.0, cos_ref[...])
        sin_t = jnp.where(is_ctx, 0.0, sin_ref[...])
        heads = []
        for hd in range(x.shape[0] // HEAD_DIM):
            xh = x[hd * HEAD_DIM:(hd + 1) * HEAD_DIM]
            pieces = []
            for lo in range(0, HEAD_DIM, 2 * part):
                pieces += [xh[lo + part:lo + 2 * part], xh[lo:lo + part]]
            out = xh * cos_t + jnp.concatenate(pieces, axis=0) * sin_t
            heads.append(out * mul if mul != 1.0 else out)
        return jnp.concatenate(heads, axis=0)

    def put(ref, val):
        for k in range(ref.shape[0]):
            ref[k] = val[:, k * ATT_BLOCK:(k + 1) * ATT_BLOCK]

    ut_ref[...] = chan(0)
    put(qat_ref, rope_t(chan(1), act_ref, ast_ref, HEAD_DIM // 4, scale))
    put(vat_ref, chan(2))
    put(qrt_ref, rope_t(chan(3), rct_ref, rst_ref, HEAD_DIM // 2, 1.0))
    put(vrt_ref, chan(4))
    put(grt_ref, chan(5))

    p = _dot(h, w_ref[...])
    ac = jnp.where(is_ctx, 1.0, ac_ref[...])
    asa = jnp.where(is_ctx, 0.0, asa_ref[...])
    asb = jnp.where(is_ctx, 0.0, asb_ref[...])
    rc = jnp.where(is_ctx, 1.0, rc_ref[...])
    rsa = jnp.where(is_ctx, 0.0, rsa_ref[...])
    rsb = jnp.where(is_ctx, 0.0, rsb_ref[...])
    offs = np.concatenate([[0], np.cumsum(sizes)])

    def rope_cols(i, cos, sa, sb, half, mul):
        blk = p[:, int(offs[i]):int(offs[i + 1])]
        slabs = [_rope_slab(blk[:, s:s + LANES], cos, sa, sb, half) for s in range(0, blk.shape[1], LANES)]
        out = slabs[0] if len(slabs) == 1 else jnp.concatenate(slabs, axis=1)
        return out * mul if mul != 1.0 else out

    ka_ref[...] = rope_cols(0, ac, asa, asb, HEAD_DIM // 4, 1.0)
    kr_ref[...] = rope_cols(1, rc, rsa, rsb, HEAD_DIM // 2, scale)


def _inproj(xz, mod, w_in, tabs, sizes, n_ctx):
    bsz, s, d = xz.shape
    tm = TOKEN_BLOCK
    nzb = n_ctx // tm
    nrow = mod.shape[0]
    offs = np.concatenate([[0], np.cumsum(sizes)])
    col = lambda i: w_in[:, int(offs[i]):int(offs[i + 1])]
    c_idx, r_idx = (0, 1, 3, 4, 6, 7), (2, 5)
    c_sizes = tuple(sizes[i] for i in c_idx)
    r_sizes = tuple(sizes[i] for i in r_idx)
    w_c_t = jnp.concatenate([col(i) for i in c_idx], axis=1).T
    w_rest = jnp.concatenate([col(i) for i in r_idx], axis=1)
    sub = tm // ATT_BLOCK
    tok = lambda width: pl.BlockSpec((None, tm, width), lambda b, j: (b, j, 0))
    chan = lambda width: pl.BlockSpec((sub, None, width, ATT_BLOCK), lambda b, j: (j, b, 0, 0))
    tab = pl.BlockSpec((tm, LANES), lambda b, j: (jnp.maximum(j - nzb, 0), 0))
    tab_t = pl.BlockSpec((HEAD_DIM, tm), lambda b, j: (0, jnp.maximum(j - nzb, 0)))
    return pl.pallas_call(
        functools.partial(_inproj_kernel, nzb=nzb, c_sizes=c_sizes, sizes=r_sizes),
        grid=(bsz, s // tm),
        in_specs=[tok(d),
                  pl.BlockSpec((None, 6, d), lambda b, j: (b, 0, 0)),
                  pl.BlockSpec((None, 6, d), lambda b, j: (nrow - 1, 0, 0)),
                  pl.BlockSpec((sum(c_sizes), d), lambda b, j: (0, 0)),
                  pl.BlockSpec((d, sum(r_sizes)), lambda b, j: (0, 0)),
                  tab_t, tab_t, tab_t, tab_t, tab, tab, tab, tab, tab, tab],
        out_specs=[pl.BlockSpec((None, None, c_sizes[0], tm), lambda b, j: (j, b, 0, 0))]
                  + [chan(w) for w in c_sizes[1:]] + [tok(w) for w in r_sizes],
        out_shape=[jax.ShapeDtypeStruct((s // tm, bsz, c_sizes[0], tm), F32)]
                  + [jax.ShapeDtypeStruct((s // ATT_BLOCK, bsz, w, ATT_BLOCK), F32) for w in c_sizes[1:]]
                  + [jax.ShapeDtypeStruct((bsz, s, w), F32) for w in r_sizes],
        compiler_params=_cparams(("parallel", "parallel")),
        name="inproj",
    )(xz, mod, mod, w_c_t, w_rest, *tabs)


def _rope_tables(t_len):
    t = jnp.arange(t_len)
    rows = (t // GRID_W).astype(F32)
    cols = (t % GRID_W).astype(F32)
    pos = t.astype(F32)

    def angles(p, dim):
        inv_freq = ROPE_BASE ** (-jnp.arange(0, dim, 2, dtype=F32) / dim)
        return p[:, None] * inv_freq[None, :]

    def head_tables(angs):
        cos = jnp.concatenate([jnp.concatenate([jnp.cos(a), jnp.cos(a)], -1) for a in angs], -1)
        sa = jnp.concatenate([jnp.concatenate([-jnp.sin(a), jnp.zeros_like(a)], -1) for a in angs], -1)
        sb = jnp.concatenate([jnp.concatenate([jnp.zeros_like(a), jnp.sin(a)], -1) for a in angs], -1)
        rep = LANES // HEAD_DIM
        return tuple(jnp.tile(x, (1, rep)) for x in (cos, sa, sb))

    att = head_tables([angles(rows, HEAD_DIM // 2), angles(cols, HEAD_DIM // 2)])
    ret = head_tables([angles(pos, HEAD_DIM)])
    signed_t = lambda tb: (tb[0][:, :HEAD_DIM].T, (tb[1] + tb[2])[:, :HEAD_DIM].T)
    return signed_t(att) + signed_t(ret) + att + ret


def _s5_kernel(u_ref, tab_ref, wsf_ref, wsb_ref, wrf_ref, wrb_ref, lam_ref, y_ref,
               lhs_ref, m_ref, acc_ref, cf_ref, cb_ref, pf_ref, pb_ref, *, bsz, nz_chunks):
    r_blk, ch, tm = u_ref.shape
    c = S5_CHUNK
    kpb = tm // c
    n_chunks = (r_blk // bsz) * kpb

    for i in range(ch):
        x = u_ref[:, i, :].astype(BF16)
        p, half = divmod(i, 2)
        for k in range(kpb):
            lhs_ref[p, k * r_blk:(k + 1) * r_blk, half * c:(half + 1) * c] = x[:, k * c:(k + 1) * c]
    acc_ref[...] = jnp.zeros_like(acc_ref)
    cf_ref[...] = jnp.zeros_like(cf_ref)
    cb_ref[...] = jnp.zeros_like(cb_ref)

    def pair(p, carry):
        for ih in range(2):
            for o in range(ch):
                tab = tab_ref[2 * p + ih, o]
                for r2 in range(c // 16):
                    lo = c - 16 * r2
                    rows = jnp.concatenate([tab[:, lo:lo + c], tab[:, lo - 8:lo - 8 + c]], axis=0)
                    m_ref[ih * c + 16 * r2:ih * c + 16 * r2 + 16, o * c:(o + 1) * c] = rows.astype(BF16)
        lhs = lhs_ref[p]
        acc_ref[...] += _dot(lhs, m_ref[...])
        cf_ref[...] += _dot(lhs, wsf_ref[p])
        cb_ref[...] += _dot(lhs, wsb_ref[p])
        return carry

    lax.fori_loop(0, ch // 2, pair, 0)
    lam = lam_ref[...]
    nst = lam.shape[1] // 2

    def advance(s, a, bc, drive):
        return s * a + pltpu.roll(s, nst, 1) * bc + drive

    def sweep(order, c_ref, p_ref, a, bc):
        s = jnp.zeros((bsz, lam.shape[1]), F32)
        for n in order:
            j, k = divmod(n, kpb)
            rows = slice(k * r_blk + j * bsz, k * r_blk + (j + 1) * bsz)
            p_ref[rows, :] = s
            s = advance(s, a, bc, c_ref[rows, :])

    sweep(list(range(n_chunks)), cf_ref, pf_ref, lam[0:1], lam[1:2])
    order_b = list(range(nz_chunks - 1, -1, -1)) + list(range(n_chunks - 1, nz_chunks - 1, -1))
    sweep(order_b, cb_ref, pb_ref, lam[2:3], lam[3:4])
    acc_ref[...] += (_dot(pf_ref[...].astype(BF16), wrf_ref[...])
                     + _dot(pb_ref[...].astype(BF16), wrb_ref[...]))
    for o in range(ch):
        for k in range(kpb):
            y_ref[:, o, k * c:(k + 1) * c] = acc_ref[k * r_blk:(k + 1) * r_blk, o * c:(o + 1) * c]


def _s5_weights(lam_re, lam_im, log_step, b_re, b_im, c_re, c_im, chunk):
    hp = lax.Precision.HIGHEST
    lam = lax.complex(lam_re.astype(F32), lam_im.astype(F32))
    lam_dt = lam * jnp.exp(log_step.astype(F32))[..., None]
    lam_bar = jnp.exp(lam_dt)
    b_bar = lax.complex(b_re.astype(F32), b_im.astype(F32)) * ((lam_bar - 1.0) / lam)[..., None]
    c_mat = lax.complex(c_re.astype(F32), c_im.astype(F32))
    g, n = lam.shape[1], lam.shape[2]
    ch = b_bar.shape[-1]
    steps = jnp.arange(chunk + 1, dtype=F32)
    pw = jnp.exp(steps[None, :, None, None] * lam_dt[:, None])
    kern = jnp.einsum('zgon,zdgn,zgni->zgdoi', c_mat, pw[:, :chunk], b_bar, precision=hp).real
    zero_lag = kern[0, :, 0] + kern[1, :, 0]
    lag_table = jnp.concatenate([jnp.zeros_like(zero_lag)[:, None], kern[1, :, :0:-1], zero_lag[:, None],
                                 kern[0, :, 1:]], axis=1).transpose(0, 3, 2, 1)
    tab = jnp.stack([jnp.roll(lag_table, q, axis=-1) for q in range(8)], axis=3)

    def state_in(pw_s, b_dir):
        w = pw_s[:, :, :, None] * b_dir[None]
        w = jnp.concatenate([w.real, w.imag], axis=2)
        return w.transpose(1, 3, 0, 2).reshape(g, ch // 2, 2 * chunk, 2 * n)

    def state_out(pw_t, c_dir):
        w = c_dir[None] * pw_t[:, :, None, :]
        w = jnp.concatenate([w.real, -w.imag], axis=3)
        return w.transpose(1, 3, 2, 0).reshape(g, 2 * n, ch * chunk)

    wsf = state_in(pw[0, chunk - 1::-1][:chunk], b_bar[0])
    wsb = state_in(pw[1, :chunk], b_bar[1])
    wrf = state_out(pw[0, 1:chunk + 1], c_mat[0])
    wrb = state_out(pw[1, chunk:0:-1], c_mat[1])
    lam_c = pw[:, chunk]
    rows = []
    for z in range(2):
        rows.append(jnp.concatenate([lam_c[z].real, lam_c[z].real], -1))
        rows.append(jnp.concatenate([-lam_c[z].imag, lam_c[z].imag], -1))
    lam_rows = jnp.stack(rows + [jnp.zeros_like(rows[0])] * 4, axis=1)
    return tab, wsf.astype(BF16), wsb.astype(BF16), wrf.astype(BF16), wrb.astype(BF16), lam_rows


def _s5_scan(ut, weights, n_ctx):
    nblk, bsz, width, tm = ut.shape
    tab, wsf, wsb, wrf, wrb, lam_rows = weights
    g, ch = tab.shape[0], tab.shape[1]
    c = S5_CHUNK
    nst2 = wsf.shape[3]
    r_blk = nblk * bsz
    rows = r_blk * (tm // c)
    grp = lambda *shape: pl.BlockSpec((None,) + shape, lambda i: (i,) + (0,) * len(shape))
    tok = pl.BlockSpec((r_blk, ch, tm), lambda i: (0, i, 0))
    y = pl.pallas_call(
        functools.partial(_s5_kernel, bsz=bsz, nz_chunks=n_ctx // c),
        grid=(g,),
        in_specs=[tok, grp(ch, ch, 8, 2 * c), grp(ch // 2, 2 * c, nst2), grp(ch // 2, 2 * c, nst2),
                  grp(nst2, ch * c), grp(nst2, ch * c), grp(8, nst2)],
        out_specs=tok,
        out_shape=jax.ShapeDtypeStruct((r_blk, width, tm), F32),
        scratch_shapes=[pltpu.VMEM((ch // 2, rows, 2 * c), BF16),
                        pltpu.VMEM((2 * c, ch * c), BF16),
                        pltpu.VMEM((rows, ch * c), F32)] + [pltpu.VMEM((rows, nst2), F32)] * 4,
        compiler_params=_cparams(("parallel",)),
        name="s5_scan",
    )(ut.reshape(r_blk, width, tm), tab, wsf, wsb, wrf, wrb, lam_rows)
    return y.reshape(nblk, bsz, width, tm)


def _attn_kernel(sink_ref, q_ref, k_ref, v_ref, o_ref, *, n_ctx, t_len, q_off, kv_heads):
    qi = pl.program_id(1) + q_off
    nzb = n_ctx // ATT_BLOCK
    band = 3 * ATT_BLOCK
    wide = ATT_REP * ATT_BLOCK
    q = q_ref[...].astype(BF16)
    kc = k_ref[0:n_ctx, :].astype(BF16)
    vc = jnp.concatenate([v_ref[p] for p in range(nzb)], axis=1).astype(BF16)

    def rows(a, i):
        return a[i * HEAD_DIM:(i + 1) * HEAD_DIM]

    def q_group(g):
        qg = jnp.concatenate([rows(q, g * ATT_REP + r) for r in range(ATT_REP)], axis=1)
        zero = jnp.zeros_like(qg)
        return jnp.concatenate([qg if gg == g else zero for gg in range(kv_heads)], axis=0)

    def finish(g, o, den):
        o = o / den
        for r in range(ATT_REP):
            h = g * ATT_REP + r
            o_ref[h * HEAD_DIM:(h + 1) * HEAD_DIM, :] = o[:, r * ATT_BLOCK:(r + 1) * ATT_BLOCK]

    @pl.when(qi < nzb)
    def _():
        for g in range(kv_heads):
            sink = sink_ref[g:g + 1, :]
            s = _dot(kc, q_group(g))
            m = jnp.maximum(jnp.max(s, axis=0, keepdims=True), sink)
            e = jnp.exp(s - m)
            den = jnp.sum(e, axis=0, keepdims=True) + jnp.exp(sink - m)
            finish(g, _dot(rows(vc, g), e.astype(BF16)), den)

    @pl.when(qi >= nzb)
    def _():
        n = qi - nzb
        start = jnp.clip((n - 1) * ATT_BLOCK, 0, t_len - band)
        kl = k_ref[pl.ds(pl.multiple_of(n_ctx + start, ATT_BLOCK), band), :].astype(BF16)
        p0 = (n_ctx + start) // ATT_BLOCK
        vl = jnp.concatenate([v_ref[p0 + t] for t in range(3)], axis=1).astype(BF16)
        k_pos = start + lax.broadcasted_iota(jnp.int32, (band, wide), 0)
        q_pos = n * ATT_BLOCK + (lax.broadcasted_iota(jnp.int32, (band, wide), 1) & (ATT_BLOCK - 1))
        valid = jnp.abs(k_pos - q_pos) <= WINDOW
        for g in range(kv_heads):
            sink = sink_ref[g:g + 1, :]
            qg = q_group(g)
            s_loc = jnp.where(valid, _dot(kl, qg), NEG_INF)
            s_ctx = _dot(kc, qg)
            m = jnp.maximum(jnp.maximum(jnp.max(s_loc, axis=0, keepdims=True),
                                        jnp.max(s_ctx, axis=0, keepdims=True)), sink)
            e_loc = jnp.exp(s_loc - m)
            e_ctx = jnp.exp(s_ctx - m)
            den = (jnp.sum(e_loc, axis=0, keepdims=True) + jnp.sum(e_ctx, axis=0, keepdims=True)
                   + jnp.exp(sink - m))
            o = _dot(rows(vl, g), e_loc.astype(BF16)) + _dot(rows(vc, g), e_ctx.astype(BF16))
            finish(g, o, den)


def _attention(qat, ka, vat, sink, n_ctx, need_ctx):
    nblk, bsz, qw, _ = qat.shape
    s, kvw = ka.shape[1], ka.shape[2]
    t_len = s - n_ctx
    kv_heads = kvw // HEAD_DIM
    q_off = 0 if need_ctx else n_ctx // ATT_BLOCK
    nq = nblk - q_off
    sink_rows = jnp.repeat(sink.astype(F32).reshape(kv_heads, ATT_REP), ATT_BLOCK, axis=1)
    return pl.pallas_call(
        functools.partial(_attn_kernel, n_ctx=n_ctx, t_len=t_len, q_off=q_off, kv_heads=kv_heads),
        grid=(bsz, nq),
        in_specs=[pl.BlockSpec(sink_rows.shape, lambda b, j: (0, 0)),
                  pl.BlockSpec((None, None, qw, ATT_BLOCK), lambda b, j: (j + q_off, b, 0, 0)),
                  pl.BlockSpec((None, s, kvw), lambda b, j: (b, 0, 0)),
                  pl.BlockSpec((nblk, None, kvw, ATT_BLOCK), lambda b, j: (0, b, 0, 0))],
        out_specs=pl.BlockSpec((None, None, qw, ATT_BLOCK), lambda b, j: (j, b, 0, 0)),
        out_shape=jax.ShapeDtypeStruct((nq, bsz, qw, ATT_BLOCK), F32),
        compiler_params=_cparams(("parallel", "arbitrary")),
        name="window_attention",
    )(sink_rows, qat, ka, vat)


def _ret_kernel(lg_ref, q_ref, k_ref, v_ref, g_ref, o_ref, acc_ref, sf_ref, sb_ref, *, nz_chunks, heads):
    c = RET_CHUNK
    n_chunks, w, _ = q_ref.shape
    f32 = lambda a: a.astype(F32)

    def per_head(shape, axis, group, direction):
        owner = lax.broadcasted_iota(jnp.int32, shape, axis) // group
        out = jnp.zeros(shape, F32)
        for h in range(heads):
            out = jnp.where(owner == h, lg_ref[direction, h], out)
        return out

    row_i = f32(lax.broadcasted_iota(jnp.int32, (w, c), 1))
    qw_f = jnp.exp(per_head((w, c), 0, HEAD_DIM, 0) * (row_i + 1.0))
    qw_b = jnp.exp(per_head((w, c), 0, HEAD_DIM, 1) * (c - row_i))
    key_j = f32(lax.broadcasted_iota(jnp.int32, (c, w), 0))
    kw_f = jnp.exp(per_head((c, w), 1, HEAD_DIM, 0) * (c - 1.0 - key_j))
    kw_b = jnp.exp(per_head((c, w), 1, HEAD_DIM, 1) * key_j)
    dec_f = jnp.exp(per_head((w, 1), 0, HEAD_DIM, 0) * c)
    dec_b = jnp.exp(per_head((w, 1), 0, HEAD_DIM, 1) * c)
    wide = heads * c
    diff = f32((lax.broadcasted_iota(jnp.int32, (c, wide), 1) & (c - 1)) - lax.broadcasted_iota(jnp.int32, (c, wide), 0))
    dec_t = jnp.where(diff >= 0, jnp.exp(per_head((c, wide), 1, c, 0) * jnp.maximum(diff, 0.0)),
                      jnp.exp(per_head((c, wide), 1, c, 1) * jnp.maximum(-diff, 0.0)))
    own_wide = (lax.broadcasted_iota(jnp.int32, (w, wide), 0) // HEAD_DIM
                == lax.broadcasted_iota(jnp.int32, (w, wide), 1) // c)
    own_sq = f32(lax.broadcasted_iota(jnp.int32, (w, w), 0) // HEAD_DIM
                 == lax.broadcasted_iota(jnp.int32, (w, w), 1) // HEAD_DIM)
    sf_ref[...] = jnp.zeros_like(sf_ref)
    sb_ref[...] = jnp.zeros_like(sb_ref)

    def block_diag(x):
        return jnp.where(own_wide, jnp.concatenate([x] * heads, axis=1), 0.0).astype(BF16)

    def fwd(n, carry):
        qt, vt = q_ref[n], v_ref[n]
        k = k_ref[pl.ds(pl.multiple_of(n * c, c), c), :]
        scores_t = _dot(k.astype(BF16), block_diag(qt)) * dec_t
        stacked = jnp.concatenate([scores_t[:, h * c:(h + 1) * c] for h in range(heads)], axis=0)
        o = _dot(block_diag(vt), stacked.astype(BF16))
        s_prev = sf_ref[...]
        o = o + _dot(s_prev.astype(BF16), (qt * qw_f).astype(BF16))
        sf_ref[...] = dec_f * s_prev + own_sq * _dot(vt.astype(BF16), (k * kw_f).astype(BF16))
        acc_ref[n] = o
        return carry

    lax.fori_loop(0, n_chunks, fwd, 0)

    def bwd(i, carry):
        n = jnp.where(i < nz_chunks, nz_chunks - 1 - i, n_chunks - 1 - i + nz_chunks)
        qt, vt = q_ref[n], v_ref[n]
        k = k_ref[pl.ds(pl.multiple_of(n * c, c), c), :]
        s_prev = sb_ref[...]
        o = acc_ref[n] + _dot(s_prev.astype(BF16), (qt * qw_b).astype(BF16))
        sb_ref[...] = dec_b * s_prev + own_sq * _dot(vt.astype(BF16), (k * kw_b).astype(BF16))
        gate = _silu(g_ref[n])
        for h in range(heads):
            rows = slice(h * HEAD_DIM, (h + 1) * HEAD_DIM)
            oh = o[rows]
            mu = jnp.mean(oh, axis=0, keepdims=True)
            oc = oh - mu
            var = jnp.mean(oc * oc, axis=0, keepdims=True)
            o_ref[n, rows, :] = oc * lax.rsqrt(var + LN_EPS) * gate[rows]
        return carry

    lax.fori_loop(0, n_chunks, bwd, 0)


def _retention(qrt, kr, vrt, grt, log_gamma, n_ctx):
    n_chunks, bsz, w, c = qrt.shape
    s = kr.shape[1]
    chan = pl.BlockSpec((n_chunks, None, w, c), lambda b: (0, b, 0, 0))
    return pl.pallas_call(
        functools.partial(_ret_kernel, nz_chunks=n_ctx // RET_CHUNK, heads=w // HEAD_DIM),
        grid=(bsz,),
        in_specs=[pl.BlockSpec(memory_space=pltpu.SMEM), chan, pl.BlockSpec((None, s, w), lambda b: (b, 0, 0)),
                  chan, chan],
        out_specs=chan,
        out_shape=jax.ShapeDtypeStruct((n_chunks, bsz, w, c), F32),
        scratch_shapes=[pltpu.VMEM((n_chunks, w, c), F32), pltpu.VMEM((w, w), F32), pltpu.VMEM((w, w), F32)],
        compiler_params=_cparams(("parallel",)),
        name="retention",
    )(log_gamma.astype(F32), qrt, kr, vrt, grt)


def _outproj_kernel(*refs, nzb, alpha, with_router, w_s5, w_att, n_exp):
    if with_router:
        (y_ref, u_ref, a_ref, r_ref, x_ref, modx_ref, modz_ref, d_ref, wg_ref, bg_ref, wo_ref,
         g1_ref, b1_ref, rt_ref, x1_ref, fx_ref, gate_ref) = refs
    else:
        (y_ref, u_ref, a_ref, r_ref, x_ref, modx_ref, modz_ref, d_ref, wg_ref, bg_ref, wo_ref,
         g1_ref, b1_ref, x1_ref, fx_ref) = refs
    is_ctx = pl.program_id(1) < nzb
    mod = jnp.where(is_ctx, modz_ref[...], modx_ref[...])
    g = _gelu_tanh(y_ref[...] + d_ref[...] * u_ref[...])
    s5 = g * _sigmoid(_dot(wg_ref[...], g.astype(BF16)) + bg_ref[...])
    def chan_major(ref, lo, hi):
        return jnp.concatenate([_dot_tn(ref[k].astype(BF16), wo_ref[lo:hi, :]) for k in range(ref.shape[0])],
                               axis=0)

    mix = (_dot_tn(s5.astype(BF16), wo_ref[0:w_s5, :]) + chan_major(a_ref, w_s5, w_s5 + w_att)
           + chan_major(r_ref, w_s5 + w_att, wo_ref.shape[0]))
    x1 = _ln(alpha * x_ref[...] + mod[2:3] * mix) * g1_ref[...] + b1_ref[...]
    x1_ref[...] = x1
    fx = _ln(x1) * (1.0 + mod[4:5]) + mod[3:4]
    fx_ref[...] = fx.astype(fx_ref.dtype)
    if with_router:
        lane = lax.broadcasted_iota(jnp.int32, (fx.shape[0], LANES), 1)
        logits = jnp.where(lane < n_exp, _dot3(fx, rt_ref[...]), -jnp.inf)
        m1 = jnp.max(logits, axis=1, keepdims=True)
        i1 = jnp.min(jnp.where(logits == m1, lane, LANES), axis=1, keepdims=True)
        rest = jnp.where(lane == i1, -jnp.inf, logits)
        m2 = jnp.max(rest, axis=1, keepdims=True)
        i2 = jnp.min(jnp.where(rest == m2, lane, LANES), axis=1, keepdims=True)
        e2 = jnp.exp(m2 - m1)
        den = 1.0 + e2
        route = jnp.where(lane == 0, 1.0 / den, jnp.where(lane == 1, e2 / den, 0.0))
        route = jnp.where(lane == 2, i1.astype(F32), jnp.where(lane == 3, i2.astype(F32), route))
        gate_ref[...] = route


def _outproj(y_s5, u, o_att, o_ret, xz, mod, s5_d, w_glu, b_glu, w_out, ln_g, ln_b, router, n_ctx, need_ctx,
             alpha):
    bsz, s, d = xz.shape
    tm = TOKEN_BLOCK
    nzb = n_ctx // tm
    off = 0 if need_ctx else nzb
    nblk = s // tm - off
    s_out = nblk * tm
    nrow = mod.shape[0]
    w_s5, w_att = y_s5.shape[2], o_att.shape[2]
    sub = tm // ATT_BLOCK
    att_off = off if o_att.shape[0] == s // ATT_BLOCK else 0
    att = pl.BlockSpec((sub, None, w_att, ATT_BLOCK), lambda b, j: (j + att_off, b, 0, 0))

    def tok(width, shift):
        return pl.BlockSpec((None, tm, width), lambda b, j: (b, j + shift, 0))

    def full(a):
        return pl.BlockSpec(a.shape, lambda b, j: (0,) * a.ndim)

    chan = pl.BlockSpec((None, None, w_s5, tm), lambda b, j: (j + off, b, 0, 0))
    vec = lambda a: a.reshape(1, -1).astype(F32)
    col = lambda a: a.reshape(-1, 1).astype(F32)
    consts = [col(s5_d), w_glu.T.astype(BF16), col(b_glu), w_out.astype(BF16), vec(ln_g), vec(ln_b)]
    ret = pl.BlockSpec((sub, None, o_ret.shape[2], ATT_BLOCK), lambda b, j: (j + off, b, 0, 0))
    in_specs = [chan, chan, att, ret, tok(d, off),
                pl.BlockSpec((None, 6, d), lambda b, j: (b, 0, 0)),
                pl.BlockSpec((None, 6, d), lambda b, j: (nrow - 1, 0, 0))] + [full(a) for a in consts]
    out_specs = [tok(d, 0), tok(d, 0)]
    fx_dtype = BF16 if router is None else F32
    out_shape = [jax.ShapeDtypeStruct((bsz, s_out, d), F32), jax.ShapeDtypeStruct((bsz, s_out, d), fx_dtype)]
    args = [y_s5, u, o_att, o_ret, xz, mod, mod] + consts
    with_router = router is not None
    n_exp = 0
    if with_router:
        n_exp = router.shape[1]
        router_pad = jnp.pad(router.astype(F32), ((0, 0), (0, LANES - n_exp)))
        args.append(router_pad)
        in_specs.append(full(router_pad))
        out_specs.append(tok(LANES, 0))
        out_shape.append(jax.ShapeDtypeStruct((bsz, s_out, LANES), F32))
    return pl.pallas_call(
        functools.partial(_outproj_kernel, nzb=nzb - off, alpha=alpha, with_router=with_router,
                          w_s5=w_s5, w_att=w_att, n_exp=n_exp),
        grid=(bsz, nblk),
        in_specs=in_specs,
        out_specs=out_specs,
        out_shape=out_shape,
        compiler_params=_cparams(("parallel", "parallel")),
        name="outproj",
    )(*args)


def _swiglu_into(x_bf16, w1_ref, w3_ref, w2_ref, acc_ref, fc):
    for s in range(0, w1_ref.shape[1], fc):
        h1 = _dot(x_bf16, w1_ref[:, s:s + fc])
        h3 = _dot(x_bf16, w3_ref[:, s:s + fc])
        acc_ref[...] += _dot((_silu(h1) * h3).astype(BF16), w2_ref[s:s + fc, :])


def _ffn_kernel(fx_ref, x1_ref, modx_ref, modz_ref, w1_ref, w3_ref, w2_ref, g2_ref, b2_ref,
                o_ref, acc_ref, *, n_ctx_tokens, alpha, fc):
    tm = acc_ref.shape[0]
    acc_ref[...] = jnp.zeros_like(acc_ref)
    _swiglu_into(fx_ref[...], w1_ref, w3_ref, w2_ref, acc_ref, fc)
    row = pl.program_id(1) * tm + lax.broadcasted_iota(jnp.int32, (tm, 1), 0)
    gate = jnp.where(row < n_ctx_tokens, modz_ref[5:6, :], modx_ref[5:6, :])
    o_ref[...] = _ln(alpha * x1_ref[...] + gate * acc_ref[...]) * g2_ref[...] + b2_ref[...]


def _ffn(fx, x1, mod, w1, w3, w2, ln_g, ln_b, n_ctx_tokens, alpha):
    bsz, s, d = x1.shape
    ff = w1.shape[1]
    tm = _pick_chunk(s, FFN_BLOCK)
    nrow = mod.shape[0]
    tok = pl.BlockSpec((None, tm, d), lambda b, j: (b, j, 0))
    full = lambda a: pl.BlockSpec(a.shape, lambda b, j: (0,) * a.ndim, pipeline_mode=pl.Buffered(1))
    vec = lambda a: a.reshape(1, -1).astype(F32)
    return pl.pallas_call(
        functools.partial(_ffn_kernel, n_ctx_tokens=n_ctx_tokens, alpha=alpha, fc=_pick_chunk(ff, 512)),
        grid=(bsz, s // tm),
        in_specs=[tok, tok,
                  pl.BlockSpec((None, 6, d), lambda b, j: (b, 0, 0)),
                  pl.BlockSpec((None, 6, d), lambda b, j: (nrow - 1, 0, 0)),
                  full(w1), full(w3), full(w2),
                  pl.BlockSpec((1, d), lambda b, j: (0, 0)), pl.BlockSpec((1, d), lambda b, j: (0, 0))],
        out_specs=tok,
        out_shape=jax.ShapeDtypeStruct((bsz, s, d), F32),
        scratch_shapes=[pltpu.VMEM((tm, d), F32)],
        compiler_params=_cparams(("parallel", "parallel")),
        name="dense_ffn",
    )(fx, x1, mod, mod, w1, w3, w2, vec(ln_g), vec(ln_b))


MOE_TILE = 1024


def _route_plan(e1, e2, n_exp, tile):
    n = e1.shape[0]
    pair_e = jnp.stack([e1, e2], axis=1).reshape(-1)
    onehot = (pair_e[:, None] == jnp.arange(n_exp, dtype=jnp.int32)[None, :]).astype(jnp.int32)
    before = jnp.cumsum(onehot, axis=0) - onehot
    rank = jnp.sum(before * onehot, axis=1)
    counts = jnp.sum(onehot, axis=0)
    padded = (counts + tile - 1) // tile * tile
    ends = jnp.cumsum(padded)
    starts = ends - padded
    dest = starts[pair_e] + rank
    n_rows = (2 * n + n_exp * (tile - 1)) // tile * tile
    n_tiles = n_rows // tile
    row_token = jnp.zeros((n_rows,), jnp.int32).at[dest].set(jnp.arange(2 * n, dtype=jnp.int32) // 2)
    tile_start = jnp.arange(n_tiles, dtype=jnp.int32) * tile
    tile_expert = jnp.minimum(jnp.sum((tile_start[:, None] >= ends[None, :]).astype(jnp.int32), axis=1),
                              n_exp - 1)
    n_used = (ends[-1] // tile).astype(jnp.int32).reshape(1)
    return row_token, dest.reshape(n, 2), tile_expert, n_used


def _moe_gemm_kernel(te_ref, nused_ref, tok_ref, tokn_ref, x_hbm, w1_ref, w3_ref, w2_ref, y_ref,
                     xbuf_ref, sem, *, fc, tile):
    i = pl.program_id(0)
    f = pl.program_id(1)
    slot = i % 2
    n_used = nused_ref[0]

    def row_copy(t_ref, r, s):
        return pltpu.make_async_copy(x_hbm.at[pl.ds(t_ref[0, r], 1)], xbuf_ref.at[s, pl.ds(r, 1)], sem.at[s])

    def issue(t_ref, s):
        def body(r, carry):
            row_copy(t_ref, r, s).start()
            return carry
        lax.fori_loop(0, tile, body, 0, unroll=8)

    @pl.when(f == 0)
    def _():
        y_ref[...] = jnp.zeros_like(y_ref)

        @pl.when(i == 0)
        def _():
            issue(tok_ref, 0)

        @pl.when(i + 1 < n_used)
        def _():
            issue(tokn_ref, 1 - slot)

        @pl.when(i < n_used)
        def _():
            pltpu.make_async_copy(x_hbm.at[pl.ds(0, tile)], xbuf_ref.at[slot], sem.at[slot]).wait()

    @pl.when(i < n_used)
    def _():
        _swiglu_into(xbuf_ref[slot].astype(BF16), w1_ref, w3_ref, w2_ref, y_ref, fc)


def _moe_gemm(x_flat, row_token, tile_expert, n_used, w1, w3, w2, tile):
    n_rows = row_token.shape[0]
    n_tiles = n_rows // tile
    d = x_flat.shape[1]
    ff = w1.shape[2]
    nf = 2 if ff % (2 * LANES) == 0 else 1
    tf = ff // nf
    last = lambda i, nu: jnp.minimum(i, nu[0] - 1)
    fsel = lambda i, f, nu: jnp.where(i < nu[0], f, nf - 1)
    tok = lambda imap: pl.BlockSpec((None, 1, tile), imap, memory_space=pltpu.SMEM)
    tokens = row_token.reshape(n_tiles, 1, tile)
    return pl.pallas_call(
        functools.partial(_moe_gemm_kernel, fc=_pick_chunk(tf, 256), tile=tile),
        grid_spec=pltpu.PrefetchScalarGridSpec(
            num_scalar_prefetch=2,
            grid=(n_tiles, nf),
            in_specs=[tok(lambda i, f, te, nu: (i, 0, 0)),
                      tok(lambda i, f, te, nu: (jnp.minimum(i + 1, n_tiles - 1), 0, 0)),
                      pl.BlockSpec(memory_space=pl.ANY),
                      pl.BlockSpec((None, d, tf), lambda i, f, te, nu: (te[last(i, nu)], 0, fsel(i, f, nu))),
                      pl.BlockSpec((None, d, tf), lambda i, f, te, nu: (te[last(i, nu)], 0, fsel(i, f, nu))),
                      pl.BlockSpec((None, tf, d), lambda i, f, te, nu: (te[last(i, nu)], fsel(i, f, nu), 0))],
            out_specs=pl.BlockSpec((tile, d), lambda i, f, te, nu: (i, 0)),
            scratch_shapes=[pltpu.VMEM((2, tile, d), F32), pltpu.SemaphoreType.DMA((2,))]),
        out_shape=jax.ShapeDtypeStruct((n_rows, d), F32),
        compiler_params=_cparams(("arbitrary", "arbitrary")),
        name="moe_gemm",
    )(tile_expert, n_used, tokens, tokens, x_flat, w1, w3, w2)


def _moe_combine_kernel(pos_ref, posn_ref, route_ref, x1_ref, modx_ref, modz_ref, g2_ref, b2_ref, y_hbm,
                        o_ref, buf_ref, sem, *, nzb, nblk, alpha, tm):
    i = pl.program_id(0)
    slot = i % 2

    def row_copy(p_ref, r, k, s):
        return pltpu.make_async_copy(y_hbm.at[pl.ds(p_ref[0, k * tm + r], 1)],
                                     buf_ref.at[s, pl.ds(k * tm + r, 1)], sem.at[s])

    def issue(p_ref, s):
        def body(r, carry):
            row_copy(p_ref, r, 0, s).start()
            row_copy(p_ref, r, 1, s).start()
            return carry
        lax.fori_loop(0, tm, body, 0, unroll=8)

    @pl.when(i == 0)
    def _():
        issue(pos_ref, 0)

    @pl.when(i + 1 < pl.num_programs(0))
    def _():
        issue(posn_ref, 1 - slot)

    pltpu.make_async_copy(y_hbm.at[pl.ds(0, 2 * tm)], buf_ref.at[slot], sem.at[slot]).wait()
    route = route_ref[...]
    f = route[:, 0:1] * buf_ref[slot, 0:tm] + route[:, 1:2] * buf_ref[slot, tm:2 * tm]
    is_ctx = (i % nblk) < nzb
    mod = jnp.where(is_ctx, modz_ref[...], modx_ref[...])
    o_ref[...] = _ln(alpha * x1_ref[...] + mod[5:6] * f) * g2_ref[...] + b2_ref[...]


def _moe_combine(ys, pos, route, x1, mod, ln_g, ln_b, n_ctx_tokens, alpha):
    bsz, s, d = x1.shape
    tm = TOKEN_BLOCK
    nblk = s // tm
    n_steps = bsz * nblk
    nrow = mod.shape[0]
    pos_steps = pos.reshape(n_steps, tm, 2).transpose(0, 2, 1).reshape(n_steps, 1, 2 * tm)
    tok = lambda width: pl.BlockSpec((tm, width), lambda i: (i, 0))
    smem = lambda imap: pl.BlockSpec((None, 1, 2 * tm), imap, memory_space=pltpu.SMEM)
    vec = lambda a: a.reshape(1, -1).astype(F32)
    out = pl.pallas_call(
        functools.partial(_moe_combine_kernel, nzb=n_ctx_tokens // tm, nblk=nblk, alpha=alpha, tm=tm),
        grid=(n_steps,),
        in_specs=[smem(lambda i: (i, 0, 0)),
                  smem(lambda i: (jnp.minimum(i + 1, n_steps - 1), 0, 0)),
                  tok(LANES), tok(d),
                  pl.BlockSpec((None, 6, d), lambda i: (i // nblk, 0, 0)),
                  pl.BlockSpec((None, 6, d), lambda i: (nrow - 1, 0, 0)),
                  pl.BlockSpec((1, d), lambda i: (0, 0)), pl.BlockSpec((1, d), lambda i: (0, 0)),
                  pl.BlockSpec(memory_space=pl.ANY)],
        out_specs=tok(d),
        out_shape=jax.ShapeDtypeStruct((bsz * s, d), F32),
        scratch_shapes=[pltpu.VMEM((2, 2 * tm, d), F32), pltpu.SemaphoreType.DMA((2,))],
        compiler_params=_cparams(("arbitrary",)),
        name="moe_combine",
    )(pos_steps, pos_steps, route.reshape(bsz * s, LANES), x1.reshape(bsz * s, d), mod, mod,
      vec(ln_g), vec(ln_b), ys)
    return out.reshape(bsz, s, d)


def _moe(fx, x1, route, mod, w1, w3, w2, ln_g, ln_b, n_ctx_tokens, alpha):
    bsz, s, d = x1.shape
    n_exp = w1.shape[0]
    idx = route.reshape(bsz * s, LANES)[:, 2:4].astype(jnp.int32)
    row_token, pos, tile_expert, n_used = _route_plan(idx[:, 0], idx[:, 1], n_exp, MOE_TILE)
    ys = _moe_gemm(fx.reshape(bsz * s, d), row_token, tile_expert, n_used, w1, w3, w2, MOE_TILE)
    return _moe_combine(ys, pos, route, x1, mod, ln_g, ln_b, n_ctx_tokens, alpha)


def _pick_chunk(total, target):
    best = LANES
    for c in range(LANES, target + 1, LANES):
        if total % c == 0:
            best = c
    return best


def kernel(x, c, ctx, c_ctx, w_mod, b_mod, w_in, s5_lam_re, s5_lam_im, s5_log_step, s5_b_re, s5_b_im,
           s5_c_re, s5_c_im, s5_d, s5_w_glu, s5_b_glu, attn_sink, ret_log_gamma, w_out,
           ln1_g, ln1_b, ln2_g, ln2_b, ffn_w1, ffn_w3, ffn_w2, moe_router, moe_w1, moe_w3, moe_w2):
    bsz, t_len, d = x.shape
    n_ctx = ctx.shape[1]
    depth = w_in.shape[0]
    alpha = (2 * depth) ** 0.25
    s5_w = s5_d.shape[1]
    att_w = attn_sink.shape[1] * HEAD_DIM
    kv_w = att_w // ATT_REP
    ret_w = ret_log_gamma.shape[2] * HEAD_DIM
    sizes = (s5_w, att_w, kv_w, kv_w, ret_w, ret_w, ret_w, ret_w)
    assert sum(sizes) == w_in.shape[2] and s5_w + att_w + ret_w == w_out.shape[1]
    assert n_ctx % TOKEN_BLOCK == 0 and t_len % TOKEN_BLOCK == 0 and t_len >= 3 * ATT_BLOCK

    pad = (-(bsz + 1)) % 8
    cvec = jnp.concatenate([jnp.zeros((pad, d), F32), c_ctx[None].astype(F32)], axis=0)
    cvec = jnp.concatenate([c.astype(F32), cvec], axis=0)
    mod_all = _modulation(cvec, w_mod.astype(F32), b_mod.astype(F32)).reshape(depth, bsz + pad + 1, 6, d)

    tabs = _rope_tables(t_len)
    xz = jnp.concatenate([ctx, x], axis=1).astype(F32)
    for l in range(depth):
        need_ctx = l < depth - 1
        mod = mod_all[l]
        u, qa, va, qr, vr, gr, ka, kr = _inproj(xz, mod, w_in[l].astype(BF16), tabs, sizes, n_ctx)
        s5w = _s5_weights(s5_lam_re[l], s5_lam_im[l], s5_log_step[l], s5_b_re[l], s5_b_im[l],
                          s5_c_re[l], s5_c_im[l], S5_CHUNK)
        y_s5 = _s5_scan(u, s5w, n_ctx)
        o_att = _attention(qa, ka, va, attn_sink[l], n_ctx, need_ctx)
        o_ret = _retention(qr, kr, vr, gr, ret_log_gamma[l], n_ctx)
        i = l // 2
        router = None if l % 2 == 0 else moe_router[i]
        outs = _outproj(y_s5, u, o_att, o_ret, xz, mod, s5_d[l], s5_w_glu[l], s5_b_glu[l], w_out[l],
                        ln1_g[l], ln1_b[l], router, n_ctx, need_ctx, alpha)
        ctx_tokens = n_ctx if need_ctx else 0
        if l % 2 == 0:
            x1, fx = outs
            xz = _ffn(fx, x1, mod, ffn_w1[i].astype(BF16), ffn_w3[i].astype(BF16), ffn_w2[i].astype(BF16),
                      ln2_g[l], ln2_b[l], ctx_tokens, alpha)
        else:
            x1, fx, route = outs
            xz = _moe(fx, x1, route, mod, moe_w1[i].astype(BF16), moe_w3[i].astype(BF16),
                      moe_w2[i].astype(BF16), ln2_g[l], ln2_b[l], ctx_tokens, alpha)
    return xz if xz.shape[1] == t_len else xz[:, n_ctx:]
```

```python
import functools
import math

import jax
import jax.numpy as jnp
import numpy as np
from jax import lax
from jax.experimental import pallas as pl
from jax.experimental.pallas import tpu as pltpu

F32 = jnp.float32
BF16 = jnp.bfloat16

GRID_W = 64
HEAD_DIM = 64
S5_GROUP = 16
ATT_REP = 4
WINDOW = 128
ATT_BLOCK = 128
RET_CHUNK = 128
TOP_K = 2
LN_EPS = 1e-5
ROPE_BASE = 10000.0
NEG_INF = -1e30

LANES = 128
S5_CHUNK = LANES
TOKEN_BLOCK = 256
FFN_BLOCK = 768
VMEM_LIMIT = 56 * 1024 * 1024


def _cparams(sem):
    return pltpu.CompilerParams(dimension_semantics=sem, vmem_limit_bytes=VMEM_LIMIT)


def _dot(a, b):
    return jnp.dot(a, b, preferred_element_type=F32)


def _dot_nt(a, b):
    return lax.dot_general(a, b, (((1,), (1,)), ((), ())), preferred_element_type=F32)


def _dot_tn(a, b):
    return lax.dot_general(a, b, (((0,), (0,)), ((), ())), preferred_element_type=F32)


def _split_bf16(a):
    hi = a.astype(BF16)
    lo = (a - hi.astype(F32)).astype(BF16)
    return hi, lo


def _dot3(a, b):
    ah, al = _split_bf16(a)
    bh, bl = _split_bf16(b)
    return _dot(ah, bh) + (_dot(ah, bl) + _dot(al, bh))


def _ln(x):
    mu = jnp.mean(x, axis=-1, keepdims=True)
    xc = x - mu
    var = jnp.mean(xc * xc, axis=-1, keepdims=True)
    return xc * lax.rsqrt(var + LN_EPS)


def _silu(x):
    return x * (1.0 / (1.0 + jnp.exp(-x)))


def _sigmoid(x):
    return 1.0 / (1.0 + jnp.exp(-x))


def _gelu_tanh(x):
    c = math.sqrt(2.0 / math.pi)
    return 0.5 * x * (1.0 + jnp.tanh(c * (x + 0.044715 * (x * x * x))))


def _mod_kernel(c_ref, w_ref, b_ref, o_ref):
    o_ref[...] = _dot3(_silu(c_ref[...]), w_ref[...]) + b_ref[...]


def _modulation(cvec, w_mod, b_mod):
    depth, d, n = w_mod.shape
    rows = cvec.shape[0]
    tn = 1536
    return pl.pallas_call(
        _mod_kernel,
        grid=(depth, n // tn),
        in_specs=[pl.BlockSpec((rows, d), lambda l, j: (0, 0)),
                  pl.BlockSpec((None, d, tn), lambda l, j: (l, 0, j)),
                  pl.BlockSpec((None, 1, tn), lambda l, j: (l, 0, j))],
        out_specs=pl.BlockSpec((None, rows, tn), lambda l, j: (l, 0, j)),
        out_shape=jax.ShapeDtypeStruct((depth, rows, n), F32),
        compiler_params=_cparams(("parallel", "parallel")),
        name="modulation",
    )(cvec, w_mod, b_mod.reshape(depth, 1, n))


def _rope_slab(xs, cos, sa, sb, half):
    return xs * cos + pltpu.roll(xs, LANES - half, 1) * sa + pltpu.roll(xs, half, 1) * sb


def _inproj_kernel(x_ref, modx_ref, modz_ref, wc_ref, w_ref, act_ref, ast_ref, rct_ref, rst_ref,
                   ac_ref, asa_ref, asb_ref, rc_ref, rsa_ref, rsb_ref,
                   ut_ref, qat_ref, vat_ref, qrt_ref, vrt_ref, grt_ref, ka_ref, kr_ref, *, nzb, c_sizes, sizes):
    is_ctx = pl.program_id(1) < nzb
    mod = jnp.where(is_ctx, modz_ref[...], modx_ref[...])
    h = (_ln(x_ref[...]) * (1.0 + mod[1:2]) + mod[0:1]).astype(BF16)
    scale = HEAD_DIM ** -0.5

    ct = _dot_nt(wc_ref[...], h)
    c_offs = [int(o) for o in np.concatenate([[0], np.cumsum(c_sizes)])]

    def chan(i):
        return ct[c_offs[i]:c_offs[i + 1]]

    def rope_t(x, cos_ref, sin_ref, part, mul):
        cos_t = jnp.where(is_ctx, 1.0, cos_ref[...])
        sin_t = jnp.where(is_ctx, 0.0, sin_ref[...])
        heads = []
        for hd in range(x.shape[0] // HEAD_DIM):
            xh = x[hd * HEAD_DIM:(hd + 1) * HEAD_DIM]
            pieces = []
            for lo in range(0, HEAD_DIM, 2 * part):
                pieces += [xh[lo + part:lo + 2 * part], xh[lo:lo + part]]
            out = xh * cos_t + jnp.concatenate(pieces, axis=0) * sin_t
            heads.append(out * mul if mul != 1.0 else out)
        return jnp.concatenate(heads, axis=0)

    def put(ref, val):
        for k in range(ref.shape[0]):
            ref[k] = val[:, k * ATT_BLOCK:(k + 1) * ATT_BLOCK]

    ut_ref[...] = chan(0)
    put(qat_ref, rope_t(chan(1), act_ref, ast_ref, HEAD_DIM // 4, scale))
    put(vat_ref, chan(2))
    put(qrt_ref, rope_t(chan(3), rct_ref, rst_ref, HEAD_DIM // 2, 1.0))
    put(vrt_ref, chan(4))
    put(grt_ref, chan(5))

    p = _dot(h, w_ref[...])
    ac = jnp.where(is_ctx, 1.0, ac_ref[...])
    asa = jnp.where(is_ctx, 0.0, asa_ref[...])
    asb = jnp.where(is_ctx, 0.0, asb_ref[...])
    rc = jnp.where(is_ctx, 1.0, rc_ref[...])
    rsa = jnp.where(is_ctx, 0.0, rsa_ref[...])
    rsb = jnp.where(is_ctx, 0.0, rsb_ref[...])
    offs = np.concatenate([[0], np.cumsum(sizes)])

    def rope_cols(i, cos, sa, sb, half, mul):
        blk = p[:, int(offs[i]):int(offs[i + 1])]
        slabs = [_rope_slab(blk[:, s:s + LANES], cos, sa, sb, half) for s in range(0, blk.shape[1], LANES)]
        out = slabs[0] if len(slabs) == 1 else jnp.concatenate(slabs, axis=1)
        return out * mul if mul != 1.0 else out

    ka_ref[...] = rope_cols(0, ac, asa, asb, HEAD_DIM // 4, 1.0)
    kr_ref[...] = rope_cols(1, rc, rsa, rsb, HEAD_DIM // 2, scale)


def _inproj(xz, mod, w_in, tabs, sizes, n_ctx):
    bsz, s, d = xz.shape
    tm = TOKEN_BLOCK
    nzb = n_ctx // tm
    nrow = mod.shape[0]
    offs = np.concatenate([[0], np.cumsum(sizes)])
    col = lambda i: w_in[:, int(offs[i]):int(offs[i + 1])]
    c_idx, r_idx = (0, 1, 3, 4, 6, 7), (2, 5)
    c_sizes = tuple(sizes[i] for i in c_idx)
    r_sizes = tuple(sizes[i] for i in r_idx)
    w_c_t = jnp.concatenate([col(i) for i in c_idx], axis=1).T
    w_rest = jnp.concatenate([col(i) for i in r_idx], axis=1)
    sub = tm // ATT_BLOCK
    tok = lambda width: pl.BlockSpec((None, tm, width), lambda b, j: (b, j, 0))
    chan = lambda width: pl.BlockSpec((sub, None, width, ATT_BLOCK), lambda b, j: (j, b, 0, 0))
    tab = pl.BlockSpec((tm, LANES), lambda b, j: (jnp.maximum(j - nzb, 0), 0))
    tab_t = pl.BlockSpec((HEAD_DIM, tm), lambda b, j: (0, jnp.maximum(j - nzb, 0)))
    return pl.pallas_call(
        functools.partial(_inproj_kernel, nzb=nzb, c_sizes=c_sizes, sizes=r_sizes),
        grid=(bsz, s // tm),
        in_specs=[tok(d),
                  pl.BlockSpec((None, 6, d), lambda b, j: (b, 0, 0)),
                  pl.BlockSpec((None, 6, d), lambda b, j: (nrow - 1, 0, 0)),
                  pl.BlockSpec((sum(c_sizes), d), lambda b, j: (0, 0)),
                  pl.BlockSpec((d, sum(r_sizes)), lambda b, j: (0, 0)),
                  tab_t, tab_t, tab_t, tab_t, tab, tab, tab, tab, tab, tab],
        out_specs=[pl.BlockSpec((None, None, c_sizes[0], tm), lambda b, j: (j, b, 0, 0))]
                  + [chan(w) for w in c_sizes[1:]] + [tok(w) for w in r_sizes],
        out_shape=[jax.ShapeDtypeStruct((s // tm, bsz, c_sizes[0], tm), F32)]
                  + [jax.ShapeDtypeStruct((s // ATT_BLOCK, bsz, w, ATT_BLOCK), F32) for w in c_sizes[1:]]
                  + [jax.ShapeDtypeStruct((bsz, s, w), F32) for w in r_sizes],
        compiler_params=_cparams(("parallel", "parallel")),
        name="inproj",
    )(xz, mod, mod, w_c_t, w_rest, *tabs)


def _rope_tables(t_len):
    t = jnp.arange(t_len)
    rows = (t // GRID_W).astype(F32)
    cols = (t % GRID_W).astype(F32)
    pos = t.astype(F32)

    def angles(p, dim):
        inv_freq = ROPE_BASE ** (-jnp.arange(0, dim, 2, dtype=F32) / dim)
        return p[:, None] * inv_freq[None, :]

    def head_tables(angs):
        cos = jnp.concatenate([jnp.concatenate([jnp.cos(a), jnp.cos(a)], -1) for a in angs], -1)
        sa = jnp.concatenate([jnp.concatenate([-jnp.sin(a), jnp.zeros_like(a)], -1) for a in angs], -1)
        sb = jnp.concatenate([jnp.concatenate([jnp.zeros_like(a), jnp.sin(a)], -1) for a in angs], -1)
        rep = LANES // HEAD_DIM
        return tuple(jnp.tile(x, (1, rep)) for x in (cos, sa, sb))

    att = head_tables([angles(rows, HEAD_DIM // 2), angles(cols, HEAD_DIM // 2)])
    ret = head_tables([angles(pos, HEAD_DIM)])
    signed_t = lambda tb: (tb[0][:, :HEAD_DIM].T, (tb[1] + tb[2])[:, :HEAD_DIM].T)
    return signed_t(att) + signed_t(ret) + att + ret


def _s5_kernel(u_ref, tab_ref, wsf_ref, wsb_ref, wrf_ref, wrb_ref, lam_ref, y_ref,
               lhs_ref, m_ref, acc_ref, cf_ref, cb_ref, pf_ref, pb_ref, *, bsz, nz_chunks):
    r_blk, ch, tm = u_ref.shape
    c = S5_CHUNK
    kpb = tm // c
    n_chunks = (r_blk // bsz) * kpb

    for i in range(ch):
        x = u_ref[:, i, :].astype(BF16)
        p, half = divmod(i, 2)
        for k in range(kpb):
            lhs_ref[p, k * r_blk:(k + 1) * r_blk, half * c:(half + 1) * c] = x[:, k * c:(k + 1) * c]
    acc_ref[...] = jnp.zeros_like(acc_ref)
    cf_ref[...] = jnp.zeros_like(cf_ref)
    cb_ref[...] = jnp.zeros_like(cb_ref)

    def pair(p, carry):
        for ih in range(2):
            for o in range(ch):
                lag_row = jnp.broadcast_to(tab_ref[2 * p + ih, o:o + 1, :], (8, 2 * c))
                tab = pltpu.roll(lag_row, 0, 1, stride=1, stride_axis=0)
                for r2 in range(c // 16):
                    lo = c - 16 * r2
                    rows = jnp.concatenate([tab[:, lo:lo + c], tab[:, lo - 8:lo - 8 + c]], axis=0)
                    m_ref[ih * c + 16 * r2:ih * c + 16 * r2 + 16, o * c:(o + 1) * c] = rows.astype(BF16)
        lhs = lhs_ref[p]
        acc_ref[...] += _dot(lhs, m_ref[...])
        cf_ref[...] += _dot(lhs, wsf_ref[p])
        cb_ref[...] += _dot(lhs, wsb_ref[p])
        return carry

    lax.fori_loop(0, ch // 2, pair, 0)
    lam = lam_ref[...]
    nst = lam.shape[1] // 2

    def advance(s, a, bc, drive):
        return s * a + pltpu.roll(s, nst, 1) * bc + drive

    def sweep(order, c_ref, p_ref, a, bc):
        s = jnp.zeros((bsz, lam.shape[1]), F32)
        for n in order:
            j, k = divmod(n, kpb)
            rows = slice(k * r_blk + j * bsz, k * r_blk + (j + 1) * bsz)
            p_ref[rows, :] = s
            s = advance(s, a, bc, c_ref[rows, :])

    sweep(list(range(n_chunks)), cf_ref, pf_ref, lam[0:1], lam[1:2])
    order_b = list(range(nz_chunks - 1, -1, -1)) + list(range(n_chunks - 1, nz_chunks - 1, -1))
    sweep(order_b, cb_ref, pb_ref, lam[2:3], lam[3:4])
    acc_ref[...] += (_dot(pf_ref[...].astype(BF16), wrf_ref[...])
                     + _dot(pb_ref[...].astype(BF16), wrb_ref[...]))
    for o in range(ch):
        for k in range(kpb):
            y_ref[:, o, k * c:(k + 1) * c] = acc_ref[k * r_blk:(k + 1) * r_blk, o * c:(o + 1) * c]


def _s5_weights(lam_re, lam_im, log_step, b_re, b_im, c_re, c_im, chunk):
    hp = lax.Precision.HIGHEST
    lam = lax.complex(lam_re.astype(F32), lam_im.astype(F32))
    lam_dt = lam * jnp.exp(log_step.astype(F32))[..., None]
    lam_bar = jnp.exp(lam_dt)
    b_bar = lax.complex(b_re.astype(F32), b_im.astype(F32)) * ((lam_bar - 1.0) / lam)[..., None]
    c_mat = lax.complex(c_re.astype(F32), c_im.astype(F32))
    g, n = lam.shape[1], lam.shape[2]
    ch = b_bar.shape[-1]
    steps = jnp.arange(chunk + 1, dtype=F32)
    pw = jnp.exp(steps[None, :, None, None] * lam_dt[:, None])
    kern = jnp.einsum('zgon,zdgn,zgni->zgdoi', c_mat, pw[:, :chunk], b_bar, precision=hp).real
    zero_lag = kern[0, :, 0] + kern[1, :, 0]
    lag_table = jnp.concatenate([jnp.zeros_like(zero_lag)[:, None], kern[1, :, :0:-1], zero_lag[:, None],
                                 kern[0, :, 1:]], axis=1).transpose(0, 3, 2, 1)

    def state_in(pw_s, b_dir):
        w = pw_s[:, :, :, None] * b_dir[None]
        w = jnp.concatenate([w.real, w.imag], axis=2)
        return w.transpose(1, 3, 0, 2).reshape(g, ch // 2, 2 * chunk, 2 * n)

    def state_out(pw_t, c_dir):
        w = c_dir[None] * pw_t[:, :, None, :]
        w = jnp.concatenate([w.real, -w.imag], axis=3)
        return w.transpose(1, 3, 2, 0).reshape(g, 2 * n, ch * chunk)

    wsf = state_in(pw[0, chunk - 1::-1][:chunk], b_bar[0])
    wsb = state_in(pw[1, :chunk], b_bar[1])
    wrf = state_out(pw[0, 1:chunk + 1], c_mat[0])
    wrb = state_out(pw[1, chunk:0:-1], c_mat[1])
    lam_c = pw[:, chunk]
    rows = []
    for z in range(2):
        rows.append(jnp.concatenate([lam_c[z].real, lam_c[z].real], -1))
        rows.append(jnp.concatenate([-lam_c[z].imag, lam_c[z].imag], -1))
    lam_rows = jnp.stack(rows + [jnp.zeros_like(rows[0])] * 4, axis=1)
    return lag_table, wsf.astype(BF16), wsb.astype(BF16), wrf.astype(BF16), wrb.astype(BF16), lam_rows


def _s5_scan(ut, weights, n_ctx):
    nblk, bsz, width, tm = ut.shape
    tab, wsf, wsb, wrf, wrb, lam_rows = weights
    g, ch = tab.shape[0], tab.shape[1]
    c = S5_CHUNK
    nst2 = wsf.shape[3]
    r_blk = nblk * bsz
    rows = r_blk * (tm // c)
    grp = lambda *shape: pl.BlockSpec((None,) + shape, lambda i: (i,) + (0,) * len(shape))
    tok = pl.BlockSpec((r_blk, ch, tm), lambda i: (0, i, 0))
    y = pl.pallas_call(
        functools.partial(_s5_kernel, bsz=bsz, nz_chunks=n_ctx // c),
        grid=(g,),
        in_specs=[tok, grp(ch, ch, 2 * c), grp(ch // 2, 2 * c, nst2), grp(ch // 2, 2 * c, nst2),
                  grp(nst2, ch * c), grp(nst2, ch * c), grp(8, nst2)],
        out_specs=tok,
        out_shape=jax.ShapeDtypeStruct((r_blk, width, tm), F32),
        scratch_shapes=[pltpu.VMEM((ch // 2, rows, 2 * c), BF16),
                        pltpu.VMEM((2 * c, ch * c), BF16),
                        pltpu.VMEM((rows, ch * c), F32)] + [pltpu.VMEM((rows, nst2), F32)] * 4,
        compiler_params=_cparams(("parallel",)),
        name="s5_scan",
    )(ut.reshape(r_blk, width, tm), tab, wsf, wsb, wrf, wrb, lam_rows)
    return y.reshape(nblk, bsz, width, tm)


def _attn_kernel(sink_ref, q_ref, k_ref, v_ref, o_ref, *, n_ctx, t_len, q_off, kv_heads):
    qi = pl.program_id(1) + q_off
    nzb = n_ctx // ATT_BLOCK
    band = 3 * ATT_BLOCK
    wide = ATT_REP * ATT_BLOCK
    q = q_ref[...].astype(BF16)
    kc = k_ref[0:n_ctx, :].astype(BF16)
    vc = jnp.concatenate([v_ref[p] for p in range(nzb)], axis=1).astype(BF16)

    def rows(a, i):
        return a[i * HEAD_DIM:(i + 1) * HEAD_DIM]

    def q_group(g):
        qg = jnp.concatenate([rows(q, g * ATT_REP + r) for r in range(ATT_REP)], axis=1)
        zero = jnp.zeros_like(qg)
        return jnp.concatenate([qg if gg == g else zero for gg in range(kv_heads)], axis=0)

    def finish(g, o, den):
        o = o / den
        for r in range(ATT_REP):
            h = g * ATT_REP + r
            o_ref[h * HEAD_DIM:(h + 1) * HEAD_DIM, :] = o[:, r * ATT_BLOCK:(r + 1) * ATT_BLOCK]

    @pl.when(qi < nzb)
    def _():
        for g in range(kv_heads):
            sink = sink_ref[g:g + 1, :]
            s = _dot(kc, q_group(g))
            m = jnp.maximum(jnp.max(s, axis=0, keepdims=True), sink)
            e = jnp.exp(s - m)
            den = jnp.sum(e, axis=0, keepdims=True) + jnp.exp(sink - m)
            finish(g, _dot(rows(vc, g), e.astype(BF16)), den)

    @pl.when(qi >= nzb)
    def _():
        n = qi - nzb
        start = jnp.clip((n - 1) * ATT_BLOCK, 0, t_len - band)
        kl = k_ref[pl.ds(pl.multiple_of(n_ctx + start, ATT_BLOCK), band), :].astype(BF16)
        p0 = (n_ctx + start) // ATT_BLOCK
        vl = jnp.concatenate([v_ref[p0 + t] for t in range(3)], axis=1).astype(BF16)
        k_pos = start + lax.broadcasted_iota(jnp.int32, (band, wide), 0)
        q_pos = n * ATT_BLOCK + (lax.broadcasted_iota(jnp.int32, (band, wide), 1) & (ATT_BLOCK - 1))
        valid = jnp.abs(k_pos - q_pos) <= WINDOW
        for g in range(kv_heads):
            sink = sink_ref[g:g + 1, :]
            qg = q_group(g)
            s_loc = jnp.where(valid, _dot(kl, qg), NEG_INF)
            s_ctx = _dot(kc, qg)
            m = jnp.maximum(jnp.maximum(jnp.max(s_loc, axis=0, keepdims=True),
                                        jnp.max(s_ctx, axis=0, keepdims=True)), sink)
            e_loc = jnp.exp(s_loc - m)
            e_ctx = jnp.exp(s_ctx - m)
            den = (jnp.sum(e_loc, axis=0, keepdims=True) + jnp.sum(e_ctx, axis=0, keepdims=True)
                   + jnp.exp(sink - m))
            o = _dot(rows(vl, g), e_loc.astype(BF16)) + _dot(rows(vc, g), e_ctx.astype(BF16))
            finish(g, o, den)


def _attention(qat, ka, vat, sink, n_ctx, need_ctx):
    nblk, bsz, qw, _ = qat.shape
    s, kvw = ka.shape[1], ka.shape[2]
    t_len = s - n_ctx
    kv_heads = kvw // HEAD_DIM
    q_off = 0 if need_ctx else n_ctx // ATT_BLOCK
    nq = nblk - q_off
    sink_rows = jnp.repeat(sink.astype(F32).reshape(kv_heads, ATT_REP), ATT_BLOCK, axis=1)
    return pl.pallas_call(
        functools.partial(_attn_kernel, n_ctx=n_ctx, t_len=t_len, q_off=q_off, kv_heads=kv_heads),
        grid=(bsz, nq),
        in_specs=[pl.BlockSpec(sink_rows.shape, lambda b, j: (0, 0)),
                  pl.BlockSpec((None, None, qw, ATT_BLOCK), lambda b, j: (j + q_off, b, 0, 0)),
                  pl.BlockSpec((None, s, kvw), lambda b, j: (b, 0, 0)),
                  pl.BlockSpec((nblk, None, kvw, ATT_BLOCK), lambda b, j: (0, b, 0, 0))],
        out_specs=pl.BlockSpec((None, None, qw, ATT_BLOCK), lambda b, j: (j, b, 0, 0)),
        out_shape=jax.ShapeDtypeStruct((nq, bsz, qw, ATT_BLOCK), F32),
        compiler_params=_cparams(("parallel", "arbitrary")),
        name="window_attention",
    )(sink_rows, qat, ka, vat)


def _ret_kernel(lg_ref, q_ref, k_ref, v_ref, g_ref, o_ref, acc_ref, sf_ref, sb_ref, *, nz_chunks, heads):
    c = RET_CHUNK
    n_chunks, w, _ = q_ref.shape
    f32 = lambda a: a.astype(F32)

    def per_head(shape, axis, group, direction):
        owner = lax.broadcasted_iota(jnp.int32, shape, axis) // group
        out = jnp.zeros(shape, F32)
        for h in range(heads):
            out = jnp.where(owner == h, lg_ref[direction, h], out)
        return out

    row_i = f32(lax.broadcasted_iota(jnp.int32, (w, c), 1))
    qw_f = jnp.exp(per_head((w, c), 0, HEAD_DIM, 0) * (row_i + 1.0))
    qw_b = jnp.exp(per_head((w, c), 0, HEAD_DIM, 1) * (c - row_i))
    key_j = f32(lax.broadcasted_iota(jnp.int32, (c, w), 0))
    kw_f = jnp.exp(per_head((c, w), 1, HEAD_DIM, 0) * (c - 1.0 - key_j))
    kw_b = jnp.exp(per_head((c, w), 1, HEAD_DIM, 1) * key_j)
    dec_f = jnp.exp(per_head((w, 1), 0, HEAD_DIM, 0) * c)
    dec_b = jnp.exp(per_head((w, 1), 0, HEAD_DIM, 1) * c)
    wide = heads * c
    diff = f32((lax.broadcasted_iota(jnp.int32, (c, wide), 1) & (c - 1)) - lax.broadcasted_iota(jnp.int32, (c, wide), 0))
    dec_t = jnp.where(diff >= 0, jnp.exp(per_head((c, wide), 1, c, 0) * jnp.maximum(diff, 0.0)),
                      jnp.exp(per_head((c, wide), 1, c, 1) * jnp.maximum(-diff, 0.0)))
    own_wide = (lax.broadcasted_iota(jnp.int32, (w, wide), 0) // HEAD_DIM
                == lax.broadcasted_iota(jnp.int32, (w, wide), 1) // c)
    own_sq = f32(lax.broadcasted_iota(jnp.int32, (w, w), 0) // HEAD_DIM
                 == lax.broadcasted_iota(jnp.int32, (w, w), 1) // HEAD_DIM)
    sf_ref[...] = jnp.zeros_like(sf_ref)
    sb_ref[...] = jnp.zeros_like(sb_ref)

    def block_diag(x):
        return jnp.where(own_wide, jnp.concatenate([x] * heads, axis=1), 0.0).astype(BF16)

    def fwd(n, carry):
        qt, vt = q_ref[n], v_ref[n]
        k = k_ref[pl.ds(pl.multiple_of(n * c, c), c), :]
        scores_t = _dot(k.astype(BF16), block_diag(qt)) * dec_t
        stacked = jnp.concatenate([scores_t[:, h * c:(h + 1) * c] for h in range(heads)], axis=0)
        o = _dot(block_diag(vt), stacked.astype(BF16))
        s_prev = sf_ref[...]
        o = o + _dot(s_prev.astype(BF16), (qt * qw_f).astype(BF16))
        sf_ref[...] = dec_f * s_prev + own_sq * _dot(vt.astype(BF16), (k * kw_f).astype(BF16))
        acc_ref[n] = o
        return carry

    lax.fori_loop(0, n_chunks, fwd, 0)

    def bwd(i, carry):
        n = jnp.where(i < nz_chunks, nz_chunks - 1 - i, n_chunks - 1 - i + nz_chunks)
        qt, vt = q_ref[n], v_ref[n]
        k = k_ref[pl.ds(pl.multiple_of(n * c, c), c), :]
        s_prev = sb_ref[...]
        o = acc_ref[n] + _dot(s_prev.astype(BF16), (qt * qw_b).astype(BF16))
        sb_ref[...] = dec_b * s_prev + own_sq * _dot(vt.astype(BF16), (k * kw_b).astype(BF16))
        gate = _silu(g_ref[n])
        for h in range(heads):
            rows = slice(h * HEAD_DIM, (h + 1) * HEAD_DIM)
            oh = o[rows]
            mu = jnp.mean(oh, axis=0, keepdims=True)
            oc = oh - mu
            var = jnp.mean(oc * oc, axis=0, keepdims=True)
            o_ref[n, rows, :] = oc * lax.rsqrt(var + LN_EPS) * gate[rows]
        return carry

    lax.fori_loop(0, n_chunks, bwd, 0)


def _retention(qrt, kr, vrt, grt, log_gamma, n_ctx):
    n_chunks, bsz, w, c = qrt.shape
    s = kr.shape[1]
    chan = pl.BlockSpec((n_chunks, None, w, c), lambda b: (0, b, 0, 0))
    return pl.pallas_call(
        functools.partial(_ret_kernel, nz_chunks=n_ctx // RET_CHUNK, heads=w // HEAD_DIM),
        grid=(bsz,),
        in_specs=[pl.BlockSpec(memory_space=pltpu.SMEM), chan, pl.BlockSpec((None, s, w), lambda b: (b, 0, 0)),
                  chan, chan],
        out_specs=chan,
        out_shape=jax.ShapeDtypeStruct((n_chunks, bsz, w, c), F32),
        scratch_shapes=[pltpu.VMEM((n_chunks, w, c), F32), pltpu.VMEM((w, w), F32), pltpu.VMEM((w, w), F32)],
        compiler_params=_cparams(("parallel",)),
        name="retention",
    )(log_gamma.astype(F32), qrt, kr, vrt, grt)


def _outproj_kernel(*refs, nzb, alpha, with_router, w_s5, w_att, n_exp):
    if with_router:
        (y_ref, u_ref, a_ref, r_ref, x_ref, modx_ref, modz_ref, d_ref, wg_ref, bg_ref, wo_ref,
         g1_ref, b1_ref, rt_ref, x1_ref, fx_ref, gate_ref) = refs
    else:
        (y_ref, u_ref, a_ref, r_ref, x_ref, modx_ref, modz_ref, d_ref, wg_ref, bg_ref, wo_ref,
         g1_ref, b1_ref, x1_ref, fx_ref) = refs
    is_ctx = pl.program_id(1) < nzb
    mod = jnp.where(is_ctx, modz_ref[...], modx_ref[...])
    g = _gelu_tanh(y_ref[...] + d_ref[...] * u_ref[...])
    s5 = g * _sigmoid(_dot(wg_ref[...], g.astype(BF16)) + bg_ref[...])
    def chan_major(ref, lo, hi):
        return jnp.concatenate([_dot_tn(ref[k].astype(BF16), wo_ref[lo:hi, :]) for k in range(ref.shape[0])],
                               axis=0)

    mix = (_dot_tn(s5.astype(BF16), wo_ref[0:w_s5, :]) + chan_major(a_ref, w_s5, w_s5 + w_att)
           + chan_major(r_ref, w_s5 + w_att, wo_ref.shape[0]))
    x1 = _ln(alpha * x_ref[...] + mod[2:3] * mix) * g1_ref[...] + b1_ref[...]
    x1_ref[...] = x1
    fx = _ln(x1) * (1.0 + mod[4:5]) + mod[3:4]
    fx_ref[...] = fx.astype(fx_ref.dtype)
    if with_router:
        lane = lax.broadcasted_iota(jnp.int32, (fx.shape[0], LANES), 1)
        logits = jnp.where(lane < n_exp, _dot3(fx, rt_ref[...]), -jnp.inf)
        m1 = jnp.max(logits, axis=1, keepdims=True)
        i1 = jnp.min(jnp.where(logits == m1, lane, LANES), axis=1, keepdims=True)
        rest = jnp.where(lane == i1, -jnp.inf, logits)
        m2 = jnp.max(rest, axis=1, keepdims=True)
        i2 = jnp.min(jnp.where(rest == m2, lane, LANES), axis=1, keepdims=True)
        e2 = jnp.exp(m2 - m1)
        den = 1.0 + e2
        route = jnp.where(lane == 0, 1.0 / den, jnp.where(lane == 1, e2 / den, 0.0))
        route = jnp.where(lane == 2, i1.astype(F32), jnp.where(lane == 3, i2.astype(F32), route))
        gate_ref[...] = route


def _outproj(y_s5, u, o_att, o_ret, xz, mod, s5_d, w_glu, b_glu, w_out, ln_g, ln_b, router, n_ctx, need_ctx,
             alpha):
    bsz, s, d = xz.shape
    tm = TOKEN_BLOCK
    nzb = n_ctx // tm
    off = 0 if need_ctx else nzb
    nblk = s // tm - off
    s_out = nblk * tm
    nrow = mod.shape[0]
    w_s5, w_att = y_s5.shape[2], o_att.shape[2]
    sub = tm // ATT_BLOCK
    att_off = off if o_att.shape[0] == s // ATT_BLOCK else 0
    att = pl.BlockSpec((sub, None, w_att, ATT_BLOCK), lambda b, j: (j + att_off, b, 0, 0))

    def tok(width, shift):
        return pl.BlockSpec((None, tm, width), lambda b, j: (b, j + shift, 0))

    def full(a):
        return pl.BlockSpec(a.shape, lambda b, j: (0,) * a.ndim)

    chan = pl.BlockSpec((None, None, w_s5, tm), lambda b, j: (j + off, b, 0, 0))
    vec = lambda a: a.reshape(1, -1).astype(F32)
    col = lambda a: a.reshape(-1, 1).astype(F32)
    consts = [col(s5_d), w_glu.T.astype(BF16), col(b_glu), w_out.astype(BF16), vec(ln_g), vec(ln_b)]
    ret = pl.BlockSpec((sub, None, o_ret.shape[2], ATT_BLOCK), lambda b, j: (j + off, b, 0, 0))
    in_specs = [chan, chan, att, ret, tok(d, off),
                pl.BlockSpec((None, 6, d), lambda b, j: (b, 0, 0)),
                pl.BlockSpec((None, 6, d), lambda b, j: (nrow - 1, 0, 0))] + [full(a) for a in consts]
    out_specs = [tok(d, 0), tok(d, 0)]
    fx_dtype = BF16 if router is None else F32
    out_shape = [jax.ShapeDtypeStruct((bsz, s_out, d), F32), jax.ShapeDtypeStruct((bsz, s_out, d), fx_dtype)]
    args = [y_s5, u, o_att, o_ret, xz, mod, mod] + consts
    with_router = router is not None
    n_exp = 0
    if with_router:
        n_exp = router.shape[1]
        router_pad = jnp.pad(router.astype(F32), ((0, 0), (0, LANES - n_exp)))
        args.append(router_pad)
        in_specs.append(full(router_pad))
        out_specs.append(tok(LANES, 0))
        out_shape.append(jax.ShapeDtypeStruct((bsz, s_out, LANES), F32))
    return pl.pallas_call(
        functools.partial(_outproj_kernel, nzb=nzb - off, alpha=alpha, with_router=with_router,
                          w_s5=w_s5, w_att=w_att, n_exp=n_exp),
        grid=(bsz, nblk),
        in_specs=in_specs,
        out_specs=out_specs,
        out_shape=out_shape,
        compiler_params=_cparams(("parallel", "parallel")),
        name="outproj",
    )(*args)


def _swiglu_into(x_bf16, w1_ref, w3_ref, w2_ref, acc_ref, fc):
    for s in range(0, w1_ref.shape[1], fc):
        h1 = _dot(x_bf16, w1_ref[:, s:s + fc])
        h3 = _dot(x_bf16, w3_ref[:, s:s + fc])
        acc_ref[...] += _dot((_silu(h1) * h3).astype(BF16), w2_ref[s:s + fc, :])


def _ffn_kernel(fx_ref, x1_ref, modx_ref, modz_ref, w1_ref, w3_ref, w2_ref, g2_ref, b2_ref,
                o_ref, acc_ref, *, n_ctx_tokens, alpha, fc):
    tm = acc_ref.shape[0]
    acc_ref[...] = jnp.zeros_like(acc_ref)
    _swiglu_into(fx_ref[...], w1_ref, w3_ref, w2_ref, acc_ref, fc)
    row = pl.program_id(1) * tm + lax.broadcasted_iota(jnp.int32, (tm, 1), 0)
    gate = jnp.where(row < n_ctx_tokens, modz_ref[5:6, :], modx_ref[5:6, :])
    o_ref[...] = _ln(alpha * x1_ref[...] + gate * acc_ref[...]) * g2_ref[...] + b2_ref[...]


def _ffn(fx, x1, mod, w1, w3, w2, ln_g, ln_b, n_ctx_tokens, alpha):
    bsz, s, d = x1.shape
    ff = w1.shape[1]
    tm = _pick_chunk(s, FFN_BLOCK)
    nrow = mod.shape[0]
    tok = pl.BlockSpec((None, tm, d), lambda b, j: (b, j, 0))
    full = lambda a: pl.BlockSpec(a.shape, lambda b, j: (0,) * a.ndim, pipeline_mode=pl.Buffered(1))
    vec = lambda a: a.reshape(1, -1).astype(F32)
    return pl.pallas_call(
        functools.partial(_ffn_kernel, n_ctx_tokens=n_ctx_tokens, alpha=alpha, fc=_pick_chunk(ff, 512)),
        grid=(bsz, s // tm),
        in_specs=[tok, tok,
                  pl.BlockSpec((None, 6, d), lambda b, j: (b, 0, 0)),
                  pl.BlockSpec((None, 6, d), lambda b, j: (nrow - 1, 0, 0)),
                  full(w1), full(w3), full(w2),
                  pl.BlockSpec((1, d), lambda b, j: (0, 0)), pl.BlockSpec((1, d), lambda b, j: (0, 0))],
        out_specs=tok,
        out_shape=jax.ShapeDtypeStruct((bsz, s, d), F32),
        scratch_shapes=[pltpu.VMEM((tm, d), F32)],
        compiler_params=_cparams(("parallel", "parallel")),
        name="dense_ffn",
    )(fx, x1, mod, mod, w1, w3, w2, vec(ln_g), vec(ln_b))


MOE_TILE = 1024


def _route_plan(e1, e2, n_exp, tile):
    n = e1.shape[0]
    pair_e = jnp.stack([e1, e2], axis=1).reshape(-1)
    onehot = (pair_e[:, None] == jnp.arange(n_exp, dtype=jnp.int32)[None, :]).astype(jnp.int32)
    before = jnp.cumsum(onehot, axis=0) - onehot
    rank = jnp.sum(before * onehot, axis=1)
    counts = jnp.sum(onehot, axis=0)
    padded = (counts + tile - 1) // tile * tile
    ends = jnp.cumsum(padded)
    starts = ends - padded
    dest = starts[pair_e] + rank
    n_rows = (2 * n + n_exp * (tile - 1)) // tile * tile
    n_tiles = n_rows // tile
    row_token = jnp.zeros((n_rows,), jnp.int32).at[dest].set(
        jnp.arange(2 * n, dtype=jnp.int32) // 2, unique_indices=True, mode="promise_in_bounds")
    tile_start = jnp.arange(n_tiles, dtype=jnp.int32) * tile
    tile_expert = jnp.minimum(jnp.sum((tile_start[:, None] >= ends[None, :]).astype(jnp.int32), axis=1),
                              n_exp - 1)
    n_used = (ends[-1] // tile).astype(jnp.int32).reshape(1)
    return row_token, dest.reshape(n, 2), tile_expert, n_used


def _moe_gemm_kernel(te_ref, nused_ref, tok_ref, tokn_ref, x_hbm, w1_ref, w3_ref, w2_ref, y_ref,
                     xbuf_ref, sem, *, fc, tile):
    i = pl.program_id(0)
    f = pl.program_id(1)
    slot = i % 2
    n_used = nused_ref[0]

    def row_copy(t_ref, r, s):
        return pltpu.make_async_copy(x_hbm.at[pl.ds(t_ref[0, r], 1)], xbuf_ref.at[s, pl.ds(r, 1)], sem.at[s])

    def issue(t_ref, s):
        def body(r, carry):
            row_copy(t_ref, r, s).start()
            return carry
        lax.fori_loop(0, tile, body, 0, unroll=8)

    @pl.when(f == 0)
    def _():
        y_ref[...] = jnp.zeros_like(y_ref)

        @pl.when(i == 0)
        def _():
            issue(tok_ref, 0)

        @pl.when(i + 1 < n_used)
        def _():
            issue(tokn_ref, 1 - slot)

        @pl.when(i < n_used)
        def _():
            pltpu.make_async_copy(x_hbm.at[pl.ds(0, tile)], xbuf_ref.at[slot], sem.at[slot]).wait()

    @pl.when(i < n_used)
    def _():
        _swiglu_into(xbuf_ref[slot].astype(BF16), w1_ref, w3_ref, w2_ref, y_ref, fc)


def _moe_gemm(x_flat, row_token, tile_expert, n_used, w1, w3, w2, tile):
    n_rows = row_token.shape[0]
    n_tiles = n_rows // tile
    d = x_flat.shape[1]
    ff = w1.shape[2]
    nf = 2 if ff % (2 * LANES) == 0 else 1
    tf = ff // nf
    last = lambda i, nu: jnp.minimum(i, nu[0] - 1)
    fsel = lambda i, f, nu: jnp.where(i < nu[0], f, nf - 1)
    tok = lambda imap: pl.BlockSpec((None, 1, tile), imap, memory_space=pltpu.SMEM)
    tokens = row_token.reshape(n_tiles, 1, tile)
    return pl.pallas_call(
        functools.partial(_moe_gemm_kernel, fc=_pick_chunk(tf, 256), tile=tile),
        grid_spec=pltpu.PrefetchScalarGridSpec(
            num_scalar_prefetch=2,
            grid=(n_tiles, nf),
            in_specs=[tok(lambda i, f, te, nu: (i, 0, 0)),
                      tok(lambda i, f, te, nu: (jnp.minimum(i + 1, n_tiles - 1), 0, 0)),
                      pl.BlockSpec(memory_space=pl.ANY),
                      pl.BlockSpec((None, d, tf), lambda i, f, te, nu: (te[last(i, nu)], 0, fsel(i, f, nu))),
                      pl.BlockSpec((None, d, tf), lambda i, f, te, nu: (te[last(i, nu)], 0, fsel(i, f, nu))),
                      pl.BlockSpec((None, tf, d), lambda i, f, te, nu: (te[last(i, nu)], fsel(i, f, nu), 0))],
            out_specs=pl.BlockSpec((tile, d), lambda i, f, te, nu: (i, 0)),
            scratch_shapes=[pltpu.VMEM((2, tile, d), F32), pltpu.SemaphoreType.DMA((2,))]),
        out_shape=jax.ShapeDtypeStruct((n_rows, d), F32),
        compiler_params=_cparams(("arbitrary", "arbitrary")),
        name="moe_gemm",
    )(tile_expert, n_used, tokens, tokens, x_flat, w1, w3, w2)


def _moe_combine_kernel(pos_ref, posn_ref, route_ref, x1_ref, modx_ref, modz_ref, g2_ref, b2_ref, y_hbm,
                        o_ref, buf_ref, sem, *, nzb, nblk, alpha, tm):
    i = pl.program_id(0)
    slot = i % 2

    def row_copy(p_ref, r, k, s):
        return pltpu.make_async_copy(y_hbm.at[pl.ds(p_ref[0, k * tm + r], 1)],
                                     buf_ref.at[s, pl.ds(k * tm + r, 1)], sem.at[s])

    def issue(p_ref, s):
        def body(r, carry):
            row_copy(p_ref, r, 0, s).start()
            row_copy(p_ref, r, 1, s).start()
            return carry
        lax.fori_loop(0, tm, body, 0, unroll=8)

    @pl.when(i == 0)
    def _():
        issue(pos_ref, 0)

    @pl.when(i + 1 < pl.num_programs(0))
    def _():
        issue(posn_ref, 1 - slot)

    pltpu.make_async_copy(y_hbm.at[pl.ds(0, 2 * tm)], buf_ref.at[slot], sem.at[slot]).wait()
    route = route_ref[...]
    f = route[:, 0:1] * buf_ref[slot, 0:tm] + route[:, 1:2] * buf_ref[slot, tm:2 * tm]
    is_ctx = (i % nblk) < nzb
    mod = jnp.where(is_ctx, modz_ref[...], modx_ref[...])
    o_ref[...] = _ln(alpha * x1_ref[...] + mod[5:6] * f) * g2_ref[...] + b2_ref[...]


def _moe_combine(ys, pos, route, x1, mod, ln_g, ln_b, n_ctx_tokens, alpha):
    bsz, s, d = x1.shape
    tm = TOKEN_BLOCK
    nblk = s // tm
    n_steps = bsz * nblk
    nrow = mod.shape[0]
    pos_steps = pos.reshape(n_steps, tm, 2).transpose(0, 2, 1).reshape(n_steps, 1, 2 * tm)
    tok = lambda width: pl.BlockSpec((tm, width), lambda i: (i, 0))
    smem = lambda imap: pl.BlockSpec((None, 1, 2 * tm), imap, memory_space=pltpu.SMEM)
    vec = lambda a: a.reshape(1, -1).astype(F32)
    out = pl.pallas_call(
        functools.partial(_moe_combine_kernel, nzb=n_ctx_tokens // tm, nblk=nblk, alpha=alpha, tm=tm),
        grid=(n_steps,),
        in_specs=[smem(lambda i: (i, 0, 0)),
                  smem(lambda i: (jnp.minimum(i + 1, n_steps - 1), 0, 0)),
                  tok(LANES), tok(d),
                  pl.BlockSpec((None, 6, d), lambda i: (i // nblk, 0, 0)),
                  pl.BlockSpec((None, 6, d), lambda i: (nrow - 1, 0, 0)),
                  pl.BlockSpec((1, d), lambda i: (0, 0)), pl.BlockSpec((1, d), lambda i: (0, 0)),
                  pl.BlockSpec(memory_space=pl.ANY)],
        out_specs=tok(d),
        out_shape=jax.ShapeDtypeStruct((bsz * s, d), F32),
        scratch_shapes=[pltpu.VMEM((2, 2 * tm, d), F32), pltpu.SemaphoreType.DMA((2,))],
        compiler_params=_cparams(("arbitrary",)),
        name="moe_combine",
    )(pos_steps, pos_steps, route.reshape(bsz * s, LANES), x1.reshape(bsz * s, d), mod, mod,
      vec(ln_g), vec(ln_b), ys)
    return out.reshape(bsz, s, d)


def _moe(fx, x1, route, mod, w1, w3, w2, ln_g, ln_b, n_ctx_tokens, alpha):
    bsz, s, d = x1.shape
    n_exp = w1.shape[0]
    idx = route.reshape(bsz * s, LANES)[:, 2:4].astype(jnp.int32)
    row_token, pos, tile_expert, n_used = _route_plan(idx[:, 0], idx[:, 1], n_exp, MOE_TILE)
    ys = _moe_gemm(fx.reshape(bsz * s, d), row_token, tile_expert, n_used, w1, w3, w2, MOE_TILE)
    return _moe_combine(ys, pos, route, x1, mod, ln_g, ln_b, n_ctx_tokens, alpha)


def _pick_chunk(total, target):
    best = LANES
    for c in range(LANES, target + 1, LANES):
        if total % c == 0:
            best = c
    return best


def kernel(x, c, ctx, c_ctx, w_mod, b_mod, w_in, s5_lam_re, s5_lam_im, s5_log_step, s5_b_re, s5_b_im,
           s5_c_re, s5_c_im, s5_d, s5_w_glu, s5_b_glu, attn_sink, ret_log_gamma, w_out,
           ln1_g, ln1_b, ln2_g, ln2_b, ffn_w1, ffn_w3, ffn_w2, moe_router, moe_w1, moe_w3, moe_w2):
    bsz, t_len, d = x.shape
    n_ctx = ctx.shape[1]
    depth = w_in.shape[0]
    alpha = (2 * depth) ** 0.25
    s5_w = s5_d.shape[1]
    att_w = attn_sink.shape[1] * HEAD_DIM
    kv_w = att_w // ATT_REP
    ret_w = ret_log_gamma.shape[2] * HEAD_DIM
    sizes = (s5_w, att_w, kv_w, kv_w, ret_w, ret_w, ret_w, ret_w)
    assert sum(sizes) == w_in.shape[2] and s5_w + att_w + ret_w == w_out.shape[1]
    assert n_ctx % TOKEN_BLOCK == 0 and t_len % TOKEN_BLOCK == 0 and t_len >= 3 * ATT_BLOCK

    pad = (-(bsz + 1)) % 8
    cvec = jnp.concatenate([jnp.zeros((pad, d), F32), c_ctx[None].astype(F32)], axis=0)
    cvec = jnp.concatenate([c.astype(F32), cvec], axis=0)
    mod_all = _modulation(cvec, w_mod.astype(F32), b_mod.astype(F32)).reshape(depth, bsz + pad + 1, 6, d)

    tabs = _rope_tables(t_len)
    xz = jnp.concatenate([ctx, x], axis=1).astype(F32)
    for l in range(depth):
        need_ctx = l < depth - 1
        mod = mod_all[l]
        u, qa, va, qr, vr, gr, ka, kr = _inproj(xz, mod, w_in[l].astype(BF16), tabs, sizes, n_ctx)
        s5w = _s5_weights(s5_lam_re[l], s5_lam_im[l], s5_log_step[l], s5_b_re[l], s5_b_im[l],
                          s5_c_re[l], s5_c_im[l], S5_CHUNK)
        y_s5 = _s5_scan(u, s5w, n_ctx)
        o_att = _attention(qa, ka, va, attn_sink[l], n_ctx, need_ctx)
        o_ret = _retention(qr, kr, vr, gr, ret_log_gamma[l], n_ctx)
        i = l // 2
        router = None if l % 2 == 0 else moe_router[i]
        outs = _outproj(y_s5, u, o_att, o_ret, xz, mod, s5_d[l], s5_w_glu[l], s5_b_glu[l], w_out[l],
                        ln1_g[l], ln1_b[l], router, n_ctx, need_ctx, alpha)
        ctx_tokens = n_ctx if need_ctx else 0
        if l % 2 == 0:
            x1, fx = outs
            xz = _ffn(fx, x1, mod, ffn_w1[i].astype(BF16), ffn_w3[i].astype(BF16), ffn_w2[i].astype(BF16),
                      ln2_g[l], ln2_b[l], ctx_tokens, alpha)
        else:
            x1, fx, route = outs
            xz = _moe(fx, x1, route, mod, moe_w1[i].astype(BF16), moe_w3[i].astype(BF16),
                      moe_w2[i].astype(BF16), ln2_g[l], ln2_b[l], ctx_tokens, alpha)
    return xz if xz.shape[1] == t_len else xz[:, n_ctx:]
```

```python
import functools
import math

import jax
import jax.numpy as jnp
import numpy as np
from jax import lax
from jax.experimental import pallas as pl
from jax.experimental.pallas import tpu as pltpu

F32 = jnp.float32
BF16 = jnp.bfloat16

GRID_W = 64
HEAD_DIM = 64
S5_GROUP = 16
ATT_REP = 4
WINDOW = 128
ATT_BLOCK = 128
RET_CHUNK = 128
TOP_K = 2
LN_EPS = 1e-5
ROPE_BASE = 10000.0
NEG_INF = -1e30

LANES = 128
S5_CHUNK = LANES
TOKEN_BLOCK = 256
INPROJ_BLOCK = 768
FFN_BLOCK = 768
VMEM_LIMIT = 56 * 1024 * 1024


def _cparams(sem):
    return pltpu.CompilerParams(dimension_semantics=sem, vmem_limit_bytes=VMEM_LIMIT)


def _dot(a, b):
    return jnp.dot(a, b, preferred_element_type=F32)


def _dot_nt(a, b):
    return lax.dot_general(a, b, (((1,), (1,)), ((), ())), preferred_element_type=F32)


def _dot_tn(a, b):
    return lax.dot_general(a, b, (((0,), (0,)), ((), ())), preferred_element_type=F32)


def _split_bf16(a):
    hi = a.astype(BF16)
    lo = (a - hi.astype(F32)).astype(BF16)
    return hi, lo


def _dot3(a, b):
    ah, al = _split_bf16(a)
    bh, bl = _split_bf16(b)
    return _dot(ah, bh) + (_dot(ah, bl) + _dot(al, bh))


def _ln(x):
    mu = jnp.mean(x, axis=-1, keepdims=True)
    xc = x - mu
    var = jnp.mean(xc * xc, axis=-1, keepdims=True)
    return xc * lax.rsqrt(var + LN_EPS)


def _silu(x):
    return x * (1.0 / (1.0 + jnp.exp(-x)))


def _sigmoid(x):
    return 1.0 / (1.0 + jnp.exp(-x))


def _gelu_tanh(x):
    c = math.sqrt(2.0 / math.pi)
    return 0.5 * x * (1.0 + jnp.tanh(c * (x + 0.044715 * (x * x * x))))


def _mod_kernel(c_ref, w_ref, b_ref, o_ref):
    o_ref[...] = _dot3(_silu(c_ref[...]), w_ref[...]) + b_ref[...]


def _modulation(cvec, w_mod, b_mod):
    depth, d, n = w_mod.shape
    rows = cvec.shape[0]
    tn = 1536
    return pl.pallas_call(
        _mod_kernel,
        grid=(depth, n // tn),
        in_specs=[pl.BlockSpec((rows, d), lambda l, j: (0, 0)),
                  pl.BlockSpec((None, d, tn), lambda l, j: (l, 0, j)),
                  pl.BlockSpec((None, 1, tn), lambda l, j: (l, 0, j))],
        out_specs=pl.BlockSpec((None, rows, tn), lambda l, j: (l, 0, j)),
        out_shape=jax.ShapeDtypeStruct((depth, rows, n), F32),
        compiler_params=_cparams(("parallel", "parallel")),
        name="modulation",
    )(cvec, w_mod, b_mod.reshape(depth, 1, n))


def _rope_slab(xs, cos, sa, sb, half):
    return xs * cos + pltpu.roll(xs, LANES - half, 1) * sa + pltpu.roll(xs, half, 1) * sb


def _inproj_kernel(x_ref, modx_ref, modz_ref, wc_ref, w_ref, act_ref, ast_ref, rct_ref, rst_ref,
                   ac_ref, asa_ref, asb_ref, rc_ref, rsa_ref, rsb_ref,
                   ut_ref, qat_ref, vat_ref, qrt_ref, vrt_ref, grt_ref, ka_ref, kr_ref, *, n_ctx, c_sizes, sizes):
    tm = x_ref.shape[0]
    is_ctx = pl.program_id(1) * tm + lax.broadcasted_iota(jnp.int32, (tm, 1), 0) < n_ctx
    shift = jnp.where(is_ctx, modz_ref[0:1, :], modx_ref[0:1, :])
    gain = 1.0 + jnp.where(is_ctx, modz_ref[1:2, :], modx_ref[1:2, :])
    h = (_ln(x_ref[...]) * gain + shift).astype(BF16)
    scale = HEAD_DIM ** -0.5

    ct = _dot_nt(wc_ref[...], h)
    c_offs = [int(o) for o in np.concatenate([[0], np.cumsum(c_sizes)])]

    def chan(i):
        return ct[c_offs[i]:c_offs[i + 1]]

    def rope_t(x, cos_ref, sin_ref, part, mul):
        cos_t, sin_t = cos_ref[...], sin_ref[...]
        heads = []
        for hd in range(x.shape[0] // HEAD_DIM):
            xh = x[hd * HEAD_DIM:(hd + 1) * HEAD_DIM]
            pieces = []
            for lo in range(0, HEAD_DIM, 2 * part):
                pieces += [xh[lo + part:lo + 2 * part], xh[lo:lo + part]]
            out = xh * cos_t + jnp.concatenate(pieces, axis=0) * sin_t
            heads.append(out * mul if mul != 1.0 else out)
        return jnp.concatenate(heads, axis=0)

    def put(ref, val):
        for k in range(ref.shape[0]):
            ref[k] = val[:, k * ATT_BLOCK:(k + 1) * ATT_BLOCK]

    ut_ref[...] = chan(0)
    put(qat_ref, rope_t(chan(1), act_ref, ast_ref, HEAD_DIM // 4, scale))
    put(vat_ref, chan(2))
    put(qrt_ref, rope_t(chan(3), rct_ref, rst_ref, HEAD_DIM // 2, 1.0))
    put(vrt_ref, chan(4))
    put(grt_ref, chan(5))

    p = _dot(h, w_ref[...])
    ac, asa, asb = ac_ref[...], asa_ref[...], asb_ref[...]
    rc, rsa, rsb = rc_ref[...], rsa_ref[...], rsb_ref[...]
    offs = np.concatenate([[0], np.cumsum(sizes)])

    def rope_cols(i, cos, sa, sb, half, mul):
        blk = p[:, int(offs[i]):int(offs[i + 1])]
        slabs = [_rope_slab(blk[:, s:s + LANES], cos, sa, sb, half) for s in range(0, blk.shape[1], LANES)]
        out = slabs[0] if len(slabs) == 1 else jnp.concatenate(slabs, axis=1)
        return out * mul if mul != 1.0 else out

    ka_ref[...] = rope_cols(0, ac, asa, asb, HEAD_DIM // 4, 1.0)
    kr_ref[...] = rope_cols(1, rc, rsa, rsb, HEAD_DIM // 2, scale)


def _inproj(xz, mod, w_in, tabs, sizes, n_ctx):
    bsz, s, d = xz.shape
    tm = _pick_chunk(s, INPROJ_BLOCK)
    assert tm % TOKEN_BLOCK == 0, "the output projection reads the S5 arrays in TOKEN_BLOCK pieces"
    nrow = mod.shape[0]
    offs = np.concatenate([[0], np.cumsum(sizes)])
    col = lambda i: w_in[:, int(offs[i]):int(offs[i + 1])]
    c_idx, r_idx = (0, 1, 3, 4, 6, 7), (2, 5)
    c_sizes = tuple(sizes[i] for i in c_idx)
    r_sizes = tuple(sizes[i] for i in r_idx)
    w_c_t = jnp.concatenate([col(i) for i in c_idx], axis=1).T
    w_rest = jnp.concatenate([col(i) for i in r_idx], axis=1)
    sub = tm // ATT_BLOCK
    tok = lambda width: pl.BlockSpec((None, tm, width), lambda b, j: (b, j, 0))
    chan = lambda width: pl.BlockSpec((sub, None, width, ATT_BLOCK), lambda b, j: (j, b, 0, 0))
    tab = pl.BlockSpec((tm, LANES), lambda b, j: (j, 0))
    tab_t = pl.BlockSpec((HEAD_DIM, tm), lambda b, j: (0, j))
    return pl.pallas_call(
        functools.partial(_inproj_kernel, n_ctx=n_ctx, c_sizes=c_sizes, sizes=r_sizes),
        grid=(bsz, s // tm),
        in_specs=[tok(d),
                  pl.BlockSpec((None, 6, d), lambda b, j: (b, 0, 0)),
                  pl.BlockSpec((None, 6, d), lambda b, j: (nrow - 1, 0, 0)),
                  pl.BlockSpec((sum(c_sizes), d), lambda b, j: (0, 0)),
                  pl.BlockSpec((d, sum(r_sizes)), lambda b, j: (0, 0)),
                  tab_t, tab_t, tab_t, tab_t, tab, tab, tab, tab, tab, tab],
        out_specs=[pl.BlockSpec((None, None, c_sizes[0], tm), lambda b, j: (j, b, 0, 0))]
                  + [chan(w) for w in c_sizes[1:]] + [tok(w) for w in r_sizes],
        out_shape=[jax.ShapeDtypeStruct((s // tm, bsz, c_sizes[0], tm), F32)]
                  + [jax.ShapeDtypeStruct((s // ATT_BLOCK, bsz, w, ATT_BLOCK), F32) for w in c_sizes[1:]]
                  + [jax.ShapeDtypeStruct((bsz, s, w), F32) for w in r_sizes],
        compiler_params=_cparams(("parallel", "parallel")),
        name="inproj",
    )(xz, mod, mod, w_c_t, w_rest, *tabs)


def _rope_tables(t_len, n_ctx):
    t = jnp.arange(t_len)
    rows = (t // GRID_W).astype(F32)
    cols = (t % GRID_W).astype(F32)
    pos = t.astype(F32)

    def angles(p, dim):
        inv_freq = ROPE_BASE ** (-jnp.arange(0, dim, 2, dtype=F32) / dim)
        return p[:, None] * inv_freq[None, :]

    def head_tables(angs):
        cos = jnp.concatenate([jnp.concatenate([jnp.cos(a), jnp.cos(a)], -1) for a in angs], -1)
        sa = jnp.concatenate([jnp.concatenate([-jnp.sin(a), jnp.zeros_like(a)], -1) for a in angs], -1)
        sb = jnp.concatenate([jnp.concatenate([jnp.zeros_like(a), jnp.sin(a)], -1) for a in angs], -1)
        rep = LANES // HEAD_DIM
        return tuple(jnp.tile(x, (1, rep)) for x in (cos, sa, sb))

    att = head_tables([angles(rows, HEAD_DIM // 2), angles(cols, HEAD_DIM // 2)])
    ret = head_tables([angles(pos, HEAD_DIM)])
    signed_t = lambda tb: (tb[0][:, :HEAD_DIM].T, (tb[1] + tb[2])[:, :HEAD_DIM].T)

    def with_ctx(tb, axis, is_cos):
        shape = list(tb.shape)
        shape[axis] = n_ctx
        lead = jnp.ones(shape, F32) if is_cos else jnp.zeros(shape, F32)
        return jnp.concatenate([lead, tb], axis=axis)

    chan = signed_t(att) + signed_t(ret)
    toks = att + ret
    return (tuple(with_ctx(tb, 1, k % 2 == 0) for k, tb in enumerate(chan))
            + tuple(with_ctx(tb, 0, k % 3 == 0) for k, tb in enumerate(toks)))


def _s5_kernel(u_ref, tab_ref, wsf_ref, wsb_ref, wrf_ref, wrb_ref, lam_ref, y_ref,
               lhs_ref, m_ref, acc_ref, cf_ref, cb_ref, pf_ref, pb_ref, *, bsz, nz_chunks):
    r_blk, ch, tm = u_ref.shape
    c = S5_CHUNK
    kpb = tm // c
    n_chunks = (r_blk // bsz) * kpb

    for i in range(ch):
        x = u_ref[:, i, :].astype(BF16)
        p, half = divmod(i, 2)
        for k in range(kpb):
            lhs_ref[p, k * r_blk:(k + 1) * r_blk, half * c:(half + 1) * c] = x[:, k * c:(k + 1) * c]
    acc_ref[...] = jnp.zeros_like(acc_ref)
    cf_ref[...] = jnp.zeros_like(cf_ref)
    cb_ref[...] = jnp.zeros_like(cb_ref)

    def pair(p, carry):
        for ih in range(2):
            for o in range(ch):
                lag_row = jnp.broadcast_to(tab_ref[2 * p + ih, o:o + 1, :], (8, 2 * c))
                tab = pltpu.roll(lag_row, 0, 1, stride=1, stride_axis=0)
                for r2 in range(c // 16):
                    lo = c - 16 * r2
                    rows = jnp.concatenate([tab[:, lo:lo + c], tab[:, lo - 8:lo - 8 + c]], axis=0)
                    m_ref[ih * c + 16 * r2:ih * c + 16 * r2 + 16, o * c:(o + 1) * c] = rows.astype(BF16)
        lhs = lhs_ref[p]
        acc_ref[...] += _dot(lhs, m_ref[...])
        cf_ref[...] += _dot(lhs, wsf_ref[p])
        cb_ref[...] += _dot(lhs, wsb_ref[p])
        return carry

    lax.fori_loop(0, ch // 2, pair, 0)
    lam = lam_ref[...]
    nst = lam.shape[1] // 2

    def advance(s, a, bc, drive):
        return s * a + pltpu.roll(s, nst, 1) * bc + drive

    def sweep(order, c_ref, p_ref, a, bc):
        s = jnp.zeros((bsz, lam.shape[1]), F32)
        for n in order:
            j, k = divmod(n, kpb)
            rows = slice(k * r_blk + j * bsz, k * r_blk + (j + 1) * bsz)
            p_ref[rows, :] = s
            s = advance(s, a, bc, c_ref[rows, :])

    sweep(list(range(n_chunks)), cf_ref, pf_ref, lam[0:1], lam[1:2])
    order_b = list(range(nz_chunks - 1, -1, -1)) + list(range(n_chunks - 1, nz_chunks - 1, -1))
    sweep(order_b, cb_ref, pb_ref, lam[2:3], lam[3:4])
    acc_ref[...] += (_dot(pf_ref[...].astype(BF16), wrf_ref[...])
                     + _dot(pb_ref[...].astype(BF16), wrb_ref[...]))
    for o in range(ch):
        for k in range(kpb):
            y_ref[:, o, k * c:(k + 1) * c] = acc_ref[k * r_blk:(k + 1) * r_blk, o * c:(o + 1) * c]


def _s5_weights(lam_re, lam_im, log_step, b_re, b_im, c_re, c_im, chunk):
    hp = lax.Precision.HIGHEST
    lam = lax.complex(lam_re.astype(F32), lam_im.astype(F32))
    lam_dt = lam * jnp.exp(log_step.astype(F32))[..., None]
    lam_bar = jnp.exp(lam_dt)
    b_bar = lax.complex(b_re.astype(F32), b_im.astype(F32)) * ((lam_bar - 1.0) / lam)[..., None]
    c_mat = lax.complex(c_re.astype(F32), c_im.astype(F32))
    g, n = lam.shape[1], lam.shape[2]
    ch = b_bar.shape[-1]
    steps = jnp.arange(chunk + 1, dtype=F32)
    pw = jnp.exp(steps[None, :, None, None] * lam_dt[:, None])
    kern = jnp.einsum('zgon,zdgn,zgni->zgdoi', c_mat, pw[:, :chunk], b_bar, precision=hp).real
    zero_lag = kern[0, :, 0] + kern[1, :, 0]
    lag_table = jnp.concatenate([jnp.zeros_like(zero_lag)[:, None], kern[1, :, :0:-1], zero_lag[:, None],
                                 kern[0, :, 1:]], axis=1).transpose(0, 3, 2, 1)

    def state_in(pw_s, b_dir):
        w = pw_s[:, :, :, None] * b_dir[None]
        w = jnp.concatenate([w.real, w.imag], axis=2)
        return w.transpose(1, 3, 0, 2).reshape(g, ch // 2, 2 * chunk, 2 * n)

    def state_out(pw_t, c_dir):
        w = c_dir[None] * pw_t[:, :, None, :]
        w = jnp.concatenate([w.real, -w.imag], axis=3)
        return w.transpose(1, 3, 2, 0).reshape(g, 2 * n, ch * chunk)

    wsf = state_in(pw[0, chunk - 1::-1][:chunk], b_bar[0])
    wsb = state_in(pw[1, :chunk], b_bar[1])
    wrf = state_out(pw[0, 1:chunk + 1], c_mat[0])
    wrb = state_out(pw[1, chunk:0:-1], c_mat[1])
    lam_c = pw[:, chunk]
    rows = []
    for z in range(2):
        rows.append(jnp.concatenate([lam_c[z].real, lam_c[z].real], -1))
        rows.append(jnp.concatenate([-lam_c[z].imag, lam_c[z].imag], -1))
    lam_rows = jnp.stack(rows + [jnp.zeros_like(rows[0])] * 4, axis=1)
    return lag_table, wsf.astype(BF16), wsb.astype(BF16), wrf.astype(BF16), wrb.astype(BF16), lam_rows


def _s5_scan(ut, weights, n_ctx):
    nblk, bsz, width, tm = ut.shape
    tab, wsf, wsb, wrf, wrb, lam_rows = weights
    g, ch = tab.shape[0], tab.shape[1]
    c = S5_CHUNK
    nst2 = wsf.shape[3]
    r_blk = nblk * bsz
    rows = r_blk * (tm // c)
    grp = lambda *shape: pl.BlockSpec((None,) + shape, lambda i: (i,) + (0,) * len(shape))
    tok = pl.BlockSpec((r_blk, ch, tm), lambda i: (0, i, 0))
    y = pl.pallas_call(
        functools.partial(_s5_kernel, bsz=bsz, nz_chunks=n_ctx // c),
        grid=(g,),
        in_specs=[tok, grp(ch, ch, 2 * c), grp(ch // 2, 2 * c, nst2), grp(ch // 2, 2 * c, nst2),
                  grp(nst2, ch * c), grp(nst2, ch * c), grp(8, nst2)],
        out_specs=tok,
        out_shape=jax.ShapeDtypeStruct((r_blk, width, tm), F32),
        scratch_shapes=[pltpu.VMEM((ch // 2, rows, 2 * c), BF16),
                        pltpu.VMEM((2 * c, ch * c), BF16),
                        pltpu.VMEM((rows, ch * c), F32)] + [pltpu.VMEM((rows, nst2), F32)] * 4,
        compiler_params=_cparams(("parallel",)),
        name="s5_scan",
    )(ut.reshape(r_blk, width, tm), tab, wsf, wsb, wrf, wrb, lam_rows)
    return y.reshape(nblk, bsz, width, tm)


def _attn_kernel(sink_ref, q_ref, k_ref, v_ref, o_ref, *, n_ctx, t_len, q_off, kv_heads):
    qi = pl.program_id(1) + q_off
    nzb = n_ctx // ATT_BLOCK
    band = 3 * ATT_BLOCK
    wide = ATT_REP * ATT_BLOCK
    q = q_ref[...].astype(BF16)
    kc = k_ref[0:n_ctx, :].astype(BF16)
    vc = jnp.concatenate([v_ref[p] for p in range(nzb)], axis=1).astype(BF16)

    def rows(a, i):
        return a[i * HEAD_DIM:(i + 1) * HEAD_DIM]

    def q_group(g):
        qg = jnp.concatenate([rows(q, g * ATT_REP + r) for r in range(ATT_REP)], axis=1)
        zero = jnp.zeros_like(qg)
        return jnp.concatenate([qg if gg == g else zero for gg in range(kv_heads)], axis=0)

    def finish(g, o, den):
        o = o / den
        for r in range(ATT_REP):
            h = g * ATT_REP + r
            o_ref[h * HEAD_DIM:(h + 1) * HEAD_DIM, :] = o[:, r * ATT_BLOCK:(r + 1) * ATT_BLOCK]

    @pl.when(qi < nzb)
    def _():
        for g in range(kv_heads):
            sink = sink_ref[g:g + 1, :]
            s = _dot(kc, q_group(g))
            m = jnp.maximum(jnp.max(s, axis=0, keepdims=True), sink)
            e = jnp.exp(s - m)
            den = jnp.sum(e, axis=0, keepdims=True) + jnp.exp(sink - m)
            finish(g, _dot(rows(vc, g), e.astype(BF16)), den)

    @pl.when(qi >= nzb)
    def _():
        n = qi - nzb
        start = jnp.clip((n - 1) * ATT_BLOCK, 0, t_len - band)
        kl = k_ref[pl.ds(pl.multiple_of(n_ctx + start, ATT_BLOCK), band), :].astype(BF16)
        p0 = (n_ctx + start) // ATT_BLOCK
        vl = jnp.concatenate([v_ref[p0 + t] for t in range(3)], axis=1).astype(BF16)
        k_pos = start + lax.broadcasted_iota(jnp.int32, (band, wide), 0)
        q_pos = n * ATT_BLOCK + (lax.broadcasted_iota(jnp.int32, (band, wide), 1) & (ATT_BLOCK - 1))
        valid = jnp.abs(k_pos - q_pos) <= WINDOW
        for g in range(kv_heads):
            sink = sink_ref[g:g + 1, :]
            qg = q_group(g)
            s_loc = jnp.where(valid, _dot(kl, qg), NEG_INF)
            s_ctx = _dot(kc, qg)
            m = jnp.maximum(jnp.maximum(jnp.max(s_loc, axis=0, keepdims=True),
                                        jnp.max(s_ctx, axis=0, keepdims=True)), sink)
            e_loc = jnp.exp(s_loc - m)
            e_ctx = jnp.exp(s_ctx - m)
            den = (jnp.sum(e_loc, axis=0, keepdims=True) + jnp.sum(e_ctx, axis=0, keepdims=True)
                   + jnp.exp(sink - m))
            o = _dot(rows(vl, g), e_loc.astype(BF16)) + _dot(rows(vc, g), e_ctx.astype(BF16))
            finish(g, o, den)


def _attention(qat, ka, vat, sink, n_ctx, need_ctx):
    nblk, bsz, qw, _ = qat.shape
    s, kvw = ka.shape[1], ka.shape[2]
    t_len = s - n_ctx
    kv_heads = kvw // HEAD_DIM
    q_off = 0 if need_ctx else n_ctx // ATT_BLOCK
    nq = nblk - q_off
    sink_rows = jnp.repeat(sink.astype(F32).reshape(kv_heads, ATT_REP), ATT_BLOCK, axis=1)
    return pl.pallas_call(
        functools.partial(_attn_kernel, n_ctx=n_ctx, t_len=t_len, q_off=q_off, kv_heads=kv_heads),
        grid=(bsz, nq),
        in_specs=[pl.BlockSpec(sink_rows.shape, lambda b, j: (0, 0)),
                  pl.BlockSpec((None, None, qw, ATT_BLOCK), lambda b, j: (j + q_off, b, 0, 0)),
                  pl.BlockSpec((None, s, kvw), lambda b, j: (b, 0, 0)),
                  pl.BlockSpec((nblk, None, kvw, ATT_BLOCK), lambda b, j: (0, b, 0, 0))],
        out_specs=pl.BlockSpec((None, None, qw, ATT_BLOCK), lambda b, j: (j, b, 0, 0)),
        out_shape=jax.ShapeDtypeStruct((nq, bsz, qw, ATT_BLOCK), F32),
        compiler_params=_cparams(("parallel", "arbitrary")),
        name="window_attention",
    )(sink_rows, qat, ka, vat)


def _ret_kernel(lg_ref, q_ref, k_ref, v_ref, g_ref, o_ref, acc_ref, sf_ref, sb_ref, *, nz_chunks, heads):
    c = RET_CHUNK
    n_chunks, w, _ = q_ref.shape
    f32 = lambda a: a.astype(F32)

    def per_head(shape, axis, group, direction):
        owner = lax.broadcasted_iota(jnp.int32, shape, axis) // group
        out = jnp.zeros(shape, F32)
        for h in range(heads):
            out = jnp.where(owner == h, lg_ref[direction, h], out)
        return out

    row_i = f32(lax.broadcasted_iota(jnp.int32, (w, c), 1))
    qw_f = jnp.exp(per_head((w, c), 0, HEAD_DIM, 0) * (row_i + 1.0))
    qw_b = jnp.exp(per_head((w, c), 0, HEAD_DIM, 1) * (c - row_i))
    key_j = f32(lax.broadcasted_iota(jnp.int32, (c, w), 0))
    kw_f = jnp.exp(per_head((c, w), 1, HEAD_DIM, 0) * (c - 1.0 - key_j))
    kw_b = jnp.exp(per_head((c, w), 1, HEAD_DIM, 1) * key_j)
    dec_f = jnp.exp(per_head((w, 1), 0, HEAD_DIM, 0) * c)
    dec_b = jnp.exp(per_head((w, 1), 0, HEAD_DIM, 1) * c)
    wide = heads * c
    diff = f32((lax.broadcasted_iota(jnp.int32, (c, wide), 1) & (c - 1)) - lax.broadcasted_iota(jnp.int32, (c, wide), 0))
    dec_t = jnp.where(diff >= 0, jnp.exp(per_head((c, wide), 1, c, 0) * jnp.maximum(diff, 0.0)),
                      jnp.exp(per_head((c, wide), 1, c, 1) * jnp.maximum(-diff, 0.0)))
    own_wide = (lax.broadcasted_iota(jnp.int32, (w, wide), 0) // HEAD_DIM
                == lax.broadcasted_iota(jnp.int32, (w, wide), 1) // c)
    own_sq = f32(lax.broadcasted_iota(jnp.int32, (w, w), 0) // HEAD_DIM
                 == lax.broadcasted_iota(jnp.int32, (w, w), 1) // HEAD_DIM)
    sf_ref[...] = jnp.zeros_like(sf_ref)
    sb_ref[...] = jnp.zeros_like(sb_ref)

    def block_diag(x):
        return jnp.where(own_wide, jnp.concatenate([x] * heads, axis=1), 0.0).astype(BF16)

    def fwd(n, carry):
        qt, vt = q_ref[n], v_ref[n]
        k = k_ref[pl.ds(pl.multiple_of(n * c, c), c), :]
        scores_t = _dot(k.astype(BF16), block_diag(qt)) * dec_t
        stacked = jnp.concatenate([scores_t[:, h * c:(h + 1) * c] for h in range(heads)], axis=0)
        o = _dot(block_diag(vt), stacked.astype(BF16))
        s_prev = sf_ref[...]
        o = o + _dot(s_prev.astype(BF16), (qt * qw_f).astype(BF16))
        sf_ref[...] = dec_f * s_prev + own_sq * _dot(vt.astype(BF16), (k * kw_f).astype(BF16))
        acc_ref[n] = o
        return carry

    lax.fori_loop(0, n_chunks, fwd, 0)

    def bwd(i, carry):
        n = jnp.where(i < nz_chunks, nz_chunks - 1 - i, n_chunks - 1 - i + nz_chunks)
        qt, vt = q_ref[n], v_ref[n]
        k = k_ref[pl.ds(pl.multiple_of(n * c, c), c), :]
        s_prev = sb_ref[...]
        o = acc_ref[n] + _dot(s_prev.astype(BF16), (qt * qw_b).astype(BF16))
        sb_ref[...] = dec_b * s_prev + own_sq * _dot(vt.astype(BF16), (k * kw_b).astype(BF16))
        gate = _silu(g_ref[n])
        for h in range(heads):
            rows = slice(h * HEAD_DIM, (h + 1) * HEAD_DIM)
            oh = o[rows]
            mu = jnp.mean(oh, axis=0, keepdims=True)
            oc = oh - mu
            var = jnp.mean(oc * oc, axis=0, keepdims=True)
            o_ref[n, rows, :] = oc * lax.rsqrt(var + LN_EPS) * gate[rows]
        return carry

    lax.fori_loop(0, n_chunks, bwd, 0)


def _retention(qrt, kr, vrt, grt, log_gamma, n_ctx):
    n_chunks, bsz, w, c = qrt.shape
    s = kr.shape[1]
    chan = pl.BlockSpec((n_chunks, None, w, c), lambda b: (0, b, 0, 0))
    return pl.pallas_call(
        functools.partial(_ret_kernel, nz_chunks=n_ctx // RET_CHUNK, heads=w // HEAD_DIM),
        grid=(bsz,),
        in_specs=[pl.BlockSpec(memory_space=pltpu.SMEM), chan, pl.BlockSpec((None, s, w), lambda b: (b, 0, 0)),
                  chan, chan],
        out_specs=chan,
        out_shape=jax.ShapeDtypeStruct((n_chunks, bsz, w, c), F32),
        scratch_shapes=[pltpu.VMEM((n_chunks, w, c), F32), pltpu.VMEM((w, w), F32), pltpu.VMEM((w, w), F32)],
        compiler_params=_cparams(("parallel",)),
        name="retention",
    )(log_gamma.astype(F32), qrt, kr, vrt, grt)


def _outproj_kernel(*refs, nzb, alpha, with_router, w_s5, w_att, n_exp):
    if with_router:
        (y_ref, u_ref, a_ref, r_ref, x_ref, modx_ref, modz_ref, d_ref, wg_ref, bg_ref, wo_ref,
         g1_ref, b1_ref, rt_ref, x1_ref, fx_ref, gate_ref) = refs
    else:
        (y_ref, u_ref, a_ref, r_ref, x_ref, modx_ref, modz_ref, d_ref, wg_ref, bg_ref, wo_ref,
         g1_ref, b1_ref, x1_ref, fx_ref) = refs
    is_ctx = pl.program_id(1) < nzb
    mod = jnp.where(is_ctx, modz_ref[...], modx_ref[...])
    g = _gelu_tanh(y_ref[...] + d_ref[...] * u_ref[...])
    s5 = g * _sigmoid(_dot(wg_ref[...], g.astype(BF16)) + bg_ref[...])
    def chan_major(ref, lo, hi):
        return jnp.concatenate([_dot_tn(ref[k].astype(BF16), wo_ref[lo:hi, :]) for k in range(ref.shape[0])],
                               axis=0)

    mix = (_dot_tn(s5.astype(BF16), wo_ref[0:w_s5, :]) + chan_major(a_ref, w_s5, w_s5 + w_att)
           + chan_major(r_ref, w_s5 + w_att, wo_ref.shape[0]))
    x1 = _ln(alpha * x_ref[...] + mod[2:3] * mix) * g1_ref[...] + b1_ref[...]
    x1_ref[...] = x1
    fx = _ln(x1) * (1.0 + mod[4:5]) + mod[3:4]
    fx_ref[...] = fx.astype(fx_ref.dtype)
    if with_router:
        lane = lax.broadcasted_iota(jnp.int32, (fx.shape[0], LANES), 1)
        logits = jnp.where(lane < n_exp, _dot3(fx, rt_ref[...]), -jnp.inf)
        m1 = jnp.max(logits, axis=1, keepdims=True)
        i1 = jnp.min(jnp.where(logits == m1, lane, LANES), axis=1, keepdims=True)
        rest = jnp.where(lane == i1, -jnp.inf, logits)
        m2 = jnp.max(rest, axis=1, keepdims=True)
        i2 = jnp.min(jnp.where(rest == m2, lane, LANES), axis=1, keepdims=True)
        e2 = jnp.exp(m2 - m1)
        den = 1.0 + e2
        route = jnp.where(lane == 0, 1.0 / den, jnp.where(lane == 1, e2 / den, 0.0))
        route = jnp.where(lane == 2, i1.astype(F32), jnp.where(lane == 3, i2.astype(F32), route))
        gate_ref[...] = route


def _outproj(y_s5, u, o_att, o_ret, xz, mod, s5_d, w_glu, b_glu, w_out, ln_g, ln_b, router, n_ctx, need_ctx,
             alpha):
    bsz, s, d = xz.shape
    tm = TOKEN_BLOCK
    nzb = n_ctx // tm
    off = 0 if need_ctx else nzb
    nblk = s // tm - off
    s_out = nblk * tm
    nrow = mod.shape[0]
    w_s5, w_att = y_s5.shape[2], o_att.shape[2]
    sub = tm // ATT_BLOCK
    att_off = off if o_att.shape[0] == s // ATT_BLOCK else 0
    att = pl.BlockSpec((sub, None, w_att, ATT_BLOCK), lambda b, j: (j + att_off, b, 0, 0))

    def tok(width, shift):
        return pl.BlockSpec((None, tm, width), lambda b, j: (b, j + shift, 0))

    def full(a):
        return pl.BlockSpec(a.shape, lambda b, j: (0,) * a.ndim)

    per_in = y_s5.shape[3] // tm
    chan = pl.BlockSpec((None, None, w_s5, tm), lambda b, j: ((j + off) // per_in, b, 0, (j + off) % per_in))
    vec = lambda a: a.reshape(1, -1).astype(F32)
    col = lambda a: a.reshape(-1, 1).astype(F32)
    consts = [col(s5_d), w_glu.T.astype(BF16), col(b_glu), w_out.astype(BF16), vec(ln_g), vec(ln_b)]
    ret = pl.BlockSpec((sub, None, o_ret.shape[2], ATT_BLOCK), lambda b, j: (j + off, b, 0, 0))
    in_specs = [chan, chan, att, ret, tok(d, off),
                pl.BlockSpec((None, 6, d), lambda b, j: (b, 0, 0)),
                pl.BlockSpec((None, 6, d), lambda b, j: (nrow - 1, 0, 0))] + [full(a) for a in consts]
    out_specs = [tok(d, 0), tok(d, 0)]
    fx_dtype = BF16 if router is None else F32
    out_shape = [jax.ShapeDtypeStruct((bsz, s_out, d), F32), jax.ShapeDtypeStruct((bsz, s_out, d), fx_dtype)]
    args = [y_s5, u, o_att, o_ret, xz, mod, mod] + consts
    with_router = router is not None
    n_exp = 0
    if with_router:
        n_exp = router.shape[1]
        router_pad = jnp.pad(router.astype(F32), ((0, 0), (0, LANES - n_exp)))
        args.append(router_pad)
        in_specs.append(full(router_pad))
        out_specs.append(tok(LANES, 0))
        out_shape.append(jax.ShapeDtypeStruct((bsz, s_out, LANES), F32))
    return pl.pallas_call(
        functools.partial(_outproj_kernel, nzb=nzb - off, alpha=alpha, with_router=with_router,
                          w_s5=w_s5, w_att=w_att, n_exp=n_exp),
        grid=(bsz, nblk),
        in_specs=in_specs,
        out_specs=out_specs,
        out_shape=out_shape,
        compiler_params=_cparams(("parallel", "parallel")),
        name="outproj",
    )(*args)


def _swiglu_into(x_bf16, w1_ref, w3_ref, w2_ref, acc_ref, fc):
    for s in range(0, w1_ref.shape[1], fc):
        h1 = _dot(x_bf16, w1_ref[:, s:s + fc])
        h3 = _dot(x_bf16, w3_ref[:, s:s + fc])
        acc_ref[...] += _dot((_silu(h1) * h3).astype(BF16), w2_ref[s:s + fc, :])


def _ffn_kernel(fx_ref, x1_ref, modx_ref, modz_ref, w1_ref, w3_ref, w2_ref, g2_ref, b2_ref,
                o_ref, acc_ref, *, n_ctx_tokens, alpha, fc):
    tm = acc_ref.shape[0]
    acc_ref[...] = jnp.zeros_like(acc_ref)
    _swiglu_into(fx_ref[...], w1_ref, w3_ref, w2_ref, acc_ref, fc)
    row = pl.program_id(1) * tm + lax.broadcasted_iota(jnp.int32, (tm, 1), 0)
    gate = jnp.where(row < n_ctx_tokens, modz_ref[5:6, :], modx_ref[5:6, :])
    o_ref[...] = _ln(alpha * x1_ref[...] + gate * acc_ref[...]) * g2_ref[...] + b2_ref[...]


def _ffn(fx, x1, mod, w1, w3, w2, ln_g, ln_b, n_ctx_tokens, alpha):
    bsz, s, d = x1.shape
    ff = w1.shape[1]
    tm = _pick_chunk(s, FFN_BLOCK)
    nrow = mod.shape[0]
    tok = pl.BlockSpec((None, tm, d), lambda b, j: (b, j, 0))
    full = lambda a: pl.BlockSpec(a.shape, lambda b, j: (0,) * a.ndim, pipeline_mode=pl.Buffered(1))
    vec = lambda a: a.reshape(1, -1).astype(F32)
    return pl.pallas_call(
        functools.partial(_ffn_kernel, n_ctx_tokens=n_ctx_tokens, alpha=alpha, fc=_pick_chunk(ff, 512)),
        grid=(bsz, s // tm),
        in_specs=[tok, tok,
                  pl.BlockSpec((None, 6, d), lambda b, j: (b, 0, 0)),
                  pl.BlockSpec((None, 6, d), lambda b, j: (nrow - 1, 0, 0)),
                  full(w1), full(w3), full(w2),
                  pl.BlockSpec((1, d), lambda b, j: (0, 0)), pl.BlockSpec((1, d), lambda b, j: (0, 0))],
        out_specs=tok,
        out_shape=jax.ShapeDtypeStruct((bsz, s, d), F32),
        scratch_shapes=[pltpu.VMEM((tm, d), F32)],
        compiler_params=_cparams(("parallel", "parallel")),
        name="dense_ffn",
    )(fx, x1, mod, mod, w1, w3, w2, vec(ln_g), vec(ln_b))


MOE_TILE = 1024


def _route_plan(e1, e2, n_exp, tile):
    n = e1.shape[0]
    pair_e = jnp.stack([e1, e2], axis=1).reshape(-1)
    onehot = (pair_e[:, None] == jnp.arange(n_exp, dtype=jnp.int32)[None, :]).astype(jnp.int32)
    before = jnp.cumsum(onehot, axis=0) - onehot
    rank = jnp.sum(before * onehot, axis=1)
    counts = jnp.sum(onehot, axis=0)
    padded = (counts + tile - 1) // tile * tile
    ends = jnp.cumsum(padded)
    starts = ends - padded
    dest = starts[pair_e] + rank
    n_rows = (2 * n + n_exp * (tile - 1)) // tile * tile
    n_tiles = n_rows // tile
    row_token = jnp.zeros((n_rows,), jnp.int32).at[dest].set(
        jnp.arange(2 * n, dtype=jnp.int32) // 2, unique_indices=True, mode="promise_in_bounds")
    tile_start = jnp.arange(n_tiles, dtype=jnp.int32) * tile
    tile_expert = jnp.minimum(jnp.sum((tile_start[:, None] >= ends[None, :]).astype(jnp.int32), axis=1),
                              n_exp - 1)
    n_used = (ends[-1] // tile).astype(jnp.int32).reshape(1)
    return row_token, dest.reshape(n, 2), tile_expert, n_used


def _moe_gemm_kernel(te_ref, nused_ref, tok_ref, tokn_ref, x_hbm, w1_ref, w3_ref, w2_ref, y_ref,
                     xbuf_ref, sem, *, fc, tile):
    i = pl.program_id(0)
    f = pl.program_id(1)
    slot = i % 2
    n_used = nused_ref[0]

    def row_copy(t_ref, r, s):
        return pltpu.make_async_copy(x_hbm.at[pl.ds(t_ref[0, r], 1)], xbuf_ref.at[s, pl.ds(r, 1)], sem.at[s])

    def issue(t_ref, s):
        def body(r, carry):
            row_copy(t_ref, r, s).start()
            return carry
        lax.fori_loop(0, tile, body, 0, unroll=8)

    @pl.when(f == 0)
    def _():
        y_ref[...] = jnp.zeros_like(y_ref)

        @pl.when(i == 0)
        def _():
            issue(tok_ref, 0)

        @pl.when(i + 1 < n_used)
        def _():
            issue(tokn_ref, 1 - slot)

        @pl.when(i < n_used)
        def _():
            pltpu.make_async_copy(x_hbm.at[pl.ds(0, tile)], xbuf_ref.at[slot], sem.at[slot]).wait()

    @pl.when(i < n_used)
    def _():
        _swiglu_into(xbuf_ref[slot].astype(BF16), w1_ref, w3_ref, w2_ref, y_ref, fc)


def _moe_gemm(x_flat, row_token, tile_expert, n_used, w1, w3, w2, tile):
    n_rows = row_token.shape[0]
    n_tiles = n_rows // tile
    d = x_flat.shape[1]
    ff = w1.shape[2]
    nf = 2 if ff % (2 * LANES) == 0 else 1
    tf = ff // nf
    last = lambda i, nu: jnp.minimum(i, nu[0] - 1)
    fsel = lambda i, f, nu: jnp.where(i < nu[0], f, nf - 1)
    tok = lambda imap: pl.BlockSpec((None, 1, tile), imap, memory_space=pltpu.SMEM)
    tokens = row_token.reshape(n_tiles, 1, tile)
    return pl.pallas_call(
        functools.partial(_moe_gemm_kernel, fc=_pick_chunk(tf, 256), tile=tile),
        grid_spec=pltpu.PrefetchScalarGridSpec(
            num_scalar_prefetch=2,
            grid=(n_tiles, nf),
            in_specs=[tok(lambda i, f, te, nu: (i, 0, 0)),
                      tok(lambda i, f, te, nu: (jnp.minimum(i + 1, n_tiles - 1), 0, 0)),
                      pl.BlockSpec(memory_space=pl.ANY),
                      pl.BlockSpec((None, d, tf), lambda i, f, te, nu: (te[last(i, nu)], 0, fsel(i, f, nu))),
                      pl.BlockSpec((None, d, tf), lambda i, f, te, nu: (te[last(i, nu)], 0, fsel(i, f, nu))),
                      pl.BlockSpec((None, tf, d), lambda i, f, te, nu: (te[last(i, nu)], fsel(i, f, nu), 0))],
            out_specs=pl.BlockSpec((tile, d), lambda i, f, te, nu: (i, 0)),
            scratch_shapes=[pltpu.VMEM((2, tile, d), F32), pltpu.SemaphoreType.DMA((2,))]),
        out_shape=jax.ShapeDtypeStruct((n_rows, d), F32),
        compiler_params=_cparams(("arbitrary", "arbitrary")),
        name="moe_gemm",
    )(tile_expert, n_used, tokens, tokens, x_flat, w1, w3, w2)


def _moe_combine_kernel(pos_ref, posn_ref, route_ref, x1_ref, modx_ref, modz_ref, g2_ref, b2_ref, y_hbm,
                        o_ref, buf_ref, sem, *, nzb, nblk, alpha, tm):
    i = pl.program_id(0)
    slot = i % 2

    def row_copy(p_ref, r, k, s):
        return pltpu.make_async_copy(y_hbm.at[pl.ds(p_ref[0, k * tm + r], 1)],
                                     buf_ref.at[s, pl.ds(k * tm + r, 1)], sem.at[s])

    def issue(p_ref, s):
        def body(r, carry):
            row_copy(p_ref, r, 0, s).start()
            row_copy(p_ref, r, 1, s).start()
            return carry
        lax.fori_loop(0, tm, body, 0, unroll=8)

    @pl.when(i == 0)
    def _():
        issue(pos_ref, 0)

    @pl.when(i + 1 < pl.num_programs(0))
    def _():
        issue(posn_ref, 1 - slot)

    pltpu.make_async_copy(y_hbm.at[pl.ds(0, 2 * tm)], buf_ref.at[slot], sem.at[slot]).wait()
    route = route_ref[...]
    f = route[:, 0:1] * buf_ref[slot, 0:tm] + route[:, 1:2] * buf_ref[slot, tm:2 * tm]
    is_ctx = (i % nblk) < nzb
    mod = jnp.where(is_ctx, modz_ref[...], modx_ref[...])
    o_ref[...] = _ln(alpha * x1_ref[...] + mod[5:6] * f) * g2_ref[...] + b2_ref[...]


def _moe_combine(ys, pos, route, x1, mod, ln_g, ln_b, n_ctx_tokens, alpha):
    bsz, s, d = x1.shape
    tm = TOKEN_BLOCK
    nblk = s // tm
    n_steps = bsz * nblk
    nrow = mod.shape[0]
    pos_steps = pos.reshape(n_steps, tm, 2).transpose(0, 2, 1).reshape(n_steps, 1, 2 * tm)
    tok = lambda width: pl.BlockSpec((tm, width), lambda i: (i, 0))
    smem = lambda imap: pl.BlockSpec((None, 1, 2 * tm), imap, memory_space=pltpu.SMEM)
    vec = lambda a: a.reshape(1, -1).astype(F32)
    out = pl.pallas_call(
        functools.partial(_moe_combine_kernel, nzb=n_ctx_tokens // tm, nblk=nblk, alpha=alpha, tm=tm),
        grid=(n_steps,),
        in_specs=[smem(lambda i: (i, 0, 0)),
                  smem(lambda i: (jnp.minimum(i + 1, n_steps - 1), 0, 0)),
                  tok(LANES), tok(d),
                  pl.BlockSpec((None, 6, d), lambda i: (i // nblk, 0, 0)),
                  pl.BlockSpec((None, 6, d), lambda i: (nrow - 1, 0, 0)),
                  pl.BlockSpec((1, d), lambda i: (0, 0)), pl.BlockSpec((1, d), lambda i: (0, 0)),
                  pl.BlockSpec(memory_space=pl.ANY)],
        out_specs=tok(d),
        out_shape=jax.ShapeDtypeStruct((bsz * s, d), F32),
        scratch_shapes=[pltpu.VMEM((2, 2 * tm, d), F32), pltpu.SemaphoreType.DMA((2,))],
        compiler_params=_cparams(("arbitrary",)),
        name="moe_combine",
    )(pos_steps, pos_steps, route.reshape(bsz * s, LANES), x1.reshape(bsz * s, d), mod, mod,
      vec(ln_g), vec(ln_b), ys)
    return out.reshape(bsz, s, d)


def _moe(fx, x1, route, mod, w1, w3, w2, ln_g, ln_b, n_ctx_tokens, alpha):
    bsz, s, d = x1.shape
    n_exp = w1.shape[0]
    idx = route.reshape(bsz * s, LANES)[:, 2:4].astype(jnp.int32)
    row_token, pos, tile_expert, n_used = _route_plan(idx[:, 0], idx[:, 1], n_exp, MOE_TILE)
    ys = _moe_gemm(fx.reshape(bsz * s, d), row_token, tile_expert, n_used, w1, w3, w2, MOE_TILE)
    return _moe_combine(ys, pos, route, x1, mod, ln_g, ln_b, n_ctx_tokens, alpha)


def _pick_chunk(total, target):
    best = LANES
    for c in range(LANES, target + 1, LANES):
        if total % c == 0:
            best = c
    return best


def kernel(x, c, ctx, c_ctx, w_mod, b_mod, w_in, s5_lam_re, s5_lam_im, s5_log_step, s5_b_re, s5_b_im,
           s5_c_re, s5_c_im, s5_d, s5_w_glu, s5_b_glu, attn_sink, ret_log_gamma, w_out,
           ln1_g, ln1_b, ln2_g, ln2_b, ffn_w1, ffn_w3, ffn_w2, moe_router, moe_w1, moe_w3, moe_w2):
    bsz, t_len, d = x.shape
    n_ctx = ctx.shape[1]
    depth = w_in.shape[0]
    alpha = (2 * depth) ** 0.25
    s5_w = s5_d.shape[1]
    att_w = attn_sink.shape[1] * HEAD_DIM
    kv_w = att_w // ATT_REP
    ret_w = ret_log_gamma.shape[2] * HEAD_DIM
    sizes = (s5_w, att_w, kv_w, kv_w, ret_w, ret_w, ret_w, ret_w)
    assert sum(sizes) == w_in.shape[2] and s5_w + att_w + ret_w == w_out.shape[1]
    assert n_ctx % TOKEN_BLOCK == 0 and t_len % TOKEN_BLOCK == 0 and t_len >= 3 * ATT_BLOCK

    pad = (-(bsz + 1)) % 8
    cvec = jnp.concatenate([jnp.zeros((pad, d), F32), c_ctx[None].astype(F32)], axis=0)
    cvec = jnp.concatenate([c.astype(F32), cvec], axis=0)
    mod_all = _modulation(cvec, w_mod.astype(F32), b_mod.astype(F32)).reshape(depth, bsz + pad + 1, 6, d)

    tabs = _rope_tables(t_len, n_ctx)
    xz = jnp.concatenate([ctx, x], axis=1).astype(F32)
    for l in range(depth):
        need_ctx = l < depth - 1
        mod = mod_all[l]
        u, qa, va, qr, vr, gr, ka, kr = _inproj(xz, mod, w_in[l].astype(BF16), tabs, sizes, n_ctx)
        s5w = _s5_weights(s5_lam_re[l], s5_lam_im[l], s5_log_step[l], s5_b_re[l], s5_b_im[l],
                          s5_c_re[l], s5_c_im[l], S5_CHUNK)
        y_s5 = _s5_scan(u, s5w, n_ctx)
        o_att = _attention(qa, ka, va, attn_sink[l], n_ctx, need_ctx)
        o_ret = _retention(qr, kr, vr, gr, ret_log_gamma[l], n_ctx)
        i = l // 2
        router = None if l % 2 == 0 else moe_router[i]
        outs = _outproj(y_s5, u, o_att, o_ret, xz, mod, s5_d[l], s5_w_glu[l], s5_b_glu[l], w_out[l],
                        ln1_g[l], ln1_b[l], router, n_ctx, need_ctx, alpha)
        ctx_tokens = n_ctx if need_ctx else 0
        if l % 2 == 0:
            x1, fx = outs
            xz = _ffn(fx, x1, mod, ffn_w1[i].astype(BF16), ffn_w3[i].astype(BF16), ffn_w2[i].astype(BF16),
                      ln2_g[l], ln2_b[l], ctx_tokens, alpha)
        else:
            x1, fx, route = outs
            xz = _moe(fx, x1, route, mod, moe_w1[i].astype(BF16), moe_w3[i].astype(BF16),
                      moe_w2[i].astype(BF16), ln2_g[l], ln2_b[l], ctx_tokens, alpha)
    return xz if xz.shape[1] == t_len else xz[:, n_ctx:]
```

```python
import functools
import math

import jax
import jax.numpy as jnp
import numpy as np
from jax import lax
from jax.experimental import pallas as pl
from jax.experimental.pallas import tpu as pltpu

F32 = jnp.float32
BF16 = jnp.bfloat16

GRID_W = 64
HEAD_DIM = 64
S5_GROUP = 16
ATT_REP = 4
WINDOW = 128
ATT_BLOCK = 128
RET_CHUNK = 128
TOP_K = 2
LN_EPS = 1e-5
ROPE_BASE = 10000.0
NEG_INF = -1e30

LANES = 128
S5_CHUNK = LANES
TOKEN_BLOCK = 256
INPROJ_BLOCK = 768
FFN_BLOCK = 768
VMEM_LIMIT = 56 * 1024 * 1024


def _cparams(sem):
    return pltpu.CompilerParams(dimension_semantics=sem, vmem_limit_bytes=VMEM_LIMIT)


def _dot(a, b):
    return jnp.dot(a, b, preferred_element_type=F32)


def _dot_nt(a, b):
    return lax.dot_general(a, b, (((1,), (1,)), ((), ())), preferred_element_type=F32)


def _dot_tn(a, b):
    return lax.dot_general(a, b, (((0,), (0,)), ((), ())), preferred_element_type=F32)


def _split_bf16(a):
    hi = a.astype(BF16)
    lo = (a - hi.astype(F32)).astype(BF16)
    return hi, lo


def _dot3(a, b):
    ah, al = _split_bf16(a)
    bh, bl = _split_bf16(b)
    return _dot(ah, bh) + (_dot(ah, bl) + _dot(al, bh))


def _ln(x):
    mu = jnp.mean(x, axis=-1, keepdims=True)
    xc = x - mu
    var = jnp.mean(xc * xc, axis=-1, keepdims=True)
    return xc * lax.rsqrt(var + LN_EPS)


def _silu(x):
    return x * (1.0 / (1.0 + jnp.exp(-x)))


def _sigmoid(x):
    return 1.0 / (1.0 + jnp.exp(-x))


def _gelu_tanh(x):
    c = math.sqrt(2.0 / math.pi)
    return 0.5 * x * (1.0 + jnp.tanh(c * (x + 0.044715 * (x * x * x))))


def _mod_kernel(c_ref, w_ref, b_ref, o_ref):
    o_ref[...] = _dot3(_silu(c_ref[...]), w_ref[...]) + b_ref[...]


def _modulation(cvec, w_mod, b_mod):
    depth, d, n = w_mod.shape
    rows = cvec.shape[0]
    tn = 1536
    return pl.pallas_call(
        _mod_kernel,
        grid=(depth, n // tn),
        in_specs=[pl.BlockSpec((rows, d), lambda l, j: (0, 0)),
                  pl.BlockSpec((None, d, tn), lambda l, j: (l, 0, j)),
                  pl.BlockSpec((None, 1, tn), lambda l, j: (l, 0, j))],
        out_specs=pl.BlockSpec((None, rows, tn), lambda l, j: (l, 0, j)),
        out_shape=jax.ShapeDtypeStruct((depth, rows, n), F32),
        compiler_params=_cparams(("parallel", "parallel")),
        name="modulation",
    )(cvec, w_mod, b_mod.reshape(depth, 1, n))


def _rope_slab(xs, cos, sa, sb, half):
    return xs * cos + pltpu.roll(xs, LANES - half, 1) * sa + pltpu.roll(xs, half, 1) * sb


def _inproj_kernel(x_ref, modx_ref, modz_ref, wc_ref, w_ref, act_ref, ast_ref, rct_ref, rst_ref,
                   ac_ref, asa_ref, asb_ref, rc_ref, rsa_ref, rsb_ref,
                   ut_ref, qat_ref, vat_ref, qrt_ref, vrt_ref, grt_ref, ka_ref, kr_ref, *, n_ctx, c_sizes, sizes):
    tm = x_ref.shape[0]
    is_ctx = pl.program_id(1) * tm + lax.broadcasted_iota(jnp.int32, (tm, 1), 0) < n_ctx
    shift = jnp.where(is_ctx, modz_ref[0:1, :], modx_ref[0:1, :])
    gain = 1.0 + jnp.where(is_ctx, modz_ref[1:2, :], modx_ref[1:2, :])
    h = (_ln(x_ref[...]) * gain + shift).astype(BF16)
    scale = HEAD_DIM ** -0.5

    ct = _dot_nt(wc_ref[...], h)
    c_offs = [int(o) for o in np.concatenate([[0], np.cumsum(c_sizes)])]

    def chan(i):
        return ct[c_offs[i]:c_offs[i + 1]]

    def rope_t(x, cos_ref, sin_ref, part, mul):
        cos_t, sin_t = cos_ref[...], sin_ref[...]
        heads = []
        for hd in range(x.shape[0] // HEAD_DIM):
            xh = x[hd * HEAD_DIM:(hd + 1) * HEAD_DIM]
            pieces = []
            for lo in range(0, HEAD_DIM, 2 * part):
                pieces += [xh[lo + part:lo + 2 * part], xh[lo:lo + part]]
            out = xh * cos_t + jnp.concatenate(pieces, axis=0) * sin_t
            heads.append(out * mul if mul != 1.0 else out)
        return jnp.concatenate(heads, axis=0)

    def put(ref, val):
        for k in range(ref.shape[0]):
            ref[k] = val[:, k * ATT_BLOCK:(k + 1) * ATT_BLOCK].astype(ref.dtype)

    ut_ref[...] = chan(0)
    put(qat_ref, rope_t(chan(1), act_ref, ast_ref, HEAD_DIM // 4, scale))
    put(vat_ref, chan(2))
    put(qrt_ref, rope_t(chan(3), rct_ref, rst_ref, HEAD_DIM // 2, 1.0))
    put(vrt_ref, chan(4))
    put(grt_ref, chan(5))

    p = _dot(h, w_ref[...])
    ac, asa, asb = ac_ref[...], asa_ref[...], asb_ref[...]
    rc, rsa, rsb = rc_ref[...], rsa_ref[...], rsb_ref[...]
    offs = np.concatenate([[0], np.cumsum(sizes)])

    def rope_cols(i, cos, sa, sb, half, mul):
        blk = p[:, int(offs[i]):int(offs[i + 1])]
        slabs = [_rope_slab(blk[:, s:s + LANES], cos, sa, sb, half) for s in range(0, blk.shape[1], LANES)]
        out = slabs[0] if len(slabs) == 1 else jnp.concatenate(slabs, axis=1)
        return out * mul if mul != 1.0 else out

    ka_ref[...] = rope_cols(0, ac, asa, asb, HEAD_DIM // 4, 1.0).astype(ka_ref.dtype)
    kr_ref[...] = rope_cols(1, rc, rsa, rsb, HEAD_DIM // 2, scale)


def _inproj(xz, mod, w_in, tabs, sizes, n_ctx):
    bsz, s, d = xz.shape
    tm = _pick_chunk(s, INPROJ_BLOCK)
    assert tm % TOKEN_BLOCK == 0, "the output projection reads the S5 arrays in TOKEN_BLOCK pieces"
    nrow = mod.shape[0]
    offs = np.concatenate([[0], np.cumsum(sizes)])
    col = lambda i: w_in[:, int(offs[i]):int(offs[i + 1])]
    c_idx, r_idx = (0, 1, 3, 4, 6, 7), (2, 5)
    c_dtypes = (BF16, BF16, F32, BF16, F32)
    r_dtypes = (BF16, F32)
    c_sizes = tuple(sizes[i] for i in c_idx)
    r_sizes = tuple(sizes[i] for i in r_idx)
    w_c_t = jnp.concatenate([col(i) for i in c_idx], axis=1).T
    w_rest = jnp.concatenate([col(i) for i in r_idx], axis=1)
    sub = tm // ATT_BLOCK
    tok = lambda width: pl.BlockSpec((None, tm, width), lambda b, j: (b, j, 0))
    chan = lambda width: pl.BlockSpec((sub, None, width, ATT_BLOCK), lambda b, j: (j, b, 0, 0))
    tab = pl.BlockSpec((tm, LANES), lambda b, j: (j, 0))
    tab_t = pl.BlockSpec((HEAD_DIM, tm), lambda b, j: (0, j))
    return pl.pallas_call(
        functools.partial(_inproj_kernel, n_ctx=n_ctx, c_sizes=c_sizes, sizes=r_sizes),
        grid=(bsz, s // tm),
        in_specs=[tok(d),
                  pl.BlockSpec((None, 6, d), lambda b, j: (b, 0, 0)),
                  pl.BlockSpec((None, 6, d), lambda b, j: (nrow - 1, 0, 0)),
                  pl.BlockSpec((sum(c_sizes), d), lambda b, j: (0, 0)),
                  pl.BlockSpec((d, sum(r_sizes)), lambda b, j: (0, 0)),
                  tab_t, tab_t, tab_t, tab_t, tab, tab, tab, tab, tab, tab],
        out_specs=[pl.BlockSpec((None, None, c_sizes[0], tm), lambda b, j: (j, b, 0, 0))]
                  + [chan(w) for w in c_sizes[1:]] + [tok(w) for w in r_sizes],
        out_shape=[jax.ShapeDtypeStruct((s // tm, bsz, c_sizes[0], tm), F32)]
                  + [jax.ShapeDtypeStruct((s // ATT_BLOCK, bsz, w, ATT_BLOCK), dt)
                     for w, dt in zip(c_sizes[1:], c_dtypes)]
                  + [jax.ShapeDtypeStruct((bsz, s, w), dt) for w, dt in zip(r_sizes, r_dtypes)],
        compiler_params=_cparams(("parallel", "parallel")),
        name="inproj",
    )(xz, mod, mod, w_c_t, w_rest, *tabs)


def _rope_tables(t_len, n_ctx):
    t = jnp.arange(t_len)
    rows = (t // GRID_W).astype(F32)
    cols = (t % GRID_W).astype(F32)
    pos = t.astype(F32)

    def angles(p, dim):
        inv_freq = ROPE_BASE ** (-jnp.arange(0, dim, 2, dtype=F32) / dim)
        return p[:, None] * inv_freq[None, :]

    def head_tables(angs):
        cos = jnp.concatenate([jnp.concatenate([jnp.cos(a), jnp.cos(a)], -1) for a in angs], -1)
        sa = jnp.concatenate([jnp.concatenate([-jnp.sin(a), jnp.zeros_like(a)], -1) for a in angs], -1)
        sb = jnp.concatenate([jnp.concatenate([jnp.zeros_like(a), jnp.sin(a)], -1) for a in angs], -1)
        rep = LANES // HEAD_DIM
        return tuple(jnp.tile(x, (1, rep)) for x in (cos, sa, sb))

    att = head_tables([angles(rows, HEAD_DIM // 2), angles(cols, HEAD_DIM // 2)])
    ret = head_tables([angles(pos, HEAD_DIM)])
    signed_t = lambda tb: (tb[0][:, :HEAD_DIM].T, (tb[1] + tb[2])[:, :HEAD_DIM].T)

    def with_ctx(tb, axis, is_cos):
        shape = list(tb.shape)
        shape[axis] = n_ctx
        lead = jnp.ones(shape, F32) if is_cos else jnp.zeros(shape, F32)
        return jnp.concatenate([lead, tb], axis=axis)

    chan = signed_t(att) + signed_t(ret)
    toks = att + ret
    return (tuple(with_ctx(tb, 1, k % 2 == 0) for k, tb in enumerate(chan))
            + tuple(with_ctx(tb, 0, k % 3 == 0) for k, tb in enumerate(toks)))


def _s5_kernel(u_ref, tab_ref, wsf_ref, wsb_ref, wrf_ref, wrb_ref, lam_ref, y_ref,
               lhs_ref, m_ref, acc_ref, cf_ref, cb_ref, pf_ref, pb_ref, *, bsz, nz_chunks):
    r_blk, ch, tm = u_ref.shape
    c = S5_CHUNK
    kpb = tm // c
    n_chunks = (r_blk // bsz) * kpb

    for i in range(ch):
        x = u_ref[:, i, :].astype(BF16)
        p, half = divmod(i, 2)
        for k in range(kpb):
            lhs_ref[p, k * r_blk:(k + 1) * r_blk, half * c:(half + 1) * c] = x[:, k * c:(k + 1) * c]
    acc_ref[...] = jnp.zeros_like(acc_ref)
    cf_ref[...] = jnp.zeros_like(cf_ref)
    cb_ref[...] = jnp.zeros_like(cb_ref)

    def pair(p, carry):
        for ih in range(2):
            for o in range(ch):
                lag_row = jnp.broadcast_to(tab_ref[2 * p + ih, o:o + 1, :], (8, 2 * c))
                tab = pltpu.roll(lag_row, 0, 1, stride=1, stride_axis=0)
                for r2 in range(c // 16):
                    lo = c - 16 * r2
                    rows = jnp.concatenate([tab[:, lo:lo + c], tab[:, lo - 8:lo - 8 + c]], axis=0)
                    m_ref[ih * c + 16 * r2:ih * c + 16 * r2 + 16, o * c:(o + 1) * c] = rows.astype(BF16)
        lhs = lhs_ref[p]
        acc_ref[...] += _dot(lhs, m_ref[...])
        cf_ref[...] += _dot(lhs, wsf_ref[p])
        cb_ref[...] += _dot(lhs, wsb_ref[p])
        return carry

    lax.fori_loop(0, ch // 2, pair, 0)
    lam = lam_ref[...]
    nst = lam.shape[1] // 2

    def advance(s, a, bc, drive):
        return s * a + pltpu.roll(s, nst, 1) * bc + drive

    def sweep(order, c_ref, p_ref, a, bc):
        s = jnp.zeros((bsz, lam.shape[1]), F32)
        for n in order:
            j, k = divmod(n, kpb)
            rows = slice(k * r_blk + j * bsz, k * r_blk + (j + 1) * bsz)
            p_ref[rows, :] = s
            s = advance(s, a, bc, c_ref[rows, :])

    sweep(list(range(n_chunks)), cf_ref, pf_ref, lam[0:1], lam[1:2])
    order_b = list(range(nz_chunks - 1, -1, -1)) + list(range(n_chunks - 1, nz_chunks - 1, -1))
    sweep(order_b, cb_ref, pb_ref, lam[2:3], lam[3:4])
    acc_ref[...] += (_dot(pf_ref[...].astype(BF16), wrf_ref[...])
                     + _dot(pb_ref[...].astype(BF16), wrb_ref[...]))
    for o in range(ch):
        for k in range(kpb):
            y_ref[:, o, k * c:(k + 1) * c] = acc_ref[k * r_blk:(k + 1) * r_blk, o * c:(o + 1) * c]


def _s5_weights(lam_re, lam_im, log_step, b_re, b_im, c_re, c_im, chunk):
    hp = lax.Precision.HIGHEST
    lam = lax.complex(lam_re.astype(F32), lam_im.astype(F32))
    lam_dt = lam * jnp.exp(log_step.astype(F32))[..., None]
    lam_bar = jnp.exp(lam_dt)
    b_bar = lax.complex(b_re.astype(F32), b_im.astype(F32)) * ((lam_bar - 1.0) / lam)[..., None]
    c_mat = lax.complex(c_re.astype(F32), c_im.astype(F32))
    g, n = lam.shape[1], lam.shape[2]
    ch = b_bar.shape[-1]
    steps = jnp.arange(chunk + 1, dtype=F32)
    pw = jnp.exp(steps[None, :, None, None] * lam_dt[:, None])
    kern = jnp.einsum('zgon,zdgn,zgni->zgdoi', c_mat, pw[:, :chunk], b_bar, precision=hp).real
    zero_lag = kern[0, :, 0] + kern[1, :, 0]
    lag_table = jnp.concatenate([jnp.zeros_like(zero_lag)[:, None], kern[1, :, :0:-1], zero_lag[:, None],
                                 kern[0, :, 1:]], axis=1).transpose(0, 3, 2, 1)

    def state_in(pw_s, b_dir):
        w = pw_s[:, :, :, None] * b_dir[None]
        w = jnp.concatenate([w.real, w.imag], axis=2)
        return w.transpose(1, 3, 0, 2).reshape(g, ch // 2, 2 * chunk, 2 * n)

    def state_out(pw_t, c_dir):
        w = c_dir[None] * pw_t[:, :, None, :]
        w = jnp.concatenate([w.real, -w.imag], axis=3)
        return w.transpose(1, 3, 2, 0).reshape(g, 2 * n, ch * chunk)

    wsf = state_in(pw[0, chunk - 1::-1][:chunk], b_bar[0])
    wsb = state_in(pw[1, :chunk], b_bar[1])
    wrf = state_out(pw[0, 1:chunk + 1], c_mat[0])
    wrb = state_out(pw[1, chunk:0:-1], c_mat[1])
    lam_c = pw[:, chunk]
    rows = []
    for z in range(2):
        rows.append(jnp.concatenate([lam_c[z].real, lam_c[z].real], -1))
        rows.append(jnp.concatenate([-lam_c[z].imag, lam_c[z].imag], -1))
    lam_rows = jnp.stack(rows + [jnp.zeros_like(rows[0])] * 4, axis=1)
    return lag_table, wsf.astype(BF16), wsb.astype(BF16), wrf.astype(BF16), wrb.astype(BF16), lam_rows


def _s5_scan(ut, weights, n_ctx):
    nblk, bsz, width, tm = ut.shape
    tab, wsf, wsb, wrf, wrb, lam_rows = weights
    g, ch = tab.shape[0], tab.shape[1]
    c = S5_CHUNK
    nst2 = wsf.shape[3]
    r_blk = nblk * bsz
    rows = r_blk * (tm // c)
    grp = lambda *shape: pl.BlockSpec((None,) + shape, lambda i: (i,) + (0,) * len(shape))
    tok = pl.BlockSpec((r_blk, ch, tm), lambda i: (0, i, 0))
    y = pl.pallas_call(
        functools.partial(_s5_kernel, bsz=bsz, nz_chunks=n_ctx // c),
        grid=(g,),
        in_specs=[tok, grp(ch, ch, 2 * c), grp(ch // 2, 2 * c, nst2), grp(ch // 2, 2 * c, nst2),
                  grp(nst2, ch * c), grp(nst2, ch * c), grp(8, nst2)],
        out_specs=tok,
        out_shape=jax.ShapeDtypeStruct((r_blk, width, tm), F32),
        scratch_shapes=[pltpu.VMEM((ch // 2, rows, 2 * c), BF16),
                        pltpu.VMEM((2 * c, ch * c), BF16),
                        pltpu.VMEM((rows, ch * c), F32)] + [pltpu.VMEM((rows, nst2), F32)] * 4,
        compiler_params=_cparams(("parallel",)),
        name="s5_scan",
    )(ut.reshape(r_blk, width, tm), tab, wsf, wsb, wrf, wrb, lam_rows)
    return y.reshape(nblk, bsz, width, tm)


def _attn_kernel(sink_ref, q_ref, k_ref, v_ref, o_ref, *, n_ctx, t_len, q_off, kv_heads):
    qi = pl.program_id(1) + q_off
    nzb = n_ctx // ATT_BLOCK
    band = 3 * ATT_BLOCK
    wide = ATT_REP * ATT_BLOCK
    q = q_ref[...].astype(BF16)
    kc = k_ref[0:n_ctx, :].astype(BF16)
    vc = jnp.concatenate([v_ref[p] for p in range(nzb)], axis=1).astype(BF16)

    def rows(a, i):
        return a[i * HEAD_DIM:(i + 1) * HEAD_DIM]

    def q_group(g):
        qg = jnp.concatenate([rows(q, g * ATT_REP + r) for r in range(ATT_REP)], axis=1)
        zero = jnp.zeros_like(qg)
        return jnp.concatenate([qg if gg == g else zero for gg in range(kv_heads)], axis=0)

    def finish(g, o, den):
        o = o / den
        for r in range(ATT_REP):
            h = g * ATT_REP + r
            o_ref[h * HEAD_DIM:(h + 1) * HEAD_DIM, :] = o[:, r * ATT_BLOCK:(r + 1) * ATT_BLOCK].astype(o_ref.dtype)

    @pl.when(qi < nzb)
    def _():
        for g in range(kv_heads):
            sink = sink_ref[g:g + 1, :]
            s = _dot(kc, q_group(g))
            m = jnp.maximum(jnp.max(s, axis=0, keepdims=True), sink)
            e = jnp.exp(s - m)
            den = jnp.sum(e, axis=0, keepdims=True) + jnp.exp(sink - m)
            finish(g, _dot(rows(vc, g), e.astype(BF16)), den)

    @pl.when(qi >= nzb)
    def _():
        n = qi - nzb
        start = jnp.clip((n - 1) * ATT_BLOCK, 0, t_len - band)
        kl = k_ref[pl.ds(pl.multiple_of(n_ctx + start, ATT_BLOCK), band), :].astype(BF16)
        p0 = (n_ctx + start) // ATT_BLOCK
        vl = jnp.concatenate([v_ref[p0 + t] for t in range(3)], axis=1).astype(BF16)
        k_pos = start + lax.broadcasted_iota(jnp.int32, (band, wide), 0)
        q_pos = n * ATT_BLOCK + (lax.broadcasted_iota(jnp.int32, (band, wide), 1) & (ATT_BLOCK - 1))
        valid = jnp.abs(k_pos - q_pos) <= WINDOW
        for g in range(kv_heads):
            sink = sink_ref[g:g + 1, :]
            qg = q_group(g)
            s_loc = jnp.where(valid, _dot(kl, qg), NEG_INF)
            s_ctx = _dot(kc, qg)
            m = jnp.maximum(jnp.maximum(jnp.max(s_loc, axis=0, keepdims=True),
                                        jnp.max(s_ctx, axis=0, keepdims=True)), sink)
            e_loc = jnp.exp(s_loc - m)
            e_ctx = jnp.exp(s_ctx - m)
            den = (jnp.sum(e_loc, axis=0, keepdims=True) + jnp.sum(e_ctx, axis=0, keepdims=True)
                   + jnp.exp(sink - m))
            o = _dot(rows(vl, g), e_loc.astype(BF16)) + _dot(rows(vc, g), e_ctx.astype(BF16))
            finish(g, o, den)


def _attention(qat, ka, vat, sink, n_ctx, need_ctx):
    nblk, bsz, qw, _ = qat.shape
    s, kvw = ka.shape[1], ka.shape[2]
    t_len = s - n_ctx
    kv_heads = kvw // HEAD_DIM
    q_off = 0 if need_ctx else n_ctx // ATT_BLOCK
    nq = nblk - q_off
    sink_rows = jnp.repeat(sink.astype(F32).reshape(kv_heads, ATT_REP), ATT_BLOCK, axis=1)
    return pl.pallas_call(
        functools.partial(_attn_kernel, n_ctx=n_ctx, t_len=t_len, q_off=q_off, kv_heads=kv_heads),
        grid=(bsz, nq),
        in_specs=[pl.BlockSpec(sink_rows.shape, lambda b, j: (0, 0)),
                  pl.BlockSpec((None, None, qw, ATT_BLOCK), lambda b, j: (j + q_off, b, 0, 0)),
                  pl.BlockSpec((None, s, kvw), lambda b, j: (b, 0, 0)),
                  pl.BlockSpec((nblk, None, kvw, ATT_BLOCK), lambda b, j: (0, b, 0, 0))],
        out_specs=pl.BlockSpec((None, None, qw, ATT_BLOCK), lambda b, j: (j, b, 0, 0)),
        out_shape=jax.ShapeDtypeStruct((nq, bsz, qw, ATT_BLOCK), BF16),
        compiler_params=_cparams(("parallel", "arbitrary")),
        name="window_attention",
    )(sink_rows, qat, ka, vat)


def _ret_kernel(lg_ref, q_ref, k_ref, v_ref, g_ref, o_ref, acc_ref, sf_ref, sb_ref, *, nz_chunks, heads):
    c = RET_CHUNK
    n_chunks, w, _ = q_ref.shape
    f32 = lambda a: a.astype(F32)

    def per_head(shape, axis, group, direction):
        owner = lax.broadcasted_iota(jnp.int32, shape, axis) // group
        out = jnp.zeros(shape, F32)
        for h in range(heads):
            out = jnp.where(owner == h, lg_ref[direction, h], out)
        return out

    row_i = f32(lax.broadcasted_iota(jnp.int32, (w, c), 1))
    qw_f = jnp.exp(per_head((w, c), 0, HEAD_DIM, 0) * (row_i + 1.0))
    qw_b = jnp.exp(per_head((w, c), 0, HEAD_DIM, 1) * (c - row_i))
    key_j = f32(lax.broadcasted_iota(jnp.int32, (c, w), 0))
    kw_f = jnp.exp(per_head((c, w), 1, HEAD_DIM, 0) * (c - 1.0 - key_j))
    kw_b = jnp.exp(per_head((c, w), 1, HEAD_DIM, 1) * key_j)
    dec_f = jnp.exp(per_head((w, 1), 0, HEAD_DIM, 0) * c)
    dec_b = jnp.exp(per_head((w, 1), 0, HEAD_DIM, 1) * c)
    wide = heads * c
    diff = f32((lax.broadcasted_iota(jnp.int32, (c, wide), 1) & (c - 1)) - lax.broadcasted_iota(jnp.int32, (c, wide), 0))
    dec_t = jnp.where(diff >= 0, jnp.exp(per_head((c, wide), 1, c, 0) * jnp.maximum(diff, 0.0)),
                      jnp.exp(per_head((c, wide), 1, c, 1) * jnp.maximum(-diff, 0.0)))
    own_wide = (lax.broadcasted_iota(jnp.int32, (w, wide), 0) // HEAD_DIM
                == lax.broadcasted_iota(jnp.int32, (w, wide), 1) // c)
    own_sq = f32(lax.broadcasted_iota(jnp.int32, (w, w), 0) // HEAD_DIM
                 == lax.broadcasted_iota(jnp.int32, (w, w), 1) // HEAD_DIM)
    sf_ref[...] = jnp.zeros_like(sf_ref)
    sb_ref[...] = jnp.zeros_like(sb_ref)

    def block_diag(x):
        return jnp.where(own_wide, jnp.concatenate([x] * heads, axis=1), 0.0).astype(BF16)

    def fwd(n, carry):
        qt, vt = q_ref[n], v_ref[n]
        k = k_ref[pl.ds(pl.multiple_of(n * c, c), c), :]
        scores_t = _dot(k.astype(BF16), block_diag(qt)) * dec_t
        stacked = jnp.concatenate([scores_t[:, h * c:(h + 1) * c] for h in range(heads)], axis=0)
        o = _dot(block_diag(vt), stacked.astype(BF16))
        s_prev = sf_ref[...]
        o = o + _dot(s_prev.astype(BF16), (qt * qw_f).astype(BF16))
        sf_ref[...] = dec_f * s_prev + own_sq * _dot(vt.astype(BF16), (k * kw_f).astype(BF16))
        acc_ref[n] = o
        return carry

    lax.fori_loop(0, n_chunks, fwd, 0)

    def bwd(i, carry):
        n = jnp.where(i < nz_chunks, nz_chunks - 1 - i, n_chunks - 1 - i + nz_chunks)
        qt, vt = q_ref[n], v_ref[n]
        k = k_ref[pl.ds(pl.multiple_of(n * c, c), c), :]
        s_prev = sb_ref[...]
        o = acc_ref[n] + _dot(s_prev.astype(BF16), (qt * qw_b).astype(BF16))
        sb_ref[...] = dec_b * s_prev + own_sq * _dot(vt.astype(BF16), (k * kw_b).astype(BF16))
        gate = _silu(g_ref[n])
        for h in range(heads):
            rows = slice(h * HEAD_DIM, (h + 1) * HEAD_DIM)
            oh = o[rows]
            mu = jnp.mean(oh, axis=0, keepdims=True)
            oc = oh - mu
            var = jnp.mean(oc * oc, axis=0, keepdims=True)
            o_ref[n, rows, :] = (oc * lax.rsqrt(var + LN_EPS) * gate[rows]).astype(o_ref.dtype)
        return carry

    lax.fori_loop(0, n_chunks, bwd, 0)


def _retention(qrt, kr, vrt, grt, log_gamma, n_ctx):
    n_chunks, bsz, w, c = qrt.shape
    s = kr.shape[1]
    chan = pl.BlockSpec((n_chunks, None, w, c), lambda b: (0, b, 0, 0))
    return pl.pallas_call(
        functools.partial(_ret_kernel, nz_chunks=n_ctx // RET_CHUNK, heads=w // HEAD_DIM),
        grid=(bsz,),
        in_specs=[pl.BlockSpec(memory_space=pltpu.SMEM), chan, pl.BlockSpec((None, s, w), lambda b: (b, 0, 0)),
                  chan, chan],
        out_specs=chan,
        out_shape=jax.ShapeDtypeStruct((n_chunks, bsz, w, c), BF16),
        scratch_shapes=[pltpu.VMEM((n_chunks, w, c), F32), pltpu.VMEM((w, w), F32), pltpu.VMEM((w, w), F32)],
        compiler_params=_cparams(("parallel",)),
        name="retention",
    )(log_gamma.astype(F32), qrt, kr, vrt, grt)


def _outproj_kernel(*refs, nzb, alpha, with_router, w_s5, w_att, n_exp):
    if with_router:
        (y_ref, u_ref, a_ref, r_ref, x_ref, modx_ref, modz_ref, d_ref, wg_ref, bg_ref, wo_ref,
         g1_ref, b1_ref, rt_ref, x1_ref, fx_ref, gate_ref) = refs
    else:
        (y_ref, u_ref, a_ref, r_ref, x_ref, modx_ref, modz_ref, d_ref, wg_ref, bg_ref, wo_ref,
         g1_ref, b1_ref, x1_ref, fx_ref) = refs
    is_ctx = pl.program_id(1) < nzb
    mod = jnp.where(is_ctx, modz_ref[...], modx_ref[...])
    g = _gelu_tanh(y_ref[...] + d_ref[...] * u_ref[...])
    s5 = g * _sigmoid(_dot(wg_ref[...], g.astype(BF16)) + bg_ref[...])
    def chan_major(ref, lo, hi):
        return jnp.concatenate([_dot_tn(ref[k].astype(BF16), wo_ref[lo:hi, :]) for k in range(ref.shape[0])],
                               axis=0)

    mix = (_dot_tn(s5.astype(BF16), wo_ref[0:w_s5, :]) + chan_major(a_ref, w_s5, w_s5 + w_att)
           + chan_major(r_ref, w_s5 + w_att, wo_ref.shape[0]))
    x1 = _ln(alpha * x_ref[...] + mod[2:3] * mix) * g1_ref[...] + b1_ref[...]
    x1_ref[...] = x1
    fx = _ln(x1) * (1.0 + mod[4:5]) + mod[3:4]
    fx_ref[...] = fx.astype(fx_ref.dtype)
    if with_router:
        lane = lax.broadcasted_iota(jnp.int32, (fx.shape[0], LANES), 1)
        logits = jnp.where(lane < n_exp, _dot3(fx, rt_ref[...]), -jnp.inf)
        m1 = jnp.max(logits, axis=1, keepdims=True)
        i1 = jnp.min(jnp.where(logits == m1, lane, LANES), axis=1, keepdims=True)
        rest = jnp.where(lane == i1, -jnp.inf, logits)
        m2 = jnp.max(rest, axis=1, keepdims=True)
        i2 = jnp.min(jnp.where(rest == m2, lane, LANES), axis=1, keepdims=True)
        e2 = jnp.exp(m2 - m1)
        den = 1.0 + e2
        route = jnp.where(lane == 0, 1.0 / den, jnp.where(lane == 1, e2 / den, 0.0))
        route = jnp.where(lane == 2, i1.astype(F32), jnp.where(lane == 3, i2.astype(F32), route))
        gate_ref[...] = route


def _outproj(y_s5, u, o_att, o_ret, xz, mod, s5_d, w_glu, b_glu, w_out, ln_g, ln_b, router, n_ctx, need_ctx,
             alpha):
    bsz, s, d = xz.shape
    tm = TOKEN_BLOCK
    nzb = n_ctx // tm
    off = 0 if need_ctx else nzb
    nblk = s // tm - off
    s_out = nblk * tm
    nrow = mod.shape[0]
    w_s5, w_att = y_s5.shape[2], o_att.shape[2]
    sub = tm // ATT_BLOCK
    att_off = off if o_att.shape[0] == s // ATT_BLOCK else 0
    att = pl.BlockSpec((sub, None, w_att, ATT_BLOCK), lambda b, j: (j + att_off, b, 0, 0))

    def tok(width, shift):
        return pl.BlockSpec((None, tm, width), lambda b, j: (b, j + shift, 0))

    def full(a):
        return pl.BlockSpec(a.shape, lambda b, j: (0,) * a.ndim)

    per_in = y_s5.shape[3] // tm
    chan = pl.BlockSpec((None, None, w_s5, tm), lambda b, j: ((j + off) // per_in, b, 0, (j + off) % per_in))
    vec = lambda a: a.reshape(1, -1).astype(F32)
    col = lambda a: a.reshape(-1, 1).astype(F32)
    consts = [col(s5_d), w_glu.T.astype(BF16), col(b_glu), w_out.astype(BF16), vec(ln_g), vec(ln_b)]
    ret = pl.BlockSpec((sub, None, o_ret.shape[2], ATT_BLOCK), lambda b, j: (j + off, b, 0, 0))
    in_specs = [chan, chan, att, ret, tok(d, off),
                pl.BlockSpec((None, 6, d), lambda b, j: (b, 0, 0)),
                pl.BlockSpec((None, 6, d), lambda b, j: (nrow - 1, 0, 0))] + [full(a) for a in consts]
    out_specs = [tok(d, 0), tok(d, 0)]
    fx_dtype = BF16 if router is None else F32
    out_shape = [jax.ShapeDtypeStruct((bsz, s_out, d), F32), jax.ShapeDtypeStruct((bsz, s_out, d), fx_dtype)]
    args = [y_s5, u, o_att, o_ret, xz, mod, mod] + consts
    with_router = router is not None
    n_exp = 0
    if with_router:
        n_exp = router.shape[1]
        router_pad = jnp.pad(router.astype(F32), ((0, 0), (0, LANES - n_exp)))
        args.append(router_pad)
        in_specs.append(full(router_pad))
        out_specs.append(tok(LANES, 0))
        out_shape.append(jax.ShapeDtypeStruct((bsz, s_out, LANES), F32))
    return pl.pallas_call(
        functools.partial(_outproj_kernel, nzb=nzb - off, alpha=alpha, with_router=with_router,
                          w_s5=w_s5, w_att=w_att, n_exp=n_exp),
        grid=(bsz, nblk),
        in_specs=in_specs,
        out_specs=out_specs,
        out_shape=out_shape,
        compiler_params=_cparams(("parallel", "parallel")),
        name="outproj",
    )(*args)


def _swiglu_into(x_bf16, w1_ref, w3_ref, w2_ref, acc_ref, fc):
    for s in range(0, w1_ref.shape[1], fc):
        h1 = _dot(x_bf16, w1_ref[:, s:s + fc])
        h3 = _dot(x_bf16, w3_ref[:, s:s + fc])
        acc_ref[...] += _dot((_silu(h1) * h3).astype(BF16), w2_ref[s:s + fc, :])


def _ffn_kernel(fx_ref, x1_ref, modx_ref, modz_ref, w1_ref, w3_ref, w2_ref, g2_ref, b2_ref,
                o_ref, acc_ref, *, n_ctx_tokens, alpha, fc):
    tm = acc_ref.shape[0]
    acc_ref[...] = jnp.zeros_like(acc_ref)
    _swiglu_into(fx_ref[...], w1_ref, w3_ref, w2_ref, acc_ref, fc)
    row = pl.program_id(1) * tm + lax.broadcasted_iota(jnp.int32, (tm, 1), 0)
    gate = jnp.where(row < n_ctx_tokens, modz_ref[5:6, :], modx_ref[5:6, :])
    o_ref[...] = _ln(alpha * x1_ref[...] + gate * acc_ref[...]) * g2_ref[...] + b2_ref[...]


def _ffn(fx, x1, mod, w1, w3, w2, ln_g, ln_b, n_ctx_tokens, alpha):
    bsz, s, d = x1.shape
    ff = w1.shape[1]
    tm = _pick_chunk(s, FFN_BLOCK)
    nrow = mod.shape[0]
    tok = pl.BlockSpec((None, tm, d), lambda b, j: (b, j, 0))
    full = lambda a: pl.BlockSpec(a.shape, lambda b, j: (0,) * a.ndim, pipeline_mode=pl.Buffered(1))
    vec = lambda a: a.reshape(1, -1).astype(F32)
    return pl.pallas_call(
        functools.partial(_ffn_kernel, n_ctx_tokens=n_ctx_tokens, alpha=alpha, fc=_pick_chunk(ff, 512)),
        grid=(bsz, s // tm),
        in_specs=[tok, tok,
                  pl.BlockSpec((None, 6, d), lambda b, j: (b, 0, 0)),
                  pl.BlockSpec((None, 6, d), lambda b, j: (nrow - 1, 0, 0)),
                  full(w1), full(w3), full(w2),
                  pl.BlockSpec((1, d), lambda b, j: (0, 0)), pl.BlockSpec((1, d), lambda b, j: (0, 0))],
        out_specs=tok,
        out_shape=jax.ShapeDtypeStruct((bsz, s, d), F32),
        scratch_shapes=[pltpu.VMEM((tm, d), F32)],
        compiler_params=_cparams(("parallel", "parallel")),
        name="dense_ffn",
    )(fx, x1, mod, mod, w1, w3, w2, vec(ln_g), vec(ln_b))


MOE_TILE = 1024


def _route_plan(e1, e2, n_exp, tile):
    n = e1.shape[0]
    pair_e = jnp.stack([e1, e2], axis=1).reshape(-1)
    onehot = (pair_e[:, None] == jnp.arange(n_exp, dtype=jnp.int32)[None, :]).astype(jnp.int32)
    before = jnp.cumsum(onehot, axis=0) - onehot
    rank = jnp.sum(before * onehot, axis=1)
    counts = jnp.sum(onehot, axis=0)
    padded = (counts + tile - 1) // tile * tile
    ends = jnp.cumsum(padded)
    starts = ends - padded
    dest = starts[pair_e] + rank
    n_rows = (2 * n + n_exp * (tile - 1)) // tile * tile
    n_tiles = n_rows // tile
    row_token = jnp.zeros((n_rows,), jnp.int32).at[dest].set(
        jnp.arange(2 * n, dtype=jnp.int32) // 2, unique_indices=True, mode="promise_in_bounds")
    tile_start = jnp.arange(n_tiles, dtype=jnp.int32) * tile
    tile_expert = jnp.minimum(jnp.sum((tile_start[:, None] >= ends[None, :]).astype(jnp.int32), axis=1),
                              n_exp - 1)
    n_used = (ends[-1] // tile).astype(jnp.int32).reshape(1)
    return row_token, dest.reshape(n, 2), tile_expert, n_used


def _moe_gemm_kernel(te_ref, nused_ref, tok_ref, tokn_ref, x_hbm, w1_ref, w3_ref, w2_ref, y_ref,
                     xbuf_ref, sem, *, fc, tile):
    i = pl.program_id(0)
    f = pl.program_id(1)
    slot = i % 2
    n_used = nused_ref[0]

    def row_copy(t_ref, r, s):
        return pltpu.make_async_copy(x_hbm.at[pl.ds(t_ref[0, r], 1)], xbuf_ref.at[s, pl.ds(r, 1)], sem.at[s])

    def issue(t_ref, s):
        def body(r, carry):
            row_copy(t_ref, r, s).start()
            return carry
        lax.fori_loop(0, tile, body, 0, unroll=8)

    @pl.when(f == 0)
    def _():
        y_ref[...] = jnp.zeros_like(y_ref)

        @pl.when(i == 0)
        def _():
            issue(tok_ref, 0)

        @pl.when(i + 1 < n_used)
        def _():
            issue(tokn_ref, 1 - slot)

        @pl.when(i < n_used)
        def _():
            pltpu.make_async_copy(x_hbm.at[pl.ds(0, tile)], xbuf_ref.at[slot], sem.at[slot]).wait()

    @pl.when(i < n_used)
    def _():
        _swiglu_into(xbuf_ref[slot].astype(BF16), w1_ref, w3_ref, w2_ref, y_ref, fc)


def _moe_gemm(x_flat, row_token, tile_expert, n_used, w1, w3, w2, tile):
    n_rows = row_token.shape[0]
    n_tiles = n_rows // tile
    d = x_flat.shape[1]
    ff = w1.shape[2]
    nf = 2 if ff % (2 * LANES) == 0 else 1
    tf = ff // nf
    last = lambda i, nu: jnp.minimum(i, nu[0] - 1)
    fsel = lambda i, f, nu: jnp.where(i < nu[0], f, nf - 1)
    tok = lambda imap: pl.BlockSpec((None, 1, tile), imap, memory_space=pltpu.SMEM)
    tokens = row_token.reshape(n_tiles, 1, tile)
    return pl.pallas_call(
        functools.partial(_moe_gemm_kernel, fc=_pick_chunk(tf, 256), tile=tile),
        grid_spec=pltpu.PrefetchScalarGridSpec(
            num_scalar_prefetch=2,
            grid=(n_tiles, nf),
            in_specs=[tok(lambda i, f, te, nu: (i, 0, 0)),
                      tok(lambda i, f, te, nu: (jnp.minimum(i + 1, n_tiles - 1), 0, 0)),
                      pl.BlockSpec(memory_space=pl.ANY),
                      pl.BlockSpec((None, d, tf), lambda i, f, te, nu: (te[last(i, nu)], 0, fsel(i, f, nu))),
                      pl.BlockSpec((None, d, tf), lambda i, f, te, nu: (te[last(i, nu)], 0, fsel(i, f, nu))),
                      pl.BlockSpec((None, tf, d), lambda i, f, te, nu: (te[last(i, nu)], fsel(i, f, nu), 0))],
            out_specs=pl.BlockSpec((tile, d), lambda i, f, te, nu: (i, 0)),
            scratch_shapes=[pltpu.VMEM((2, tile, d), F32), pltpu.SemaphoreType.DMA((2,))]),
        out_shape=jax.ShapeDtypeStruct((n_rows, d), F32),
        compiler_params=_cparams(("arbitrary", "arbitrary")),
        name="moe_gemm",
    )(tile_expert, n_used, tokens, tokens, x_flat, w1, w3, w2)


def _moe_combine_kernel(pos_ref, posn_ref, route_ref, x1_ref, modx_ref, modz_ref, g2_ref, b2_ref, y_hbm,
                        o_ref, buf_ref, sem, *, nzb, nblk, alpha, tm):
    i = pl.program_id(0)
    slot = i % 2

    def row_copy(p_ref, r, k, s):
        return pltpu.make_async_copy(y_hbm.at[pl.ds(p_ref[0, k * tm + r], 1)],
                                     buf_ref.at[s, pl.ds(k * tm + r, 1)], sem.at[s])

    def issue(p_ref, s):
        def body(r, carry):
            row_copy(p_ref, r, 0, s).start()
            row_copy(p_ref, r, 1, s).start()
            return carry
        lax.fori_loop(0, tm, body, 0, unroll=8)

    @pl.when(i == 0)
    def _():
        issue(pos_ref, 0)

    @pl.when(i + 1 < pl.num_programs(0))
    def _():
        issue(posn_ref, 1 - slot)

    pltpu.make_async_copy(y_hbm.at[pl.ds(0, 2 * tm)], buf_ref.at[slot], sem.at[slot]).wait()
    route = route_ref[...]
    f = route[:, 0:1] * buf_ref[slot, 0:tm] + route[:, 1:2] * buf_ref[slot, tm:2 * tm]
    is_ctx = (i % nblk) < nzb
    mod = jnp.where(is_ctx, modz_ref[...], modx_ref[...])
    o_ref[...] = _ln(alpha * x1_ref[...] + mod[5:6] * f) * g2_ref[...] + b2_ref[...]


def _moe_combine(ys, pos, route, x1, mod, ln_g, ln_b, n_ctx_tokens, alpha):
    bsz, s, d = x1.shape
    tm = TOKEN_BLOCK
    nblk = s // tm
    n_steps = bsz * nblk
    nrow = mod.shape[0]
    pos_steps = pos.reshape(n_steps, tm, 2).transpose(0, 2, 1).reshape(n_steps, 1, 2 * tm)
    tok = lambda width: pl.BlockSpec((tm, width), lambda i: (i, 0))
    smem = lambda imap: pl.BlockSpec((None, 1, 2 * tm), imap, memory_space=pltpu.SMEM)
    vec = lambda a: a.reshape(1, -1).astype(F32)
    out = pl.pallas_call(
        functools.partial(_moe_combine_kernel, nzb=n_ctx_tokens // tm, nblk=nblk, alpha=alpha, tm=tm),
        grid=(n_steps,),
        in_specs=[smem(lambda i: (i, 0, 0)),
                  smem(lambda i: (jnp.minimum(i + 1, n_steps - 1), 0, 0)),
                  tok(LANES), tok(d),
                  pl.BlockSpec((None, 6, d), lambda i: (i // nblk, 0, 0)),
                  pl.BlockSpec((None, 6, d), lambda i: (nrow - 1, 0, 0)),
                  pl.BlockSpec((1, d), lambda i: (0, 0)), pl.BlockSpec((1, d), lambda i: (0, 0)),
                  pl.BlockSpec(memory_space=pl.ANY)],
        out_specs=tok(d),
        out_shape=jax.ShapeDtypeStruct((bsz * s, d), F32),
        scratch_shapes=[pltpu.VMEM((2, 2 * tm, d), F32), pltpu.SemaphoreType.DMA((2,))],
        compiler_params=_cparams(("arbitrary",)),
        name="moe_combine",
    )(pos_steps, pos_steps, route.reshape(bsz * s, LANES), x1.reshape(bsz * s, d), mod, mod,
      vec(ln_g), vec(ln_b), ys)
    return out.reshape(bsz, s, d)


def _moe(fx, x1, route, mod, w1, w3, w2, ln_g, ln_b, n_ctx_tokens, alpha):
    bsz, s, d = x1.shape
    n_exp = w1.shape[0]
    idx = route.reshape(bsz * s, LANES)[:, 2:4].astype(jnp.int32)
    row_token, pos, tile_expert, n_used = _route_plan(idx[:, 0], idx[:, 1], n_exp, MOE_TILE)
    ys = _moe_gemm(fx.reshape(bsz * s, d), row_token, tile_expert, n_used, w1, w3, w2, MOE_TILE)
    return _moe_combine(ys, pos, route, x1, mod, ln_g, ln_b, n_ctx_tokens, alpha)


def _pick_chunk(total, target):
    best = LANES
    for c in range(LANES, target + 1, LANES):
        if total % c == 0:
            best = c
    return best


def kernel(x, c, ctx, c_ctx, w_mod, b_mod, w_in, s5_lam_re, s5_lam_im, s5_log_step, s5_b_re, s5_b_im,
           s5_c_re, s5_c_im, s5_d, s5_w_glu, s5_b_glu, attn_sink, ret_log_gamma, w_out,
           ln1_g, ln1_b, ln2_g, ln2_b, ffn_w1, ffn_w3, ffn_w2, moe_router, moe_w1, moe_w3, moe_w2):
    bsz, t_len, d = x.shape
    n_ctx = ctx.shape[1]
    depth = w_in.shape[0]
    alpha = (2 * depth) ** 0.25
    s5_w = s5_d.shape[1]
    att_w = attn_sink.shape[1] * HEAD_DIM
    kv_w = att_w // ATT_REP
    ret_w = ret_log_gamma.shape[2] * HEAD_DIM
    sizes = (s5_w, att_w, kv_w, kv_w, ret_w, ret_w, ret_w, ret_w)
    assert sum(sizes) == w_in.shape[2] and s5_w + att_w + ret_w == w_out.shape[1]
    assert n_ctx % TOKEN_BLOCK == 0 and t_len % TOKEN_BLOCK == 0 and t_len >= 3 * ATT_BLOCK

    pad = (-(bsz + 1)) % 8
    cvec = jnp.concatenate([jnp.zeros((pad, d), F32), c_ctx[None].astype(F32)], axis=0)
    cvec = jnp.concatenate([c.astype(F32), cvec], axis=0)
    mod_all = _modulation(cvec, w_mod.astype(F32), b_mod.astype(F32)).reshape(depth, bsz + pad + 1, 6, d)

    tabs = _rope_tables(t_len, n_ctx)
    xz = jnp.concatenate([ctx, x], axis=1).astype(F32)
    for l in range(depth):
        need_ctx = l < depth - 1
        mod = mod_all[l]
        u, qa, va, qr, vr, gr, ka, kr = _inproj(xz, mod, w_in[l].astype(BF16), tabs, sizes, n_ctx)
        s5w = _s5_weights(s5_lam_re[l], s5_lam_im[l], s5_log_step[l], s5_b_re[l], s5_b_im[l],
                          s5_c_re[l], s5_c_im[l], S5_CHUNK)
        y_s5 = _s5_scan(u, s5w, n_ctx)
        o_att = _attention(qa, ka, va, attn_sink[l], n_ctx, need_ctx)
        o_ret = _retention(qr, kr, vr, gr, ret_log_gamma[l], n_ctx)
        i = l // 2
        router = None if l % 2 == 0 else moe_router[i]
        outs = _outproj(y_s5, u, o_att, o_ret, xz, mod, s5_d[l], s5_w_glu[l], s5_b_glu[l], w_out[l],
                        ln1_g[l], ln1_b[l], router, n_ctx, need_ctx, alpha)
        ctx_tokens = n_ctx if need_ctx else 0
        if l % 2 == 0:
            x1, fx = outs
            xz = _ffn(fx, x1, mod, ffn_w1[i].astype(BF16), ffn_w3[i].astype(BF16), ffn_w2[i].astype(BF16),
                      ln2_g[l], ln2_b[l], ctx_tokens, alpha)
        else:
            x1, fx, route = outs
            xz = _moe(fx, x1, route, mod, moe_w1[i].astype(BF16), moe_w3[i].astype(BF16),
                      moe_w2[i].astype(BF16), ln2_g[l], ln2_b[l], ctx_tokens, alpha)
    return xz if xz.shape[1] == t_len else xz[:, n_ctx:]
```

```python
import functools
import math

import jax
import jax.numpy as jnp
import numpy as np
from jax import lax
from jax.experimental import pallas as pl
from jax.experimental.pallas import tpu as pltpu

F32 = jnp.float32
BF16 = jnp.bfloat16

GRID_W = 64
HEAD_DIM = 64
S5_GROUP = 16
ATT_REP = 4
WINDOW = 128
ATT_BLOCK = 128
RET_CHUNK = 128
TOP_K = 2
LN_EPS = 1e-5
ROPE_BASE = 10000.0
NEG_INF = -1e30

LANES = 128
S5_CHUNK = LANES
TOKEN_BLOCK = 256
INPROJ_BLOCK = 768
FFN_BLOCK = 768
VMEM_LIMIT = 56 * 1024 * 1024


def _cparams(sem):
    return pltpu.CompilerParams(dimension_semantics=sem, vmem_limit_bytes=VMEM_LIMIT)


def _dot(a, b):
    return jnp.dot(a, b, preferred_element_type=F32)


def _dot_nt(a, b):
    return lax.dot_general(a, b, (((1,), (1,)), ((), ())), preferred_element_type=F32)


def _dot_tn(a, b):
    return lax.dot_general(a, b, (((0,), (0,)), ((), ())), preferred_element_type=F32)


def _split_bf16(a):
    hi = a.astype(BF16)
    lo = (a - hi.astype(F32)).astype(BF16)
    return hi, lo


def _dot3(a, b):
    ah, al = _split_bf16(a)
    bh, bl = _split_bf16(b)
    return _dot(ah, bh) + (_dot(ah, bl) + _dot(al, bh))


def _ln(x):
    mu = jnp.mean(x, axis=-1, keepdims=True)
    xc = x - mu
    var = jnp.mean(xc * xc, axis=-1, keepdims=True)
    return xc * lax.rsqrt(var + LN_EPS)


def _silu(x):
    return x * (1.0 / (1.0 + jnp.exp(-x)))


def _sigmoid(x):
    return 1.0 / (1.0 + jnp.exp(-x))


def _gelu_tanh(x):
    c = math.sqrt(2.0 / math.pi)
    return 0.5 * x * (1.0 + jnp.tanh(c * (x + 0.044715 * (x * x * x))))


def _mod_kernel(c_ref, w_ref, b_ref, o_ref):
    o_ref[...] = _dot3(_silu(c_ref[...]), w_ref[...]) + b_ref[...]


def _modulation(cvec, w_mod, b_mod):
    depth, d, n = w_mod.shape
    rows = cvec.shape[0]
    tn = 1536
    return pl.pallas_call(
        _mod_kernel,
        grid=(depth, n // tn),
        in_specs=[pl.BlockSpec((rows, d), lambda l, j: (0, 0)),
                  pl.BlockSpec((None, d, tn), lambda l, j: (l, 0, j)),
                  pl.BlockSpec((None, 1, tn), lambda l, j: (l, 0, j))],
        out_specs=pl.BlockSpec((None, rows, tn), lambda l, j: (l, 0, j)),
        out_shape=jax.ShapeDtypeStruct((depth, rows, n), F32),
        compiler_params=_cparams(("parallel", "parallel")),
        name="modulation",
    )(cvec, w_mod, b_mod.reshape(depth, 1, n))


def _rope_slab(xs, cos, sa, sb, half):
    return xs * cos + pltpu.roll(xs, LANES - half, 1) * sa + pltpu.roll(xs, half, 1) * sb


def _inproj_kernel(x_ref, modx_ref, modz_ref, wc_ref, w_ref, act_ref, ast_ref, rct_ref, rst_ref,
                   ac_ref, asa_ref, asb_ref, rc_ref, rsa_ref, rsb_ref,
                   ut_ref, qat_ref, vat_ref, qrt_ref, vrt_ref, grt_ref, ka_ref, kr_ref, *, n_ctx, c_sizes, sizes):
    tm = x_ref.shape[0]
    is_ctx = pl.program_id(1) * tm + lax.broadcasted_iota(jnp.int32, (tm, 1), 0) < n_ctx
    shift = jnp.where(is_ctx, modz_ref[0:1, :], modx_ref[0:1, :])
    gain = 1.0 + jnp.where(is_ctx, modz_ref[1:2, :], modx_ref[1:2, :])
    h = (_ln(x_ref[...]) * gain + shift).astype(BF16)
    scale = HEAD_DIM ** -0.5

    ct = _dot_nt(wc_ref[...], h)
    c_offs = [int(o) for o in np.concatenate([[0], np.cumsum(c_sizes)])]

    def chan(i):
        return ct[c_offs[i]:c_offs[i + 1]]

    def rope_t(x, cos_ref, sin_ref, part, mul):
        cos_t, sin_t = cos_ref[...], sin_ref[...]
        heads = []
        for hd in range(x.shape[0] // HEAD_DIM):
            xh = x[hd * HEAD_DIM:(hd + 1) * HEAD_DIM]
            pieces = []
            for lo in range(0, HEAD_DIM, 2 * part):
                pieces += [xh[lo + part:lo + 2 * part], xh[lo:lo + part]]
            out = xh * cos_t + jnp.concatenate(pieces, axis=0) * sin_t
            heads.append(out * mul if mul != 1.0 else out)
        return jnp.concatenate(heads, axis=0)

    def put(ref, val):
        for k in range(ref.shape[0]):
            ref[k] = val[:, k * ATT_BLOCK:(k + 1) * ATT_BLOCK].astype(ref.dtype)

    ut_ref[...] = chan(0)
    put(qat_ref, rope_t(chan(1), act_ref, ast_ref, HEAD_DIM // 4, scale))
    put(vat_ref, chan(2))
    put(qrt_ref, rope_t(chan(3), rct_ref, rst_ref, HEAD_DIM // 2, 1.0))
    put(vrt_ref, chan(4))
    put(grt_ref, chan(5))

    p = _dot(h, w_ref[...])
    ac, asa, asb = ac_ref[...], asa_ref[...], asb_ref[...]
    rc, rsa, rsb = rc_ref[...], rsa_ref[...], rsb_ref[...]
    offs = np.concatenate([[0], np.cumsum(sizes)])

    def rope_cols(i, cos, sa, sb, half, mul):
        blk = p[:, int(offs[i]):int(offs[i + 1])]
        slabs = [_rope_slab(blk[:, s:s + LANES], cos, sa, sb, half) for s in range(0, blk.shape[1], LANES)]
        out = slabs[0] if len(slabs) == 1 else jnp.concatenate(slabs, axis=1)
        return out * mul if mul != 1.0 else out

    ka_ref[...] = rope_cols(0, ac, asa, asb, HEAD_DIM // 4, 1.0).astype(ka_ref.dtype)
    kr_ref[...] = rope_cols(1, rc, rsa, rsb, HEAD_DIM // 2, scale)


def _inproj(xz, mod, w_in, tabs, sizes, n_ctx):
    bsz, s, d = xz.shape
    tm = _pick_chunk(s, INPROJ_BLOCK)
    assert tm % TOKEN_BLOCK == 0, "the output projection reads the S5 arrays in TOKEN_BLOCK pieces"
    nrow = mod.shape[0]
    offs = np.concatenate([[0], np.cumsum(sizes)])
    col = lambda i: w_in[:, int(offs[i]):int(offs[i + 1])]
    c_idx, r_idx = (0, 1, 3, 4, 6, 7), (2, 5)
    c_dtypes = (BF16, BF16, F32, BF16, F32)
    r_dtypes = (BF16, F32)
    c_sizes = tuple(sizes[i] for i in c_idx)
    r_sizes = tuple(sizes[i] for i in r_idx)
    w_c_t = jnp.concatenate([col(i) for i in c_idx], axis=1).T
    w_rest = jnp.concatenate([col(i) for i in r_idx], axis=1)
    sub = tm // ATT_BLOCK
    tok = lambda width: pl.BlockSpec((None, tm, width), lambda b, j: (b, j, 0))
    chan = lambda width: pl.BlockSpec((sub, None, width, ATT_BLOCK), lambda b, j: (j, b, 0, 0))
    tab = pl.BlockSpec((tm, LANES), lambda b, j: (j, 0))
    tab_t = pl.BlockSpec((HEAD_DIM, tm), lambda b, j: (0, j))
    return pl.pallas_call(
        functools.partial(_inproj_kernel, n_ctx=n_ctx, c_sizes=c_sizes, sizes=r_sizes),
        grid=(bsz, s // tm),
        in_specs=[tok(d),
                  pl.BlockSpec((None, 6, d), lambda b, j: (b, 0, 0)),
                  pl.BlockSpec((None, 6, d), lambda b, j: (nrow - 1, 0, 0)),
                  pl.BlockSpec((sum(c_sizes), d), lambda b, j: (0, 0)),
                  pl.BlockSpec((d, sum(r_sizes)), lambda b, j: (0, 0)),
                  tab_t, tab_t, tab_t, tab_t, tab, tab, tab, tab, tab, tab],
        out_specs=[pl.BlockSpec((None, None, c_sizes[0], tm), lambda b, j: (j, b, 0, 0))]
                  + [chan(w) for w in c_sizes[1:]] + [tok(w) for w in r_sizes],
        out_shape=[jax.ShapeDtypeStruct((s // tm, bsz, c_sizes[0], tm), F32)]
                  + [jax.ShapeDtypeStruct((s // ATT_BLOCK, bsz, w, ATT_BLOCK), dt)
                     for w, dt in zip(c_sizes[1:], c_dtypes)]
                  + [jax.ShapeDtypeStruct((bsz, s, w), dt) for w, dt in zip(r_sizes, r_dtypes)],
        compiler_params=_cparams(("parallel", "parallel")),
        name="inproj",
    )(xz, mod, mod, w_c_t, w_rest, *tabs)


def _rope_tables(t_len, n_ctx):
    t = jnp.arange(t_len)
    rows = (t // GRID_W).astype(F32)
    cols = (t % GRID_W).astype(F32)
    pos = t.astype(F32)

    def angles(p, dim):
        inv_freq = ROPE_BASE ** (-jnp.arange(0, dim, 2, dtype=F32) / dim)
        return p[:, None] * inv_freq[None, :]

    def head_tables(angs):
        cos = jnp.concatenate([jnp.concatenate([jnp.cos(a), jnp.cos(a)], -1) for a in angs], -1)
        sa = jnp.concatenate([jnp.concatenate([-jnp.sin(a), jnp.zeros_like(a)], -1) for a in angs], -1)
        sb = jnp.concatenate([jnp.concatenate([jnp.zeros_like(a), jnp.sin(a)], -1) for a in angs], -1)
        rep = LANES // HEAD_DIM
        return tuple(jnp.tile(x, (1, rep)) for x in (cos, sa, sb))

    att = head_tables([angles(rows, HEAD_DIM // 2), angles(cols, HEAD_DIM // 2)])
    ret = head_tables([angles(pos, HEAD_DIM)])
    signed_t = lambda tb: (tb[0][:, :HEAD_DIM].T, (tb[1] + tb[2])[:, :HEAD_DIM].T)

    def with_ctx(tb, axis, is_cos):
        shape = list(tb.shape)
        shape[axis] = n_ctx
        lead = jnp.ones(shape, F32) if is_cos else jnp.zeros(shape, F32)
        return jnp.concatenate([lead, tb], axis=axis)

    chan = signed_t(att) + signed_t(ret)
    toks = att + ret
    return (tuple(with_ctx(tb, 1, k % 2 == 0) for k, tb in enumerate(chan))
            + tuple(with_ctx(tb, 0, k % 3 == 0) for k, tb in enumerate(toks)))


def _s5_kernel(u_ref, tab_ref, wsf_ref, wsb_ref, wrf_ref, wrb_ref, lam_ref, y_ref,
               lhs_ref, m_ref, acc_ref, cf_ref, cb_ref, pf_ref, pb_ref, *, bsz, nz_chunks):
    r_blk, ch, tm = u_ref.shape
    c = S5_CHUNK
    kpb = tm // c
    n_chunks = (r_blk // bsz) * kpb

    for i in range(ch):
        x = u_ref[:, i, :].astype(BF16)
        p, half = divmod(i, 2)
        for k in range(kpb):
            lhs_ref[p, k * r_blk:(k + 1) * r_blk, half * c:(half + 1) * c] = x[:, k * c:(k + 1) * c]
    acc_ref[...] = jnp.zeros_like(acc_ref)
    cf_ref[...] = jnp.zeros_like(cf_ref)
    cb_ref[...] = jnp.zeros_like(cb_ref)

    def pair(p, carry):
        for ih in range(2):
            for o in range(ch):
                lag_row = jnp.broadcast_to(tab_ref[2 * p + ih, o:o + 1, :], (8, 2 * c))
                tab = pltpu.roll(lag_row, 0, 1, stride=1, stride_axis=0)
                for r2 in range(c // 16):
                    lo = c - 16 * r2
                    rows = jnp.concatenate([tab[:, lo:lo + c], tab[:, lo - 8:lo - 8 + c]], axis=0)
                    m_ref[ih * c + 16 * r2:ih * c + 16 * r2 + 16, o * c:(o + 1) * c] = rows.astype(BF16)
        lhs = lhs_ref[p]
        acc_ref[...] += _dot(lhs, m_ref[...])
        cf_ref[...] += _dot(lhs, wsf_ref[p])
        cb_ref[...] += _dot(lhs, wsb_ref[p])
        return carry

    lax.fori_loop(0, ch // 2, pair, 0)
    lam = lam_ref[...]
    nst = lam.shape[1] // 2

    def advance(s, a, bc, drive):
        return s * a + pltpu.roll(s, nst, 1) * bc + drive

    def sweep(order, c_ref, p_ref, a, bc):
        s = jnp.zeros((bsz, lam.shape[1]), F32)
        for n in order:
            j, k = divmod(n, kpb)
            rows = slice(k * r_blk + j * bsz, k * r_blk + (j + 1) * bsz)
            p_ref[rows, :] = s
            s = advance(s, a, bc, c_ref[rows, :])

    sweep(list(range(n_chunks)), cf_ref, pf_ref, lam[0:1], lam[1:2])
    order_b = list(range(nz_chunks - 1, -1, -1)) + list(range(n_chunks - 1, nz_chunks - 1, -1))
    sweep(order_b, cb_ref, pb_ref, lam[2:3], lam[3:4])
    acc_ref[...] += (_dot(pf_ref[...].astype(BF16), wrf_ref[...])
                     + _dot(pb_ref[...].astype(BF16), wrb_ref[...]))
    for o in range(ch):
        for k in range(kpb):
            y_ref[:, o, k * c:(k + 1) * c] = acc_ref[k * r_blk:(k + 1) * r_blk, o * c:(o + 1) * c]


def _s5_weights(lam_re, lam_im, log_step, b_re, b_im, c_re, c_im, chunk):
    hp = lax.Precision.HIGHEST
    lam = lax.complex(lam_re.astype(F32), lam_im.astype(F32))
    lam_dt = lam * jnp.exp(log_step.astype(F32))[..., None]
    lam_bar = jnp.exp(lam_dt)
    b_bar = lax.complex(b_re.astype(F32), b_im.astype(F32)) * ((lam_bar - 1.0) / lam)[..., None]
    c_mat = lax.complex(c_re.astype(F32), c_im.astype(F32))
    g, n = lam.shape[1], lam.shape[2]
    ch = b_bar.shape[-1]
    steps = jnp.arange(chunk + 1, dtype=F32)
    pw = jnp.exp(steps[None, :, None, None] * lam_dt[:, None])
    kern = jnp.einsum('zgon,zdgn,zgni->zgdoi', c_mat, pw[:, :chunk], b_bar, precision=hp).real
    zero_lag = kern[0, :, 0] + kern[1, :, 0]
    lag_table = jnp.concatenate([jnp.zeros_like(zero_lag)[:, None], kern[1, :, :0:-1], zero_lag[:, None],
                                 kern[0, :, 1:]], axis=1).transpose(0, 3, 2, 1)

    def state_in(pw_s, b_dir):
        w = pw_s[:, :, :, None] * b_dir[None]
        w = jnp.concatenate([w.real, w.imag], axis=2)
        return w.transpose(1, 3, 0, 2).reshape(g, ch // 2, 2 * chunk, 2 * n)

    def state_out(pw_t, c_dir):
        w = c_dir[None] * pw_t[:, :, None, :]
        w = jnp.concatenate([w.real, -w.imag], axis=3)
        return w.transpose(1, 3, 2, 0).reshape(g, 2 * n, ch * chunk)

    wsf = state_in(pw[0, chunk - 1::-1][:chunk], b_bar[0])
    wsb = state_in(pw[1, :chunk], b_bar[1])
    wrf = state_out(pw[0, 1:chunk + 1], c_mat[0])
    wrb = state_out(pw[1, chunk:0:-1], c_mat[1])
    lam_c = pw[:, chunk]
    rows = []
    for z in range(2):
        rows.append(jnp.concatenate([lam_c[z].real, lam_c[z].real], -1))
        rows.append(jnp.concatenate([-lam_c[z].imag, lam_c[z].imag], -1))
    lam_rows = jnp.stack(rows + [jnp.zeros_like(rows[0])] * 4, axis=1)
    return lag_table, wsf.astype(BF16), wsb.astype(BF16), wrf.astype(BF16), wrb.astype(BF16), lam_rows


def _s5_scan(ut, weights, n_ctx):
    nblk, bsz, width, tm = ut.shape
    tab, wsf, wsb, wrf, wrb, lam_rows = weights
    g, ch = tab.shape[0], tab.shape[1]
    c = S5_CHUNK
    nst2 = wsf.shape[3]
    r_blk = nblk * bsz
    rows = r_blk * (tm // c)
    grp = lambda *shape: pl.BlockSpec((None,) + shape, lambda i: (i,) + (0,) * len(shape))
    tok = pl.BlockSpec((r_blk, ch, tm), lambda i: (0, i, 0))
    y = pl.pallas_call(
        functools.partial(_s5_kernel, bsz=bsz, nz_chunks=n_ctx // c),
        grid=(g,),
        in_specs=[tok, grp(ch, ch, 2 * c), grp(ch // 2, 2 * c, nst2), grp(ch // 2, 2 * c, nst2),
                  grp(nst2, ch * c), grp(nst2, ch * c), grp(8, nst2)],
        out_specs=tok,
        out_shape=jax.ShapeDtypeStruct((r_blk, width, tm), F32),
        scratch_shapes=[pltpu.VMEM((ch // 2, rows, 2 * c), BF16),
                        pltpu.VMEM((2 * c, ch * c), BF16),
                        pltpu.VMEM((rows, ch * c), F32)] + [pltpu.VMEM((rows, nst2), F32)] * 4,
        compiler_params=_cparams(("parallel",)),
        name="s5_scan",
    )(ut.reshape(r_blk, width, tm), tab, wsf, wsb, wrf, wrb, lam_rows)
    return y.reshape(nblk, bsz, width, tm)


def _attn_kernel(sink_ref, q_ref, k_ref, v_ref, o_ref, *, n_ctx, t_len, q_off, kv_heads):
    qi = pl.program_id(1) + q_off
    nzb = n_ctx // ATT_BLOCK
    band = 3 * ATT_BLOCK
    wide = ATT_REP * ATT_BLOCK
    q = q_ref[...].astype(BF16)
    kc = k_ref[0:n_ctx, :].astype(BF16)
    vc = jnp.concatenate([v_ref[p] for p in range(nzb)], axis=1).astype(BF16)

    def rows(a, i):
        return a[i * HEAD_DIM:(i + 1) * HEAD_DIM]

    def q_group(g):
        qg = jnp.concatenate([rows(q, g * ATT_REP + r) for r in range(ATT_REP)], axis=1)
        zero = jnp.zeros_like(qg)
        return jnp.concatenate([qg if gg == g else zero for gg in range(kv_heads)], axis=0)

    def finish(g, o, den):
        o = o / den
        for r in range(ATT_REP):
            h = g * ATT_REP + r
            o_ref[h * HEAD_DIM:(h + 1) * HEAD_DIM, :] = o[:, r * ATT_BLOCK:(r + 1) * ATT_BLOCK].astype(o_ref.dtype)

    @pl.when(qi < nzb)
    def _():
        for g in range(kv_heads):
            sink = sink_ref[g:g + 1, :]
            s = _dot(kc, q_group(g))
            m = jnp.maximum(jnp.max(s, axis=0, keepdims=True), sink)
            e = jnp.exp(s - m)
            den = jnp.sum(e, axis=0, keepdims=True) + jnp.exp(sink - m)
            finish(g, _dot(rows(vc, g), e.astype(BF16)), den)

    @pl.when(qi >= nzb)
    def _():
        n = qi - nzb
        start = jnp.clip((n - 1) * ATT_BLOCK, 0, t_len - band)
        kl = k_ref[pl.ds(pl.multiple_of(n_ctx + start, ATT_BLOCK), band), :].astype(BF16)
        p0 = (n_ctx + start) // ATT_BLOCK
        vl = jnp.concatenate([v_ref[p0 + t] for t in range(3)], axis=1).astype(BF16)
        k_pos = start + lax.broadcasted_iota(jnp.int32, (band, wide), 0)
        q_pos = n * ATT_BLOCK + (lax.broadcasted_iota(jnp.int32, (band, wide), 1) & (ATT_BLOCK - 1))
        valid = jnp.abs(k_pos - q_pos) <= WINDOW
        for g in range(kv_heads):
            sink = sink_ref[g:g + 1, :]
            qg = q_group(g)
            s_loc = jnp.where(valid, _dot(kl, qg), NEG_INF)
            s_ctx = _dot(kc, qg)
            m = jnp.maximum(jnp.maximum(jnp.max(s_loc, axis=0, keepdims=True),
                                        jnp.max(s_ctx, axis=0, keepdims=True)), sink)
            e_loc = jnp.exp(s_loc - m)
            e_ctx = jnp.exp(s_ctx - m)
            den = (jnp.sum(e_loc, axis=0, keepdims=True) + jnp.sum(e_ctx, axis=0, keepdims=True)
                   + jnp.exp(sink - m))
            o = _dot(rows(vl, g), e_loc.astype(BF16)) + _dot(rows(vc, g), e_ctx.astype(BF16))
            finish(g, o, den)


def _attention(qat, ka, vat, sink, n_ctx, need_ctx):
    nblk, bsz, qw, _ = qat.shape
    s, kvw = ka.shape[1], ka.shape[2]
    t_len = s - n_ctx
    kv_heads = kvw // HEAD_DIM
    q_off = 0 if need_ctx else n_ctx // ATT_BLOCK
    nq = nblk - q_off
    sink_rows = jnp.repeat(sink.astype(F32).reshape(kv_heads, ATT_REP), ATT_BLOCK, axis=1)
    return pl.pallas_call(
        functools.partial(_attn_kernel, n_ctx=n_ctx, t_len=t_len, q_off=q_off, kv_heads=kv_heads),
        grid=(bsz, nq),
        in_specs=[pl.BlockSpec(sink_rows.shape, lambda b, j: (0, 0)),
                  pl.BlockSpec((None, None, qw, ATT_BLOCK), lambda b, j: (j + q_off, b, 0, 0)),
                  pl.BlockSpec((None, s, kvw), lambda b, j: (b, 0, 0)),
                  pl.BlockSpec((nblk, None, kvw, ATT_BLOCK), lambda b, j: (0, b, 0, 0))],
        out_specs=pl.BlockSpec((None, None, qw, ATT_BLOCK), lambda b, j: (j, b, 0, 0)),
        out_shape=jax.ShapeDtypeStruct((nq, bsz, qw, ATT_BLOCK), BF16),
        compiler_params=_cparams(("parallel", "arbitrary")),
        name="window_attention",
    )(sink_rows, qat, ka, vat)


def _ret_kernel(lg_ref, q_ref, k_ref, v_ref, g_ref, o_ref, acc_ref, sf_ref, sb_ref, *, nz_chunks, heads):
    c = RET_CHUNK
    n_chunks, w, _ = q_ref.shape
    f32 = lambda a: a.astype(F32)

    def per_head(shape, axis, group, direction):
        owner = lax.broadcasted_iota(jnp.int32, shape, axis) // group
        out = jnp.zeros(shape, F32)
        for h in range(heads):
            out = jnp.where(owner == h, lg_ref[direction, h], out)
        return out

    row_i = f32(lax.broadcasted_iota(jnp.int32, (w, c), 1))
    qw_f = jnp.exp(per_head((w, c), 0, HEAD_DIM, 0) * (row_i + 1.0))
    qw_b = jnp.exp(per_head((w, c), 0, HEAD_DIM, 1) * (c - row_i))
    key_j = f32(lax.broadcasted_iota(jnp.int32, (c, w), 0))
    kw_f = jnp.exp(per_head((c, w), 1, HEAD_DIM, 0) * (c - 1.0 - key_j))
    kw_b = jnp.exp(per_head((c, w), 1, HEAD_DIM, 1) * key_j)
    dec_f = jnp.exp(per_head((w, 1), 0, HEAD_DIM, 0) * c)
    dec_b = jnp.exp(per_head((w, 1), 0, HEAD_DIM, 1) * c)
    wide = heads * c
    diff = f32((lax.broadcasted_iota(jnp.int32, (c, wide), 1) & (c - 1)) - lax.broadcasted_iota(jnp.int32, (c, wide), 0))
    dec_t = jnp.where(diff >= 0, jnp.exp(per_head((c, wide), 1, c, 0) * jnp.maximum(diff, 0.0)),
                      jnp.exp(per_head((c, wide), 1, c, 1) * jnp.maximum(-diff, 0.0)))
    own_wide = (lax.broadcasted_iota(jnp.int32, (w, wide), 0) // HEAD_DIM
                == lax.broadcasted_iota(jnp.int32, (w, wide), 1) // c)
    own_sq = f32(lax.broadcasted_iota(jnp.int32, (w, w), 0) // HEAD_DIM
                 == lax.broadcasted_iota(jnp.int32, (w, w), 1) // HEAD_DIM)
    sf_ref[...] = jnp.zeros_like(sf_ref)
    sb_ref[...] = jnp.zeros_like(sb_ref)

    def block_diag(x):
        return jnp.where(own_wide, jnp.concatenate([x] * heads, axis=1), 0.0).astype(BF16)

    def fwd(n, carry):
        qt, vt = q_ref[n], v_ref[n]
        k = k_ref[pl.ds(pl.multiple_of(n * c, c), c), :]
        scores_t = _dot(k.astype(BF16), block_diag(qt)) * dec_t
        stacked = jnp.concatenate([scores_t[:, h * c:(h + 1) * c] for h in range(heads)], axis=0)
        o = _dot(block_diag(vt), stacked.astype(BF16))
        s_prev = sf_ref[...]
        o = o + _dot(s_prev.astype(BF16), (qt * qw_f).astype(BF16))
        sf_ref[...] = dec_f * s_prev + own_sq * _dot(vt.astype(BF16), (k * kw_f).astype(BF16))
        acc_ref[n] = o
        return carry

    lax.fori_loop(0, n_chunks, fwd, 0)

    def bwd(i, carry):
        n = jnp.where(i < nz_chunks, nz_chunks - 1 - i, n_chunks - 1 - i + nz_chunks)
        qt, vt = q_ref[n], v_ref[n]
        k = k_ref[pl.ds(pl.multiple_of(n * c, c), c), :]
        s_prev = sb_ref[...]
        o = acc_ref[n] + _dot(s_prev.astype(BF16), (qt * qw_b).astype(BF16))
        sb_ref[...] = dec_b * s_prev + own_sq * _dot(vt.astype(BF16), (k * kw_b).astype(BF16))
        gate = _silu(g_ref[n])
        for h in range(heads):
            rows = slice(h * HEAD_DIM, (h + 1) * HEAD_DIM)
            oh = o[rows]
            mu = jnp.mean(oh, axis=0, keepdims=True)
            oc = oh - mu
            var = jnp.mean(oc * oc, axis=0, keepdims=True)
            o_ref[n, rows, :] = (oc * lax.rsqrt(var + LN_EPS) * gate[rows]).astype(o_ref.dtype)
        return carry

    lax.fori_loop(0, n_chunks, bwd, 0)


def _retention(qrt, kr, vrt, grt, log_gamma, n_ctx):
    n_chunks, bsz, w, c = qrt.shape
    s = kr.shape[1]
    chan = pl.BlockSpec((n_chunks, None, w, c), lambda b: (0, b, 0, 0))
    return pl.pallas_call(
        functools.partial(_ret_kernel, nz_chunks=n_ctx // RET_CHUNK, heads=w // HEAD_DIM),
        grid=(bsz,),
        in_specs=[pl.BlockSpec(memory_space=pltpu.SMEM), chan, pl.BlockSpec((None, s, w), lambda b: (b, 0, 0)),
                  chan, chan],
        out_specs=chan,
        out_shape=jax.ShapeDtypeStruct((n_chunks, bsz, w, c), BF16),
        scratch_shapes=[pltpu.VMEM((n_chunks, w, c), F32), pltpu.VMEM((w, w), F32), pltpu.VMEM((w, w), F32)],
        compiler_params=_cparams(("parallel",)),
        name="retention",
    )(log_gamma.astype(F32), qrt, kr, vrt, grt)


def _outproj_kernel(*refs, nzb, alpha, with_router, w_s5, w_att, n_exp):
    if with_router:
        (y_ref, u_ref, a_ref, r_ref, x_ref, modx_ref, modz_ref, d_ref, wg_ref, bg_ref, wo_ref,
         g1_ref, b1_ref, rt_ref, x1_ref, fx_ref, gate_ref) = refs
    else:
        (y_ref, u_ref, a_ref, r_ref, x_ref, modx_ref, modz_ref, d_ref, wg_ref, bg_ref, wo_ref,
         g1_ref, b1_ref, x1_ref, fx_ref) = refs
    is_ctx = pl.program_id(1) < nzb
    mod = jnp.where(is_ctx, modz_ref[...], modx_ref[...])
    g = _gelu_tanh(y_ref[...] + d_ref[...] * u_ref[...])
    s5 = g * _sigmoid(_dot(wg_ref[...], g.astype(BF16)) + bg_ref[...])
    def chan_major(ref, lo, hi):
        return jnp.concatenate([_dot_tn(ref[k].astype(BF16), wo_ref[lo:hi, :]) for k in range(ref.shape[0])],
                               axis=0)

    mix = (_dot_tn(s5.astype(BF16), wo_ref[0:w_s5, :]) + chan_major(a_ref, w_s5, w_s5 + w_att)
           + chan_major(r_ref, w_s5 + w_att, wo_ref.shape[0]))
    x1 = _ln(alpha * x_ref[...] + mod[2:3] * mix) * g1_ref[...] + b1_ref[...]
    x1_ref[...] = x1
    fx = _ln(x1) * (1.0 + mod[4:5]) + mod[3:4]
    fx_ref[...] = fx.astype(fx_ref.dtype)
    if with_router:
        lane = lax.broadcasted_iota(jnp.int32, (fx.shape[0], LANES), 1)
        logits = jnp.where(lane < n_exp, _dot3(fx, rt_ref[...]), -jnp.inf)
        m1 = jnp.max(logits, axis=1, keepdims=True)
        i1 = jnp.min(jnp.where(logits == m1, lane, LANES), axis=1, keepdims=True)
        rest = jnp.where(lane == i1, -jnp.inf, logits)
        m2 = jnp.max(rest, axis=1, keepdims=True)
        i2 = jnp.min(jnp.where(rest == m2, lane, LANES), axis=1, keepdims=True)
        e2 = jnp.exp(m2 - m1)
        den = 1.0 + e2
        route = jnp.where(lane == 0, 1.0 / den, jnp.where(lane == 1, e2 / den, 0.0))
        route = jnp.where(lane == 2, i1.astype(F32), jnp.where(lane == 3, i2.astype(F32), route))
        gate_ref[...] = route


def _outproj(y_s5, u, o_att, o_ret, xz, mod, s5_d, w_glu, b_glu, w_out, ln_g, ln_b, router, n_ctx, need_ctx,
             alpha):
    bsz, s, d = xz.shape
    tm = TOKEN_BLOCK
    nzb = n_ctx // tm
    off = 0 if need_ctx else nzb
    nblk = s // tm - off
    s_out = nblk * tm
    nrow = mod.shape[0]
    w_s5, w_att = y_s5.shape[2], o_att.shape[2]
    sub = tm // ATT_BLOCK
    att_off = off if o_att.shape[0] == s // ATT_BLOCK else 0
    att = pl.BlockSpec((sub, None, w_att, ATT_BLOCK), lambda b, j: (j + att_off, b, 0, 0))

    def tok(width, shift):
        return pl.BlockSpec((None, tm, width), lambda b, j: (b, j + shift, 0))

    def full(a):
        return pl.BlockSpec(a.shape, lambda b, j: (0,) * a.ndim)

    per_in = y_s5.shape[3] // tm
    chan = pl.BlockSpec((None, None, w_s5, tm), lambda b, j: ((j + off) // per_in, b, 0, (j + off) % per_in))
    vec = lambda a: a.reshape(1, -1).astype(F32)
    col = lambda a: a.reshape(-1, 1).astype(F32)
    consts = [col(s5_d), w_glu.T.astype(BF16), col(b_glu), w_out.astype(BF16), vec(ln_g), vec(ln_b)]
    ret = pl.BlockSpec((sub, None, o_ret.shape[2], ATT_BLOCK), lambda b, j: (j + off, b, 0, 0))
    in_specs = [chan, chan, att, ret, tok(d, off),
                pl.BlockSpec((None, 6, d), lambda b, j: (b, 0, 0)),
                pl.BlockSpec((None, 6, d), lambda b, j: (nrow - 1, 0, 0))] + [full(a) for a in consts]
    out_specs = [tok(d, 0), tok(d, 0)]
    fx_dtype = BF16 if router is None else F32
    out_shape = [jax.ShapeDtypeStruct((bsz, s_out, d), F32), jax.ShapeDtypeStruct((bsz, s_out, d), fx_dtype)]
    args = [y_s5, u, o_att, o_ret, xz, mod, mod] + consts
    with_router = router is not None
    n_exp = 0
    if with_router:
        n_exp = router.shape[1]
        router_pad = jnp.pad(router.astype(F32), ((0, 0), (0, LANES - n_exp)))
        args.append(router_pad)
        in_specs.append(full(router_pad))
        out_specs.append(tok(LANES, 0))
        out_shape.append(jax.ShapeDtypeStruct((bsz, s_out, LANES), F32))
    return pl.pallas_call(
        functools.partial(_outproj_kernel, nzb=nzb - off, alpha=alpha, with_router=with_router,
                          w_s5=w_s5, w_att=w_att, n_exp=n_exp),
        grid=(bsz, nblk),
        in_specs=in_specs,
        out_specs=out_specs,
        out_shape=out_shape,
        compiler_params=_cparams(("parallel", "parallel")),
        name="outproj",
    )(*args)


def _swiglu_into(x_bf16, w1_ref, w3_ref, w2_ref, acc_ref, fc):
    for s in range(0, w1_ref.shape[1], fc):
        h1 = _dot(x_bf16, w1_ref[:, s:s + fc])
        h3 = _dot(x_bf16, w3_ref[:, s:s + fc])
        acc_ref[...] += _dot((_silu(h1) * h3).astype(BF16), w2_ref[s:s + fc, :])


def _ffn_kernel(fx_ref, x1_ref, modx_ref, modz_ref, w1_ref, w3_ref, w2_ref, g2_ref, b2_ref,
                o_ref, acc_ref, *, n_ctx_tokens, alpha, fc):
    tm = acc_ref.shape[0]
    acc_ref[...] = jnp.zeros_like(acc_ref)
    _swiglu_into(fx_ref[...], w1_ref, w3_ref, w2_ref, acc_ref, fc)
    row = pl.program_id(1) * tm + lax.broadcasted_iota(jnp.int32, (tm, 1), 0)
    gate = jnp.where(row < n_ctx_tokens, modz_ref[5:6, :], modx_ref[5:6, :])
    o_ref[...] = _ln(alpha * x1_ref[...] + gate * acc_ref[...]) * g2_ref[...] + b2_ref[...]


def _ffn(fx, x1, mod, w1, w3, w2, ln_g, ln_b, n_ctx_tokens, alpha):
    bsz, s, d = x1.shape
    ff = w1.shape[1]
    tm = _pick_chunk(s, FFN_BLOCK)
    nrow = mod.shape[0]
    tok = pl.BlockSpec((None, tm, d), lambda b, j: (b, j, 0))
    full = lambda a: pl.BlockSpec(a.shape, lambda b, j: (0,) * a.ndim, pipeline_mode=pl.Buffered(1))
    vec = lambda a: a.reshape(1, -1).astype(F32)
    return pl.pallas_call(
        functools.partial(_ffn_kernel, n_ctx_tokens=n_ctx_tokens, alpha=alpha, fc=_pick_chunk(ff, 512)),
        grid=(bsz, s // tm),
        in_specs=[tok, tok,
                  pl.BlockSpec((None, 6, d), lambda b, j: (b, 0, 0)),
                  pl.BlockSpec((None, 6, d), lambda b, j: (nrow - 1, 0, 0)),
                  full(w1), full(w3), full(w2),
                  pl.BlockSpec((1, d), lambda b, j: (0, 0)), pl.BlockSpec((1, d), lambda b, j: (0, 0))],
        out_specs=tok,
        out_shape=jax.ShapeDtypeStruct((bsz, s, d), F32),
        scratch_shapes=[pltpu.VMEM((tm, d), F32)],
        compiler_params=_cparams(("parallel", "parallel")),
        name="dense_ffn",
    )(fx, x1, mod, mod, w1, w3, w2, vec(ln_g), vec(ln_b))


MOE_TILE = 1024


def _route_plan(e1, e2, n_exp, tile):
    n = e1.shape[0]
    pair_e = jnp.stack([e1, e2], axis=1).reshape(-1)
    onehot = (pair_e[:, None] == jnp.arange(n_exp, dtype=jnp.int32)[None, :]).astype(jnp.int32)
    before = jnp.cumsum(onehot, axis=0) - onehot
    rank = jnp.sum(before * onehot, axis=1)
    counts = jnp.sum(onehot, axis=0)
    padded = (counts + tile - 1) // tile * tile
    ends = jnp.cumsum(padded)
    starts = ends - padded
    dest = starts[pair_e] + rank
    n_rows = (2 * n + n_exp * (tile - 1)) // tile * tile
    n_tiles = n_rows // tile
    tile_start = jnp.arange(n_tiles, dtype=jnp.int32) * tile
    tile_expert = jnp.minimum(jnp.sum((tile_start[:, None] >= ends[None, :]).astype(jnp.int32), axis=1),
                              n_exp - 1)
    by_expert = jnp.sort(pair_e * (2 * n) + jnp.arange(2 * n, dtype=jnp.int32)) % (2 * n)
    row = jnp.arange(n_rows, dtype=jnp.int32)
    row_e = jnp.repeat(tile_expert, tile)
    in_group = row - starts[row_e]
    src = jnp.minimum(in_group + (jnp.cumsum(counts) - counts)[row_e], 2 * n - 1)
    row_token = jnp.where(in_group < counts[row_e], by_expert[src] // 2, 0)
    n_used = (ends[-1] // tile).astype(jnp.int32).reshape(1)
    return row_token, dest.reshape(n, 2), tile_expert, n_used


def _moe_gemm_kernel(te_ref, nused_ref, tok_ref, tokn_ref, x_hbm, w1_ref, w3_ref, w2_ref, y_ref,
                     xbuf_ref, sem, *, fc, tile):
    i = pl.program_id(0)
    f = pl.program_id(1)
    slot = i % 2
    n_used = nused_ref[0]

    def row_copy(t_ref, r, s):
        return pltpu.make_async_copy(x_hbm.at[pl.ds(t_ref[0, r], 1)], xbuf_ref.at[s, pl.ds(r, 1)], sem.at[s])

    def issue(t_ref, s):
        def body(it, carry):
            base = pl.multiple_of(it * 8, 8)
            for k in range(8):
                row_copy(t_ref, base + k, s).start()
            return carry
        lax.fori_loop(0, tile // 8, body, 0)

    @pl.when(f == 0)
    def _():
        y_ref[...] = jnp.zeros_like(y_ref)

        @pl.when(i == 0)
        def _():
            issue(tok_ref, 0)

        @pl.when(i + 1 < n_used)
        def _():
            issue(tokn_ref, 1 - slot)

        @pl.when(i < n_used)
        def _():
            pltpu.make_async_copy(x_hbm.at[pl.ds(0, tile)], xbuf_ref.at[slot], sem.at[slot]).wait()

    @pl.when(i < n_used)
    def _():
        _swiglu_into(xbuf_ref[slot].astype(BF16), w1_ref, w3_ref, w2_ref, y_ref, fc)


def _moe_gemm(x_flat, row_token, tile_expert, n_used, w1, w3, w2, tile):
    n_rows = row_token.shape[0]
    n_tiles = n_rows // tile
    d = x_flat.shape[1]
    ff = w1.shape[2]
    nf = 2 if ff % (2 * LANES) == 0 else 1
    tf = ff // nf
    last = lambda i, nu: jnp.minimum(i, nu[0] - 1)
    fsel = lambda i, f, nu: jnp.where(i < nu[0], f, nf - 1)
    tok = lambda imap: pl.BlockSpec((None, 1, tile), imap, memory_space=pltpu.SMEM)
    tokens = row_token.reshape(n_tiles, 1, tile)
    return pl.pallas_call(
        functools.partial(_moe_gemm_kernel, fc=_pick_chunk(tf, 256), tile=tile),
        grid_spec=pltpu.PrefetchScalarGridSpec(
            num_scalar_prefetch=2,
            grid=(n_tiles, nf),
            in_specs=[tok(lambda i, f, te, nu: (i, 0, 0)),
                      tok(lambda i, f, te, nu: (jnp.minimum(i + 1, n_tiles - 1), 0, 0)),
                      pl.BlockSpec(memory_space=pl.ANY),
                      pl.BlockSpec((None, d, tf), lambda i, f, te, nu: (te[last(i, nu)], 0, fsel(i, f, nu))),
                      pl.BlockSpec((None, d, tf), lambda i, f, te, nu: (te[last(i, nu)], 0, fsel(i, f, nu))),
                      pl.BlockSpec((None, tf, d), lambda i, f, te, nu: (te[last(i, nu)], fsel(i, f, nu), 0))],
            out_specs=pl.BlockSpec((tile, d), lambda i, f, te, nu: (i, 0)),
            scratch_shapes=[pltpu.VMEM((2, tile, d), F32), pltpu.SemaphoreType.DMA((2,))]),
        out_shape=jax.ShapeDtypeStruct((n_rows, d), F32),
        compiler_params=_cparams(("arbitrary", "arbitrary")),
        name="moe_gemm",
    )(tile_expert, n_used, tokens, tokens, x_flat, w1, w3, w2)


def _moe_combine_kernel(pos_ref, posn_ref, route_ref, x1_ref, modx_ref, modz_ref, g2_ref, b2_ref, y_hbm,
                        o_ref, buf_ref, sem, *, nzb, nblk, alpha, tm):
    i = pl.program_id(0)
    slot = i % 2

    def row_copy(p_ref, r, k, s):
        return pltpu.make_async_copy(y_hbm.at[pl.ds(p_ref[0, k * tm + r], 1)],
                                     buf_ref.at[s, pl.ds(k * tm + r, 1)], sem.at[s])

    def issue(p_ref, s):
        def body(r, carry):
            row_copy(p_ref, r, 0, s).start()
            row_copy(p_ref, r, 1, s).start()
            return carry
        lax.fori_loop(0, tm, body, 0, unroll=8)

    @pl.when(i == 0)
    def _():
        issue(pos_ref, 0)

    @pl.when(i + 1 < pl.num_programs(0))
    def _():
        issue(posn_ref, 1 - slot)

    pltpu.make_async_copy(y_hbm.at[pl.ds(0, 2 * tm)], buf_ref.at[slot], sem.at[slot]).wait()
    route = route_ref[...]
    f = route[:, 0:1] * buf_ref[slot, 0:tm] + route[:, 1:2] * buf_ref[slot, tm:2 * tm]
    is_ctx = (i % nblk) < nzb
    mod = jnp.where(is_ctx, modz_ref[...], modx_ref[...])
    o_ref[...] = _ln(alpha * x1_ref[...] + mod[5:6] * f) * g2_ref[...] + b2_ref[...]


def _moe_combine(ys, pos, route, x1, mod, ln_g, ln_b, n_ctx_tokens, alpha):
    bsz, s, d = x1.shape
    tm = TOKEN_BLOCK
    nblk = s // tm
    n_steps = bsz * nblk
    nrow = mod.shape[0]
    pos_steps = pos.reshape(n_steps, tm, 2).transpose(0, 2, 1).reshape(n_steps, 1, 2 * tm)
    tok = lambda width: pl.BlockSpec((tm, width), lambda i: (i, 0))
    smem = lambda imap: pl.BlockSpec((None, 1, 2 * tm), imap, memory_space=pltpu.SMEM)
    vec = lambda a: a.reshape(1, -1).astype(F32)
    out = pl.pallas_call(
        functools.partial(_moe_combine_kernel, nzb=n_ctx_tokens // tm, nblk=nblk, alpha=alpha, tm=tm),
        grid=(n_steps,),
        in_specs=[smem(lambda i: (i, 0, 0)),
                  smem(lambda i: (jnp.minimum(i + 1, n_steps - 1), 0, 0)),
                  tok(LANES), tok(d),
                  pl.BlockSpec((None, 6, d), lambda i: (i // nblk, 0, 0)),
                  pl.BlockSpec((None, 6, d), lambda i: (nrow - 1, 0, 0)),
                  pl.BlockSpec((1, d), lambda i: (0, 0)), pl.BlockSpec((1, d), lambda i: (0, 0)),
                  pl.BlockSpec(memory_space=pl.ANY)],
        out_specs=tok(d),
        out_shape=jax.ShapeDtypeStruct((bsz * s, d), F32),
        scratch_shapes=[pltpu.VMEM((2, 2 * tm, d), F32), pltpu.SemaphoreType.DMA((2,))],
        compiler_params=_cparams(("arbitrary",)),
        name="moe_combine",
    )(pos_steps, pos_steps, route.reshape(bsz * s, LANES), x1.reshape(bsz * s, d), mod, mod,
      vec(ln_g), vec(ln_b), ys)
    return out.reshape(bsz, s, d)


def _moe(fx, x1, route, mod, w1, w3, w2, ln_g, ln_b, n_ctx_tokens, alpha):
    bsz, s, d = x1.shape
    n_exp = w1.shape[0]
    idx = route.reshape(bsz * s, LANES)[:, 2:4].astype(jnp.int32)
    row_token, pos, tile_expert, n_used = _route_plan(idx[:, 0], idx[:, 1], n_exp, MOE_TILE)
    ys = _moe_gemm(fx.reshape(bsz * s, d), row_token, tile_expert, n_used, w1, w3, w2, MOE_TILE)
    return _moe_combine(ys, pos, route, x1, mod, ln_g, ln_b, n_ctx_tokens, alpha)


def _pick_chunk(total, target):
    best = LANES
    for c in range(LANES, target + 1, LANES):
        if total % c == 0:
            best = c
    return best


def kernel(x, c, ctx, c_ctx, w_mod, b_mod, w_in, s5_lam_re, s5_lam_im, s5_log_step, s5_b_re, s5_b_im,
           s5_c_re, s5_c_im, s5_d, s5_w_glu, s5_b_glu, attn_sink, ret_log_gamma, w_out,
           ln1_g, ln1_b, ln2_g, ln2_b, ffn_w1, ffn_w3, ffn_w2, moe_router, moe_w1, moe_w3, moe_w2):
    bsz, t_len, d = x.shape
    n_ctx = ctx.shape[1]
    depth = w_in.shape[0]
    alpha = (2 * depth) ** 0.25
    s5_w = s5_d.shape[1]
    att_w = attn_sink.shape[1] * HEAD_DIM
    kv_w = att_w // ATT_REP
    ret_w = ret_log_gamma.shape[2] * HEAD_DIM
    sizes = (s5_w, att_w, kv_w, kv_w, ret_w, ret_w, ret_w, ret_w)
    assert sum(sizes) == w_in.shape[2] and s5_w + att_w + ret_w == w_out.shape[1]
    assert n_ctx % TOKEN_BLOCK == 0 and t_len % TOKEN_BLOCK == 0 and t_len >= 3 * ATT_BLOCK

    pad = (-(bsz + 1)) % 8
    cvec = jnp.concatenate([jnp.zeros((pad, d), F32), c_ctx[None].astype(F32)], axis=0)
    cvec = jnp.concatenate([c.astype(F32), cvec], axis=0)
    mod_all = _modulation(cvec, w_mod.astype(F32), b_mod.astype(F32)).reshape(depth, bsz + pad + 1, 6, d)

    tabs = _rope_tables(t_len, n_ctx)
    xz = jnp.concatenate([ctx, x], axis=1).astype(F32)
    for l in range(depth):
        need_ctx = l < depth - 1
        mod = mod_all[l]
        u, qa, va, qr, vr, gr, ka, kr = _inproj(xz, mod, w_in[l].astype(BF16), tabs, sizes, n_ctx)
        s5w = _s5_weights(s5_lam_re[l], s5_lam_im[l], s5_log_step[l], s5_b_re[l], s5_b_im[l],
                          s5_c_re[l], s5_c_im[l], S5_CHUNK)
        y_s5 = _s5_scan(u, s5w, n_ctx)
        o_att = _attention(qa, ka, va, attn_sink[l], n_ctx, need_ctx)
        o_ret = _retention(qr, kr, vr, gr, ret_log_gamma[l], n_ctx)
        i = l // 2
        router = None if l % 2 == 0 else moe_router[i]
        outs = _outproj(y_s5, u, o_att, o_ret, xz, mod, s5_d[l], s5_w_glu[l], s5_b_glu[l], w_out[l],
                        ln1_g[l], ln1_b[l], router, n_ctx, need_ctx, alpha)
        ctx_tokens = n_ctx if need_ctx else 0
        if l % 2 == 0:
            x1, fx = outs
            xz = _ffn(fx, x1, mod, ffn_w1[i].astype(BF16), ffn_w3[i].astype(BF16), ffn_w2[i].astype(BF16),
                      ln2_g[l], ln2_b[l], ctx_tokens, alpha)
        else:
            x1, fx, route = outs
            xz = _moe(fx, x1, route, mod, moe_w1[i].astype(BF16), moe_w3[i].astype(BF16),
                      moe_w2[i].astype(BF16), ln2_g[l], ln2_b[l], ctx_tokens, alpha)
    return xz if xz.shape[1] == t_len else xz[:, n_ctx:]
```

```python
import functools
import math

import jax
import jax.numpy as jnp
import numpy as np
from jax import lax
from jax.experimental import pallas as pl
from jax.experimental.pallas import tpu as pltpu

F32 = jnp.float32
BF16 = jnp.bfloat16

GRID_W = 64
HEAD_DIM = 64
S5_GROUP = 16
ATT_REP = 4
WINDOW = 128
ATT_BLOCK = 128
RET_CHUNK = 128
TOP_K = 2
LN_EPS = 1e-5
ROPE_BASE = 10000.0
NEG_INF = -1e30

LANES = 128
S5_CHUNK = LANES
TOKEN_BLOCK = 256
INPROJ_BLOCK = 768
FFN_BLOCK = 768
VMEM_LIMIT = 56 * 1024 * 1024


def _cparams(sem):
    return pltpu.CompilerParams(dimension_semantics=sem, vmem_limit_bytes=VMEM_LIMIT)


def _dot(a, b):
    return jnp.dot(a, b, preferred_element_type=F32)


def _dot_nt(a, b):
    return lax.dot_general(a, b, (((1,), (1,)), ((), ())), preferred_element_type=F32)


def _dot_tn(a, b):
    return lax.dot_general(a, b, (((0,), (0,)), ((), ())), preferred_element_type=F32)


def _split_bf16(a):
    hi = a.astype(BF16)
    lo = (a - hi.astype(F32)).astype(BF16)
    return hi, lo


def _dot3(a, b):
    ah, al = _split_bf16(a)
    bh, bl = _split_bf16(b)
    return _dot(ah, bh) + (_dot(ah, bl) + _dot(al, bh))


def _ln(x):
    mu = jnp.mean(x, axis=-1, keepdims=True)
    xc = x - mu
    var = jnp.mean(xc * xc, axis=-1, keepdims=True)
    return xc * lax.rsqrt(var + LN_EPS)


def _silu(x):
    return x * (1.0 / (1.0 + jnp.exp(-x)))


def _sigmoid(x):
    return 1.0 / (1.0 + jnp.exp(-x))


def _gelu_tanh(x):
    c = math.sqrt(2.0 / math.pi)
    return 0.5 * x * (1.0 + jnp.tanh(c * (x + 0.044715 * (x * x * x))))


def _mod_kernel(c_ref, w_ref, b_ref, o_ref):
    o_ref[...] = _dot3(_silu(c_ref[...]), w_ref[...]) + b_ref[...]


def _modulation(cvec, w_mod, b_mod):
    depth, d, n = w_mod.shape
    rows = cvec.shape[0]
    tn = 1536
    return pl.pallas_call(
        _mod_kernel,
        grid=(depth, n // tn),
        in_specs=[pl.BlockSpec((rows, d), lambda l, j: (0, 0)),
                  pl.BlockSpec((None, d, tn), lambda l, j: (l, 0, j)),
                  pl.BlockSpec((None, 1, tn), lambda l, j: (l, 0, j))],
        out_specs=pl.BlockSpec((None, rows, tn), lambda l, j: (l, 0, j)),
        out_shape=jax.ShapeDtypeStruct((depth, rows, n), F32),
        compiler_params=_cparams(("parallel", "parallel")),
        name="modulation",
    )(cvec, w_mod, b_mod.reshape(depth, 1, n))


def _rope_slab(xs, cos, sa, sb, half):
    return xs * cos + pltpu.roll(xs, LANES - half, 1) * sa + pltpu.roll(xs, half, 1) * sb


def _inproj_kernel(x_ref, modx_ref, modz_ref, wc_ref, w_ref, act_ref, ast_ref, rct_ref, rst_ref,
                   ac_ref, asa_ref, asb_ref, rc_ref, rsa_ref, rsb_ref,
                   ut_ref, qat_ref, vat_ref, qrt_ref, vrt_ref, grt_ref, ka_ref, kr_ref, *, n_ctx, c_sizes, sizes):
    tm = x_ref.shape[0]
    is_ctx = pl.program_id(1) * tm + lax.broadcasted_iota(jnp.int32, (tm, 1), 0) < n_ctx
    shift = jnp.where(is_ctx, modz_ref[0:1, :], modx_ref[0:1, :])
    gain = 1.0 + jnp.where(is_ctx, modz_ref[1:2, :], modx_ref[1:2, :])
    h = (_ln(x_ref[...]) * gain + shift).astype(BF16)
    scale = HEAD_DIM ** -0.5

    ct = _dot_nt(wc_ref[...], h)
    c_offs = [int(o) for o in np.concatenate([[0], np.cumsum(c_sizes)])]

    def chan(i):
        return ct[c_offs[i]:c_offs[i + 1]]

    def rope_t(x, cos_ref, sin_ref, part, mul):
        cos_t, sin_t = cos_ref[...], sin_ref[...]
        heads = []
        for hd in range(x.shape[0] // HEAD_DIM):
            xh = x[hd * HEAD_DIM:(hd + 1) * HEAD_DIM]
            pieces = []
            for lo in range(0, HEAD_DIM, 2 * part):
                pieces += [xh[lo + part:lo + 2 * part], xh[lo:lo + part]]
            out = xh * cos_t + jnp.concatenate(pieces, axis=0) * sin_t
            heads.append(out * mul if mul != 1.0 else out)
        return jnp.concatenate(heads, axis=0)

    def put(ref, val):
        for k in range(ref.shape[0]):
            ref[k] = val[:, k * ATT_BLOCK:(k + 1) * ATT_BLOCK].astype(ref.dtype)

    ut_ref[...] = chan(0)
    put(qat_ref, rope_t(chan(1), act_ref, ast_ref, HEAD_DIM // 4, scale))
    put(vat_ref, chan(2))
    put(qrt_ref, rope_t(chan(3), rct_ref, rst_ref, HEAD_DIM // 2, 1.0))
    put(vrt_ref, chan(4))
    put(grt_ref, chan(5))

    p = _dot(h, w_ref[...])
    ac, asa, asb = ac_ref[...], asa_ref[...], asb_ref[...]
    rc, rsa, rsb = rc_ref[...], rsa_ref[...], rsb_ref[...]
    offs = np.concatenate([[0], np.cumsum(sizes)])

    def rope_cols(i, cos, sa, sb, half, mul):
        blk = p[:, int(offs[i]):int(offs[i + 1])]
        slabs = [_rope_slab(blk[:, s:s + LANES], cos, sa, sb, half) for s in range(0, blk.shape[1], LANES)]
        out = slabs[0] if len(slabs) == 1 else jnp.concatenate(slabs, axis=1)
        return out * mul if mul != 1.0 else out

    ka_ref[...] = rope_cols(0, ac, asa, asb, HEAD_DIM // 4, 1.0).astype(ka_ref.dtype)
    kr_ref[...] = rope_cols(1, rc, rsa, rsb, HEAD_DIM // 2, scale)


def _inproj(xz, mod, w_in, tabs, sizes, n_ctx):
    bsz, s, d = xz.shape
    tm = _pick_chunk(s, INPROJ_BLOCK)
    assert tm % TOKEN_BLOCK == 0, "the output projection reads the S5 arrays in TOKEN_BLOCK pieces"
    nrow = mod.shape[0]
    offs = np.concatenate([[0], np.cumsum(sizes)])
    col = lambda i: w_in[:, int(offs[i]):int(offs[i + 1])]
    c_idx, r_idx = (0, 1, 3, 4, 6, 7), (2, 5)
    c_dtypes = (BF16, BF16, F32, BF16, F32)
    r_dtypes = (BF16, F32)
    c_sizes = tuple(sizes[i] for i in c_idx)
    r_sizes = tuple(sizes[i] for i in r_idx)
    w_c_t = jnp.concatenate([col(i) for i in c_idx], axis=1).T
    w_rest = jnp.concatenate([col(i) for i in r_idx], axis=1)
    sub = tm // ATT_BLOCK
    tok = lambda width: pl.BlockSpec((None, tm, width), lambda b, j: (b, j, 0))
    chan = lambda width: pl.BlockSpec((sub, None, width, ATT_BLOCK), lambda b, j: (j, b, 0, 0))
    tab = pl.BlockSpec((tm, LANES), lambda b, j: (j, 0))
    tab_t = pl.BlockSpec((HEAD_DIM, tm), lambda b, j: (0, j))
    return pl.pallas_call(
        functools.partial(_inproj_kernel, n_ctx=n_ctx, c_sizes=c_sizes, sizes=r_sizes),
        grid=(bsz, s // tm),
        in_specs=[tok(d),
                  pl.BlockSpec((None, 6, d), lambda b, j: (b, 0, 0)),
                  pl.BlockSpec((None, 6, d), lambda b, j: (nrow - 1, 0, 0)),
                  pl.BlockSpec((sum(c_sizes), d), lambda b, j: (0, 0)),
                  pl.BlockSpec((d, sum(r_sizes)), lambda b, j: (0, 0)),
                  tab_t, tab_t, tab_t, tab_t, tab, tab, tab, tab, tab, tab],
        out_specs=[pl.BlockSpec((None, None, c_sizes[0], tm), lambda b, j: (j, b, 0, 0))]
                  + [chan(w) for w in c_sizes[1:]] + [tok(w) for w in r_sizes],
        out_shape=[jax.ShapeDtypeStruct((s // tm, bsz, c_sizes[0], tm), F32)]
                  + [jax.ShapeDtypeStruct((s // ATT_BLOCK, bsz, w, ATT_BLOCK), dt)
                     for w, dt in zip(c_sizes[1:], c_dtypes)]
                  + [jax.ShapeDtypeStruct((bsz, s, w), dt) for w, dt in zip(r_sizes, r_dtypes)],
        compiler_params=_cparams(("parallel", "parallel")),
        name="inproj",
    )(xz, mod, mod, w_c_t, w_rest, *tabs)


def _rope_tables(t_len, n_ctx):
    t = jnp.arange(t_len)
    rows = (t // GRID_W).astype(F32)
    cols = (t % GRID_W).astype(F32)
    pos = t.astype(F32)

    def angles(p, dim):
        inv_freq = ROPE_BASE ** (-jnp.arange(0, dim, 2, dtype=F32) / dim)
        return p[:, None] * inv_freq[None, :]

    def head_tables(angs):
        cos = jnp.concatenate([jnp.concatenate([jnp.cos(a), jnp.cos(a)], -1) for a in angs], -1)
        sa = jnp.concatenate([jnp.concatenate([-jnp.sin(a), jnp.zeros_like(a)], -1) for a in angs], -1)
        sb = jnp.concatenate([jnp.concatenate([jnp.zeros_like(a), jnp.sin(a)], -1) for a in angs], -1)
        rep = LANES // HEAD_DIM
        return tuple(jnp.tile(x, (1, rep)) for x in (cos, sa, sb))

    att = head_tables([angles(rows, HEAD_DIM // 2), angles(cols, HEAD_DIM // 2)])
    ret = head_tables([angles(pos, HEAD_DIM)])
    signed_t = lambda tb: (tb[0][:, :HEAD_DIM].T, (tb[1] + tb[2])[:, :HEAD_DIM].T)

    def with_ctx(tb, axis, is_cos):
        shape = list(tb.shape)
        shape[axis] = n_ctx
        lead = jnp.ones(shape, F32) if is_cos else jnp.zeros(shape, F32)
        return jnp.concatenate([lead, tb], axis=axis)

    chan = signed_t(att) + signed_t(ret)
    toks = att + ret
    return (tuple(with_ctx(tb, 1, k % 2 == 0) for k, tb in enumerate(chan))
            + tuple(with_ctx(tb, 0, k % 3 == 0) for k, tb in enumerate(toks)))


def _s5_kernel(u_ref, tab_ref, wsf_ref, wsb_ref, wrf_ref, wrb_ref, lam_ref, y_ref,
               lhs_ref, m_ref, acc_ref, cf_ref, cb_ref, pf_ref, pb_ref, *, bsz, nz_chunks):
    r_blk, ch, tm = u_ref.shape
    c = S5_CHUNK
    kpb = tm // c
    n_chunks = (r_blk // bsz) * kpb

    for i in range(ch):
        x = u_ref[:, i, :].astype(BF16)
        p, half = divmod(i, 2)
        for k in range(kpb):
            lhs_ref[p, k * r_blk:(k + 1) * r_blk, half * c:(half + 1) * c] = x[:, k * c:(k + 1) * c]
    acc_ref[...] = jnp.zeros_like(acc_ref)
    cf_ref[...] = jnp.zeros_like(cf_ref)
    cb_ref[...] = jnp.zeros_like(cb_ref)

    def pair(p, carry):
        for ih in range(2):
            for o in range(ch):
                lag_row = jnp.broadcast_to(tab_ref[2 * p + ih, o:o + 1, :], (8, 2 * c))
                tab = pltpu.roll(lag_row, 0, 1, stride=1, stride_axis=0)
                for r2 in range(c // 16):
                    lo = c - 16 * r2
                    rows = jnp.concatenate([tab[:, lo:lo + c], tab[:, lo - 8:lo - 8 + c]], axis=0)
                    m_ref[ih * c + 16 * r2:ih * c + 16 * r2 + 16, o * c:(o + 1) * c] = rows.astype(BF16)
        lhs = lhs_ref[p]
        acc_ref[...] += _dot(lhs, m_ref[...])
        cf_ref[...] += _dot(lhs, wsf_ref[p])
        cb_ref[...] += _dot(lhs, wsb_ref[p])
        return carry

    lax.fori_loop(0, ch // 2, pair, 0)
    lam = lam_ref[...]
    nst = lam.shape[1] // 2

    def advance(s, a, bc, drive):
        return s * a + pltpu.roll(s, nst, 1) * bc + drive

    def sweep(order, c_ref, p_ref, a, bc):
        s = jnp.zeros((bsz, lam.shape[1]), F32)
        for n in order:
            j, k = divmod(n, kpb)
            rows = slice(k * r_blk + j * bsz, k * r_blk + (j + 1) * bsz)
            p_ref[rows, :] = s
            s = advance(s, a, bc, c_ref[rows, :])

    sweep(list(range(n_chunks)), cf_ref, pf_ref, lam[0:1], lam[1:2])
    order_b = list(range(nz_chunks - 1, -1, -1)) + list(range(n_chunks - 1, nz_chunks - 1, -1))
    sweep(order_b, cb_ref, pb_ref, lam[2:3], lam[3:4])
    acc_ref[...] += (_dot(pf_ref[...].astype(BF16), wrf_ref[...])
                     + _dot(pb_ref[...].astype(BF16), wrb_ref[...]))
    for o in range(ch):
        for k in range(kpb):
            y_ref[:, o, k * c:(k + 1) * c] = acc_ref[k * r_blk:(k + 1) * r_blk, o * c:(o + 1) * c]


def _s5_weights(lam_re, lam_im, log_step, b_re, b_im, c_re, c_im, chunk):
    hp = lax.Precision.HIGHEST
    lam = lax.complex(lam_re.astype(F32), lam_im.astype(F32))
    lam_dt = lam * jnp.exp(log_step.astype(F32))[..., None]
    lam_bar = jnp.exp(lam_dt)
    b_bar = lax.complex(b_re.astype(F32), b_im.astype(F32)) * ((lam_bar - 1.0) / lam)[..., None]
    c_mat = lax.complex(c_re.astype(F32), c_im.astype(F32))
    g, n = lam.shape[1], lam.shape[2]
    ch = b_bar.shape[-1]
    steps = jnp.arange(chunk + 1, dtype=F32)
    pw = jnp.exp(steps[None, :, None, None] * lam_dt[:, None])
    kern = jnp.einsum('zgon,zdgn,zgni->zgdoi', c_mat, pw[:, :chunk], b_bar, precision=hp).real
    zero_lag = kern[0, :, 0] + kern[1, :, 0]
    lag_table = jnp.concatenate([jnp.zeros_like(zero_lag)[:, None], kern[1, :, :0:-1], zero_lag[:, None],
                                 kern[0, :, 1:]], axis=1).transpose(0, 3, 2, 1)

    def state_in(pw_s, b_dir):
        w = pw_s.transpose(1, 0, 2)[:, None] * b_dir.transpose(0, 2, 1)[:, :, None]
        w = jnp.concatenate([w.real, w.imag], axis=3)
        return w.reshape(g, ch // 2, 2 * chunk, 2 * n)

    def state_out(pw_t, c_dir):
        w = c_dir.transpose(0, 2, 1)[:, :, :, None] * pw_t.transpose(1, 2, 0)[:, :, None]
        w = jnp.concatenate([w.real, -w.imag], axis=1)
        return w.reshape(g, 2 * n, ch * chunk)

    wsf = state_in(pw[0, chunk - 1::-1][:chunk], b_bar[0])
    wsb = state_in(pw[1, :chunk], b_bar[1])
    wrf = state_out(pw[0, 1:chunk + 1], c_mat[0])
    wrb = state_out(pw[1, chunk:0:-1], c_mat[1])
    lam_c = pw[:, chunk]
    rows = []
    for z in range(2):
        rows.append(jnp.concatenate([lam_c[z].real, lam_c[z].real], -1))
        rows.append(jnp.concatenate([-lam_c[z].imag, lam_c[z].imag], -1))
    lam_rows = jnp.stack(rows + [jnp.zeros_like(rows[0])] * 4, axis=1)
    return lag_table, wsf.astype(BF16), wsb.astype(BF16), wrf.astype(BF16), wrb.astype(BF16), lam_rows


def _s5_scan(ut, weights, n_ctx):
    nblk, bsz, width, tm = ut.shape
    tab, wsf, wsb, wrf, wrb, lam_rows = weights
    g, ch = tab.shape[0], tab.shape[1]
    c = S5_CHUNK
    nst2 = wsf.shape[3]
    r_blk = nblk * bsz
    rows = r_blk * (tm // c)
    grp = lambda *shape: pl.BlockSpec((None,) + shape, lambda i: (i,) + (0,) * len(shape))
    tok = pl.BlockSpec((r_blk, ch, tm), lambda i: (0, i, 0))
    y = pl.pallas_call(
        functools.partial(_s5_kernel, bsz=bsz, nz_chunks=n_ctx // c),
        grid=(g,),
        in_specs=[tok, grp(ch, ch, 2 * c), grp(ch // 2, 2 * c, nst2), grp(ch // 2, 2 * c, nst2),
                  grp(nst2, ch * c), grp(nst2, ch * c), grp(8, nst2)],
        out_specs=tok,
        out_shape=jax.ShapeDtypeStruct((r_blk, width, tm), F32),
        scratch_shapes=[pltpu.VMEM((ch // 2, rows, 2 * c), BF16),
                        pltpu.VMEM((2 * c, ch * c), BF16),
                        pltpu.VMEM((rows, ch * c), F32)] + [pltpu.VMEM((rows, nst2), F32)] * 4,
        compiler_params=_cparams(("parallel",)),
        name="s5_scan",
    )(ut.reshape(r_blk, width, tm), tab, wsf, wsb, wrf, wrb, lam_rows)
    return y.reshape(nblk, bsz, width, tm)


def _attn_kernel(sink_ref, q_ref, k_ref, v_ref, o_ref, *, n_ctx, t_len, q_off, kv_heads):
    qi = pl.program_id(1) + q_off
    nzb = n_ctx // ATT_BLOCK
    band = 3 * ATT_BLOCK
    wide = ATT_REP * ATT_BLOCK
    q = q_ref[...].astype(BF16)
    kc = k_ref[0:n_ctx, :].astype(BF16)
    vc = jnp.concatenate([v_ref[p] for p in range(nzb)], axis=1).astype(BF16)

    def rows(a, i):
        return a[i * HEAD_DIM:(i + 1) * HEAD_DIM]

    def q_group(g):
        qg = jnp.concatenate([rows(q, g * ATT_REP + r) for r in range(ATT_REP)], axis=1)
        zero = jnp.zeros_like(qg)
        return jnp.concatenate([qg if gg == g else zero for gg in range(kv_heads)], axis=0)

    def finish(g, o, den):
        o = o / den
        for r in range(ATT_REP):
            h = g * ATT_REP + r
            o_ref[h * HEAD_DIM:(h + 1) * HEAD_DIM, :] = o[:, r * ATT_BLOCK:(r + 1) * ATT_BLOCK].astype(o_ref.dtype)

    @pl.when(qi < nzb)
    def _():
        for g in range(kv_heads):
            sink = sink_ref[g:g + 1, :]
            s = _dot(kc, q_group(g))
            m = jnp.maximum(jnp.max(s, axis=0, keepdims=True), sink)
            e = jnp.exp(s - m)
            den = jnp.sum(e, axis=0, keepdims=True) + jnp.exp(sink - m)
            finish(g, _dot(rows(vc, g), e.astype(BF16)), den)

    @pl.when(qi >= nzb)
    def _():
        n = qi - nzb
        start = jnp.clip((n - 1) * ATT_BLOCK, 0, t_len - band)
        kl = k_ref[pl.ds(pl.multiple_of(n_ctx + start, ATT_BLOCK), band), :].astype(BF16)
        p0 = (n_ctx + start) // ATT_BLOCK
        vl = jnp.concatenate([v_ref[p0 + t] for t in range(3)], axis=1).astype(BF16)
        k_pos = start + lax.broadcasted_iota(jnp.int32, (band, wide), 0)
        q_pos = n * ATT_BLOCK + (lax.broadcasted_iota(jnp.int32, (band, wide), 1) & (ATT_BLOCK - 1))
        valid = jnp.abs(k_pos - q_pos) <= WINDOW
        for g in range(kv_heads):
            sink = sink_ref[g:g + 1, :]
            qg = q_group(g)
            s_loc = jnp.where(valid, _dot(kl, qg), NEG_INF)
            s_ctx = _dot(kc, qg)
            m = jnp.maximum(jnp.maximum(jnp.max(s_loc, axis=0, keepdims=True),
                                        jnp.max(s_ctx, axis=0, keepdims=True)), sink)
            e_loc = jnp.exp(s_loc - m)
            e_ctx = jnp.exp(s_ctx - m)
            den = (jnp.sum(e_loc, axis=0, keepdims=True) + jnp.sum(e_ctx, axis=0, keepdims=True)
                   + jnp.exp(sink - m))
            o = _dot(rows(vl, g), e_loc.astype(BF16)) + _dot(rows(vc, g), e_ctx.astype(BF16))
            finish(g, o, den)


def _attention(qat, ka, vat, sink, n_ctx, need_ctx):
    nblk, bsz, qw, _ = qat.shape
    s, kvw = ka.shape[1], ka.shape[2]
    t_len = s - n_ctx
    kv_heads = kvw // HEAD_DIM
    q_off = 0 if need_ctx else n_ctx // ATT_BLOCK
    nq = nblk - q_off
    sink_rows = jnp.repeat(sink.astype(F32).reshape(kv_heads, ATT_REP), ATT_BLOCK, axis=1)
    return pl.pallas_call(
        functools.partial(_attn_kernel, n_ctx=n_ctx, t_len=t_len, q_off=q_off, kv_heads=kv_heads),
        grid=(bsz, nq),
        in_specs=[pl.BlockSpec(sink_rows.shape, lambda b, j: (0, 0)),
                  pl.BlockSpec((None, None, qw, ATT_BLOCK), lambda b, j: (j + q_off, b, 0, 0)),
                  pl.BlockSpec((None, s, kvw), lambda b, j: (b, 0, 0)),
                  pl.BlockSpec((nblk, None, kvw, ATT_BLOCK), lambda b, j: (0, b, 0, 0))],
        out_specs=pl.BlockSpec((None, None, qw, ATT_BLOCK), lambda b, j: (j, b, 0, 0)),
        out_shape=jax.ShapeDtypeStruct((nq, bsz, qw, ATT_BLOCK), BF16),
        compiler_params=_cparams(("parallel", "arbitrary")),
        name="window_attention",
    )(sink_rows, qat, ka, vat)


def _ret_kernel(lg_ref, q_ref, k_ref, v_ref, g_ref, o_ref, acc_ref, sf_ref, sb_ref, *, nz_chunks, heads):
    c = RET_CHUNK
    n_chunks, w, _ = q_ref.shape
    f32 = lambda a: a.astype(F32)

    def per_head(shape, axis, group, direction):
        owner = lax.broadcasted_iota(jnp.int32, shape, axis) // group
        out = jnp.zeros(shape, F32)
        for h in range(heads):
            out = jnp.where(owner == h, lg_ref[direction, h], out)
        return out

    row_i = f32(lax.broadcasted_iota(jnp.int32, (w, c), 1))
    qw_f = jnp.exp(per_head((w, c), 0, HEAD_DIM, 0) * (row_i + 1.0))
    qw_b = jnp.exp(per_head((w, c), 0, HEAD_DIM, 1) * (c - row_i))
    key_j = f32(lax.broadcasted_iota(jnp.int32, (c, w), 0))
    kw_f = jnp.exp(per_head((c, w), 1, HEAD_DIM, 0) * (c - 1.0 - key_j))
    kw_b = jnp.exp(per_head((c, w), 1, HEAD_DIM, 1) * key_j)
    dec_f = jnp.exp(per_head((w, 1), 0, HEAD_DIM, 0) * c)
    dec_b = jnp.exp(per_head((w, 1), 0, HEAD_DIM, 1) * c)
    wide = heads * c
    diff = f32((lax.broadcasted_iota(jnp.int32, (c, wide), 1) & (c - 1)) - lax.broadcasted_iota(jnp.int32, (c, wide), 0))
    dec_t = jnp.where(diff >= 0, jnp.exp(per_head((c, wide), 1, c, 0) * jnp.maximum(diff, 0.0)),
                      jnp.exp(per_head((c, wide), 1, c, 1) * jnp.maximum(-diff, 0.0)))
    own_wide = (lax.broadcasted_iota(jnp.int32, (w, wide), 0) // HEAD_DIM
                == lax.broadcasted_iota(jnp.int32, (w, wide), 1) // c)
    own_sq = f32(lax.broadcasted_iota(jnp.int32, (w, w), 0) // HEAD_DIM
                 == lax.broadcasted_iota(jnp.int32, (w, w), 1) // HEAD_DIM)
    sf_ref[...] = jnp.zeros_like(sf_ref)
    sb_ref[...] = jnp.zeros_like(sb_ref)

    def block_diag(x):
        return jnp.where(own_wide, jnp.concatenate([x] * heads, axis=1), 0.0).astype(BF16)

    def fwd(n, carry):
        qt, vt = q_ref[n], v_ref[n]
        k = k_ref[pl.ds(pl.multiple_of(n * c, c), c), :]
        scores_t = _dot(k.astype(BF16), block_diag(qt)) * dec_t
        stacked = jnp.concatenate([scores_t[:, h * c:(h + 1) * c] for h in range(heads)], axis=0)
        o = _dot(block_diag(vt), stacked.astype(BF16))
        s_prev = sf_ref[...]
        o = o + _dot(s_prev.astype(BF16), (qt * qw_f).astype(BF16))
        sf_ref[...] = dec_f * s_prev + own_sq * _dot(vt.astype(BF16), (k * kw_f).astype(BF16))
        acc_ref[n] = o
        return carry

    lax.fori_loop(0, n_chunks, fwd, 0)

    def bwd(i, carry):
        n = jnp.where(i < nz_chunks, nz_chunks - 1 - i, n_chunks - 1 - i + nz_chunks)
        qt, vt = q_ref[n], v_ref[n]
        k = k_ref[pl.ds(pl.multiple_of(n * c, c), c), :]
        s_prev = sb_ref[...]
        o = acc_ref[n] + _dot(s_prev.astype(BF16), (qt * qw_b).astype(BF16))
        sb_ref[...] = dec_b * s_prev + own_sq * _dot(vt.astype(BF16), (k * kw_b).astype(BF16))
        gate = _silu(g_ref[n])
        for h in range(heads):
            rows = slice(h * HEAD_DIM, (h + 1) * HEAD_DIM)
            oh = o[rows]
            mu = jnp.mean(oh, axis=0, keepdims=True)
            oc = oh - mu
            var = jnp.mean(oc * oc, axis=0, keepdims=True)
            o_ref[n, rows, :] = (oc * lax.rsqrt(var + LN_EPS) * gate[rows]).astype(o_ref.dtype)
        return carry

    lax.fori_loop(0, n_chunks, bwd, 0)


def _retention(qrt, kr, vrt, grt, log_gamma, n_ctx):
    n_chunks, bsz, w, c = qrt.shape
    s = kr.shape[1]
    chan = pl.BlockSpec((n_chunks, None, w, c), lambda b: (0, b, 0, 0))
    return pl.pallas_call(
        functools.partial(_ret_kernel, nz_chunks=n_ctx // RET_CHUNK, heads=w // HEAD_DIM),
        grid=(bsz,),
        in_specs=[pl.BlockSpec(memory_space=pltpu.SMEM), chan, pl.BlockSpec((None, s, w), lambda b: (b, 0, 0)),
                  chan, chan],
        out_specs=chan,
        out_shape=jax.ShapeDtypeStruct((n_chunks, bsz, w, c), BF16),
        scratch_shapes=[pltpu.VMEM((n_chunks, w, c), F32), pltpu.VMEM((w, w), F32), pltpu.VMEM((w, w), F32)],
        compiler_params=_cparams(("parallel",)),
        name="retention",
    )(log_gamma.astype(F32), qrt, kr, vrt, grt)


def _outproj_kernel(*refs, nzb, alpha, with_router, w_s5, w_att, n_exp):
    if with_router:
        (y_ref, u_ref, a_ref, r_ref, x_ref, modx_ref, modz_ref, d_ref, wg_ref, bg_ref, wo_ref,
         g1_ref, b1_ref, rt_ref, x1_ref, fx_ref, gate_ref) = refs
    else:
        (y_ref, u_ref, a_ref, r_ref, x_ref, modx_ref, modz_ref, d_ref, wg_ref, bg_ref, wo_ref,
         g1_ref, b1_ref, x1_ref, fx_ref) = refs
    is_ctx = pl.program_id(1) < nzb
    mod = jnp.where(is_ctx, modz_ref[...], modx_ref[...])
    g = _gelu_tanh(y_ref[...] + d_ref[...] * u_ref[...])
    s5 = g * _sigmoid(_dot(wg_ref[...], g.astype(BF16)) + bg_ref[...])
    def chan_major(ref, lo, hi):
        return jnp.concatenate([_dot_tn(ref[k].astype(BF16), wo_ref[lo:hi, :]) for k in range(ref.shape[0])],
                               axis=0)

    mix = (_dot_tn(s5.astype(BF16), wo_ref[0:w_s5, :]) + chan_major(a_ref, w_s5, w_s5 + w_att)
           + chan_major(r_ref, w_s5 + w_att, wo_ref.shape[0]))
    x1 = _ln(alpha * x_ref[...] + mod[2:3] * mix) * g1_ref[...] + b1_ref[...]
    x1_ref[...] = x1
    fx = _ln(x1) * (1.0 + mod[4:5]) + mod[3:4]
    fx_ref[...] = fx.astype(fx_ref.dtype)
    if with_router:
        lane = lax.broadcasted_iota(jnp.int32, (fx.shape[0], LANES), 1)
        logits = jnp.where(lane < n_exp, _dot3(fx, rt_ref[...]), -jnp.inf)
        m1 = jnp.max(logits, axis=1, keepdims=True)
        i1 = jnp.min(jnp.where(logits == m1, lane, LANES), axis=1, keepdims=True)
        rest = jnp.where(lane == i1, -jnp.inf, logits)
        m2 = jnp.max(rest, axis=1, keepdims=True)
        i2 = jnp.min(jnp.where(rest == m2, lane, LANES), axis=1, keepdims=True)
        e2 = jnp.exp(m2 - m1)
        den = 1.0 + e2
        route = jnp.where(lane == 0, 1.0 / den, jnp.where(lane == 1, e2 / den, 0.0))
        route = jnp.where(lane == 2, i1.astype(F32), jnp.where(lane == 3, i2.astype(F32), route))
        gate_ref[...] = route


def _outproj(y_s5, u, o_att, o_ret, xz, mod, s5_d, w_glu, b_glu, w_out, ln_g, ln_b, router, n_ctx, need_ctx,
             alpha):
    bsz, s, d = xz.shape
    tm = TOKEN_BLOCK
    nzb = n_ctx // tm
    off = 0 if need_ctx else nzb
    nblk = s // tm - off
    s_out = nblk * tm
    nrow = mod.shape[0]
    w_s5, w_att = y_s5.shape[2], o_att.shape[2]
    sub = tm // ATT_BLOCK
    att_off = off if o_att.shape[0] == s // ATT_BLOCK else 0
    att = pl.BlockSpec((sub, None, w_att, ATT_BLOCK), lambda b, j: (j + att_off, b, 0, 0))

    def tok(width, shift):
        return pl.BlockSpec((None, tm, width), lambda b, j: (b, j + shift, 0))

    def full(a):
        return pl.BlockSpec(a.shape, lambda b, j: (0,) * a.ndim)

    per_in = y_s5.shape[3] // tm
    chan = pl.BlockSpec((None, None, w_s5, tm), lambda b, j: ((j + off) // per_in, b, 0, (j + off) % per_in))
    vec = lambda a: a.reshape(1, -1).astype(F32)
    col = lambda a: a.reshape(-1, 1).astype(F32)
    consts = [col(s5_d), w_glu.T.astype(BF16), col(b_glu), w_out.astype(BF16), vec(ln_g), vec(ln_b)]
    ret = pl.BlockSpec((sub, None, o_ret.shape[2], ATT_BLOCK), lambda b, j: (j + off, b, 0, 0))
    in_specs = [chan, chan, att, ret, tok(d, off),
                pl.BlockSpec((None, 6, d), lambda b, j: (b, 0, 0)),
                pl.BlockSpec((None, 6, d), lambda b, j: (nrow - 1, 0, 0))] + [full(a) for a in consts]
    out_specs = [tok(d, 0), tok(d, 0)]
    fx_dtype = BF16 if router is None else F32
    out_shape = [jax.ShapeDtypeStruct((bsz, s_out, d), F32), jax.ShapeDtypeStruct((bsz, s_out, d), fx_dtype)]
    args = [y_s5, u, o_att, o_ret, xz, mod, mod] + consts
    with_router = router is not None
    n_exp = 0
    if with_router:
        n_exp = router.shape[1]
        router_pad = jnp.pad(router.astype(F32), ((0, 0), (0, LANES - n_exp)))
        args.append(router_pad)
        in_specs.append(full(router_pad))
        out_specs.append(tok(LANES, 0))
        out_shape.append(jax.ShapeDtypeStruct((bsz, s_out, LANES), F32))
    return pl.pallas_call(
        functools.partial(_outproj_kernel, nzb=nzb - off, alpha=alpha, with_router=with_router,
                          w_s5=w_s5, w_att=w_att, n_exp=n_exp),
        grid=(bsz, nblk),
        in_specs=in_specs,
        out_specs=out_specs,
        out_shape=out_shape,
        compiler_params=_cparams(("parallel", "parallel")),
        name="outproj",
    )(*args)


def _swiglu_into(x_bf16, w1_ref, w3_ref, w2_ref, acc_ref, fc):
    for s in range(0, w1_ref.shape[1], fc):
        h1 = _dot(x_bf16, w1_ref[:, s:s + fc])
        h3 = _dot(x_bf16, w3_ref[:, s:s + fc])
        acc_ref[...] += _dot((_silu(h1) * h3).astype(BF16), w2_ref[s:s + fc, :])


def _ffn_kernel(fx_ref, x1_ref, modx_ref, modz_ref, w1_ref, w3_ref, w2_ref, g2_ref, b2_ref,
                o_ref, acc_ref, *, n_ctx_tokens, alpha, fc):
    tm = acc_ref.shape[0]
    acc_ref[...] = jnp.zeros_like(acc_ref)
    _swiglu_into(fx_ref[...], w1_ref, w3_ref, w2_ref, acc_ref, fc)
    row = pl.program_id(1) * tm + lax.broadcasted_iota(jnp.int32, (tm, 1), 0)
    gate = jnp.where(row < n_ctx_tokens, modz_ref[5:6, :], modx_ref[5:6, :])
    o_ref[...] = _ln(alpha * x1_ref[...] + gate * acc_ref[...]) * g2_ref[...] + b2_ref[...]


def _ffn(fx, x1, mod, w1, w3, w2, ln_g, ln_b, n_ctx_tokens, alpha):
    bsz, s, d = x1.shape
    ff = w1.shape[1]
    tm = _pick_chunk(s, FFN_BLOCK)
    nrow = mod.shape[0]
    tok = pl.BlockSpec((None, tm, d), lambda b, j: (b, j, 0))
    full = lambda a: pl.BlockSpec(a.shape, lambda b, j: (0,) * a.ndim, pipeline_mode=pl.Buffered(1))
    vec = lambda a: a.reshape(1, -1).astype(F32)
    return pl.pallas_call(
        functools.partial(_ffn_kernel, n_ctx_tokens=n_ctx_tokens, alpha=alpha, fc=_pick_chunk(ff, 512)),
        grid=(bsz, s // tm),
        in_specs=[tok, tok,
                  pl.BlockSpec((None, 6, d), lambda b, j: (b, 0, 0)),
                  pl.BlockSpec((None, 6, d), lambda b, j: (nrow - 1, 0, 0)),
                  full(w1), full(w3), full(w2),
                  pl.BlockSpec((1, d), lambda b, j: (0, 0)), pl.BlockSpec((1, d), lambda b, j: (0, 0))],
        out_specs=tok,
        out_shape=jax.ShapeDtypeStruct((bsz, s, d), F32),
        scratch_shapes=[pltpu.VMEM((tm, d), F32)],
        compiler_params=_cparams(("parallel", "parallel")),
        name="dense_ffn",
    )(fx, x1, mod, mod, w1, w3, w2, vec(ln_g), vec(ln_b))


MOE_TILE = 1024


def _route_plan(e1, e2, n_exp, tile):
    n = e1.shape[0]
    pair_e = jnp.stack([e1, e2], axis=1).reshape(-1)
    blk = LANES * 8
    pad = (-2 * n) % blk
    onehot_t = (jnp.arange(n_exp, dtype=jnp.int32)[:, None] == jnp.pad(pair_e, (0, pad), constant_values=-1)[None, :])
    blocks = onehot_t.reshape(n_exp, -1, blk).astype(BF16)
    upper = (jnp.arange(blk)[:, None] < jnp.arange(blk)[None, :]).astype(BF16)
    within = jnp.einsum('ebj,jk->ebk', blocks, upper, preferred_element_type=F32)
    totals = jnp.sum(blocks.astype(F32), axis=2)
    carried = jnp.cumsum(totals, axis=1) - totals
    before_t = (within + carried[:, :, None]).reshape(n_exp, -1)[:, :2 * n].astype(jnp.int32)
    rank = jnp.sum(jnp.where(onehot_t[:, :2 * n], before_t, 0), axis=0)
    counts = jnp.sum(totals, axis=1).astype(jnp.int32)
    padded = (counts + tile - 1) // tile * tile
    ends = jnp.cumsum(padded)
    starts = ends - padded
    dest = starts[pair_e] + rank
    n_rows = (2 * n + n_exp * (tile - 1)) // tile * tile
    n_tiles = n_rows // tile
    tile_start = jnp.arange(n_tiles, dtype=jnp.int32) * tile
    tile_expert = jnp.minimum(jnp.sum((tile_start[:, None] >= ends[None, :]).astype(jnp.int32), axis=1),
                              n_exp - 1)
    by_expert = jnp.sort(pair_e * (2 * n) + jnp.arange(2 * n, dtype=jnp.int32)) % (2 * n)
    row = jnp.arange(n_rows, dtype=jnp.int32)
    row_e = jnp.repeat(tile_expert, tile)
    in_group = row - starts[row_e]
    src = jnp.minimum(in_group + (jnp.cumsum(counts) - counts)[row_e], 2 * n - 1)
    row_token = jnp.where(in_group < counts[row_e], by_expert[src] // 2, 0)
    n_used = (ends[-1] // tile).astype(jnp.int32).reshape(1)
    return row_token, dest.reshape(n, 2), tile_expert, n_used


def _moe_gemm_kernel(te_ref, nused_ref, tok_ref, tokn_ref, x_hbm, w1_ref, w3_ref, w2_ref, y_ref,
                     xbuf_ref, sem, *, fc, tile):
    i = pl.program_id(0)
    f = pl.program_id(1)
    slot = i % 2
    n_used = nused_ref[0]

    def row_copy(t_ref, r, s):
        return pltpu.make_async_copy(x_hbm.at[pl.ds(t_ref[0, r], 1)], xbuf_ref.at[s, pl.ds(r, 1)], sem.at[s])

    def issue(t_ref, s):
        def body(it, carry):
            base = pl.multiple_of(it * 8, 8)
            for k in range(8):
                row_copy(t_ref, base + k, s).start()
            return carry
        lax.fori_loop(0, tile // 8, body, 0)

    @pl.when(f == 0)
    def _():
        y_ref[...] = jnp.zeros_like(y_ref)

        @pl.when(i == 0)
        def _():
            issue(tok_ref, 0)

        @pl.when(i + 1 < n_used)
        def _():
            issue(tokn_ref, 1 - slot)

        @pl.when(i < n_used)
        def _():
            pltpu.make_async_copy(x_hbm.at[pl.ds(0, tile)], xbuf_ref.at[slot], sem.at[slot]).wait()

    @pl.when(i < n_used)
    def _():
        _swiglu_into(xbuf_ref[slot].astype(BF16), w1_ref, w3_ref, w2_ref, y_ref, fc)


def _moe_gemm(x_flat, row_token, tile_expert, n_used, w1, w3, w2, tile):
    n_rows = row_token.shape[0]
    n_tiles = n_rows // tile
    d = x_flat.shape[1]
    ff = w1.shape[2]
    nf = 2 if ff % (2 * LANES) == 0 else 1
    tf = ff // nf
    last = lambda i, nu: jnp.minimum(i, nu[0] - 1)
    fsel = lambda i, f, nu: jnp.where(i < nu[0], f, nf - 1)
    tok = lambda imap: pl.BlockSpec((None, 1, tile), imap, memory_space=pltpu.SMEM)
    tokens = row_token.reshape(n_tiles, 1, tile)
    return pl.pallas_call(
        functools.partial(_moe_gemm_kernel, fc=_pick_chunk(tf, 256), tile=tile),
        grid_spec=pltpu.PrefetchScalarGridSpec(
            num_scalar_prefetch=2,
            grid=(n_tiles, nf),
            in_specs=[tok(lambda i, f, te, nu: (i, 0, 0)),
                      tok(lambda i, f, te, nu: (jnp.minimum(i + 1, n_tiles - 1), 0, 0)),
                      pl.BlockSpec(memory_space=pl.ANY),
                      pl.BlockSpec((None, d, tf), lambda i, f, te, nu: (te[last(i, nu)], 0, fsel(i, f, nu))),
                      pl.BlockSpec((None, d, tf), lambda i, f, te, nu: (te[last(i, nu)], 0, fsel(i, f, nu))),
                      pl.BlockSpec((None, tf, d), lambda i, f, te, nu: (te[last(i, nu)], fsel(i, f, nu), 0))],
            out_specs=pl.BlockSpec((tile, d), lambda i, f, te, nu: (i, 0)),
            scratch_shapes=[pltpu.VMEM((2, tile, d), F32), pltpu.SemaphoreType.DMA((2,))]),
        out_shape=jax.ShapeDtypeStruct((n_rows, d), F32),
        compiler_params=_cparams(("arbitrary", "arbitrary")),
        name="moe_gemm",
    )(tile_expert, n_used, tokens, tokens, x_flat, w1, w3, w2)


def _moe_combine_kernel(pos_ref, posn_ref, route_ref, x1_ref, modx_ref, modz_ref, g2_ref, b2_ref, y_hbm,
                        o_ref, buf_ref, sem, *, nzb, nblk, alpha, tm):
    i = pl.program_id(0)
    slot = i % 2

    def row_copy(p_ref, r, k, s):
        return pltpu.make_async_copy(y_hbm.at[pl.ds(p_ref[0, k * tm + r], 1)],
                                     buf_ref.at[s, pl.ds(k * tm + r, 1)], sem.at[s])

    def issue(p_ref, s):
        def body(r, carry):
            row_copy(p_ref, r, 0, s).start()
            row_copy(p_ref, r, 1, s).start()
            return carry
        lax.fori_loop(0, tm, body, 0, unroll=8)

    @pl.when(i == 0)
    def _():
        issue(pos_ref, 0)

    @pl.when(i + 1 < pl.num_programs(0))
    def _():
        issue(posn_ref, 1 - slot)

    pltpu.make_async_copy(y_hbm.at[pl.ds(0, 2 * tm)], buf_ref.at[slot], sem.at[slot]).wait()
    route = route_ref[...]
    f = route[:, 0:1] * buf_ref[slot, 0:tm] + route[:, 1:2] * buf_ref[slot, tm:2 * tm]
    is_ctx = (i % nblk) < nzb
    mod = jnp.where(is_ctx, modz_ref[...], modx_ref[...])
    o_ref[...] = _ln(alpha * x1_ref[...] + mod[5:6] * f) * g2_ref[...] + b2_ref[...]


def _moe_combine(ys, pos, route, x1, mod, ln_g, ln_b, n_ctx_tokens, alpha):
    bsz, s, d = x1.shape
    tm = TOKEN_BLOCK
    nblk = s // tm
    n_steps = bsz * nblk
    nrow = mod.shape[0]
    pos_steps = pos.reshape(n_steps, tm, 2).transpose(0, 2, 1).reshape(n_steps, 1, 2 * tm)
    tok = lambda width: pl.BlockSpec((tm, width), lambda i: (i, 0))
    smem = lambda imap: pl.BlockSpec((None, 1, 2 * tm), imap, memory_space=pltpu.SMEM)
    vec = lambda a: a.reshape(1, -1).astype(F32)
    out = pl.pallas_call(
        functools.partial(_moe_combine_kernel, nzb=n_ctx_tokens // tm, nblk=nblk, alpha=alpha, tm=tm),
        grid=(n_steps,),
        in_specs=[smem(lambda i: (i, 0, 0)),
                  smem(lambda i: (jnp.minimum(i + 1, n_steps - 1), 0, 0)),
                  tok(LANES), tok(d),
                  pl.BlockSpec((None, 6, d), lambda i: (i // nblk, 0, 0)),
                  pl.BlockSpec((None, 6, d), lambda i: (nrow - 1, 0, 0)),
                  pl.BlockSpec((1, d), lambda i: (0, 0)), pl.BlockSpec((1, d), lambda i: (0, 0)),
                  pl.BlockSpec(memory_space=pl.ANY)],
        out_specs=tok(d),
        out_shape=jax.ShapeDtypeStruct((bsz * s, d), F32),
        scratch_shapes=[pltpu.VMEM((2, 2 * tm, d), F32), pltpu.SemaphoreType.DMA((2,))],
        compiler_params=_cparams(("arbitrary",)),
        name="moe_combine",
    )(pos_steps, pos_steps, route.reshape(bsz * s, LANES), x1.reshape(bsz * s, d), mod, mod,
      vec(ln_g), vec(ln_b), ys)
    return out.reshape(bsz, s, d)


def _moe(fx, x1, route, mod, w1, w3, w2, ln_g, ln_b, n_ctx_tokens, alpha):
    bsz, s, d = x1.shape
    n_exp = w1.shape[0]
    idx = route.reshape(bsz * s, LANES)[:, 2:4].astype(jnp.int32)
    row_token, pos, tile_expert, n_used = _route_plan(idx[:, 0], idx[:, 1], n_exp, MOE_TILE)
    ys = _moe_gemm(fx.reshape(bsz * s, d), row_token, tile_expert, n_used, w1, w3, w2, MOE_TILE)
    return _moe_combine(ys, pos, route, x1, mod, ln_g, ln_b, n_ctx_tokens, alpha)


def _pick_chunk(total, target):
    best = LANES
    for c in range(LANES, target + 1, LANES):
        if total % c == 0:
            best = c
    return best


def kernel(x, c, ctx, c_ctx, w_mod, b_mod, w_in, s5_lam_re, s5_lam_im, s5_log_step, s5_b_re, s5_b_im,
           s5_c_re, s5_c_im, s5_d, s5_w_glu, s5_b_glu, attn_sink, ret_log_gamma, w_out,
           ln1_g, ln1_b, ln2_g, ln2_b, ffn_w1, ffn_w3, ffn_w2, moe_router, moe_w1, moe_w3, moe_w2):
    bsz, t_len, d = x.shape
    n_ctx = ctx.shape[1]
    depth = w_in.shape[0]
    alpha = (2 * depth) ** 0.25
    s5_w = s5_d.shape[1]
    att_w = attn_sink.shape[1] * HEAD_DIM
    kv_w = att_w // ATT_REP
    ret_w = ret_log_gamma.shape[2] * HEAD_DIM
    sizes = (s5_w, att_w, kv_w, kv_w, ret_w, ret_w, ret_w, ret_w)
    assert sum(sizes) == w_in.shape[2] and s5_w + att_w + ret_w == w_out.shape[1]
    assert n_ctx % TOKEN_BLOCK == 0 and t_len % TOKEN_BLOCK == 0 and t_len >= 3 * ATT_BLOCK

    pad = (-(bsz + 1)) % 8
    cvec = jnp.concatenate([jnp.zeros((pad, d), F32), c_ctx[None].astype(F32)], axis=0)
    cvec = jnp.concatenate([c.astype(F32), cvec], axis=0)
    mod_all = _modulation(cvec, w_mod.astype(F32), b_mod.astype(F32)).reshape(depth, bsz + pad + 1, 6, d)

    tabs = _rope_tables(t_len, n_ctx)
    xz = jnp.concatenate([ctx, x], axis=1).astype(F32)
    for l in range(depth):
        need_ctx = l < depth - 1
        mod = mod_all[l]
        u, qa, va, qr, vr, gr, ka, kr = _inproj(xz, mod, w_in[l].astype(BF16), tabs, sizes, n_ctx)
        s5w = _s5_weights(s5_lam_re[l], s5_lam_im[l], s5_log_step[l], s5_b_re[l], s5_b_im[l],
                          s5_c_re[l], s5_c_im[l], S5_CHUNK)
        y_s5 = _s5_scan(u, s5w, n_ctx)
        o_att = _attention(qa, ka, va, attn_sink[l], n_ctx, need_ctx)
        o_ret = _retention(qr, kr, vr, gr, ret_log_gamma[l], n_ctx)
        i = l // 2
        router = None if l % 2 == 0 else moe_router[i]
        outs = _outproj(y_s5, u, o_att, o_ret, xz, mod, s5_d[l], s5_w_glu[l], s5_b_glu[l], w_out[l],
                        ln1_g[l], ln1_b[l], router, n_ctx, need_ctx, alpha)
        ctx_tokens = n_ctx if need_ctx else 0
        if l % 2 == 0:
            x1, fx = outs
            xz = _ffn(fx, x1, mod, ffn_w1[i].astype(BF16), ffn_w3[i].astype(BF16), ffn_w2[i].astype(BF16),
                      ln2_g[l], ln2_b[l], ctx_tokens, alpha)
        else:
            x1, fx, route = outs
            xz = _moe(fx, x1, route, mod, moe_w1[i].astype(BF16), moe_w3[i].astype(BF16),
                      moe_w2[i].astype(BF16), ln2_g[l], ln2_b[l], ctx_tokens, alpha)
    return xz if xz.shape[1] == t_len else xz[:, n_ctx:]
```

```python
import functools
import math

import jax
import jax.numpy as jnp
import numpy as np
from jax import lax
from jax.experimental import pallas as pl
from jax.experimental.pallas import tpu as pltpu

F32 = jnp.float32
BF16 = jnp.bfloat16

GRID_W = 64
HEAD_DIM = 64
ATT_REP = 4
WINDOW = 128
ATT_BLOCK = 128
RET_CHUNK = 128
LN_EPS = 1e-5
ROPE_BASE = 10000.0
NEG_INF = -1e30

LANES = 128
S5_CHUNK = LANES
TOKEN_BLOCK = 256
INPROJ_BLOCK = 768
FFN_BLOCK = 768
FFN_CHUNK = 512
MOE_CHUNK = 256
MOD_BLOCK = 1536
VMEM_LIMIT = 56 * 1024 * 1024


def _cparams(sem):
    return pltpu.CompilerParams(dimension_semantics=sem, vmem_limit_bytes=VMEM_LIMIT)


def _pick_chunk(total, target):
    best = LANES
    for c in range(LANES, target + 1, LANES):
        if total % c == 0:
            best = c
    return best


def _dot(a, b):
    return jnp.dot(a, b, preferred_element_type=F32)


def _dot_nt(a, b):
    return lax.dot_general(a, b, (((1,), (1,)), ((), ())), preferred_element_type=F32)


def _dot_tn(a, b):
    return lax.dot_general(a, b, (((0,), (0,)), ((), ())), preferred_element_type=F32)


def _split_bf16(a):
    hi = a.astype(BF16)
    lo = (a - hi.astype(F32)).astype(BF16)
    return hi, lo


def _dot3(a, b):
    ah, al = _split_bf16(a)
    bh, bl = _split_bf16(b)
    return _dot(ah, bh) + (_dot(ah, bl) + _dot(al, bh))


def _ln(x):
    mu = jnp.mean(x, axis=-1, keepdims=True)
    xc = x - mu
    var = jnp.mean(xc * xc, axis=-1, keepdims=True)
    return xc * lax.rsqrt(var + LN_EPS)


def _silu(x):
    return x * (1.0 / (1.0 + jnp.exp(-x)))


def _sigmoid(x):
    return 1.0 / (1.0 + jnp.exp(-x))


def _gelu_tanh(x):
    c = math.sqrt(2.0 / math.pi)
    return 0.5 * x * (1.0 + jnp.tanh(c * (x + 0.044715 * (x * x * x))))


def _mod_kernel(c_ref, w_ref, b_ref, o_ref):
    o_ref[...] = _dot3(_silu(c_ref[...]), w_ref[...]) + b_ref[...]


def _modulation(cvec, w_mod, b_mod):
    depth, d, n = w_mod.shape
    rows = cvec.shape[0]
    tn = _pick_chunk(n, MOD_BLOCK)
    return pl.pallas_call(
        _mod_kernel,
        grid=(depth, n // tn),
        in_specs=[pl.BlockSpec((rows, d), lambda l, j: (0, 0)),
                  pl.BlockSpec((None, d, tn), lambda l, j: (l, 0, j)),
                  pl.BlockSpec((None, 1, tn), lambda l, j: (l, 0, j))],
        out_specs=pl.BlockSpec((None, rows, tn), lambda l, j: (l, 0, j)),
        out_shape=jax.ShapeDtypeStruct((depth, rows, n), F32),
        compiler_params=_cparams(("parallel", "parallel")),
        name="modulation",
    )(cvec, w_mod, b_mod.reshape(depth, 1, n))


def _rope_slab(xs, cos, sa, sb, half):
    return xs * cos + pltpu.roll(xs, LANES - half, 1) * sa + pltpu.roll(xs, half, 1) * sb


def _inproj_kernel(x_ref, modx_ref, modz_ref, wc_ref, w_ref, act_ref, ast_ref, rct_ref, rst_ref,
                   ac_ref, asa_ref, asb_ref, rc_ref, rsa_ref, rsb_ref,
                   ut_ref, qat_ref, vat_ref, qrt_ref, vrt_ref, grt_ref, ka_ref, kr_ref, *, n_ctx, c_sizes, sizes):
    tm = x_ref.shape[0]
    is_ctx = pl.program_id(1) * tm + lax.broadcasted_iota(jnp.int32, (tm, 1), 0) < n_ctx
    shift = jnp.where(is_ctx, modz_ref[0:1, :], modx_ref[0:1, :])
    gain = 1.0 + jnp.where(is_ctx, modz_ref[1:2, :], modx_ref[1:2, :])
    h = (_ln(x_ref[...]) * gain + shift).astype(BF16)
    scale = HEAD_DIM ** -0.5

    ct = _dot_nt(wc_ref[...], h)
    c_offs = [int(o) for o in np.concatenate([[0], np.cumsum(c_sizes)])]

    def chan(i):
        return ct[c_offs[i]:c_offs[i + 1]]

    def rope_t(x, cos_ref, sin_ref, part, mul):
        cos_t, sin_t = cos_ref[...], sin_ref[...]
        heads = []
        for hd in range(x.shape[0] // HEAD_DIM):
            xh = x[hd * HEAD_DIM:(hd + 1) * HEAD_DIM]
            pieces = []
            for lo in range(0, HEAD_DIM, 2 * part):
                pieces += [xh[lo + part:lo + 2 * part], xh[lo:lo + part]]
            out = xh * cos_t + jnp.concatenate(pieces, axis=0) * sin_t
            heads.append(out * mul if mul != 1.0 else out)
        return jnp.concatenate(heads, axis=0)

    def put(ref, val):
        for k in range(ref.shape[0]):
            ref[k] = val[:, k * ATT_BLOCK:(k + 1) * ATT_BLOCK].astype(ref.dtype)

    ut_ref[...] = chan(0)
    put(qat_ref, rope_t(chan(1), act_ref, ast_ref, HEAD_DIM // 4, scale))
    put(vat_ref, chan(2))
    put(qrt_ref, rope_t(chan(3), rct_ref, rst_ref, HEAD_DIM // 2, 1.0))
    put(vrt_ref, chan(4))
    put(grt_ref, chan(5))

    p = _dot(h, w_ref[...])
    ac, asa, asb = ac_ref[...], asa_ref[...], asb_ref[...]
    rc, rsa, rsb = rc_ref[...], rsa_ref[...], rsb_ref[...]
    offs = np.concatenate([[0], np.cumsum(sizes)])

    def rope_cols(i, cos, sa, sb, half, mul):
        blk = p[:, int(offs[i]):int(offs[i + 1])]
        slabs = [_rope_slab(blk[:, s:s + LANES], cos, sa, sb, half) for s in range(0, blk.shape[1], LANES)]
        out = slabs[0] if len(slabs) == 1 else jnp.concatenate(slabs, axis=1)
        return out * mul if mul != 1.0 else out

    ka_ref[...] = rope_cols(0, ac, asa, asb, HEAD_DIM // 4, 1.0).astype(ka_ref.dtype)
    kr_ref[...] = rope_cols(1, rc, rsa, rsb, HEAD_DIM // 2, scale)


def _inproj(xz, mod, w_in, tabs, sizes, n_ctx):
    bsz, s, d = xz.shape
    tm = _pick_chunk(s, INPROJ_BLOCK)
    assert tm % TOKEN_BLOCK == 0, "the output projection reads the S5 arrays in TOKEN_BLOCK pieces"
    nrow = mod.shape[0]
    offs = np.concatenate([[0], np.cumsum(sizes)])
    col = lambda i: w_in[:, int(offs[i]):int(offs[i + 1])]
    c_idx, r_idx = (0, 1, 3, 4, 6, 7), (2, 5)
    c_dtypes = (BF16, BF16, F32, BF16, F32)
    r_dtypes = (BF16, F32)
    c_sizes = tuple(sizes[i] for i in c_idx)
    r_sizes = tuple(sizes[i] for i in r_idx)
    w_c_t = jnp.concatenate([col(i) for i in c_idx], axis=1).T
    w_rest = jnp.concatenate([col(i) for i in r_idx], axis=1)
    sub = tm // ATT_BLOCK
    tok = lambda width: pl.BlockSpec((None, tm, width), lambda b, j: (b, j, 0))
    chan = lambda width: pl.BlockSpec((sub, None, width, ATT_BLOCK), lambda b, j: (j, b, 0, 0))
    tab = pl.BlockSpec((tm, LANES), lambda b, j: (j, 0))
    tab_t = pl.BlockSpec((HEAD_DIM, tm), lambda b, j: (0, j))
    return pl.pallas_call(
        functools.partial(_inproj_kernel, n_ctx=n_ctx, c_sizes=c_sizes, sizes=r_sizes),
        grid=(bsz, s // tm),
        in_specs=[tok(d),
                  pl.BlockSpec((None, 6, d), lambda b, j: (b, 0, 0)),
                  pl.BlockSpec((None, 6, d), lambda b, j: (nrow - 1, 0, 0)),
                  pl.BlockSpec((sum(c_sizes), d), lambda b, j: (0, 0)),
                  pl.BlockSpec((d, sum(r_sizes)), lambda b, j: (0, 0)),
                  tab_t, tab_t, tab_t, tab_t, tab, tab, tab, tab, tab, tab],
        out_specs=[pl.BlockSpec((None, None, c_sizes[0], tm), lambda b, j: (j, b, 0, 0))]
                  + [chan(w) for w in c_sizes[1:]] + [tok(w) for w in r_sizes],
        out_shape=[jax.ShapeDtypeStruct((s // tm, bsz, c_sizes[0], tm), F32)]
                  + [jax.ShapeDtypeStruct((s // ATT_BLOCK, bsz, w, ATT_BLOCK), dt)
                     for w, dt in zip(c_sizes[1:], c_dtypes)]
                  + [jax.ShapeDtypeStruct((bsz, s, w), dt) for w, dt in zip(r_sizes, r_dtypes)],
        compiler_params=_cparams(("parallel", "parallel")),
        name="inproj",
    )(xz, mod, mod, w_c_t, w_rest, *tabs)


def _rope_tables(t_len, n_ctx):
    t = jnp.arange(t_len)
    rows = (t // GRID_W).astype(F32)
    cols = (t % GRID_W).astype(F32)
    pos = t.astype(F32)

    def angles(p, dim):
        inv_freq = ROPE_BASE ** (-jnp.arange(0, dim, 2, dtype=F32) / dim)
        return p[:, None] * inv_freq[None, :]

    def head_tables(angs):
        cos = jnp.concatenate([jnp.concatenate([jnp.cos(a), jnp.cos(a)], -1) for a in angs], -1)
        sa = jnp.concatenate([jnp.concatenate([-jnp.sin(a), jnp.zeros_like(a)], -1) for a in angs], -1)
        sb = jnp.concatenate([jnp.concatenate([jnp.zeros_like(a), jnp.sin(a)], -1) for a in angs], -1)
        rep = LANES // HEAD_DIM
        return tuple(jnp.tile(x, (1, rep)) for x in (cos, sa, sb))

    att = head_tables([angles(rows, HEAD_DIM // 2), angles(cols, HEAD_DIM // 2)])
    ret = head_tables([angles(pos, HEAD_DIM)])
    signed_t = lambda tb: (tb[0][:, :HEAD_DIM].T, (tb[1] + tb[2])[:, :HEAD_DIM].T)

    def with_ctx(tb, axis, is_cos):
        shape = list(tb.shape)
        shape[axis] = n_ctx
        lead = jnp.ones(shape, F32) if is_cos else jnp.zeros(shape, F32)
        return jnp.concatenate([lead, tb], axis=axis)

    chan = signed_t(att) + signed_t(ret)
    toks = att + ret
    return (tuple(with_ctx(tb, 1, k % 2 == 0) for k, tb in enumerate(chan))
            + tuple(with_ctx(tb, 0, k % 3 == 0) for k, tb in enumerate(toks)))


def _s5_kernel(u_ref, tab_ref, wsf_ref, wsb_ref, wrf_ref, wrb_ref, lam_ref, y_ref,
               lhs_ref, m_ref, acc_ref, cf_ref, cb_ref, pf_ref, pb_ref, *, bsz, nz_chunks):
    r_blk, ch, tm = u_ref.shape
    c = S5_CHUNK
    kpb = tm // c
    n_chunks = (r_blk // bsz) * kpb

    for i in range(ch):
        x = u_ref[:, i, :].astype(BF16)
        p, half = divmod(i, 2)
        for k in range(kpb):
            lhs_ref[p, k * r_blk:(k + 1) * r_blk, half * c:(half + 1) * c] = x[:, k * c:(k + 1) * c]
    acc_ref[...] = jnp.zeros_like(acc_ref)
    cf_ref[...] = jnp.zeros_like(cf_ref)
    cb_ref[...] = jnp.zeros_like(cb_ref)

    def pair(p, carry):
        for ih in range(2):
            for o in range(ch):
                lag_row = jnp.broadcast_to(tab_ref[2 * p + ih, o:o + 1, :], (8, 2 * c))
                tab = pltpu.roll(lag_row, 0, 1, stride=1, stride_axis=0)
                for r2 in range(c // 16):
                    lo = c - 16 * r2
                    rows = jnp.concatenate([tab[:, lo:lo + c], tab[:, lo - 8:lo - 8 + c]], axis=0)
                    m_ref[ih * c + 16 * r2:ih * c + 16 * r2 + 16, o * c:(o + 1) * c] = rows.astype(BF16)
        lhs = lhs_ref[p]
        acc_ref[...] += _dot(lhs, m_ref[...])
        cf_ref[...] += _dot(lhs, wsf_ref[p])
        cb_ref[...] += _dot(lhs, wsb_ref[p])
        return carry

    lax.fori_loop(0, ch // 2, pair, 0)
    lam = lam_ref[...]
    nst = lam.shape[1] // 2

    def advance(s, a, bc, drive):
        return s * a + pltpu.roll(s, nst, 1) * bc + drive

    def sweep(order, c_ref, p_ref, a, bc):
        s = jnp.zeros((bsz, lam.shape[1]), F32)
        for n in order:
            j, k = divmod(n, kpb)
            rows = slice(k * r_blk + j * bsz, k * r_blk + (j + 1) * bsz)
            p_ref[rows, :] = s
            s = advance(s, a, bc, c_ref[rows, :])

    sweep(list(range(n_chunks)), cf_ref, pf_ref, lam[0:1], lam[1:2])
    order_b = list(range(nz_chunks - 1, -1, -1)) + list(range(n_chunks - 1, nz_chunks - 1, -1))
    sweep(order_b, cb_ref, pb_ref, lam[2:3], lam[3:4])
    acc_ref[...] += (_dot(pf_ref[...].astype(BF16), wrf_ref[...])
                     + _dot(pb_ref[...].astype(BF16), wrb_ref[...]))
    for o in range(ch):
        for k in range(kpb):
            y_ref[:, o, k * c:(k + 1) * c] = acc_ref[k * r_blk:(k + 1) * r_blk, o * c:(o + 1) * c]


def _s5_weights(lam_re, lam_im, log_step, b_re, b_im, c_re, c_im, chunk):
    hp = lax.Precision.HIGHEST
    lam = lax.complex(lam_re.astype(F32), lam_im.astype(F32))
    lam_dt = lam * jnp.exp(log_step.astype(F32))[..., None]
    lam_bar = jnp.exp(lam_dt)
    b_bar = lax.complex(b_re.astype(F32), b_im.astype(F32)) * ((lam_bar - 1.0) / lam)[..., None]
    c_mat = lax.complex(c_re.astype(F32), c_im.astype(F32))
    g, n = lam.shape[1], lam.shape[2]
    ch = b_bar.shape[-1]
    steps = jnp.arange(chunk + 1, dtype=F32)
    pw = jnp.exp(steps[None, :, None, None] * lam_dt[:, None])
    kern = jnp.einsum('zgon,zdgn,zgni->zgdoi', c_mat, pw[:, :chunk], b_bar, precision=hp).real
    zero_lag = kern[0, :, 0] + kern[1, :, 0]
    lag_table = jnp.concatenate([jnp.zeros_like(zero_lag)[:, None], kern[1, :, :0:-1], zero_lag[:, None],
                                 kern[0, :, 1:]], axis=1).transpose(0, 3, 2, 1)

    def state_in(pw_s, b_dir):
        w = pw_s[:, :, :, None] * b_dir[None]
        w = jnp.concatenate([w.real, w.imag], axis=2)
        return w.transpose(1, 3, 0, 2).reshape(g, ch // 2, 2 * chunk, 2 * n)

    def state_out(pw_t, c_dir):
        w = c_dir[None] * pw_t[:, :, None, :]
        w = jnp.concatenate([w.real, -w.imag], axis=3)
        return w.transpose(1, 3, 2, 0).reshape(g, 2 * n, ch * chunk)

    wsf = state_in(pw[0, chunk - 1::-1][:chunk], b_bar[0])
    wsb = state_in(pw[1, :chunk], b_bar[1])
    wrf = state_out(pw[0, 1:chunk + 1], c_mat[0])
    wrb = state_out(pw[1, chunk:0:-1], c_mat[1])
    lam_c = pw[:, chunk]
    rows = []
    for z in range(2):
        rows.append(jnp.concatenate([lam_c[z].real, lam_c[z].real], -1))
        rows.append(jnp.concatenate([-lam_c[z].imag, lam_c[z].imag], -1))
    lam_rows = jnp.stack(rows + [jnp.zeros_like(rows[0])] * 4, axis=1)
    return lag_table, wsf.astype(BF16), wsb.astype(BF16), wrf.astype(BF16), wrb.astype(BF16), lam_rows


def _s5_scan(ut, weights, n_ctx):
    nblk, bsz, width, tm = ut.shape
    tab, wsf, wsb, wrf, wrb, lam_rows = weights
    g, ch = tab.shape[0], tab.shape[1]
    c = S5_CHUNK
    nst2 = wsf.shape[3]
    r_blk = nblk * bsz
    rows = r_blk * (tm // c)
    grp = lambda *shape: pl.BlockSpec((None,) + shape, lambda i: (i,) + (0,) * len(shape))
    tok = pl.BlockSpec((r_blk, ch, tm), lambda i: (0, i, 0))
    y = pl.pallas_call(
        functools.partial(_s5_kernel, bsz=bsz, nz_chunks=n_ctx // c),
        grid=(g,),
        in_specs=[tok, grp(ch, ch, 2 * c), grp(ch // 2, 2 * c, nst2), grp(ch // 2, 2 * c, nst2),
                  grp(nst2, ch * c), grp(nst2, ch * c), grp(8, nst2)],
        out_specs=tok,
        out_shape=jax.ShapeDtypeStruct((r_blk, width, tm), F32),
        scratch_shapes=[pltpu.VMEM((ch // 2, rows, 2 * c), BF16),
                        pltpu.VMEM((2 * c, ch * c), BF16),
                        pltpu.VMEM((rows, ch * c), F32)] + [pltpu.VMEM((rows, nst2), F32)] * 4,
        compiler_params=_cparams(("parallel",)),
        name="s5_scan",
    )(ut.reshape(r_blk, width, tm), tab, wsf, wsb, wrf, wrb, lam_rows)
    return y.reshape(nblk, bsz, width, tm)


def _attn_kernel(sink_ref, q_ref, k_ref, v_ref, o_ref, *, n_ctx, t_len, q_off, kv_heads):
    qi = pl.program_id(1) + q_off
    nzb = n_ctx // ATT_BLOCK
    band = 3 * ATT_BLOCK
    wide = ATT_REP * ATT_BLOCK
    q = q_ref[...].astype(BF16)
    kc = k_ref[0:n_ctx, :].astype(BF16)
    vc = jnp.concatenate([v_ref[p] for p in range(nzb)], axis=1).astype(BF16)

    def rows(a, i):
        return a[i * HEAD_DIM:(i + 1) * HEAD_DIM]

    def q_group(g):
        qg = jnp.concatenate([rows(q, g * ATT_REP + r) for r in range(ATT_REP)], axis=1)
        zero = jnp.zeros_like(qg)
        return jnp.concatenate([qg if gg == g else zero for gg in range(kv_heads)], axis=0)

    def finish(g, o, den):
        o = o / den
        for r in range(ATT_REP):
            h = g * ATT_REP + r
            o_ref[h * HEAD_DIM:(h + 1) * HEAD_DIM, :] = o[:, r * ATT_BLOCK:(r + 1) * ATT_BLOCK].astype(o_ref.dtype)

    @pl.when(qi < nzb)
    def _():
        for g in range(kv_heads):
            sink = sink_ref[g:g + 1, :]
            s = _dot(kc, q_group(g))
            m = jnp.maximum(jnp.max(s, axis=0, keepdims=True), sink)
            e = jnp.exp(s - m)
            den = jnp.sum(e, axis=0, keepdims=True) + jnp.exp(sink - m)
            finish(g, _dot(rows(vc, g), e.astype(BF16)), den)

    @pl.when(qi >= nzb)
    def _():
        n = qi - nzb
        start = jnp.clip((n - 1) * ATT_BLOCK, 0, t_len - band)
        kl = k_ref[pl.ds(pl.multiple_of(n_ctx + start, ATT_BLOCK), band), :].astype(BF16)
        p0 = (n_ctx + start) // ATT_BLOCK
        vl = jnp.concatenate([v_ref[p0 + t] for t in range(3)], axis=1).astype(BF16)
        k_pos = start + lax.broadcasted_iota(jnp.int32, (band, wide), 0)
        q_pos = n * ATT_BLOCK + (lax.broadcasted_iota(jnp.int32, (band, wide), 1) & (ATT_BLOCK - 1))
        valid = jnp.abs(k_pos - q_pos) <= WINDOW
        for g in range(kv_heads):
            sink = sink_ref[g:g + 1, :]
            qg = q_group(g)
            s_loc = jnp.where(valid, _dot(kl, qg), NEG_INF)
            s_ctx = _dot(kc, qg)
            m = jnp.maximum(jnp.maximum(jnp.max(s_loc, axis=0, keepdims=True),
                                        jnp.max(s_ctx, axis=0, keepdims=True)), sink)
            e_loc = jnp.exp(s_loc - m)
            e_ctx = jnp.exp(s_ctx - m)
            den = (jnp.sum(e_loc, axis=0, keepdims=True) + jnp.sum(e_ctx, axis=0, keepdims=True)
                   + jnp.exp(sink - m))
            o = _dot(rows(vl, g), e_loc.astype(BF16)) + _dot(rows(vc, g), e_ctx.astype(BF16))
            finish(g, o, den)


def _attention(qat, ka, vat, sink, n_ctx, need_ctx):
    nblk, bsz, qw, _ = qat.shape
    s, kvw = ka.shape[1], ka.shape[2]
    t_len = s - n_ctx
    kv_heads = kvw // HEAD_DIM
    q_off = 0 if need_ctx else n_ctx // ATT_BLOCK
    nq = nblk - q_off
    sink_rows = jnp.repeat(sink.astype(F32).reshape(kv_heads, ATT_REP), ATT_BLOCK, axis=1)
    return pl.pallas_call(
        functools.partial(_attn_kernel, n_ctx=n_ctx, t_len=t_len, q_off=q_off, kv_heads=kv_heads),
        grid=(bsz, nq),
        in_specs=[pl.BlockSpec(sink_rows.shape, lambda b, j: (0, 0)),
                  pl.BlockSpec((None, None, qw, ATT_BLOCK), lambda b, j: (j + q_off, b, 0, 0)),
                  pl.BlockSpec((None, s, kvw), lambda b, j: (b, 0, 0)),
                  pl.BlockSpec((nblk, None, kvw, ATT_BLOCK), lambda b, j: (0, b, 0, 0))],
        out_specs=pl.BlockSpec((None, None, qw, ATT_BLOCK), lambda b, j: (j, b, 0, 0)),
        out_shape=jax.ShapeDtypeStruct((nq, bsz, qw, ATT_BLOCK), BF16),
        compiler_params=_cparams(("parallel", "arbitrary")),
        name="window_attention",
    )(sink_rows, qat, ka, vat)


def _ret_kernel(lg_ref, q_ref, k_ref, v_ref, g_ref, o_ref, acc_ref, sf_ref, sb_ref, *, nz_chunks, heads):
    c = RET_CHUNK
    n_chunks, w, _ = q_ref.shape
    f32 = lambda a: a.astype(F32)

    def per_head(shape, axis, group, direction):
        owner = lax.broadcasted_iota(jnp.int32, shape, axis) // group
        out = jnp.zeros(shape, F32)
        for h in range(heads):
            out = jnp.where(owner == h, lg_ref[direction, h], out)
        return out

    row_i = f32(lax.broadcasted_iota(jnp.int32, (w, c), 1))
    qw_f = jnp.exp(per_head((w, c), 0, HEAD_DIM, 0) * (row_i + 1.0))
    qw_b = jnp.exp(per_head((w, c), 0, HEAD_DIM, 1) * (c - row_i))
    key_j = f32(lax.broadcasted_iota(jnp.int32, (c, w), 0))
    kw_f = jnp.exp(per_head((c, w), 1, HEAD_DIM, 0) * (c - 1.0 - key_j))
    kw_b = jnp.exp(per_head((c, w), 1, HEAD_DIM, 1) * key_j)
    dec_f = jnp.exp(per_head((w, 1), 0, HEAD_DIM, 0) * c)
    dec_b = jnp.exp(per_head((w, 1), 0, HEAD_DIM, 1) * c)
    wide = heads * c
    diff = f32((lax.broadcasted_iota(jnp.int32, (c, wide), 1) & (c - 1)) - lax.broadcasted_iota(jnp.int32, (c, wide), 0))
    dec_t = jnp.where(diff >= 0, jnp.exp(per_head((c, wide), 1, c, 0) * jnp.maximum(diff, 0.0)),
                      jnp.exp(per_head((c, wide), 1, c, 1) * jnp.maximum(-diff, 0.0)))
    own_wide = (lax.broadcasted_iota(jnp.int32, (w, wide), 0) // HEAD_DIM
                == lax.broadcasted_iota(jnp.int32, (w, wide), 1) // c)
    own_sq = f32(lax.broadcasted_iota(jnp.int32, (w, w), 0) // HEAD_DIM
                 == lax.broadcasted_iota(jnp.int32, (w, w), 1) // HEAD_DIM)
    sf_ref[...] = jnp.zeros_like(sf_ref)
    sb_ref[...] = jnp.zeros_like(sb_ref)

    def block_diag(x):
        return jnp.where(own_wide, jnp.concatenate([x] * heads, axis=1), 0.0).astype(BF16)

    def fwd(n, carry):
        qt, vt = q_ref[n], v_ref[n]
        k = k_ref[pl.ds(pl.multiple_of(n * c, c), c), :]
        scores_t = _dot(k.astype(BF16), block_diag(qt)) * dec_t
        stacked = jnp.concatenate([scores_t[:, h * c:(h + 1) * c] for h in range(heads)], axis=0)
        o = _dot(block_diag(vt), stacked.astype(BF16))
        s_prev = sf_ref[...]
        o = o + _dot(s_prev.astype(BF16), (qt * qw_f).astype(BF16))
        sf_ref[...] = dec_f * s_prev + own_sq * _dot(vt.astype(BF16), (k * kw_f).astype(BF16))
        acc_ref[n] = o
        return carry

    lax.fori_loop(0, n_chunks, fwd, 0)

    def bwd(i, carry):
        n = jnp.where(i < nz_chunks, nz_chunks - 1 - i, n_chunks - 1 - i + nz_chunks)
        qt, vt = q_ref[n], v_ref[n]
        k = k_ref[pl.ds(pl.multiple_of(n * c, c), c), :]
        s_prev = sb_ref[...]
        o = acc_ref[n] + _dot(s_prev.astype(BF16), (qt * qw_b).astype(BF16))
        sb_ref[...] = dec_b * s_prev + own_sq * _dot(vt.astype(BF16), (k * kw_b).astype(BF16))
        gate = _silu(g_ref[n])
        for h in range(heads):
            rows = slice(h * HEAD_DIM, (h + 1) * HEAD_DIM)
            oh = o[rows]
            mu = jnp.mean(oh, axis=0, keepdims=True)
            oc = oh - mu
            var = jnp.mean(oc * oc, axis=0, keepdims=True)
            o_ref[n, rows, :] = (oc * lax.rsqrt(var + LN_EPS) * gate[rows]).astype(o_ref.dtype)
        return carry

    lax.fori_loop(0, n_chunks, bwd, 0)


def _retention(qrt, kr, vrt, grt, log_gamma, n_ctx):
    n_chunks, bsz, w, c = qrt.shape
    s = kr.shape[1]
    chan = pl.BlockSpec((n_chunks, None, w, c), lambda b: (0, b, 0, 0))
    return pl.pallas_call(
        functools.partial(_ret_kernel, nz_chunks=n_ctx // RET_CHUNK, heads=w // HEAD_DIM),
        grid=(bsz,),
        in_specs=[pl.BlockSpec(memory_space=pltpu.SMEM), chan, pl.BlockSpec((None, s, w), lambda b: (b, 0, 0)),
                  chan, chan],
        out_specs=chan,
        out_shape=jax.ShapeDtypeStruct((n_chunks, bsz, w, c), BF16),
        scratch_shapes=[pltpu.VMEM((n_chunks, w, c), F32), pltpu.VMEM((w, w), F32), pltpu.VMEM((w, w), F32)],
        compiler_params=_cparams(("parallel",)),
        name="retention",
    )(log_gamma.astype(F32), qrt, kr, vrt, grt)


def _outproj_kernel(*refs, nzb, alpha, with_router, w_s5, w_att, n_exp):
    if with_router:
        (y_ref, u_ref, a_ref, r_ref, x_ref, modx_ref, modz_ref, d_ref, wg_ref, bg_ref, wo_ref,
         g1_ref, b1_ref, rt_ref, x1_ref, fx_ref, gate_ref) = refs
    else:
        (y_ref, u_ref, a_ref, r_ref, x_ref, modx_ref, modz_ref, d_ref, wg_ref, bg_ref, wo_ref,
         g1_ref, b1_ref, x1_ref, fx_ref) = refs
    is_ctx = pl.program_id(1) < nzb
    mod = jnp.where(is_ctx, modz_ref[...], modx_ref[...])
    g = _gelu_tanh(y_ref[...] + d_ref[...] * u_ref[...])
    s5 = g * _sigmoid(_dot(wg_ref[...], g.astype(BF16)) + bg_ref[...])
    def chan_major(ref, lo, hi):
        return jnp.concatenate([_dot_tn(ref[k].astype(BF16), wo_ref[lo:hi, :]) for k in range(ref.shape[0])],
                               axis=0)

    mix = (_dot_tn(s5.astype(BF16), wo_ref[0:w_s5, :]) + chan_major(a_ref, w_s5, w_s5 + w_att)
           + chan_major(r_ref, w_s5 + w_att, wo_ref.shape[0]))
    x1 = _ln(alpha * x_ref[...] + mod[2:3] * mix) * g1_ref[...] + b1_ref[...]
    x1_ref[...] = x1
    fx = _ln(x1) * (1.0 + mod[4:5]) + mod[3:4]
    fx_ref[...] = fx.astype(fx_ref.dtype)
    if with_router:
        lane = lax.broadcasted_iota(jnp.int32, (fx.shape[0], LANES), 1)
        logits = jnp.where(lane < n_exp, _dot3(fx, rt_ref[...]), -jnp.inf)
        m1 = jnp.max(logits, axis=1, keepdims=True)
        i1 = jnp.min(jnp.where(logits == m1, lane, LANES), axis=1, keepdims=True)
        rest = jnp.where(lane == i1, -jnp.inf, logits)
        m2 = jnp.max(rest, axis=1, keepdims=True)
        i2 = jnp.min(jnp.where(rest == m2, lane, LANES), axis=1, keepdims=True)
        e2 = jnp.exp(m2 - m1)
        den = 1.0 + e2
        route = jnp.where(lane == 0, 1.0 / den, jnp.where(lane == 1, e2 / den, 0.0))
        route = jnp.where(lane == 2, i1.astype(F32), jnp.where(lane == 3, i2.astype(F32), route))
        gate_ref[...] = route


def _outproj(y_s5, u, o_att, o_ret, xz, mod, s5_d, w_glu, b_glu, w_out, ln_g, ln_b, router, n_ctx, need_ctx,
             alpha):
    bsz, s, d = xz.shape
    tm = TOKEN_BLOCK
    nzb = n_ctx // tm
    off = 0 if need_ctx else nzb
    nblk = s // tm - off
    s_out = nblk * tm
    nrow = mod.shape[0]
    w_s5, w_att = y_s5.shape[2], o_att.shape[2]
    sub = tm // ATT_BLOCK
    att_off = off if o_att.shape[0] == s // ATT_BLOCK else 0
    att = pl.BlockSpec((sub, None, w_att, ATT_BLOCK), lambda b, j: (j + att_off, b, 0, 0))

    def tok(width, shift):
        return pl.BlockSpec((None, tm, width), lambda b, j: (b, j + shift, 0))

    def full(a):
        return pl.BlockSpec(a.shape, lambda b, j: (0,) * a.ndim)

    per_in = y_s5.shape[3] // tm
    chan = pl.BlockSpec((None, None, w_s5, tm), lambda b, j: ((j + off) // per_in, b, 0, (j + off) % per_in))
    vec = lambda a: a.reshape(1, -1).astype(F32)
    col = lambda a: a.reshape(-1, 1).astype(F32)
    consts = [col(s5_d), w_glu.T.astype(BF16), col(b_glu), w_out.astype(BF16), vec(ln_g), vec(ln_b)]
    ret = pl.BlockSpec((sub, None, o_ret.shape[2], ATT_BLOCK), lambda b, j: (j + off, b, 0, 0))
    in_specs = [chan, chan, att, ret, tok(d, off),
                pl.BlockSpec((None, 6, d), lambda b, j: (b, 0, 0)),
                pl.BlockSpec((None, 6, d), lambda b, j: (nrow - 1, 0, 0))] + [full(a) for a in consts]
    out_specs = [tok(d, 0), tok(d, 0)]
    fx_dtype = BF16 if router is None else F32
    out_shape = [jax.ShapeDtypeStruct((bsz, s_out, d), F32), jax.ShapeDtypeStruct((bsz, s_out, d), fx_dtype)]
    args = [y_s5, u, o_att, o_ret, xz, mod, mod] + consts
    with_router = router is not None
    n_exp = 0
    if with_router:
        n_exp = router.shape[1]
        router_pad = jnp.pad(router.astype(F32), ((0, 0), (0, LANES - n_exp)))
        args.append(router_pad)
        in_specs.append(full(router_pad))
        out_specs.append(tok(LANES, 0))
        out_shape.append(jax.ShapeDtypeStruct((bsz, s_out, LANES), F32))
    return pl.pallas_call(
        functools.partial(_outproj_kernel, nzb=nzb - off, alpha=alpha, with_router=with_router,
                          w_s5=w_s5, w_att=w_att, n_exp=n_exp),
        grid=(bsz, nblk),
        in_specs=in_specs,
        out_specs=out_specs,
        out_shape=out_shape,
        compiler_params=_cparams(("parallel", "parallel")),
        name="outproj",
    )(*args)


def _swiglu_into(x_bf16, w1_ref, w3_ref, w2_ref, acc_ref, fc):
    for s in range(0, w1_ref.shape[1], fc):
        h1 = _dot(x_bf16, w1_ref[:, s:s + fc])
        h3 = _dot(x_bf16, w3_ref[:, s:s + fc])
        acc_ref[...] += _dot((_silu(h1) * h3).astype(BF16), w2_ref[s:s + fc, :])


def _ffn_kernel(fx_ref, x1_ref, modx_ref, modz_ref, w1_ref, w3_ref, w2_ref, g2_ref, b2_ref,
                o_ref, acc_ref, *, n_ctx_tokens, alpha, fc):
    tm = acc_ref.shape[0]
    acc_ref[...] = jnp.zeros_like(acc_ref)
    _swiglu_into(fx_ref[...], w1_ref, w3_ref, w2_ref, acc_ref, fc)
    row = pl.program_id(1) * tm + lax.broadcasted_iota(jnp.int32, (tm, 1), 0)
    gate = jnp.where(row < n_ctx_tokens, modz_ref[5:6, :], modx_ref[5:6, :])
    o_ref[...] = _ln(alpha * x1_ref[...] + gate * acc_ref[...]) * g2_ref[...] + b2_ref[...]


def _ffn(fx, x1, mod, w1, w3, w2, ln_g, ln_b, n_ctx_tokens, alpha):
    bsz, s, d = x1.shape
    ff = w1.shape[1]
    tm = _pick_chunk(s, FFN_BLOCK)
    nrow = mod.shape[0]
    tok = pl.BlockSpec((None, tm, d), lambda b, j: (b, j, 0))
    full = lambda a: pl.BlockSpec(a.shape, lambda b, j: (0,) * a.ndim, pipeline_mode=pl.Buffered(1))
    vec = lambda a: a.reshape(1, -1).astype(F32)
    return pl.pallas_call(
        functools.partial(_ffn_kernel, n_ctx_tokens=n_ctx_tokens, alpha=alpha, fc=_pick_chunk(ff, FFN_CHUNK)),
        grid=(bsz, s // tm),
        in_specs=[tok, tok,
                  pl.BlockSpec((None, 6, d), lambda b, j: (b, 0, 0)),
                  pl.BlockSpec((None, 6, d), lambda b, j: (nrow - 1, 0, 0)),
                  full(w1), full(w3), full(w2),
                  pl.BlockSpec((1, d), lambda b, j: (0, 0)), pl.BlockSpec((1, d), lambda b, j: (0, 0))],
        out_specs=tok,
        out_shape=jax.ShapeDtypeStruct((bsz, s, d), F32),
        scratch_shapes=[pltpu.VMEM((tm, d), F32)],
        compiler_params=_cparams(("parallel", "parallel")),
        name="dense_ffn",
    )(fx, x1, mod, mod, w1, w3, w2, vec(ln_g), vec(ln_b))


MOE_TILE = 1024


def _route_plan(e1, e2, n_exp, tile):
    n = e1.shape[0]
    pair_e = jnp.stack([e1, e2], axis=1).reshape(-1)
    onehot = (pair_e[:, None] == jnp.arange(n_exp, dtype=jnp.int32)[None, :]).astype(jnp.int32)
    before = jnp.cumsum(onehot, axis=0) - onehot
    rank = jnp.sum(before * onehot, axis=1)
    counts = jnp.sum(onehot, axis=0)
    padded = (counts + tile - 1) // tile * tile
    ends = jnp.cumsum(padded)
    starts = ends - padded
    dest = starts[pair_e] + rank
    n_rows = (2 * n + n_exp * (tile - 1)) // tile * tile
    n_tiles = n_rows // tile
    tile_start = jnp.arange(n_tiles, dtype=jnp.int32) * tile
    tile_expert = jnp.minimum(jnp.sum((tile_start[:, None] >= ends[None, :]).astype(jnp.int32), axis=1),
                              n_exp - 1)
    by_expert = jnp.sort(pair_e * (2 * n) + jnp.arange(2 * n, dtype=jnp.int32)) % (2 * n)
    row = jnp.arange(n_rows, dtype=jnp.int32)
    row_e = jnp.repeat(tile_expert, tile)
    in_group = row - starts[row_e]
    src = jnp.minimum(in_group + (jnp.cumsum(counts) - counts)[row_e], 2 * n - 1)
    row_token = jnp.where(in_group < counts[row_e], by_expert[src] // 2, 0)
    n_used = (ends[-1] // tile).astype(jnp.int32).reshape(1)
    return row_token, dest.reshape(n, 2), tile_expert, n_used


def _moe_gemm_kernel(te_ref, nused_ref, tok_ref, tokn_ref, x_hbm, w1_ref, w3_ref, w2_ref, y_ref,
                     xbuf_ref, sem, *, fc, tile):
    i = pl.program_id(0)
    f = pl.program_id(1)
    slot = i % 2
    n_used = nused_ref[0]

    def row_copy(t_ref, r, s):
        return pltpu.make_async_copy(x_hbm.at[pl.ds(t_ref[0, r], 1)], xbuf_ref.at[s, pl.ds(r, 1)], sem.at[s])

    def issue(t_ref, s):
        def body(it, carry):
            base = pl.multiple_of(it * 8, 8)
            for k in range(8):
                row_copy(t_ref, base + k, s).start()
            return carry
        lax.fori_loop(0, tile // 8, body, 0)

    @pl.when(f == 0)
    def _():
        y_ref[...] = jnp.zeros_like(y_ref)

        @pl.when(i == 0)
        def _():
            issue(tok_ref, 0)

        @pl.when(i + 1 < n_used)
        def _():
            issue(tokn_ref, 1 - slot)

        @pl.when(i < n_used)
        def _():
            pltpu.make_async_copy(x_hbm.at[pl.ds(0, tile)], xbuf_ref.at[slot], sem.at[slot]).wait()

    @pl.when(i < n_used)
    def _():
        _swiglu_into(xbuf_ref[slot].astype(BF16), w1_ref, w3_ref, w2_ref, y_ref, fc)


def _moe_gemm(x_flat, row_token, tile_expert, n_used, w1, w3, w2, tile):
    n_rows = row_token.shape[0]
    n_tiles = n_rows // tile
    d = x_flat.shape[1]
    ff = w1.shape[2]
    nf = 2 if ff % (2 * LANES) == 0 else 1
    tf = ff // nf
    last = lambda i, nu: jnp.minimum(i, nu[0] - 1)
    fsel = lambda i, f, nu: jnp.where(i < nu[0], f, nf - 1)
    tok = lambda imap: pl.BlockSpec((None, 1, tile), imap, memory_space=pltpu.SMEM)
    tokens = row_token.reshape(n_tiles, 1, tile)
    return pl.pallas_call(
        functools.partial(_moe_gemm_kernel, fc=_pick_chunk(tf, MOE_CHUNK), tile=tile),
        grid_spec=pltpu.PrefetchScalarGridSpec(
            num_scalar_prefetch=2,
            grid=(n_tiles, nf),
            in_specs=[tok(lambda i, f, te, nu: (i, 0, 0)),
                      tok(lambda i, f, te, nu: (jnp.minimum(i + 1, n_tiles - 1), 0, 0)),
                      pl.BlockSpec(memory_space=pl.ANY),
                      pl.BlockSpec((None, d, tf), lambda i, f, te, nu: (te[last(i, nu)], 0, fsel(i, f, nu))),
                      pl.BlockSpec((None, d, tf), lambda i, f, te, nu: (te[last(i, nu)], 0, fsel(i, f, nu))),
                      pl.BlockSpec((None, tf, d), lambda i, f, te, nu: (te[last(i, nu)], fsel(i, f, nu), 0))],
            out_specs=pl.BlockSpec((tile, d), lambda i, f, te, nu: (i, 0)),
            scratch_shapes=[pltpu.VMEM((2, tile, d), F32), pltpu.SemaphoreType.DMA((2,))]),
        out_shape=jax.ShapeDtypeStruct((n_rows, d), F32),
        compiler_params=_cparams(("arbitrary", "arbitrary")),
        name="moe_gemm",
    )(tile_expert, n_used, tokens, tokens, x_flat, w1, w3, w2)


def _moe_combine_kernel(pos_ref, posn_ref, route_ref, x1_ref, modx_ref, modz_ref, g2_ref, b2_ref, y_hbm,
                        o_ref, buf_ref, sem, *, nzb, nblk, alpha, tm):
    i = pl.program_id(0)
    slot = i % 2

    def row_copy(p_ref, r, k, s):
        return pltpu.make_async_copy(y_hbm.at[pl.ds(p_ref[0, k * tm + r], 1)],
                                     buf_ref.at[s, pl.ds(k * tm + r, 1)], sem.at[s])

    def issue(p_ref, s):
        def body(r, carry):
            row_copy(p_ref, r, 0, s).start()
            row_copy(p_ref, r, 1, s).start()
            return carry
        lax.fori_loop(0, tm, body, 0, unroll=8)

    @pl.when(i == 0)
    def _():
        issue(pos_ref, 0)

    @pl.when(i + 1 < pl.num_programs(0))
    def _():
        issue(posn_ref, 1 - slot)

    pltpu.make_async_copy(y_hbm.at[pl.ds(0, 2 * tm)], buf_ref.at[slot], sem.at[slot]).wait()
    route = route_ref[...]
    f = route[:, 0:1] * buf_ref[slot, 0:tm] + route[:, 1:2] * buf_ref[slot, tm:2 * tm]
    is_ctx = (i % nblk) < nzb
    mod = jnp.where(is_ctx, modz_ref[...], modx_ref[...])
    o_ref[...] = _ln(alpha * x1_ref[...] + mod[5:6] * f) * g2_ref[...] + b2_ref[...]


def _moe_combine(ys, pos, route, x1, mod, ln_g, ln_b, n_ctx_tokens, alpha):
    bsz, s, d = x1.shape
    tm = TOKEN_BLOCK
    nblk = s // tm
    n_steps = bsz * nblk
    nrow = mod.shape[0]
    pos_steps = pos.reshape(n_steps, tm, 2).transpose(0, 2, 1).reshape(n_steps, 1, 2 * tm)
    tok = lambda width: pl.BlockSpec((tm, width), lambda i: (i, 0))
    smem = lambda imap: pl.BlockSpec((None, 1, 2 * tm), imap, memory_space=pltpu.SMEM)
    vec = lambda a: a.reshape(1, -1).astype(F32)
    out = pl.pallas_call(
        functools.partial(_moe_combine_kernel, nzb=n_ctx_tokens // tm, nblk=nblk, alpha=alpha, tm=tm),
        grid=(n_steps,),
        in_specs=[smem(lambda i: (i, 0, 0)),
                  smem(lambda i: (jnp.minimum(i + 1, n_steps - 1), 0, 0)),
                  tok(LANES), tok(d),
                  pl.BlockSpec((None, 6, d), lambda i: (i // nblk, 0, 0)),
                  pl.BlockSpec((None, 6, d), lambda i: (nrow - 1, 0, 0)),
                  pl.BlockSpec((1, d), lambda i: (0, 0)), pl.BlockSpec((1, d), lambda i: (0, 0)),
                  pl.BlockSpec(memory_space=pl.ANY)],
        out_specs=tok(d),
        out_shape=jax.ShapeDtypeStruct((bsz * s, d), F32),
        scratch_shapes=[pltpu.VMEM((2, 2 * tm, d), F32), pltpu.SemaphoreType.DMA((2,))],
        compiler_params=_cparams(("arbitrary",)),
        name="moe_combine",
    )(pos_steps, pos_steps, route.reshape(bsz * s, LANES), x1.reshape(bsz * s, d), mod, mod,
      vec(ln_g), vec(ln_b), ys)
    return out.reshape(bsz, s, d)


def _moe(fx, x1, route, mod, w1, w3, w2, ln_g, ln_b, n_ctx_tokens, alpha):
    bsz, s, d = x1.shape
    n_exp = w1.shape[0]
    idx = route.reshape(bsz * s, LANES)[:, 2:4].astype(jnp.int32)
    row_token, pos, tile_expert, n_used = _route_plan(idx[:, 0], idx[:, 1], n_exp, MOE_TILE)
    ys = _moe_gemm(fx.reshape(bsz * s, d), row_token, tile_expert, n_used, w1, w3, w2, MOE_TILE)
    return _moe_combine(ys, pos, route, x1, mod, ln_g, ln_b, n_ctx_tokens, alpha)


def kernel(x, c, ctx, c_ctx, w_mod, b_mod, w_in, s5_lam_re, s5_lam_im, s5_log_step, s5_b_re, s5_b_im,
           s5_c_re, s5_c_im, s5_d, s5_w_glu, s5_b_glu, attn_sink, ret_log_gamma, w_out,
           ln1_g, ln1_b, ln2_g, ln2_b, ffn_w1, ffn_w3, ffn_w2, moe_router, moe_w1, moe_w3, moe_w2):
    bsz, t_len, d = x.shape
    n_ctx = ctx.shape[1]
    depth = w_in.shape[0]
    alpha = (2 * depth) ** 0.25
    s5_w = s5_d.shape[1]
    att_w = attn_sink.shape[1] * HEAD_DIM
    kv_w = att_w // ATT_REP
    ret_w = ret_log_gamma.shape[2] * HEAD_DIM
    sizes = (s5_w, att_w, kv_w, kv_w, ret_w, ret_w, ret_w, ret_w)
    assert sum(sizes) == w_in.shape[2] and s5_w + att_w + ret_w == w_out.shape[1]
    assert n_ctx % TOKEN_BLOCK == 0 and t_len % TOKEN_BLOCK == 0 and t_len >= 3 * ATT_BLOCK

    pad = (-(bsz + 1)) % 8
    cvec = jnp.concatenate([jnp.zeros((pad, d), F32), c_ctx[None].astype(F32)], axis=0)
    cvec = jnp.concatenate([c.astype(F32), cvec], axis=0)
    mod_all = _modulation(cvec, w_mod.astype(F32), b_mod.astype(F32)).reshape(depth, bsz + pad + 1, 6, d)

    tabs = _rope_tables(t_len, n_ctx)
    s5w_all = jax.vmap(functools.partial(_s5_weights, chunk=S5_CHUNK))(
        s5_lam_re, s5_lam_im, s5_log_step, s5_b_re, s5_b_im, s5_c_re, s5_c_im)
    xz = jnp.concatenate([ctx, x], axis=1).astype(F32)
    for l in range(depth):
        need_ctx = l < depth - 1
        mod = mod_all[l]
        u, qa, va, qr, vr, gr, ka, kr = _inproj(xz, mod, w_in[l].astype(BF16), tabs, sizes, n_ctx)
        y_s5 = _s5_scan(u, tuple(w[l] for w in s5w_all), n_ctx)
        o_att = _attention(qa, ka, va, attn_sink[l], n_ctx, need_ctx)
        o_ret = _retention(qr, kr, vr, gr, ret_log_gamma[l], n_ctx)
        i = l // 2
        router = None if l % 2 == 0 else moe_router[i]
        outs = _outproj(y_s5, u, o_att, o_ret, xz, mod, s5_d[l], s5_w_glu[l], s5_b_glu[l], w_out[l],
                        ln1_g[l], ln1_b[l], router, n_ctx, need_ctx, alpha)
        ctx_tokens = n_ctx if need_ctx else 0
        if l % 2 == 0:
            x1, fx = outs
            xz = _ffn(fx, x1, mod, ffn_w1[i].astype(BF16), ffn_w3[i].astype(BF16), ffn_w2[i].astype(BF16),
                      ln2_g[l], ln2_b[l], ctx_tokens, alpha)
        else:
            x1, fx, route = outs
            xz = _moe(fx, x1, route, mod, moe_w1[i].astype(BF16), moe_w3[i].astype(BF16),
                      moe_w2[i].astype(BF16), ln2_g[l], ln2_b[l], ctx_tokens, alpha)
    return xz if xz.shape[1] == t_len else xz[:, n_ctx:]
```

```python
import functools
import math

import jax
import jax.numpy as jnp
import numpy as np
from jax import lax
from jax.experimental import pallas as pl
from jax.experimental.pallas import tpu as pltpu

F32 = jnp.float32
BF16 = jnp.bfloat16

GRID_W = 64
HEAD_DIM = 64
ATT_REP = 4
WINDOW = 128
ATT_BLOCK = 128
RET_CHUNK = 128
LN_EPS = 1e-5
ROPE_BASE = 10000.0
NEG_INF = -1e30
LOG2_E = math.log2(math.e)

LANES = 128
S5_CHUNK = LANES
TOKEN_BLOCK = 256
INPROJ_BLOCK = 768
FFN_BLOCK = 768
FFN_CHUNK = 512
MOE_CHUNK = 256
MOD_BLOCK = 1536
VMEM_LIMIT = 56 * 1024 * 1024


def _cparams(sem):
    return pltpu.CompilerParams(dimension_semantics=sem, vmem_limit_bytes=VMEM_LIMIT)


def _pick_chunk(total, target):
    best = LANES
    for c in range(LANES, target + 1, LANES):
        if total % c == 0:
            best = c
    return best


def _dot(a, b):
    return jnp.dot(a, b, preferred_element_type=F32)


def _dot_nt(a, b):
    return lax.dot_general(a, b, (((1,), (1,)), ((), ())), preferred_element_type=F32)


def _dot_tn(a, b):
    return lax.dot_general(a, b, (((0,), (0,)), ((), ())), preferred_element_type=F32)


def _split_bf16(a):
    hi = a.astype(BF16)
    lo = (a - hi.astype(F32)).astype(BF16)
    return hi, lo


def _dot3(a, b):
    ah, al = _split_bf16(a)
    bh, bl = _split_bf16(b)
    return _dot(ah, bh) + (_dot(ah, bl) + _dot(al, bh))


def _ln(x):
    mu = jnp.mean(x, axis=-1, keepdims=True)
    xc = x - mu
    var = jnp.mean(xc * xc, axis=-1, keepdims=True)
    return xc * lax.rsqrt(var + LN_EPS)


def _silu(x):
    return x * (1.0 / (1.0 + jnp.exp(-x)))


def _sigmoid(x):
    return 1.0 / (1.0 + jnp.exp(-x))


def _gelu_tanh(x):
    c = math.sqrt(2.0 / math.pi)
    return 0.5 * x * (1.0 + jnp.tanh(c * (x + 0.044715 * (x * x * x))))


def _mod_kernel(c_ref, w_ref, b_ref, o_ref):
    o_ref[...] = _dot3(_silu(c_ref[...]), w_ref[...]) + b_ref[...]


def _modulation(cvec, w_mod, b_mod):
    depth, d, n = w_mod.shape
    rows = cvec.shape[0]
    tn = _pick_chunk(n, MOD_BLOCK)
    return pl.pallas_call(
        _mod_kernel,
        grid=(depth, n // tn),
        in_specs=[pl.BlockSpec((rows, d), lambda l, j: (0, 0)),
                  pl.BlockSpec((None, d, tn), lambda l, j: (l, 0, j)),
                  pl.BlockSpec((None, 1, tn), lambda l, j: (l, 0, j))],
        out_specs=pl.BlockSpec((None, rows, tn), lambda l, j: (l, 0, j)),
        out_shape=jax.ShapeDtypeStruct((depth, rows, n), F32),
        compiler_params=_cparams(("parallel", "parallel")),
        name="modulation",
    )(cvec, w_mod, b_mod.reshape(depth, 1, n))


def _rope_slab(xs, cos, sa, sb, half):
    return xs * cos + pltpu.roll(xs, LANES - half, 1) * sa + pltpu.roll(xs, half, 1) * sb


def _inproj_kernel(x_ref, modx_ref, modz_ref, wc_ref, w_ref, act_ref, ast_ref, rct_ref, rst_ref,
                   ac_ref, asa_ref, asb_ref, rc_ref, rsa_ref, rsb_ref,
                   ut_ref, qat_ref, vat_ref, qrt_ref, vrt_ref, grt_ref, ka_ref, kr_ref, *, n_ctx, c_sizes, sizes):
    tm = x_ref.shape[0]
    is_ctx = pl.program_id(1) * tm + lax.broadcasted_iota(jnp.int32, (tm, 1), 0) < n_ctx
    shift = jnp.where(is_ctx, modz_ref[0:1, :], modx_ref[0:1, :])
    gain = 1.0 + jnp.where(is_ctx, modz_ref[1:2, :], modx_ref[1:2, :])
    h = (_ln(x_ref[...]) * gain + shift).astype(BF16)
    scale = HEAD_DIM ** -0.5

    ct = _dot_nt(wc_ref[...], h)
    c_offs = [int(o) for o in np.concatenate([[0], np.cumsum(c_sizes)])]

    def chan(i):
        return ct[c_offs[i]:c_offs[i + 1]]

    def rope_t(x, cos_ref, sin_ref, part, mul):
        cos_t, sin_t = cos_ref[...], sin_ref[...]
        heads = []
        for hd in range(x.shape[0] // HEAD_DIM):
            xh = x[hd * HEAD_DIM:(hd + 1) * HEAD_DIM]
            pieces = []
            for lo in range(0, HEAD_DIM, 2 * part):
                pieces += [xh[lo + part:lo + 2 * part], xh[lo:lo + part]]
            out = xh * cos_t + jnp.concatenate(pieces, axis=0) * sin_t
            heads.append(out * mul if mul != 1.0 else out)
        return jnp.concatenate(heads, axis=0)

    def put(ref, val):
        for k in range(ref.shape[0]):
            ref[k] = val[:, k * ATT_BLOCK:(k + 1) * ATT_BLOCK].astype(ref.dtype)

    ut_ref[...] = chan(0)
    put(qat_ref, rope_t(chan(1), act_ref, ast_ref, HEAD_DIM // 4, scale * LOG2_E))
    put(vat_ref, chan(2))
    put(qrt_ref, rope_t(chan(3), rct_ref, rst_ref, HEAD_DIM // 2, 1.0))
    put(vrt_ref, chan(4))
    put(grt_ref, chan(5))

    p = _dot(h, w_ref[...])
    ac, asa, asb = ac_ref[...], asa_ref[...], asb_ref[...]
    rc, rsa, rsb = rc_ref[...], rsa_ref[...], rsb_ref[...]
    offs = np.concatenate([[0], np.cumsum(sizes)])

    def rope_cols(i, cos, sa, sb, half, mul):
        blk = p[:, int(offs[i]):int(offs[i + 1])]
        slabs = [_rope_slab(blk[:, s:s + LANES], cos, sa, sb, half) for s in range(0, blk.shape[1], LANES)]
        out = slabs[0] if len(slabs) == 1 else jnp.concatenate(slabs, axis=1)
        return out * mul if mul != 1.0 else out

    ka_ref[...] = rope_cols(0, ac, asa, asb, HEAD_DIM // 4, 1.0).astype(ka_ref.dtype)
    kr_ref[...] = rope_cols(1, rc, rsa, rsb, HEAD_DIM // 2, scale)


def _inproj(xz, mod, w_in, tabs, sizes, n_ctx):
    bsz, s, d = xz.shape
    tm = _pick_chunk(s, INPROJ_BLOCK)
    assert tm % TOKEN_BLOCK == 0, "the output projection reads the S5 arrays in TOKEN_BLOCK pieces"
    nrow = mod.shape[0]
    offs = np.concatenate([[0], np.cumsum(sizes)])
    col = lambda i: w_in[:, int(offs[i]):int(offs[i + 1])]
    c_idx, r_idx = (0, 1, 3, 4, 6, 7), (2, 5)
    c_dtypes = (BF16, BF16, F32, BF16, F32)
    r_dtypes = (BF16, F32)
    c_sizes = tuple(sizes[i] for i in c_idx)
    r_sizes = tuple(sizes[i] for i in r_idx)
    w_c_t = jnp.concatenate([col(i) for i in c_idx], axis=1).T
    w_rest = jnp.concatenate([col(i) for i in r_idx], axis=1)
    sub = tm // ATT_BLOCK
    tok = lambda width: pl.BlockSpec((None, tm, width), lambda b, j: (b, j, 0))
    chan = lambda width: pl.BlockSpec((sub, None, width, ATT_BLOCK), lambda b, j: (j, b, 0, 0))
    tab = pl.BlockSpec((tm, LANES), lambda b, j: (j, 0))
    tab_t = pl.BlockSpec((HEAD_DIM, tm), lambda b, j: (0, j))
    return pl.pallas_call(
        functools.partial(_inproj_kernel, n_ctx=n_ctx, c_sizes=c_sizes, sizes=r_sizes),
        grid=(bsz, s // tm),
        in_specs=[tok(d),
                  pl.BlockSpec((None, 6, d), lambda b, j: (b, 0, 0)),
                  pl.BlockSpec((None, 6, d), lambda b, j: (nrow - 1, 0, 0)),
                  pl.BlockSpec((sum(c_sizes), d), lambda b, j: (0, 0)),
                  pl.BlockSpec((d, sum(r_sizes)), lambda b, j: (0, 0)),
                  tab_t, tab_t, tab_t, tab_t, tab, tab, tab, tab, tab, tab],
        out_specs=[pl.BlockSpec((None, None, c_sizes[0], tm), lambda b, j: (j, b, 0, 0))]
                  + [chan(w) for w in c_sizes[1:]] + [tok(w) for w in r_sizes],
        out_shape=[jax.ShapeDtypeStruct((s // tm, bsz, c_sizes[0], tm), F32)]
                  + [jax.ShapeDtypeStruct((s // ATT_BLOCK, bsz, w, ATT_BLOCK), dt)
                     for w, dt in zip(c_sizes[1:], c_dtypes)]
                  + [jax.ShapeDtypeStruct((bsz, s, w), dt) for w, dt in zip(r_sizes, r_dtypes)],
        compiler_params=_cparams(("parallel", "parallel")),
        name="inproj",
    )(xz, mod, mod, w_c_t, w_rest, *tabs)


def _rope_tables(t_len, n_ctx):
    t = jnp.arange(t_len)
    rows = (t // GRID_W).astype(F32)
    cols = (t % GRID_W).astype(F32)
    pos = t.astype(F32)

    def angles(p, dim):
        inv_freq = ROPE_BASE ** (-jnp.arange(0, dim, 2, dtype=F32) / dim)
        return p[:, None] * inv_freq[None, :]

    def head_tables(angs):
        cos = jnp.concatenate([jnp.concatenate([jnp.cos(a), jnp.cos(a)], -1) for a in angs], -1)
        sa = jnp.concatenate([jnp.concatenate([-jnp.sin(a), jnp.zeros_like(a)], -1) for a in angs], -1)
        sb = jnp.concatenate([jnp.concatenate([jnp.zeros_like(a), jnp.sin(a)], -1) for a in angs], -1)
        rep = LANES // HEAD_DIM
        return tuple(jnp.tile(x, (1, rep)) for x in (cos, sa, sb))

    att = head_tables([angles(rows, HEAD_DIM // 2), angles(cols, HEAD_DIM // 2)])
    ret = head_tables([angles(pos, HEAD_DIM)])
    signed_t = lambda tb: (tb[0][:, :HEAD_DIM].T, (tb[1] + tb[2])[:, :HEAD_DIM].T)

    def with_ctx(tb, axis, is_cos):
        shape = list(tb.shape)
        shape[axis] = n_ctx
        lead = jnp.ones(shape, F32) if is_cos else jnp.zeros(shape, F32)
        return jnp.concatenate([lead, tb], axis=axis)

    chan = signed_t(att) + signed_t(ret)
    toks = att + ret
    return (tuple(with_ctx(tb, 1, k % 2 == 0) for k, tb in enumerate(chan))
            + tuple(with_ctx(tb, 0, k % 3 == 0) for k, tb in enumerate(toks)))


def _s5_kernel(u_hbm, tab_ref, wsf_ref, wsb_ref, wrf_ref, wrb_ref, lam_ref, y_ref,
               lhs_ref, sem, m_ref, acc_ref, cf_ref, cb_ref, pf_ref, pb_ref, *, bsz, nz_chunks):
    r_blk, ch, tm = y_ref.shape
    c = S5_CHUNK
    kpb = tm // c
    n_chunks = (r_blk // bsz) * kpb
    g = pl.program_id(0)
    slot = g % 2

    def fetch(group, s):
        for i in range(ch):
            p, half = divmod(i, 2)
            for k in range(kpb):
                pltpu.make_async_copy(u_hbm.at[:, group * ch + i, pl.ds(k * c, c)],
                                      lhs_ref.at[s, p, pl.ds(k * r_blk, r_blk), pl.ds(half * c, c)],
                                      sem.at[s]).start()

    @pl.when(g == 0)
    def _():
        fetch(0, 0)

    @pl.when(g + 1 < pl.num_programs(0))
    def _():
        fetch(g + 1, 1 - slot)

    pltpu.make_async_copy(lhs_ref.at[1 - slot], lhs_ref.at[slot], sem.at[slot]).wait()
    acc_ref[...] = jnp.zeros_like(acc_ref)
    cf_ref[...] = jnp.zeros_like(cf_ref)
    cb_ref[...] = jnp.zeros_like(cb_ref)

    def pair(p, carry):
        for ih in range(2):
            for o in range(ch):
                lag_row = jnp.broadcast_to(tab_ref[2 * p + ih, o:o + 1, :], (16, 2 * c))
                tab = pltpu.roll(lag_row, 0, 1, stride=1, stride_axis=0).astype(BF16)
                for r2 in range(c // 16):
                    lo = c - 16 * r2
                    m_ref[ih * c + 16 * r2:ih * c + 16 * r2 + 16, o * c:(o + 1) * c] = tab[:, lo:lo + c]
        lhs = lhs_ref[slot, p].astype(BF16)
        acc_ref[...] += _dot(lhs, m_ref[...])
        cf_ref[...] += _dot(lhs, wsf_ref[p])
        cb_ref[...] += _dot(lhs, wsb_ref[p])
        return carry

    lax.fori_loop(0, ch // 2, pair, 0)
    lam = lam_ref[...]
    nst = lam.shape[1] // 2

    def advance(s, a, bc, drive):
        return s * a + pltpu.roll(s, nst, 1) * bc + drive

    def sweep(order, c_ref, p_ref, a, bc):
        s = jnp.zeros((bsz, lam.shape[1]), F32)
        for n in order:
            j, k = divmod(n, kpb)
            rows = slice(k * r_blk + j * bsz, k * r_blk + (j + 1) * bsz)
            p_ref[rows, :] = s
            s = advance(s, a, bc, c_ref[rows, :])

    sweep(list(range(n_chunks)), cf_ref, pf_ref, lam[0:1], lam[1:2])
    order_b = list(range(nz_chunks - 1, -1, -1)) + list(range(n_chunks - 1, nz_chunks - 1, -1))
    sweep(order_b, cb_ref, pb_ref, lam[2:3], lam[3:4])
    acc_ref[...] += (_dot(pf_ref[...].astype(BF16), wrf_ref[...])
                     + _dot(pb_ref[...].astype(BF16), wrb_ref[...]))
    for o in range(ch):
        for k in range(kpb):
            y_ref[:, o, k * c:(k + 1) * c] = acc_ref[k * r_blk:(k + 1) * r_blk, o * c:(o + 1) * c]


def _s5_weights(lam_re, lam_im, log_step, b_re, b_im, c_re, c_im, chunk):
    hp = lax.Precision.HIGHEST
    lam = lax.complex(lam_re.astype(F32), lam_im.astype(F32))
    lam_dt = lam * jnp.exp(log_step.astype(F32))[..., None]
    lam_bar = jnp.exp(lam_dt)
    b_bar = lax.complex(b_re.astype(F32), b_im.astype(F32)) * ((lam_bar - 1.0) / lam)[..., None]
    c_mat = lax.complex(c_re.astype(F32), c_im.astype(F32))
    g, n = lam.shape[1], lam.shape[2]
    ch = b_bar.shape[-1]
    steps = jnp.arange(chunk + 1, dtype=F32)
    pw = jnp.exp(steps[None, :, None, None] * lam_dt[:, None])
    kern = jnp.einsum('zgon,zdgn,zgni->zgdoi', c_mat, pw[:, :chunk], b_bar, precision=hp).real
    zero_lag = kern[0, :, 0] + kern[1, :, 0]
    lag_table = jnp.concatenate([jnp.zeros_like(zero_lag)[:, None], kern[1, :, :0:-1], zero_lag[:, None],
                                 kern[0, :, 1:]], axis=1).transpose(0, 3, 2, 1)

    def state_in(pw_s, b_dir):
        w = pw_s[:, :, :, None] * b_dir[None]
        w = jnp.concatenate([w.real, w.imag], axis=2)
        return w.transpose(1, 3, 0, 2).reshape(g, ch // 2, 2 * chunk, 2 * n)

    def state_out(pw_t, c_dir):
        w = c_dir[None] * pw_t[:, :, None, :]
        w = jnp.concatenate([w.real, -w.imag], axis=3)
        return w.transpose(1, 3, 2, 0).reshape(g, 2 * n, ch * chunk)

    wsf = state_in(pw[0, chunk - 1::-1][:chunk], b_bar[0])
    wsb = state_in(pw[1, :chunk], b_bar[1])
    wrf = state_out(pw[0, 1:chunk + 1], c_mat[0])
    wrb = state_out(pw[1, chunk:0:-1], c_mat[1])
    lam_c = pw[:, chunk]
    rows = []
    for z in range(2):
        rows.append(jnp.concatenate([lam_c[z].real, lam_c[z].real], -1))
        rows.append(jnp.concatenate([-lam_c[z].imag, lam_c[z].imag], -1))
    lam_rows = jnp.stack(rows + [jnp.zeros_like(rows[0])] * 4, axis=1)
    return lag_table, wsf.astype(BF16), wsb.astype(BF16), wrf.astype(BF16), wrb.astype(BF16), lam_rows


def _s5_scan(ut, weights, n_ctx):
    nblk, bsz, width, tm = ut.shape
    tab, wsf, wsb, wrf, wrb, lam_rows = weights
    g, ch = tab.shape[0], tab.shape[1]
    c = S5_CHUNK
    nst2 = wsf.shape[3]
    r_blk = nblk * bsz
    rows = r_blk * (tm // c)
    grp = lambda *shape: pl.BlockSpec((None,) + shape, lambda i: (i,) + (0,) * len(shape))
    tok = pl.BlockSpec((r_blk, ch, tm), lambda i: (0, i, 0))
    y = pl.pallas_call(
        functools.partial(_s5_kernel, bsz=bsz, nz_chunks=n_ctx // c),
        grid=(g,),
        in_specs=[pl.BlockSpec(memory_space=pl.ANY), grp(ch, ch, 2 * c), grp(ch // 2, 2 * c, nst2), grp(ch // 2, 2 * c, nst2),
                  grp(nst2, ch * c), grp(nst2, ch * c), grp(8, nst2)],
        out_specs=tok,
        out_shape=jax.ShapeDtypeStruct((r_blk, width, tm), F32),
        scratch_shapes=[pltpu.VMEM((2, ch // 2, rows, 2 * c), F32), pltpu.SemaphoreType.DMA((2,)),
                        pltpu.VMEM((2 * c, ch * c), BF16),
                        pltpu.VMEM((rows, ch * c), F32)] + [pltpu.VMEM((rows, nst2), F32)] * 4,
        compiler_params=_cparams(("arbitrary",)),
        name="s5_scan",
    )(ut.reshape(r_blk, width, tm), tab, wsf, wsb, wrf, wrb, lam_rows)
    return y.reshape(nblk, bsz, width, tm)


def _attn_kernel(sink_ref, q_ref, k_ref, v_ref, o_ref, *, n_ctx, t_len, q_off, kv_heads):
    qi = pl.program_id(1) + q_off
    nzb = n_ctx // ATT_BLOCK
    band = 3 * ATT_BLOCK
    wide = ATT_REP * ATT_BLOCK
    q = q_ref[...].astype(BF16)
    kc = k_ref[0:n_ctx, :].astype(BF16)
    vc = jnp.concatenate([v_ref[p] for p in range(nzb)], axis=1).astype(BF16)

    def rows(a, i):
        return a[i * HEAD_DIM:(i + 1) * HEAD_DIM]

    def q_group(g):
        qg = jnp.concatenate([rows(q, g * ATT_REP + r) for r in range(ATT_REP)], axis=1)
        zero = jnp.zeros_like(qg)
        return jnp.concatenate([qg if gg == g else zero for gg in range(kv_heads)], axis=0)

    ones_rows = 16

    def v_aug(v, g):
        return jnp.concatenate([rows(v, g), jnp.ones((ones_rows, v.shape[1]), BF16)], axis=0)

    def finish(g, o_aug, extra_den):
        o = o_aug[:HEAD_DIM] / (o_aug[HEAD_DIM:HEAD_DIM + 1] + extra_den)
        for r in range(ATT_REP):
            h = g * ATT_REP + r
            o_ref[h * HEAD_DIM:(h + 1) * HEAD_DIM, :] = o[:, r * ATT_BLOCK:(r + 1) * ATT_BLOCK].astype(o_ref.dtype)

    @pl.when(qi < nzb)
    def _():
        for g in range(kv_heads):
            sink = sink_ref[g:g + 1, :]
            s = _dot(kc, q_group(g))
            m = jnp.maximum(jnp.max(s, axis=0, keepdims=True), sink)
            e = jnp.exp2(s - m)
            finish(g, _dot(v_aug(vc, g), e.astype(BF16)), jnp.exp2(sink - m))

    @pl.when(qi >= nzb)
    def _():
        n = qi - nzb
        start = jnp.clip((n - 1) * ATT_BLOCK, 0, t_len - band)
        kl = k_ref[pl.ds(pl.multiple_of(n_ctx + start, ATT_BLOCK), band), :].astype(BF16)
        p0 = (n_ctx + start) // ATT_BLOCK
        vl = jnp.concatenate([v_ref[p0 + t] for t in range(3)], axis=1).astype(BF16)
        k_pos = start + lax.broadcasted_iota(jnp.int32, (band, wide), 0)
        q_pos = n * ATT_BLOCK + (lax.broadcasted_iota(jnp.int32, (band, wide), 1) & (ATT_BLOCK - 1))
        valid = jnp.abs(k_pos - q_pos) <= WINDOW
        for g in range(kv_heads):
            sink = sink_ref[g:g + 1, :]
            qg = q_group(g)
            s_loc = jnp.where(valid, _dot(kl, qg), NEG_INF)
            s_ctx = _dot(kc, qg)
            m = jnp.maximum(jnp.maximum(jnp.max(s_loc, axis=0, keepdims=True),
                                        jnp.max(s_ctx, axis=0, keepdims=True)), sink)
            e_loc = jnp.exp2(s_loc - m)
            e_ctx = jnp.exp2(s_ctx - m)
            o_aug = _dot(v_aug(vl, g), e_loc.astype(BF16)) + _dot(v_aug(vc, g), e_ctx.astype(BF16))
            finish(g, o_aug, jnp.exp2(sink - m))


def _attention(qat, ka, vat, sink, n_ctx, need_ctx):
    nblk, bsz, qw, _ = qat.shape
    s, kvw = ka.shape[1], ka.shape[2]
    t_len = s - n_ctx
    kv_heads = kvw // HEAD_DIM
    q_off = 0 if need_ctx else n_ctx // ATT_BLOCK
    nq = nblk - q_off
    sink_rows = jnp.repeat(sink.astype(F32).reshape(kv_heads, ATT_REP), ATT_BLOCK, axis=1) * LOG2_E
    return pl.pallas_call(
        functools.partial(_attn_kernel, n_ctx=n_ctx, t_len=t_len, q_off=q_off, kv_heads=kv_heads),
        grid=(bsz, nq),
        in_specs=[pl.BlockSpec(sink_rows.shape, lambda b, j: (0, 0)),
                  pl.BlockSpec((None, None, qw, ATT_BLOCK), lambda b, j: (j + q_off, b, 0, 0)),
                  pl.BlockSpec((None, s, kvw), lambda b, j: (b, 0, 0)),
                  pl.BlockSpec((nblk, None, kvw, ATT_BLOCK), lambda b, j: (0, b, 0, 0))],
        out_specs=pl.BlockSpec((None, None, qw, ATT_BLOCK), lambda b, j: (j, b, 0, 0)),
        out_shape=jax.ShapeDtypeStruct((nq, bsz, qw, ATT_BLOCK), BF16),
        compiler_params=_cparams(("parallel", "arbitrary")),
        name="window_attention",
    )(sink_rows, qat, ka, vat)


def _ret_kernel(lg_ref, q_ref, k_ref, v_ref, g_ref, o_ref, acc_ref, sf_ref, sb_ref, *, nz_chunks, heads):
    c = RET_CHUNK
    n_chunks, w, _ = q_ref.shape
    f32 = lambda a: a.astype(F32)

    def per_head(shape, axis, group, direction):
        owner = lax.broadcasted_iota(jnp.int32, shape, axis) // group
        out = jnp.zeros(shape, F32)
        for h in range(heads):
            out = jnp.where(owner == h, lg_ref[direction, h], out)
        return out

    row_i = f32(lax.broadcasted_iota(jnp.int32, (w, c), 1))
    qw_f = jnp.exp(per_head((w, c), 0, HEAD_DIM, 0) * (row_i + 1.0))
    qw_b = jnp.exp(per_head((w, c), 0, HEAD_DIM, 1) * (c - row_i))
    key_j = f32(lax.broadcasted_iota(jnp.int32, (c, w), 0))
    kw_f = jnp.exp(per_head((c, w), 1, HEAD_DIM, 0) * (c - 1.0 - key_j))
    kw_b = jnp.exp(per_head((c, w), 1, HEAD_DIM, 1) * key_j)
    dec_f = jnp.exp(per_head((w, 1), 0, HEAD_DIM, 0) * c)
    dec_b = jnp.exp(per_head((w, 1), 0, HEAD_DIM, 1) * c)
    wide = heads * c
    diff = f32((lax.broadcasted_iota(jnp.int32, (c, wide), 1) & (c - 1)) - lax.broadcasted_iota(jnp.int32, (c, wide), 0))
    dec_t = jnp.where(diff >= 0, jnp.exp(per_head((c, wide), 1, c, 0) * jnp.maximum(diff, 0.0)),
                      jnp.exp(per_head((c, wide), 1, c, 1) * jnp.maximum(-diff, 0.0)))
    own_wide = (lax.broadcasted_iota(jnp.int32, (w, wide), 0) // HEAD_DIM
                == lax.broadcasted_iota(jnp.int32, (w, wide), 1) // c)
    own_sq = f32(lax.broadcasted_iota(jnp.int32, (w, w), 0) // HEAD_DIM
                 == lax.broadcasted_iota(jnp.int32, (w, w), 1) // HEAD_DIM)
    sf_ref[...] = jnp.zeros_like(sf_ref)
    sb_ref[...] = jnp.zeros_like(sb_ref)

    def block_diag(x):
        return jnp.where(own_wide, jnp.concatenate([x] * heads, axis=1), 0.0).astype(BF16)

    def fwd(n, carry):
        qt, vt = q_ref[n], v_ref[n]
        k = k_ref[pl.ds(pl.multiple_of(n * c, c), c), :]
        scores_t = _dot(k.astype(BF16), block_diag(qt)) * dec_t
        stacked = jnp.concatenate([scores_t[:, h * c:(h + 1) * c] for h in range(heads)], axis=0)
        o = _dot(block_diag(vt), stacked.astype(BF16))
        s_prev = sf_ref[...]
        o = o + _dot(s_prev.astype(BF16), (qt * qw_f).astype(BF16))
        sf_ref[...] = dec_f * s_prev + own_sq * _dot(vt.astype(BF16), (k * kw_f).astype(BF16))
        acc_ref[n] = o
        return carry

    lax.fori_loop(0, n_chunks, fwd, 0)

    def bwd(i, carry):
        n = jnp.where(i < nz_chunks, nz_chunks - 1 - i, n_chunks - 1 - i + nz_chunks)
        qt, vt = q_ref[n], v_ref[n]
        k = k_ref[pl.ds(pl.multiple_of(n * c, c), c), :]
        s_prev = sb_ref[...]
        o = acc_ref[n] + _dot(s_prev.astype(BF16), (qt * qw_b).astype(BF16))
        sb_ref[...] = dec_b * s_prev + own_sq * _dot(vt.astype(BF16), (k * kw_b).astype(BF16))
        gate = _silu(g_ref[n])
        for h in range(heads):
            rows = slice(h * HEAD_DIM, (h + 1) * HEAD_DIM)
            oh = o[rows]
            mu = jnp.mean(oh, axis=0, keepdims=True)
            oc = oh - mu
            var = jnp.mean(oc * oc, axis=0, keepdims=True)
            o_ref[n, rows, :] = (oc * lax.rsqrt(var + LN_EPS) * gate[rows]).astype(o_ref.dtype)
        return carry

    lax.fori_loop(0, n_chunks, bwd, 0)


def _retention(qrt, kr, vrt, grt, log_gamma, n_ctx):
    n_chunks, bsz, w, c = qrt.shape
    s = kr.shape[1]
    chan = pl.BlockSpec((n_chunks, None, w, c), lambda b: (0, b, 0, 0))
    return pl.pallas_call(
        functools.partial(_ret_kernel, nz_chunks=n_ctx // RET_CHUNK, heads=w // HEAD_DIM),
        grid=(bsz,),
        in_specs=[pl.BlockSpec(memory_space=pltpu.SMEM), chan, pl.BlockSpec((None, s, w), lambda b: (b, 0, 0)),
                  chan, chan],
        out_specs=chan,
        out_shape=jax.ShapeDtypeStruct((n_chunks, bsz, w, c), BF16),
        scratch_shapes=[pltpu.VMEM((n_chunks, w, c), F32), pltpu.VMEM((w, w), F32), pltpu.VMEM((w, w), F32)],
        compiler_params=_cparams(("parallel",)),
        name="retention",
    )(log_gamma.astype(F32), qrt, kr, vrt, grt)


def _outproj_kernel(*refs, nzb, alpha, with_router, w_s5, w_att, n_exp):
    if with_router:
        (y_ref, u_ref, a_ref, r_ref, x_ref, modx_ref, modz_ref, d_ref, wg_ref, bg_ref, wo_ref,
         g1_ref, b1_ref, rt_ref, x1_ref, fx_ref, gate_ref) = refs
    else:
        (y_ref, u_ref, a_ref, r_ref, x_ref, modx_ref, modz_ref, d_ref, wg_ref, bg_ref, wo_ref,
         g1_ref, b1_ref, x1_ref, fx_ref) = refs
    is_ctx = pl.program_id(1) < nzb
    mod = jnp.where(is_ctx, modz_ref[...], modx_ref[...])
    g = _gelu_tanh(y_ref[...] + d_ref[...] * u_ref[...])
    s5 = g * _sigmoid(_dot(wg_ref[...], g.astype(BF16)) + bg_ref[...])
    def chan_major(ref, lo, hi):
        return jnp.concatenate([_dot_tn(ref[k].astype(BF16), wo_ref[lo:hi, :]) for k in range(ref.shape[0])],
                               axis=0)

    mix = (_dot_tn(s5.astype(BF16), wo_ref[0:w_s5, :]) + chan_major(a_ref, w_s5, w_s5 + w_att)
           + chan_major(r_ref, w_s5 + w_att, wo_ref.shape[0]))
    x1 = _ln(alpha * x_ref[...] + mod[2:3] * mix) * g1_ref[...] + b1_ref[...]
    x1_ref[...] = x1
    fx = _ln(x1) * (1.0 + mod[4:5]) + mod[3:4]
    fx_ref[...] = fx.astype(fx_ref.dtype)
    if with_router:
        lane = lax.broadcasted_iota(jnp.int32, (fx.shape[0], LANES), 1)
        logits = jnp.where(lane < n_exp, _dot3(fx, rt_ref[...]), -jnp.inf)
        m1 = jnp.max(logits, axis=1, keepdims=True)
        i1 = jnp.min(jnp.where(logits == m1, lane, LANES), axis=1, keepdims=True)
        rest = jnp.where(lane == i1, -jnp.inf, logits)
        m2 = jnp.max(rest, axis=1, keepdims=True)
        i2 = jnp.min(jnp.where(rest == m2, lane, LANES), axis=1, keepdims=True)
        e2 = jnp.exp(m2 - m1)
        den = 1.0 + e2
        route = jnp.where(lane == 0, 1.0 / den, jnp.where(lane == 1, e2 / den, 0.0))
        route = jnp.where(lane == 2, i1.astype(F32), jnp.where(lane == 3, i2.astype(F32), route))
        gate_ref[...] = route


def _outproj(y_s5, u, o_att, o_ret, xz, mod, s5_d, w_glu, b_glu, w_out, ln_g, ln_b, router, n_ctx, need_ctx,
             alpha):
    bsz, s, d = xz.shape
    tm = TOKEN_BLOCK
    nzb = n_ctx // tm
    off = 0 if need_ctx else nzb
    nblk = s // tm - off
    s_out = nblk * tm
    nrow = mod.shape[0]
    w_s5, w_att = y_s5.shape[2], o_att.shape[2]
    sub = tm // ATT_BLOCK
    att_off = off if o_att.shape[0] == s // ATT_BLOCK else 0
    att = pl.BlockSpec((sub, None, w_att, ATT_BLOCK), lambda b, j: (j + att_off, b, 0, 0))

    def tok(width, shift):
        return pl.BlockSpec((None, tm, width), lambda b, j: (b, j + shift, 0))

    def full(a):
        return pl.BlockSpec(a.shape, lambda b, j: (0,) * a.ndim)

    per_in = y_s5.shape[3] // tm
    chan = pl.BlockSpec((None, None, w_s5, tm), lambda b, j: ((j + off) // per_in, b, 0, (j + off) % per_in))
    vec = lambda a: a.reshape(1, -1).astype(F32)
    col = lambda a: a.reshape(-1, 1).astype(F32)
    consts = [col(s5_d), w_glu.T.astype(BF16), col(b_glu), w_out.astype(BF16), vec(ln_g), vec(ln_b)]
    ret = pl.BlockSpec((sub, None, o_ret.shape[2], ATT_BLOCK), lambda b, j: (j + off, b, 0, 0))
    in_specs = [chan, chan, att, ret, tok(d, off),
                pl.BlockSpec((None, 6, d), lambda b, j: (b, 0, 0)),
                pl.BlockSpec((None, 6, d), lambda b, j: (nrow - 1, 0, 0))] + [full(a) for a in consts]
    out_specs = [tok(d, 0), tok(d, 0)]
    fx_dtype = BF16 if router is None else F32
    out_shape = [jax.ShapeDtypeStruct((bsz, s_out, d), F32), jax.ShapeDtypeStruct((bsz, s_out, d), fx_dtype)]
    args = [y_s5, u, o_att, o_ret, xz, mod, mod] + consts
    with_router = router is not None
    n_exp = 0
    if with_router:
        n_exp = router.shape[1]
        router_pad = jnp.pad(router.astype(F32), ((0, 0), (0, LANES - n_exp)))
        args.append(router_pad)
        in_specs.append(full(router_pad))
        out_specs.append(tok(LANES, 0))
        out_shape.append(jax.ShapeDtypeStruct((bsz, s_out, LANES), F32))
    return pl.pallas_call(
        functools.partial(_outproj_kernel, nzb=nzb - off, alpha=alpha, with_router=with_router,
                          w_s5=w_s5, w_att=w_att, n_exp=n_exp),
        grid=(bsz, nblk),
        in_specs=in_specs,
        out_specs=out_specs,
        out_shape=out_shape,
        compiler_params=_cparams(("parallel", "parallel")),
        name="outproj",
    )(*args)


def _swiglu_into(x_bf16, w1_ref, w3_ref, w2_ref, acc_ref, fc):
    for s in range(0, w1_ref.shape[1], fc):
        h1 = _dot(x_bf16, w1_ref[:, s:s + fc])
        h3 = _dot(x_bf16, w3_ref[:, s:s + fc])
        acc_ref[...] += _dot((_silu(h1) * h3).astype(BF16), w2_ref[s:s + fc, :])


def _ffn_kernel(fx_ref, x1_ref, modx_ref, modz_ref, w1_ref, w3_ref, w2_ref, g2_ref, b2_ref,
                o_ref, acc_ref, *, n_ctx_tokens, alpha, fc):
    tm = acc_ref.shape[0]
    acc_ref[...] = jnp.zeros_like(acc_ref)
    _swiglu_into(fx_ref[...], w1_ref, w3_ref, w2_ref, acc_ref, fc)
    row = pl.program_id(1) * tm + lax.broadcasted_iota(jnp.int32, (tm, 1), 0)
    gate = jnp.where(row < n_ctx_tokens, modz_ref[5:6, :], modx_ref[5:6, :])
    o_ref[...] = _ln(alpha * x1_ref[...] + gate * acc_ref[...]) * g2_ref[...] + b2_ref[...]


def _ffn(fx, x1, mod, w1, w3, w2, ln_g, ln_b, n_ctx_tokens, alpha):
    bsz, s, d = x1.shape
    ff = w1.shape[1]
    tm = _pick_chunk(s, FFN_BLOCK)
    nrow = mod.shape[0]
    tok = pl.BlockSpec((None, tm, d), lambda b, j: (b, j, 0))
    full = lambda a: pl.BlockSpec(a.shape, lambda b, j: (0,) * a.ndim, pipeline_mode=pl.Buffered(1))
    vec = lambda a: a.reshape(1, -1).astype(F32)
    return pl.pallas_call(
        functools.partial(_ffn_kernel, n_ctx_tokens=n_ctx_tokens, alpha=alpha, fc=_pick_chunk(ff, FFN_CHUNK)),
        grid=(bsz, s // tm),
        in_specs=[tok, tok,
                  pl.BlockSpec((None, 6, d), lambda b, j: (b, 0, 0)),
                  pl.BlockSpec((None, 6, d), lambda b, j: (nrow - 1, 0, 0)),
                  full(w1), full(w3), full(w2),
                  pl.BlockSpec((1, d), lambda b, j: (0, 0)), pl.BlockSpec((1, d), lambda b, j: (0, 0))],
        out_specs=tok,
        out_shape=jax.ShapeDtypeStruct((bsz, s, d), F32),
        scratch_shapes=[pltpu.VMEM((tm, d), F32)],
        compiler_params=_cparams(("parallel", "parallel")),
        name="dense_ffn",
    )(fx, x1, mod, mod, w1, w3, w2, vec(ln_g), vec(ln_b))


MOE_TILE = 1024


def _route_plan(e1, e2, n_exp, tile):
    n = e1.shape[0]
    pair_e = jnp.stack([e1, e2], axis=1).reshape(-1)
    onehot = (pair_e[:, None] == jnp.arange(n_exp, dtype=jnp.int32)[None, :]).astype(jnp.int32)
    before = jnp.cumsum(onehot, axis=0) - onehot
    rank = jnp.sum(before * onehot, axis=1)
    counts = jnp.sum(onehot, axis=0)
    padded = (counts + tile - 1) // tile * tile
    ends = jnp.cumsum(padded)
    starts = ends - padded
    dest = starts[pair_e] + rank
    n_rows = (2 * n + n_exp * (tile - 1)) // tile * tile
    n_tiles = n_rows // tile
    tile_start = jnp.arange(n_tiles, dtype=jnp.int32) * tile
    tile_expert = jnp.minimum(jnp.sum((tile_start[:, None] >= ends[None, :]).astype(jnp.int32), axis=1),
                              n_exp - 1)
    by_expert = jnp.sort(pair_e * (2 * n) + jnp.arange(2 * n, dtype=jnp.int32)) % (2 * n)
    row = jnp.arange(n_rows, dtype=jnp.int32)
    row_e = jnp.repeat(tile_expert, tile)
    in_group = row - starts[row_e]
    src = jnp.minimum(in_group + (jnp.cumsum(counts) - counts)[row_e], 2 * n - 1)
    row_token = jnp.where(in_group < counts[row_e], by_expert[src] // 2, 0)
    n_used = (ends[-1] // tile).astype(jnp.int32).reshape(1)
    return row_token, dest.reshape(n, 2), tile_expert, n_used


def _moe_gemm_kernel(te_ref, nused_ref, tok_ref, tokn_ref, x_hbm, w1_ref, w3_ref, w2_ref, y_ref,
                     xbuf_ref, sem, *, fc, tile):
    i = pl.program_id(0)
    f = pl.program_id(1)
    slot = i % 2
    n_used = nused_ref[0]

    def row_copy(t_ref, r, s):
        return pltpu.make_async_copy(x_hbm.at[pl.ds(t_ref[0, r], 1)], xbuf_ref.at[s, pl.ds(r, 1)], sem.at[s])

    def issue(t_ref, s):
        def body(it, carry):
            base = pl.multiple_of(it * 8, 8)
            for k in range(8):
                row_copy(t_ref, base + k, s).start()
            return carry
        lax.fori_loop(0, tile // 8, body, 0)

    @pl.when(f == 0)
    def _():
        y_ref[...] = jnp.zeros_like(y_ref)

        @pl.when(i == 0)
        def _():
            issue(tok_ref, 0)

        @pl.when(i + 1 < n_used)
        def _():
            issue(tokn_ref, 1 - slot)

        @pl.when(i < n_used)
        def _():
            pltpu.make_async_copy(x_hbm.at[pl.ds(0, tile)], xbuf_ref.at[slot], sem.at[slot]).wait()

    @pl.when(i < n_used)
    def _():
        _swiglu_into(xbuf_ref[slot].astype(BF16), w1_ref, w3_ref, w2_ref, y_ref, fc)


def _moe_gemm(x_flat, row_token, tile_expert, n_used, w1, w3, w2, tile):
    n_rows = row_token.shape[0]
    n_tiles = n_rows // tile
    d = x_flat.shape[1]
    ff = w1.shape[2]
    nf = 2 if ff % (2 * LANES) == 0 else 1
    tf = ff // nf
    last = lambda i, nu: jnp.minimum(i, nu[0] - 1)
    fsel = lambda i, f, nu: jnp.where(i < nu[0], f, nf - 1)
    tok = lambda imap: pl.BlockSpec((None, 1, tile), imap, memory_space=pltpu.SMEM)
    tokens = row_token.reshape(n_tiles, 1, tile)
    return pl.pallas_call(
        functools.partial(_moe_gemm_kernel, fc=_pick_chunk(tf, MOE_CHUNK), tile=tile),
        grid_spec=pltpu.PrefetchScalarGridSpec(
            num_scalar_prefetch=2,
            grid=(n_tiles, nf),
            in_specs=[tok(lambda i, f, te, nu: (i, 0, 0)),
                      tok(lambda i, f, te, nu: (jnp.minimum(i + 1, n_tiles - 1), 0, 0)),
                      pl.BlockSpec(memory_space=pl.ANY),
                      pl.BlockSpec((None, d, tf), lambda i, f, te, nu: (te[last(i, nu)], 0, fsel(i, f, nu))),
                      pl.BlockSpec((None, d, tf), lambda i, f, te, nu: (te[last(i, nu)], 0, fsel(i, f, nu))),
                      pl.BlockSpec((None, tf, d), lambda i, f, te, nu: (te[last(i, nu)], fsel(i, f, nu), 0))],
            out_specs=pl.BlockSpec((tile, d), lambda i, f, te, nu: (i, 0)),
            scratch_shapes=[pltpu.VMEM((2, tile, d), F32), pltpu.SemaphoreType.DMA((2,))]),
        out_shape=jax.ShapeDtypeStruct((n_rows, d), F32),
        compiler_params=_cparams(("arbitrary", "arbitrary")),
        name="moe_gemm",
    )(tile_expert, n_used, tokens, tokens, x_flat, w1, w3, w2)


def _moe_combine_kernel(pos_ref, posn_ref, route_ref, x1_ref, modx_ref, modz_ref, g2_ref, b2_ref, y_hbm,
                        o_ref, buf_ref, sem, *, nzb, nblk, alpha, tm):
    i = pl.program_id(0)
    slot = i % 2

    def row_copy(p_ref, r, k, s):
        return pltpu.make_async_copy(y_hbm.at[pl.ds(p_ref[0, k * tm + r], 1)],
                                     buf_ref.at[s, pl.ds(k * tm + r, 1)], sem.at[s])

    def issue(p_ref, s):
        def body(r, carry):
            row_copy(p_ref, r, 0, s).start()
            row_copy(p_ref, r, 1, s).start()
            return carry
        lax.fori_loop(0, tm, body, 0, unroll=8)

    @pl.when(i == 0)
    def _():
        issue(pos_ref, 0)

    @pl.when(i + 1 < pl.num_programs(0))
    def _():
        issue(posn_ref, 1 - slot)

    pltpu.make_async_copy(y_hbm.at[pl.ds(0, 2 * tm)], buf_ref.at[slot], sem.at[slot]).wait()
    route = route_ref[...]
    f = route[:, 0:1] * buf_ref[slot, 0:tm] + route[:, 1:2] * buf_ref[slot, tm:2 * tm]
    is_ctx = (i % nblk) < nzb
    mod = jnp.where(is_ctx, modz_ref[...], modx_ref[...])
    o_ref[...] = _ln(alpha * x1_ref[...] + mod[5:6] * f) * g2_ref[...] + b2_ref[...]


def _moe_combine(ys, pos, route, x1, mod, ln_g, ln_b, n_ctx_tokens, alpha):
    bsz, s, d = x1.shape
    tm = TOKEN_BLOCK
    nblk = s // tm
    n_steps = bsz * nblk
    nrow = mod.shape[0]
    pos_steps = pos.reshape(n_steps, tm, 2).transpose(0, 2, 1).reshape(n_steps, 1, 2 * tm)
    tok = lambda width: pl.BlockSpec((tm, width), lambda i: (i, 0))
    smem = lambda imap: pl.BlockSpec((None, 1, 2 * tm), imap, memory_space=pltpu.SMEM)
    vec = lambda a: a.reshape(1, -1).astype(F32)
    out = pl.pallas_call(
        functools.partial(_moe_combine_kernel, nzb=n_ctx_tokens // tm, nblk=nblk, alpha=alpha, tm=tm),
        grid=(n_steps,),
        in_specs=[smem(lambda i: (i, 0, 0)),
                  smem(lambda i: (jnp.minimum(i + 1, n_steps - 1), 0, 0)),
                  tok(LANES), tok(d),
                  pl.BlockSpec((None, 6, d), lambda i: (i // nblk, 0, 0)),
                  pl.BlockSpec((None, 6, d), lambda i: (nrow - 1, 0, 0)),
                  pl.BlockSpec((1, d), lambda i: (0, 0)), pl.BlockSpec((1, d), lambda i: (0, 0)),
                  pl.BlockSpec(memory_space=pl.ANY)],
        out_specs=tok(d),
        out_shape=jax.ShapeDtypeStruct((bsz * s, d), F32),
        scratch_shapes=[pltpu.VMEM((2, 2 * tm, d), F32), pltpu.SemaphoreType.DMA((2,))],
        compiler_params=_cparams(("arbitrary",)),
        name="moe_combine",
    )(pos_steps, pos_steps, route.reshape(bsz * s, LANES), x1.reshape(bsz * s, d), mod, mod,
      vec(ln_g), vec(ln_b), ys)
    return out.reshape(bsz, s, d)


def _moe(fx, x1, route, mod, w1, w3, w2, ln_g, ln_b, n_ctx_tokens, alpha):
    bsz, s, d = x1.shape
    n_exp = w1.shape[0]
    idx = route.reshape(bsz * s, LANES)[:, 2:4].astype(jnp.int32)
    row_token, pos, tile_expert, n_used = _route_plan(idx[:, 0], idx[:, 1], n_exp, MOE_TILE)
    ys = _moe_gemm(fx.reshape(bsz * s, d), row_token, tile_expert, n_used, w1, w3, w2, MOE_TILE)
    return _moe_combine(ys, pos, route, x1, mod, ln_g, ln_b, n_ctx_tokens, alpha)


def kernel(x, c, ctx, c_ctx, w_mod, b_mod, w_in, s5_lam_re, s5_lam_im, s5_log_step, s5_b_re, s5_b_im,
           s5_c_re, s5_c_im, s5_d, s5_w_glu, s5_b_glu, attn_sink, ret_log_gamma, w_out,
           ln1_g, ln1_b, ln2_g, ln2_b, ffn_w1, ffn_w3, ffn_w2, moe_router, moe_w1, moe_w3, moe_w2):
    bsz, t_len, d = x.shape
    n_ctx = ctx.shape[1]
    depth = w_in.shape[0]
    alpha = (2 * depth) ** 0.25
    s5_w = s5_d.shape[1]
    att_w = attn_sink.shape[1] * HEAD_DIM
    kv_w = att_w // ATT_REP
    ret_w = ret_log_gamma.shape[2] * HEAD_DIM
    sizes = (s5_w, att_w, kv_w, kv_w, ret_w, ret_w, ret_w, ret_w)
    assert sum(sizes) == w_in.shape[2] and s5_w + att_w + ret_w == w_out.shape[1]
    assert n_ctx % TOKEN_BLOCK == 0 and t_len % TOKEN_BLOCK == 0 and t_len >= 3 * ATT_BLOCK

    pad = (-(bsz + 1)) % 8
    cvec = jnp.concatenate([jnp.zeros((pad, d), F32), c_ctx[None].astype(F32)], axis=0)
    cvec = jnp.concatenate([c.astype(F32), cvec], axis=0)
    mod_all = _modulation(cvec, w_mod.astype(F32), b_mod.astype(F32)).reshape(depth, bsz + pad + 1, 6, d)

    tabs = _rope_tables(t_len, n_ctx)
    xz = jnp.concatenate([ctx, x], axis=1).astype(F32)
    for l in range(depth):
        need_ctx = l < depth - 1
        mod = mod_all[l]
        u, qa, va, qr, vr, gr, ka, kr = _inproj(xz, mod, w_in[l].astype(BF16), tabs, sizes, n_ctx)
        s5w = _s5_weights(s5_lam_re[l], s5_lam_im[l], s5_log_step[l], s5_b_re[l], s5_b_im[l],
                          s5_c_re[l], s5_c_im[l], S5_CHUNK)
        y_s5 = _s5_scan(u, s5w, n_ctx)
        o_att = _attention(qa, ka, va, attn_sink[l], n_ctx, need_ctx)
        o_ret = _retention(qr, kr, vr, gr, ret_log_gamma[l], n_ctx)
        i = l // 2
        router = None if l % 2 == 0 else moe_router[i]
        outs = _outproj(y_s5, u, o_att, o_ret, xz, mod, s5_d[l], s5_w_glu[l], s5_b_glu[l], w_out[l],
                        ln1_g[l], ln1_b[l], router, n_ctx, need_ctx, alpha)
        ctx_tokens = n_ctx if need_ctx else 0
        if l % 2 == 0:
            x1, fx = outs
            xz = _ffn(fx, x1, mod, ffn_w1[i].astype(BF16), ffn_w3[i].astype(BF16), ffn_w2[i].astype(BF16),
                      ln2_g[l], ln2_b[l], ctx_tokens, alpha)
        else:
            x1, fx, route = outs
            xz = _moe(fx, x1, route, mod, moe_w1[i].astype(BF16), moe_w3[i].astype(BF16),
                      moe_w2[i].astype(BF16), ln2_g[l], ln2_b[l], ctx_tokens, alpha)
    return xz if xz.shape[1] == t_len else xz[:, n_ctx:]
```

```python
import functools
import math

import jax
import jax.numpy as jnp
import numpy as np
from jax import lax
from jax.experimental import pallas as pl
from jax.experimental.pallas import tpu as pltpu

F32 = jnp.float32
BF16 = jnp.bfloat16

GRID_W = 64
HEAD_DIM = 64
ATT_REP = 4
WINDOW = 128
ATT_BLOCK = 128
RET_CHUNK = 128
LN_EPS = 1e-5
ROPE_BASE = 10000.0
NEG_INF = -1e30
LOG2_E = math.log2(math.e)

LANES = 128
S5_CHUNK = LANES
TOKEN_BLOCK = 256
INPROJ_BLOCK = 768
FFN_BLOCK = 768
FFN_CHUNK = 512
MOE_CHUNK = 256
MOD_BLOCK = 1536
VMEM_LIMIT = 56 * 1024 * 1024


def _cparams(sem):
    return pltpu.CompilerParams(dimension_semantics=sem, vmem_limit_bytes=VMEM_LIMIT)


def _pick_chunk(total, target):
    best = LANES
    for c in range(LANES, target + 1, LANES):
        if total % c == 0:
            best = c
    return best


def _dot(a, b):
    return jnp.dot(a, b, preferred_element_type=F32)


def _dot_nt(a, b):
    return lax.dot_general(a, b, (((1,), (1,)), ((), ())), preferred_element_type=F32)


def _dot_tn(a, b):
    return lax.dot_general(a, b, (((0,), (0,)), ((), ())), preferred_element_type=F32)


def _split_bf16(a):
    hi = a.astype(BF16)
    lo = (a - hi.astype(F32)).astype(BF16)
    return hi, lo


def _dot3(a, b):
    ah, al = _split_bf16(a)
    bh, bl = _split_bf16(b)
    return _dot(ah, bh) + (_dot(ah, bl) + _dot(al, bh))


def _ln(x):
    mu = jnp.mean(x, axis=-1, keepdims=True)
    xc = x - mu
    var = jnp.mean(xc * xc, axis=-1, keepdims=True)
    return xc * lax.rsqrt(var + LN_EPS)


def _silu(x):
    return x * (1.0 / (1.0 + jnp.exp(-x)))


def _sigmoid(x):
    return 1.0 / (1.0 + jnp.exp(-x))


def _gelu_tanh(x):
    c = math.sqrt(2.0 / math.pi)
    return 0.5 * x * (1.0 + jnp.tanh(c * (x + 0.044715 * (x * x * x))))


def _mod_kernel(c_ref, w_ref, b_ref, o_ref):
    o_ref[...] = _dot3(_silu(c_ref[...]), w_ref[...]) + b_ref[...]


def _modulation(cvec, w_mod, b_mod):
    depth, d, n = w_mod.shape
    rows = cvec.shape[0]
    tn = _pick_chunk(n, MOD_BLOCK)
    return pl.pallas_call(
        _mod_kernel,
        grid=(depth, n // tn),
        in_specs=[pl.BlockSpec((rows, d), lambda l, j: (0, 0)),
                  pl.BlockSpec((None, d, tn), lambda l, j: (l, 0, j)),
                  pl.BlockSpec((None, 1, tn), lambda l, j: (l, 0, j))],
        out_specs=pl.BlockSpec((None, rows, tn), lambda l, j: (l, 0, j)),
        out_shape=jax.ShapeDtypeStruct((depth, rows, n), F32),
        compiler_params=_cparams(("parallel", "parallel")),
        name="modulation",
    )(cvec, w_mod, b_mod.reshape(depth, 1, n))


def _rope_slab(xs, cos, sa, sb, half):
    return xs * cos + pltpu.roll(xs, LANES - half, 1) * sa + pltpu.roll(xs, half, 1) * sb


def _inproj_kernel(x_ref, modx_ref, modz_ref, wc_ref, w_ref, act_ref, ast_ref, rct_ref, rst_ref,
                   ac_ref, asa_ref, asb_ref, rc_ref, rsa_ref, rsb_ref,
                   ut_ref, qat_ref, vat_ref, qrt_ref, vrt_ref, grt_ref, ka_ref, kr_ref, *, n_ctx, c_sizes, sizes):
    tm = x_ref.shape[0]
    is_ctx = pl.program_id(1) * tm + lax.broadcasted_iota(jnp.int32, (tm, 1), 0) < n_ctx
    shift = jnp.where(is_ctx, modz_ref[0:1, :], modx_ref[0:1, :])
    gain = 1.0 + jnp.where(is_ctx, modz_ref[1:2, :], modx_ref[1:2, :])
    h = (_ln(x_ref[...]) * gain + shift).astype(BF16)
    scale = HEAD_DIM ** -0.5

    ct = _dot_nt(wc_ref[...], h)
    c_offs = [int(o) for o in np.concatenate([[0], np.cumsum(c_sizes)])]

    def chan(i):
        return ct[c_offs[i]:c_offs[i + 1]]

    def rope_t(x, cos_ref, sin_ref, part, mul):
        cos_t, sin_t = cos_ref[...], sin_ref[...]
        heads = []
        for hd in range(x.shape[0] // HEAD_DIM):
            xh = x[hd * HEAD_DIM:(hd + 1) * HEAD_DIM]
            pieces = []
            for lo in range(0, HEAD_DIM, 2 * part):
                pieces += [xh[lo + part:lo + 2 * part], xh[lo:lo + part]]
            out = xh * cos_t + jnp.concatenate(pieces, axis=0) * sin_t
            heads.append(out * mul if mul != 1.0 else out)
        return jnp.concatenate(heads, axis=0)

    def put(ref, val):
        for k in range(ref.shape[0]):
            ref[k] = val[:, k * ATT_BLOCK:(k + 1) * ATT_BLOCK].astype(ref.dtype)

    ut_ref[...] = chan(0)
    put(qat_ref, rope_t(chan(1), act_ref, ast_ref, HEAD_DIM // 4, scale * LOG2_E))
    put(vat_ref, chan(2))
    put(qrt_ref, rope_t(chan(3), rct_ref, rst_ref, HEAD_DIM // 2, 1.0))
    put(vrt_ref, chan(4))
    put(grt_ref, chan(5))

    p = _dot(h, w_ref[...])
    ac, asa, asb = ac_ref[...], asa_ref[...], asb_ref[...]
    rc, rsa, rsb = rc_ref[...], rsa_ref[...], rsb_ref[...]
    offs = np.concatenate([[0], np.cumsum(sizes)])

    def rope_cols(i, cos, sa, sb, half, mul):
        blk = p[:, int(offs[i]):int(offs[i + 1])]
        slabs = [_rope_slab(blk[:, s:s + LANES], cos, sa, sb, half) for s in range(0, blk.shape[1], LANES)]
        out = slabs[0] if len(slabs) == 1 else jnp.concatenate(slabs, axis=1)
        return out * mul if mul != 1.0 else out

    ka_ref[...] = rope_cols(0, ac, asa, asb, HEAD_DIM // 4, 1.0).astype(ka_ref.dtype)
    kr_ref[...] = rope_cols(1, rc, rsa, rsb, HEAD_DIM // 2, scale)


def _inproj(xz, mod, w_in, tabs, sizes, n_ctx):
    bsz, s, d = xz.shape
    tm = _pick_chunk(s, INPROJ_BLOCK)
    assert tm % TOKEN_BLOCK == 0, "the output projection reads the S5 arrays in TOKEN_BLOCK pieces"
    nrow = mod.shape[0]
    offs = np.concatenate([[0], np.cumsum(sizes)])
    col = lambda i: w_in[:, int(offs[i]):int(offs[i + 1])]
    c_idx, r_idx = (0, 1, 3, 4, 6, 7), (2, 5)
    c_dtypes = (BF16, BF16, F32, BF16, F32)
    r_dtypes = (BF16, F32)
    c_sizes = tuple(sizes[i] for i in c_idx)
    r_sizes = tuple(sizes[i] for i in r_idx)
    w_c_t = jnp.concatenate([col(i) for i in c_idx], axis=1).T
    w_rest = jnp.concatenate([col(i) for i in r_idx], axis=1)
    sub = tm // ATT_BLOCK
    tok = lambda width: pl.BlockSpec((None, tm, width), lambda b, j: (b, j, 0))
    chan = lambda width: pl.BlockSpec((sub, None, width, ATT_BLOCK), lambda b, j: (j, b, 0, 0))
    tab = pl.BlockSpec((tm, LANES), lambda b, j: (j, 0))
    tab_t = pl.BlockSpec((HEAD_DIM, tm), lambda b, j: (0, j))
    return pl.pallas_call(
        functools.partial(_inproj_kernel, n_ctx=n_ctx, c_sizes=c_sizes, sizes=r_sizes),
        grid=(bsz, s // tm),
        in_specs=[tok(d),
                  pl.BlockSpec((None, 6, d), lambda b, j: (b, 0, 0)),
                  pl.BlockSpec((None, 6, d), lambda b, j: (nrow - 1, 0, 0)),
                  pl.BlockSpec((sum(c_sizes), d), lambda b, j: (0, 0)),
                  pl.BlockSpec((d, sum(r_sizes)), lambda b, j: (0, 0)),
                  tab_t, tab_t, tab_t, tab_t, tab, tab, tab, tab, tab, tab],
        out_specs=[pl.BlockSpec((None, None, c_sizes[0], tm), lambda b, j: (j, b, 0, 0))]
                  + [chan(w) for w in c_sizes[1:]] + [tok(w) for w in r_sizes],
        out_shape=[jax.ShapeDtypeStruct((s // tm, bsz, c_sizes[0], tm), F32)]
                  + [jax.ShapeDtypeStruct((s // ATT_BLOCK, bsz, w, ATT_BLOCK), dt)
                     for w, dt in zip(c_sizes[1:], c_dtypes)]
                  + [jax.ShapeDtypeStruct((bsz, s, w), dt) for w, dt in zip(r_sizes, r_dtypes)],
        compiler_params=_cparams(("parallel", "parallel")),
        name="inproj",
    )(xz, mod, mod, w_c_t, w_rest, *tabs)


def _rope_tables(t_len, n_ctx):
    t = jnp.arange(t_len)
    rows = (t // GRID_W).astype(F32)
    cols = (t % GRID_W).astype(F32)
    pos = t.astype(F32)

    def angles(p, dim):
        inv_freq = ROPE_BASE ** (-jnp.arange(0, dim, 2, dtype=F32) / dim)
        return p[:, None] * inv_freq[None, :]

    def head_tables(angs):
        cos = jnp.concatenate([jnp.concatenate([jnp.cos(a), jnp.cos(a)], -1) for a in angs], -1)
        sa = jnp.concatenate([jnp.concatenate([-jnp.sin(a), jnp.zeros_like(a)], -1) for a in angs], -1)
        sb = jnp.concatenate([jnp.concatenate([jnp.zeros_like(a), jnp.sin(a)], -1) for a in angs], -1)
        rep = LANES // HEAD_DIM
        return tuple(jnp.tile(x, (1, rep)) for x in (cos, sa, sb))

    att = head_tables([angles(rows, HEAD_DIM // 2), angles(cols, HEAD_DIM // 2)])
    ret = head_tables([angles(pos, HEAD_DIM)])
    signed_t = lambda tb: (tb[0][:, :HEAD_DIM].T, (tb[1] + tb[2])[:, :HEAD_DIM].T)

    def with_ctx(tb, axis, is_cos):
        shape = list(tb.shape)
        shape[axis] = n_ctx
        lead = jnp.ones(shape, F32) if is_cos else jnp.zeros(shape, F32)
        return jnp.concatenate([lead, tb], axis=axis)

    chan = signed_t(att) + signed_t(ret)
    toks = att + ret
    return (tuple(with_ctx(tb, 1, k % 2 == 0) for k, tb in enumerate(chan))
            + tuple(with_ctx(tb, 0, k % 3 == 0) for k, tb in enumerate(toks)))


def _s5_kernel(u_hbm, tab_ref, wsf_ref, wsb_ref, wrf_ref, wrb_ref, lam_ref, y_ref,
               lhs_ref, sem, m_ref, acc_ref, cf_ref, cb_ref, pf_ref, pb_ref, *, bsz, nz_chunks):
    r_blk, ch, tm = y_ref.shape
    c = S5_CHUNK
    kpb = tm // c
    n_chunks = (r_blk // bsz) * kpb
    g = pl.program_id(0)
    slot = g % 2

    def fetch(group, s):
        for i in range(ch):
            p, half = divmod(i, 2)
            for k in range(kpb):
                pltpu.make_async_copy(u_hbm.at[:, group * ch + i, pl.ds(k * c, c)],
                                      lhs_ref.at[s, p, pl.ds(k * r_blk, r_blk), pl.ds(half * c, c)],
                                      sem.at[s]).start()

    @pl.when(g == 0)
    def _():
        fetch(0, 0)

    @pl.when(g + 1 < pl.num_programs(0))
    def _():
        fetch(g + 1, 1 - slot)

    pltpu.make_async_copy(lhs_ref.at[1 - slot], lhs_ref.at[slot], sem.at[slot]).wait()
    acc_ref[...] = jnp.zeros_like(acc_ref)
    cf_ref[...] = jnp.zeros_like(cf_ref)
    cb_ref[...] = jnp.zeros_like(cb_ref)

    def pair(p, carry):
        for ih in range(2):
            for o in range(ch):
                lag_row = jnp.broadcast_to(tab_ref[2 * p + ih, o:o + 1, :], (16, 2 * c))
                tab = pltpu.roll(lag_row, 0, 1, stride=1, stride_axis=0).astype(BF16)
                for r2 in range(c // 16):
                    lo = c - 16 * r2
                    m_ref[ih * c + 16 * r2:ih * c + 16 * r2 + 16, o * c:(o + 1) * c] = tab[:, lo:lo + c]
        lhs = lhs_ref[slot, p].astype(BF16)
        acc_ref[...] += _dot(lhs, m_ref[...])
        cf_ref[...] += _dot(lhs, wsf_ref[p])
        cb_ref[...] += _dot(lhs, wsb_ref[p])
        return carry

    lax.fori_loop(0, ch // 2, pair, 0)
    lam = lam_ref[...]
    nst = lam.shape[1] // 2

    def advance(s, a, bc, drive):
        return s * a + pltpu.roll(s, nst, 1) * bc + drive

    def sweep(order, c_ref, p_ref, a, bc):
        s = jnp.zeros((bsz, lam.shape[1]), F32)
        for n in order:
            j, k = divmod(n, kpb)
            rows = slice(k * r_blk + j * bsz, k * r_blk + (j + 1) * bsz)
            p_ref[rows, :] = s
            s = advance(s, a, bc, c_ref[rows, :])

    sweep(list(range(n_chunks)), cf_ref, pf_ref, lam[0:1], lam[1:2])
    order_b = list(range(nz_chunks - 1, -1, -1)) + list(range(n_chunks - 1, nz_chunks - 1, -1))
    sweep(order_b, cb_ref, pb_ref, lam[2:3], lam[3:4])
    acc_ref[...] += (_dot(pf_ref[...].astype(BF16), wrf_ref[...])
                     + _dot(pb_ref[...].astype(BF16), wrb_ref[...]))
    for o in range(ch):
        for k in range(kpb):
            y_ref[:, o, k * c:(k + 1) * c] = acc_ref[k * r_blk:(k + 1) * r_blk, o * c:(o + 1) * c]


def _s5_weights(lam_re, lam_im, log_step, b_re, b_im, c_re, c_im, chunk):
    hp = lax.Precision.HIGHEST
    lam = lax.complex(lam_re.astype(F32), lam_im.astype(F32))
    lam_dt = lam * jnp.exp(log_step.astype(F32))[..., None]
    lam_bar = jnp.exp(lam_dt)
    b_bar = lax.complex(b_re.astype(F32), b_im.astype(F32)) * ((lam_bar - 1.0) / lam)[..., None]
    c_mat = lax.complex(c_re.astype(F32), c_im.astype(F32))
    g, n = lam.shape[1], lam.shape[2]
    ch = b_bar.shape[-1]
    steps = jnp.arange(chunk + 1, dtype=F32)
    pw = jnp.exp(steps[None, :, None, None] * lam_dt[:, None])
    kern = jnp.einsum('zgon,zdgn,zgni->zgdoi', c_mat, pw[:, :chunk], b_bar, precision=hp).real
    zero_lag = kern[0, :, 0] + kern[1, :, 0]
    lag_table = jnp.concatenate([jnp.zeros_like(zero_lag)[:, None], kern[1, :, :0:-1], zero_lag[:, None],
                                 kern[0, :, 1:]], axis=1).transpose(0, 3, 2, 1)

    def state_in(pw_s, b_dir):
        w = pw_s[:, :, :, None] * b_dir[None]
        w = jnp.concatenate([w.real, w.imag], axis=2)
        return w.transpose(1, 3, 0, 2).reshape(g, ch // 2, 2 * chunk, 2 * n)

    def state_out(pw_t, c_dir):
        w = c_dir[None] * pw_t[:, :, None, :]
        w = jnp.concatenate([w.real, -w.imag], axis=3)
        return w.transpose(1, 3, 2, 0).reshape(g, 2 * n, ch * chunk)

    wsf = state_in(pw[0, chunk - 1::-1][:chunk], b_bar[0])
    wsb = state_in(pw[1, :chunk], b_bar[1])
    wrf = state_out(pw[0, 1:chunk + 1], c_mat[0])
    wrb = state_out(pw[1, chunk:0:-1], c_mat[1])
    lam_c = pw[:, chunk]
    rows = []
    for z in range(2):
        rows.append(jnp.concatenate([lam_c[z].real, lam_c[z].real], -1))
        rows.append(jnp.concatenate([-lam_c[z].imag, lam_c[z].imag], -1))
    lam_rows = jnp.stack(rows + [jnp.zeros_like(rows[0])] * 4, axis=1)
    return lag_table, wsf.astype(BF16), wsb.astype(BF16), wrf.astype(BF16), wrb.astype(BF16), lam_rows


def _s5_scan(ut, weights, n_ctx):
    nblk, bsz, width, tm = ut.shape
    tab, wsf, wsb, wrf, wrb, lam_rows = weights
    g, ch = tab.shape[0], tab.shape[1]
    c = S5_CHUNK
    nst2 = wsf.shape[3]
    r_blk = nblk * bsz
    rows = r_blk * (tm // c)
    grp = lambda *shape: pl.BlockSpec((None,) + shape, lambda i: (i,) + (0,) * len(shape))
    tok = pl.BlockSpec((r_blk, ch, tm), lambda i: (0, i, 0))
    y = pl.pallas_call(
        functools.partial(_s5_kernel, bsz=bsz, nz_chunks=n_ctx // c),
        grid=(g,),
        in_specs=[pl.BlockSpec(memory_space=pl.ANY), grp(ch, ch, 2 * c), grp(ch // 2, 2 * c, nst2), grp(ch // 2, 2 * c, nst2),
                  grp(nst2, ch * c), grp(nst2, ch * c), grp(8, nst2)],
        out_specs=tok,
        out_shape=jax.ShapeDtypeStruct((r_blk, width, tm), F32),
        scratch_shapes=[pltpu.VMEM((2, ch // 2, rows, 2 * c), F32), pltpu.SemaphoreType.DMA((2,)),
                        pltpu.VMEM((2 * c, ch * c), BF16),
                        pltpu.VMEM((rows, ch * c), F32)] + [pltpu.VMEM((rows, nst2), F32)] * 4,
        compiler_params=_cparams(("arbitrary",)),
        name="s5_scan",
    )(ut.reshape(r_blk, width, tm), tab, wsf, wsb, wrf, wrb, lam_rows)
    return y.reshape(nblk, bsz, width, tm)


def _attn_kernel(sink_ref, q_ref, k_ref, v_ref, o_ref, s_ref, e_ref, *, n_ctx, t_len, q_off, kv_heads):
    qi = pl.program_id(1) + q_off
    nzb = n_ctx // ATT_BLOCK
    band = 3 * ATT_BLOCK
    wide = ATT_REP * ATT_BLOCK
    q = q_ref[...].astype(BF16)
    kc = k_ref[0:n_ctx, :].astype(BF16)
    vc = jnp.concatenate([v_ref[p] for p in range(nzb)], axis=1).astype(BF16)

    def rows(a, i):
        return a[i * HEAD_DIM:(i + 1) * HEAD_DIM]

    def q_group(g):
        qg = jnp.concatenate([rows(q, g * ATT_REP + r) for r in range(ATT_REP)], axis=1)
        zero = jnp.zeros_like(qg)
        return jnp.concatenate([qg if gg == g else zero for gg in range(kv_heads)], axis=0)

    ones_rows = 16

    def v_aug(v, g):
        return jnp.concatenate([rows(v, g), jnp.ones((ones_rows, v.shape[1]), BF16)], axis=0)

    def finish(g, o_aug, extra_den):
        o = o_aug[:HEAD_DIM] / (o_aug[HEAD_DIM:HEAD_DIM + 1] + extra_den)
        for r in range(ATT_REP):
            h = g * ATT_REP + r
            o_ref[h * HEAD_DIM:(h + 1) * HEAD_DIM, :] = o[:, r * ATT_BLOCK:(r + 1) * ATT_BLOCK].astype(o_ref.dtype)

    @pl.when(qi < nzb)
    def _():
        for g in range(kv_heads):
            sink = sink_ref[g:g + 1, :]
            s = _dot(kc, q_group(g))
            m = jnp.maximum(jnp.max(s, axis=0, keepdims=True), sink)
            e = jnp.exp2(s - m)
            finish(g, _dot(v_aug(vc, g), e.astype(BF16)), jnp.exp2(sink - m))

    @pl.when(qi >= nzb)
    def _():
        n = qi - nzb
        start = jnp.clip((n - 1) * ATT_BLOCK, 0, t_len - band)
        kl = k_ref[pl.ds(pl.multiple_of(n_ctx + start, ATT_BLOCK), band), :].astype(BF16)
        p0 = (n_ctx + start) // ATT_BLOCK
        vl = jnp.concatenate([v_ref[p0 + t] for t in range(3)], axis=1).astype(BF16)
        k_pos = start + lax.broadcasted_iota(jnp.int32, (band, wide), 0)
        q_pos = n * ATT_BLOCK + (lax.broadcasted_iota(jnp.int32, (band, wide), 1) & (ATT_BLOCK - 1))
        valid = jnp.abs(k_pos - q_pos) <= WINDOW
        v_all = jnp.concatenate([vl, vc], axis=1)
        n_keys = band + n_ctx
        row_max = []
        for g in range(kv_heads):
            qg = q_group(g)
            m = sink_ref[g:g + 1, :]
            for j in range(0, n_keys, ATT_BLOCK):
                if j < band:
                    s = jnp.where(valid[j:j + ATT_BLOCK], _dot(kl[j:j + ATT_BLOCK], qg), NEG_INF)
                else:
                    s = _dot(kc[j - band:j - band + ATT_BLOCK], qg)
                s_ref[g, j:j + ATT_BLOCK, :] = s
                m = jnp.maximum(m, jnp.max(s, axis=0, keepdims=True))
            row_max.append(m)
        for g in range(kv_heads):
            m = row_max[g]
            for j in range(0, n_keys, ATT_BLOCK):
                e_ref[g, j:j + ATT_BLOCK, :] = jnp.exp2(s_ref[g, j:j + ATT_BLOCK, :] - m).astype(BF16)
            finish(g, _dot(v_aug(v_all, g), e_ref[g]), jnp.exp2(sink_ref[g:g + 1, :] - m))


def _attention(qat, ka, vat, sink, n_ctx, need_ctx):
    nblk, bsz, qw, _ = qat.shape
    s, kvw = ka.shape[1], ka.shape[2]
    t_len = s - n_ctx
    kv_heads = kvw // HEAD_DIM
    q_off = 0 if need_ctx else n_ctx // ATT_BLOCK
    nq = nblk - q_off
    sink_rows = jnp.repeat(sink.astype(F32).reshape(kv_heads, ATT_REP), ATT_BLOCK, axis=1) * LOG2_E
    return pl.pallas_call(
        functools.partial(_attn_kernel, n_ctx=n_ctx, t_len=t_len, q_off=q_off, kv_heads=kv_heads),
        grid=(bsz, nq),
        in_specs=[pl.BlockSpec(sink_rows.shape, lambda b, j: (0, 0)),
                  pl.BlockSpec((None, None, qw, ATT_BLOCK), lambda b, j: (j + q_off, b, 0, 0)),
                  pl.BlockSpec((None, s, kvw), lambda b, j: (b, 0, 0)),
                  pl.BlockSpec((nblk, None, kvw, ATT_BLOCK), lambda b, j: (0, b, 0, 0))],
        out_specs=pl.BlockSpec((None, None, qw, ATT_BLOCK), lambda b, j: (j, b, 0, 0)),
        out_shape=jax.ShapeDtypeStruct((nq, bsz, qw, ATT_BLOCK), BF16),
        scratch_shapes=[pltpu.VMEM((kv_heads, 3 * ATT_BLOCK + n_ctx, ATT_REP * ATT_BLOCK), F32),
                        pltpu.VMEM((kv_heads, 3 * ATT_BLOCK + n_ctx, ATT_REP * ATT_BLOCK), BF16)],
        compiler_params=_cparams(("parallel", "arbitrary")),
        name="window_attention",
    )(sink_rows, qat, ka, vat)


def _ret_kernel(lg_ref, q_ref, k_ref, v_ref, g_ref, o_ref, acc_ref, sf_ref, sb_ref, *, nz_chunks, heads):
    c = RET_CHUNK
    n_chunks, w, _ = q_ref.shape
    f32 = lambda a: a.astype(F32)

    def per_head(shape, axis, group, direction):
        owner = lax.broadcasted_iota(jnp.int32, shape, axis) // group
        out = jnp.zeros(shape, F32)
        for h in range(heads):
            out = jnp.where(owner == h, lg_ref[direction, h], out)
        return out

    row_i = f32(lax.broadcasted_iota(jnp.int32, (w, c), 1))
    qw_f = jnp.exp(per_head((w, c), 0, HEAD_DIM, 0) * (row_i + 1.0))
    qw_b = jnp.exp(per_head((w, c), 0, HEAD_DIM, 1) * (c - row_i))
    key_j = f32(lax.broadcasted_iota(jnp.int32, (c, w), 0))
    kw_f = jnp.exp(per_head((c, w), 1, HEAD_DIM, 0) * (c - 1.0 - key_j))
    kw_b = jnp.exp(per_head((c, w), 1, HEAD_DIM, 1) * key_j)
    dec_f = jnp.exp(per_head((w, 1), 0, HEAD_DIM, 0) * c)
    dec_b = jnp.exp(per_head((w, 1), 0, HEAD_DIM, 1) * c)
    wide = heads * c
    diff = f32((lax.broadcasted_iota(jnp.int32, (c, wide), 1) & (c - 1)) - lax.broadcasted_iota(jnp.int32, (c, wide), 0))
    dec_t = jnp.where(diff >= 0, jnp.exp(per_head((c, wide), 1, c, 0) * jnp.maximum(diff, 0.0)),
                      jnp.exp(per_head((c, wide), 1, c, 1) * jnp.maximum(-diff, 0.0)))
    own_wide = (lax.broadcasted_iota(jnp.int32, (w, wide), 0) // HEAD_DIM
                == lax.broadcasted_iota(jnp.int32, (w, wide), 1) // c)
    own_sq = f32(lax.broadcasted_iota(jnp.int32, (w, w), 0) // HEAD_DIM
                 == lax.broadcasted_iota(jnp.int32, (w, w), 1) // HEAD_DIM)
    sf_ref[...] = jnp.zeros_like(sf_ref)
    sb_ref[...] = jnp.zeros_like(sb_ref)

    def block_diag(x):
        return jnp.where(own_wide, jnp.concatenate([x] * heads, axis=1), 0.0).astype(BF16)

    def fwd(n, carry):
        qt, vt = q_ref[n], v_ref[n]
        k = k_ref[pl.ds(pl.multiple_of(n * c, c), c), :]
        scores_t = _dot(k.astype(BF16), block_diag(qt)) * dec_t
        stacked = jnp.concatenate([scores_t[:, h * c:(h + 1) * c] for h in range(heads)], axis=0)
        o = _dot(block_diag(vt), stacked.astype(BF16))
        s_prev = sf_ref[...]
        o = o + _dot(s_prev.astype(BF16), (qt * qw_f).astype(BF16))
        sf_ref[...] = dec_f * s_prev + own_sq * _dot(vt.astype(BF16), (k * kw_f).astype(BF16))
        acc_ref[n] = o
        return carry

    lax.fori_loop(0, n_chunks, fwd, 0)

    def bwd(i, carry):
        n = jnp.where(i < nz_chunks, nz_chunks - 1 - i, n_chunks - 1 - i + nz_chunks)
        qt, vt = q_ref[n], v_ref[n]
        k = k_ref[pl.ds(pl.multiple_of(n * c, c), c), :]
        s_prev = sb_ref[...]
        o = acc_ref[n] + _dot(s_prev.astype(BF16), (qt * qw_b).astype(BF16))
        sb_ref[...] = dec_b * s_prev + own_sq * _dot(vt.astype(BF16), (k * kw_b).astype(BF16))
        gate = _silu(g_ref[n])
        for h in range(heads):
            rows = slice(h * HEAD_DIM, (h + 1) * HEAD_DIM)
            oh = o[rows]
            mu = jnp.mean(oh, axis=0, keepdims=True)
            oc = oh - mu
            var = jnp.mean(oc * oc, axis=0, keepdims=True)
            o_ref[n, rows, :] = (oc * lax.rsqrt(var + LN_EPS) * gate[rows]).astype(o_ref.dtype)
        return carry

    lax.fori_loop(0, n_chunks, bwd, 0)


def _retention(qrt, kr, vrt, grt, log_gamma, n_ctx):
    n_chunks, bsz, w, c = qrt.shape
    s = kr.shape[1]
    chan = pl.BlockSpec((n_chunks, None, w, c), lambda b: (0, b, 0, 0))
    return pl.pallas_call(
        functools.partial(_ret_kernel, nz_chunks=n_ctx // RET_CHUNK, heads=w // HEAD_DIM),
        grid=(bsz,),
        in_specs=[pl.BlockSpec(memory_space=pltpu.SMEM), chan, pl.BlockSpec((None, s, w), lambda b: (b, 0, 0)),
                  chan, chan],
        out_specs=chan,
        out_shape=jax.ShapeDtypeStruct((n_chunks, bsz, w, c), BF16),
        scratch_shapes=[pltpu.VMEM((n_chunks, w, c), F32), pltpu.VMEM((w, w), F32), pltpu.VMEM((w, w), F32)],
        compiler_params=_cparams(("parallel",)),
        name="retention",
    )(log_gamma.astype(F32), qrt, kr, vrt, grt)


def _outproj_kernel(*refs, nzb, alpha, with_router, w_s5, w_att, n_exp):
    if with_router:
        (y_ref, u_ref, a_ref, r_ref, x_ref, modx_ref, modz_ref, d_ref, wg_ref, bg_ref, wo_ref,
         g1_ref, b1_ref, rt_ref, x1_ref, fx_ref, gate_ref) = refs
    else:
        (y_ref, u_ref, a_ref, r_ref, x_ref, modx_ref, modz_ref, d_ref, wg_ref, bg_ref, wo_ref,
         g1_ref, b1_ref, x1_ref, fx_ref) = refs
    is_ctx = pl.program_id(1) < nzb
    mod = jnp.where(is_ctx, modz_ref[...], modx_ref[...])
    g = _gelu_tanh(y_ref[...] + d_ref[...] * u_ref[...])
    s5 = g * _sigmoid(_dot(wg_ref[...], g.astype(BF16)) + bg_ref[...])
    def chan_major(ref, lo, hi):
        return jnp.concatenate([_dot_tn(ref[k].astype(BF16), wo_ref[lo:hi, :]) for k in range(ref.shape[0])],
                               axis=0)

    mix = (_dot_tn(s5.astype(BF16), wo_ref[0:w_s5, :]) + chan_major(a_ref, w_s5, w_s5 + w_att)
           + chan_major(r_ref, w_s5 + w_att, wo_ref.shape[0]))
    x1 = _ln(alpha * x_ref[...] + mod[2:3] * mix) * g1_ref[...] + b1_ref[...]
    x1_ref[...] = x1
    fx = _ln(x1) * (1.0 + mod[4:5]) + mod[3:4]
    fx_ref[...] = fx.astype(fx_ref.dtype)
    if with_router:
        lane = lax.broadcasted_iota(jnp.int32, (fx.shape[0], LANES), 1)
        logits = jnp.where(lane < n_exp, _dot3(fx, rt_ref[...]), -jnp.inf)
        m1 = jnp.max(logits, axis=1, keepdims=True)
        i1 = jnp.min(jnp.where(logits == m1, lane, LANES), axis=1, keepdims=True)
        rest = jnp.where(lane == i1, -jnp.inf, logits)
        m2 = jnp.max(rest, axis=1, keepdims=True)
        i2 = jnp.min(jnp.where(rest == m2, lane, LANES), axis=1, keepdims=True)
        e2 = jnp.exp(m2 - m1)
        den = 1.0 + e2
        route = jnp.where(lane == 0, 1.0 / den, jnp.where(lane == 1, e2 / den, 0.0))
        route = jnp.where(lane == 2, i1.astype(F32), jnp.where(lane == 3, i2.astype(F32), route))
        gate_ref[...] = route


def _outproj(y_s5, u, o_att, o_ret, xz, mod, s5_d, w_glu, b_glu, w_out, ln_g, ln_b, router, n_ctx, need_ctx,
             alpha):
    bsz, s, d = xz.shape
    tm = TOKEN_BLOCK
    nzb = n_ctx // tm
    off = 0 if need_ctx else nzb
    nblk = s // tm - off
    s_out = nblk * tm
    nrow = mod.shape[0]
    w_s5, w_att = y_s5.shape[2], o_att.shape[2]
    sub = tm // ATT_BLOCK
    att_off = off if o_att.shape[0] == s // ATT_BLOCK else 0
    att = pl.BlockSpec((sub, None, w_att, ATT_BLOCK), lambda b, j: (j + att_off, b, 0, 0))

    def tok(width, shift):
        return pl.BlockSpec((None, tm, width), lambda b, j: (b, j + shift, 0))

    def full(a):
        return pl.BlockSpec(a.shape, lambda b, j: (0,) * a.ndim)

    per_in = y_s5.shape[3] // tm
    chan = pl.BlockSpec((None, None, w_s5, tm), lambda b, j: ((j + off) // per_in, b, 0, (j + off) % per_in))
    vec = lambda a: a.reshape(1, -1).astype(F32)
    col = lambda a: a.reshape(-1, 1).astype(F32)
    consts = [col(s5_d), w_glu.T.astype(BF16), col(b_glu), w_out.astype(BF16), vec(ln_g), vec(ln_b)]
    ret = pl.BlockSpec((sub, None, o_ret.shape[2], ATT_BLOCK), lambda b, j: (j + off, b, 0, 0))
    in_specs = [chan, chan, att, ret, tok(d, off),
                pl.BlockSpec((None, 6, d), lambda b, j: (b, 0, 0)),
                pl.BlockSpec((None, 6, d), lambda b, j: (nrow - 1, 0, 0))] + [full(a) for a in consts]
    out_specs = [tok(d, 0), tok(d, 0)]
    fx_dtype = BF16 if router is None else F32
    out_shape = [jax.ShapeDtypeStruct((bsz, s_out, d), F32), jax.ShapeDtypeStruct((bsz, s_out, d), fx_dtype)]
    args = [y_s5, u, o_att, o_ret, xz, mod, mod] + consts
    with_router = router is not None
    n_exp = 0
    if with_router:
        n_exp = router.shape[1]
        router_pad = jnp.pad(router.astype(F32), ((0, 0), (0, LANES - n_exp)))
        args.append(router_pad)
        in_specs.append(full(router_pad))
        out_specs.append(tok(LANES, 0))
        out_shape.append(jax.ShapeDtypeStruct((bsz, s_out, LANES), F32))
    return pl.pallas_call(
        functools.partial(_outproj_kernel, nzb=nzb - off, alpha=alpha, with_router=with_router,
                          w_s5=w_s5, w_att=w_att, n_exp=n_exp),
        grid=(bsz, nblk),
        in_specs=in_specs,
        out_specs=out_specs,
        out_shape=out_shape,
        compiler_params=_cparams(("parallel", "parallel")),
        name="outproj",
    )(*args)


def _swiglu_into(x_bf16, w1_ref, w3_ref, w2_ref, acc_ref, fc):
    for s in range(0, w1_ref.shape[1], fc):
        h1 = _dot(x_bf16, w1_ref[:, s:s + fc])
        h3 = _dot(x_bf16, w3_ref[:, s:s + fc])
        acc_ref[...] += _dot((_silu(h1) * h3).astype(BF16), w2_ref[s:s + fc, :])


def _ffn_kernel(fx_ref, x1_ref, modx_ref, modz_ref, w1_ref, w3_ref, w2_ref, g2_ref, b2_ref,
                o_ref, acc_ref, *, n_ctx_tokens, alpha, fc):
    tm = acc_ref.shape[0]
    acc_ref[...] = jnp.zeros_like(acc_ref)
    _swiglu_into(fx_ref[...], w1_ref, w3_ref, w2_ref, acc_ref, fc)
    row = pl.program_id(1) * tm + lax.broadcasted_iota(jnp.int32, (tm, 1), 0)
    gate = jnp.where(row < n_ctx_tokens, modz_ref[5:6, :], modx_ref[5:6, :])
    o_ref[...] = _ln(alpha * x1_ref[...] + gate * acc_ref[...]) * g2_ref[...] + b2_ref[...]


def _ffn(fx, x1, mod, w1, w3, w2, ln_g, ln_b, n_ctx_tokens, alpha):
    bsz, s, d = x1.shape
    ff = w1.shape[1]
    tm = _pick_chunk(s, FFN_BLOCK)
    nrow = mod.shape[0]
    tok = pl.BlockSpec((None, tm, d), lambda b, j: (b, j, 0))
    full = lambda a: pl.BlockSpec(a.shape, lambda b, j: (0,) * a.ndim, pipeline_mode=pl.Buffered(1))
    vec = lambda a: a.reshape(1, -1).astype(F32)
    return pl.pallas_call(
        functools.partial(_ffn_kernel, n_ctx_tokens=n_ctx_tokens, alpha=alpha, fc=_pick_chunk(ff, FFN_CHUNK)),
        grid=(bsz, s // tm),
        in_specs=[tok, tok,
                  pl.BlockSpec((None, 6, d), lambda b, j: (b, 0, 0)),
                  pl.BlockSpec((None, 6, d), lambda b, j: (nrow - 1, 0, 0)),
                  full(w1), full(w3), full(w2),
                  pl.BlockSpec((1, d), lambda b, j: (0, 0)), pl.BlockSpec((1, d), lambda b, j: (0, 0))],
        out_specs=tok,
        out_shape=jax.ShapeDtypeStruct((bsz, s, d), F32),
        scratch_shapes=[pltpu.VMEM((tm, d), F32)],
        compiler_params=_cparams(("parallel", "parallel")),
        name="dense_ffn",
    )(fx, x1, mod, mod, w1, w3, w2, vec(ln_g), vec(ln_b))


MOE_TILE = 1024


def _route_plan(e1, e2, n_exp, tile):
    n = e1.shape[0]
    pair_e = jnp.stack([e1, e2], axis=1).reshape(-1)
    onehot = (pair_e[:, None] == jnp.arange(n_exp, dtype=jnp.int32)[None, :]).astype(jnp.int32)
    before = jnp.cumsum(onehot, axis=0) - onehot
    rank = jnp.sum(before * onehot, axis=1)
    counts = jnp.sum(onehot, axis=0)
    padded = (counts + tile - 1) // tile * tile
    ends = jnp.cumsum(padded)
    starts = ends - padded
    dest = starts[pair_e] + rank
    n_rows = (2 * n + n_exp * (tile - 1)) // tile * tile
    n_tiles = n_rows // tile
    tile_start = jnp.arange(n_tiles, dtype=jnp.int32) * tile
    tile_expert = jnp.minimum(jnp.sum((tile_start[:, None] >= ends[None, :]).astype(jnp.int32), axis=1),
                              n_exp - 1)
    by_expert = jnp.sort(pair_e * (2 * n) + jnp.arange(2 * n, dtype=jnp.int32)) % (2 * n)
    row = jnp.arange(n_rows, dtype=jnp.int32)
    row_e = jnp.repeat(tile_expert, tile)
    in_group = row - starts[row_e]
    src = jnp.minimum(in_group + (jnp.cumsum(counts) - counts)[row_e], 2 * n - 1)
    row_token = jnp.where(in_group < counts[row_e], by_expert[src] // 2, 0)
    n_used = (ends[-1] // tile).astype(jnp.int32).reshape(1)
    return row_token, dest.reshape(n, 2), tile_expert, n_used


def _moe_gemm_kernel(te_ref, nused_ref, tok_ref, tokn_ref, x_hbm, w1_ref, w3_ref, w2_ref, y_ref,
                     xbuf_ref, sem, *, fc, tile):
    i = pl.program_id(0)
    f = pl.program_id(1)
    slot = i % 2
    n_used = nused_ref[0]

    def row_copy(t_ref, r, s):
        return pltpu.make_async_copy(x_hbm.at[pl.ds(t_ref[0, r], 1)], xbuf_ref.at[s, pl.ds(r, 1)], sem.at[s])

    def issue(t_ref, s):
        def body(it, carry):
            base = pl.multiple_of(it * 8, 8)
            for k in range(8):
                row_copy(t_ref, base + k, s).start()
            return carry
        lax.fori_loop(0, tile // 8, body, 0)

    @pl.when(f == 0)
    def _():
        y_ref[...] = jnp.zeros_like(y_ref)

        @pl.when(i == 0)
        def _():
            issue(tok_ref, 0)

        @pl.when(i + 1 < n_used)
        def _():
            issue(tokn_ref, 1 - slot)

        @pl.when(i < n_used)
        def _():
            pltpu.make_async_copy(x_hbm.at[pl.ds(0, tile)], xbuf_ref.at[slot], sem.at[slot]).wait()

    @pl.when(i < n_used)
    def _():
        _swiglu_into(xbuf_ref[slot].astype(BF16), w1_ref, w3_ref, w2_ref, y_ref, fc)


def _moe_gemm(x_flat, row_token, tile_expert, n_used, w1, w3, w2, tile):
    n_rows = row_token.shape[0]
    n_tiles = n_rows // tile
    d = x_flat.shape[1]
    ff = w1.shape[2]
    nf = 2 if ff % (2 * LANES) == 0 else 1
    tf = ff // nf
    last = lambda i, nu: jnp.minimum(i, nu[0] - 1)
    fsel = lambda i, f, nu: jnp.where(i < nu[0], f, nf - 1)
    tok = lambda imap: pl.BlockSpec((None, 1, tile), imap, memory_space=pltpu.SMEM)
    tokens = row_token.reshape(n_tiles, 1, tile)
    return pl.pallas_call(
        functools.partial(_moe_gemm_kernel, fc=_pick_chunk(tf, MOE_CHUNK), tile=tile),
        grid_spec=pltpu.PrefetchScalarGridSpec(
            num_scalar_prefetch=2,
            grid=(n_tiles, nf),
            in_specs=[tok(lambda i, f, te, nu: (i, 0, 0)),
                      tok(lambda i, f, te, nu: (jnp.minimum(i + 1, n_tiles - 1), 0, 0)),
                      pl.BlockSpec(memory_space=pl.ANY),
                      pl.BlockSpec((None, d, tf), lambda i, f, te, nu: (te[last(i, nu)], 0, fsel(i, f, nu))),
                      pl.BlockSpec((None, d, tf), lambda i, f, te, nu: (te[last(i, nu)], 0, fsel(i, f, nu))),
                      pl.BlockSpec((None, tf, d), lambda i, f, te, nu: (te[last(i, nu)], fsel(i, f, nu), 0))],
            out_specs=pl.BlockSpec((tile, d), lambda i, f, te, nu: (i, 0)),
            scratch_shapes=[pltpu.VMEM((2, tile, d), F32), pltpu.SemaphoreType.DMA((2,))]),
        out_shape=jax.ShapeDtypeStruct((n_rows, d), F32),
        compiler_params=_cparams(("arbitrary", "arbitrary")),
        name="moe_gemm",
    )(tile_expert, n_used, tokens, tokens, x_flat, w1, w3, w2)


def _moe_combine_kernel(pos_ref, posn_ref, route_ref, x1_ref, modx_ref, modz_ref, g2_ref, b2_ref, y_hbm,
                        o_ref, buf_ref, sem, *, nzb, nblk, alpha, tm):
    i = pl.program_id(0)
    slot = i % 2

    def row_copy(p_ref, r, k, s):
        return pltpu.make_async_copy(y_hbm.at[pl.ds(p_ref[0, k * tm + r], 1)],
                                     buf_ref.at[s, pl.ds(k * tm + r, 1)], sem.at[s])

    def issue(p_ref, s):
        def body(r, carry):
            row_copy(p_ref, r, 0, s).start()
            row_copy(p_ref, r, 1, s).start()
            return carry
        lax.fori_loop(0, tm, body, 0, unroll=8)

    @pl.when(i == 0)
    def _():
        issue(pos_ref, 0)

    @pl.when(i + 1 < pl.num_programs(0))
    def _():
        issue(posn_ref, 1 - slot)

    pltpu.make_async_copy(y_hbm.at[pl.ds(0, 2 * tm)], buf_ref.at[slot], sem.at[slot]).wait()
    route = route_ref[...]
    f = route[:, 0:1] * buf_ref[slot, 0:tm] + route[:, 1:2] * buf_ref[slot, tm:2 * tm]
    is_ctx = (i % nblk) < nzb
    mod = jnp.where(is_ctx, modz_ref[...], modx_ref[...])
    o_ref[...] = _ln(alpha * x1_ref[...] + mod[5:6] * f) * g2_ref[...] + b2_ref[...]


def _moe_combine(ys, pos, route, x1, mod, ln_g, ln_b, n_ctx_tokens, alpha):
    bsz, s, d = x1.shape
    tm = TOKEN_BLOCK
    nblk = s // tm
    n_steps = bsz * nblk
    nrow = mod.shape[0]
    pos_steps = pos.reshape(n_steps, tm, 2).transpose(0, 2, 1).reshape(n_steps, 1, 2 * tm)
    tok = lambda width: pl.BlockSpec((tm, width), lambda i: (i, 0))
    smem = lambda imap: pl.BlockSpec((None, 1, 2 * tm), imap, memory_space=pltpu.SMEM)
    vec = lambda a: a.reshape(1, -1).astype(F32)
    out = pl.pallas_call(
        functools.partial(_moe_combine_kernel, nzb=n_ctx_tokens // tm, nblk=nblk, alpha=alpha, tm=tm),
        grid=(n_steps,),
        in_specs=[smem(lambda i: (i, 0, 0)),
                  smem(lambda i: (jnp.minimum(i + 1, n_steps - 1), 0, 0)),
                  tok(LANES), tok(d),
                  pl.BlockSpec((None, 6, d), lambda i: (i // nblk, 0, 0)),
                  pl.BlockSpec((None, 6, d), lambda i: (nrow - 1, 0, 0)),
                  pl.BlockSpec((1, d), lambda i: (0, 0)), pl.BlockSpec((1, d), lambda i: (0, 0)),
                  pl.BlockSpec(memory_space=pl.ANY)],
        out_specs=tok(d),
        out_shape=jax.ShapeDtypeStruct((bsz * s, d), F32),
        scratch_shapes=[pltpu.VMEM((2, 2 * tm, d), F32), pltpu.SemaphoreType.DMA((2,))],
        compiler_params=_cparams(("arbitrary",)),
        name="moe_combine",
    )(pos_steps, pos_steps, route.reshape(bsz * s, LANES), x1.reshape(bsz * s, d), mod, mod,
      vec(ln_g), vec(ln_b), ys)
    return out.reshape(bsz, s, d)


def _moe(fx, x1, route, mod, w1, w3, w2, ln_g, ln_b, n_ctx_tokens, alpha):
    bsz, s, d = x1.shape
    n_exp = w1.shape[0]
    idx = route.reshape(bsz * s, LANES)[:, 2:4].astype(jnp.int32)
    row_token, pos, tile_expert, n_used = _route_plan(idx[:, 0], idx[:, 1], n_exp, MOE_TILE)
    ys = _moe_gemm(fx.reshape(bsz * s, d), row_token, tile_expert, n_used, w1, w3, w2, MOE_TILE)
    return _moe_combine(ys, pos, route, x1, mod, ln_g, ln_b, n_ctx_tokens, alpha)


def kernel(x, c, ctx, c_ctx, w_mod, b_mod, w_in, s5_lam_re, s5_lam_im, s5_log_step, s5_b_re, s5_b_im,
           s5_c_re, s5_c_im, s5_d, s5_w_glu, s5_b_glu, attn_sink, ret_log_gamma, w_out,
           ln1_g, ln1_b, ln2_g, ln2_b, ffn_w1, ffn_w3, ffn_w2, moe_router, moe_w1, moe_w3, moe_w2):
    bsz, t_len, d = x.shape
    n_ctx = ctx.shape[1]
    depth = w_in.shape[0]
    alpha = (2 * depth) ** 0.25
    s5_w = s5_d.shape[1]
    att_w = attn_sink.shape[1] * HEAD_DIM
    kv_w = att_w // ATT_REP
    ret_w = ret_log_gamma.shape[2] * HEAD_DIM
    sizes = (s5_w, att_w, kv_w, kv_w, ret_w, ret_w, ret_w, ret_w)
    assert sum(sizes) == w_in.shape[2] and s5_w + att_w + ret_w == w_out.shape[1]
    assert n_ctx % TOKEN_BLOCK == 0 and t_len % TOKEN_BLOCK == 0 and t_len >= 3 * ATT_BLOCK

    pad = (-(bsz + 1)) % 8
    cvec = jnp.concatenate([jnp.zeros((pad, d), F32), c_ctx[None].astype(F32)], axis=0)
    cvec = jnp.concatenate([c.astype(F32), cvec], axis=0)
    mod_all = _modulation(cvec, w_mod.astype(F32), b_mod.astype(F32)).reshape(depth, bsz + pad + 1, 6, d)

    tabs = _rope_tables(t_len, n_ctx)
    xz = jnp.concatenate([ctx, x], axis=1).astype(F32)
    for l in range(depth):
        need_ctx = l < depth - 1
        mod = mod_all[l]
        u, qa, va, qr, vr, gr, ka, kr = _inproj(xz, mod, w_in[l].astype(BF16), tabs, sizes, n_ctx)
        s5w = _s5_weights(s5_lam_re[l], s5_lam_im[l], s5_log_step[l], s5_b_re[l], s5_b_im[l],
                          s5_c_re[l], s5_c_im[l], S5_CHUNK)
        y_s5 = _s5_scan(u, s5w, n_ctx)
        o_att = _attention(qa, ka, va, attn_sink[l], n_ctx, need_ctx)
        o_ret = _retention(qr, kr, vr, gr, ret_log_gamma[l], n_ctx)
        i = l // 2
        router = None if l % 2 == 0 else moe_router[i]
        outs = _outproj(y_s5, u, o_att, o_ret, xz, mod, s5_d[l], s5_w_glu[l], s5_b_glu[l], w_out[l],
                        ln1_g[l], ln1_b[l], router, n_ctx, need_ctx, alpha)
        ctx_tokens = n_ctx if need_ctx else 0
        if l % 2 == 0:
            x1, fx = outs
            xz = _ffn(fx, x1, mod, ffn_w1[i].astype(BF16), ffn_w3[i].astype(BF16), ffn_w2[i].astype(BF16),
                      ln2_g[l], ln2_b[l], ctx_tokens, alpha)
        else:
            x1, fx, route = outs
            xz = _moe(fx, x1, route, mod, moe_w1[i].astype(BF16), moe_w3[i].astype(BF16),
                      moe_w2[i].astype(BF16), ln2_g[l], ln2_b[l], ctx_tokens, alpha)
    return xz if xz.shape[1] == t_len else xz[:, n_ctx:]
```

```python
import functools
import math

import jax
import jax.numpy as jnp
import numpy as np
from jax import lax
from jax.experimental import pallas as pl
from jax.experimental.pallas import tpu as pltpu

F32 = jnp.float32
BF16 = jnp.bfloat16

GRID_W = 64
HEAD_DIM = 64
ATT_REP = 4
WINDOW = 128
ATT_BLOCK = 128
RET_CHUNK = 128
LN_EPS = 1e-5
ROPE_BASE = 10000.0
NEG_INF = -1e30
LOG2_E = math.log2(math.e)

LANES = 128
S5_CHUNK = LANES
TOKEN_BLOCK = 256
INPROJ_BLOCK = 768
FFN_BLOCK = 768
FFN_CHUNK = 512
MOE_CHUNK = 256
ATT_Q_BLOCKS = 2
MOD_BLOCK = 1536
VMEM_LIMIT = 56 * 1024 * 1024


def _cparams(sem):
    return pltpu.CompilerParams(dimension_semantics=sem, vmem_limit_bytes=VMEM_LIMIT)


def _pick_chunk(total, target):
    best = LANES
    for c in range(LANES, target + 1, LANES):
        if total % c == 0:
            best = c
    return best


def _dot(a, b):
    return jnp.dot(a, b, preferred_element_type=F32)


def _dot_nt(a, b):
    return lax.dot_general(a, b, (((1,), (1,)), ((), ())), preferred_element_type=F32)


def _dot_tn(a, b):
    return lax.dot_general(a, b, (((0,), (0,)), ((), ())), preferred_element_type=F32)


def _split_bf16(a):
    hi = a.astype(BF16)
    lo = (a - hi.astype(F32)).astype(BF16)
    return hi, lo


def _dot3(a, b):
    ah, al = _split_bf16(a)
    bh, bl = _split_bf16(b)
    return _dot(ah, bh) + (_dot(ah, bl) + _dot(al, bh))


def _ln(x):
    mu = jnp.mean(x, axis=-1, keepdims=True)
    xc = x - mu
    var = jnp.mean(xc * xc, axis=-1, keepdims=True)
    return xc * lax.rsqrt(var + LN_EPS)


def _silu(x):
    return x * (1.0 / (1.0 + jnp.exp(-x)))


def _sigmoid(x):
    return 1.0 / (1.0 + jnp.exp(-x))


def _gelu_tanh(x):
    c = math.sqrt(2.0 / math.pi)
    return 0.5 * x * (1.0 + jnp.tanh(c * (x + 0.044715 * (x * x * x))))


def _mod_kernel(c_ref, w_ref, b_ref, o_ref):
    o_ref[...] = _dot3(_silu(c_ref[...]), w_ref[...]) + b_ref[...]


def _modulation(cvec, w_mod, b_mod):
    depth, d, n = w_mod.shape
    rows = cvec.shape[0]
    tn = _pick_chunk(n, MOD_BLOCK)
    return pl.pallas_call(
        _mod_kernel,
        grid=(depth, n // tn),
        in_specs=[pl.BlockSpec((rows, d), lambda l, j: (0, 0)),
                  pl.BlockSpec((None, d, tn), lambda l, j: (l, 0, j)),
                  pl.BlockSpec((None, 1, tn), lambda l, j: (l, 0, j))],
        out_specs=pl.BlockSpec((None, rows, tn), lambda l, j: (l, 0, j)),
        out_shape=jax.ShapeDtypeStruct((depth, rows, n), F32),
        compiler_params=_cparams(("parallel", "parallel")),
        name="modulation",
    )(cvec, w_mod, b_mod.reshape(depth, 1, n))


def _rope_slab(xs, cos, sa, sb, half):
    return xs * cos + pltpu.roll(xs, LANES - half, 1) * sa + pltpu.roll(xs, half, 1) * sb


def _inproj_kernel(x_ref, modx_ref, modz_ref, wc_ref, w_ref, act_ref, ast_ref, rct_ref, rst_ref,
                   ac_ref, asa_ref, asb_ref, rc_ref, rsa_ref, rsb_ref,
                   ut_ref, qat_ref, vat_ref, qrt_ref, vrt_ref, grt_ref, ka_ref, kr_ref, *, n_ctx, c_sizes, sizes):
    tm = x_ref.shape[0]
    is_ctx = pl.program_id(1) * tm + lax.broadcasted_iota(jnp.int32, (tm, 1), 0) < n_ctx
    shift = jnp.where(is_ctx, modz_ref[0:1, :], modx_ref[0:1, :])
    gain = 1.0 + jnp.where(is_ctx, modz_ref[1:2, :], modx_ref[1:2, :])
    h = (_ln(x_ref[...]) * gain + shift).astype(BF16)
    scale = HEAD_DIM ** -0.5

    ct = _dot_nt(wc_ref[...], h)
    c_offs = [int(o) for o in np.concatenate([[0], np.cumsum(c_sizes)])]

    def chan(i):
        return ct[c_offs[i]:c_offs[i + 1]]

    def rope_t(x, cos_ref, sin_ref, part, mul):
        cos_t, sin_t = cos_ref[...], sin_ref[...]
        heads = []
        for hd in range(x.shape[0] // HEAD_DIM):
            xh = x[hd * HEAD_DIM:(hd + 1) * HEAD_DIM]
            pieces = []
            for lo in range(0, HEAD_DIM, 2 * part):
                pieces += [xh[lo + part:lo + 2 * part], xh[lo:lo + part]]
            out = xh * cos_t + jnp.concatenate(pieces, axis=0) * sin_t
            heads.append(out * mul if mul != 1.0 else out)
        return jnp.concatenate(heads, axis=0)

    def put(ref, val):
        for k in range(ref.shape[0]):
            ref[k] = val[:, k * ATT_BLOCK:(k + 1) * ATT_BLOCK].astype(ref.dtype)

    ut_ref[...] = chan(0)
    put(qat_ref, rope_t(chan(1), act_ref, ast_ref, HEAD_DIM // 4, scale * LOG2_E))
    put(vat_ref, chan(2))
    put(qrt_ref, rope_t(chan(3), rct_ref, rst_ref, HEAD_DIM // 2, 1.0))
    put(vrt_ref, chan(4))
    put(grt_ref, chan(5))

    p = _dot(h, w_ref[...])
    ac, asa, asb = ac_ref[...], asa_ref[...], asb_ref[...]
    rc, rsa, rsb = rc_ref[...], rsa_ref[...], rsb_ref[...]
    offs = np.concatenate([[0], np.cumsum(sizes)])

    def rope_cols(i, cos, sa, sb, half, mul):
        blk = p[:, int(offs[i]):int(offs[i + 1])]
        slabs = [_rope_slab(blk[:, s:s + LANES], cos, sa, sb, half) for s in range(0, blk.shape[1], LANES)]
        out = slabs[0] if len(slabs) == 1 else jnp.concatenate(slabs, axis=1)
        return out * mul if mul != 1.0 else out

    ka_ref[...] = rope_cols(0, ac, asa, asb, HEAD_DIM // 4, 1.0).astype(ka_ref.dtype)
    kr_ref[...] = rope_cols(1, rc, rsa, rsb, HEAD_DIM // 2, scale)


def _inproj(xz, mod, w_in, tabs, sizes, n_ctx):
    bsz, s, d = xz.shape
    tm = _pick_chunk(s, INPROJ_BLOCK)
    assert tm % TOKEN_BLOCK == 0, "the output projection reads the S5 arrays in TOKEN_BLOCK pieces"
    nrow = mod.shape[0]
    offs = np.concatenate([[0], np.cumsum(sizes)])
    col = lambda i: w_in[:, int(offs[i]):int(offs[i + 1])]
    c_idx, r_idx = (0, 1, 3, 4, 6, 7), (2, 5)
    c_dtypes = (BF16, BF16, F32, BF16, F32)
    r_dtypes = (BF16, F32)
    c_sizes = tuple(sizes[i] for i in c_idx)
    r_sizes = tuple(sizes[i] for i in r_idx)
    w_c_t = jnp.concatenate([col(i) for i in c_idx], axis=1).T
    w_rest = jnp.concatenate([col(i) for i in r_idx], axis=1)
    sub = tm // ATT_BLOCK
    tok = lambda width: pl.BlockSpec((None, tm, width), lambda b, j: (b, j, 0))
    chan = lambda width: pl.BlockSpec((sub, None, width, ATT_BLOCK), lambda b, j: (j, b, 0, 0))
    tab = pl.BlockSpec((tm, LANES), lambda b, j: (j, 0))
    tab_t = pl.BlockSpec((HEAD_DIM, tm), lambda b, j: (0, j))
    return pl.pallas_call(
        functools.partial(_inproj_kernel, n_ctx=n_ctx, c_sizes=c_sizes, sizes=r_sizes),
        grid=(bsz, s // tm),
        in_specs=[tok(d),
                  pl.BlockSpec((None, 6, d), lambda b, j: (b, 0, 0)),
                  pl.BlockSpec((None, 6, d), lambda b, j: (nrow - 1, 0, 0)),
                  pl.BlockSpec((sum(c_sizes), d), lambda b, j: (0, 0)),
                  pl.BlockSpec((d, sum(r_sizes)), lambda b, j: (0, 0)),
                  tab_t, tab_t, tab_t, tab_t, tab, tab, tab, tab, tab, tab],
        out_specs=[pl.BlockSpec((None, None, c_sizes[0], tm), lambda b, j: (j, b, 0, 0))]
                  + [chan(w) for w in c_sizes[1:]] + [tok(w) for w in r_sizes],
        out_shape=[jax.ShapeDtypeStruct((s // tm, bsz, c_sizes[0], tm), F32)]
                  + [jax.ShapeDtypeStruct((s // ATT_BLOCK, bsz, w, ATT_BLOCK), dt)
                     for w, dt in zip(c_sizes[1:], c_dtypes)]
                  + [jax.ShapeDtypeStruct((bsz, s, w), dt) for w, dt in zip(r_sizes, r_dtypes)],
        compiler_params=_cparams(("parallel", "parallel")),
        name="inproj",
    )(xz, mod, mod, w_c_t, w_rest, *tabs)


def _rope_tables(t_len, n_ctx):
    t = jnp.arange(t_len)
    rows = (t // GRID_W).astype(F32)
    cols = (t % GRID_W).astype(F32)
    pos = t.astype(F32)

    def angles(p, dim):
        inv_freq = ROPE_BASE ** (-jnp.arange(0, dim, 2, dtype=F32) / dim)
        return p[:, None] * inv_freq[None, :]

    def head_tables(angs):
        cos = jnp.concatenate([jnp.concatenate([jnp.cos(a), jnp.cos(a)], -1) for a in angs], -1)
        sa = jnp.concatenate([jnp.concatenate([-jnp.sin(a), jnp.zeros_like(a)], -1) for a in angs], -1)
        sb = jnp.concatenate([jnp.concatenate([jnp.zeros_like(a), jnp.sin(a)], -1) for a in angs], -1)
        rep = LANES // HEAD_DIM
        return tuple(jnp.tile(x, (1, rep)) for x in (cos, sa, sb))

    att = head_tables([angles(rows, HEAD_DIM // 2), angles(cols, HEAD_DIM // 2)])
    ret = head_tables([angles(pos, HEAD_DIM)])
    signed_t = lambda tb: (tb[0][:, :HEAD_DIM].T, (tb[1] + tb[2])[:, :HEAD_DIM].T)

    def with_ctx(tb, axis, is_cos):
        shape = list(tb.shape)
        shape[axis] = n_ctx
        lead = jnp.ones(shape, F32) if is_cos else jnp.zeros(shape, F32)
        return jnp.concatenate([lead, tb], axis=axis)

    chan = signed_t(att) + signed_t(ret)
    toks = att + ret
    return (tuple(with_ctx(tb, 1, k % 2 == 0) for k, tb in enumerate(chan))
            + tuple(with_ctx(tb, 0, k % 3 == 0) for k, tb in enumerate(toks)))


def _s5_kernel(u_hbm, tab_ref, wsf_ref, wsb_ref, wrf_ref, wrb_ref, lam_ref, y_ref,
               lhs_ref, sem, m_ref, acc_ref, cf_ref, cb_ref, pf_ref, pb_ref, *, bsz, nz_chunks):
    r_blk, ch, tm = y_ref.shape
    c = S5_CHUNK
    kpb = tm // c
    n_chunks = (r_blk // bsz) * kpb
    g = pl.program_id(0)
    slot = g % 2

    def fetch(group, s):
        for i in range(ch):
            p, half = divmod(i, 2)
            for k in range(kpb):
                pltpu.make_async_copy(u_hbm.at[:, group * ch + i, pl.ds(k * c, c)],
                                      lhs_ref.at[s, p, pl.ds(k * r_blk, r_blk), pl.ds(half * c, c)],
                                      sem.at[s]).start()

    @pl.when(g == 0)
    def _():
        fetch(0, 0)

    @pl.when(g + 1 < pl.num_programs(0))
    def _():
        fetch(g + 1, 1 - slot)

    pltpu.make_async_copy(lhs_ref.at[1 - slot], lhs_ref.at[slot], sem.at[slot]).wait()
    acc_ref[...] = jnp.zeros_like(acc_ref)
    cf_ref[...] = jnp.zeros_like(cf_ref)
    cb_ref[...] = jnp.zeros_like(cb_ref)

    def pair(p, carry):
        for ih in range(2):
            for o in range(ch):
                lag_row = jnp.broadcast_to(tab_ref[2 * p + ih, o:o + 1, :], (16, 2 * c))
                tab = pltpu.roll(lag_row, 0, 1, stride=1, stride_axis=0).astype(BF16)
                for r2 in range(c // 16):
                    lo = c - 16 * r2
                    m_ref[ih * c + 16 * r2:ih * c + 16 * r2 + 16, o * c:(o + 1) * c] = tab[:, lo:lo + c]
        lhs = lhs_ref[slot, p].astype(BF16)
        acc_ref[...] += _dot(lhs, m_ref[...])
        cf_ref[...] += _dot(lhs, wsf_ref[p])
        cb_ref[...] += _dot(lhs, wsb_ref[p])
        return carry

    lax.fori_loop(0, ch // 2, pair, 0)
    lam = lam_ref[...]
    nst = lam.shape[1] // 2

    def advance(s, a, bc, drive):
        return s * a + pltpu.roll(s, nst, 1) * bc + drive

    def sweep(order, c_ref, p_ref, a, bc):
        s = jnp.zeros((bsz, lam.shape[1]), F32)
        for n in order:
            j, k = divmod(n, kpb)
            rows = slice(k * r_blk + j * bsz, k * r_blk + (j + 1) * bsz)
            p_ref[rows, :] = s
            s = advance(s, a, bc, c_ref[rows, :])

    sweep(list(range(n_chunks)), cf_ref, pf_ref, lam[0:1], lam[1:2])
    order_b = list(range(nz_chunks - 1, -1, -1)) + list(range(n_chunks - 1, nz_chunks - 1, -1))
    sweep(order_b, cb_ref, pb_ref, lam[2:3], lam[3:4])
    acc_ref[...] += (_dot(pf_ref[...].astype(BF16), wrf_ref[...])
                     + _dot(pb_ref[...].astype(BF16), wrb_ref[...]))
    for o in range(ch):
        for k in range(kpb):
            y_ref[:, o, k * c:(k + 1) * c] = acc_ref[k * r_blk:(k + 1) * r_blk, o * c:(o + 1) * c]


def _s5_weights(lam_re, lam_im, log_step, b_re, b_im, c_re, c_im, chunk):
    hp = lax.Precision.HIGHEST
    lam = lax.complex(lam_re.astype(F32), lam_im.astype(F32))
    lam_dt = lam * jnp.exp(log_step.astype(F32))[..., None]
    lam_bar = jnp.exp(lam_dt)
    b_bar = lax.complex(b_re.astype(F32), b_im.astype(F32)) * ((lam_bar - 1.0) / lam)[..., None]
    c_mat = lax.complex(c_re.astype(F32), c_im.astype(F32))
    g, n = lam.shape[1], lam.shape[2]
    ch = b_bar.shape[-1]
    steps = jnp.arange(chunk + 1, dtype=F32)
    pw = jnp.exp(steps[None, :, None, None] * lam_dt[:, None])
    kern = jnp.einsum('zgon,zdgn,zgni->zgdoi', c_mat, pw[:, :chunk], b_bar, precision=hp).real
    zero_lag = kern[0, :, 0] + kern[1, :, 0]
    lag_table = jnp.concatenate([jnp.zeros_like(zero_lag)[:, None], kern[1, :, :0:-1], zero_lag[:, None],
                                 kern[0, :, 1:]], axis=1).transpose(0, 3, 2, 1)

    def state_in(pw_s, b_dir):
        w = pw_s[:, :, :, None] * b_dir[None]
        w = jnp.concatenate([w.real, w.imag], axis=2)
        return w.transpose(1, 3, 0, 2).reshape(g, ch // 2, 2 * chunk, 2 * n)

    def state_out(pw_t, c_dir):
        w = c_dir[None] * pw_t[:, :, None, :]
        w = jnp.concatenate([w.real, -w.imag], axis=3)
        return w.transpose(1, 3, 2, 0).reshape(g, 2 * n, ch * chunk)

    wsf = state_in(pw[0, chunk - 1::-1][:chunk], b_bar[0])
    wsb = state_in(pw[1, :chunk], b_bar[1])
    wrf = state_out(pw[0, 1:chunk + 1], c_mat[0])
    wrb = state_out(pw[1, chunk:0:-1], c_mat[1])
    lam_c = pw[:, chunk]
    rows = []
    for z in range(2):
        rows.append(jnp.concatenate([lam_c[z].real, lam_c[z].real], -1))
        rows.append(jnp.concatenate([-lam_c[z].imag, lam_c[z].imag], -1))
    lam_rows = jnp.stack(rows + [jnp.zeros_like(rows[0])] * 4, axis=1)
    return lag_table, wsf.astype(BF16), wsb.astype(BF16), wrf.astype(BF16), wrb.astype(BF16), lam_rows


def _s5_scan(ut, weights, n_ctx):
    nblk, bsz, width, tm = ut.shape
    tab, wsf, wsb, wrf, wrb, lam_rows = weights
    g, ch = tab.shape[0], tab.shape[1]
    c = S5_CHUNK
    nst2 = wsf.shape[3]
    r_blk = nblk * bsz
    rows = r_blk * (tm // c)
    grp = lambda *shape: pl.BlockSpec((None,) + shape, lambda i: (i,) + (0,) * len(shape))
    tok = pl.BlockSpec((r_blk, ch, tm), lambda i: (0, i, 0))
    y = pl.pallas_call(
        functools.partial(_s5_kernel, bsz=bsz, nz_chunks=n_ctx // c),
        grid=(g,),
        in_specs=[pl.BlockSpec(memory_space=pl.ANY), grp(ch, ch, 2 * c), grp(ch // 2, 2 * c, nst2), grp(ch // 2, 2 * c, nst2),
                  grp(nst2, ch * c), grp(nst2, ch * c), grp(8, nst2)],
        out_specs=tok,
        out_shape=jax.ShapeDtypeStruct((r_blk, width, tm), F32),
        scratch_shapes=[pltpu.VMEM((2, ch // 2, rows, 2 * c), F32), pltpu.SemaphoreType.DMA((2,)),
                        pltpu.VMEM((2 * c, ch * c), BF16),
                        pltpu.VMEM((rows, ch * c), F32)] + [pltpu.VMEM((rows, nst2), F32)] * 4,
        compiler_params=_cparams(("arbitrary",)),
        name="s5_scan",
    )(ut.reshape(r_blk, width, tm), tab, wsf, wsb, wrf, wrb, lam_rows)
    return y.reshape(nblk, bsz, width, tm)


def _attn_kernel(*refs, q_blocks, **static):
    for sb in range(q_blocks):
        _attn_block(*refs, sb=sb, q_blocks=q_blocks, **static)


def _attn_block(sink_ref, q_ref, k_ref, v_ref, o_ref, s_ref, e_ref, *, sb, q_blocks, n_ctx, t_len, q_off,
                kv_heads):
    qi = pl.program_id(1) * q_blocks + sb + q_off
    nzb = n_ctx // ATT_BLOCK
    band = 3 * ATT_BLOCK
    wide = ATT_REP * ATT_BLOCK
    q = q_ref[sb].astype(BF16)
    kc = k_ref[0:n_ctx, :].astype(BF16)
    vc = jnp.concatenate([v_ref[p] for p in range(nzb)], axis=1).astype(BF16)

    def rows(a, i):
        return a[i * HEAD_DIM:(i + 1) * HEAD_DIM]

    def q_group(g):
        qg = jnp.concatenate([rows(q, g * ATT_REP + r) for r in range(ATT_REP)], axis=1)
        zero = jnp.zeros_like(qg)
        return jnp.concatenate([qg if gg == g else zero for gg in range(kv_heads)], axis=0)

    ones_rows = 16

    def v_aug(v, g):
        return jnp.concatenate([rows(v, g), jnp.ones((ones_rows, v.shape[1]), BF16)], axis=0)

    def finish(g, o_aug, extra_den):
        o = o_aug[:HEAD_DIM] / (o_aug[HEAD_DIM:HEAD_DIM + 1] + extra_den)
        for r in range(ATT_REP):
            h = g * ATT_REP + r
            o_ref[sb, h * HEAD_DIM:(h + 1) * HEAD_DIM, :] = (
                o[:, r * ATT_BLOCK:(r + 1) * ATT_BLOCK].astype(o_ref.dtype))

    @pl.when(qi < nzb)
    def _():
        for g in range(kv_heads):
            sink = sink_ref[g:g + 1, :]
            s = _dot(kc, q_group(g))
            m = jnp.maximum(jnp.max(s, axis=0, keepdims=True), sink)
            e = jnp.exp2(s - m)
            finish(g, _dot(v_aug(vc, g), e.astype(BF16)), jnp.exp2(sink - m))

    @pl.when(qi >= nzb)
    def _():
        n = qi - nzb
        start = jnp.clip((n - 1) * ATT_BLOCK, 0, t_len - band)
        kl = k_ref[pl.ds(pl.multiple_of(n_ctx + start, ATT_BLOCK), band), :].astype(BF16)
        p0 = (n_ctx + start) // ATT_BLOCK
        vl = jnp.concatenate([v_ref[p0 + t] for t in range(3)], axis=1).astype(BF16)
        k_pos = start + lax.broadcasted_iota(jnp.int32, (band, wide), 0)
        q_pos = n * ATT_BLOCK + (lax.broadcasted_iota(jnp.int32, (band, wide), 1) & (ATT_BLOCK - 1))
        valid = jnp.abs(k_pos - q_pos) <= WINDOW
        v_all = jnp.concatenate([vl, vc], axis=1)
        n_keys = band + n_ctx
        row_max = []
        for g in range(kv_heads):
            qg = q_group(g)
            m = sink_ref[g:g + 1, :]
            for j in range(0, n_keys, ATT_BLOCK):
                if j < band:
                    s = jnp.where(valid[j:j + ATT_BLOCK], _dot(kl[j:j + ATT_BLOCK], qg), NEG_INF)
                else:
                    s = _dot(kc[j - band:j - band + ATT_BLOCK], qg)
                s_ref[g, j:j + ATT_BLOCK, :] = s
                m = jnp.maximum(m, jnp.max(s, axis=0, keepdims=True))
            row_max.append(m)
        for g in range(kv_heads):
            m = row_max[g]
            for j in range(0, n_keys, ATT_BLOCK):
                e_ref[g, j:j + ATT_BLOCK, :] = jnp.exp2(s_ref[g, j:j + ATT_BLOCK, :] - m).astype(BF16)
            finish(g, _dot(v_aug(v_all, g), e_ref[g]), jnp.exp2(sink_ref[g:g + 1, :] - m))


def _attention(qat, ka, vat, sink, n_ctx, need_ctx):
    nblk, bsz, qw, _ = qat.shape
    s, kvw = ka.shape[1], ka.shape[2]
    t_len = s - n_ctx
    kv_heads = kvw // HEAD_DIM
    q_off = 0 if need_ctx else n_ctx // ATT_BLOCK
    nq = nblk - q_off
    qb = ATT_Q_BLOCKS
    assert nq % qb == 0 and q_off % qb == 0
    sink_rows = jnp.repeat(sink.astype(F32).reshape(kv_heads, ATT_REP), ATT_BLOCK, axis=1) * LOG2_E
    return pl.pallas_call(
        functools.partial(_attn_kernel, q_blocks=qb, n_ctx=n_ctx, t_len=t_len, q_off=q_off, kv_heads=kv_heads),
        grid=(bsz, nq // qb),
        in_specs=[pl.BlockSpec(sink_rows.shape, lambda b, j: (0, 0)),
                  pl.BlockSpec((qb, None, qw, ATT_BLOCK), lambda b, j: (j + q_off // qb, b, 0, 0)),
                  pl.BlockSpec((None, s, kvw), lambda b, j: (b, 0, 0)),
                  pl.BlockSpec((nblk, None, kvw, ATT_BLOCK), lambda b, j: (0, b, 0, 0))],
        out_specs=pl.BlockSpec((qb, None, qw, ATT_BLOCK), lambda b, j: (j, b, 0, 0)),
        out_shape=jax.ShapeDtypeStruct((nq, bsz, qw, ATT_BLOCK), BF16),
        scratch_shapes=[pltpu.VMEM((kv_heads, 3 * ATT_BLOCK + n_ctx, ATT_REP * ATT_BLOCK), F32),
                        pltpu.VMEM((kv_heads, 3 * ATT_BLOCK + n_ctx, ATT_REP * ATT_BLOCK), BF16)],
        compiler_params=_cparams(("parallel", "arbitrary")),
        name="window_attention",
    )(sink_rows, qat, ka, vat)


def _ret_kernel(lg_ref, q_ref, k_ref, v_ref, g_ref, o_ref, acc_ref, sf_ref, sb_ref, *, nz_chunks, heads):
    c = RET_CHUNK
    n_chunks, w, _ = q_ref.shape
    f32 = lambda a: a.astype(F32)

    def per_head(shape, axis, group, direction):
        owner = lax.broadcasted_iota(jnp.int32, shape, axis) // group
        out = jnp.zeros(shape, F32)
        for h in range(heads):
            out = jnp.where(owner == h, lg_ref[direction, h], out)
        return out

    row_i = f32(lax.broadcasted_iota(jnp.int32, (w, c), 1))
    qw_f = jnp.exp(per_head((w, c), 0, HEAD_DIM, 0) * (row_i + 1.0))
    qw_b = jnp.exp(per_head((w, c), 0, HEAD_DIM, 1) * (c - row_i))
    key_j = f32(lax.broadcasted_iota(jnp.int32, (c, w), 0))
    kw_f = jnp.exp(per_head((c, w), 1, HEAD_DIM, 0) * (c - 1.0 - key_j))
    kw_b = jnp.exp(per_head((c, w), 1, HEAD_DIM, 1) * key_j)
    dec_f = jnp.exp(per_head((w, 1), 0, HEAD_DIM, 0) * c)
    dec_b = jnp.exp(per_head((w, 1), 0, HEAD_DIM, 1) * c)
    wide = heads * c
    diff = f32((lax.broadcasted_iota(jnp.int32, (c, wide), 1) & (c - 1)) - lax.broadcasted_iota(jnp.int32, (c, wide), 0))
    dec_t = jnp.where(diff >= 0, jnp.exp(per_head((c, wide), 1, c, 0) * jnp.maximum(diff, 0.0)),
                      jnp.exp(per_head((c, wide), 1, c, 1) * jnp.maximum(-diff, 0.0)))
    own_wide = (lax.broadcasted_iota(jnp.int32, (w, wide), 0) // HEAD_DIM
                == lax.broadcasted_iota(jnp.int32, (w, wide), 1) // c)
    own_sq = f32(lax.broadcasted_iota(jnp.int32, (w, w), 0) // HEAD_DIM
                 == lax.broadcasted_iota(jnp.int32, (w, w), 1) // HEAD_DIM)
    sf_ref[...] = jnp.zeros_like(sf_ref)
    sb_ref[...] = jnp.zeros_like(sb_ref)

    def block_diag(x):
        return jnp.where(own_wide, jnp.concatenate([x] * heads, axis=1), 0.0).astype(BF16)

    def fwd(n, carry):
        qt, vt = q_ref[n], v_ref[n]
        k = k_ref[pl.ds(pl.multiple_of(n * c, c), c), :]
        scores_t = _dot(k.astype(BF16), block_diag(qt)) * dec_t
        stacked = jnp.concatenate([scores_t[:, h * c:(h + 1) * c] for h in range(heads)], axis=0)
        o = _dot(block_diag(vt), stacked.astype(BF16))
        s_prev = sf_ref[...]
        o = o + _dot(s_prev.astype(BF16), (qt * qw_f).astype(BF16))
        sf_ref[...] = dec_f * s_prev + own_sq * _dot(vt.astype(BF16), (k * kw_f).astype(BF16))
        acc_ref[n] = o
        return carry

    lax.fori_loop(0, n_chunks, fwd, 0)

    def bwd(i, carry):
        n = jnp.where(i < nz_chunks, nz_chunks - 1 - i, n_chunks - 1 - i + nz_chunks)
        qt, vt = q_ref[n], v_ref[n]
        k = k_ref[pl.ds(pl.multiple_of(n * c, c), c), :]
        s_prev = sb_ref[...]
        o = acc_ref[n] + _dot(s_prev.astype(BF16), (qt * qw_b).astype(BF16))
        sb_ref[...] = dec_b * s_prev + own_sq * _dot(vt.astype(BF16), (k * kw_b).astype(BF16))
        gate = _silu(g_ref[n])
        for h in range(heads):
            rows = slice(h * HEAD_DIM, (h + 1) * HEAD_DIM)
            oh = o[rows]
            mu = jnp.mean(oh, axis=0, keepdims=True)
            oc = oh - mu
            var = jnp.mean(oc * oc, axis=0, keepdims=True)
            o_ref[n, rows, :] = (oc * lax.rsqrt(var + LN_EPS) * gate[rows]).astype(o_ref.dtype)
        return carry

    lax.fori_loop(0, n_chunks, bwd, 0)


def _retention(qrt, kr, vrt, grt, log_gamma, n_ctx):
    n_chunks, bsz, w, c = qrt.shape
    s = kr.shape[1]
    chan = pl.BlockSpec((n_chunks, None, w, c), lambda b: (0, b, 0, 0))
    return pl.pallas_call(
        functools.partial(_ret_kernel, nz_chunks=n_ctx // RET_CHUNK, heads=w // HEAD_DIM),
        grid=(bsz,),
        in_specs=[pl.BlockSpec(memory_space=pltpu.SMEM), chan, pl.BlockSpec((None, s, w), lambda b: (b, 0, 0)),
                  chan, chan],
        out_specs=chan,
        out_shape=jax.ShapeDtypeStruct((n_chunks, bsz, w, c), BF16),
        scratch_shapes=[pltpu.VMEM((n_chunks, w, c), F32), pltpu.VMEM((w, w), F32), pltpu.VMEM((w, w), F32)],
        compiler_params=_cparams(("parallel",)),
        name="retention",
    )(log_gamma.astype(F32), qrt, kr, vrt, grt)


def _outproj_kernel(*refs, nzb, alpha, with_router, w_s5, w_att, n_exp):
    if with_router:
        (y_ref, u_ref, a_ref, r_ref, x_ref, modx_ref, modz_ref, d_ref, wg_ref, bg_ref, wo_ref,
         g1_ref, b1_ref, rt_ref, x1_ref, fx_ref, gate_ref) = refs
    else:
        (y_ref, u_ref, a_ref, r_ref, x_ref, modx_ref, modz_ref, d_ref, wg_ref, bg_ref, wo_ref,
         g1_ref, b1_ref, x1_ref, fx_ref) = refs
    is_ctx = pl.program_id(1) < nzb
    mod = jnp.where(is_ctx, modz_ref[...], modx_ref[...])
    g = _gelu_tanh(y_ref[...] + d_ref[...] * u_ref[...])
    s5 = g * _sigmoid(_dot(wg_ref[...], g.astype(BF16)) + bg_ref[...])
    def chan_major(ref, lo, hi):
        return jnp.concatenate([_dot_tn(ref[k].astype(BF16), wo_ref[lo:hi, :]) for k in range(ref.shape[0])],
                               axis=0)

    mix = (_dot_tn(s5.astype(BF16), wo_ref[0:w_s5, :]) + chan_major(a_ref, w_s5, w_s5 + w_att)
           + chan_major(r_ref, w_s5 + w_att, wo_ref.shape[0]))
    x1 = _ln(alpha * x_ref[...] + mod[2:3] * mix) * g1_ref[...] + b1_ref[...]
    x1_ref[...] = x1
    fx = _ln(x1) * (1.0 + mod[4:5]) + mod[3:4]
    fx_ref[...] = fx.astype(fx_ref.dtype)
    if with_router:
        lane = lax.broadcasted_iota(jnp.int32, (fx.shape[0], LANES), 1)
        logits = jnp.where(lane < n_exp, _dot3(fx, rt_ref[...]), -jnp.inf)
        m1 = jnp.max(logits, axis=1, keepdims=True)
        i1 = jnp.min(jnp.where(logits == m1, lane, LANES), axis=1, keepdims=True)
        rest = jnp.where(lane == i1, -jnp.inf, logits)
        m2 = jnp.max(rest, axis=1, keepdims=True)
        i2 = jnp.min(jnp.where(rest == m2, lane, LANES), axis=1, keepdims=True)
        e2 = jnp.exp(m2 - m1)
        den = 1.0 + e2
        route = jnp.where(lane == 0, 1.0 / den, jnp.where(lane == 1, e2 / den, 0.0))
        route = jnp.where(lane == 2, i1.astype(F32), jnp.where(lane == 3, i2.astype(F32), route))
        gate_ref[...] = route


def _outproj(y_s5, u, o_att, o_ret, xz, mod, s5_d, w_glu, b_glu, w_out, ln_g, ln_b, router, n_ctx, need_ctx,
             alpha):
    bsz, s, d = xz.shape
    tm = TOKEN_BLOCK
    nzb = n_ctx // tm
    off = 0 if need_ctx else nzb
    nblk = s // tm - off
    s_out = nblk * tm
    nrow = mod.shape[0]
    w_s5, w_att = y_s5.shape[2], o_att.shape[2]
    sub = tm // ATT_BLOCK
    att_off = off if o_att.shape[0] == s // ATT_BLOCK else 0
    att = pl.BlockSpec((sub, None, w_att, ATT_BLOCK), lambda b, j: (j + att_off, b, 0, 0))

    def tok(width, shift):
        return pl.BlockSpec((None, tm, width), lambda b, j: (b, j + shift, 0))

    def full(a):
        return pl.BlockSpec(a.shape, lambda b, j: (0,) * a.ndim)

    per_in = y_s5.shape[3] // tm
    chan = pl.BlockSpec((None, None, w_s5, tm), lambda b, j: ((j + off) // per_in, b, 0, (j + off) % per_in))
    vec = lambda a: a.reshape(1, -1).astype(F32)
    col = lambda a: a.reshape(-1, 1).astype(F32)
    consts = [col(s5_d), w_glu.T.astype(BF16), col(b_glu), w_out.astype(BF16), vec(ln_g), vec(ln_b)]
    ret = pl.BlockSpec((sub, None, o_ret.shape[2], ATT_BLOCK), lambda b, j: (j + off, b, 0, 0))
    in_specs = [chan, chan, att, ret, tok(d, off),
                pl.BlockSpec((None, 6, d), lambda b, j: (b, 0, 0)),
                pl.BlockSpec((None, 6, d), lambda b, j: (nrow - 1, 0, 0))] + [full(a) for a in consts]
    out_specs = [tok(d, 0), tok(d, 0)]
    fx_dtype = BF16 if router is None else F32
    out_shape = [jax.ShapeDtypeStruct((bsz, s_out, d), F32), jax.ShapeDtypeStruct((bsz, s_out, d), fx_dtype)]
    args = [y_s5, u, o_att, o_ret, xz, mod, mod] + consts
    with_router = router is not None
    n_exp = 0
    if with_router:
        n_exp = router.shape[1]
        router_pad = jnp.pad(router.astype(F32), ((0, 0), (0, LANES - n_exp)))
        args.append(router_pad)
        in_specs.append(full(router_pad))
        out_specs.append(tok(LANES, 0))
        out_shape.append(jax.ShapeDtypeStruct((bsz, s_out, LANES), F32))
    return pl.pallas_call(
        functools.partial(_outproj_kernel, nzb=nzb - off, alpha=alpha, with_router=with_router,
                          w_s5=w_s5, w_att=w_att, n_exp=n_exp),
        grid=(bsz, nblk),
        in_specs=in_specs,
        out_specs=out_specs,
        out_shape=out_shape,
        compiler_params=_cparams(("parallel", "parallel")),
        name="outproj",
    )(*args)


def _swiglu_into(x_bf16, w1_ref, w3_ref, w2_ref, acc_ref, fc):
    for s in range(0, w1_ref.shape[1], fc):
        h1 = _dot(x_bf16, w1_ref[:, s:s + fc])
        h3 = _dot(x_bf16, w3_ref[:, s:s + fc])
        acc_ref[...] += _dot((_silu(h1) * h3).astype(BF16), w2_ref[s:s + fc, :])


def _ffn_kernel(fx_ref, x1_ref, modx_ref, modz_ref, w1_ref, w3_ref, w2_ref, g2_ref, b2_ref,
                o_ref, acc_ref, *, n_ctx_tokens, alpha, fc):
    tm = acc_ref.shape[0]
    acc_ref[...] = jnp.zeros_like(acc_ref)
    _swiglu_into(fx_ref[...], w1_ref, w3_ref, w2_ref, acc_ref, fc)
    row = pl.program_id(1) * tm + lax.broadcasted_iota(jnp.int32, (tm, 1), 0)
    gate = jnp.where(row < n_ctx_tokens, modz_ref[5:6, :], modx_ref[5:6, :])
    o_ref[...] = _ln(alpha * x1_ref[...] + gate * acc_ref[...]) * g2_ref[...] + b2_ref[...]


def _ffn(fx, x1, mod, w1, w3, w2, ln_g, ln_b, n_ctx_tokens, alpha):
    bsz, s, d = x1.shape
    ff = w1.shape[1]
    tm = _pick_chunk(s, FFN_BLOCK)
    nrow = mod.shape[0]
    tok = pl.BlockSpec((None, tm, d), lambda b, j: (b, j, 0))
    full = lambda a: pl.BlockSpec(a.shape, lambda b, j: (0,) * a.ndim, pipeline_mode=pl.Buffered(1))
    vec = lambda a: a.reshape(1, -1).astype(F32)
    return pl.pallas_call(
        functools.partial(_ffn_kernel, n_ctx_tokens=n_ctx_tokens, alpha=alpha, fc=_pick_chunk(ff, FFN_CHUNK)),
        grid=(bsz, s // tm),
        in_specs=[tok, tok,
                  pl.BlockSpec((None, 6, d), lambda b, j: (b, 0, 0)),
                  pl.BlockSpec((None, 6, d), lambda b, j: (nrow - 1, 0, 0)),
                  full(w1), full(w3), full(w2),
                  pl.BlockSpec((1, d), lambda b, j: (0, 0)), pl.BlockSpec((1, d), lambda b, j: (0, 0))],
        out_specs=tok,
        out_shape=jax.ShapeDtypeStruct((bsz, s, d), F32),
        scratch_shapes=[pltpu.VMEM((tm, d), F32)],
        compiler_params=_cparams(("parallel", "parallel")),
        name="dense_ffn",
    )(fx, x1, mod, mod, w1, w3, w2, vec(ln_g), vec(ln_b))


MOE_TILE = 1024


def _route_plan(e1, e2, n_exp, tile):
    n = e1.shape[0]
    pair_e = jnp.stack([e1, e2], axis=1).reshape(-1)
    onehot = (pair_e[:, None] == jnp.arange(n_exp, dtype=jnp.int32)[None, :]).astype(jnp.int32)
    before = jnp.cumsum(onehot, axis=0) - onehot
    rank = jnp.sum(before * onehot, axis=1)
    counts = jnp.sum(onehot, axis=0)
    padded = (counts + tile - 1) // tile * tile
    ends = jnp.cumsum(padded)
    starts = ends - padded
    dest = starts[pair_e] + rank
    n_rows = (2 * n + n_exp * (tile - 1)) // tile * tile
    n_tiles = n_rows // tile
    tile_start = jnp.arange(n_tiles, dtype=jnp.int32) * tile
    tile_expert = jnp.minimum(jnp.sum((tile_start[:, None] >= ends[None, :]).astype(jnp.int32), axis=1),
                              n_exp - 1)
    by_expert = jnp.sort(pair_e * (2 * n) + jnp.arange(2 * n, dtype=jnp.int32)) % (2 * n)
    row = jnp.arange(n_rows, dtype=jnp.int32)
    row_e = jnp.repeat(tile_expert, tile)
    in_group = row - starts[row_e]
    src = jnp.minimum(in_group + (jnp.cumsum(counts) - counts)[row_e], 2 * n - 1)
    row_token = jnp.where(in_group < counts[row_e], by_expert[src] // 2, 0)
    n_used = (ends[-1] // tile).astype(jnp.int32).reshape(1)
    return row_token, dest.reshape(n, 2), tile_expert, n_used


def _moe_gemm_kernel(te_ref, nused_ref, tok_ref, tokn_ref, x_hbm, w1_ref, w3_ref, w2_ref, y_ref,
                     xbuf_ref, sem, *, fc, tile):
    i = pl.program_id(0)
    f = pl.program_id(1)
    slot = i % 2
    n_used = nused_ref[0]

    def row_copy(t_ref, r, s):
        return pltpu.make_async_copy(x_hbm.at[pl.ds(t_ref[0, r], 1)], xbuf_ref.at[s, pl.ds(r, 1)], sem.at[s])

    def issue(t_ref, s):
        def body(it, carry):
            base = pl.multiple_of(it * 8, 8)
            for k in range(8):
                row_copy(t_ref, base + k, s).start()
            return carry
        lax.fori_loop(0, tile // 8, body, 0)

    @pl.when(f == 0)
    def _():
        y_ref[...] = jnp.zeros_like(y_ref)

        @pl.when(i == 0)
        def _():
            issue(tok_ref, 0)

        @pl.when(i + 1 < n_used)
        def _():
            issue(tokn_ref, 1 - slot)

        @pl.when(i < n_used)
        def _():
            pltpu.make_async_copy(x_hbm.at[pl.ds(0, tile)], xbuf_ref.at[slot], sem.at[slot]).wait()

    @pl.when(i < n_used)
    def _():
        _swiglu_into(xbuf_ref[slot].astype(BF16), w1_ref, w3_ref, w2_ref, y_ref, fc)


def _moe_gemm(x_flat, row_token, tile_expert, n_used, w1, w3, w2, tile):
    n_rows = row_token.shape[0]
    n_tiles = n_rows // tile
    d = x_flat.shape[1]
    ff = w1.shape[2]
    nf = 2 if ff % (2 * LANES) == 0 else 1
    tf = ff // nf
    last = lambda i, nu: jnp.minimum(i, nu[0] - 1)
    fsel = lambda i, f, nu: jnp.where(i < nu[0], f, nf - 1)
    tok = lambda imap: pl.BlockSpec((None, 1, tile), imap, memory_space=pltpu.SMEM)
    tokens = row_token.reshape(n_tiles, 1, tile)
    return pl.pallas_call(
        functools.partial(_moe_gemm_kernel, fc=_pick_chunk(tf, MOE_CHUNK), tile=tile),
        grid_spec=pltpu.PrefetchScalarGridSpec(
            num_scalar_prefetch=2,
            grid=(n_tiles, nf),
            in_specs=[tok(lambda i, f, te, nu: (i, 0, 0)),
                      tok(lambda i, f, te, nu: (jnp.minimum(i + 1, n_tiles - 1), 0, 0)),
                      pl.BlockSpec(memory_space=pl.ANY),
                      pl.BlockSpec((None, d, tf), lambda i, f, te, nu: (te[last(i, nu)], 0, fsel(i, f, nu))),
                      pl.BlockSpec((None, d, tf), lambda i, f, te, nu: (te[last(i, nu)], 0, fsel(i, f, nu))),
                      pl.BlockSpec((None, tf, d), lambda i, f, te, nu: (te[last(i, nu)], fsel(i, f, nu), 0))],
            out_specs=pl.BlockSpec((tile, d), lambda i, f, te, nu: (i, 0)),
            scratch_shapes=[pltpu.VMEM((2, tile, d), F32), pltpu.SemaphoreType.DMA((2,))]),
        out_shape=jax.ShapeDtypeStruct((n_rows, d), F32),
        compiler_params=_cparams(("arbitrary", "arbitrary")),
        name="moe_gemm",
    )(tile_expert, n_used, tokens, tokens, x_flat, w1, w3, w2)


def _moe_combine_kernel(pos_ref, posn_ref, route_ref, x1_ref, modx_ref, modz_ref, g2_ref, b2_ref, y_hbm,
                        o_ref, buf_ref, sem, *, nzb, nblk, alpha, tm):
    i = pl.program_id(0)
    slot = i % 2

    def row_copy(p_ref, r, k, s):
        return pltpu.make_async_copy(y_hbm.at[pl.ds(p_ref[0, k * tm + r], 1)],
                                     buf_ref.at[s, pl.ds(k * tm + r, 1)], sem.at[s])

    def issue(p_ref, s):
        def body(r, carry):
            row_copy(p_ref, r, 0, s).start()
            row_copy(p_ref, r, 1, s).start()
            return carry
        lax.fori_loop(0, tm, body, 0, unroll=8)

    @pl.when(i == 0)
    def _():
        issue(pos_ref, 0)

    @pl.when(i + 1 < pl.num_programs(0))
    def _():
        issue(posn_ref, 1 - slot)

    pltpu.make_async_copy(y_hbm.at[pl.ds(0, 2 * tm)], buf_ref.at[slot], sem.at[slot]).wait()
    route = route_ref[...]
    f = route[:, 0:1] * buf_ref[slot, 0:tm] + route[:, 1:2] * buf_ref[slot, tm:2 * tm]
    is_ctx = (i % nblk) < nzb
    mod = jnp.where(is_ctx, modz_ref[...], modx_ref[...])
    o_ref[...] = _ln(alpha * x1_ref[...] + mod[5:6] * f) * g2_ref[...] + b2_ref[...]


def _moe_combine(ys, pos, route, x1, mod, ln_g, ln_b, n_ctx_tokens, alpha):
    bsz, s, d = x1.shape
    tm = TOKEN_BLOCK
    nblk = s // tm
    n_steps = bsz * nblk
    nrow = mod.shape[0]
    pos_steps = pos.reshape(n_steps, tm, 2).transpose(0, 2, 1).reshape(n_steps, 1, 2 * tm)
    tok = lambda width: pl.BlockSpec((tm, width), lambda i: (i, 0))
    smem = lambda imap: pl.BlockSpec((None, 1, 2 * tm), imap, memory_space=pltpu.SMEM)
    vec = lambda a: a.reshape(1, -1).astype(F32)
    out = pl.pallas_call(
        functools.partial(_moe_combine_kernel, nzb=n_ctx_tokens // tm, nblk=nblk, alpha=alpha, tm=tm),
        grid=(n_steps,),
        in_specs=[smem(lambda i: (i, 0, 0)),
                  smem(lambda i: (jnp.minimum(i + 1, n_steps - 1), 0, 0)),
                  tok(LANES), tok(d),
                  pl.BlockSpec((None, 6, d), lambda i: (i // nblk, 0, 0)),
                  pl.BlockSpec((None, 6, d), lambda i: (nrow - 1, 0, 0)),
                  pl.BlockSpec((1, d), lambda i: (0, 0)), pl.BlockSpec((1, d), lambda i: (0, 0)),
                  pl.BlockSpec(memory_space=pl.ANY)],
        out_specs=tok(d),
        out_shape=jax.ShapeDtypeStruct((bsz * s, d), F32),
        scratch_shapes=[pltpu.VMEM((2, 2 * tm, d), F32), pltpu.SemaphoreType.DMA((2,))],
        compiler_params=_cparams(("arbitrary",)),
        name="moe_combine",
    )(pos_steps, pos_steps, route.reshape(bsz * s, LANES), x1.reshape(bsz * s, d), mod, mod,
      vec(ln_g), vec(ln_b), ys)
    return out.reshape(bsz, s, d)


def _moe(fx, x1, route, mod, w1, w3, w2, ln_g, ln_b, n_ctx_tokens, alpha):
    bsz, s, d = x1.shape
    n_exp = w1.shape[0]
    idx = route.reshape(bsz * s, LANES)[:, 2:4].astype(jnp.int32)
    row_token, pos, tile_expert, n_used = _route_plan(idx[:, 0], idx[:, 1], n_exp, MOE_TILE)
    ys = _moe_gemm(fx.reshape(bsz * s, d), row_token, tile_expert, n_used, w1, w3, w2, MOE_TILE)
    return _moe_combine(ys, pos, route, x1, mod, ln_g, ln_b, n_ctx_tokens, alpha)


def kernel(x, c, ctx, c_ctx, w_mod, b_mod, w_in, s5_lam_re, s5_lam_im, s5_log_step, s5_b_re, s5_b_im,
           s5_c_re, s5_c_im, s5_d, s5_w_glu, s5_b_glu, attn_sink, ret_log_gamma, w_out,
           ln1_g, ln1_b, ln2_g, ln2_b, ffn_w1, ffn_w3, ffn_w2, moe_router, moe_w1, moe_w3, moe_w2):
    bsz, t_len, d = x.shape
    n_ctx = ctx.shape[1]
    depth = w_in.shape[0]
    alpha = (2 * depth) ** 0.25
    s5_w = s5_d.shape[1]
    att_w = attn_sink.shape[1] * HEAD_DIM
    kv_w = att_w // ATT_REP
    ret_w = ret_log_gamma.shape[2] * HEAD_DIM
    sizes = (s5_w, att_w, kv_w, kv_w, ret_w, ret_w, ret_w, ret_w)
    assert sum(sizes) == w_in.shape[2] and s5_w + att_w + ret_w == w_out.shape[1]
    assert n_ctx % TOKEN_BLOCK == 0 and t_len % TOKEN_BLOCK == 0 and t_len >= 3 * ATT_BLOCK

    pad = (-(bsz + 1)) % 8
    cvec = jnp.concatenate([jnp.zeros((pad, d), F32), c_ctx[None].astype(F32)], axis=0)
    cvec = jnp.concatenate([c.astype(F32), cvec], axis=0)
    mod_all = _modulation(cvec, w_mod.astype(F32), b_mod.astype(F32)).reshape(depth, bsz + pad + 1, 6, d)

    tabs = _rope_tables(t_len, n_ctx)
    xz = jnp.concatenate([ctx, x], axis=1).astype(F32)
    for l in range(depth):
        need_ctx = l < depth - 1
        mod = mod_all[l]
        u, qa, va, qr, vr, gr, ka, kr = _inproj(xz, mod, w_in[l].astype(BF16), tabs, sizes, n_ctx)
        s5w = _s5_weights(s5_lam_re[l], s5_lam_im[l], s5_log_step[l], s5_b_re[l], s5_b_im[l],
                          s5_c_re[l], s5_c_im[l], S5_CHUNK)
        y_s5 = _s5_scan(u, s5w, n_ctx)
        o_att = _attention(qa, ka, va, attn_sink[l], n_ctx, need_ctx)
        o_ret = _retention(qr, kr, vr, gr, ret_log_gamma[l], n_ctx)
        i = l // 2
        router = None if l % 2 == 0 else moe_router[i]
        outs = _outproj(y_s5, u, o_att, o_ret, xz, mod, s5_d[l], s5_w_glu[l], s5_b_glu[l], w_out[l],
                        ln1_g[l], ln1_b[l], router, n_ctx, need_ctx, alpha)
        ctx_tokens = n_ctx if need_ctx else 0
        if l % 2 == 0:
            x1, fx = outs
            xz = _ffn(fx, x1, mod, ffn_w1[i].astype(BF16), ffn_w3[i].astype(BF16), ffn_w2[i].astype(BF16),
                      ln2_g[l], ln2_b[l], ctx_tokens, alpha)
        else:
            x1, fx, route = outs
            xz = _moe(fx, x1, route, mod, moe_w1[i].astype(BF16), moe_w3[i].astype(BF16),
                      moe_w2[i].astype(BF16), ln2_g[l], ln2_b[l], ctx_tokens, alpha)
    return xz if xz.shape[1] == t_len else xz[:, n_ctx:]
```

```python
import functools
import math

import jax
import jax.numpy as jnp
import numpy as np
from jax import lax
from jax.experimental import pallas as pl
from jax.experimental.pallas import tpu as pltpu

F32 = jnp.float32
BF16 = jnp.bfloat16

GRID_W = 64
HEAD_DIM = 64
ATT_REP = 4
WINDOW = 128
ATT_BLOCK = 128
RET_CHUNK = 128
LN_EPS = 1e-5
ROPE_BASE = 10000.0
NEG_INF = -1e30
LOG2_E = math.log2(math.e)

LANES = 128
S5_CHUNK = LANES
TOKEN_BLOCK = 256
INPROJ_BLOCK = 768
FFN_BLOCK = 768
FFN_CHUNK = 512
MOE_CHUNK = 256
MOD_BLOCK = 1536
VMEM_LIMIT = 56 * 1024 * 1024


def _cparams(sem):
    return pltpu.CompilerParams(dimension_semantics=sem, vmem_limit_bytes=VMEM_LIMIT)


def _pick_chunk(total, target):
    best = LANES
    for c in range(LANES, target + 1, LANES):
        if total % c == 0:
            best = c
    return best


def _dot(a, b):
    return jnp.dot(a, b, preferred_element_type=F32)


def _dot_nt(a, b):
    return lax.dot_general(a, b, (((1,), (1,)), ((), ())), preferred_element_type=F32)


def _dot_tn(a, b):
    return lax.dot_general(a, b, (((0,), (0,)), ((), ())), preferred_element_type=F32)


def _split_bf16(a):
    hi = a.astype(BF16)
    lo = (a - hi.astype(F32)).astype(BF16)
    return hi, lo


def _dot3(a, b):
    ah, al = _split_bf16(a)
    bh, bl = _split_bf16(b)
    return _dot(ah, bh) + (_dot(ah, bl) + _dot(al, bh))


def _ln(x):
    mu = jnp.mean(x, axis=-1, keepdims=True)
    xc = x - mu
    var = jnp.mean(xc * xc, axis=-1, keepdims=True)
    return xc * lax.rsqrt(var + LN_EPS)


def _silu(x):
    return x * (1.0 / (1.0 + jnp.exp(-x)))


def _sigmoid(x):
    return 1.0 / (1.0 + jnp.exp(-x))


def _gelu_tanh(x):
    c = math.sqrt(2.0 / math.pi)
    return 0.5 * x * (1.0 + jnp.tanh(c * (x + 0.044715 * (x * x * x))))


def _mod_kernel(c_ref, w_ref, b_ref, o_ref):
    o_ref[...] = _dot3(_silu(c_ref[...]), w_ref[...]) + b_ref[...]


def _modulation(cvec, w_mod, b_mod):
    depth, d, n = w_mod.shape
    rows = cvec.shape[0]
    tn = _pick_chunk(n, MOD_BLOCK)
    return pl.pallas_call(
        _mod_kernel,
        grid=(depth, n // tn),
        in_specs=[pl.BlockSpec((rows, d), lambda l, j: (0, 0)),
                  pl.BlockSpec((None, d, tn), lambda l, j: (l, 0, j)),
                  pl.BlockSpec((None, 1, tn), lambda l, j: (l, 0, j))],
        out_specs=pl.BlockSpec((None, rows, tn), lambda l, j: (l, 0, j)),
        out_shape=jax.ShapeDtypeStruct((depth, rows, n), F32),
        compiler_params=_cparams(("parallel", "parallel")),
        name="modulation",
    )(cvec, w_mod, b_mod.reshape(depth, 1, n))


def _rope_slab(xs, cos, sa, sb, half):
    return xs * cos + pltpu.roll(xs, LANES - half, 1) * sa + pltpu.roll(xs, half, 1) * sb


def _stream_specs(x, ctx, tm, first_block=0):
    d = x.shape[2]
    pieces = tm // TOKEN_BLOCK
    piece = lambda imap: pl.BlockSpec((None, TOKEN_BLOCK, d), imap)
    specs, args = [], []
    for k in range(pieces):
        if ctx is None:
            specs.append(piece(lambda b, j, k=k: (b, (j + first_block) * pieces + k, 0)))
            args.append(x)
        else:
            nzb = ctx.shape[1] // TOKEN_BLOCK
            specs.append(piece(lambda b, j, k=k: (b, jnp.minimum((j + first_block) * pieces + k, nzb - 1), 0)))
            specs.append(piece(lambda b, j, k=k: (b, jnp.maximum((j + first_block) * pieces + k - nzb, 0), 0)))
            args += [ctx, x]
    return specs, args


def _stream_pieces(refs, block, n_ctx, split):
    if not split:
        return [r[...] for r in refs]
    pieces = len(refs) // 2
    return [jnp.where((block * pieces + k) * TOKEN_BLOCK < n_ctx, refs[2 * k][...], refs[2 * k + 1][...])
            for k in range(pieces)]


def _inproj_kernel(*refs, n_stream, split, n_ctx, c_sizes, sizes):
    (modx_ref, modz_ref, wc_ref, w_ref, act_ref, ast_ref, rct_ref, rst_ref,
     ac_ref, asa_ref, asb_ref, rc_ref, rsa_ref, rsb_ref,
     ut_ref, qat_ref, vat_ref, qrt_ref, vrt_ref, grt_ref, ka_ref, kr_ref) = refs[n_stream:]
    x_pieces = _stream_pieces(refs[:n_stream], pl.program_id(1), n_ctx, split)
    tm = len(x_pieces) * TOKEN_BLOCK
    is_ctx = pl.program_id(1) * tm + lax.broadcasted_iota(jnp.int32, (tm, 1), 0) < n_ctx
    shift = jnp.where(is_ctx, modz_ref[0:1, :], modx_ref[0:1, :])
    gain = 1.0 + jnp.where(is_ctx, modz_ref[1:2, :], modx_ref[1:2, :])
    h = jnp.concatenate(
        [(_ln(xp) * gain[k * TOKEN_BLOCK:(k + 1) * TOKEN_BLOCK] + shift[k * TOKEN_BLOCK:(k + 1) * TOKEN_BLOCK]
          ).astype(BF16) for k, xp in enumerate(x_pieces)], axis=0)
    scale = HEAD_DIM ** -0.5

    ct = _dot_nt(wc_ref[...], h)
    c_offs = [int(o) for o in np.concatenate([[0], np.cumsum(c_sizes)])]

    def chan(i):
        return ct[c_offs[i]:c_offs[i + 1]]

    def rope_t(x, cos_ref, sin_ref, part, mul):
        cos_t, sin_t = cos_ref[...], sin_ref[...]
        heads = []
        for hd in range(x.shape[0] // HEAD_DIM):
            xh = x[hd * HEAD_DIM:(hd + 1) * HEAD_DIM]
            pieces = []
            for lo in range(0, HEAD_DIM, 2 * part):
                pieces += [xh[lo + part:lo + 2 * part], xh[lo:lo + part]]
            out = xh * cos_t + jnp.concatenate(pieces, axis=0) * sin_t
            heads.append(out * mul if mul != 1.0 else out)
        return jnp.concatenate(heads, axis=0)

    def put(ref, val):
        for k in range(ref.shape[0]):
            ref[k] = val[:, k * ATT_BLOCK:(k + 1) * ATT_BLOCK].astype(ref.dtype)

    ut_ref[...] = chan(0)
    put(qat_ref, rope_t(chan(1), act_ref, ast_ref, HEAD_DIM // 4, scale * LOG2_E))
    put(vat_ref, chan(2))
    put(qrt_ref, rope_t(chan(3), rct_ref, rst_ref, HEAD_DIM // 2, 1.0))
    put(vrt_ref, chan(4))
    put(grt_ref, chan(5))

    p = _dot(h, w_ref[...])
    ac, asa, asb = ac_ref[...], asa_ref[...], asb_ref[...]
    rc, rsa, rsb = rc_ref[...], rsa_ref[...], rsb_ref[...]
    offs = np.concatenate([[0], np.cumsum(sizes)])

    def rope_cols(i, cos, sa, sb, half, mul):
        blk = p[:, int(offs[i]):int(offs[i + 1])]
        slabs = [_rope_slab(blk[:, s:s + LANES], cos, sa, sb, half) for s in range(0, blk.shape[1], LANES)]
        out = slabs[0] if len(slabs) == 1 else jnp.concatenate(slabs, axis=1)
        return out * mul if mul != 1.0 else out

    ka_ref[...] = rope_cols(0, ac, asa, asb, HEAD_DIM // 4, 1.0).astype(ka_ref.dtype)
    kr_ref[...] = rope_cols(1, rc, rsa, rsb, HEAD_DIM // 2, scale)


def _inproj(x, ctx, mod, w_in, tabs, sizes, n_ctx):
    bsz, d = x.shape[0], x.shape[2]
    s = x.shape[1] + (0 if ctx is None else ctx.shape[1])
    tm = _pick_chunk(s, INPROJ_BLOCK)
    stream_specs, stream_args = _stream_specs(x, ctx, tm)
    assert tm % TOKEN_BLOCK == 0, "the output projection reads the S5 arrays in TOKEN_BLOCK pieces"
    nrow = mod.shape[0]
    offs = np.concatenate([[0], np.cumsum(sizes)])
    col = lambda i: w_in[:, int(offs[i]):int(offs[i + 1])]
    c_idx, r_idx = (0, 1, 3, 4, 6, 7), (2, 5)
    c_dtypes = (BF16, BF16, F32, BF16, F32)
    r_dtypes = (BF16, F32)
    c_sizes = tuple(sizes[i] for i in c_idx)
    r_sizes = tuple(sizes[i] for i in r_idx)
    w_c_t = jnp.concatenate([col(i) for i in c_idx], axis=1).T
    w_rest = jnp.concatenate([col(i) for i in r_idx], axis=1)
    sub = tm // ATT_BLOCK
    tok = lambda width: pl.BlockSpec((None, tm, width), lambda b, j: (b, j, 0))
    chan = lambda width: pl.BlockSpec((sub, None, width, ATT_BLOCK), lambda b, j: (j, b, 0, 0))
    tab = pl.BlockSpec((tm, LANES), lambda b, j: (j, 0))
    tab_t = pl.BlockSpec((HEAD_DIM, tm), lambda b, j: (0, j))
    return pl.pallas_call(
        functools.partial(_inproj_kernel, n_stream=len(stream_specs), split=ctx is not None, n_ctx=n_ctx,
                          c_sizes=c_sizes, sizes=r_sizes),
        grid=(bsz, s // tm),
        in_specs=stream_specs + [
                  pl.BlockSpec((None, 6, d), lambda b, j: (b, 0, 0)),
                  pl.BlockSpec((None, 6, d), lambda b, j: (nrow - 1, 0, 0)),
                  pl.BlockSpec((sum(c_sizes), d), lambda b, j: (0, 0)),
                  pl.BlockSpec((d, sum(r_sizes)), lambda b, j: (0, 0)),
                  tab_t, tab_t, tab_t, tab_t, tab, tab, tab, tab, tab, tab],
        out_specs=[pl.BlockSpec((None, None, c_sizes[0], tm), lambda b, j: (j, b, 0, 0))]
                  + [chan(w) for w in c_sizes[1:]] + [tok(w) for w in r_sizes],
        out_shape=[jax.ShapeDtypeStruct((s // tm, bsz, c_sizes[0], tm), F32)]
                  + [jax.ShapeDtypeStruct((s // ATT_BLOCK, bsz, w, ATT_BLOCK), dt)
                     for w, dt in zip(c_sizes[1:], c_dtypes)]
                  + [jax.ShapeDtypeStruct((bsz, s, w), dt) for w, dt in zip(r_sizes, r_dtypes)],
        compiler_params=_cparams(("parallel", "parallel")),
        name="inproj",
    )(*stream_args, mod, mod, w_c_t, w_rest, *tabs)


def _rope_tables(t_len, n_ctx):
    t = jnp.arange(t_len)
    rows = (t // GRID_W).astype(F32)
    cols = (t % GRID_W).astype(F32)
    pos = t.astype(F32)

    def angles(p, dim):
        inv_freq = ROPE_BASE ** (-jnp.arange(0, dim, 2, dtype=F32) / dim)
        return p[:, None] * inv_freq[None, :]

    def head_tables(angs):
        cos = jnp.concatenate([jnp.concatenate([jnp.cos(a), jnp.cos(a)], -1) for a in angs], -1)
        sa = jnp.concatenate([jnp.concatenate([-jnp.sin(a), jnp.zeros_like(a)], -1) for a in angs], -1)
        sb = jnp.concatenate([jnp.concatenate([jnp.zeros_like(a), jnp.sin(a)], -1) for a in angs], -1)
        rep = LANES // HEAD_DIM
        return tuple(jnp.tile(x, (1, rep)) for x in (cos, sa, sb))

    att = head_tables([angles(rows, HEAD_DIM // 2), angles(cols, HEAD_DIM // 2)])
    ret = head_tables([angles(pos, HEAD_DIM)])
    signed_t = lambda tb: (tb[0][:, :HEAD_DIM].T, (tb[1] + tb[2])[:, :HEAD_DIM].T)

    def with_ctx(tb, axis, is_cos):
        shape = list(tb.shape)
        shape[axis] = n_ctx
        lead = jnp.ones(shape, F32) if is_cos else jnp.zeros(shape, F32)
        return jnp.concatenate([lead, tb], axis=axis)

    chan = signed_t(att) + signed_t(ret)
    toks = att + ret
    return (tuple(with_ctx(tb, 1, k % 2 == 0) for k, tb in enumerate(chan))
            + tuple(with_ctx(tb, 0, k % 3 == 0) for k, tb in enumerate(toks)))


def _s5_kernel(u_hbm, tab_ref, wsf_ref, wsb_ref, wrf_ref, wrb_ref, lam_ref, y_ref,
               lhs_ref, sem, m_ref, acc_ref, cf_ref, cb_ref, pf_ref, pb_ref, *, bsz, nz_chunks):
    r_blk, ch, tm = y_ref.shape
    c = S5_CHUNK
    kpb = tm // c
    n_chunks = (r_blk // bsz) * kpb
    g = pl.program_id(0)
    slot = g % 2

    def fetch(group, s):
        for i in range(ch):
            p, half = divmod(i, 2)
            for k in range(kpb):
                pltpu.make_async_copy(u_hbm.at[:, group * ch + i, pl.ds(k * c, c)],
                                      lhs_ref.at[s, p, pl.ds(k * r_blk, r_blk), pl.ds(half * c, c)],
                                      sem.at[s]).start()

    @pl.when(g == 0)
    def _():
        fetch(0, 0)

    @pl.when(g + 1 < pl.num_programs(0))
    def _():
        fetch(g + 1, 1 - slot)

    pltpu.make_async_copy(lhs_ref.at[1 - slot], lhs_ref.at[slot], sem.at[slot]).wait()
    acc_ref[...] = jnp.zeros_like(acc_ref)
    cf_ref[...] = jnp.zeros_like(cf_ref)
    cb_ref[...] = jnp.zeros_like(cb_ref)

    def pair(p, carry):
        for ih in range(2):
            for o in range(ch):
                lag_row = jnp.broadcast_to(tab_ref[2 * p + ih, o:o + 1, :], (16, 2 * c))
                tab = pltpu.roll(lag_row, 0, 1, stride=1, stride_axis=0).astype(BF16)
                for r2 in range(c // 16):
                    lo = c - 16 * r2
                    m_ref[ih * c + 16 * r2:ih * c + 16 * r2 + 16, o * c:(o + 1) * c] = tab[:, lo:lo + c]
        lhs = lhs_ref[slot, p].astype(BF16)
        acc_ref[...] += _dot(lhs, m_ref[...])
        cf_ref[...] += _dot(lhs, wsf_ref[p])
        cb_ref[...] += _dot(lhs, wsb_ref[p])
        return carry

    lax.fori_loop(0, ch // 2, pair, 0)
    lam = lam_ref[...]
    nst = lam.shape[1] // 2

    def advance(s, a, bc, drive):
        return s * a + pltpu.roll(s, nst, 1) * bc + drive

    def sweep(order, c_ref, p_ref, a, bc):
        s = jnp.zeros((bsz, lam.shape[1]), F32)
        for n in order:
            j, k = divmod(n, kpb)
            rows = slice(k * r_blk + j * bsz, k * r_blk + (j + 1) * bsz)
            p_ref[rows, :] = s
            s = advance(s, a, bc, c_ref[rows, :])

    sweep(list(range(n_chunks)), cf_ref, pf_ref, lam[0:1], lam[1:2])
    order_b = list(range(nz_chunks - 1, -1, -1)) + list(range(n_chunks - 1, nz_chunks - 1, -1))
    sweep(order_b, cb_ref, pb_ref, lam[2:3], lam[3:4])
    acc_ref[...] += (_dot(pf_ref[...].astype(BF16), wrf_ref[...])
                     + _dot(pb_ref[...].astype(BF16), wrb_ref[...]))
    for o in range(ch):
        for k in range(kpb):
            y_ref[:, o, k * c:(k + 1) * c] = acc_ref[k * r_blk:(k + 1) * r_blk, o * c:(o + 1) * c]


def _s5_weights(lam_re, lam_im, log_step, b_re, b_im, c_re, c_im, chunk):
    hp = lax.Precision.HIGHEST
    lam = lax.complex(lam_re.astype(F32), lam_im.astype(F32))
    lam_dt = lam * jnp.exp(log_step.astype(F32))[..., None]
    lam_bar = jnp.exp(lam_dt)
    b_bar = lax.complex(b_re.astype(F32), b_im.astype(F32)) * ((lam_bar - 1.0) / lam)[..., None]
    c_mat = lax.complex(c_re.astype(F32), c_im.astype(F32))
    g, n = lam.shape[1], lam.shape[2]
    ch = b_bar.shape[-1]
    steps = jnp.arange(chunk + 1, dtype=F32)
    pw = jnp.exp(steps[None, :, None, None] * lam_dt[:, None])
    kern = jnp.einsum('zgon,zdgn,zgni->zgdoi', c_mat, pw[:, :chunk], b_bar, precision=hp).real
    zero_lag = kern[0, :, 0] + kern[1, :, 0]
    lag_table = jnp.concatenate([jnp.zeros_like(zero_lag)[:, None], kern[1, :, :0:-1], zero_lag[:, None],
                                 kern[0, :, 1:]], axis=1).transpose(0, 3, 2, 1)

    def state_in(pw_s, b_dir):
        w = pw_s[:, :, :, None] * b_dir[None]
        w = jnp.concatenate([w.real, w.imag], axis=2)
        return w.transpose(1, 3, 0, 2).reshape(g, ch // 2, 2 * chunk, 2 * n)

    def state_out(pw_t, c_dir):
        w = c_dir[None] * pw_t[:, :, None, :]
        w = jnp.concatenate([w.real, -w.imag], axis=3)
        return w.transpose(1, 3, 2, 0).reshape(g, 2 * n, ch * chunk)

    wsf = state_in(pw[0, chunk - 1::-1][:chunk], b_bar[0])
    wsb = state_in(pw[1, :chunk], b_bar[1])
    wrf = state_out(pw[0, 1:chunk + 1], c_mat[0])
    wrb = state_out(pw[1, chunk:0:-1], c_mat[1])
    lam_c = pw[:, chunk]
    rows = []
    for z in range(2):
        rows.append(jnp.concatenate([lam_c[z].real, lam_c[z].real], -1))
        rows.append(jnp.concatenate([-lam_c[z].imag, lam_c[z].imag], -1))
    lam_rows = jnp.stack(rows + [jnp.zeros_like(rows[0])] * 4, axis=1)
    return lag_table, wsf.astype(BF16), wsb.astype(BF16), wrf.astype(BF16), wrb.astype(BF16), lam_rows


def _s5_scan(ut, weights, n_ctx):
    nblk, bsz, width, tm = ut.shape
    tab, wsf, wsb, wrf, wrb, lam_rows = weights
    g, ch = tab.shape[0], tab.shape[1]
    c = S5_CHUNK
    nst2 = wsf.shape[3]
    r_blk = nblk * bsz
    rows = r_blk * (tm // c)
    grp = lambda *shape: pl.BlockSpec((None,) + shape, lambda i: (i,) + (0,) * len(shape))
    tok = pl.BlockSpec((r_blk, ch, tm), lambda i: (0, i, 0))
    y = pl.pallas_call(
        functools.partial(_s5_kernel, bsz=bsz, nz_chunks=n_ctx // c),
        grid=(g,),
        in_specs=[pl.BlockSpec(memory_space=pl.ANY), grp(ch, ch, 2 * c), grp(ch // 2, 2 * c, nst2), grp(ch // 2, 2 * c, nst2),
                  grp(nst2, ch * c), grp(nst2, ch * c), grp(8, nst2)],
        out_specs=tok,
        out_shape=jax.ShapeDtypeStruct((r_blk, width, tm), F32),
        scratch_shapes=[pltpu.VMEM((2, ch // 2, rows, 2 * c), F32), pltpu.SemaphoreType.DMA((2,)),
                        pltpu.VMEM((2 * c, ch * c), BF16),
                        pltpu.VMEM((rows, ch * c), F32)] + [pltpu.VMEM((rows, nst2), F32)] * 4,
        compiler_params=_cparams(("arbitrary",)),
        name="s5_scan",
    )(ut.reshape(r_blk, width, tm), tab, wsf, wsb, wrf, wrb, lam_rows)
    return y.reshape(nblk, bsz, width, tm)


def _attn_kernel(sink_ref, q_ref, k_ref, v_ref, o_ref, s_ref, e_ref, *, n_ctx, t_len, q_off, kv_heads):
    qi = pl.program_id(1) + q_off
    nzb = n_ctx // ATT_BLOCK
    band = 3 * ATT_BLOCK
    wide = ATT_REP * ATT_BLOCK
    q = q_ref[...].astype(BF16)
    kc = k_ref[0:n_ctx, :].astype(BF16)
    vc = jnp.concatenate([v_ref[p] for p in range(nzb)], axis=1).astype(BF16)

    def rows(a, i):
        return a[i * HEAD_DIM:(i + 1) * HEAD_DIM]

    def q_group(g):
        qg = jnp.concatenate([rows(q, g * ATT_REP + r) for r in range(ATT_REP)], axis=1)
        zero = jnp.zeros_like(qg)
        return jnp.concatenate([qg if gg == g else zero for gg in range(kv_heads)], axis=0)

    ones_rows = 16

    def v_aug(v, g):
        return jnp.concatenate([rows(v, g), jnp.ones((ones_rows, v.shape[1]), BF16)], axis=0)

    def finish(g, o_aug, extra_den):
        o = o_aug[:HEAD_DIM] / (o_aug[HEAD_DIM:HEAD_DIM + 1] + extra_den)
        for r in range(ATT_REP):
            h = g * ATT_REP + r
            o_ref[h * HEAD_DIM:(h + 1) * HEAD_DIM, :] = o[:, r * ATT_BLOCK:(r + 1) * ATT_BLOCK].astype(o_ref.dtype)

    @pl.when(qi < nzb)
    def _():
        for g in range(kv_heads):
            sink = sink_ref[g:g + 1, :]
            s = _dot(kc, q_group(g))
            m = jnp.maximum(jnp.max(s, axis=0, keepdims=True), sink)
            e = jnp.exp2(s - m)
            finish(g, _dot(v_aug(vc, g), e.astype(BF16)), jnp.exp2(sink - m))

    @pl.when(qi >= nzb)
    def _():
        n = qi - nzb
        start = jnp.clip((n - 1) * ATT_BLOCK, 0, t_len - band)
        kl = k_ref[pl.ds(pl.multiple_of(n_ctx + start, ATT_BLOCK), band), :].astype(BF16)
        p0 = (n_ctx + start) // ATT_BLOCK
        vl = jnp.concatenate([v_ref[p0 + t] for t in range(3)], axis=1).astype(BF16)
        k_pos = start + lax.broadcasted_iota(jnp.int32, (band, wide), 0)
        q_pos = n * ATT_BLOCK + (lax.broadcasted_iota(jnp.int32, (band, wide), 1) & (ATT_BLOCK - 1))
        valid = jnp.abs(k_pos - q_pos) <= WINDOW
        v_all = jnp.concatenate([vl, vc], axis=1)
        n_keys = band + n_ctx
        row_max = []
        for g in range(kv_heads):
            qg = q_group(g)
            m = sink_ref[g:g + 1, :]
            for j in range(0, n_keys, ATT_BLOCK):
                if j < band:
                    s = jnp.where(valid[j:j + ATT_BLOCK], _dot(kl[j:j + ATT_BLOCK], qg), NEG_INF)
                else:
                    s = _dot(kc[j - band:j - band + ATT_BLOCK], qg)
                s_ref[g, j:j + ATT_BLOCK, :] = s
                m = jnp.maximum(m, jnp.max(s, axis=0, keepdims=True))
            row_max.append(m)
        for g in range(kv_heads):
            m = row_max[g]
            for j in range(0, n_keys, ATT_BLOCK):
                e_ref[g, j:j + ATT_BLOCK, :] = jnp.exp2(s_ref[g, j:j + ATT_BLOCK, :] - m).astype(BF16)
            finish(g, _dot(v_aug(v_all, g), e_ref[g]), jnp.exp2(sink_ref[g:g + 1, :] - m))


def _attention(qat, ka, vat, sink, n_ctx, need_ctx):
    nblk, bsz, qw, _ = qat.shape
    s, kvw = ka.shape[1], ka.shape[2]
    t_len = s - n_ctx
    kv_heads = kvw // HEAD_DIM
    q_off = 0 if need_ctx else n_ctx // ATT_BLOCK
    nq = nblk - q_off
    sink_rows = jnp.repeat(sink.astype(F32).reshape(kv_heads, ATT_REP), ATT_BLOCK, axis=1) * LOG2_E
    return pl.pallas_call(
        functools.partial(_attn_kernel, n_ctx=n_ctx, t_len=t_len, q_off=q_off, kv_heads=kv_heads),
        grid=(bsz, nq),
        in_specs=[pl.BlockSpec(sink_rows.shape, lambda b, j: (0, 0)),
                  pl.BlockSpec((None, None, qw, ATT_BLOCK), lambda b, j: (j + q_off, b, 0, 0)),
                  pl.BlockSpec((None, s, kvw), lambda b, j: (b, 0, 0)),
                  pl.BlockSpec((nblk, None, kvw, ATT_BLOCK), lambda b, j: (0, b, 0, 0))],
        out_specs=pl.BlockSpec((None, None, qw, ATT_BLOCK), lambda b, j: (j, b, 0, 0)),
        out_shape=jax.ShapeDtypeStruct((nq, bsz, qw, ATT_BLOCK), BF16),
        scratch_shapes=[pltpu.VMEM((kv_heads, 3 * ATT_BLOCK + n_ctx, ATT_REP * ATT_BLOCK), F32),
                        pltpu.VMEM((kv_heads, 3 * ATT_BLOCK + n_ctx, ATT_REP * ATT_BLOCK), BF16)],
        compiler_params=_cparams(("parallel", "arbitrary")),
        name="window_attention",
    )(sink_rows, qat, ka, vat)


def _ret_kernel(lg_ref, q_ref, k_ref, v_ref, g_ref, o_ref, acc_ref, sf_ref, sb_ref, *, nz_chunks, heads):
    c = RET_CHUNK
    n_chunks, w, _ = q_ref.shape
    f32 = lambda a: a.astype(F32)

    def per_head(shape, axis, group, direction):
        owner = lax.broadcasted_iota(jnp.int32, shape, axis) // group
        out = jnp.zeros(shape, F32)
        for h in range(heads):
            out = jnp.where(owner == h, lg_ref[direction, h], out)
        return out

    row_i = f32(lax.broadcasted_iota(jnp.int32, (w, c), 1))
    qw_f = jnp.exp(per_head((w, c), 0, HEAD_DIM, 0) * (row_i + 1.0))
    qw_b = jnp.exp(per_head((w, c), 0, HEAD_DIM, 1) * (c - row_i))
    key_j = f32(lax.broadcasted_iota(jnp.int32, (c, w), 0))
    kw_f = jnp.exp(per_head((c, w), 1, HEAD_DIM, 0) * (c - 1.0 - key_j))
    kw_b = jnp.exp(per_head((c, w), 1, HEAD_DIM, 1) * key_j)
    dec_f = jnp.exp(per_head((w, 1), 0, HEAD_DIM, 0) * c)
    dec_b = jnp.exp(per_head((w, 1), 0, HEAD_DIM, 1) * c)
    wide = heads * c
    diff = f32((lax.broadcasted_iota(jnp.int32, (c, wide), 1) & (c - 1)) - lax.broadcasted_iota(jnp.int32, (c, wide), 0))
    dec_t = jnp.where(diff >= 0, jnp.exp(per_head((c, wide), 1, c, 0) * jnp.maximum(diff, 0.0)),
                      jnp.exp(per_head((c, wide), 1, c, 1) * jnp.maximum(-diff, 0.0)))
    own_wide = (lax.broadcasted_iota(jnp.int32, (w, wide), 0) // HEAD_DIM
                == lax.broadcasted_iota(jnp.int32, (w, wide), 1) // c)
    own_sq = f32(lax.broadcasted_iota(jnp.int32, (w, w), 0) // HEAD_DIM
                 == lax.broadcasted_iota(jnp.int32, (w, w), 1) // HEAD_DIM)
    sf_ref[...] = jnp.zeros_like(sf_ref)
    sb_ref[...] = jnp.zeros_like(sb_ref)

    def block_diag(x):
        return jnp.where(own_wide, jnp.concatenate([x] * heads, axis=1), 0.0).astype(BF16)

    def fwd(n, carry):
        qt, vt = q_ref[n], v_ref[n]
        k = k_ref[pl.ds(pl.multiple_of(n * c, c), c), :]
        scores_t = _dot(k.astype(BF16), block_diag(qt)) * dec_t
        stacked = jnp.concatenate([scores_t[:, h * c:(h + 1) * c] for h in range(heads)], axis=0)
        o = _dot(block_diag(vt), stacked.astype(BF16))
        s_prev = sf_ref[...]
        o = o + _dot(s_prev.astype(BF16), (qt * qw_f).astype(BF16))
        sf_ref[...] = dec_f * s_prev + own_sq * _dot(vt.astype(BF16), (k * kw_f).astype(BF16))
        acc_ref[n] = o
        return carry

    lax.fori_loop(0, n_chunks, fwd, 0)

    def bwd(i, carry):
        n = jnp.where(i < nz_chunks, nz_chunks - 1 - i, n_chunks - 1 - i + nz_chunks)
        qt, vt = q_ref[n], v_ref[n]
        k = k_ref[pl.ds(pl.multiple_of(n * c, c), c), :]
        s_prev = sb_ref[...]
        o = acc_ref[n] + _dot(s_prev.astype(BF16), (qt * qw_b).astype(BF16))
        sb_ref[...] = dec_b * s_prev + own_sq * _dot(vt.astype(BF16), (k * kw_b).astype(BF16))
        gate = _silu(g_ref[n])
        for h in range(heads):
            rows = slice(h * HEAD_DIM, (h + 1) * HEAD_DIM)
            oh = o[rows]
            mu = jnp.mean(oh, axis=0, keepdims=True)
            oc = oh - mu
            var = jnp.mean(oc * oc, axis=0, keepdims=True)
            o_ref[n, rows, :] = (oc * lax.rsqrt(var + LN_EPS) * gate[rows]).astype(o_ref.dtype)
        return carry

    lax.fori_loop(0, n_chunks, bwd, 0)


def _retention(qrt, kr, vrt, grt, log_gamma, n_ctx):
    n_chunks, bsz, w, c = qrt.shape
    s = kr.shape[1]
    chan = pl.BlockSpec((n_chunks, None, w, c), lambda b: (0, b, 0, 0))
    return pl.pallas_call(
        functools.partial(_ret_kernel, nz_chunks=n_ctx // RET_CHUNK, heads=w // HEAD_DIM),
        grid=(bsz,),
        in_specs=[pl.BlockSpec(memory_space=pltpu.SMEM), chan, pl.BlockSpec((None, s, w), lambda b: (b, 0, 0)),
                  chan, chan],
        out_specs=chan,
        out_shape=jax.ShapeDtypeStruct((n_chunks, bsz, w, c), BF16),
        scratch_shapes=[pltpu.VMEM((n_chunks, w, c), F32), pltpu.VMEM((w, w), F32), pltpu.VMEM((w, w), F32)],
        compiler_params=_cparams(("parallel",)),
        name="retention",
    )(log_gamma.astype(F32), qrt, kr, vrt, grt)


def _outproj_kernel(*refs, n_stream, split, first_block, n_ctx, nzb, alpha, with_router, w_s5, w_att, n_exp):
    if with_router:
        (y_ref, u_ref, a_ref, r_ref, modx_ref, modz_ref, d_ref, wg_ref, bg_ref, wo_ref,
         g1_ref, b1_ref, rt_ref, x1_ref, fx_ref, gate_ref) = refs[n_stream:]
    else:
        (y_ref, u_ref, a_ref, r_ref, modx_ref, modz_ref, d_ref, wg_ref, bg_ref, wo_ref,
         g1_ref, b1_ref, x1_ref, fx_ref) = refs[n_stream:]
    (x_in,) = _stream_pieces(refs[:n_stream], pl.program_id(1) + first_block, n_ctx, split)
    is_ctx = pl.program_id(1) < nzb
    mod = jnp.where(is_ctx, modz_ref[...], modx_ref[...])
    g = _gelu_tanh(y_ref[...] + d_ref[...] * u_ref[...])
    s5 = g * _sigmoid(_dot(wg_ref[...], g.astype(BF16)) + bg_ref[...])
    def chan_major(ref, lo, hi):
        return jnp.concatenate([_dot_tn(ref[k].astype(BF16), wo_ref[lo:hi, :]) for k in range(ref.shape[0])],
                               axis=0)

    mix = (_dot_tn(s5.astype(BF16), wo_ref[0:w_s5, :]) + chan_major(a_ref, w_s5, w_s5 + w_att)
           + chan_major(r_ref, w_s5 + w_att, wo_ref.shape[0]))
    x1 = _ln(alpha * x_in + mod[2:3] * mix) * g1_ref[...] + b1_ref[...]
    x1_ref[...] = x1
    fx = _ln(x1) * (1.0 + mod[4:5]) + mod[3:4]
    fx_ref[...] = fx.astype(fx_ref.dtype)
    if with_router:
        lane = lax.broadcasted_iota(jnp.int32, (fx.shape[0], LANES), 1)
        logits = jnp.where(lane < n_exp, _dot3(fx, rt_ref[...]), -jnp.inf)
        m1 = jnp.max(logits, axis=1, keepdims=True)
        i1 = jnp.min(jnp.where(logits == m1, lane, LANES), axis=1, keepdims=True)
        rest = jnp.where(lane == i1, -jnp.inf, logits)
        m2 = jnp.max(rest, axis=1, keepdims=True)
        i2 = jnp.min(jnp.where(rest == m2, lane, LANES), axis=1, keepdims=True)
        e2 = jnp.exp(m2 - m1)
        den = 1.0 + e2
        route = jnp.where(lane == 0, 1.0 / den, jnp.where(lane == 1, e2 / den, 0.0))
        route = jnp.where(lane == 2, i1.astype(F32), jnp.where(lane == 3, i2.astype(F32), route))
        gate_ref[...] = route


def _outproj(y_s5, u, o_att, o_ret, x, ctx, mod, s5_d, w_glu, b_glu, w_out, ln_g, ln_b, router, n_ctx,
             need_ctx, alpha):
    bsz, d = x.shape[0], x.shape[2]
    s = x.shape[1] + (0 if ctx is None else ctx.shape[1])
    tm = TOKEN_BLOCK
    nzb = n_ctx // tm
    off = 0 if need_ctx else nzb
    nblk = s // tm - off
    s_out = nblk * tm
    nrow = mod.shape[0]
    w_s5, w_att = y_s5.shape[2], o_att.shape[2]
    sub = tm // ATT_BLOCK
    att_off = off if o_att.shape[0] == s // ATT_BLOCK else 0
    att = pl.BlockSpec((sub, None, w_att, ATT_BLOCK), lambda b, j: (j + att_off, b, 0, 0))

    def tok(width, shift):
        return pl.BlockSpec((None, tm, width), lambda b, j: (b, j + shift, 0))

    def full(a):
        return pl.BlockSpec(a.shape, lambda b, j: (0,) * a.ndim)

    per_in = y_s5.shape[3] // tm
    chan = pl.BlockSpec((None, None, w_s5, tm), lambda b, j: ((j + off) // per_in, b, 0, (j + off) % per_in))
    vec = lambda a: a.reshape(1, -1).astype(F32)
    col = lambda a: a.reshape(-1, 1).astype(F32)
    consts = [col(s5_d), w_glu.T.astype(BF16), col(b_glu), w_out.astype(BF16), vec(ln_g), vec(ln_b)]
    ret = pl.BlockSpec((sub, None, o_ret.shape[2], ATT_BLOCK), lambda b, j: (j + off, b, 0, 0))
    stream_specs, stream_args = _stream_specs(x, ctx, tm, first_block=off)
    in_specs = stream_specs + [chan, chan, att, ret,
                pl.BlockSpec((None, 6, d), lambda b, j: (b, 0, 0)),
                pl.BlockSpec((None, 6, d), lambda b, j: (nrow - 1, 0, 0))] + [full(a) for a in consts]
    out_specs = [tok(d, 0), tok(d, 0)]
    fx_dtype = BF16 if router is None else F32
    out_shape = [jax.ShapeDtypeStruct((bsz, s_out, d), F32), jax.ShapeDtypeStruct((bsz, s_out, d), fx_dtype)]
    args = stream_args + [y_s5, u, o_att, o_ret, mod, mod] + consts
    with_router = router is not None
    n_exp = 0
    if with_router:
        n_exp = router.shape[1]
        router_pad = jnp.pad(router.astype(F32), ((0, 0), (0, LANES - n_exp)))
        args.append(router_pad)
        in_specs.append(full(router_pad))
        out_specs.append(tok(LANES, 0))
        out_shape.append(jax.ShapeDtypeStruct((bsz, s_out, LANES), F32))
    return pl.pallas_call(
        functools.partial(_outproj_kernel, n_stream=len(stream_specs), split=ctx is not None, first_block=off,
                          n_ctx=n_ctx, nzb=nzb - off, alpha=alpha, with_router=with_router,
                          w_s5=w_s5, w_att=w_att, n_exp=n_exp),
        grid=(bsz, nblk),
        in_specs=in_specs,
        out_specs=out_specs,
        out_shape=out_shape,
        compiler_params=_cparams(("parallel", "parallel")),
        name="outproj",
    )(*args)


def _swiglu_into(x_bf16, w1_ref, w3_ref, w2_ref, acc_ref, fc):
    for s in range(0, w1_ref.shape[1], fc):
        h1 = _dot(x_bf16, w1_ref[:, s:s + fc])
        h3 = _dot(x_bf16, w3_ref[:, s:s + fc])
        acc_ref[...] += _dot((_silu(h1) * h3).astype(BF16), w2_ref[s:s + fc, :])


def _ffn_kernel(fx_ref, x1_ref, modx_ref, modz_ref, w1_ref, w3_ref, w2_ref, g2_ref, b2_ref,
                o_ref, acc_ref, *, n_ctx_tokens, alpha, fc):
    tm = acc_ref.shape[0]
    acc_ref[...] = jnp.zeros_like(acc_ref)
    _swiglu_into(fx_ref[...], w1_ref, w3_ref, w2_ref, acc_ref, fc)
    row = pl.program_id(1) * tm + lax.broadcasted_iota(jnp.int32, (tm, 1), 0)
    gate = jnp.where(row < n_ctx_tokens, modz_ref[5:6, :], modx_ref[5:6, :])
    o_ref[...] = _ln(alpha * x1_ref[...] + gate * acc_ref[...]) * g2_ref[...] + b2_ref[...]


def _ffn(fx, x1, mod, w1, w3, w2, ln_g, ln_b, n_ctx_tokens, alpha):
    bsz, s, d = x1.shape
    ff = w1.shape[1]
    tm = _pick_chunk(s, FFN_BLOCK)
    nrow = mod.shape[0]
    tok = pl.BlockSpec((None, tm, d), lambda b, j: (b, j, 0))
    full = lambda a: pl.BlockSpec(a.shape, lambda b, j: (0,) * a.ndim, pipeline_mode=pl.Buffered(1))
    vec = lambda a: a.reshape(1, -1).astype(F32)
    return pl.pallas_call(
        functools.partial(_ffn_kernel, n_ctx_tokens=n_ctx_tokens, alpha=alpha, fc=_pick_chunk(ff, FFN_CHUNK)),
        grid=(bsz, s // tm),
        in_specs=[tok, tok,
                  pl.BlockSpec((None, 6, d), lambda b, j: (b, 0, 0)),
                  pl.BlockSpec((None, 6, d), lambda b, j: (nrow - 1, 0, 0)),
                  full(w1), full(w3), full(w2),
                  pl.BlockSpec((1, d), lambda b, j: (0, 0)), pl.BlockSpec((1, d), lambda b, j: (0, 0))],
        out_specs=tok,
        out_shape=jax.ShapeDtypeStruct((bsz, s, d), F32),
        scratch_shapes=[pltpu.VMEM((tm, d), F32)],
        compiler_params=_cparams(("parallel", "parallel")),
        name="dense_ffn",
    )(fx, x1, mod, mod, w1, w3, w2, vec(ln_g), vec(ln_b))


MOE_TILE = 1024


def _route_plan(e1, e2, n_exp, tile):
    n = e1.shape[0]
    pair_e = jnp.stack([e1, e2], axis=1).reshape(-1)
    onehot = (pair_e[:, None] == jnp.arange(n_exp, dtype=jnp.int32)[None, :]).astype(jnp.int32)
    before = jnp.cumsum(onehot, axis=0) - onehot
    rank = jnp.sum(before * onehot, axis=1)
    counts = jnp.sum(onehot, axis=0)
    padded = (counts + tile - 1) // tile * tile
    ends = jnp.cumsum(padded)
    starts = ends - padded
    dest = starts[pair_e] + rank
    n_rows = (2 * n + n_exp * (tile - 1)) // tile * tile
    n_tiles = n_rows // tile
    tile_start = jnp.arange(n_tiles, dtype=jnp.int32) * tile
    tile_expert = jnp.minimum(jnp.sum((tile_start[:, None] >= ends[None, :]).astype(jnp.int32), axis=1),
                              n_exp - 1)
    by_expert = jnp.sort(pair_e * (2 * n) + jnp.arange(2 * n, dtype=jnp.int32)) % (2 * n)
    row = jnp.arange(n_rows, dtype=jnp.int32)
    row_e = jnp.repeat(tile_expert, tile)
    in_group = row - starts[row_e]
    src = jnp.minimum(in_group + (jnp.cumsum(counts) - counts)[row_e], 2 * n - 1)
    row_token = jnp.where(in_group < counts[row_e], by_expert[src] // 2, 0)
    n_used = (ends[-1] // tile).astype(jnp.int32).reshape(1)
    return row_token, dest.reshape(n, 2), tile_expert, n_used


def _moe_gemm_kernel(te_ref, nused_ref, tok_ref, tokn_ref, x_hbm, w1_ref, w3_ref, w2_ref, y_ref,
                     xbuf_ref, sem, *, fc, tile):
    i = pl.program_id(0)
    f = pl.program_id(1)
    slot = i % 2
    n_used = nused_ref[0]

    def row_copy(t_ref, r, s):
        return pltpu.make_async_copy(x_hbm.at[pl.ds(t_ref[0, r], 1)], xbuf_ref.at[s, pl.ds(r, 1)], sem.at[s])

    def issue(t_ref, s):
        def body(it, carry):
            base = pl.multiple_of(it * 8, 8)
            for k in range(8):
                row_copy(t_ref, base + k, s).start()
            return carry
        lax.fori_loop(0, tile // 8, body, 0)

    @pl.when(f == 0)
    def _():
        y_ref[...] = jnp.zeros_like(y_ref)

        @pl.when(i == 0)
        def _():
            issue(tok_ref, 0)

        @pl.when(i + 1 < n_used)
        def _():
            issue(tokn_ref, 1 - slot)

        @pl.when(i < n_used)
        def _():
            pltpu.make_async_copy(x_hbm.at[pl.ds(0, tile)], xbuf_ref.at[slot], sem.at[slot]).wait()

    @pl.when(i < n_used)
    def _():
        _swiglu_into(xbuf_ref[slot].astype(BF16), w1_ref, w3_ref, w2_ref, y_ref, fc)


def _moe_gemm(x_flat, row_token, tile_expert, n_used, w1, w3, w2, tile):
    n_rows = row_token.shape[0]
    n_tiles = n_rows // tile
    d = x_flat.shape[1]
    ff = w1.shape[2]
    nf = 2 if ff % (2 * LANES) == 0 else 1
    tf = ff // nf
    last = lambda i, nu: jnp.minimum(i, nu[0] - 1)
    fsel = lambda i, f, nu: jnp.where(i < nu[0], f, nf - 1)
    tok = lambda imap: pl.BlockSpec((None, 1, tile), imap, memory_space=pltpu.SMEM)
    tokens = row_token.reshape(n_tiles, 1, tile)
    return pl.pallas_call(
        functools.partial(_moe_gemm_kernel, fc=_pick_chunk(tf, MOE_CHUNK), tile=tile),
        grid_spec=pltpu.PrefetchScalarGridSpec(
            num_scalar_prefetch=2,
            grid=(n_tiles, nf),
            in_specs=[tok(lambda i, f, te, nu: (i, 0, 0)),
                      tok(lambda i, f, te, nu: (jnp.minimum(i + 1, n_tiles - 1), 0, 0)),
                      pl.BlockSpec(memory_space=pl.ANY),
                      pl.BlockSpec((None, d, tf), lambda i, f, te, nu: (te[last(i, nu)], 0, fsel(i, f, nu))),
                      pl.BlockSpec((None, d, tf), lambda i, f, te, nu: (te[last(i, nu)], 0, fsel(i, f, nu))),
                      pl.BlockSpec((None, tf, d), lambda i, f, te, nu: (te[last(i, nu)], fsel(i, f, nu), 0))],
            out_specs=pl.BlockSpec((tile, d), lambda i, f, te, nu: (i, 0)),
            scratch_shapes=[pltpu.VMEM((2, tile, d), F32), pltpu.SemaphoreType.DMA((2,))]),
        out_shape=jax.ShapeDtypeStruct((n_rows, d), F32),
        compiler_params=_cparams(("arbitrary", "arbitrary")),
        name="moe_gemm",
    )(tile_expert, n_used, tokens, tokens, x_flat, w1, w3, w2)


def _moe_combine_kernel(pos_ref, posn_ref, route_ref, x1_ref, modx_ref, modz_ref, g2_ref, b2_ref, y_hbm,
                        o_ref, buf_ref, sem, *, nzb, nblk, alpha, tm):
    i = pl.program_id(0)
    slot = i % 2

    def row_copy(p_ref, r, k, s):
        return pltpu.make_async_copy(y_hbm.at[pl.ds(p_ref[0, k * tm + r], 1)],
                                     buf_ref.at[s, pl.ds(k * tm + r, 1)], sem.at[s])

    def issue(p_ref, s):
        def body(r, carry):
            row_copy(p_ref, r, 0, s).start()
            row_copy(p_ref, r, 1, s).start()
            return carry
        lax.fori_loop(0, tm, body, 0, unroll=8)

    @pl.when(i == 0)
    def _():
        issue(pos_ref, 0)

    @pl.when(i + 1 < pl.num_programs(0))
    def _():
        issue(posn_ref, 1 - slot)

    pltpu.make_async_copy(y_hbm.at[pl.ds(0, 2 * tm)], buf_ref.at[slot], sem.at[slot]).wait()
    route = route_ref[...]
    f = route[:, 0:1] * buf_ref[slot, 0:tm] + route[:, 1:2] * buf_ref[slot, tm:2 * tm]
    is_ctx = (i % nblk) < nzb
    mod = jnp.where(is_ctx, modz_ref[...], modx_ref[...])
    o_ref[...] = _ln(alpha * x1_ref[...] + mod[5:6] * f) * g2_ref[...] + b2_ref[...]


def _moe_combine(ys, pos, route, x1, mod, ln_g, ln_b, n_ctx_tokens, alpha):
    bsz, s, d = x1.shape
    tm = TOKEN_BLOCK
    nblk = s // tm
    n_steps = bsz * nblk
    nrow = mod.shape[0]
    pos_steps = pos.reshape(n_steps, tm, 2).transpose(0, 2, 1).reshape(n_steps, 1, 2 * tm)
    tok = lambda width: pl.BlockSpec((tm, width), lambda i: (i, 0))
    smem = lambda imap: pl.BlockSpec((None, 1, 2 * tm), imap, memory_space=pltpu.SMEM)
    vec = lambda a: a.reshape(1, -1).astype(F32)
    out = pl.pallas_call(
        functools.partial(_moe_combine_kernel, nzb=n_ctx_tokens // tm, nblk=nblk, alpha=alpha, tm=tm),
        grid=(n_steps,),
        in_specs=[smem(lambda i: (i, 0, 0)),
                  smem(lambda i: (jnp.minimum(i + 1, n_steps - 1), 0, 0)),
                  tok(LANES), tok(d),
                  pl.BlockSpec((None, 6, d), lambda i: (i // nblk, 0, 0)),
                  pl.BlockSpec((None, 6, d), lambda i: (nrow - 1, 0, 0)),
                  pl.BlockSpec((1, d), lambda i: (0, 0)), pl.BlockSpec((1, d), lambda i: (0, 0)),
                  pl.BlockSpec(memory_space=pl.ANY)],
        out_specs=tok(d),
        out_shape=jax.ShapeDtypeStruct((bsz * s, d), F32),
        scratch_shapes=[pltpu.VMEM((2, 2 * tm, d), F32), pltpu.SemaphoreType.DMA((2,))],
        compiler_params=_cparams(("arbitrary",)),
        name="moe_combine",
    )(pos_steps, pos_steps, route.reshape(bsz * s, LANES), x1.reshape(bsz * s, d), mod, mod,
      vec(ln_g), vec(ln_b), ys)
    return out.reshape(bsz, s, d)


def _moe(fx, x1, route, mod, w1, w3, w2, ln_g, ln_b, n_ctx_tokens, alpha):
    bsz, s, d = x1.shape
    n_exp = w1.shape[0]
    idx = route.reshape(bsz * s, LANES)[:, 2:4].astype(jnp.int32)
    row_token, pos, tile_expert, n_used = _route_plan(idx[:, 0], idx[:, 1], n_exp, MOE_TILE)
    ys = _moe_gemm(fx.reshape(bsz * s, d), row_token, tile_expert, n_used, w1, w3, w2, MOE_TILE)
    return _moe_combine(ys, pos, route, x1, mod, ln_g, ln_b, n_ctx_tokens, alpha)


def kernel(x, c, ctx, c_ctx, w_mod, b_mod, w_in, s5_lam_re, s5_lam_im, s5_log_step, s5_b_re, s5_b_im,
           s5_c_re, s5_c_im, s5_d, s5_w_glu, s5_b_glu, attn_sink, ret_log_gamma, w_out,
           ln1_g, ln1_b, ln2_g, ln2_b, ffn_w1, ffn_w3, ffn_w2, moe_router, moe_w1, moe_w3, moe_w2):
    bsz, t_len, d = x.shape
    n_ctx = ctx.shape[1]
    depth = w_in.shape[0]
    alpha = (2 * depth) ** 0.25
    s5_w = s5_d.shape[1]
    att_w = attn_sink.shape[1] * HEAD_DIM
    kv_w = att_w // ATT_REP
    ret_w = ret_log_gamma.shape[2] * HEAD_DIM
    sizes = (s5_w, att_w, kv_w, kv_w, ret_w, ret_w, ret_w, ret_w)
    assert sum(sizes) == w_in.shape[2] and s5_w + att_w + ret_w == w_out.shape[1]
    assert n_ctx % TOKEN_BLOCK == 0 and t_len % TOKEN_BLOCK == 0 and t_len >= 3 * ATT_BLOCK

    pad = (-(bsz + 1)) % 8
    cvec = jnp.concatenate([jnp.zeros((pad, d), F32), c_ctx[None].astype(F32)], axis=0)
    cvec = jnp.concatenate([c.astype(F32), cvec], axis=0)
    mod_all = _modulation(cvec, w_mod.astype(F32), b_mod.astype(F32)).reshape(depth, bsz + pad + 1, 6, d)

    tabs = _rope_tables(t_len, n_ctx)
    xz, ctx_in = x.astype(F32), ctx.astype(F32)
    for l in range(depth):
        need_ctx = l < depth - 1
        mod = mod_all[l]
        u, qa, va, qr, vr, gr, ka, kr = _inproj(xz, ctx_in, mod, w_in[l].astype(BF16), tabs, sizes, n_ctx)
        s5w = _s5_weights(s5_lam_re[l], s5_lam_im[l], s5_log_step[l], s5_b_re[l], s5_b_im[l],
                          s5_c_re[l], s5_c_im[l], S5_CHUNK)
        y_s5 = _s5_scan(u, s5w, n_ctx)
        o_att = _attention(qa, ka, va, attn_sink[l], n_ctx, need_ctx)
        o_ret = _retention(qr, kr, vr, gr, ret_log_gamma[l], n_ctx)
        i = l // 2
        router = None if l % 2 == 0 else moe_router[i]
        outs = _outproj(y_s5, u, o_att, o_ret, xz, ctx_in, mod, s5_d[l], s5_w_glu[l], s5_b_glu[l], w_out[l],
                        ln1_g[l], ln1_b[l], router, n_ctx, need_ctx, alpha)
        ctx_in = None
        ctx_tokens = n_ctx if need_ctx else 0
        if l % 2 == 0:
            x1, fx = outs
            xz = _ffn(fx, x1, mod, ffn_w1[i].astype(BF16), ffn_w3[i].astype(BF16), ffn_w2[i].astype(BF16),
                      ln2_g[l], ln2_b[l], ctx_tokens, alpha)
        else:
            x1, fx, route = outs
            xz = _moe(fx, x1, route, mod, moe_w1[i].astype(BF16), moe_w3[i].astype(BF16),
                      moe_w2[i].astype(BF16), ln2_g[l], ln2_b[l], ctx_tokens, alpha)
    return xz if xz.shape[1] == t_len else xz[:, n_ctx:]
```

```python
import functools
import math

import jax
import jax.numpy as jnp
import numpy as np
from jax import lax
from jax.experimental import pallas as pl
from jax.experimental.pallas import tpu as pltpu

F32 = jnp.float32
BF16 = jnp.bfloat16

GRID_W = 64
HEAD_DIM = 64
ATT_REP = 4
WINDOW = 128
ATT_BLOCK = 128
RET_CHUNK = 128
LN_EPS = 1e-5
ROPE_BASE = 10000.0
NEG_INF = -1e30
LOG2_E = math.log2(math.e)

LANES = 128
S5_CHUNK = LANES
TOKEN_BLOCK = 256
INPROJ_BLOCK = 768
FFN_BLOCK = 768
FFN_CHUNK = 512
MOE_CHUNK = 256
COMBINE_BLOCK = 512
MOD_BLOCK = 1536
VMEM_LIMIT = 56 * 1024 * 1024


def _cparams(sem):
    return pltpu.CompilerParams(dimension_semantics=sem, vmem_limit_bytes=VMEM_LIMIT)


def _pick_chunk(total, target):
    best = LANES
    for c in range(LANES, target + 1, LANES):
        if total % c == 0:
            best = c
    return best


def _dot(a, b):
    return jnp.dot(a, b, preferred_element_type=F32)


def _dot_nt(a, b):
    return lax.dot_general(a, b, (((1,), (1,)), ((), ())), preferred_element_type=F32)


def _dot_tn(a, b):
    return lax.dot_general(a, b, (((0,), (0,)), ((), ())), preferred_element_type=F32)


def _split_bf16(a):
    hi = a.astype(BF16)
    lo = (a - hi.astype(F32)).astype(BF16)
    return hi, lo


def _dot3(a, b):
    ah, al = _split_bf16(a)
    bh, bl = _split_bf16(b)
    return _dot(ah, bh) + (_dot(ah, bl) + _dot(al, bh))


def _ln(x):
    mu = jnp.mean(x, axis=-1, keepdims=True)
    xc = x - mu
    var = jnp.mean(xc * xc, axis=-1, keepdims=True)
    return xc * lax.rsqrt(var + LN_EPS)


def _silu(x):
    return x * (1.0 / (1.0 + jnp.exp(-x)))


def _sigmoid(x):
    return 1.0 / (1.0 + jnp.exp(-x))


def _gelu_tanh(x):
    c = math.sqrt(2.0 / math.pi)
    return 0.5 * x * (1.0 + jnp.tanh(c * (x + 0.044715 * (x * x * x))))


def _mod_kernel(c_ref, w_ref, b_ref, o_ref):
    o_ref[...] = _dot3(_silu(c_ref[...]), w_ref[...]) + b_ref[...]


def _modulation(cvec, w_mod, b_mod):
    depth, d, n = w_mod.shape
    rows = cvec.shape[0]
    tn = _pick_chunk(n, MOD_BLOCK)
    return pl.pallas_call(
        _mod_kernel,
        grid=(depth, n // tn),
        in_specs=[pl.BlockSpec((rows, d), lambda l, j: (0, 0)),
                  pl.BlockSpec((None, d, tn), lambda l, j: (l, 0, j)),
                  pl.BlockSpec((None, 1, tn), lambda l, j: (l, 0, j))],
        out_specs=pl.BlockSpec((None, rows, tn), lambda l, j: (l, 0, j)),
        out_shape=jax.ShapeDtypeStruct((depth, rows, n), F32),
        compiler_params=_cparams(("parallel", "parallel")),
        name="modulation",
    )(cvec, w_mod, b_mod.reshape(depth, 1, n))


def _rope_slab(xs, cos, sa, sb, half):
    return xs * cos + pltpu.roll(xs, LANES - half, 1) * sa + pltpu.roll(xs, half, 1) * sb


def _stream_specs(x, ctx, tm, first_block=0):
    d = x.shape[2]
    pieces = tm // TOKEN_BLOCK
    piece = lambda imap: pl.BlockSpec((None, TOKEN_BLOCK, d), imap)
    specs, args = [], []
    for k in range(pieces):
        if ctx is None:
            specs.append(piece(lambda b, j, k=k: (b, (j + first_block) * pieces + k, 0)))
            args.append(x)
        else:
            nzb = ctx.shape[1] // TOKEN_BLOCK
            specs.append(piece(lambda b, j, k=k: (b, jnp.minimum((j + first_block) * pieces + k, nzb - 1), 0)))
            specs.append(piece(lambda b, j, k=k: (b, jnp.maximum((j + first_block) * pieces + k - nzb, 0), 0)))
            args += [ctx, x]
    return specs, args


def _stream_pieces(refs, block, n_ctx, split):
    if not split:
        return [r[...] for r in refs]
    pieces = len(refs) // 2
    return [jnp.where((block * pieces + k) * TOKEN_BLOCK < n_ctx, refs[2 * k][...], refs[2 * k + 1][...])
            for k in range(pieces)]


def _inproj_kernel(*refs, n_stream, split, n_ctx, c_sizes, sizes):
    (modx_ref, modz_ref, wc_ref, w_ref, act_ref, ast_ref, rct_ref, rst_ref,
     ac_ref, asa_ref, asb_ref, rc_ref, rsa_ref, rsb_ref,
     ut_ref, qat_ref, vat_ref, qrt_ref, vrt_ref, grt_ref, ka_ref, kr_ref) = refs[n_stream:]
    x_pieces = _stream_pieces(refs[:n_stream], pl.program_id(1), n_ctx, split)
    tm = len(x_pieces) * TOKEN_BLOCK
    is_ctx = pl.program_id(1) * tm + lax.broadcasted_iota(jnp.int32, (tm, 1), 0) < n_ctx
    shift = jnp.where(is_ctx, modz_ref[0:1, :], modx_ref[0:1, :])
    gain = 1.0 + jnp.where(is_ctx, modz_ref[1:2, :], modx_ref[1:2, :])
    h = jnp.concatenate(
        [(_ln(xp) * gain[k * TOKEN_BLOCK:(k + 1) * TOKEN_BLOCK] + shift[k * TOKEN_BLOCK:(k + 1) * TOKEN_BLOCK]
          ).astype(BF16) for k, xp in enumerate(x_pieces)], axis=0)
    scale = HEAD_DIM ** -0.5

    ct = _dot_nt(wc_ref[...], h)
    c_offs = [int(o) for o in np.concatenate([[0], np.cumsum(c_sizes)])]

    def chan(i):
        return ct[c_offs[i]:c_offs[i + 1]]

    def rope_t(x, cos_ref, sin_ref, part, mul):
        cos_t, sin_t = cos_ref[...], sin_ref[...]
        heads = []
        for hd in range(x.shape[0] // HEAD_DIM):
            xh = x[hd * HEAD_DIM:(hd + 1) * HEAD_DIM]
            pieces = []
            for lo in range(0, HEAD_DIM, 2 * part):
                pieces += [xh[lo + part:lo + 2 * part], xh[lo:lo + part]]
            out = xh * cos_t + jnp.concatenate(pieces, axis=0) * sin_t
            heads.append(out * mul if mul != 1.0 else out)
        return jnp.concatenate(heads, axis=0)

    def put(ref, val):
        for k in range(ref.shape[0]):
            ref[k] = val[:, k * ATT_BLOCK:(k + 1) * ATT_BLOCK].astype(ref.dtype)

    ut_ref[...] = chan(0)
    put(qat_ref, rope_t(chan(1), act_ref, ast_ref, HEAD_DIM // 4, scale * LOG2_E))
    put(vat_ref, chan(2))
    put(qrt_ref, rope_t(chan(3), rct_ref, rst_ref, HEAD_DIM // 2, 1.0))
    put(vrt_ref, chan(4))
    put(grt_ref, chan(5))

    p = _dot(h, w_ref[...])
    ac, asa, asb = ac_ref[...], asa_ref[...], asb_ref[...]
    rc, rsa, rsb = rc_ref[...], rsa_ref[...], rsb_ref[...]
    offs = np.concatenate([[0], np.cumsum(sizes)])

    def rope_cols(i, cos, sa, sb, half, mul):
        blk = p[:, int(offs[i]):int(offs[i + 1])]
        slabs = [_rope_slab(blk[:, s:s + LANES], cos, sa, sb, half) for s in range(0, blk.shape[1], LANES)]
        out = slabs[0] if len(slabs) == 1 else jnp.concatenate(slabs, axis=1)
        return out * mul if mul != 1.0 else out

    ka_ref[...] = rope_cols(0, ac, asa, asb, HEAD_DIM // 4, 1.0).astype(ka_ref.dtype)
    kr_ref[...] = rope_cols(1, rc, rsa, rsb, HEAD_DIM // 2, scale)


def _inproj(x, ctx, mod, w_in, tabs, sizes, n_ctx):
    bsz, d = x.shape[0], x.shape[2]
    s = x.shape[1] + (0 if ctx is None else ctx.shape[1])
    tm = _pick_chunk(s, INPROJ_BLOCK)
    stream_specs, stream_args = _stream_specs(x, ctx, tm)
    assert tm % TOKEN_BLOCK == 0, "the output projection reads the S5 arrays in TOKEN_BLOCK pieces"
    nrow = mod.shape[0]
    offs = np.concatenate([[0], np.cumsum(sizes)])
    col = lambda i: w_in[:, int(offs[i]):int(offs[i + 1])]
    c_idx, r_idx = (0, 1, 3, 4, 6, 7), (2, 5)
    c_dtypes = (BF16, BF16, F32, BF16, F32)
    r_dtypes = (BF16, F32)
    c_sizes = tuple(sizes[i] for i in c_idx)
    r_sizes = tuple(sizes[i] for i in r_idx)
    w_c_t = jnp.concatenate([col(i) for i in c_idx], axis=1).T
    w_rest = jnp.concatenate([col(i) for i in r_idx], axis=1)
    sub = tm // ATT_BLOCK
    tok = lambda width: pl.BlockSpec((None, tm, width), lambda b, j: (b, j, 0))
    chan = lambda width: pl.BlockSpec((sub, None, width, ATT_BLOCK), lambda b, j: (j, b, 0, 0))
    tab = pl.BlockSpec((tm, LANES), lambda b, j: (j, 0))
    tab_t = pl.BlockSpec((HEAD_DIM, tm), lambda b, j: (0, j))
    return pl.pallas_call(
        functools.partial(_inproj_kernel, n_stream=len(stream_specs), split=ctx is not None, n_ctx=n_ctx,
                          c_sizes=c_sizes, sizes=r_sizes),
        grid=(bsz, s // tm),
        in_specs=stream_specs + [
                  pl.BlockSpec((None, 6, d), lambda b, j: (b, 0, 0)),
                  pl.BlockSpec((None, 6, d), lambda b, j: (nrow - 1, 0, 0)),
                  pl.BlockSpec((sum(c_sizes), d), lambda b, j: (0, 0)),
                  pl.BlockSpec((d, sum(r_sizes)), lambda b, j: (0, 0)),
                  tab_t, tab_t, tab_t, tab_t, tab, tab, tab, tab, tab, tab],
        out_specs=[pl.BlockSpec((None, None, c_sizes[0], tm), lambda b, j: (j, b, 0, 0))]
                  + [chan(w) for w in c_sizes[1:]] + [tok(w) for w in r_sizes],
        out_shape=[jax.ShapeDtypeStruct((s // tm, bsz, c_sizes[0], tm), F32)]
                  + [jax.ShapeDtypeStruct((s // ATT_BLOCK, bsz, w, ATT_BLOCK), dt)
                     for w, dt in zip(c_sizes[1:], c_dtypes)]
                  + [jax.ShapeDtypeStruct((bsz, s, w), dt) for w, dt in zip(r_sizes, r_dtypes)],
        compiler_params=_cparams(("parallel", "parallel")),
        name="inproj",
    )(*stream_args, mod, mod, w_c_t, w_rest, *tabs)


def _rope_tables(t_len, n_ctx):
    t = jnp.arange(t_len)
    rows = (t // GRID_W).astype(F32)
    cols = (t % GRID_W).astype(F32)
    pos = t.astype(F32)

    def angles(p, dim):
        inv_freq = ROPE_BASE ** (-jnp.arange(0, dim, 2, dtype=F32) / dim)
        return p[:, None] * inv_freq[None, :]

    def head_tables(angs):
        cos = jnp.concatenate([jnp.concatenate([jnp.cos(a), jnp.cos(a)], -1) for a in angs], -1)
        sa = jnp.concatenate([jnp.concatenate([-jnp.sin(a), jnp.zeros_like(a)], -1) for a in angs], -1)
        sb = jnp.concatenate([jnp.concatenate([jnp.zeros_like(a), jnp.sin(a)], -1) for a in angs], -1)
        rep = LANES // HEAD_DIM
        return tuple(jnp.tile(x, (1, rep)) for x in (cos, sa, sb))

    att = head_tables([angles(rows, HEAD_DIM // 2), angles(cols, HEAD_DIM // 2)])
    ret = head_tables([angles(pos, HEAD_DIM)])
    signed_t = lambda tb: (tb[0][:, :HEAD_DIM].T, (tb[1] + tb[2])[:, :HEAD_DIM].T)

    def with_ctx(tb, axis, is_cos):
        shape = list(tb.shape)
        shape[axis] = n_ctx
        lead = jnp.ones(shape, F32) if is_cos else jnp.zeros(shape, F32)
        return jnp.concatenate([lead, tb], axis=axis)

    chan = signed_t(att) + signed_t(ret)
    toks = att + ret
    return (tuple(with_ctx(tb, 1, k % 2 == 0) for k, tb in enumerate(chan))
            + tuple(with_ctx(tb, 0, k % 3 == 0) for k, tb in enumerate(toks)))


def _s5_kernel(u_hbm, tab_ref, wsf_ref, wsb_ref, wrf_ref, wrb_ref, lam_ref, y_ref,
               lhs_ref, sem, m_ref, acc_ref, cf_ref, cb_ref, pf_ref, pb_ref, *, bsz, nz_chunks):
    r_blk, ch, tm = y_ref.shape
    c = S5_CHUNK
    kpb = tm // c
    n_chunks = (r_blk // bsz) * kpb
    g = pl.program_id(0)
    slot = g % 2

    def fetch(group, s):
        for i in range(ch):
            p, half = divmod(i, 2)
            for k in range(kpb):
                pltpu.make_async_copy(u_hbm.at[:, group * ch + i, pl.ds(k * c, c)],
                                      lhs_ref.at[s, p, pl.ds(k * r_blk, r_blk), pl.ds(half * c, c)],
                                      sem.at[s]).start()

    @pl.when(g == 0)
    def _():
        fetch(0, 0)

    @pl.when(g + 1 < pl.num_programs(0))
    def _():
        fetch(g + 1, 1 - slot)

    pltpu.make_async_copy(lhs_ref.at[1 - slot], lhs_ref.at[slot], sem.at[slot]).wait()
    acc_ref[...] = jnp.zeros_like(acc_ref)
    cf_ref[...] = jnp.zeros_like(cf_ref)
    cb_ref[...] = jnp.zeros_like(cb_ref)

    def pair(p, carry):
        for ih in range(2):
            for o in range(ch):
                lag_row = jnp.broadcast_to(tab_ref[2 * p + ih, o:o + 1, :], (16, 2 * c))
                tab = pltpu.roll(lag_row, 0, 1, stride=1, stride_axis=0).astype(BF16)
                for r2 in range(c // 16):
                    lo = c - 16 * r2
                    m_ref[ih * c + 16 * r2:ih * c + 16 * r2 + 16, o * c:(o + 1) * c] = tab[:, lo:lo + c]
        lhs = lhs_ref[slot, p].astype(BF16)
        acc_ref[...] += _dot(lhs, m_ref[...])
        cf_ref[...] += _dot(lhs, wsf_ref[p])
        cb_ref[...] += _dot(lhs, wsb_ref[p])
        return carry

    lax.fori_loop(0, ch // 2, pair, 0)
    lam = lam_ref[...]
    nst = lam.shape[1] // 2

    def advance(s, a, bc, drive):
        return s * a + pltpu.roll(s, nst, 1) * bc + drive

    def sweep(order, c_ref, p_ref, a, bc):
        s = jnp.zeros((bsz, lam.shape[1]), F32)
        for n in order:
            j, k = divmod(n, kpb)
            rows = slice(k * r_blk + j * bsz, k * r_blk + (j + 1) * bsz)
            p_ref[rows, :] = s
            s = advance(s, a, bc, c_ref[rows, :])

    sweep(list(range(n_chunks)), cf_ref, pf_ref, lam[0:1], lam[1:2])
    order_b = list(range(nz_chunks - 1, -1, -1)) + list(range(n_chunks - 1, nz_chunks - 1, -1))
    sweep(order_b, cb_ref, pb_ref, lam[2:3], lam[3:4])
    acc_ref[...] += (_dot(pf_ref[...].astype(BF16), wrf_ref[...])
                     + _dot(pb_ref[...].astype(BF16), wrb_ref[...]))
    for o in range(ch):
        for k in range(kpb):
            y_ref[:, o, k * c:(k + 1) * c] = acc_ref[k * r_blk:(k + 1) * r_blk, o * c:(o + 1) * c]


def _s5_weights(lam_re, lam_im, log_step, b_re, b_im, c_re, c_im, chunk):
    hp = lax.Precision.HIGHEST
    lam = lax.complex(lam_re.astype(F32), lam_im.astype(F32))
    lam_dt = lam * jnp.exp(log_step.astype(F32))[..., None]
    lam_bar = jnp.exp(lam_dt)
    b_bar = lax.complex(b_re.astype(F32), b_im.astype(F32)) * ((lam_bar - 1.0) / lam)[..., None]
    c_mat = lax.complex(c_re.astype(F32), c_im.astype(F32))
    g, n = lam.shape[1], lam.shape[2]
    ch = b_bar.shape[-1]
    steps = jnp.arange(chunk + 1, dtype=F32)
    pw = jnp.exp(steps[None, :, None, None] * lam_dt[:, None])
    kern = jnp.einsum('zgon,zdgn,zgni->zgdoi', c_mat, pw[:, :chunk], b_bar, precision=hp).real
    zero_lag = kern[0, :, 0] + kern[1, :, 0]
    lag_table = jnp.concatenate([jnp.zeros_like(zero_lag)[:, None], kern[1, :, :0:-1], zero_lag[:, None],
                                 kern[0, :, 1:]], axis=1).transpose(0, 3, 2, 1)

    def state_in(pw_s, b_dir):
        w = pw_s[:, :, :, None] * b_dir[None]
        w = jnp.concatenate([w.real, w.imag], axis=2)
        return w.transpose(1, 3, 0, 2).reshape(g, ch // 2, 2 * chunk, 2 * n)

    def state_out(pw_t, c_dir):
        w = c_dir[None] * pw_t[:, :, None, :]
        w = jnp.concatenate([w.real, -w.imag], axis=3)
        return w.transpose(1, 3, 2, 0).reshape(g, 2 * n, ch * chunk)

    wsf = state_in(pw[0, chunk - 1::-1][:chunk], b_bar[0])
    wsb = state_in(pw[1, :chunk], b_bar[1])
    wrf = state_out(pw[0, 1:chunk + 1], c_mat[0])
    wrb = state_out(pw[1, chunk:0:-1], c_mat[1])
    lam_c = pw[:, chunk]
    rows = []
    for z in range(2):
        rows.append(jnp.concatenate([lam_c[z].real, lam_c[z].real], -1))
        rows.append(jnp.concatenate([-lam_c[z].imag, lam_c[z].imag], -1))
    lam_rows = jnp.stack(rows + [jnp.zeros_like(rows[0])] * 4, axis=1)
    return lag_table, wsf.astype(BF16), wsb.astype(BF16), wrf.astype(BF16), wrb.astype(BF16), lam_rows


def _s5_scan(ut, weights, n_ctx):
    nblk, bsz, width, tm = ut.shape
    tab, wsf, wsb, wrf, wrb, lam_rows = weights
    g, ch = tab.shape[0], tab.shape[1]
    c = S5_CHUNK
    nst2 = wsf.shape[3]
    r_blk = nblk * bsz
    rows = r_blk * (tm // c)
    grp = lambda *shape: pl.BlockSpec((None,) + shape, lambda i: (i,) + (0,) * len(shape))
    tok = pl.BlockSpec((r_blk, ch, tm), lambda i: (0, i, 0))
    y = pl.pallas_call(
        functools.partial(_s5_kernel, bsz=bsz, nz_chunks=n_ctx // c),
        grid=(g,),
        in_specs=[pl.BlockSpec(memory_space=pl.ANY), grp(ch, ch, 2 * c), grp(ch // 2, 2 * c, nst2), grp(ch // 2, 2 * c, nst2),
                  grp(nst2, ch * c), grp(nst2, ch * c), grp(8, nst2)],
        out_specs=tok,
        out_shape=jax.ShapeDtypeStruct((r_blk, width, tm), F32),
        scratch_shapes=[pltpu.VMEM((2, ch // 2, rows, 2 * c), F32), pltpu.SemaphoreType.DMA((2,)),
                        pltpu.VMEM((2 * c, ch * c), BF16),
                        pltpu.VMEM((rows, ch * c), F32)] + [pltpu.VMEM((rows, nst2), F32)] * 4,
        compiler_params=_cparams(("arbitrary",)),
        name="s5_scan",
    )(ut.reshape(r_blk, width, tm), tab, wsf, wsb, wrf, wrb, lam_rows)
    return y.reshape(nblk, bsz, width, tm)


def _attn_kernel(sink_ref, q_ref, k_ref, v_ref, o_ref, s_ref, e_ref, *, n_ctx, t_len, q_off, kv_heads):
    qi = pl.program_id(1) + q_off
    nzb = n_ctx // ATT_BLOCK
    band = 3 * ATT_BLOCK
    wide = ATT_REP * ATT_BLOCK
    q = q_ref[...].astype(BF16)
    kc = k_ref[0:n_ctx, :].astype(BF16)
    vc = jnp.concatenate([v_ref[p] for p in range(nzb)], axis=1).astype(BF16)

    def rows(a, i):
        return a[i * HEAD_DIM:(i + 1) * HEAD_DIM]

    def q_group(g):
        qg = jnp.concatenate([rows(q, g * ATT_REP + r) for r in range(ATT_REP)], axis=1)
        zero = jnp.zeros_like(qg)
        return jnp.concatenate([qg if gg == g else zero for gg in range(kv_heads)], axis=0)

    ones_rows = 16

    def v_aug(v, g):
        return jnp.concatenate([rows(v, g), jnp.ones((ones_rows, v.shape[1]), BF16)], axis=0)

    def finish(g, o_aug, extra_den):
        o = o_aug[:HEAD_DIM] / (o_aug[HEAD_DIM:HEAD_DIM + 1] + extra_den)
        for r in range(ATT_REP):
            h = g * ATT_REP + r
            o_ref[h * HEAD_DIM:(h + 1) * HEAD_DIM, :] = o[:, r * ATT_BLOCK:(r + 1) * ATT_BLOCK].astype(o_ref.dtype)

    @pl.when(qi < nzb)
    def _():
        for g in range(kv_heads):
            sink = sink_ref[g:g + 1, :]
            s = _dot(kc, q_group(g))
            m = jnp.maximum(jnp.max(s, axis=0, keepdims=True), sink)
            e = jnp.exp2(s - m)
            finish(g, _dot(v_aug(vc, g), e.astype(BF16)), jnp.exp2(sink - m))

    @pl.when(qi >= nzb)
    def _():
        n = qi - nzb
        start = jnp.clip((n - 1) * ATT_BLOCK, 0, t_len - band)
        kl = k_ref[pl.ds(pl.multiple_of(n_ctx + start, ATT_BLOCK), band), :].astype(BF16)
        p0 = (n_ctx + start) // ATT_BLOCK
        vl = jnp.concatenate([v_ref[p0 + t] for t in range(3)], axis=1).astype(BF16)
        k_pos = start + lax.broadcasted_iota(jnp.int32, (band, wide), 0)
        q_pos = n * ATT_BLOCK + (lax.broadcasted_iota(jnp.int32, (band, wide), 1) & (ATT_BLOCK - 1))
        valid = jnp.abs(k_pos - q_pos) <= WINDOW
        v_all = jnp.concatenate([vl, vc], axis=1)
        n_keys = band + n_ctx
        row_max = []
        for g in range(kv_heads):
            qg = q_group(g)
            m = sink_ref[g:g + 1, :]
            for j in range(0, n_keys, ATT_BLOCK):
                if j < band:
                    s = jnp.where(valid[j:j + ATT_BLOCK], _dot(kl[j:j + ATT_BLOCK], qg), NEG_INF)
                else:
                    s = _dot(kc[j - band:j - band + ATT_BLOCK], qg)
                s_ref[g, j:j + ATT_BLOCK, :] = s
                m = jnp.maximum(m, jnp.max(s, axis=0, keepdims=True))
            row_max.append(m)
        for g in range(kv_heads):
            m = row_max[g]
            for j in range(0, n_keys, ATT_BLOCK):
                e_ref[g, j:j + ATT_BLOCK, :] = jnp.exp2(s_ref[g, j:j + ATT_BLOCK, :] - m).astype(BF16)
            finish(g, _dot(v_aug(v_all, g), e_ref[g]), jnp.exp2(sink_ref[g:g + 1, :] - m))


def _attention(qat, ka, vat, sink, n_ctx, need_ctx):
    nblk, bsz, qw, _ = qat.shape
    s, kvw = ka.shape[1], ka.shape[2]
    t_len = s - n_ctx
    kv_heads = kvw // HEAD_DIM
    q_off = 0 if need_ctx else n_ctx // ATT_BLOCK
    nq = nblk - q_off
    sink_rows = jnp.repeat(sink.astype(F32).reshape(kv_heads, ATT_REP), ATT_BLOCK, axis=1) * LOG2_E
    return pl.pallas_call(
        functools.partial(_attn_kernel, n_ctx=n_ctx, t_len=t_len, q_off=q_off, kv_heads=kv_heads),
        grid=(bsz, nq),
        in_specs=[pl.BlockSpec(sink_rows.shape, lambda b, j: (0, 0)),
                  pl.BlockSpec((None, None, qw, ATT_BLOCK), lambda b, j: (j + q_off, b, 0, 0)),
                  pl.BlockSpec((None, s, kvw), lambda b, j: (b, 0, 0)),
                  pl.BlockSpec((nblk, None, kvw, ATT_BLOCK), lambda b, j: (0, b, 0, 0))],
        out_specs=pl.BlockSpec((None, None, qw, ATT_BLOCK), lambda b, j: (j, b, 0, 0)),
        out_shape=jax.ShapeDtypeStruct((nq, bsz, qw, ATT_BLOCK), BF16),
        scratch_shapes=[pltpu.VMEM((kv_heads, 3 * ATT_BLOCK + n_ctx, ATT_REP * ATT_BLOCK), F32),
                        pltpu.VMEM((kv_heads, 3 * ATT_BLOCK + n_ctx, ATT_REP * ATT_BLOCK), BF16)],
        compiler_params=_cparams(("parallel", "arbitrary")),
        name="window_attention",
    )(sink_rows, qat, ka, vat)


def _ret_kernel(lg_ref, q_ref, k_ref, v_ref, g_ref, o_ref, acc_ref, sf_ref, sb_ref, *, nz_chunks, heads):
    c = RET_CHUNK
    n_chunks, w, _ = q_ref.shape
    f32 = lambda a: a.astype(F32)

    def per_head(shape, axis, group, direction):
        owner = lax.broadcasted_iota(jnp.int32, shape, axis) // group
        out = jnp.zeros(shape, F32)
        for h in range(heads):
            out = jnp.where(owner == h, lg_ref[direction, h], out)
        return out

    row_i = f32(lax.broadcasted_iota(jnp.int32, (w, c), 1))
    qw_f = jnp.exp(per_head((w, c), 0, HEAD_DIM, 0) * (row_i + 1.0))
    qw_b = jnp.exp(per_head((w, c), 0, HEAD_DIM, 1) * (c - row_i))
    key_j = f32(lax.broadcasted_iota(jnp.int32, (c, w), 0))
    kw_f = jnp.exp(per_head((c, w), 1, HEAD_DIM, 0) * (c - 1.0 - key_j))
    kw_b = jnp.exp(per_head((c, w), 1, HEAD_DIM, 1) * key_j)
    dec_f = jnp.exp(per_head((w, 1), 0, HEAD_DIM, 0) * c)
    dec_b = jnp.exp(per_head((w, 1), 0, HEAD_DIM, 1) * c)
    wide = heads * c
    diff = f32((lax.broadcasted_iota(jnp.int32, (c, wide), 1) & (c - 1)) - lax.broadcasted_iota(jnp.int32, (c, wide), 0))
    dec_t = jnp.where(diff >= 0, jnp.exp(per_head((c, wide), 1, c, 0) * jnp.maximum(diff, 0.0)),
                      jnp.exp(per_head((c, wide), 1, c, 1) * jnp.maximum(-diff, 0.0)))
    own_wide = (lax.broadcasted_iota(jnp.int32, (w, wide), 0) // HEAD_DIM
                == lax.broadcasted_iota(jnp.int32, (w, wide), 1) // c)
    own_sq = f32(lax.broadcasted_iota(jnp.int32, (w, w), 0) // HEAD_DIM
                 == lax.broadcasted_iota(jnp.int32, (w, w), 1) // HEAD_DIM)
    sf_ref[...] = jnp.zeros_like(sf_ref)
    sb_ref[...] = jnp.zeros_like(sb_ref)

    def block_diag(x):
        return jnp.where(own_wide, jnp.concatenate([x] * heads, axis=1), 0.0).astype(BF16)

    def fwd(n, carry):
        qt, vt = q_ref[n], v_ref[n]
        k = k_ref[pl.ds(pl.multiple_of(n * c, c), c), :]
        scores_t = _dot(k.astype(BF16), block_diag(qt)) * dec_t
        stacked = jnp.concatenate([scores_t[:, h * c:(h + 1) * c] for h in range(heads)], axis=0)
        o = _dot(block_diag(vt), stacked.astype(BF16))
        s_prev = sf_ref[...]
        o = o + _dot(s_prev.astype(BF16), (qt * qw_f).astype(BF16))
        sf_ref[...] = dec_f * s_prev + own_sq * _dot(vt.astype(BF16), (k * kw_f).astype(BF16))
        acc_ref[n] = o
        return carry

    lax.fori_loop(0, n_chunks, fwd, 0)

    def bwd(i, carry):
        n = jnp.where(i < nz_chunks, nz_chunks - 1 - i, n_chunks - 1 - i + nz_chunks)
        qt, vt = q_ref[n], v_ref[n]
        k = k_ref[pl.ds(pl.multiple_of(n * c, c), c), :]
        s_prev = sb_ref[...]
        o = acc_ref[n] + _dot(s_prev.astype(BF16), (qt * qw_b).astype(BF16))
        sb_ref[...] = dec_b * s_prev + own_sq * _dot(vt.astype(BF16), (k * kw_b).astype(BF16))
        gate = _silu(g_ref[n])
        for h in range(heads):
            rows = slice(h * HEAD_DIM, (h + 1) * HEAD_DIM)
            oh = o[rows]
            mu = jnp.mean(oh, axis=0, keepdims=True)
            oc = oh - mu
            var = jnp.mean(oc * oc, axis=0, keepdims=True)
            o_ref[n, rows, :] = (oc * lax.rsqrt(var + LN_EPS) * gate[rows]).astype(o_ref.dtype)
        return carry

    lax.fori_loop(0, n_chunks, bwd, 0)


def _retention(qrt, kr, vrt, grt, log_gamma, n_ctx):
    n_chunks, bsz, w, c = qrt.shape
    s = kr.shape[1]
    chan = pl.BlockSpec((n_chunks, None, w, c), lambda b: (0, b, 0, 0))
    return pl.pallas_call(
        functools.partial(_ret_kernel, nz_chunks=n_ctx // RET_CHUNK, heads=w // HEAD_DIM),
        grid=(bsz,),
        in_specs=[pl.BlockSpec(memory_space=pltpu.SMEM), chan, pl.BlockSpec((None, s, w), lambda b: (b, 0, 0)),
                  chan, chan],
        out_specs=chan,
        out_shape=jax.ShapeDtypeStruct((n_chunks, bsz, w, c), BF16),
        scratch_shapes=[pltpu.VMEM((n_chunks, w, c), F32), pltpu.VMEM((w, w), F32), pltpu.VMEM((w, w), F32)],
        compiler_params=_cparams(("parallel",)),
        name="retention",
    )(log_gamma.astype(F32), qrt, kr, vrt, grt)


def _outproj_kernel(*refs, n_stream, split, first_block, n_ctx, nzb, alpha, with_router, w_s5, w_att, n_exp):
    if with_router:
        (y_ref, u_ref, a_ref, r_ref, modx_ref, modz_ref, d_ref, wg_ref, bg_ref, wo_ref,
         g1_ref, b1_ref, rt_ref, x1_ref, fx_ref, gate_ref) = refs[n_stream:]
    else:
        (y_ref, u_ref, a_ref, r_ref, modx_ref, modz_ref, d_ref, wg_ref, bg_ref, wo_ref,
         g1_ref, b1_ref, x1_ref, fx_ref) = refs[n_stream:]
    (x_in,) = _stream_pieces(refs[:n_stream], pl.program_id(1) + first_block, n_ctx, split)
    is_ctx = pl.program_id(1) < nzb
    mod = jnp.where(is_ctx, modz_ref[...], modx_ref[...])
    g = _gelu_tanh(y_ref[...] + d_ref[...] * u_ref[...])
    s5 = g * _sigmoid(_dot(wg_ref[...], g.astype(BF16)) + bg_ref[...])
    def chan_major(ref, lo, hi):
        return jnp.concatenate([_dot_tn(ref[k].astype(BF16), wo_ref[lo:hi, :]) for k in range(ref.shape[0])],
                               axis=0)

    mix = (_dot_tn(s5.astype(BF16), wo_ref[0:w_s5, :]) + chan_major(a_ref, w_s5, w_s5 + w_att)
           + chan_major(r_ref, w_s5 + w_att, wo_ref.shape[0]))
    x1 = _ln(alpha * x_in + mod[2:3] * mix) * g1_ref[...] + b1_ref[...]
    x1_ref[...] = x1
    fx = _ln(x1) * (1.0 + mod[4:5]) + mod[3:4]
    fx_ref[...] = fx.astype(fx_ref.dtype)
    if with_router:
        lane = lax.broadcasted_iota(jnp.int32, (fx.shape[0], LANES), 1)
        logits = jnp.where(lane < n_exp, _dot3(fx, rt_ref[...]), -jnp.inf)
        m1 = jnp.max(logits, axis=1, keepdims=True)
        i1 = jnp.min(jnp.where(logits == m1, lane, LANES), axis=1, keepdims=True)
        rest = jnp.where(lane == i1, -jnp.inf, logits)
        m2 = jnp.max(rest, axis=1, keepdims=True)
        i2 = jnp.min(jnp.where(rest == m2, lane, LANES), axis=1, keepdims=True)
        e2 = jnp.exp(m2 - m1)
        den = 1.0 + e2
        route = jnp.where(lane == 0, 1.0 / den, jnp.where(lane == 1, e2 / den, 0.0))
        route = jnp.where(lane == 2, i1.astype(F32), jnp.where(lane == 3, i2.astype(F32), route))
        gate_ref[...] = route


def _outproj(y_s5, u, o_att, o_ret, x, ctx, mod, s5_d, w_glu, b_glu, w_out, ln_g, ln_b, router, n_ctx,
             need_ctx, alpha):
    bsz, d = x.shape[0], x.shape[2]
    s = x.shape[1] + (0 if ctx is None else ctx.shape[1])
    tm = TOKEN_BLOCK
    nzb = n_ctx // tm
    off = 0 if need_ctx else nzb
    nblk = s // tm - off
    s_out = nblk * tm
    nrow = mod.shape[0]
    w_s5, w_att = y_s5.shape[2], o_att.shape[2]
    sub = tm // ATT_BLOCK
    att_off = off if o_att.shape[0] == s // ATT_BLOCK else 0
    att = pl.BlockSpec((sub, None, w_att, ATT_BLOCK), lambda b, j: (j + att_off, b, 0, 0))

    def tok(width, shift):
        return pl.BlockSpec((None, tm, width), lambda b, j: (b, j + shift, 0))

    def full(a):
        return pl.BlockSpec(a.shape, lambda b, j: (0,) * a.ndim)

    per_in = y_s5.shape[3] // tm
    chan = pl.BlockSpec((None, None, w_s5, tm), lambda b, j: ((j + off) // per_in, b, 0, (j + off) % per_in))
    vec = lambda a: a.reshape(1, -1).astype(F32)
    col = lambda a: a.reshape(-1, 1).astype(F32)
    consts = [col(s5_d), w_glu.T.astype(BF16), col(b_glu), w_out.astype(BF16), vec(ln_g), vec(ln_b)]
    ret = pl.BlockSpec((sub, None, o_ret.shape[2], ATT_BLOCK), lambda b, j: (j + off, b, 0, 0))
    stream_specs, stream_args = _stream_specs(x, ctx, tm, first_block=off)
    in_specs = stream_specs + [chan, chan, att, ret,
                pl.BlockSpec((None, 6, d), lambda b, j: (b, 0, 0)),
                pl.BlockSpec((None, 6, d), lambda b, j: (nrow - 1, 0, 0))] + [full(a) for a in consts]
    out_specs = [tok(d, 0), tok(d, 0)]
    fx_dtype = BF16 if router is None else F32
    out_shape = [jax.ShapeDtypeStruct((bsz, s_out, d), F32), jax.ShapeDtypeStruct((bsz, s_out, d), fx_dtype)]
    args = stream_args + [y_s5, u, o_att, o_ret, mod, mod] + consts
    with_router = router is not None
    n_exp = 0
    if with_router:
        n_exp = router.shape[1]
        router_pad = jnp.pad(router.astype(F32), ((0, 0), (0, LANES - n_exp)))
        args.append(router_pad)
        in_specs.append(full(router_pad))
        out_specs.append(tok(LANES, 0))
        out_shape.append(jax.ShapeDtypeStruct((bsz, s_out, LANES), F32))
    return pl.pallas_call(
        functools.partial(_outproj_kernel, n_stream=len(stream_specs), split=ctx is not None, first_block=off,
                          n_ctx=n_ctx, nzb=nzb - off, alpha=alpha, with_router=with_router,
                          w_s5=w_s5, w_att=w_att, n_exp=n_exp),
        grid=(bsz, nblk),
        in_specs=in_specs,
        out_specs=out_specs,
        out_shape=out_shape,
        compiler_params=_cparams(("parallel", "parallel")),
        name="outproj",
    )(*args)


def _swiglu_into(x_bf16, w1_ref, w3_ref, w2_ref, acc_ref, fc):
    for s in range(0, w1_ref.shape[1], fc):
        h1 = _dot(x_bf16, w1_ref[:, s:s + fc])
        h3 = _dot(x_bf16, w3_ref[:, s:s + fc])
        acc_ref[...] += _dot((_silu(h1) * h3).astype(BF16), w2_ref[s:s + fc, :])


def _ffn_kernel(fx_ref, x1_ref, modx_ref, modz_ref, w1_ref, w3_ref, w2_ref, g2_ref, b2_ref,
                o_ref, acc_ref, *, n_ctx_tokens, alpha, fc):
    tm = acc_ref.shape[0]
    acc_ref[...] = jnp.zeros_like(acc_ref)
    _swiglu_into(fx_ref[...], w1_ref, w3_ref, w2_ref, acc_ref, fc)
    row = pl.program_id(1) * tm + lax.broadcasted_iota(jnp.int32, (tm, 1), 0)
    gate = jnp.where(row < n_ctx_tokens, modz_ref[5:6, :], modx_ref[5:6, :])
    o_ref[...] = _ln(alpha * x1_ref[...] + gate * acc_ref[...]) * g2_ref[...] + b2_ref[...]


def _ffn(fx, x1, mod, w1, w3, w2, ln_g, ln_b, n_ctx_tokens, alpha):
    bsz, s, d = x1.shape
    ff = w1.shape[1]
    tm = _pick_chunk(s, FFN_BLOCK)
    nrow = mod.shape[0]
    tok = pl.BlockSpec((None, tm, d), lambda b, j: (b, j, 0))
    full = lambda a: pl.BlockSpec(a.shape, lambda b, j: (0,) * a.ndim, pipeline_mode=pl.Buffered(1))
    vec = lambda a: a.reshape(1, -1).astype(F32)
    return pl.pallas_call(
        functools.partial(_ffn_kernel, n_ctx_tokens=n_ctx_tokens, alpha=alpha, fc=_pick_chunk(ff, FFN_CHUNK)),
        grid=(bsz, s // tm),
        in_specs=[tok, tok,
                  pl.BlockSpec((None, 6, d), lambda b, j: (b, 0, 0)),
                  pl.BlockSpec((None, 6, d), lambda b, j: (nrow - 1, 0, 0)),
                  full(w1), full(w3), full(w2),
                  pl.BlockSpec((1, d), lambda b, j: (0, 0)), pl.BlockSpec((1, d), lambda b, j: (0, 0))],
        out_specs=tok,
        out_shape=jax.ShapeDtypeStruct((bsz, s, d), F32),
        scratch_shapes=[pltpu.VMEM((tm, d), F32)],
        compiler_params=_cparams(("parallel", "parallel")),
        name="dense_ffn",
    )(fx, x1, mod, mod, w1, w3, w2, vec(ln_g), vec(ln_b))


MOE_TILE = 1024


def _route_plan(e1, e2, n_exp, tile):
    n = e1.shape[0]
    pair_e = jnp.stack([e1, e2], axis=1).reshape(-1)
    onehot = (pair_e[:, None] == jnp.arange(n_exp, dtype=jnp.int32)[None, :]).astype(jnp.int32)
    before = jnp.cumsum(onehot, axis=0) - onehot
    rank = jnp.sum(before * onehot, axis=1)
    counts = jnp.sum(onehot, axis=0)
    padded = (counts + tile - 1) // tile * tile
    ends = jnp.cumsum(padded)
    starts = ends - padded
    dest = starts[pair_e] + rank
    n_rows = (2 * n + n_exp * (tile - 1)) // tile * tile
    n_tiles = n_rows // tile
    tile_start = jnp.arange(n_tiles, dtype=jnp.int32) * tile
    tile_expert = jnp.minimum(jnp.sum((tile_start[:, None] >= ends[None, :]).astype(jnp.int32), axis=1),
                              n_exp - 1)
    by_expert = jnp.sort(pair_e * (2 * n) + jnp.arange(2 * n, dtype=jnp.int32)) % (2 * n)
    row = jnp.arange(n_rows, dtype=jnp.int32)
    row_e = jnp.repeat(tile_expert, tile)
    in_group = row - starts[row_e]
    src = jnp.minimum(in_group + (jnp.cumsum(counts) - counts)[row_e], 2 * n - 1)
    row_token = jnp.where(in_group < counts[row_e], by_expert[src] // 2, 0)
    n_used = (ends[-1] // tile).astype(jnp.int32).reshape(1)
    return row_token, dest.reshape(n, 2), tile_expert, n_used


def _moe_gemm_kernel(te_ref, nused_ref, tok_ref, tokn_ref, x_hbm, w1_ref, w3_ref, w2_ref, y_ref,
                     xbuf_ref, sem, *, fc, tile):
    i = pl.program_id(0)
    f = pl.program_id(1)
    slot = i % 2
    n_used = nused_ref[0]

    def row_copy(t_ref, r, s):
        return pltpu.make_async_copy(x_hbm.at[pl.ds(t_ref[0, r], 1)], xbuf_ref.at[s, pl.ds(r, 1)], sem.at[s])

    def issue(t_ref, s):
        def body(it, carry):
            base = pl.multiple_of(it * 8, 8)
            for k in range(8):
                row_copy(t_ref, base + k, s).start()
            return carry
        lax.fori_loop(0, tile // 8, body, 0)

    @pl.when(f == 0)
    def _():
        y_ref[...] = jnp.zeros_like(y_ref)

        @pl.when(i == 0)
        def _():
            issue(tok_ref, 0)

        @pl.when(i + 1 < n_used)
        def _():
            issue(tokn_ref, 1 - slot)

        @pl.when(i < n_used)
        def _():
            pltpu.make_async_copy(x_hbm.at[pl.ds(0, tile)], xbuf_ref.at[slot], sem.at[slot]).wait()

    @pl.when(i < n_used)
    def _():
        _swiglu_into(xbuf_ref[slot].astype(BF16), w1_ref, w3_ref, w2_ref, y_ref, fc)


def _moe_gemm(x_flat, row_token, tile_expert, n_used, w1, w3, w2, tile):
    n_rows = row_token.shape[0]
    n_tiles = n_rows // tile
    d = x_flat.shape[1]
    ff = w1.shape[2]
    nf = 2 if ff % (2 * LANES) == 0 else 1
    tf = ff // nf
    last = lambda i, nu: jnp.minimum(i, nu[0] - 1)
    fsel = lambda i, f, nu: jnp.where(i < nu[0], f, nf - 1)
    tok = lambda imap: pl.BlockSpec((None, 1, tile), imap, memory_space=pltpu.SMEM)
    tokens = row_token.reshape(n_tiles, 1, tile)
    return pl.pallas_call(
        functools.partial(_moe_gemm_kernel, fc=_pick_chunk(tf, MOE_CHUNK), tile=tile),
        grid_spec=pltpu.PrefetchScalarGridSpec(
            num_scalar_prefetch=2,
            grid=(n_tiles, nf),
            in_specs=[tok(lambda i, f, te, nu: (i, 0, 0)),
                      tok(lambda i, f, te, nu: (jnp.minimum(i + 1, n_tiles - 1), 0, 0)),
                      pl.BlockSpec(memory_space=pl.ANY),
                      pl.BlockSpec((None, d, tf), lambda i, f, te, nu: (te[last(i, nu)], 0, fsel(i, f, nu))),
                      pl.BlockSpec((None, d, tf), lambda i, f, te, nu: (te[last(i, nu)], 0, fsel(i, f, nu))),
                      pl.BlockSpec((None, tf, d), lambda i, f, te, nu: (te[last(i, nu)], fsel(i, f, nu), 0))],
            out_specs=pl.BlockSpec((tile, d), lambda i, f, te, nu: (i, 0)),
            scratch_shapes=[pltpu.VMEM((2, tile, d), F32), pltpu.SemaphoreType.DMA((2,))]),
        out_shape=jax.ShapeDtypeStruct((n_rows, d), F32),
        compiler_params=_cparams(("arbitrary", "arbitrary")),
        name="moe_gemm",
    )(tile_expert, n_used, tokens, tokens, x_flat, w1, w3, w2)


def _moe_combine_kernel(pos_ref, posn_ref, route_ref, x1_ref, modx_ref, modz_ref, g2_ref, b2_ref, y_hbm,
                        o_ref, buf_ref, sem, *, nzb, nblk, alpha, tm):
    i = pl.program_id(0)
    slot = i % 2

    def row_copy(p_ref, r, k, s):
        return pltpu.make_async_copy(y_hbm.at[pl.ds(p_ref[0, k * tm + r], 1)],
                                     buf_ref.at[s, pl.ds(k * tm + r, 1)], sem.at[s])

    def issue(p_ref, s):
        def body(r, carry):
            row_copy(p_ref, r, 0, s).start()
            row_copy(p_ref, r, 1, s).start()
            return carry
        lax.fori_loop(0, tm, body, 0, unroll=8)

    @pl.when(i == 0)
    def _():
        issue(pos_ref, 0)

    @pl.when(i + 1 < pl.num_programs(0))
    def _():
        issue(posn_ref, 1 - slot)

    pltpu.make_async_copy(y_hbm.at[pl.ds(0, 2 * tm)], buf_ref.at[slot], sem.at[slot]).wait()
    route = route_ref[...]
    f = route[:, 0:1] * buf_ref[slot, 0:tm] + route[:, 1:2] * buf_ref[slot, tm:2 * tm]
    is_ctx = (i % nblk) < nzb
    mod = jnp.where(is_ctx, modz_ref[...], modx_ref[...])
    o_ref[...] = _ln(alpha * x1_ref[...] + mod[5:6] * f) * g2_ref[...] + b2_ref[...]


def _moe_combine(ys, pos, route, x1, mod, ln_g, ln_b, n_ctx_tokens, alpha):
    bsz, s, d = x1.shape
    wide_ok = s % COMBINE_BLOCK == 0 and n_ctx_tokens % COMBINE_BLOCK == 0
    tm = COMBINE_BLOCK if wide_ok else TOKEN_BLOCK
    nblk = s // tm
    n_steps = bsz * nblk
    nrow = mod.shape[0]
    pos_steps = pos.reshape(n_steps, tm, 2).transpose(0, 2, 1).reshape(n_steps, 1, 2 * tm)
    tok = lambda width: pl.BlockSpec((tm, width), lambda i: (i, 0))
    smem = lambda imap: pl.BlockSpec((None, 1, 2 * tm), imap, memory_space=pltpu.SMEM)
    vec = lambda a: a.reshape(1, -1).astype(F32)
    out = pl.pallas_call(
        functools.partial(_moe_combine_kernel, nzb=n_ctx_tokens // tm, nblk=nblk, alpha=alpha, tm=tm),
        grid=(n_steps,),
        in_specs=[smem(lambda i: (i, 0, 0)),
                  smem(lambda i: (jnp.minimum(i + 1, n_steps - 1), 0, 0)),
                  tok(LANES), tok(d),
                  pl.BlockSpec((None, 6, d), lambda i: (i // nblk, 0, 0)),
                  pl.BlockSpec((None, 6, d), lambda i: (nrow - 1, 0, 0)),
                  pl.BlockSpec((1, d), lambda i: (0, 0)), pl.BlockSpec((1, d), lambda i: (0, 0)),
                  pl.BlockSpec(memory_space=pl.ANY)],
        out_specs=tok(d),
        out_shape=jax.ShapeDtypeStruct((bsz * s, d), F32),
        scratch_shapes=[pltpu.VMEM((2, 2 * tm, d), F32), pltpu.SemaphoreType.DMA((2,))],
        compiler_params=_cparams(("arbitrary",)),
        name="moe_combine",
    )(pos_steps, pos_steps, route.reshape(bsz * s, LANES), x1.reshape(bsz * s, d), mod, mod,
      vec(ln_g), vec(ln_b), ys)
    return out.reshape(bsz, s, d)


def _moe(fx, x1, route, mod, w1, w3, w2, ln_g, ln_b, n_ctx_tokens, alpha):
    bsz, s, d = x1.shape
    n_exp = w1.shape[0]
    idx = route.reshape(bsz * s, LANES)[:, 2:4].astype(jnp.int32)
    row_token, pos, tile_expert, n_used = _route_plan(idx[:, 0], idx[:, 1], n_exp, MOE_TILE)
    ys = _moe_gemm(fx.reshape(bsz * s, d), row_token, tile_expert, n_used, w1, w3, w2, MOE_TILE)
    return _moe_combine(ys, pos, route, x1, mod, ln_g, ln_b, n_ctx_tokens, alpha)


def kernel(x, c, ctx, c_ctx, w_mod, b_mod, w_in, s5_lam_re, s5_lam_im, s5_log_step, s5_b_re, s5_b_im,
           s5_c_re, s5_c_im, s5_d, s5_w_glu, s5_b_glu, attn_sink, ret_log_gamma, w_out,
           ln1_g, ln1_b, ln2_g, ln2_b, ffn_w1, ffn_w3, ffn_w2, moe_router, moe_w1, moe_w3, moe_w2):
    bsz, t_len, d = x.shape
    n_ctx = ctx.shape[1]
    depth = w_in.shape[0]
    alpha = (2 * depth) ** 0.25
    s5_w = s5_d.shape[1]
    att_w = attn_sink.shape[1] * HEAD_DIM
    kv_w = att_w // ATT_REP
    ret_w = ret_log_gamma.shape[2] * HEAD_DIM
    sizes = (s5_w, att_w, kv_w, kv_w, ret_w, ret_w, ret_w, ret_w)
    assert sum(sizes) == w_in.shape[2] and s5_w + att_w + ret_w == w_out.shape[1]
    assert n_ctx % TOKEN_BLOCK == 0 and t_len % TOKEN_BLOCK == 0 and t_len >= 3 * ATT_BLOCK

    pad = (-(bsz + 1)) % 8
    cvec = jnp.concatenate([jnp.zeros((pad, d), F32), c_ctx[None].astype(F32)], axis=0)
    cvec = jnp.concatenate([c.astype(F32), cvec], axis=0)
    mod_all = _modulation(cvec, w_mod.astype(F32), b_mod.astype(F32)).reshape(depth, bsz + pad + 1, 6, d)

    tabs = _rope_tables(t_len, n_ctx)
    xz, ctx_in = x.astype(F32), ctx.astype(F32)
    for l in range(depth):
        need_ctx = l < depth - 1
        mod = mod_all[l]
        u, qa, va, qr, vr, gr, ka, kr = _inproj(xz, ctx_in, mod, w_in[l].astype(BF16), tabs, sizes, n_ctx)
        s5w = _s5_weights(s5_lam_re[l], s5_lam_im[l], s5_log_step[l], s5_b_re[l], s5_b_im[l],
                          s5_c_re[l], s5_c_im[l], S5_CHUNK)
        y_s5 = _s5_scan(u, s5w, n_ctx)
        o_att = _attention(qa, ka, va, attn_sink[l], n_ctx, need_ctx)
        o_ret = _retention(qr, kr, vr, gr, ret_log_gamma[l], n_ctx)
        i = l // 2
        router = None if l % 2 == 0 else moe_router[i]
        outs = _outproj(y_s5, u, o_att, o_ret, xz, ctx_in, mod, s5_d[l], s5_w_glu[l], s5_b_glu[l], w_out[l],
                        ln1_g[l], ln1_b[l], router, n_ctx, need_ctx, alpha)
        ctx_in = None
        ctx_tokens = n_ctx if need_ctx else 0
        if l % 2 == 0:
            x1, fx = outs
            xz = _ffn(fx, x1, mod, ffn_w1[i].astype(BF16), ffn_w3[i].astype(BF16), ffn_w2[i].astype(BF16),
                      ln2_g[l], ln2_b[l], ctx_tokens, alpha)
        else:
            x1, fx, route = outs
            xz = _moe(fx, x1, route, mod, moe_w1[i].astype(BF16), moe_w3[i].astype(BF16),
                      moe_w2[i].astype(BF16), ln2_g[l], ln2_b[l], ctx_tokens, alpha)
    return xz if xz.shape[1] == t_len else xz[:, n_ctx:]
```

```python
import functools
import math

import jax
import jax.numpy as jnp
import numpy as np
from jax import lax
from jax.experimental import pallas as pl
from jax.experimental.pallas import tpu as pltpu

F32 = jnp.float32
BF16 = jnp.bfloat16

GRID_W = 64
HEAD_DIM = 64
ATT_REP = 4
WINDOW = 128
ATT_BLOCK = 128
RET_CHUNK = 128
LN_EPS = 1e-5
ROPE_BASE = 10000.0
NEG_INF = -1e30
LOG2_E = math.log2(math.e)

LANES = 128
S5_CHUNK = LANES
TOKEN_BLOCK = 256
INPROJ_BLOCK = 768
FFN_BLOCK = 768
FFN_CHUNK = 512
MOE_CHUNK = 256
COMBINE_BLOCK = 512
MOD_BLOCK = 1536
VMEM_LIMIT = 56 * 1024 * 1024


def _cparams(sem):
    return pltpu.CompilerParams(dimension_semantics=sem, vmem_limit_bytes=VMEM_LIMIT)


def _pick_chunk(total, target):
    best = LANES
    for c in range(LANES, target + 1, LANES):
        if total % c == 0:
            best = c
    return best


def _dot(a, b):
    return jnp.dot(a, b, preferred_element_type=F32)


def _dot_nt(a, b):
    return lax.dot_general(a, b, (((1,), (1,)), ((), ())), preferred_element_type=F32)


def _dot_tn(a, b):
    return lax.dot_general(a, b, (((0,), (0,)), ((), ())), preferred_element_type=F32)


def _split_bf16(a):
    hi = a.astype(BF16)
    lo = (a - hi.astype(F32)).astype(BF16)
    return hi, lo


def _dot3(a, b):
    ah, al = _split_bf16(a)
    bh, bl = _split_bf16(b)
    return _dot(ah, bh) + (_dot(ah, bl) + _dot(al, bh))


def _ln(x):
    mu = jnp.mean(x, axis=-1, keepdims=True)
    xc = x - mu
    var = jnp.mean(xc * xc, axis=-1, keepdims=True)
    return xc * lax.rsqrt(var + LN_EPS)


def _silu(x):
    return x * (1.0 / (1.0 + jnp.exp(-x)))


def _sigmoid(x):
    return 1.0 / (1.0 + jnp.exp(-x))


def _gelu_tanh(x):
    c = math.sqrt(2.0 / math.pi)
    return 0.5 * x * (1.0 + jnp.tanh(c * (x + 0.044715 * (x * x * x))))


def _mod_kernel(c_ref, w_ref, b_ref, o_ref):
    o_ref[...] = _dot3(_silu(c_ref[...]), w_ref[...]) + b_ref[...]


def _modulation(cvec, w_mod, b_mod):
    depth, d, n = w_mod.shape
    rows = cvec.shape[0]
    tn = _pick_chunk(n, MOD_BLOCK)
    return pl.pallas_call(
        _mod_kernel,
        grid=(depth, n // tn),
        in_specs=[pl.BlockSpec((rows, d), lambda l, j: (0, 0)),
                  pl.BlockSpec((None, d, tn), lambda l, j: (l, 0, j)),
                  pl.BlockSpec((None, 1, tn), lambda l, j: (l, 0, j))],
        out_specs=pl.BlockSpec((None, rows, tn), lambda l, j: (l, 0, j)),
        out_shape=jax.ShapeDtypeStruct((depth, rows, n), F32),
        compiler_params=_cparams(("parallel", "parallel")),
        name="modulation",
    )(cvec, w_mod, b_mod.reshape(depth, 1, n))


def _rope_slab(xs, cos, sa, sb, half):
    return xs * cos + pltpu.roll(xs, LANES - half, 1) * sa + pltpu.roll(xs, half, 1) * sb


def _stream_specs(x, ctx, tm, first_block=0):
    d = x.shape[2]
    pieces = tm // TOKEN_BLOCK
    piece = lambda imap: pl.BlockSpec((None, TOKEN_BLOCK, d), imap)
    specs, args = [], []
    for k in range(pieces):
        if ctx is None:
            specs.append(piece(lambda b, j, k=k: (b, (j + first_block) * pieces + k, 0)))
            args.append(x)
        else:
            nzb = ctx.shape[1] // TOKEN_BLOCK
            specs.append(piece(lambda b, j, k=k: (b, jnp.minimum((j + first_block) * pieces + k, nzb - 1), 0)))
            specs.append(piece(lambda b, j, k=k: (b, jnp.maximum((j + first_block) * pieces + k - nzb, 0), 0)))
            args += [ctx, x]
    return specs, args


def _stream_pieces(refs, block, n_ctx, split):
    if not split:
        return [r[...] for r in refs]
    pieces = len(refs) // 2
    return [jnp.where((block * pieces + k) * TOKEN_BLOCK < n_ctx, refs[2 * k][...], refs[2 * k + 1][...])
            for k in range(pieces)]


def _inproj_kernel(*refs, n_stream, split, n_ctx, c_sizes, sizes):
    (modx_ref, modz_ref, wc_ref, w_ref, act_ref, ast_ref, rct_ref, rst_ref,
     ac_ref, asa_ref, asb_ref, rc_ref, rsa_ref, rsb_ref,
     ut_ref, qat_ref, vat_ref, qrt_ref, vrt_ref, grt_ref, ka_ref, kr_ref) = refs[n_stream:]
    x_pieces = _stream_pieces(refs[:n_stream], pl.program_id(1), n_ctx, split)
    tm = len(x_pieces) * TOKEN_BLOCK
    is_ctx = pl.program_id(1) * tm + lax.broadcasted_iota(jnp.int32, (tm, 1), 0) < n_ctx
    shift = jnp.where(is_ctx, modz_ref[0:1, :], modx_ref[0:1, :])
    gain = 1.0 + jnp.where(is_ctx, modz_ref[1:2, :], modx_ref[1:2, :])
    h = jnp.concatenate(
        [(_ln(xp) * gain[k * TOKEN_BLOCK:(k + 1) * TOKEN_BLOCK] + shift[k * TOKEN_BLOCK:(k + 1) * TOKEN_BLOCK]
          ).astype(BF16) for k, xp in enumerate(x_pieces)], axis=0)
    scale = HEAD_DIM ** -0.5

    ct = _dot_nt(wc_ref[...], h)
    c_offs = [int(o) for o in np.concatenate([[0], np.cumsum(c_sizes)])]

    def chan(i):
        return ct[c_offs[i]:c_offs[i + 1]]

    def rope_t(x, cos_ref, sin_ref, part, mul):
        cos_t, sin_t = cos_ref[...], sin_ref[...]
        heads = []
        for hd in range(x.shape[0] // HEAD_DIM):
            xh = x[hd * HEAD_DIM:(hd + 1) * HEAD_DIM]
            pieces = []
            for lo in range(0, HEAD_DIM, 2 * part):
                pieces += [xh[lo + part:lo + 2 * part], xh[lo:lo + part]]
            out = xh * cos_t + jnp.concatenate(pieces, axis=0) * sin_t
            heads.append(out * mul if mul != 1.0 else out)
        return jnp.concatenate(heads, axis=0)

    def put(ref, val):
        for k in range(ref.shape[0]):
            ref[k] = val[:, k * ATT_BLOCK:(k + 1) * ATT_BLOCK].astype(ref.dtype)

    ut_ref[...] = chan(0)
    put(qat_ref, rope_t(chan(1), act_ref, ast_ref, HEAD_DIM // 4, scale * LOG2_E))
    put(vat_ref, chan(2))
    put(qrt_ref, rope_t(chan(3), rct_ref, rst_ref, HEAD_DIM // 2, 1.0))
    put(vrt_ref, chan(4))
    put(grt_ref, chan(5))

    p = _dot(h, w_ref[...])
    ac, asa, asb = ac_ref[...], asa_ref[...], asb_ref[...]
    rc, rsa, rsb = rc_ref[...], rsa_ref[...], rsb_ref[...]
    offs = np.concatenate([[0], np.cumsum(sizes)])

    def rope_cols(i, cos, sa, sb, half, mul):
        blk = p[:, int(offs[i]):int(offs[i + 1])]
        slabs = [_rope_slab(blk[:, s:s + LANES], cos, sa, sb, half) for s in range(0, blk.shape[1], LANES)]
        out = slabs[0] if len(slabs) == 1 else jnp.concatenate(slabs, axis=1)
        return out * mul if mul != 1.0 else out

    ka_ref[...] = rope_cols(0, ac, asa, asb, HEAD_DIM // 4, 1.0).astype(ka_ref.dtype)
    kr_ref[...] = rope_cols(1, rc, rsa, rsb, HEAD_DIM // 2, scale)


def _inproj(x, ctx, mod, w_in, tabs, sizes, n_ctx):
    bsz, d = x.shape[0], x.shape[2]
    s = x.shape[1] + (0 if ctx is None else ctx.shape[1])
    tm = _pick_chunk(s, INPROJ_BLOCK)
    stream_specs, stream_args = _stream_specs(x, ctx, tm)
    assert tm % TOKEN_BLOCK == 0, "the output projection reads the S5 arrays in TOKEN_BLOCK pieces"
    nrow = mod.shape[0]
    offs = np.concatenate([[0], np.cumsum(sizes)])
    col = lambda i: w_in[:, int(offs[i]):int(offs[i + 1])]
    c_idx, r_idx = (0, 1, 3, 4, 6, 7), (2, 5)
    c_dtypes = (BF16, BF16, F32, BF16, F32)
    r_dtypes = (BF16, F32)
    c_sizes = tuple(sizes[i] for i in c_idx)
    r_sizes = tuple(sizes[i] for i in r_idx)
    w_c_t = jnp.concatenate([col(i) for i in c_idx], axis=1).T
    w_rest = jnp.concatenate([col(i) for i in r_idx], axis=1)
    sub = tm // ATT_BLOCK
    tok = lambda width: pl.BlockSpec((None, tm, width), lambda b, j: (b, j, 0))
    chan = lambda width: pl.BlockSpec((sub, None, width, ATT_BLOCK), lambda b, j: (j, b, 0, 0))
    tab = pl.BlockSpec((tm, LANES), lambda b, j: (j, 0))
    tab_t = pl.BlockSpec((HEAD_DIM, tm), lambda b, j: (0, j))
    return pl.pallas_call(
        functools.partial(_inproj_kernel, n_stream=len(stream_specs), split=ctx is not None, n_ctx=n_ctx,
                          c_sizes=c_sizes, sizes=r_sizes),
        grid=(bsz, s // tm),
        in_specs=stream_specs + [
                  pl.BlockSpec((None, 6, d), lambda b, j: (b, 0, 0)),
                  pl.BlockSpec((None, 6, d), lambda b, j: (nrow - 1, 0, 0)),
                  pl.BlockSpec((sum(c_sizes), d), lambda b, j: (0, 0)),
                  pl.BlockSpec((d, sum(r_sizes)), lambda b, j: (0, 0)),
                  tab_t, tab_t, tab_t, tab_t, tab, tab, tab, tab, tab, tab],
        out_specs=[pl.BlockSpec((None, None, c_sizes[0], tm), lambda b, j: (j, b, 0, 0))]
                  + [chan(w) for w in c_sizes[1:]] + [tok(w) for w in r_sizes],
        out_shape=[jax.ShapeDtypeStruct((s // tm, bsz, c_sizes[0], tm), F32)]
                  + [jax.ShapeDtypeStruct((s // ATT_BLOCK, bsz, w, ATT_BLOCK), dt)
                     for w, dt in zip(c_sizes[1:], c_dtypes)]
                  + [jax.ShapeDtypeStruct((bsz, s, w), dt) for w, dt in zip(r_sizes, r_dtypes)],
        compiler_params=_cparams(("parallel", "parallel")),
        name="inproj",
    )(*stream_args, mod, mod, w_c_t, w_rest, *tabs)


def _rope_tables(t_len, n_ctx):
    t = jnp.arange(t_len)
    rows = (t // GRID_W).astype(F32)
    cols = (t % GRID_W).astype(F32)
    pos = t.astype(F32)

    def angles(p, dim):
        inv_freq = ROPE_BASE ** (-jnp.arange(0, dim, 2, dtype=F32) / dim)
        return p[:, None] * inv_freq[None, :]

    def head_tables(angs):
        cos = jnp.concatenate([jnp.concatenate([jnp.cos(a), jnp.cos(a)], -1) for a in angs], -1)
        sa = jnp.concatenate([jnp.concatenate([-jnp.sin(a), jnp.zeros_like(a)], -1) for a in angs], -1)
        sb = jnp.concatenate([jnp.concatenate([jnp.zeros_like(a), jnp.sin(a)], -1) for a in angs], -1)
        rep = LANES // HEAD_DIM
        return tuple(jnp.tile(x, (1, rep)) for x in (cos, sa, sb))

    att = head_tables([angles(rows, HEAD_DIM // 2), angles(cols, HEAD_DIM // 2)])
    ret = head_tables([angles(pos, HEAD_DIM)])
    signed_t = lambda tb: (tb[0][:, :HEAD_DIM].T, (tb[1] + tb[2])[:, :HEAD_DIM].T)

    def with_ctx(tb, axis, is_cos):
        shape = list(tb.shape)
        shape[axis] = n_ctx
        lead = jnp.ones(shape, F32) if is_cos else jnp.zeros(shape, F32)
        return jnp.concatenate([lead, tb], axis=axis)

    chan = signed_t(att) + signed_t(ret)
    toks = att + ret
    return (tuple(with_ctx(tb, 1, k % 2 == 0) for k, tb in enumerate(chan))
            + tuple(with_ctx(tb, 0, k % 3 == 0) for k, tb in enumerate(toks)))


def _s5_kernel(u_hbm, tab_ref, wsf_ref, wsb_ref, wrf_ref, wrb_ref, lam_ref, y_ref,
               lhs_ref, sem, m_ref, acc_ref, cf_ref, cb_ref, pf_ref, pb_ref, *, bsz, nz_chunks):
    r_blk, ch, tm = y_ref.shape
    c = S5_CHUNK
    kpb = tm // c
    n_chunks = (r_blk // bsz) * kpb
    g = pl.program_id(0)
    slot = g % 2

    def fetch(group, s):
        for i in range(ch):
            p, half = divmod(i, 2)
            for k in range(kpb):
                pltpu.make_async_copy(u_hbm.at[:, group * ch + i, pl.ds(k * c, c)],
                                      lhs_ref.at[s, p, pl.ds(k * r_blk, r_blk), pl.ds(half * c, c)],
                                      sem.at[s]).start()

    @pl.when(g == 0)
    def _():
        fetch(0, 0)

    @pl.when(g + 1 < pl.num_programs(0))
    def _():
        fetch(g + 1, 1 - slot)

    pltpu.make_async_copy(lhs_ref.at[1 - slot], lhs_ref.at[slot], sem.at[slot]).wait()
    acc_ref[...] = jnp.zeros_like(acc_ref)
    cf_ref[...] = jnp.zeros_like(cf_ref)
    cb_ref[...] = jnp.zeros_like(cb_ref)

    def pair(p, carry):
        for ih in range(2):
            for o in range(ch):
                lag_row = jnp.broadcast_to(tab_ref[2 * p + ih, o:o + 1, :], (16, 2 * c))
                tab = pltpu.roll(lag_row, 0, 1, stride=1, stride_axis=0).astype(BF16)
                for r2 in range(c // 16):
                    lo = c - 16 * r2
                    m_ref[ih * c + 16 * r2:ih * c + 16 * r2 + 16, o * c:(o + 1) * c] = tab[:, lo:lo + c]
        lhs = lhs_ref[slot, p].astype(BF16)
        acc_ref[...] += _dot(lhs, m_ref[...])
        cf_ref[...] += _dot(lhs, wsf_ref[p])
        cb_ref[...] += _dot(lhs, wsb_ref[p])
        return carry

    lax.fori_loop(0, ch // 2, pair, 0)
    lam = lam_ref[...]
    nst = lam.shape[1] // 2

    def advance(s, a, bc, drive):
        return s * a + pltpu.roll(s, nst, 1) * bc + drive

    def sweep(order, c_ref, p_ref, a, bc):
        s = jnp.zeros((bsz, lam.shape[1]), F32)
        for n in order:
            j, k = divmod(n, kpb)
            rows = slice(k * r_blk + j * bsz, k * r_blk + (j + 1) * bsz)
            p_ref[rows, :] = s
            s = advance(s, a, bc, c_ref[rows, :])

    sweep(list(range(n_chunks)), cf_ref, pf_ref, lam[0:1], lam[1:2])
    order_b = list(range(nz_chunks - 1, -1, -1)) + list(range(n_chunks - 1, nz_chunks - 1, -1))
    sweep(order_b, cb_ref, pb_ref, lam[2:3], lam[3:4])
    acc_ref[...] += (_dot(pf_ref[...].astype(BF16), wrf_ref[...])
                     + _dot(pb_ref[...].astype(BF16), wrb_ref[...]))
    for o in range(ch):
        for k in range(kpb):
            y_ref[:, o, k * c:(k + 1) * c] = acc_ref[k * r_blk:(k + 1) * r_blk, o * c:(o + 1) * c]


def _s5_weights(lam_re, lam_im, log_step, b_re, b_im, c_re, c_im, chunk):
    hp = lax.Precision.HIGHEST
    lam = lax.complex(lam_re.astype(F32), lam_im.astype(F32))
    lam_dt = lam * jnp.exp(log_step.astype(F32))[..., None]
    lam_bar = jnp.exp(lam_dt)
    b_bar = lax.complex(b_re.astype(F32), b_im.astype(F32)) * ((lam_bar - 1.0) / lam)[..., None]
    c_mat = lax.complex(c_re.astype(F32), c_im.astype(F32))
    g, n = lam.shape[1], lam.shape[2]
    ch = b_bar.shape[-1]
    steps = jnp.arange(chunk + 1, dtype=F32)
    pw = jnp.exp(steps[None, :, None, None] * lam_dt[:, None])
    kern = jnp.einsum('zgon,zdgn,zgni->zgdoi', c_mat, pw[:, :chunk], b_bar, precision=hp).real
    zero_lag = kern[0, :, 0] + kern[1, :, 0]
    lag_table = jnp.concatenate([jnp.zeros_like(zero_lag)[:, None], kern[1, :, :0:-1], zero_lag[:, None],
                                 kern[0, :, 1:]], axis=1).transpose(0, 3, 2, 1)

    def state_in(pw_s, b_dir):
        w = pw_s[:, :, :, None] * b_dir[None]
        w = jnp.concatenate([w.real, w.imag], axis=2)
        return w.transpose(1, 3, 0, 2).reshape(g, ch // 2, 2 * chunk, 2 * n)

    def state_out(pw_t, c_dir):
        w = c_dir[None] * pw_t[:, :, None, :]
        w = jnp.concatenate([w.real, -w.imag], axis=3)
        return w.transpose(1, 3, 2, 0).reshape(g, 2 * n, ch * chunk)

    wsf = state_in(pw[0, chunk - 1::-1][:chunk], b_bar[0])
    wsb = state_in(pw[1, :chunk], b_bar[1])
    wrf = state_out(pw[0, 1:chunk + 1], c_mat[0])
    wrb = state_out(pw[1, chunk:0:-1], c_mat[1])
    lam_c = pw[:, chunk]
    rows = []
    for z in range(2):
        rows.append(jnp.concatenate([lam_c[z].real, lam_c[z].real], -1))
        rows.append(jnp.concatenate([-lam_c[z].imag, lam_c[z].imag], -1))
    lam_rows = jnp.stack(rows + [jnp.zeros_like(rows[0])] * 4, axis=1)
    return lag_table, wsf.astype(BF16), wsb.astype(BF16), wrf.astype(BF16), wrb.astype(BF16), lam_rows


def _s5_scan(ut, weights, n_ctx):
    nblk, bsz, width, tm = ut.shape
    tab, wsf, wsb, wrf, wrb, lam_rows = weights
    g, ch = tab.shape[0], tab.shape[1]
    c = S5_CHUNK
    nst2 = wsf.shape[3]
    r_blk = nblk * bsz
    rows = r_blk * (tm // c)
    grp = lambda *shape: pl.BlockSpec((None,) + shape, lambda i: (i,) + (0,) * len(shape))
    tok = pl.BlockSpec((r_blk, ch, tm), lambda i: (0, i, 0))
    y = pl.pallas_call(
        functools.partial(_s5_kernel, bsz=bsz, nz_chunks=n_ctx // c),
        grid=(g,),
        in_specs=[pl.BlockSpec(memory_space=pl.ANY), grp(ch, ch, 2 * c), grp(ch // 2, 2 * c, nst2), grp(ch // 2, 2 * c, nst2),
                  grp(nst2, ch * c), grp(nst2, ch * c), grp(8, nst2)],
        out_specs=tok,
        out_shape=jax.ShapeDtypeStruct((r_blk, width, tm), F32),
        scratch_shapes=[pltpu.VMEM((2, ch // 2, rows, 2 * c), F32), pltpu.SemaphoreType.DMA((2,)),
                        pltpu.VMEM((2 * c, ch * c), BF16),
                        pltpu.VMEM((rows, ch * c), F32)] + [pltpu.VMEM((rows, nst2), F32)] * 4,
        compiler_params=_cparams(("arbitrary",)),
        name="s5_scan",
    )(ut.reshape(r_blk, width, tm), tab, wsf, wsb, wrf, wrb, lam_rows)
    return y.reshape(nblk, bsz, width, tm)


def _attn_kernel(sink_ref, q_ref, k_ref, v_ref, o_ref, s_ref, e_ref, *, n_ctx, t_len, q_off, kv_heads):
    qi = pl.program_id(1) + q_off
    nzb = n_ctx // ATT_BLOCK
    band = 3 * ATT_BLOCK
    wide = ATT_REP * ATT_BLOCK
    q = q_ref[...].astype(BF16)
    kc = k_ref[0:n_ctx, :].astype(BF16)
    vc = jnp.concatenate([v_ref[p] for p in range(nzb)], axis=1).astype(BF16)

    def rows(a, i):
        return a[i * HEAD_DIM:(i + 1) * HEAD_DIM]

    def q_group(g):
        qg = jnp.concatenate([rows(q, g * ATT_REP + r) for r in range(ATT_REP)], axis=1)
        zero = jnp.zeros_like(qg)
        return jnp.concatenate([qg if gg == g else zero for gg in range(kv_heads)], axis=0)

    ones_rows = 16

    def v_aug(v, g):
        return jnp.concatenate([rows(v, g), jnp.ones((ones_rows, v.shape[1]), BF16)], axis=0)

    def finish(g, o_aug, extra_den):
        o = o_aug[:HEAD_DIM] / (o_aug[HEAD_DIM:HEAD_DIM + 1] + extra_den)
        for r in range(ATT_REP):
            h = g * ATT_REP + r
            o_ref[h * HEAD_DIM:(h + 1) * HEAD_DIM, :] = o[:, r * ATT_BLOCK:(r + 1) * ATT_BLOCK].astype(o_ref.dtype)

    @pl.when(qi < nzb)
    def _():
        for g in range(kv_heads):
            sink = sink_ref[g:g + 1, :]
            s = _dot(kc, q_group(g))
            m = jnp.maximum(jnp.max(s, axis=0, keepdims=True), sink)
            e = jnp.exp2(s - m)
            finish(g, _dot(v_aug(vc, g), e.astype(BF16)), jnp.exp2(sink - m))

    @pl.when(qi >= nzb)
    def _():
        n = qi - nzb
        start = jnp.clip((n - 1) * ATT_BLOCK, 0, t_len - band)
        kl = k_ref[pl.ds(pl.multiple_of(n_ctx + start, ATT_BLOCK), band), :].astype(BF16)
        p0 = (n_ctx + start) // ATT_BLOCK
        vl = jnp.concatenate([v_ref[p0 + t] for t in range(3)], axis=1).astype(BF16)
        k_pos = start + lax.broadcasted_iota(jnp.int32, (band, wide), 0)
        q_pos = n * ATT_BLOCK + (lax.broadcasted_iota(jnp.int32, (band, wide), 1) & (ATT_BLOCK - 1))
        valid = jnp.abs(k_pos - q_pos) <= WINDOW
        v_all = jnp.concatenate([vl, vc], axis=1)
        n_keys = band + n_ctx
        row_max = []
        for g in range(kv_heads):
            qg = q_group(g)
            m = sink_ref[g:g + 1, :]
            for j in range(0, n_keys, ATT_BLOCK):
                if j < band:
                    s = jnp.where(valid[j:j + ATT_BLOCK], _dot(kl[j:j + ATT_BLOCK], qg), NEG_INF)
                else:
                    s = _dot(kc[j - band:j - band + ATT_BLOCK], qg)
                s_ref[g, j:j + ATT_BLOCK, :] = s
                m = jnp.maximum(m, jnp.max(s, axis=0, keepdims=True))
            row_max.append(m)
        for g in range(kv_heads):
            m = row_max[g]
            for j in range(0, n_keys, ATT_BLOCK):
                e_ref[g, j:j + ATT_BLOCK, :] = jnp.exp2(s_ref[g, j:j + ATT_BLOCK, :] - m).astype(BF16)
            finish(g, _dot(v_aug(v_all, g), e_ref[g]), jnp.exp2(sink_ref[g:g + 1, :] - m))


def _attention(qat, ka, vat, sink, n_ctx, need_ctx):
    nblk, bsz, qw, _ = qat.shape
    s, kvw = ka.shape[1], ka.shape[2]
    t_len = s - n_ctx
    kv_heads = kvw // HEAD_DIM
    q_off = 0 if need_ctx else n_ctx // ATT_BLOCK
    nq = nblk - q_off
    sink_rows = jnp.repeat(sink.astype(F32).reshape(kv_heads, ATT_REP), ATT_BLOCK, axis=1) * LOG2_E
    return pl.pallas_call(
        functools.partial(_attn_kernel, n_ctx=n_ctx, t_len=t_len, q_off=q_off, kv_heads=kv_heads),
        grid=(bsz, nq),
        in_specs=[pl.BlockSpec(sink_rows.shape, lambda b, j: (0, 0)),
                  pl.BlockSpec((None, None, qw, ATT_BLOCK), lambda b, j: (j + q_off, b, 0, 0)),
                  pl.BlockSpec((None, s, kvw), lambda b, j: (b, 0, 0)),
                  pl.BlockSpec((nblk, None, kvw, ATT_BLOCK), lambda b, j: (0, b, 0, 0))],
        out_specs=pl.BlockSpec((None, None, qw, ATT_BLOCK), lambda b, j: (j, b, 0, 0)),
        out_shape=jax.ShapeDtypeStruct((nq, bsz, qw, ATT_BLOCK), BF16),
        scratch_shapes=[pltpu.VMEM((kv_heads, 3 * ATT_BLOCK + n_ctx, ATT_REP * ATT_BLOCK), F32),
                        pltpu.VMEM((kv_heads, 3 * ATT_BLOCK + n_ctx, ATT_REP * ATT_BLOCK), BF16)],
        compiler_params=_cparams(("parallel", "arbitrary")),
        name="window_attention",
    )(sink_rows, qat, ka, vat)


def _ret_kernel(lg_ref, q_ref, k_ref, v_ref, g_ref, o_ref, acc_ref, sf_ref, sb_ref, *, nz_chunks, heads):
    c = RET_CHUNK
    n_chunks, w, _ = q_ref.shape
    f32 = lambda a: a.astype(F32)

    def per_head(shape, axis, group, direction):
        owner = lax.broadcasted_iota(jnp.int32, shape, axis) // group
        out = jnp.zeros(shape, F32)
        for h in range(heads):
            out = jnp.where(owner == h, lg_ref[direction, h], out)
        return out

    row_i = f32(lax.broadcasted_iota(jnp.int32, (w, c), 1))
    qw_f = jnp.exp(per_head((w, c), 0, HEAD_DIM, 0) * (row_i + 1.0))
    qw_b = jnp.exp(per_head((w, c), 0, HEAD_DIM, 1) * (c - row_i))
    key_j = f32(lax.broadcasted_iota(jnp.int32, (c, w), 0))
    kw_f = jnp.exp(per_head((c, w), 1, HEAD_DIM, 0) * (c - 1.0 - key_j))
    kw_b = jnp.exp(per_head((c, w), 1, HEAD_DIM, 1) * key_j)
    dec_f = jnp.exp(per_head((w, 1), 0, HEAD_DIM, 0) * c)
    dec_b = jnp.exp(per_head((w, 1), 0, HEAD_DIM, 1) * c)
    wide = heads * c
    diff = f32((lax.broadcasted_iota(jnp.int32, (c, wide), 1) & (c - 1)) - lax.broadcasted_iota(jnp.int32, (c, wide), 0))
    dec_t = jnp.where(diff >= 0, jnp.exp(per_head((c, wide), 1, c, 0) * jnp.maximum(diff, 0.0)),
                      jnp.exp(per_head((c, wide), 1, c, 1) * jnp.maximum(-diff, 0.0)))
    own_wide = (lax.broadcasted_iota(jnp.int32, (w, wide), 0) // HEAD_DIM
                == lax.broadcasted_iota(jnp.int32, (w, wide), 1) // c)
    own_sq = f32(lax.broadcasted_iota(jnp.int32, (w, w), 0) // HEAD_DIM
                 == lax.broadcasted_iota(jnp.int32, (w, w), 1) // HEAD_DIM)
    sf_ref[...] = jnp.zeros_like(sf_ref)
    sb_ref[...] = jnp.zeros_like(sb_ref)

    def block_diag(x):
        return jnp.where(own_wide, jnp.concatenate([x] * heads, axis=1), 0.0).astype(BF16)

    def fwd(n, carry):
        qt, vt = q_ref[n], v_ref[n]
        k = k_ref[pl.ds(pl.multiple_of(n * c, c), c), :]
        scores_t = _dot(k.astype(BF16), block_diag(qt)) * dec_t
        stacked = jnp.concatenate([scores_t[:, h * c:(h + 1) * c] for h in range(heads)], axis=0)
        o = _dot(block_diag(vt), stacked.astype(BF16))
        s_prev = sf_ref[...]
        o = o + _dot(s_prev.astype(BF16), (qt * qw_f).astype(BF16))
        sf_ref[...] = dec_f * s_prev + own_sq * _dot(vt.astype(BF16), (k * kw_f).astype(BF16))
        acc_ref[n] = o
        return carry

    lax.fori_loop(0, n_chunks, fwd, 0, unroll=3)

    def bwd(i, carry):
        n = jnp.where(i < nz_chunks, nz_chunks - 1 - i, n_chunks - 1 - i + nz_chunks)
        qt, vt = q_ref[n], v_ref[n]
        k = k_ref[pl.ds(pl.multiple_of(n * c, c), c), :]
        s_prev = sb_ref[...]
        o = acc_ref[n] + _dot(s_prev.astype(BF16), (qt * qw_b).astype(BF16))
        sb_ref[...] = dec_b * s_prev + own_sq * _dot(vt.astype(BF16), (k * kw_b).astype(BF16))
        gate = _silu(g_ref[n])
        for h in range(heads):
            rows = slice(h * HEAD_DIM, (h + 1) * HEAD_DIM)
            oh = o[rows]
            mu = jnp.mean(oh, axis=0, keepdims=True)
            oc = oh - mu
            var = jnp.mean(oc * oc, axis=0, keepdims=True)
            o_ref[n, rows, :] = (oc * lax.rsqrt(var + LN_EPS) * gate[rows]).astype(o_ref.dtype)
        return carry

    lax.fori_loop(0, n_chunks, bwd, 0, unroll=3)


def _retention(qrt, kr, vrt, grt, log_gamma, n_ctx):
    n_chunks, bsz, w, c = qrt.shape
    s = kr.shape[1]
    chan = pl.BlockSpec((n_chunks, None, w, c), lambda b: (0, b, 0, 0))
    return pl.pallas_call(
        functools.partial(_ret_kernel, nz_chunks=n_ctx // RET_CHUNK, heads=w // HEAD_DIM),
        grid=(bsz,),
        in_specs=[pl.BlockSpec(memory_space=pltpu.SMEM), chan, pl.BlockSpec((None, s, w), lambda b: (b, 0, 0)),
                  chan, chan],
        out_specs=chan,
        out_shape=jax.ShapeDtypeStruct((n_chunks, bsz, w, c), BF16),
        scratch_shapes=[pltpu.VMEM((n_chunks, w, c), F32), pltpu.VMEM((w, w), F32), pltpu.VMEM((w, w), F32)],
        compiler_params=_cparams(("parallel",)),
        name="retention",
    )(log_gamma.astype(F32), qrt, kr, vrt, grt)


def _outproj_kernel(*refs, n_stream, split, first_block, n_ctx, nzb, alpha, with_router, w_s5, w_att, n_exp):
    if with_router:
        (y_ref, u_ref, a_ref, r_ref, modx_ref, modz_ref, d_ref, wg_ref, bg_ref, wo_ref,
         g1_ref, b1_ref, rt_ref, x1_ref, fx_ref, gate_ref) = refs[n_stream:]
    else:
        (y_ref, u_ref, a_ref, r_ref, modx_ref, modz_ref, d_ref, wg_ref, bg_ref, wo_ref,
         g1_ref, b1_ref, x1_ref, fx_ref) = refs[n_stream:]
    (x_in,) = _stream_pieces(refs[:n_stream], pl.program_id(1) + first_block, n_ctx, split)
    is_ctx = pl.program_id(1) < nzb
    mod = jnp.where(is_ctx, modz_ref[...], modx_ref[...])
    g = _gelu_tanh(y_ref[...] + d_ref[...] * u_ref[...])
    s5 = g * _sigmoid(_dot(wg_ref[...], g.astype(BF16)) + bg_ref[...])
    def chan_major(ref, lo, hi):
        return jnp.concatenate([_dot_tn(ref[k].astype(BF16), wo_ref[lo:hi, :]) for k in range(ref.shape[0])],
                               axis=0)

    mix = (_dot_tn(s5.astype(BF16), wo_ref[0:w_s5, :]) + chan_major(a_ref, w_s5, w_s5 + w_att)
           + chan_major(r_ref, w_s5 + w_att, wo_ref.shape[0]))
    x1 = _ln(alpha * x_in + mod[2:3] * mix) * g1_ref[...] + b1_ref[...]
    x1_ref[...] = x1
    fx = _ln(x1) * (1.0 + mod[4:5]) + mod[3:4]
    fx_ref[...] = fx.astype(fx_ref.dtype)
    if with_router:
        lane = lax.broadcasted_iota(jnp.int32, (fx.shape[0], LANES), 1)
        logits = jnp.where(lane < n_exp, _dot3(fx, rt_ref[...]), -jnp.inf)
        m1 = jnp.max(logits, axis=1, keepdims=True)
        i1 = jnp.min(jnp.where(logits == m1, lane, LANES), axis=1, keepdims=True)
        rest = jnp.where(lane == i1, -jnp.inf, logits)
        m2 = jnp.max(rest, axis=1, keepdims=True)
        i2 = jnp.min(jnp.where(rest == m2, lane, LANES), axis=1, keepdims=True)
        e2 = jnp.exp(m2 - m1)
        den = 1.0 + e2
        route = jnp.where(lane == 0, 1.0 / den, jnp.where(lane == 1, e2 / den, 0.0))
        route = jnp.where(lane == 2, i1.astype(F32), jnp.where(lane == 3, i2.astype(F32), route))
        gate_ref[...] = route


def _outproj(y_s5, u, o_att, o_ret, x, ctx, mod, s5_d, w_glu, b_glu, w_out, ln_g, ln_b, router, n_ctx,
             need_ctx, alpha):
    bsz, d = x.shape[0], x.shape[2]
    s = x.shape[1] + (0 if ctx is None else ctx.shape[1])
    tm = TOKEN_BLOCK
    nzb = n_ctx // tm
    off = 0 if need_ctx else nzb
    nblk = s // tm - off
    s_out = nblk * tm
    nrow = mod.shape[0]
    w_s5, w_att = y_s5.shape[2], o_att.shape[2]
    sub = tm // ATT_BLOCK
    att_off = off if o_att.shape[0] == s // ATT_BLOCK else 0
    att = pl.BlockSpec((sub, None, w_att, ATT_BLOCK), lambda b, j: (j + att_off, b, 0, 0))

    def tok(width, shift):
        return pl.BlockSpec((None, tm, width), lambda b, j: (b, j + shift, 0))

    def full(a):
        return pl.BlockSpec(a.shape, lambda b, j: (0,) * a.ndim)

    per_in = y_s5.shape[3] // tm
    chan = pl.BlockSpec((None, None, w_s5, tm), lambda b, j: ((j + off) // per_in, b, 0, (j + off) % per_in))
    vec = lambda a: a.reshape(1, -1).astype(F32)
    col = lambda a: a.reshape(-1, 1).astype(F32)
    consts = [col(s5_d), w_glu.T.astype(BF16), col(b_glu), w_out.astype(BF16), vec(ln_g), vec(ln_b)]
    ret = pl.BlockSpec((sub, None, o_ret.shape[2], ATT_BLOCK), lambda b, j: (j + off, b, 0, 0))
    stream_specs, stream_args = _stream_specs(x, ctx, tm, first_block=off)
    in_specs = stream_specs + [chan, chan, att, ret,
                pl.BlockSpec((None, 6, d), lambda b, j: (b, 0, 0)),
                pl.BlockSpec((None, 6, d), lambda b, j: (nrow - 1, 0, 0))] + [full(a) for a in consts]
    out_specs = [tok(d, 0), tok(d, 0)]
    fx_dtype = BF16 if router is None else F32
    out_shape = [jax.ShapeDtypeStruct((bsz, s_out, d), F32), jax.ShapeDtypeStruct((bsz, s_out, d), fx_dtype)]
    args = stream_args + [y_s5, u, o_att, o_ret, mod, mod] + consts
    with_router = router is not None
    n_exp = 0
    if with_router:
        n_exp = router.shape[1]
        router_pad = jnp.pad(router.astype(F32), ((0, 0), (0, LANES - n_exp)))
        args.append(router_pad)
        in_specs.append(full(router_pad))
        out_specs.append(tok(LANES, 0))
        out_shape.append(jax.ShapeDtypeStruct((bsz, s_out, LANES), F32))
    return pl.pallas_call(
        functools.partial(_outproj_kernel, n_stream=len(stream_specs), split=ctx is not None, first_block=off,
                          n_ctx=n_ctx, nzb=nzb - off, alpha=alpha, with_router=with_router,
                          w_s5=w_s5, w_att=w_att, n_exp=n_exp),
        grid=(bsz, nblk),
        in_specs=in_specs,
        out_specs=out_specs,
        out_shape=out_shape,
        compiler_params=_cparams(("parallel", "parallel")),
        name="outproj",
    )(*args)


def _swiglu_into(x_bf16, w1_ref, w3_ref, w2_ref, acc_ref, fc):
    for s in range(0, w1_ref.shape[1], fc):
        h1 = _dot(x_bf16, w1_ref[:, s:s + fc])
        h3 = _dot(x_bf16, w3_ref[:, s:s + fc])
        acc_ref[...] += _dot((_silu(h1) * h3).astype(BF16), w2_ref[s:s + fc, :])


def _ffn_kernel(fx_ref, x1_ref, modx_ref, modz_ref, w1_ref, w3_ref, w2_ref, g2_ref, b2_ref,
                o_ref, acc_ref, *, n_ctx_tokens, alpha, fc):
    tm = acc_ref.shape[0]
    acc_ref[...] = jnp.zeros_like(acc_ref)
    _swiglu_into(fx_ref[...], w1_ref, w3_ref, w2_ref, acc_ref, fc)
    row = pl.program_id(1) * tm + lax.broadcasted_iota(jnp.int32, (tm, 1), 0)
    gate = jnp.where(row < n_ctx_tokens, modz_ref[5:6, :], modx_ref[5:6, :])
    o_ref[...] = _ln(alpha * x1_ref[...] + gate * acc_ref[...]) * g2_ref[...] + b2_ref[...]


def _ffn(fx, x1, mod, w1, w3, w2, ln_g, ln_b, n_ctx_tokens, alpha):
    bsz, s, d = x1.shape
    ff = w1.shape[1]
    tm = _pick_chunk(s, FFN_BLOCK)
    nrow = mod.shape[0]
    tok = pl.BlockSpec((None, tm, d), lambda b, j: (b, j, 0))
    full = lambda a: pl.BlockSpec(a.shape, lambda b, j: (0,) * a.ndim, pipeline_mode=pl.Buffered(1))
    vec = lambda a: a.reshape(1, -1).astype(F32)
    return pl.pallas_call(
        functools.partial(_ffn_kernel, n_ctx_tokens=n_ctx_tokens, alpha=alpha, fc=_pick_chunk(ff, FFN_CHUNK)),
        grid=(bsz, s // tm),
        in_specs=[tok, tok,
                  pl.BlockSpec((None, 6, d), lambda b, j: (b, 0, 0)),
                  pl.BlockSpec((None, 6, d), lambda b, j: (nrow - 1, 0, 0)),
                  full(w1), full(w3), full(w2),
                  pl.BlockSpec((1, d), lambda b, j: (0, 0)), pl.BlockSpec((1, d), lambda b, j: (0, 0))],
        out_specs=tok,
        out_shape=jax.ShapeDtypeStruct((bsz, s, d), F32),
        scratch_shapes=[pltpu.VMEM((tm, d), F32)],
        compiler_params=_cparams(("parallel", "parallel")),
        name="dense_ffn",
    )(fx, x1, mod, mod, w1, w3, w2, vec(ln_g), vec(ln_b))


MOE_TILE = 1024


def _route_plan(e1, e2, n_exp, tile):
    n = e1.shape[0]
    pair_e = jnp.stack([e1, e2], axis=1).reshape(-1)
    onehot = (pair_e[:, None] == jnp.arange(n_exp, dtype=jnp.int32)[None, :]).astype(jnp.int32)
    before = jnp.cumsum(onehot, axis=0) - onehot
    rank = jnp.sum(before * onehot, axis=1)
    counts = jnp.sum(onehot, axis=0)
    padded = (counts + tile - 1) // tile * tile
    ends = jnp.cumsum(padded)
    starts = ends - padded
    dest = starts[pair_e] + rank
    n_rows = (2 * n + n_exp * (tile - 1)) // tile * tile
    n_tiles = n_rows // tile
    tile_start = jnp.arange(n_tiles, dtype=jnp.int32) * tile
    tile_expert = jnp.minimum(jnp.sum((tile_start[:, None] >= ends[None, :]).astype(jnp.int32), axis=1),
                              n_exp - 1)
    by_expert = jnp.sort(pair_e * (2 * n) + jnp.arange(2 * n, dtype=jnp.int32)) % (2 * n)
    row = jnp.arange(n_rows, dtype=jnp.int32)
    row_e = jnp.repeat(tile_expert, tile)
    in_group = row - starts[row_e]
    src = jnp.minimum(in_group + (jnp.cumsum(counts) - counts)[row_e], 2 * n - 1)
    row_token = jnp.where(in_group < counts[row_e], by_expert[src] // 2, 0)
    n_used = (ends[-1] // tile).astype(jnp.int32).reshape(1)
    return row_token, dest.reshape(n, 2), tile_expert, n_used


def _moe_gemm_kernel(te_ref, nused_ref, tok_ref, tokn_ref, x_hbm, w1_ref, w3_ref, w2_ref, y_ref,
                     xbuf_ref, sem, *, fc, tile):
    i = pl.program_id(0)
    f = pl.program_id(1)
    slot = i % 2
    n_used = nused_ref[0]

    def row_copy(t_ref, r, s):
        return pltpu.make_async_copy(x_hbm.at[pl.ds(t_ref[0, r], 1)], xbuf_ref.at[s, pl.ds(r, 1)], sem.at[s])

    def issue(t_ref, s):
        def body(it, carry):
            base = pl.multiple_of(it * 8, 8)
            for k in range(8):
                row_copy(t_ref, base + k, s).start()
            return carry
        lax.fori_loop(0, tile // 8, body, 0)

    @pl.when(f == 0)
    def _():
        y_ref[...] = jnp.zeros_like(y_ref)

        @pl.when(i == 0)
        def _():
            issue(tok_ref, 0)

        @pl.when(i + 1 < n_used)
        def _():
            issue(tokn_ref, 1 - slot)

        @pl.when(i < n_used)
        def _():
            pltpu.make_async_copy(x_hbm.at[pl.ds(0, tile)], xbuf_ref.at[slot], sem.at[slot]).wait()

    @pl.when(i < n_used)
    def _():
        _swiglu_into(xbuf_ref[slot].astype(BF16), w1_ref, w3_ref, w2_ref, y_ref, fc)


def _moe_gemm(x_flat, row_token, tile_expert, n_used, w1, w3, w2, tile):
    n_rows = row_token.shape[0]
    n_tiles = n_rows // tile
    d = x_flat.shape[1]
    ff = w1.shape[2]
    nf = 2 if ff % (2 * LANES) == 0 else 1
    tf = ff // nf
    last = lambda i, nu: jnp.minimum(i, nu[0] - 1)
    fsel = lambda i, f, nu: jnp.where(i < nu[0], f, nf - 1)
    tok = lambda imap: pl.BlockSpec((None, 1, tile), imap, memory_space=pltpu.SMEM)
    tokens = row_token.reshape(n_tiles, 1, tile)
    return pl.pallas_call(
        functools.partial(_moe_gemm_kernel, fc=_pick_chunk(tf, MOE_CHUNK), tile=tile),
        grid_spec=pltpu.PrefetchScalarGridSpec(
            num_scalar_prefetch=2,
            grid=(n_tiles, nf),
            in_specs=[tok(lambda i, f, te, nu: (i, 0, 0)),
                      tok(lambda i, f, te, nu: (jnp.minimum(i + 1, n_tiles - 1), 0, 0)),
                      pl.BlockSpec(memory_space=pl.ANY),
                      pl.BlockSpec((None, d, tf), lambda i, f, te, nu: (te[last(i, nu)], 0, fsel(i, f, nu))),
                      pl.BlockSpec((None, d, tf), lambda i, f, te, nu: (te[last(i, nu)], 0, fsel(i, f, nu))),
                      pl.BlockSpec((None, tf, d), lambda i, f, te, nu: (te[last(i, nu)], fsel(i, f, nu), 0))],
            out_specs=pl.BlockSpec((tile, d), lambda i, f, te, nu: (i, 0)),
            scratch_shapes=[pltpu.VMEM((2, tile, d), F32), pltpu.SemaphoreType.DMA((2,))]),
        out_shape=jax.ShapeDtypeStruct((n_rows, d), F32),
        compiler_params=_cparams(("arbitrary", "arbitrary")),
        name="moe_gemm",
    )(tile_expert, n_used, tokens, tokens, x_flat, w1, w3, w2)


def _moe_combine_kernel(pos_ref, posn_ref, route_ref, x1_ref, modx_ref, modz_ref, g2_ref, b2_ref, y_hbm,
                        o_ref, buf_ref, sem, *, nzb, nblk, alpha, tm):
    i = pl.program_id(0)
    slot = i % 2

    def row_copy(p_ref, r, k, s):
        return pltpu.make_async_copy(y_hbm.at[pl.ds(p_ref[0, k * tm + r], 1)],
                                     buf_ref.at[s, pl.ds(k * tm + r, 1)], sem.at[s])

    def issue(p_ref, s):
        def body(r, carry):
            row_copy(p_ref, r, 0, s).start()
            row_copy(p_ref, r, 1, s).start()
            return carry
        lax.fori_loop(0, tm, body, 0, unroll=8)

    @pl.when(i == 0)
    def _():
        issue(pos_ref, 0)

    @pl.when(i + 1 < pl.num_programs(0))
    def _():
        issue(posn_ref, 1 - slot)

    pltpu.make_async_copy(y_hbm.at[pl.ds(0, 2 * tm)], buf_ref.at[slot], sem.at[slot]).wait()
    route = route_ref[...]
    f = route[:, 0:1] * buf_ref[slot, 0:tm] + route[:, 1:2] * buf_ref[slot, tm:2 * tm]
    is_ctx = (i % nblk) < nzb
    mod = jnp.where(is_ctx, modz_ref[...], modx_ref[...])
    o_ref[...] = _ln(alpha * x1_ref[...] + mod[5:6] * f) * g2_ref[...] + b2_ref[...]


def _moe_combine(ys, pos, route, x1, mod, ln_g, ln_b, n_ctx_tokens, alpha):
    bsz, s, d = x1.shape
    wide_ok = s % COMBINE_BLOCK == 0 and n_ctx_tokens % COMBINE_BLOCK == 0
    tm = COMBINE_BLOCK if wide_ok else TOKEN_BLOCK
    nblk = s // tm
    n_steps = bsz * nblk
    nrow = mod.shape[0]
    pos_steps = pos.reshape(n_steps, tm, 2).transpose(0, 2, 1).reshape(n_steps, 1, 2 * tm)
    tok = lambda width: pl.BlockSpec((tm, width), lambda i: (i, 0))
    smem = lambda imap: pl.BlockSpec((None, 1, 2 * tm), imap, memory_space=pltpu.SMEM)
    vec = lambda a: a.reshape(1, -1).astype(F32)
    out = pl.pallas_call(
        functools.partial(_moe_combine_kernel, nzb=n_ctx_tokens // tm, nblk=nblk, alpha=alpha, tm=tm),
        grid=(n_steps,),
        in_specs=[smem(lambda i: (i, 0, 0)),
                  smem(lambda i: (jnp.minimum(i + 1, n_steps - 1), 0, 0)),
                  tok(LANES), tok(d),
                  pl.BlockSpec((None, 6, d), lambda i: (i // nblk, 0, 0)),
                  pl.BlockSpec((None, 6, d), lambda i: (nrow - 1, 0, 0)),
                  pl.BlockSpec((1, d), lambda i: (0, 0)), pl.BlockSpec((1, d), lambda i: (0, 0)),
                  pl.BlockSpec(memory_space=pl.ANY)],
        out_specs=tok(d),
        out_shape=jax.ShapeDtypeStruct((bsz * s, d), F32),
        scratch_shapes=[pltpu.VMEM((2, 2 * tm, d), F32), pltpu.SemaphoreType.DMA((2,))],
        compiler_params=_cparams(("arbitrary",)),
        name="moe_combine",
    )(pos_steps, pos_steps, route.reshape(bsz * s, LANES), x1.reshape(bsz * s, d), mod, mod,
      vec(ln_g), vec(ln_b), ys)
    return out.reshape(bsz, s, d)


def _moe(fx, x1, route, mod, w1, w3, w2, ln_g, ln_b, n_ctx_tokens, alpha):
    bsz, s, d = x1.shape
    n_exp = w1.shape[0]
    idx = route.reshape(bsz * s, LANES)[:, 2:4].astype(jnp.int32)
    row_token, pos, tile_expert, n_used = _route_plan(idx[:, 0], idx[:, 1], n_exp, MOE_TILE)
    ys = _moe_gemm(fx.reshape(bsz * s, d), row_token, tile_expert, n_used, w1, w3, w2, MOE_TILE)
    return _moe_combine(ys, pos, route, x1, mod, ln_g, ln_b, n_ctx_tokens, alpha)


def kernel(x, c, ctx, c_ctx, w_mod, b_mod, w_in, s5_lam_re, s5_lam_im, s5_log_step, s5_b_re, s5_b_im,
           s5_c_re, s5_c_im, s5_d, s5_w_glu, s5_b_glu, attn_sink, ret_log_gamma, w_out,
           ln1_g, ln1_b, ln2_g, ln2_b, ffn_w1, ffn_w3, ffn_w2, moe_router, moe_w1, moe_w3, moe_w2):
    bsz, t_len, d = x.shape
    n_ctx = ctx.shape[1]
    depth = w_in.shape[0]
    alpha = (2 * depth) ** 0.25
    s5_w = s5_d.shape[1]
    att_w = attn_sink.shape[1] * HEAD_DIM
    kv_w = att_w // ATT_REP
    ret_w = ret_log_gamma.shape[2] * HEAD_DIM
    sizes = (s5_w, att_w, kv_w, kv_w, ret_w, ret_w, ret_w, ret_w)
    assert sum(sizes) == w_in.shape[2] and s5_w + att_w + ret_w == w_out.shape[1]
    assert n_ctx % TOKEN_BLOCK == 0 and t_len % TOKEN_BLOCK == 0 and t_len >= 3 * ATT_BLOCK

    pad = (-(bsz + 1)) % 8
    cvec = jnp.concatenate([jnp.zeros((pad, d), F32), c_ctx[None].astype(F32)], axis=0)
    cvec = jnp.concatenate([c.astype(F32), cvec], axis=0)
    mod_all = _modulation(cvec, w_mod.astype(F32), b_mod.astype(F32)).reshape(depth, bsz + pad + 1, 6, d)

    tabs = _rope_tables(t_len, n_ctx)
    xz, ctx_in = x.astype(F32), ctx.astype(F32)
    for l in range(depth):
        need_ctx = l < depth - 1
        mod = mod_all[l]
        u, qa, va, qr, vr, gr, ka, kr = _inproj(xz, ctx_in, mod, w_in[l].astype(BF16), tabs, sizes, n_ctx)
        s5w = _s5_weights(s5_lam_re[l], s5_lam_im[l], s5_log_step[l], s5_b_re[l], s5_b_im[l],
                          s5_c_re[l], s5_c_im[l], S5_CHUNK)
        y_s5 = _s5_scan(u, s5w, n_ctx)
        o_att = _attention(qa, ka, va, attn_sink[l], n_ctx, need_ctx)
        o_ret = _retention(qr, kr, vr, gr, ret_log_gamma[l], n_ctx)
        i = l // 2
        router = None if l % 2 == 0 else moe_router[i]
        outs = _outproj(y_s5, u, o_att, o_ret, xz, ctx_in, mod, s5_d[l], s5_w_glu[l], s5_b_glu[l], w_out[l],
                        ln1_g[l], ln1_b[l], router, n_ctx, need_ctx, alpha)
        ctx_in = None
        ctx_tokens = n_ctx if need_ctx else 0
        if l % 2 == 0:
            x1, fx = outs
            xz = _ffn(fx, x1, mod, ffn_w1[i].astype(BF16), ffn_w3[i].astype(BF16), ffn_w2[i].astype(BF16),
                      ln2_g[l], ln2_b[l], ctx_tokens, alpha)
        else:
            x1, fx, route = outs
            xz = _moe(fx, x1, route, mod, moe_w1[i].astype(BF16), moe_w3[i].astype(BF16),
                      moe_w2[i].astype(BF16), ln2_g[l], ln2_b[l], ctx_tokens, alpha)
    return xz if xz.shape[1] == t_len else xz[:, n_ctx:]
```

```python
import functools
import math

import jax
import jax.numpy as jnp
import numpy as np
from jax import lax
from jax.experimental import pallas as pl
from jax.experimental.pallas import tpu as pltpu

F32 = jnp.float32
BF16 = jnp.bfloat16

GRID_W = 64
HEAD_DIM = 64
ATT_REP = 4
WINDOW = 128
ATT_BLOCK = 128
RET_CHUNK = 128
LN_EPS = 1e-5
ROPE_BASE = 10000.0
NEG_INF = -1e30
LOG2_E = math.log2(math.e)

LANES = 128
S5_CHUNK = LANES
TOKEN_BLOCK = 256
INPROJ_BLOCK = 768
FFN_BLOCK = 768
FFN_CHUNK = 512
MOE_CHUNK = 256
COMBINE_BLOCK = 512
MOD_BLOCK = 1536
VMEM_LIMIT = 56 * 1024 * 1024


def _cparams(sem):
    return pltpu.CompilerParams(dimension_semantics=sem, vmem_limit_bytes=VMEM_LIMIT)


def _pick_chunk(total, target):
    best = LANES
    for c in range(LANES, target + 1, LANES):
        if total % c == 0:
            best = c
    return best


def _dot(a, b):
    return jnp.dot(a, b, preferred_element_type=F32)


def _dot_nt(a, b):
    return lax.dot_general(a, b, (((1,), (1,)), ((), ())), preferred_element_type=F32)


def _dot_tn(a, b):
    return lax.dot_general(a, b, (((0,), (0,)), ((), ())), preferred_element_type=F32)


def _split_bf16(a):
    hi = a.astype(BF16)
    lo = (a - hi.astype(F32)).astype(BF16)
    return hi, lo


def _dot3(a, b):
    ah, al = _split_bf16(a)
    bh, bl = _split_bf16(b)
    return _dot(ah, bh) + (_dot(ah, bl) + _dot(al, bh))


def _ln(x):
    mu = jnp.mean(x, axis=-1, keepdims=True)
    xc = x - mu
    var = jnp.mean(xc * xc, axis=-1, keepdims=True)
    return xc * lax.rsqrt(var + LN_EPS)


def _silu(x):
    return x * (1.0 / (1.0 + jnp.exp(-x)))


def _sigmoid(x):
    return 1.0 / (1.0 + jnp.exp(-x))


def _gelu_tanh(x):
    c = math.sqrt(2.0 / math.pi)
    return 0.5 * x * (1.0 + jnp.tanh(c * (x + 0.044715 * (x * x * x))))


def _mod_kernel(c_ref, w_ref, b_ref, o_ref):
    o_ref[...] = _dot3(_silu(c_ref[...]), w_ref[...]) + b_ref[...]


def _modulation(cvec, w_mod, b_mod):
    depth, d, n = w_mod.shape
    rows = cvec.shape[0]
    tn = _pick_chunk(n, MOD_BLOCK)
    return pl.pallas_call(
        _mod_kernel,
        grid=(depth, n // tn),
        in_specs=[pl.BlockSpec((rows, d), lambda l, j: (0, 0)),
                  pl.BlockSpec((None, d, tn), lambda l, j: (l, 0, j)),
                  pl.BlockSpec((None, 1, tn), lambda l, j: (l, 0, j))],
        out_specs=pl.BlockSpec((None, rows, tn), lambda l, j: (l, 0, j)),
        out_shape=jax.ShapeDtypeStruct((depth, rows, n), F32),
        compiler_params=_cparams(("parallel", "parallel")),
        name="modulation",
    )(cvec, w_mod, b_mod.reshape(depth, 1, n))


def _rope_slab(xs, cos, sa, sb, half):
    return xs * cos + pltpu.roll(xs, LANES - half, 1) * sa + pltpu.roll(xs, half, 1) * sb


def _stream_specs(x, ctx, tm, first_block=0):
    d = x.shape[2]
    pieces = tm // TOKEN_BLOCK
    piece = lambda imap: pl.BlockSpec((None, TOKEN_BLOCK, d), imap)
    specs, args = [], []
    for k in range(pieces):
        if ctx is None:
            specs.append(piece(lambda b, j, k=k: (b, (j + first_block) * pieces + k, 0)))
            args.append(x)
        else:
            nzb = ctx.shape[1] // TOKEN_BLOCK
            specs.append(piece(lambda b, j, k=k: (b, jnp.minimum((j + first_block) * pieces + k, nzb - 1), 0)))
            specs.append(piece(lambda b, j, k=k: (b, jnp.maximum((j + first_block) * pieces + k - nzb, 0), 0)))
            args += [ctx, x]
    return specs, args


def _stream_pieces(refs, block, n_ctx, split):
    if not split:
        return [r[...] for r in refs]
    pieces = len(refs) // 2
    return [jnp.where((block * pieces + k) * TOKEN_BLOCK < n_ctx, refs[2 * k][...], refs[2 * k + 1][...])
            for k in range(pieces)]


def _inproj_kernel(*refs, n_stream, split, n_ctx, c_sizes, sizes):
    (modx_ref, modz_ref, wc_ref, w_ref, act_ref, ast_ref, rct_ref, rst_ref,
     ac_ref, asa_ref, asb_ref, rc_ref, rsa_ref, rsb_ref,
     ut_ref, qat_ref, vat_ref, qrt_ref, vrt_ref, grt_ref, ka_ref, kr_ref) = refs[n_stream:]
    x_pieces = _stream_pieces(refs[:n_stream], pl.program_id(1), n_ctx, split)
    tm = len(x_pieces) * TOKEN_BLOCK
    is_ctx = pl.program_id(1) * tm + lax.broadcasted_iota(jnp.int32, (tm, 1), 0) < n_ctx
    shift = jnp.where(is_ctx, modz_ref[0:1, :], modx_ref[0:1, :])
    gain = 1.0 + jnp.where(is_ctx, modz_ref[1:2, :], modx_ref[1:2, :])
    h = jnp.concatenate(
        [(_ln(xp) * gain[k * TOKEN_BLOCK:(k + 1) * TOKEN_BLOCK] + shift[k * TOKEN_BLOCK:(k + 1) * TOKEN_BLOCK]
          ).astype(BF16) for k, xp in enumerate(x_pieces)], axis=0)
    scale = HEAD_DIM ** -0.5

    ct = _dot_nt(wc_ref[...], h)
    c_offs = [int(o) for o in np.concatenate([[0], np.cumsum(c_sizes)])]

    def chan(i):
        return ct[c_offs[i]:c_offs[i + 1]]

    def rope_t(x, cos_ref, sin_ref, part, mul):
        cos_t, sin_t = cos_ref[...], sin_ref[...]
        heads = []
        for hd in range(x.shape[0] // HEAD_DIM):
            xh = x[hd * HEAD_DIM:(hd + 1) * HEAD_DIM]
            pieces = []
            for lo in range(0, HEAD_DIM, 2 * part):
                pieces += [xh[lo + part:lo + 2 * part], xh[lo:lo + part]]
            out = xh * cos_t + jnp.concatenate(pieces, axis=0) * sin_t
            heads.append(out * mul if mul != 1.0 else out)
        return jnp.concatenate(heads, axis=0)

    def put(ref, val):
        for k in range(ref.shape[0]):
            ref[k] = val[:, k * ATT_BLOCK:(k + 1) * ATT_BLOCK].astype(ref.dtype)

    ut_ref[...] = chan(0)
    put(qat_ref, rope_t(chan(1), act_ref, ast_ref, HEAD_DIM // 4, scale * LOG2_E))
    put(vat_ref, chan(2))
    put(qrt_ref, rope_t(chan(3), rct_ref, rst_ref, HEAD_DIM // 2, 1.0))
    put(vrt_ref, chan(4))
    put(grt_ref, chan(5))

    p = _dot(h, w_ref[...])
    ac, asa, asb = ac_ref[...], asa_ref[...], asb_ref[...]
    rc, rsa, rsb = rc_ref[...], rsa_ref[...], rsb_ref[...]
    offs = np.concatenate([[0], np.cumsum(sizes)])

    def rope_cols(i, cos, sa, sb, half, mul):
        blk = p[:, int(offs[i]):int(offs[i + 1])]
        slabs = [_rope_slab(blk[:, s:s + LANES], cos, sa, sb, half) for s in range(0, blk.shape[1], LANES)]
        out = slabs[0] if len(slabs) == 1 else jnp.concatenate(slabs, axis=1)
        return out * mul if mul != 1.0 else out

    ka_ref[...] = rope_cols(0, ac, asa, asb, HEAD_DIM // 4, 1.0).astype(ka_ref.dtype)
    kr_ref[...] = rope_cols(1, rc, rsa, rsb, HEAD_DIM // 2, scale)


def _inproj(x, ctx, mod, w_in, tabs, sizes, n_ctx):
    bsz, d = x.shape[0], x.shape[2]
    s = x.shape[1] + (0 if ctx is None else ctx.shape[1])
    tm = _pick_chunk(s, INPROJ_BLOCK)
    stream_specs, stream_args = _stream_specs(x, ctx, tm)
    assert tm % TOKEN_BLOCK == 0, "the output projection reads the S5 arrays in TOKEN_BLOCK pieces"
    nrow = mod.shape[0]
    offs = np.concatenate([[0], np.cumsum(sizes)])
    col = lambda i: w_in[:, int(offs[i]):int(offs[i + 1])]
    c_idx, r_idx = (0, 1, 3, 4, 6, 7), (2, 5)
    c_dtypes = (BF16, BF16, F32, BF16, F32)
    r_dtypes = (BF16, F32)
    c_sizes = tuple(sizes[i] for i in c_idx)
    r_sizes = tuple(sizes[i] for i in r_idx)
    w_c_t = jnp.concatenate([col(i) for i in c_idx], axis=1).T
    w_rest = jnp.concatenate([col(i) for i in r_idx], axis=1)
    sub = tm // ATT_BLOCK
    tok = lambda width: pl.BlockSpec((None, tm, width), lambda b, j: (b, j, 0))
    chan = lambda width: pl.BlockSpec((sub, None, width, ATT_BLOCK), lambda b, j: (j, b, 0, 0))
    tab = pl.BlockSpec((tm, LANES), lambda b, j: (j, 0))
    tab_t = pl.BlockSpec((HEAD_DIM, tm), lambda b, j: (0, j))
    return pl.pallas_call(
        functools.partial(_inproj_kernel, n_stream=len(stream_specs), split=ctx is not None, n_ctx=n_ctx,
                          c_sizes=c_sizes, sizes=r_sizes),
        grid=(bsz, s // tm),
        in_specs=stream_specs + [
                  pl.BlockSpec((None, 6, d), lambda b, j: (b, 0, 0)),
                  pl.BlockSpec((None, 6, d), lambda b, j: (nrow - 1, 0, 0)),
                  pl.BlockSpec((sum(c_sizes), d), lambda b, j: (0, 0)),
                  pl.BlockSpec((d, sum(r_sizes)), lambda b, j: (0, 0)),
                  tab_t, tab_t, tab_t, tab_t, tab, tab, tab, tab, tab, tab],
        out_specs=[pl.BlockSpec((None, None, c_sizes[0], tm), lambda b, j: (j, b, 0, 0))]
                  + [chan(w) for w in c_sizes[1:]] + [tok(w) for w in r_sizes],
        out_shape=[jax.ShapeDtypeStruct((s // tm, bsz, c_sizes[0], tm), F32)]
                  + [jax.ShapeDtypeStruct((s // ATT_BLOCK, bsz, w, ATT_BLOCK), dt)
                     for w, dt in zip(c_sizes[1:], c_dtypes)]
                  + [jax.ShapeDtypeStruct((bsz, s, w), dt) for w, dt in zip(r_sizes, r_dtypes)],
        compiler_params=_cparams(("parallel", "parallel")),
        name="inproj",
    )(*stream_args, mod, mod, w_c_t, w_rest, *tabs)


def _rope_tables(t_len, n_ctx):
    t = jnp.arange(t_len)
    rows = (t // GRID_W).astype(F32)
    cols = (t % GRID_W).astype(F32)
    pos = t.astype(F32)

    def angles(p, dim):
        inv_freq = ROPE_BASE ** (-jnp.arange(0, dim, 2, dtype=F32) / dim)
        return p[:, None] * inv_freq[None, :]

    def head_tables(angs):
        cos = jnp.concatenate([jnp.concatenate([jnp.cos(a), jnp.cos(a)], -1) for a in angs], -1)
        sa = jnp.concatenate([jnp.concatenate([-jnp.sin(a), jnp.zeros_like(a)], -1) for a in angs], -1)
        sb = jnp.concatenate([jnp.concatenate([jnp.zeros_like(a), jnp.sin(a)], -1) for a in angs], -1)
        rep = LANES // HEAD_DIM
        return tuple(jnp.tile(x, (1, rep)) for x in (cos, sa, sb))

    att = head_tables([angles(rows, HEAD_DIM // 2), angles(cols, HEAD_DIM // 2)])
    ret = head_tables([angles(pos, HEAD_DIM)])
    signed_t = lambda tb: (tb[0][:, :HEAD_DIM].T, (tb[1] + tb[2])[:, :HEAD_DIM].T)

    def with_ctx(tb, axis, is_cos):
        shape = list(tb.shape)
        shape[axis] = n_ctx
        lead = jnp.ones(shape, F32) if is_cos else jnp.zeros(shape, F32)
        return jnp.concatenate([lead, tb], axis=axis)

    chan = signed_t(att) + signed_t(ret)
    toks = att + ret
    return (tuple(with_ctx(tb, 1, k % 2 == 0) for k, tb in enumerate(chan))
            + tuple(with_ctx(tb, 0, k % 3 == 0) for k, tb in enumerate(toks)))


def _s5_kernel(u_hbm, tab_ref, wsf_ref, wsb_ref, wrf_ref, wrb_ref, lam_ref, y_ref,
               lhs_ref, sem, m_ref, acc_ref, cf_ref, cb_ref, pf_ref, pb_ref, *, bsz, nz_chunks):
    r_blk, ch, tm = y_ref.shape
    c = S5_CHUNK
    kpb = tm // c
    n_chunks = (r_blk // bsz) * kpb
    g = pl.program_id(0)
    slot = g % 2

    def fetch(group, s):
        for i in range(ch):
            p, half = divmod(i, 2)
            for k in range(kpb):
                pltpu.make_async_copy(u_hbm.at[:, group * ch + i, pl.ds(k * c, c)],
                                      lhs_ref.at[s, p, pl.ds(k * r_blk, r_blk), pl.ds(half * c, c)],
                                      sem.at[s]).start()

    @pl.when(g == 0)
    def _():
        fetch(0, 0)

    @pl.when(g + 1 < pl.num_programs(0))
    def _():
        fetch(g + 1, 1 - slot)

    pltpu.make_async_copy(lhs_ref.at[1 - slot], lhs_ref.at[slot], sem.at[slot]).wait()
    acc_ref[...] = jnp.zeros_like(acc_ref)
    cf_ref[...] = jnp.zeros_like(cf_ref)
    cb_ref[...] = jnp.zeros_like(cb_ref)

    def pair(p, carry):
        for ih in range(2):
            for o in range(ch):
                lag_row = jnp.broadcast_to(tab_ref[2 * p + ih, o:o + 1, :], (16, 2 * c))
                tab = pltpu.roll(lag_row, 0, 1, stride=1, stride_axis=0).astype(BF16)
                for r2 in range(c // 16):
                    lo = c - 16 * r2
                    m_ref[ih * c + 16 * r2:ih * c + 16 * r2 + 16, o * c:(o + 1) * c] = tab[:, lo:lo + c]
        lhs = lhs_ref[slot, p].astype(BF16)
        acc_ref[...] += _dot(lhs, m_ref[...])
        cf_ref[...] += _dot(lhs, wsf_ref[p])
        cb_ref[...] += _dot(lhs, wsb_ref[p])
        return carry

    lax.fori_loop(0, ch // 2, pair, 0)
    lam = lam_ref[...]
    nst = lam.shape[1] // 2

    def advance(s, a, bc, drive):
        return s * a + pltpu.roll(s, nst, 1) * bc + drive

    def sweep(order, c_ref, p_ref, a, bc):
        s = jnp.zeros((bsz, lam.shape[1]), F32)
        for n in order:
            j, k = divmod(n, kpb)
            rows = slice(k * r_blk + j * bsz, k * r_blk + (j + 1) * bsz)
            p_ref[rows, :] = s
            s = advance(s, a, bc, c_ref[rows, :])

    sweep(list(range(n_chunks)), cf_ref, pf_ref, lam[0:1], lam[1:2])
    order_b = list(range(nz_chunks - 1, -1, -1)) + list(range(n_chunks - 1, nz_chunks - 1, -1))
    sweep(order_b, cb_ref, pb_ref, lam[2:3], lam[3:4])
    acc_ref[...] += (_dot(pf_ref[...].astype(BF16), wrf_ref[...])
                     + _dot(pb_ref[...].astype(BF16), wrb_ref[...]))
    for o in range(ch):
        for k in range(kpb):
            y_ref[:, o, k * c:(k + 1) * c] = acc_ref[k * r_blk:(k + 1) * r_blk, o * c:(o + 1) * c]


def _s5_weights(lam_re, lam_im, log_step, b_re, b_im, c_re, c_im, chunk):
    hp = lax.Precision.HIGHEST
    lam = lax.complex(lam_re.astype(F32), lam_im.astype(F32))
    lam_dt = lam * jnp.exp(log_step.astype(F32))[..., None]
    lam_bar = jnp.exp(lam_dt)
    b_bar = lax.complex(b_re.astype(F32), b_im.astype(F32)) * ((lam_bar - 1.0) / lam)[..., None]
    c_mat = lax.complex(c_re.astype(F32), c_im.astype(F32))
    g, n = lam.shape[1], lam.shape[2]
    ch = b_bar.shape[-1]
    steps = jnp.arange(chunk + 1, dtype=F32)
    pw = jnp.exp(steps[None, :, None, None] * lam_dt[:, None])
    kern = jnp.einsum('zgon,zdgn,zgni->zgdoi', c_mat, pw[:, :chunk], b_bar, precision=hp).real
    zero_lag = kern[0, :, 0] + kern[1, :, 0]
    lag_table = jnp.concatenate([jnp.zeros_like(zero_lag)[:, None], kern[1, :, :0:-1], zero_lag[:, None],
                                 kern[0, :, 1:]], axis=1).transpose(0, 3, 2, 1)

    def state_in(pw_s, b_dir):
        w = pw_s[:, :, :, None] * b_dir[None]
        w = jnp.concatenate([w.real, w.imag], axis=2)
        return w.transpose(1, 3, 0, 2).reshape(g, ch // 2, 2 * chunk, 2 * n)

    def state_out(pw_t, c_dir):
        w = c_dir[None] * pw_t[:, :, None, :]
        w = jnp.concatenate([w.real, -w.imag], axis=3)
        return w.transpose(1, 3, 2, 0).reshape(g, 2 * n, ch * chunk)

    wsf = state_in(pw[0, chunk - 1::-1][:chunk], b_bar[0])
    wsb = state_in(pw[1, :chunk], b_bar[1])
    wrf = state_out(pw[0, 1:chunk + 1], c_mat[0])
    wrb = state_out(pw[1, chunk:0:-1], c_mat[1])
    lam_c = pw[:, chunk]
    rows = []
    for z in range(2):
        rows.append(jnp.concatenate([lam_c[z].real, lam_c[z].real], -1))
        rows.append(jnp.concatenate([-lam_c[z].imag, lam_c[z].imag], -1))
    lam_rows = jnp.stack(rows + [jnp.zeros_like(rows[0])] * 4, axis=1)
    return lag_table, wsf.astype(BF16), wsb.astype(BF16), wrf.astype(BF16), wrb.astype(BF16), lam_rows


def _s5_scan(ut, weights, n_ctx):
    nblk, bsz, width, tm = ut.shape
    tab, wsf, wsb, wrf, wrb, lam_rows = weights
    g, ch = tab.shape[0], tab.shape[1]
    c = S5_CHUNK
    nst2 = wsf.shape[3]
    r_blk = nblk * bsz
    rows = r_blk * (tm // c)
    grp = lambda *shape: pl.BlockSpec((None,) + shape, lambda i: (i,) + (0,) * len(shape))
    tok = pl.BlockSpec((r_blk, ch, tm), lambda i: (0, i, 0))
    y = pl.pallas_call(
        functools.partial(_s5_kernel, bsz=bsz, nz_chunks=n_ctx // c),
        grid=(g,),
        in_specs=[pl.BlockSpec(memory_space=pl.ANY), grp(ch, ch, 2 * c), grp(ch // 2, 2 * c, nst2), grp(ch // 2, 2 * c, nst2),
                  grp(nst2, ch * c), grp(nst2, ch * c), grp(8, nst2)],
        out_specs=tok,
        out_shape=jax.ShapeDtypeStruct((r_blk, width, tm), F32),
        scratch_shapes=[pltpu.VMEM((2, ch // 2, rows, 2 * c), F32), pltpu.SemaphoreType.DMA((2,)),
                        pltpu.VMEM((2 * c, ch * c), BF16),
                        pltpu.VMEM((rows, ch * c), F32)] + [pltpu.VMEM((rows, nst2), F32)] * 4,
        compiler_params=_cparams(("arbitrary",)),
        name="s5_scan",
    )(ut.reshape(r_blk, width, tm), tab, wsf, wsb, wrf, wrb, lam_rows)
    return y.reshape(nblk, bsz, width, tm)


def _attn_kernel(sink_ref, q_ref, k_ref, v_ref, o_ref, s_ref, e_ref, *, n_ctx, t_len, q_off, kv_heads):
    qi = pl.program_id(1) + q_off
    nzb = n_ctx // ATT_BLOCK
    band = 3 * ATT_BLOCK
    wide = ATT_REP * ATT_BLOCK
    q = q_ref[...].astype(BF16)
    kc = k_ref[0:n_ctx, :].astype(BF16)
    vc = jnp.concatenate([v_ref[p] for p in range(nzb)], axis=1).astype(BF16)

    def rows(a, i):
        return a[i * HEAD_DIM:(i + 1) * HEAD_DIM]

    def q_group(g):
        qg = jnp.concatenate([rows(q, g * ATT_REP + r) for r in range(ATT_REP)], axis=1)
        zero = jnp.zeros_like(qg)
        return jnp.concatenate([qg if gg == g else zero for gg in range(kv_heads)], axis=0)

    ones_rows = 16

    def v_aug(v, g):
        return jnp.concatenate([rows(v, g), jnp.ones((ones_rows, v.shape[1]), BF16)], axis=0)

    def finish(g, o_aug, extra_den):
        o = o_aug[:HEAD_DIM] / (o_aug[HEAD_DIM:HEAD_DIM + 1] + extra_den)
        for r in range(ATT_REP):
            h = g * ATT_REP + r
            o_ref[h * HEAD_DIM:(h + 1) * HEAD_DIM, :] = o[:, r * ATT_BLOCK:(r + 1) * ATT_BLOCK].astype(o_ref.dtype)

    @pl.when(qi < nzb)
    def _():
        for g in range(kv_heads):
            sink = sink_ref[g:g + 1, :]
            s = _dot(kc, q_group(g))
            m = jnp.maximum(jnp.max(s, axis=0, keepdims=True), sink)
            e = jnp.exp2(s - m)
            finish(g, _dot(v_aug(vc, g), e.astype(BF16)), jnp.exp2(sink - m))

    @pl.when(qi >= nzb)
    def _():
        n = qi - nzb
        start = jnp.clip((n - 1) * ATT_BLOCK, 0, t_len - band)
        kl = k_ref[pl.ds(pl.multiple_of(n_ctx + start, ATT_BLOCK), band), :].astype(BF16)
        p0 = (n_ctx + start) // ATT_BLOCK
        vl = jnp.concatenate([v_ref[p0 + t] for t in range(3)], axis=1).astype(BF16)
        k_pos = start + lax.broadcasted_iota(jnp.int32, (band, wide), 0)
        q_pos = n * ATT_BLOCK + (lax.broadcasted_iota(jnp.int32, (band, wide), 1) & (ATT_BLOCK - 1))
        valid = jnp.abs(k_pos - q_pos) <= WINDOW
        v_all = jnp.concatenate([vl, vc], axis=1)
        n_keys = band + n_ctx
        row_max = []
        for g in range(kv_heads):
            qg = q_group(g)
            m = sink_ref[g:g + 1, :]
            for j in range(0, n_keys, ATT_BLOCK):
                if j < band:
                    s = jnp.where(valid[j:j + ATT_BLOCK], _dot(kl[j:j + ATT_BLOCK], qg), NEG_INF)
                else:
                    s = _dot(kc[j - band:j - band + ATT_BLOCK], qg)
                s_ref[g, j:j + ATT_BLOCK, :] = s
                m = jnp.maximum(m, jnp.max(s, axis=0, keepdims=True))
            row_max.append(m)
        for g in range(kv_heads):
            m = row_max[g]
            for j in range(0, n_keys, ATT_BLOCK):
                e_ref[g, j:j + ATT_BLOCK, :] = jnp.exp2(s_ref[g, j:j + ATT_BLOCK, :] - m).astype(BF16)
            finish(g, _dot(v_aug(v_all, g), e_ref[g]), jnp.exp2(sink_ref[g:g + 1, :] - m))


def _attention(qat, ka, vat, sink, n_ctx, need_ctx):
    nblk, bsz, qw, _ = qat.shape
    s, kvw = ka.shape[1], ka.shape[2]
    t_len = s - n_ctx
    kv_heads = kvw // HEAD_DIM
    q_off = 0 if need_ctx else n_ctx // ATT_BLOCK
    nq = nblk - q_off
    sink_rows = jnp.repeat(sink.astype(F32).reshape(kv_heads, ATT_REP), ATT_BLOCK, axis=1) * LOG2_E
    return pl.pallas_call(
        functools.partial(_attn_kernel, n_ctx=n_ctx, t_len=t_len, q_off=q_off, kv_heads=kv_heads),
        grid=(bsz, nq),
        in_specs=[pl.BlockSpec(sink_rows.shape, lambda b, j: (0, 0)),
                  pl.BlockSpec((None, None, qw, ATT_BLOCK), lambda b, j: (j + q_off, b, 0, 0)),
                  pl.BlockSpec((None, s, kvw), lambda b, j: (b, 0, 0)),
                  pl.BlockSpec((nblk, None, kvw, ATT_BLOCK), lambda b, j: (0, b, 0, 0))],
        out_specs=pl.BlockSpec((None, None, qw, ATT_BLOCK), lambda b, j: (j, b, 0, 0)),
        out_shape=jax.ShapeDtypeStruct((nq, bsz, qw, ATT_BLOCK), BF16),
        scratch_shapes=[pltpu.VMEM((kv_heads, 3 * ATT_BLOCK + n_ctx, ATT_REP * ATT_BLOCK), F32),
                        pltpu.VMEM((kv_heads, 3 * ATT_BLOCK + n_ctx, ATT_REP * ATT_BLOCK), BF16)],
        compiler_params=_cparams(("parallel", "arbitrary")),
        name="window_attention",
    )(sink_rows, qat, ka, vat)


def _ret_kernel(lg_ref, q_ref, k_ref, v_ref, g_ref, o_ref, acc_ref, sf_ref, sb_ref, *, nz_chunks, heads):
    c = RET_CHUNK
    n_chunks, w, _ = q_ref.shape
    f32 = lambda a: a.astype(F32)

    def per_head(shape, axis, group, direction):
        owner = lax.broadcasted_iota(jnp.int32, shape, axis) // group
        out = jnp.zeros(shape, F32)
        for h in range(heads):
            out = jnp.where(owner == h, lg_ref[direction, h], out)
        return out

    row_i = f32(lax.broadcasted_iota(jnp.int32, (w, c), 1))
    qw_f = jnp.exp(per_head((w, c), 0, HEAD_DIM, 0) * (row_i + 1.0))
    qw_b = jnp.exp(per_head((w, c), 0, HEAD_DIM, 1) * (c - row_i))
    key_j = f32(lax.broadcasted_iota(jnp.int32, (c, w), 0))
    kw_f = jnp.exp(per_head((c, w), 1, HEAD_DIM, 0) * (c - 1.0 - key_j))
    kw_b = jnp.exp(per_head((c, w), 1, HEAD_DIM, 1) * key_j)
    dec_f = jnp.exp(per_head((w, 1), 0, HEAD_DIM, 0) * c)
    dec_b = jnp.exp(per_head((w, 1), 0, HEAD_DIM, 1) * c)
    wide = heads * c
    diff = f32((lax.broadcasted_iota(jnp.int32, (c, wide), 1) & (c - 1)) - lax.broadcasted_iota(jnp.int32, (c, wide), 0))
    dec_t = jnp.where(diff >= 0, jnp.exp(per_head((c, wide), 1, c, 0) * jnp.maximum(diff, 0.0)),
                      jnp.exp(per_head((c, wide), 1, c, 1) * jnp.maximum(-diff, 0.0)))
    own_wide = (lax.broadcasted_iota(jnp.int32, (w, wide), 0) // HEAD_DIM
                == lax.broadcasted_iota(jnp.int32, (w, wide), 1) // c)
    own_sq = f32(lax.broadcasted_iota(jnp.int32, (w, w), 0) // HEAD_DIM
                 == lax.broadcasted_iota(jnp.int32, (w, w), 1) // HEAD_DIM)
    sf_ref[...] = jnp.zeros_like(sf_ref)
    sb_ref[...] = jnp.zeros_like(sb_ref)

    def block_diag(x):
        return jnp.where(own_wide, jnp.concatenate([x] * heads, axis=1), 0.0).astype(BF16)

    def fwd(n, carry):
        qt, vt = q_ref[n], v_ref[n]
        k = k_ref[pl.ds(pl.multiple_of(n * c, c), c), :]
        scores_t = _dot(k.astype(BF16), block_diag(qt)) * dec_t
        stacked = jnp.concatenate([scores_t[:, h * c:(h + 1) * c] for h in range(heads)], axis=0)
        o = _dot(block_diag(vt), stacked.astype(BF16))
        s_prev = sf_ref[...]
        o = o + _dot(s_prev.astype(BF16), (qt * qw_f).astype(BF16))
        sf_ref[...] = dec_f * s_prev + own_sq * _dot(vt.astype(BF16), (k * kw_f).astype(BF16))
        acc_ref[n] = o
        return carry

    lax.fori_loop(0, n_chunks, fwd, 0, unroll=6)

    def bwd(i, carry):
        n = jnp.where(i < nz_chunks, nz_chunks - 1 - i, n_chunks - 1 - i + nz_chunks)
        qt, vt = q_ref[n], v_ref[n]
        k = k_ref[pl.ds(pl.multiple_of(n * c, c), c), :]
        s_prev = sb_ref[...]
        o = acc_ref[n] + _dot(s_prev.astype(BF16), (qt * qw_b).astype(BF16))
        sb_ref[...] = dec_b * s_prev + own_sq * _dot(vt.astype(BF16), (k * kw_b).astype(BF16))
        gate = _silu(g_ref[n])
        for h in range(heads):
            rows = slice(h * HEAD_DIM, (h + 1) * HEAD_DIM)
            oh = o[rows]
            mu = jnp.mean(oh, axis=0, keepdims=True)
            oc = oh - mu
            var = jnp.mean(oc * oc, axis=0, keepdims=True)
            o_ref[n, rows, :] = (oc * lax.rsqrt(var + LN_EPS) * gate[rows]).astype(o_ref.dtype)
        return carry

    lax.fori_loop(0, n_chunks, bwd, 0, unroll=6)


def _retention(qrt, kr, vrt, grt, log_gamma, n_ctx):
    n_chunks, bsz, w, c = qrt.shape
    s = kr.shape[1]
    chan = pl.BlockSpec((n_chunks, None, w, c), lambda b: (0, b, 0, 0))
    return pl.pallas_call(
        functools.partial(_ret_kernel, nz_chunks=n_ctx // RET_CHUNK, heads=w // HEAD_DIM),
        grid=(bsz,),
        in_specs=[pl.BlockSpec(memory_space=pltpu.SMEM), chan, pl.BlockSpec((None, s, w), lambda b: (b, 0, 0)),
                  chan, chan],
        out_specs=chan,
        out_shape=jax.ShapeDtypeStruct((n_chunks, bsz, w, c), BF16),
        scratch_shapes=[pltpu.VMEM((n_chunks, w, c), F32), pltpu.VMEM((w, w), F32), pltpu.VMEM((w, w), F32)],
        compiler_params=_cparams(("parallel",)),
        name="retention",
    )(log_gamma.astype(F32), qrt, kr, vrt, grt)


def _outproj_kernel(*refs, n_stream, split, first_block, n_ctx, nzb, alpha, with_router, w_s5, w_att, n_exp):
    if with_router:
        (y_ref, u_ref, a_ref, r_ref, modx_ref, modz_ref, d_ref, wg_ref, bg_ref, wo_ref,
         g1_ref, b1_ref, rt_ref, x1_ref, fx_ref, gate_ref) = refs[n_stream:]
    else:
        (y_ref, u_ref, a_ref, r_ref, modx_ref, modz_ref, d_ref, wg_ref, bg_ref, wo_ref,
         g1_ref, b1_ref, x1_ref, fx_ref) = refs[n_stream:]
    (x_in,) = _stream_pieces(refs[:n_stream], pl.program_id(1) + first_block, n_ctx, split)
    is_ctx = pl.program_id(1) < nzb
    mod = jnp.where(is_ctx, modz_ref[...], modx_ref[...])
    g = _gelu_tanh(y_ref[...] + d_ref[...] * u_ref[...])
    s5 = g * _sigmoid(_dot(wg_ref[...], g.astype(BF16)) + bg_ref[...])
    def chan_major(ref, lo, hi):
        return jnp.concatenate([_dot_tn(ref[k].astype(BF16), wo_ref[lo:hi, :]) for k in range(ref.shape[0])],
                               axis=0)

    mix = (_dot_tn(s5.astype(BF16), wo_ref[0:w_s5, :]) + chan_major(a_ref, w_s5, w_s5 + w_att)
           + chan_major(r_ref, w_s5 + w_att, wo_ref.shape[0]))
    x1 = _ln(alpha * x_in + mod[2:3] * mix) * g1_ref[...] + b1_ref[...]
    x1_ref[...] = x1
    fx = _ln(x1) * (1.0 + mod[4:5]) + mod[3:4]
    fx_ref[...] = fx.astype(fx_ref.dtype)
    if with_router:
        lane = lax.broadcasted_iota(jnp.int32, (fx.shape[0], LANES), 1)
        logits = jnp.where(lane < n_exp, _dot3(fx, rt_ref[...]), -jnp.inf)
        m1 = jnp.max(logits, axis=1, keepdims=True)
        i1 = jnp.min(jnp.where(logits == m1, lane, LANES), axis=1, keepdims=True)
        rest = jnp.where(lane == i1, -jnp.inf, logits)
        m2 = jnp.max(rest, axis=1, keepdims=True)
        i2 = jnp.min(jnp.where(rest == m2, lane, LANES), axis=1, keepdims=True)
        e2 = jnp.exp(m2 - m1)
        den = 1.0 + e2
        route = jnp.where(lane == 0, 1.0 / den, jnp.where(lane == 1, e2 / den, 0.0))
        route = jnp.where(lane == 2, i1.astype(F32), jnp.where(lane == 3, i2.astype(F32), route))
        gate_ref[...] = route


def _outproj(y_s5, u, o_att, o_ret, x, ctx, mod, s5_d, w_glu, b_glu, w_out, ln_g, ln_b, router, n_ctx,
             need_ctx, alpha):
    bsz, d = x.shape[0], x.shape[2]
    s = x.shape[1] + (0 if ctx is None else ctx.shape[1])
    tm = TOKEN_BLOCK
    nzb = n_ctx // tm
    off = 0 if need_ctx else nzb
    nblk = s // tm - off
    s_out = nblk * tm
    nrow = mod.shape[0]
    w_s5, w_att = y_s5.shape[2], o_att.shape[2]
    sub = tm // ATT_BLOCK
    att_off = off if o_att.shape[0] == s // ATT_BLOCK else 0
    att = pl.BlockSpec((sub, None, w_att, ATT_BLOCK), lambda b, j: (j + att_off, b, 0, 0))

    def tok(width, shift):
        return pl.BlockSpec((None, tm, width), lambda b, j: (b, j + shift, 0))

    def full(a):
        return pl.BlockSpec(a.shape, lambda b, j: (0,) * a.ndim)

    per_in = y_s5.shape[3] // tm
    chan = pl.BlockSpec((None, None, w_s5, tm), lambda b, j: ((j + off) // per_in, b, 0, (j + off) % per_in))
    vec = lambda a: a.reshape(1, -1).astype(F32)
    col = lambda a: a.reshape(-1, 1).astype(F32)
    consts = [col(s5_d), w_glu.T.astype(BF16), col(b_glu), w_out.astype(BF16), vec(ln_g), vec(ln_b)]
    ret = pl.BlockSpec((sub, None, o_ret.shape[2], ATT_BLOCK), lambda b, j: (j + off, b, 0, 0))
    stream_specs, stream_args = _stream_specs(x, ctx, tm, first_block=off)
    in_specs = stream_specs + [chan, chan, att, ret,
                pl.BlockSpec((None, 6, d), lambda b, j: (b, 0, 0)),
                pl.BlockSpec((None, 6, d), lambda b, j: (nrow - 1, 0, 0))] + [full(a) for a in consts]
    out_specs = [tok(d, 0), tok(d, 0)]
    fx_dtype = BF16 if router is None else F32
    out_shape = [jax.ShapeDtypeStruct((bsz, s_out, d), F32), jax.ShapeDtypeStruct((bsz, s_out, d), fx_dtype)]
    args = stream_args + [y_s5, u, o_att, o_ret, mod, mod] + consts
    with_router = router is not None
    n_exp = 0
    if with_router:
        n_exp = router.shape[1]
        router_pad = jnp.pad(router.astype(F32), ((0, 0), (0, LANES - n_exp)))
        args.append(router_pad)
        in_specs.append(full(router_pad))
        out_specs.append(tok(LANES, 0))
        out_shape.append(jax.ShapeDtypeStruct((bsz, s_out, LANES), F32))
    return pl.pallas_call(
        functools.partial(_outproj_kernel, n_stream=len(stream_specs), split=ctx is not None, first_block=off,
                          n_ctx=n_ctx, nzb=nzb - off, alpha=alpha, with_router=with_router,
                          w_s5=w_s5, w_att=w_att, n_exp=n_exp),
        grid=(bsz, nblk),
        in_specs=in_specs,
        out_specs=out_specs,
        out_shape=out_shape,
        compiler_params=_cparams(("parallel", "parallel")),
        name="outproj",
    )(*args)


def _swiglu_into(x_bf16, w1_ref, w3_ref, w2_ref, acc_ref, fc):
    for s in range(0, w1_ref.shape[1], fc):
        h1 = _dot(x_bf16, w1_ref[:, s:s + fc])
        h3 = _dot(x_bf16, w3_ref[:, s:s + fc])
        acc_ref[...] += _dot((_silu(h1) * h3).astype(BF16), w2_ref[s:s + fc, :])


def _ffn_kernel(fx_ref, x1_ref, modx_ref, modz_ref, w1_ref, w3_ref, w2_ref, g2_ref, b2_ref,
                o_ref, acc_ref, *, n_ctx_tokens, alpha, fc):
    tm = acc_ref.shape[0]
    acc_ref[...] = jnp.zeros_like(acc_ref)
    _swiglu_into(fx_ref[...], w1_ref, w3_ref, w2_ref, acc_ref, fc)
    row = pl.program_id(1) * tm + lax.broadcasted_iota(jnp.int32, (tm, 1), 0)
    gate = jnp.where(row < n_ctx_tokens, modz_ref[5:6, :], modx_ref[5:6, :])
    o_ref[...] = _ln(alpha * x1_ref[...] + gate * acc_ref[...]) * g2_ref[...] + b2_ref[...]


def _ffn(fx, x1, mod, w1, w3, w2, ln_g, ln_b, n_ctx_tokens, alpha):
    bsz, s, d = x1.shape
    ff = w1.shape[1]
    tm = _pick_chunk(s, FFN_BLOCK)
    nrow = mod.shape[0]
    tok = pl.BlockSpec((None, tm, d), lambda b, j: (b, j, 0))
    full = lambda a: pl.BlockSpec(a.shape, lambda b, j: (0,) * a.ndim, pipeline_mode=pl.Buffered(1))
    vec = lambda a: a.reshape(1, -1).astype(F32)
    return pl.pallas_call(
        functools.partial(_ffn_kernel, n_ctx_tokens=n_ctx_tokens, alpha=alpha, fc=_pick_chunk(ff, FFN_CHUNK)),
        grid=(bsz, s // tm),
        in_specs=[tok, tok,
                  pl.BlockSpec((None, 6, d), lambda b, j: (b, 0, 0)),
                  pl.BlockSpec((None, 6, d), lambda b, j: (nrow - 1, 0, 0)),
                  full(w1), full(w3), full(w2),
                  pl.BlockSpec((1, d), lambda b, j: (0, 0)), pl.BlockSpec((1, d), lambda b, j: (0, 0))],
        out_specs=tok,
        out_shape=jax.ShapeDtypeStruct((bsz, s, d), F32),
        scratch_shapes=[pltpu.VMEM((tm, d), F32)],
        compiler_params=_cparams(("parallel", "parallel")),
        name="dense_ffn",
    )(fx, x1, mod, mod, w1, w3, w2, vec(ln_g), vec(ln_b))


MOE_TILE = 1024


def _route_plan(e1, e2, n_exp, tile):
    n = e1.shape[0]
    pair_e = jnp.stack([e1, e2], axis=1).reshape(-1)
    onehot = (pair_e[:, None] == jnp.arange(n_exp, dtype=jnp.int32)[None, :]).astype(jnp.int32)
    before = jnp.cumsum(onehot, axis=0) - onehot
    rank = jnp.sum(before * onehot, axis=1)
    counts = jnp.sum(onehot, axis=0)
    padded = (counts + tile - 1) // tile * tile
    ends = jnp.cumsum(padded)
    starts = ends - padded
    dest = starts[pair_e] + rank
    n_rows = (2 * n + n_exp * (tile - 1)) // tile * tile
    n_tiles = n_rows // tile
    tile_start = jnp.arange(n_tiles, dtype=jnp.int32) * tile
    tile_expert = jnp.minimum(jnp.sum((tile_start[:, None] >= ends[None, :]).astype(jnp.int32), axis=1),
                              n_exp - 1)
    by_expert = jnp.sort(pair_e * (2 * n) + jnp.arange(2 * n, dtype=jnp.int32)) % (2 * n)
    row = jnp.arange(n_rows, dtype=jnp.int32)
    row_e = jnp.repeat(tile_expert, tile)
    in_group = row - starts[row_e]
    src = jnp.minimum(in_group + (jnp.cumsum(counts) - counts)[row_e], 2 * n - 1)
    row_token = jnp.where(in_group < counts[row_e], by_expert[src] // 2, 0)
    n_used = (ends[-1] // tile).astype(jnp.int32).reshape(1)
    return row_token, dest.reshape(n, 2), tile_expert, n_used


def _moe_gemm_kernel(te_ref, nused_ref, tok_ref, tokn_ref, x_hbm, w1_ref, w3_ref, w2_ref, y_ref,
                     xbuf_ref, sem, *, fc, tile):
    i = pl.program_id(0)
    f = pl.program_id(1)
    slot = i % 2
    n_used = nused_ref[0]

    def row_copy(t_ref, r, s):
        return pltpu.make_async_copy(x_hbm.at[pl.ds(t_ref[0, r], 1)], xbuf_ref.at[s, pl.ds(r, 1)], sem.at[s])

    def issue(t_ref, s):
        def body(it, carry):
            base = pl.multiple_of(it * 8, 8)
            for k in range(8):
                row_copy(t_ref, base + k, s).start()
            return carry
        lax.fori_loop(0, tile // 8, body, 0)

    @pl.when(f == 0)
    def _():
        y_ref[...] = jnp.zeros_like(y_ref)

        @pl.when(i == 0)
        def _():
            issue(tok_ref, 0)

        @pl.when(i + 1 < n_used)
        def _():
            issue(tokn_ref, 1 - slot)

        @pl.when(i < n_used)
        def _():
            pltpu.make_async_copy(x_hbm.at[pl.ds(0, tile)], xbuf_ref.at[slot], sem.at[slot]).wait()

    @pl.when(i < n_used)
    def _():
        _swiglu_into(xbuf_ref[slot].astype(BF16), w1_ref, w3_ref, w2_ref, y_ref, fc)


def _moe_gemm(x_flat, row_token, tile_expert, n_used, w1, w3, w2, tile):
    n_rows = row_token.shape[0]
    n_tiles = n_rows // tile
    d = x_flat.shape[1]
    ff = w1.shape[2]
    nf = 2 if ff % (2 * LANES) == 0 else 1
    tf = ff // nf
    last = lambda i, nu: jnp.minimum(i, nu[0] - 1)
    fsel = lambda i, f, nu: jnp.where(i < nu[0], f, nf - 1)
    tok = lambda imap: pl.BlockSpec((None, 1, tile), imap, memory_space=pltpu.SMEM)
    tokens = row_token.reshape(n_tiles, 1, tile)
    return pl.pallas_call(
        functools.partial(_moe_gemm_kernel, fc=_pick_chunk(tf, MOE_CHUNK), tile=tile),
        grid_spec=pltpu.PrefetchScalarGridSpec(
            num_scalar_prefetch=2,
            grid=(n_tiles, nf),
            in_specs=[tok(lambda i, f, te, nu: (i, 0, 0)),
                      tok(lambda i, f, te, nu: (jnp.minimum(i + 1, n_tiles - 1), 0, 0)),
                      pl.BlockSpec(memory_space=pl.ANY),
                      pl.BlockSpec((None, d, tf), lambda i, f, te, nu: (te[last(i, nu)], 0, fsel(i, f, nu))),
                      pl.BlockSpec((None, d, tf), lambda i, f, te, nu: (te[last(i, nu)], 0, fsel(i, f, nu))),
                      pl.BlockSpec((None, tf, d), lambda i, f, te, nu: (te[last(i, nu)], fsel(i, f, nu), 0))],
            out_specs=pl.BlockSpec((tile, d), lambda i, f, te, nu: (i, 0)),
            scratch_shapes=[pltpu.VMEM((2, tile, d), F32), pltpu.SemaphoreType.DMA((2,))]),
        out_shape=jax.ShapeDtypeStruct((n_rows, d), F32),
        compiler_params=_cparams(("arbitrary", "arbitrary")),
        name="moe_gemm",
    )(tile_expert, n_used, tokens, tokens, x_flat, w1, w3, w2)


def _moe_combine_kernel(pos_ref, posn_ref, route_ref, x1_ref, modx_ref, modz_ref, g2_ref, b2_ref, y_hbm,
                        o_ref, buf_ref, sem, *, nzb, nblk, alpha, tm):
    i = pl.program_id(0)
    slot = i % 2

    def row_copy(p_ref, r, k, s):
        return pltpu.make_async_copy(y_hbm.at[pl.ds(p_ref[0, k * tm + r], 1)],
                                     buf_ref.at[s, pl.ds(k * tm + r, 1)], sem.at[s])

    def issue(p_ref, s):
        def body(r, carry):
            row_copy(p_ref, r, 0, s).start()
            row_copy(p_ref, r, 1, s).start()
            return carry
        lax.fori_loop(0, tm, body, 0, unroll=8)

    @pl.when(i == 0)
    def _():
        issue(pos_ref, 0)

    @pl.when(i + 1 < pl.num_programs(0))
    def _():
        issue(posn_ref, 1 - slot)

    pltpu.make_async_copy(y_hbm.at[pl.ds(0, 2 * tm)], buf_ref.at[slot], sem.at[slot]).wait()
    route = route_ref[...]
    f = route[:, 0:1] * buf_ref[slot, 0:tm] + route[:, 1:2] * buf_ref[slot, tm:2 * tm]
    is_ctx = (i % nblk) < nzb
    mod = jnp.where(is_ctx, modz_ref[...], modx_ref[...])
    o_ref[...] = _ln(alpha * x1_ref[...] + mod[5:6] * f) * g2_ref[...] + b2_ref[...]


def _moe_combine(ys, pos, route, x1, mod, ln_g, ln_b, n_ctx_tokens, alpha):
    bsz, s, d = x1.shape
    wide_ok = s % COMBINE_BLOCK == 0 and n_ctx_tokens % COMBINE_BLOCK == 0
    tm = COMBINE_BLOCK if wide_ok else TOKEN_BLOCK
    nblk = s // tm
    n_steps = bsz * nblk
    nrow = mod.shape[0]
    pos_steps = pos.reshape(n_steps, tm, 2).transpose(0, 2, 1).reshape(n_steps, 1, 2 * tm)
    tok = lambda width: pl.BlockSpec((tm, width), lambda i: (i, 0))
    smem = lambda imap: pl.BlockSpec((None, 1, 2 * tm), imap, memory_space=pltpu.SMEM)
    vec = lambda a: a.reshape(1, -1).astype(F32)
    out = pl.pallas_call(
        functools.partial(_moe_combine_kernel, nzb=n_ctx_tokens // tm, nblk=nblk, alpha=alpha, tm=tm),
        grid=(n_steps,),
        in_specs=[smem(lambda i: (i, 0, 0)),
                  smem(lambda i: (jnp.minimum(i + 1, n_steps - 1), 0, 0)),
                  tok(LANES), tok(d),
                  pl.BlockSpec((None, 6, d), lambda i: (i // nblk, 0, 0)),
                  pl.BlockSpec((None, 6, d), lambda i: (nrow - 1, 0, 0)),
                  pl.BlockSpec((1, d), lambda i: (0, 0)), pl.BlockSpec((1, d), lambda i: (0, 0)),
                  pl.BlockSpec(memory_space=pl.ANY)],
        out_specs=tok(d),
        out_shape=jax.ShapeDtypeStruct((bsz * s, d), F32),
        scratch_shapes=[pltpu.VMEM((2, 2 * tm, d), F32), pltpu.SemaphoreType.DMA((2,))],
        compiler_params=_cparams(("arbitrary",)),
        name="moe_combine",
    )(pos_steps, pos_steps, route.reshape(bsz * s, LANES), x1.reshape(bsz * s, d), mod, mod,
      vec(ln_g), vec(ln_b), ys)
    return out.reshape(bsz, s, d)


def _moe(fx, x1, route, mod, w1, w3, w2, ln_g, ln_b, n_ctx_tokens, alpha):
    bsz, s, d = x1.shape
    n_exp = w1.shape[0]
    idx = route.reshape(bsz * s, LANES)[:, 2:4].astype(jnp.int32)
    row_token, pos, tile_expert, n_used = _route_plan(idx[:, 0], idx[:, 1], n_exp, MOE_TILE)
    ys = _moe_gemm(fx.reshape(bsz * s, d), row_token, tile_expert, n_used, w1, w3, w2, MOE_TILE)
    return _moe_combine(ys, pos, route, x1, mod, ln_g, ln_b, n_ctx_tokens, alpha)


def kernel(x, c, ctx, c_ctx, w_mod, b_mod, w_in, s5_lam_re, s5_lam_im, s5_log_step, s5_b_re, s5_b_im,
           s5_c_re, s5_c_im, s5_d, s5_w_glu, s5_b_glu, attn_sink, ret_log_gamma, w_out,
           ln1_g, ln1_b, ln2_g, ln2_b, ffn_w1, ffn_w3, ffn_w2, moe_router, moe_w1, moe_w3, moe_w2):
    bsz, t_len, d = x.shape
    n_ctx = ctx.shape[1]
    depth = w_in.shape[0]
    alpha = (2 * depth) ** 0.25
    s5_w = s5_d.shape[1]
    att_w = attn_sink.shape[1] * HEAD_DIM
    kv_w = att_w // ATT_REP
    ret_w = ret_log_gamma.shape[2] * HEAD_DIM
    sizes = (s5_w, att_w, kv_w, kv_w, ret_w, ret_w, ret_w, ret_w)
    assert sum(sizes) == w_in.shape[2] and s5_w + att_w + ret_w == w_out.shape[1]
    assert n_ctx % TOKEN_BLOCK == 0 and t_len % TOKEN_BLOCK == 0 and t_len >= 3 * ATT_BLOCK

    pad = (-(bsz + 1)) % 8
    cvec = jnp.concatenate([jnp.zeros((pad, d), F32), c_ctx[None].astype(F32)], axis=0)
    cvec = jnp.concatenate([c.astype(F32), cvec], axis=0)
    mod_all = _modulation(cvec, w_mod.astype(F32), b_mod.astype(F32)).reshape(depth, bsz + pad + 1, 6, d)

    tabs = _rope_tables(t_len, n_ctx)
    xz, ctx_in = x.astype(F32), ctx.astype(F32)
    for l in range(depth):
        need_ctx = l < depth - 1
        mod = mod_all[l]
        u, qa, va, qr, vr, gr, ka, kr = _inproj(xz, ctx_in, mod, w_in[l].astype(BF16), tabs, sizes, n_ctx)
        s5w = _s5_weights(s5_lam_re[l], s5_lam_im[l], s5_log_step[l], s5_b_re[l], s5_b_im[l],
                          s5_c_re[l], s5_c_im[l], S5_CHUNK)
        y_s5 = _s5_scan(u, s5w, n_ctx)
        o_att = _attention(qa, ka, va, attn_sink[l], n_ctx, need_ctx)
        o_ret = _retention(qr, kr, vr, gr, ret_log_gamma[l], n_ctx)
        i = l // 2
        router = None if l % 2 == 0 else moe_router[i]
        outs = _outproj(y_s5, u, o_att, o_ret, xz, ctx_in, mod, s5_d[l], s5_w_glu[l], s5_b_glu[l], w_out[l],
                        ln1_g[l], ln1_b[l], router, n_ctx, need_ctx, alpha)
        ctx_in = None
        ctx_tokens = n_ctx if need_ctx else 0
        if l % 2 == 0:
            x1, fx = outs
            xz = _ffn(fx, x1, mod, ffn_w1[i].astype(BF16), ffn_w3[i].astype(BF16), ffn_w2[i].astype(BF16),
                      ln2_g[l], ln2_b[l], ctx_tokens, alpha)
        else:
            x1, fx, route = outs
            xz = _moe(fx, x1, route, mod, moe_w1[i].astype(BF16), moe_w3[i].astype(BF16),
                      moe_w2[i].astype(BF16), ln2_g[l], ln2_b[l], ctx_tokens, alpha)
    return xz if xz.shape[1] == t_len else xz[:, n_ctx:]
```

```python
import functools
import math

import jax
import jax.numpy as jnp
import numpy as np
from jax import lax
from jax.experimental import pallas as pl
from jax.experimental.pallas import tpu as pltpu

F32 = jnp.float32
BF16 = jnp.bfloat16

GRID_W = 64
HEAD_DIM = 64
ATT_REP = 4
WINDOW = 128
ATT_BLOCK = 128
RET_CHUNK = 128
LN_EPS = 1e-5
ROPE_BASE = 10000.0
NEG_INF = -1e30
LOG2_E = math.log2(math.e)

LANES = 128
S5_CHUNK = LANES
TOKEN_BLOCK = 256
INPROJ_BLOCK = 768
FFN_BLOCK = 768
FFN_CHUNK = 512
MOE_CHUNK = 256
COMBINE_BLOCK = 512
MOD_BLOCK = 1536
VMEM_LIMIT = 56 * 1024 * 1024


def _cparams(sem):
    return pltpu.CompilerParams(dimension_semantics=sem, vmem_limit_bytes=VMEM_LIMIT)


def _pick_chunk(total, target):
    best = LANES
    for c in range(LANES, target + 1, LANES):
        if total % c == 0:
            best = c
    return best


def _dot(a, b):
    return jnp.dot(a, b, preferred_element_type=F32)


def _dot_nt(a, b):
    return lax.dot_general(a, b, (((1,), (1,)), ((), ())), preferred_element_type=F32)


def _dot_tn(a, b):
    return lax.dot_general(a, b, (((0,), (0,)), ((), ())), preferred_element_type=F32)


def _split_bf16(a):
    hi = a.astype(BF16)
    lo = (a - hi.astype(F32)).astype(BF16)
    return hi, lo


def _dot3(a, b):
    ah, al = _split_bf16(a)
    bh, bl = _split_bf16(b)
    return _dot(ah, bh) + (_dot(ah, bl) + _dot(al, bh))


def _ln(x):
    mu = jnp.mean(x, axis=-1, keepdims=True)
    xc = x - mu
    var = jnp.mean(xc * xc, axis=-1, keepdims=True)
    return xc * lax.rsqrt(var + LN_EPS)


def _silu(x):
    return x * (1.0 / (1.0 + jnp.exp(-x)))


def _sigmoid(x):
    return 1.0 / (1.0 + jnp.exp(-x))


def _gelu_tanh(x):
    c = math.sqrt(2.0 / math.pi)
    return 0.5 * x * (1.0 + jnp.tanh(c * (x + 0.044715 * (x * x * x))))


def _mod_kernel(c_ref, w_ref, b_ref, o_ref):
    o_ref[...] = _dot3(_silu(c_ref[...]), w_ref[...]) + b_ref[...]


def _modulation(cvec, w_mod, b_mod):
    depth, d, n = w_mod.shape
    rows = cvec.shape[0]
    tn = _pick_chunk(n, MOD_BLOCK)
    return pl.pallas_call(
        _mod_kernel,
        grid=(depth, n // tn),
        in_specs=[pl.BlockSpec((rows, d), lambda l, j: (0, 0)),
                  pl.BlockSpec((None, d, tn), lambda l, j: (l, 0, j)),
                  pl.BlockSpec((None, 1, tn), lambda l, j: (l, 0, j))],
        out_specs=pl.BlockSpec((None, rows, tn), lambda l, j: (l, 0, j)),
        out_shape=jax.ShapeDtypeStruct((depth, rows, n), F32),
        compiler_params=_cparams(("parallel", "parallel")),
        name="modulation",
    )(cvec, w_mod, b_mod.reshape(depth, 1, n))


def _rope_slab(xs, cos, sa, sb, half):
    return xs * cos + pltpu.roll(xs, LANES - half, 1) * sa + pltpu.roll(xs, half, 1) * sb


def _stream_specs(x, ctx, tm, first_block=0):
    d = x.shape[2]
    pieces = tm // TOKEN_BLOCK
    piece = lambda imap: pl.BlockSpec((None, TOKEN_BLOCK, d), imap)
    specs, args = [], []
    for k in range(pieces):
        if ctx is None:
            specs.append(piece(lambda b, j, k=k: (b, (j + first_block) * pieces + k, 0)))
            args.append(x)
        else:
            nzb = ctx.shape[1] // TOKEN_BLOCK
            specs.append(piece(lambda b, j, k=k: (b, jnp.minimum((j + first_block) * pieces + k, nzb - 1), 0)))
            specs.append(piece(lambda b, j, k=k: (b, jnp.maximum((j + first_block) * pieces + k - nzb, 0), 0)))
            args += [ctx, x]
    return specs, args


def _stream_pieces(refs, block, n_ctx, split):
    if not split:
        return [r[...] for r in refs]
    pieces = len(refs) // 2
    return [jnp.where((block * pieces + k) * TOKEN_BLOCK < n_ctx, refs[2 * k][...], refs[2 * k + 1][...])
            for k in range(pieces)]


def _inproj_kernel(*refs, n_stream, split, n_ctx, c_sizes, sizes):
    (modx_ref, modz_ref, wc_ref, w_ref, act_ref, ast_ref, rct_ref, rst_ref,
     ac_ref, asa_ref, asb_ref, rc_ref, rsa_ref, rsb_ref,
     ut_ref, qat_ref, vat_ref, qrt_ref, vrt_ref, grt_ref, ka_ref, kr_ref) = refs[n_stream:]
    x_pieces = _stream_pieces(refs[:n_stream], pl.program_id(1), n_ctx, split)
    tm = len(x_pieces) * TOKEN_BLOCK
    is_ctx = pl.program_id(1) * tm + lax.broadcasted_iota(jnp.int32, (tm, 1), 0) < n_ctx
    shift = jnp.where(is_ctx, modz_ref[0:1, :], modx_ref[0:1, :])
    gain = 1.0 + jnp.where(is_ctx, modz_ref[1:2, :], modx_ref[1:2, :])
    h = jnp.concatenate(
        [(_ln(xp) * gain[k * TOKEN_BLOCK:(k + 1) * TOKEN_BLOCK] + shift[k * TOKEN_BLOCK:(k + 1) * TOKEN_BLOCK]
          ).astype(BF16) for k, xp in enumerate(x_pieces)], axis=0)
    scale = HEAD_DIM ** -0.5

    ct = _dot_nt(wc_ref[...], h)
    c_offs = [int(o) for o in np.concatenate([[0], np.cumsum(c_sizes)])]

    def chan(i):
        return ct[c_offs[i]:c_offs[i + 1]]

    def rope_t(x, cos_ref, sin_ref, part, mul):
        cos_t, sin_t = cos_ref[...], sin_ref[...]
        heads = []
        for hd in range(x.shape[0] // HEAD_DIM):
            xh = x[hd * HEAD_DIM:(hd + 1) * HEAD_DIM]
            pieces = []
            for lo in range(0, HEAD_DIM, 2 * part):
                pieces += [xh[lo + part:lo + 2 * part], xh[lo:lo + part]]
            out = xh * cos_t + jnp.concatenate(pieces, axis=0) * sin_t
            heads.append(out * mul if mul != 1.0 else out)
        return jnp.concatenate(heads, axis=0)

    def put(ref, val):
        for k in range(ref.shape[0]):
            ref[k] = val[:, k * ATT_BLOCK:(k + 1) * ATT_BLOCK].astype(ref.dtype)

    ut_ref[...] = chan(0)
    put(qat_ref, rope_t(chan(1), act_ref, ast_ref, HEAD_DIM // 4, scale * LOG2_E))
    put(vat_ref, chan(2))
    put(qrt_ref, rope_t(chan(3), rct_ref, rst_ref, HEAD_DIM // 2, 1.0))
    put(vrt_ref, chan(4))
    put(grt_ref, chan(5))

    p = _dot(h, w_ref[...])
    ac, asa, asb = ac_ref[...], asa_ref[...], asb_ref[...]
    rc, rsa, rsb = rc_ref[...], rsa_ref[...], rsb_ref[...]
    offs = np.concatenate([[0], np.cumsum(sizes)])

    def rope_cols(i, cos, sa, sb, half, mul):
        blk = p[:, int(offs[i]):int(offs[i + 1])]
        slabs = [_rope_slab(blk[:, s:s + LANES], cos, sa, sb, half) for s in range(0, blk.shape[1], LANES)]
        out = slabs[0] if len(slabs) == 1 else jnp.concatenate(slabs, axis=1)
        return out * mul if mul != 1.0 else out

    ka_ref[...] = rope_cols(0, ac, asa, asb, HEAD_DIM // 4, 1.0).astype(ka_ref.dtype)
    kr_ref[...] = rope_cols(1, rc, rsa, rsb, HEAD_DIM // 2, scale)


def _inproj(x, ctx, mod, w_in, tabs, sizes, n_ctx):
    bsz, d = x.shape[0], x.shape[2]
    s = x.shape[1] + (0 if ctx is None else ctx.shape[1])
    tm = _pick_chunk(s, INPROJ_BLOCK)
    stream_specs, stream_args = _stream_specs(x, ctx, tm)
    assert tm % TOKEN_BLOCK == 0, "the output projection reads the S5 arrays in TOKEN_BLOCK pieces"
    nrow = mod.shape[0]
    offs = np.concatenate([[0], np.cumsum(sizes)])
    col = lambda i: w_in[:, int(offs[i]):int(offs[i + 1])]
    c_idx, r_idx = (0, 1, 3, 4, 6, 7), (2, 5)
    c_dtypes = (BF16, BF16, F32, BF16, F32)
    r_dtypes = (BF16, F32)
    c_sizes = tuple(sizes[i] for i in c_idx)
    r_sizes = tuple(sizes[i] for i in r_idx)
    w_c_t = jnp.concatenate([col(i) for i in c_idx], axis=1).T
    w_rest = jnp.concatenate([col(i) for i in r_idx], axis=1)
    sub = tm // ATT_BLOCK
    tok = lambda width: pl.BlockSpec((None, tm, width), lambda b, j: (b, j, 0))
    chan = lambda width: pl.BlockSpec((sub, None, width, ATT_BLOCK), lambda b, j: (j, b, 0, 0))
    tab = pl.BlockSpec((tm, LANES), lambda b, j: (j, 0))
    tab_t = pl.BlockSpec((HEAD_DIM, tm), lambda b, j: (0, j))
    return pl.pallas_call(
        functools.partial(_inproj_kernel, n_stream=len(stream_specs), split=ctx is not None, n_ctx=n_ctx,
                          c_sizes=c_sizes, sizes=r_sizes),
        grid=(bsz, s // tm),
        in_specs=stream_specs + [
                  pl.BlockSpec((None, 6, d), lambda b, j: (b, 0, 0)),
                  pl.BlockSpec((None, 6, d), lambda b, j: (nrow - 1, 0, 0)),
                  pl.BlockSpec((sum(c_sizes), d), lambda b, j: (0, 0)),
                  pl.BlockSpec((d, sum(r_sizes)), lambda b, j: (0, 0)),
                  tab_t, tab_t, tab_t, tab_t, tab, tab, tab, tab, tab, tab],
        out_specs=[pl.BlockSpec((None, None, c_sizes[0], tm), lambda b, j: (j, b, 0, 0))]
                  + [chan(w) for w in c_sizes[1:]] + [tok(w) for w in r_sizes],
        out_shape=[jax.ShapeDtypeStruct((s // tm, bsz, c_sizes[0], tm), F32)]
                  + [jax.ShapeDtypeStruct((s // ATT_BLOCK, bsz, w, ATT_BLOCK), dt)
                     for w, dt in zip(c_sizes[1:], c_dtypes)]
                  + [jax.ShapeDtypeStruct((bsz, s, w), dt) for w, dt in zip(r_sizes, r_dtypes)],
        compiler_params=_cparams(("parallel", "parallel")),
        name="inproj",
    )(*stream_args, mod, mod, w_c_t, w_rest, *tabs)


def _rope_tables(t_len, n_ctx):
    t = jnp.arange(t_len)
    rows = (t // GRID_W).astype(F32)
    cols = (t % GRID_W).astype(F32)
    pos = t.astype(F32)

    def angles(p, dim):
        inv_freq = ROPE_BASE ** (-jnp.arange(0, dim, 2, dtype=F32) / dim)
        return p[:, None] * inv_freq[None, :]

    def head_tables(angs):
        cos = jnp.concatenate([jnp.concatenate([jnp.cos(a), jnp.cos(a)], -1) for a in angs], -1)
        sa = jnp.concatenate([jnp.concatenate([-jnp.sin(a), jnp.zeros_like(a)], -1) for a in angs], -1)
        sb = jnp.concatenate([jnp.concatenate([jnp.zeros_like(a), jnp.sin(a)], -1) for a in angs], -1)
        rep = LANES // HEAD_DIM
        return tuple(jnp.tile(x, (1, rep)) for x in (cos, sa, sb))

    att = head_tables([angles(rows, HEAD_DIM // 2), angles(cols, HEAD_DIM // 2)])
    ret = head_tables([angles(pos, HEAD_DIM)])
    signed_t = lambda tb: (tb[0][:, :HEAD_DIM].T, (tb[1] + tb[2])[:, :HEAD_DIM].T)

    def with_ctx(tb, axis, is_cos):
        shape = list(tb.shape)
        shape[axis] = n_ctx
        lead = jnp.ones(shape, F32) if is_cos else jnp.zeros(shape, F32)
        return jnp.concatenate([lead, tb], axis=axis)

    chan = signed_t(att) + signed_t(ret)
    toks = att + ret
    return (tuple(with_ctx(tb, 1, k % 2 == 0) for k, tb in enumerate(chan))
            + tuple(with_ctx(tb, 0, k % 3 == 0) for k, tb in enumerate(toks)))


def _s5_kernel(u_hbm, tab_ref, wsf_ref, wsb_ref, wrf_ref, wrb_ref, lam_ref, y_ref,
               lhs_ref, sem, m_ref, acc_ref, cf_ref, cb_ref, pf_ref, pb_ref, *, bsz, nz_chunks):
    r_blk, ch, tm = y_ref.shape
    c = S5_CHUNK
    kpb = tm // c
    n_chunks = (r_blk // bsz) * kpb
    g = pl.program_id(0)
    slot = g % 2

    def fetch(group, s):
        for i in range(ch):
            p, half = divmod(i, 2)
            for k in range(kpb):
                pltpu.make_async_copy(u_hbm.at[:, group * ch + i, pl.ds(k * c, c)],
                                      lhs_ref.at[s, p, pl.ds(k * r_blk, r_blk), pl.ds(half * c, c)],
                                      sem.at[s]).start()

    @pl.when(g == 0)
    def _():
        fetch(0, 0)

    @pl.when(g + 1 < pl.num_programs(0))
    def _():
        fetch(g + 1, 1 - slot)

    pltpu.make_async_copy(lhs_ref.at[1 - slot], lhs_ref.at[slot], sem.at[slot]).wait()
    acc_ref[...] = jnp.zeros_like(acc_ref)
    cf_ref[...] = jnp.zeros_like(cf_ref)
    cb_ref[...] = jnp.zeros_like(cb_ref)

    def pair(p, carry):
        for ih in range(2):
            for o in range(ch):
                lag_row = jnp.broadcast_to(tab_ref[2 * p + ih, o:o + 1, :], (16, 2 * c))
                tab = pltpu.roll(lag_row, 0, 1, stride=1, stride_axis=0).astype(BF16)
                for r2 in range(c // 16):
                    lo = c - 16 * r2
                    m_ref[ih * c + 16 * r2:ih * c + 16 * r2 + 16, o * c:(o + 1) * c] = tab[:, lo:lo + c]
        lhs = lhs_ref[slot, p].astype(BF16)
        acc_ref[...] += _dot(lhs, m_ref[...])
        cf_ref[...] += _dot(lhs, wsf_ref[p])
        cb_ref[...] += _dot(lhs, wsb_ref[p])
        return carry

    lax.fori_loop(0, ch // 2, pair, 0)
    lam = lam_ref[...]
    nst = lam.shape[1] // 2

    def advance(s, a, bc, drive):
        return s * a + pltpu.roll(s, nst, 1) * bc + drive

    def sweep(order, c_ref, p_ref, a, bc):
        s = jnp.zeros((bsz, lam.shape[1]), F32)
        for n in order:
            j, k = divmod(n, kpb)
            rows = slice(k * r_blk + j * bsz, k * r_blk + (j + 1) * bsz)
            p_ref[rows, :] = s
            s = advance(s, a, bc, c_ref[rows, :])

    sweep(list(range(n_chunks)), cf_ref, pf_ref, lam[0:1], lam[1:2])
    order_b = list(range(nz_chunks - 1, -1, -1)) + list(range(n_chunks - 1, nz_chunks - 1, -1))
    sweep(order_b, cb_ref, pb_ref, lam[2:3], lam[3:4])
    acc_ref[...] += (_dot(pf_ref[...].astype(BF16), wrf_ref[...])
                     + _dot(pb_ref[...].astype(BF16), wrb_ref[...]))
    for o in range(ch):
        for k in range(kpb):
            y_ref[:, o, k * c:(k + 1) * c] = acc_ref[k * r_blk:(k + 1) * r_blk, o * c:(o + 1) * c]


def _s5_weights(lam_re, lam_im, log_step, b_re, b_im, c_re, c_im, chunk):
    hp = lax.Precision.HIGHEST
    lam = lax.complex(lam_re.astype(F32), lam_im.astype(F32))
    lam_dt = lam * jnp.exp(log_step.astype(F32))[..., None]
    lam_bar = jnp.exp(lam_dt)
    b_bar = lax.complex(b_re.astype(F32), b_im.astype(F32)) * ((lam_bar - 1.0) / lam)[..., None]
    c_mat = lax.complex(c_re.astype(F32), c_im.astype(F32))
    g, n = lam.shape[1], lam.shape[2]
    ch = b_bar.shape[-1]
    steps = jnp.arange(chunk + 1, dtype=F32)
    pw = jnp.exp(steps[None, :, None, None] * lam_dt[:, None])
    kern = jnp.einsum('zgon,zdgn,zgni->zgdoi', c_mat, pw[:, :chunk], b_bar, precision=hp).real
    zero_lag = kern[0, :, 0] + kern[1, :, 0]
    lag_table = jnp.concatenate([jnp.zeros_like(zero_lag)[:, None], kern[1, :, :0:-1], zero_lag[:, None],
                                 kern[0, :, 1:]], axis=1).transpose(0, 3, 2, 1)

    def state_in(pw_s, b_dir):
        w = pw_s[:, :, :, None] * b_dir[None]
        w = jnp.concatenate([w.real, w.imag], axis=2)
        return w.transpose(1, 3, 0, 2).reshape(g, ch // 2, 2 * chunk, 2 * n)

    def state_out(pw_t, c_dir):
        w = c_dir[None] * pw_t[:, :, None, :]
        w = jnp.concatenate([w.real, -w.imag], axis=3)
        return w.transpose(1, 3, 2, 0).reshape(g, 2 * n, ch * chunk)

    wsf = state_in(pw[0, chunk - 1::-1][:chunk], b_bar[0])
    wsb = state_in(pw[1, :chunk], b_bar[1])
    wrf = state_out(pw[0, 1:chunk + 1], c_mat[0])
    wrb = state_out(pw[1, chunk:0:-1], c_mat[1])
    lam_c = pw[:, chunk]
    rows = []
    for z in range(2):
        rows.append(jnp.concatenate([lam_c[z].real, lam_c[z].real], -1))
        rows.append(jnp.concatenate([-lam_c[z].imag, lam_c[z].imag], -1))
    lam_rows = jnp.stack(rows + [jnp.zeros_like(rows[0])] * 4, axis=1)
    return lag_table, wsf.astype(BF16), wsb.astype(BF16), wrf.astype(BF16), wrb.astype(BF16), lam_rows


def _s5_scan(ut, weights, n_ctx):
    nblk, bsz, width, tm = ut.shape
    tab, wsf, wsb, wrf, wrb, lam_rows = weights
    g, ch = tab.shape[0], tab.shape[1]
    c = S5_CHUNK
    nst2 = wsf.shape[3]
    r_blk = nblk * bsz
    rows = r_blk * (tm // c)
    grp = lambda *shape: pl.BlockSpec((None,) + shape, lambda i: (i,) + (0,) * len(shape))
    tok = pl.BlockSpec((r_blk, ch, tm), lambda i: (0, i, 0))
    y = pl.pallas_call(
        functools.partial(_s5_kernel, bsz=bsz, nz_chunks=n_ctx // c),
        grid=(g,),
        in_specs=[pl.BlockSpec(memory_space=pl.ANY), grp(ch, ch, 2 * c), grp(ch // 2, 2 * c, nst2), grp(ch // 2, 2 * c, nst2),
                  grp(nst2, ch * c), grp(nst2, ch * c), grp(8, nst2)],
        out_specs=tok,
        out_shape=jax.ShapeDtypeStruct((r_blk, width, tm), F32),
        scratch_shapes=[pltpu.VMEM((2, ch // 2, rows, 2 * c), F32), pltpu.SemaphoreType.DMA((2,)),
                        pltpu.VMEM((2 * c, ch * c), BF16),
                        pltpu.VMEM((rows, ch * c), F32)] + [pltpu.VMEM((rows, nst2), F32)] * 4,
        compiler_params=_cparams(("arbitrary",)),
        name="s5_scan",
    )(ut.reshape(r_blk, width, tm), tab, wsf, wsb, wrf, wrb, lam_rows)
    return y.reshape(nblk, bsz, width, tm)


def _attn_kernel(sink_ref, q_ref, k_ref, v_ref, o_ref, s_ref, e_ref, *, n_ctx, t_len, q_off, kv_heads):
    qi = pl.program_id(1) + q_off
    nzb = n_ctx // ATT_BLOCK
    band = 3 * ATT_BLOCK
    wide = ATT_REP * ATT_BLOCK
    q = q_ref[...].astype(BF16)
    kc = k_ref[0:n_ctx, :].astype(BF16)
    vc = jnp.concatenate([v_ref[p] for p in range(nzb)], axis=1).astype(BF16)

    def rows(a, i):
        return a[i * HEAD_DIM:(i + 1) * HEAD_DIM]

    def q_group(g):
        qg = jnp.concatenate([rows(q, g * ATT_REP + r) for r in range(ATT_REP)], axis=1)
        zero = jnp.zeros_like(qg)
        return jnp.concatenate([qg if gg == g else zero for gg in range(kv_heads)], axis=0)

    ones_rows = 16

    def v_aug(v, g):
        return jnp.concatenate([rows(v, g), jnp.ones((ones_rows, v.shape[1]), BF16)], axis=0)

    def finish(g, o_aug, extra_den):
        o = o_aug[:HEAD_DIM] / (o_aug[HEAD_DIM:HEAD_DIM + 1] + extra_den)
        for r in range(ATT_REP):
            h = g * ATT_REP + r
            o_ref[h * HEAD_DIM:(h + 1) * HEAD_DIM, :] = o[:, r * ATT_BLOCK:(r + 1) * ATT_BLOCK].astype(o_ref.dtype)

    @pl.when(qi < nzb)
    def _():
        for g in range(kv_heads):
            sink = sink_ref[g:g + 1, :]
            s = _dot(kc, q_group(g))
            m = jnp.maximum(jnp.max(s, axis=0, keepdims=True), sink)
            e = jnp.exp2(s - m)
            finish(g, _dot(v_aug(vc, g), e.astype(BF16)), jnp.exp2(sink - m))

    @pl.when(qi >= nzb)
    def _():
        n = qi - nzb
        start = jnp.clip((n - 1) * ATT_BLOCK, 0, t_len - band)
        kl = k_ref[pl.ds(pl.multiple_of(n_ctx + start, ATT_BLOCK), band), :].astype(BF16)
        p0 = (n_ctx + start) // ATT_BLOCK
        vl = jnp.concatenate([v_ref[p0 + t] for t in range(3)], axis=1).astype(BF16)
        k_pos = start + lax.broadcasted_iota(jnp.int32, (band, wide), 0)
        q_pos = n * ATT_BLOCK + (lax.broadcasted_iota(jnp.int32, (band, wide), 1) & (ATT_BLOCK - 1))
        valid = jnp.abs(k_pos - q_pos) <= WINDOW
        v_all = jnp.concatenate([vl, vc], axis=1)
        n_keys = band + n_ctx
        row_max = []
        for g in range(kv_heads):
            qg = q_group(g)
            m = sink_ref[g:g + 1, :]
            for j in range(0, n_keys, ATT_BLOCK):
                if j < band:
                    s = jnp.where(valid[j:j + ATT_BLOCK], _dot(kl[j:j + ATT_BLOCK], qg), NEG_INF)
                else:
                    s = _dot(kc[j - band:j - band + ATT_BLOCK], qg)
                s_ref[g, j:j + ATT_BLOCK, :] = s
                m = jnp.maximum(m, jnp.max(s, axis=0, keepdims=True))
            row_max.append(m)
        for g in range(kv_heads):
            m = row_max[g]
            for j in range(0, n_keys, ATT_BLOCK):
                e_ref[g, j:j + ATT_BLOCK, :] = jnp.exp2(s_ref[g, j:j + ATT_BLOCK, :] - m).astype(BF16)
            finish(g, _dot(v_aug(v_all, g), e_ref[g]), jnp.exp2(sink_ref[g:g + 1, :] - m))


def _attention(qat, ka, vat, sink, n_ctx, need_ctx):
    nblk, bsz, qw, _ = qat.shape
    s, kvw = ka.shape[1], ka.shape[2]
    t_len = s - n_ctx
    kv_heads = kvw // HEAD_DIM
    q_off = 0 if need_ctx else n_ctx // ATT_BLOCK
    nq = nblk - q_off
    sink_rows = jnp.repeat(sink.astype(F32).reshape(kv_heads, ATT_REP), ATT_BLOCK, axis=1) * LOG2_E
    return pl.pallas_call(
        functools.partial(_attn_kernel, n_ctx=n_ctx, t_len=t_len, q_off=q_off, kv_heads=kv_heads),
        grid=(bsz, nq),
        in_specs=[pl.BlockSpec(sink_rows.shape, lambda b, j: (0, 0)),
                  pl.BlockSpec((None, None, qw, ATT_BLOCK), lambda b, j: (j + q_off, b, 0, 0)),
                  pl.BlockSpec((None, s, kvw), lambda b, j: (b, 0, 0)),
                  pl.BlockSpec((nblk, None, kvw, ATT_BLOCK), lambda b, j: (0, b, 0, 0))],
        out_specs=pl.BlockSpec((None, None, qw, ATT_BLOCK), lambda b, j: (j, b, 0, 0)),
        out_shape=jax.ShapeDtypeStruct((nq, bsz, qw, ATT_BLOCK), BF16),
        scratch_shapes=[pltpu.VMEM((kv_heads, 3 * ATT_BLOCK + n_ctx, ATT_REP * ATT_BLOCK), F32),
                        pltpu.VMEM((kv_heads, 3 * ATT_BLOCK + n_ctx, ATT_REP * ATT_BLOCK), BF16)],
        compiler_params=_cparams(("parallel", "arbitrary")),
        name="window_attention",
    )(sink_rows, qat, ka, vat)


def _ret_kernel(lg_ref, q_ref, k_ref, v_ref, g_ref, o_ref, acc_ref, sf_ref, sb_ref, *, nz_chunks, heads):
    c = RET_CHUNK
    n_chunks, w, _ = q_ref.shape
    f32 = lambda a: a.astype(F32)

    def per_head(shape, axis, group, direction):
        owner = lax.broadcasted_iota(jnp.int32, shape, axis) // group
        out = jnp.zeros(shape, F32)
        for h in range(heads):
            out = jnp.where(owner == h, lg_ref[direction, h], out)
        return out

    row_i = f32(lax.broadcasted_iota(jnp.int32, (w, c), 1))
    qw_f = jnp.exp(per_head((w, c), 0, HEAD_DIM, 0) * (row_i + 1.0))
    qw_b = jnp.exp(per_head((w, c), 0, HEAD_DIM, 1) * (c - row_i))
    key_j = f32(lax.broadcasted_iota(jnp.int32, (c, w), 0))
    kw_f = jnp.exp(per_head((c, w), 1, HEAD_DIM, 0) * (c - 1.0 - key_j))
    kw_b = jnp.exp(per_head((c, w), 1, HEAD_DIM, 1) * key_j)
    dec_f = jnp.exp(per_head((w, 1), 0, HEAD_DIM, 0) * c)
    dec_b = jnp.exp(per_head((w, 1), 0, HEAD_DIM, 1) * c)
    wide = heads * c
    diff = f32((lax.broadcasted_iota(jnp.int32, (c, wide), 1) & (c - 1)) - lax.broadcasted_iota(jnp.int32, (c, wide), 0))
    dec_t = jnp.where(diff >= 0, jnp.exp(per_head((c, wide), 1, c, 0) * jnp.maximum(diff, 0.0)),
                      jnp.exp(per_head((c, wide), 1, c, 1) * jnp.maximum(-diff, 0.0)))
    own_wide = (lax.broadcasted_iota(jnp.int32, (w, wide), 0) // HEAD_DIM
                == lax.broadcasted_iota(jnp.int32, (w, wide), 1) // c)
    own_sq = f32(lax.broadcasted_iota(jnp.int32, (w, w), 0) // HEAD_DIM
                 == lax.broadcasted_iota(jnp.int32, (w, w), 1) // HEAD_DIM)
    sf_ref[...] = jnp.zeros_like(sf_ref)
    sb_ref[...] = jnp.zeros_like(sb_ref)

    def block_diag(x):
        return jnp.where(own_wide, jnp.concatenate([x] * heads, axis=1), 0.0).astype(BF16)

    def fwd(n, carry):
        qt, vt = q_ref[n], v_ref[n]
        k = k_ref[pl.ds(pl.multiple_of(n * c, c), c), :]
        scores_t = _dot(k.astype(BF16), block_diag(qt)) * dec_t
        stacked = jnp.concatenate([scores_t[:, h * c:(h + 1) * c] for h in range(heads)], axis=0)
        o = _dot(block_diag(vt), stacked.astype(BF16))
        s_prev = sf_ref[...]
        o = o + _dot(s_prev.astype(BF16), (qt * qw_f).astype(BF16))
        sf_ref[...] = dec_f * s_prev + own_sq * _dot(vt.astype(BF16), (k * kw_f).astype(BF16))
        acc_ref[n] = o
        return carry

    lax.fori_loop(0, n_chunks, fwd, 0, unroll=6)

    def bwd(i, carry):
        n = jnp.where(i < nz_chunks, nz_chunks - 1 - i, n_chunks - 1 - i + nz_chunks)
        qt, vt = q_ref[n], v_ref[n]
        k = k_ref[pl.ds(pl.multiple_of(n * c, c), c), :]
        s_prev = sb_ref[...]
        o = acc_ref[n] + _dot(s_prev.astype(BF16), (qt * qw_b).astype(BF16))
        sb_ref[...] = dec_b * s_prev + own_sq * _dot(vt.astype(BF16), (k * kw_b).astype(BF16))
        gate = _silu(g_ref[n])
        for h in range(heads):
            rows = slice(h * HEAD_DIM, (h + 1) * HEAD_DIM)
            oh = o[rows]
            mu = jnp.mean(oh, axis=0, keepdims=True)
            oc = oh - mu
            var = jnp.mean(oc * oc, axis=0, keepdims=True)
            o_ref[n, rows, :] = (oc * lax.rsqrt(var + LN_EPS) * gate[rows]).astype(o_ref.dtype)
        return carry

    lax.fori_loop(0, n_chunks, bwd, 0, unroll=6)


def _retention(qrt, kr, vrt, grt, log_gamma, n_ctx):
    n_chunks, bsz, w, c = qrt.shape
    s = kr.shape[1]
    chan = pl.BlockSpec((n_chunks, None, w, c), lambda b: (0, b, 0, 0))
    return pl.pallas_call(
        functools.partial(_ret_kernel, nz_chunks=n_ctx // RET_CHUNK, heads=w // HEAD_DIM),
        grid=(bsz,),
        in_specs=[pl.BlockSpec(memory_space=pltpu.SMEM), chan, pl.BlockSpec((None, s, w), lambda b: (b, 0, 0)),
                  chan, chan],
        out_specs=chan,
        out_shape=jax.ShapeDtypeStruct((n_chunks, bsz, w, c), BF16),
        scratch_shapes=[pltpu.VMEM((n_chunks, w, c), F32), pltpu.VMEM((w, w), F32), pltpu.VMEM((w, w), F32)],
        compiler_params=_cparams(("parallel",)),
        name="retention",
    )(log_gamma.astype(F32), qrt, kr, vrt, grt)


def _outproj_kernel(*refs, n_stream, split, first_block, n_ctx, nzb, alpha, with_router, w_s5, w_att, n_exp):
    if with_router:
        (y_ref, u_ref, a_ref, r_ref, modx_ref, modz_ref, d_ref, wg_ref, bg_ref, wo_ref,
         g1_ref, b1_ref, rt_ref, x1_ref, fx_ref, gate_ref) = refs[n_stream:]
    else:
        (y_ref, u_ref, a_ref, r_ref, modx_ref, modz_ref, d_ref, wg_ref, bg_ref, wo_ref,
         g1_ref, b1_ref, x1_ref, fx_ref) = refs[n_stream:]
    (x_in,) = _stream_pieces(refs[:n_stream], pl.program_id(1) + first_block, n_ctx, split)
    is_ctx = pl.program_id(1) < nzb
    mod = jnp.where(is_ctx, modz_ref[...], modx_ref[...])
    g = _gelu_tanh(y_ref[...] + d_ref[...] * u_ref[...])
    s5 = g * _sigmoid(_dot(wg_ref[...], g.astype(BF16)) + bg_ref[...])
    def chan_major(ref, lo, hi):
        return jnp.concatenate([_dot_tn(ref[k].astype(BF16), wo_ref[lo:hi, :]) for k in range(ref.shape[0])],
                               axis=0)

    mix = (_dot_tn(s5.astype(BF16), wo_ref[0:w_s5, :]) + chan_major(a_ref, w_s5, w_s5 + w_att)
           + chan_major(r_ref, w_s5 + w_att, wo_ref.shape[0]))
    x1 = _ln(alpha * x_in + mod[2:3] * mix) * g1_ref[...] + b1_ref[...]
    x1_ref[...] = x1
    fx = _ln(x1) * (1.0 + mod[4:5]) + mod[3:4]
    fx_ref[...] = fx.astype(fx_ref.dtype)
    if with_router:
        lane = lax.broadcasted_iota(jnp.int32, (fx.shape[0], LANES), 1)
        logits = jnp.where(lane < n_exp, _dot3(fx, rt_ref[...]), -jnp.inf)
        m1 = jnp.max(logits, axis=1, keepdims=True)
        i1 = jnp.min(jnp.where(logits == m1, lane, LANES), axis=1, keepdims=True)
        rest = jnp.where(lane == i1, -jnp.inf, logits)
        m2 = jnp.max(rest, axis=1, keepdims=True)
        i2 = jnp.min(jnp.where(rest == m2, lane, LANES), axis=1, keepdims=True)
        e2 = jnp.exp(m2 - m1)
        den = 1.0 + e2
        route = jnp.where(lane == 0, 1.0 / den, jnp.where(lane == 1, e2 / den, 0.0))
        route = jnp.where(lane == 2, i1.astype(F32), jnp.where(lane == 3, i2.astype(F32), route))
        gate_ref[...] = route


def _outproj(y_s5, u, o_att, o_ret, x, ctx, mod, s5_d, w_glu, b_glu, w_out, ln_g, ln_b, router, n_ctx,
             need_ctx, alpha):
    bsz, d = x.shape[0], x.shape[2]
    s = x.shape[1] + (0 if ctx is None else ctx.shape[1])
    tm = TOKEN_BLOCK
    nzb = n_ctx // tm
    off = 0 if need_ctx else nzb
    nblk = s // tm - off
    s_out = nblk * tm
    nrow = mod.shape[0]
    w_s5, w_att = y_s5.shape[2], o_att.shape[2]
    sub = tm // ATT_BLOCK
    att_off = off if o_att.shape[0] == s // ATT_BLOCK else 0
    att = pl.BlockSpec((sub, None, w_att, ATT_BLOCK), lambda b, j: (j + att_off, b, 0, 0))

    def tok(width, shift):
        return pl.BlockSpec((None, tm, width), lambda b, j: (b, j + shift, 0))

    def full(a):
        return pl.BlockSpec(a.shape, lambda b, j: (0,) * a.ndim)

    per_in = y_s5.shape[3] // tm
    chan = pl.BlockSpec((None, None, w_s5, tm), lambda b, j: ((j + off) // per_in, b, 0, (j + off) % per_in))
    vec = lambda a: a.reshape(1, -1).astype(F32)
    col = lambda a: a.reshape(-1, 1).astype(F32)
    consts = [col(s5_d), w_glu.T.astype(BF16), col(b_glu), w_out.astype(BF16), vec(ln_g), vec(ln_b)]
    ret = pl.BlockSpec((sub, None, o_ret.shape[2], ATT_BLOCK), lambda b, j: (j + off, b, 0, 0))
    stream_specs, stream_args = _stream_specs(x, ctx, tm, first_block=off)
    in_specs = stream_specs + [chan, chan, att, ret,
                pl.BlockSpec((None, 6, d), lambda b, j: (b, 0, 0)),
                pl.BlockSpec((None, 6, d), lambda b, j: (nrow - 1, 0, 0))] + [full(a) for a in consts]
    out_specs = [tok(d, 0), tok(d, 0)]
    fx_dtype = BF16 if router is None else F32
    out_shape = [jax.ShapeDtypeStruct((bsz, s_out, d), F32), jax.ShapeDtypeStruct((bsz, s_out, d), fx_dtype)]
    args = stream_args + [y_s5, u, o_att, o_ret, mod, mod] + consts
    with_router = router is not None
    n_exp = 0
    if with_router:
        n_exp = router.shape[1]
        router_pad = jnp.pad(router.astype(F32), ((0, 0), (0, LANES - n_exp)))
        args.append(router_pad)
        in_specs.append(full(router_pad))
        out_specs.append(tok(LANES, 0))
        out_shape.append(jax.ShapeDtypeStruct((bsz, s_out, LANES), F32))
    return pl.pallas_call(
        functools.partial(_outproj_kernel, n_stream=len(stream_specs), split=ctx is not None, first_block=off,
                          n_ctx=n_ctx, nzb=nzb - off, alpha=alpha, with_router=with_router,
                          w_s5=w_s5, w_att=w_att, n_exp=n_exp),
        grid=(bsz, nblk),
        in_specs=in_specs,
        out_specs=out_specs,
        out_shape=out_shape,
        compiler_params=_cparams(("parallel", "parallel")),
        name="outproj",
    )(*args)


def _swiglu_into(x_bf16, w1_ref, w3_ref, w2_ref, acc_ref, fc):
    for s in range(0, w1_ref.shape[1], fc):
        h1 = _dot(x_bf16, w1_ref[:, s:s + fc])
        h3 = _dot(x_bf16, w3_ref[:, s:s + fc])
        acc_ref[...] += _dot((_silu(h1) * h3).astype(BF16), w2_ref[s:s + fc, :])


def _ffn_kernel(fx_ref, x1_ref, modx_ref, modz_ref, w1_ref, w3_ref, w2_ref, g2_ref, b2_ref,
                o_ref, acc_ref, *, n_ctx_tokens, alpha, fc):
    tm = acc_ref.shape[0]
    acc_ref[...] = jnp.zeros_like(acc_ref)
    _swiglu_into(fx_ref[...], w1_ref, w3_ref, w2_ref, acc_ref, fc)
    row = pl.program_id(1) * tm + lax.broadcasted_iota(jnp.int32, (tm, 1), 0)
    gate = jnp.where(row < n_ctx_tokens, modz_ref[5:6, :], modx_ref[5:6, :])
    o_ref[...] = _ln(alpha * x1_ref[...] + gate * acc_ref[...]) * g2_ref[...] + b2_ref[...]


def _ffn(fx, x1, mod, w1, w3, w2, ln_g, ln_b, n_ctx_tokens, alpha):
    bsz, s, d = x1.shape
    ff = w1.shape[1]
    tm = _pick_chunk(s, FFN_BLOCK)
    nrow = mod.shape[0]
    tok = pl.BlockSpec((None, tm, d), lambda b, j: (b, j, 0))
    full = lambda a: pl.BlockSpec(a.shape, lambda b, j: (0,) * a.ndim, pipeline_mode=pl.Buffered(1))
    vec = lambda a: a.reshape(1, -1).astype(F32)
    return pl.pallas_call(
        functools.partial(_ffn_kernel, n_ctx_tokens=n_ctx_tokens, alpha=alpha, fc=_pick_chunk(ff, FFN_CHUNK)),
        grid=(bsz, s // tm),
        in_specs=[tok, tok,
                  pl.BlockSpec((None, 6, d), lambda b, j: (b, 0, 0)),
                  pl.BlockSpec((None, 6, d), lambda b, j: (nrow - 1, 0, 0)),
                  full(w1), full(w3), full(w2),
                  pl.BlockSpec((1, d), lambda b, j: (0, 0)), pl.BlockSpec((1, d), lambda b, j: (0, 0))],
        out_specs=tok,
        out_shape=jax.ShapeDtypeStruct((bsz, s, d), F32),
        scratch_shapes=[pltpu.VMEM((tm, d), F32)],
        compiler_params=_cparams(("parallel", "parallel")),
        name="dense_ffn",
    )(fx, x1, mod, mod, w1, w3, w2, vec(ln_g), vec(ln_b))


MOE_TILE = 1024


def _route_plan(e1, e2, n_exp, tile):
    n = e1.shape[0]
    pair_e = jnp.stack([e1, e2], axis=1).reshape(-1)
    onehot = (pair_e[:, None] == jnp.arange(n_exp, dtype=jnp.int32)[None, :]).astype(jnp.int32)
    before = jnp.cumsum(onehot, axis=0) - onehot
    rank = jnp.sum(before * onehot, axis=1)
    counts = jnp.sum(onehot, axis=0)
    padded = (counts + tile - 1) // tile * tile
    ends = jnp.cumsum(padded)
    starts = ends - padded
    dest = starts[pair_e] + rank
    n_rows = (2 * n + n_exp * (tile - 1)) // tile * tile
    n_tiles = n_rows // tile
    tile_start = jnp.arange(n_tiles, dtype=jnp.int32) * tile
    tile_expert = jnp.minimum(jnp.sum((tile_start[:, None] >= ends[None, :]).astype(jnp.int32), axis=1),
                              n_exp - 1)
    by_expert = jnp.sort(pair_e * (2 * n) + jnp.arange(2 * n, dtype=jnp.int32)) % (2 * n)
    row = jnp.arange(n_rows, dtype=jnp.int32)
    row_e = jnp.repeat(tile_expert, tile)
    in_group = row - starts[row_e]
    src = jnp.minimum(in_group + (jnp.cumsum(counts) - counts)[row_e], 2 * n - 1)
    row_token = jnp.where(in_group < counts[row_e], by_expert[src] // 2, 0)
    n_used = (ends[-1] // tile).astype(jnp.int32).reshape(1)
    return row_token, dest.reshape(n, 2), tile_expert, n_used


def _moe_gemm_kernel(te_ref, nused_ref, tok_ref, tokn_ref, x_hbm, w1_ref, w3_ref, w2_ref, y_ref,
                     xbuf_ref, sem, *, fc, tile):
    i = pl.program_id(0)
    f = pl.program_id(1)
    slot = i % 2
    n_used = nused_ref[0]

    def row_copy(t_ref, r, s):
        return pltpu.make_async_copy(x_hbm.at[pl.ds(t_ref[0, r], 1)], xbuf_ref.at[s, pl.ds(r, 1)], sem.at[s])

    def issue(t_ref, s):
        def body(it, carry):
            base = pl.multiple_of(it * 8, 8)
            for k in range(8):
                row_copy(t_ref, base + k, s).start()
            return carry
        lax.fori_loop(0, tile // 8, body, 0)

    @pl.when(f == 0)
    def _():
        y_ref[...] = jnp.zeros_like(y_ref)

        @pl.when(i == 0)
        def _():
            issue(tok_ref, 0)

        @pl.when(i + 1 < n_used)
        def _():
            issue(tokn_ref, 1 - slot)

        @pl.when(i < n_used)
        def _():
            pltpu.make_async_copy(x_hbm.at[pl.ds(0, tile)], xbuf_ref.at[slot], sem.at[slot]).wait()

    @pl.when(i < n_used)
    def _():
        _swiglu_into(xbuf_ref[slot].astype(BF16), w1_ref, w3_ref, w2_ref, y_ref, fc)


def _moe_gemm(x_flat, row_token, tile_expert, n_used, w1, w3, w2, tile):
    n_rows = row_token.shape[0]
    n_tiles = n_rows // tile
    d = x_flat.shape[1]
    ff = w1.shape[2]
    nf = 2 if ff % (2 * LANES) == 0 else 1
    tf = ff // nf
    last = lambda i, nu: jnp.minimum(i, nu[0] - 1)
    fsel = lambda i, f, nu: jnp.where(i < nu[0], f, nf - 1)
    tok = lambda imap: pl.BlockSpec((None, 1, tile), imap, memory_space=pltpu.SMEM)
    tokens = row_token.reshape(n_tiles, 1, tile)
    return pl.pallas_call(
        functools.partial(_moe_gemm_kernel, fc=_pick_chunk(tf, MOE_CHUNK), tile=tile),
        grid_spec=pltpu.PrefetchScalarGridSpec(
            num_scalar_prefetch=2,
            grid=(n_tiles, nf),
            in_specs=[tok(lambda i, f, te, nu: (i, 0, 0)),
                      tok(lambda i, f, te, nu: (jnp.minimum(i + 1, n_tiles - 1), 0, 0)),
                      pl.BlockSpec(memory_space=pl.ANY),
                      pl.BlockSpec((None, d, tf), lambda i, f, te, nu: (te[last(i, nu)], 0, fsel(i, f, nu))),
                      pl.BlockSpec((None, d, tf), lambda i, f, te, nu: (te[last(i, nu)], 0, fsel(i, f, nu))),
                      pl.BlockSpec((None, tf, d), lambda i, f, te, nu: (te[last(i, nu)], fsel(i, f, nu), 0))],
            out_specs=pl.BlockSpec((tile, d), lambda i, f, te, nu: (i, 0)),
            scratch_shapes=[pltpu.VMEM((2, tile, d), F32), pltpu.SemaphoreType.DMA((2,))]),
        out_shape=jax.ShapeDtypeStruct((n_rows, d), F32),
        compiler_params=_cparams(("arbitrary", "arbitrary")),
        name="moe_gemm",
    )(tile_expert, n_used, tokens, tokens, x_flat, w1, w3, w2)


def _moe_combine_kernel(pos_ref, posn_ref, route_ref, x1_ref, modx_ref, modz_ref, g2_ref, b2_ref, y_hbm,
                        o_ref, buf_ref, sem, *, nzb, nblk, alpha, tm):
    i = pl.program_id(0)
    slot = i % 2

    def row_copy(p_ref, r, k, s):
        return pltpu.make_async_copy(y_hbm.at[pl.ds(p_ref[0, k * tm + r], 1)],
                                     buf_ref.at[s, pl.ds(k * tm + r, 1)], sem.at[s])

    def issue(p_ref, s):
        def body(r, carry):
            row_copy(p_ref, r, 0, s).start(priority=0)
            row_copy(p_ref, r, 1, s).start(priority=1)
            return carry
        lax.fori_loop(0, tm, body, 0, unroll=8)

    @pl.when(i == 0)
    def _():
        issue(pos_ref, 0)

    @pl.when(i + 1 < pl.num_programs(0))
    def _():
        issue(posn_ref, 1 - slot)

    pltpu.make_async_copy(y_hbm.at[pl.ds(0, 2 * tm)], buf_ref.at[slot], sem.at[slot]).wait()
    route = route_ref[...]
    f = route[:, 0:1] * buf_ref[slot, 0:tm] + route[:, 1:2] * buf_ref[slot, tm:2 * tm]
    is_ctx = (i % nblk) < nzb
    mod = jnp.where(is_ctx, modz_ref[...], modx_ref[...])
    o_ref[...] = _ln(alpha * x1_ref[...] + mod[5:6] * f) * g2_ref[...] + b2_ref[...]


def _moe_combine(ys, pos, route, x1, mod, ln_g, ln_b, n_ctx_tokens, alpha):
    bsz, s, d = x1.shape
    wide_ok = s % COMBINE_BLOCK == 0 and n_ctx_tokens % COMBINE_BLOCK == 0
    tm = COMBINE_BLOCK if wide_ok else TOKEN_BLOCK
    nblk = s // tm
    n_steps = bsz * nblk
    nrow = mod.shape[0]
    pos_steps = pos.reshape(n_steps, tm, 2).transpose(0, 2, 1).reshape(n_steps, 1, 2 * tm)
    tok = lambda width: pl.BlockSpec((tm, width), lambda i: (i, 0))
    smem = lambda imap: pl.BlockSpec((None, 1, 2 * tm), imap, memory_space=pltpu.SMEM)
    vec = lambda a: a.reshape(1, -1).astype(F32)
    out = pl.pallas_call(
        functools.partial(_moe_combine_kernel, nzb=n_ctx_tokens // tm, nblk=nblk, alpha=alpha, tm=tm),
        grid=(n_steps,),
        in_specs=[smem(lambda i: (i, 0, 0)),
                  smem(lambda i: (jnp.minimum(i + 1, n_steps - 1), 0, 0)),
                  tok(LANES), tok(d),
                  pl.BlockSpec((None, 6, d), lambda i: (i // nblk, 0, 0)),
                  pl.BlockSpec((None, 6, d), lambda i: (nrow - 1, 0, 0)),
                  pl.BlockSpec((1, d), lambda i: (0, 0)), pl.BlockSpec((1, d), lambda i: (0, 0)),
                  pl.BlockSpec(memory_space=pl.ANY)],
        out_specs=tok(d),
        out_shape=jax.ShapeDtypeStruct((bsz * s, d), F32),
        scratch_shapes=[pltpu.VMEM((2, 2 * tm, d), F32), pltpu.SemaphoreType.DMA((2,))],
        compiler_params=_cparams(("arbitrary",)),
        name="moe_combine",
    )(pos_steps, pos_steps, route.reshape(bsz * s, LANES), x1.reshape(bsz * s, d), mod, mod,
      vec(ln_g), vec(ln_b), ys)
    return out.reshape(bsz, s, d)


def _moe(fx, x1, route, mod, w1, w3, w2, ln_g, ln_b, n_ctx_tokens, alpha):
    bsz, s, d = x1.shape
    n_exp = w1.shape[0]
    idx = route.reshape(bsz * s, LANES)[:, 2:4].astype(jnp.int32)
    row_token, pos, tile_expert, n_used = _route_plan(idx[:, 0], idx[:, 1], n_exp, MOE_TILE)
    ys = _moe_gemm(fx.reshape(bsz * s, d), row_token, tile_expert, n_used, w1, w3, w2, MOE_TILE)
    return _moe_combine(ys, pos, route, x1, mod, ln_g, ln_b, n_ctx_tokens, alpha)


def kernel(x, c, ctx, c_ctx, w_mod, b_mod, w_in, s5_lam_re, s5_lam_im, s5_log_step, s5_b_re, s5_b_im,
           s5_c_re, s5_c_im, s5_d, s5_w_glu, s5_b_glu, attn_sink, ret_log_gamma, w_out,
           ln1_g, ln1_b, ln2_g, ln2_b, ffn_w1, ffn_w3, ffn_w2, moe_router, moe_w1, moe_w3, moe_w2):
    bsz, t_len, d = x.shape
    n_ctx = ctx.shape[1]
    depth = w_in.shape[0]
    alpha = (2 * depth) ** 0.25
    s5_w = s5_d.shape[1]
    att_w = attn_sink.shape[1] * HEAD_DIM
    kv_w = att_w // ATT_REP
    ret_w = ret_log_gamma.shape[2] * HEAD_DIM
    sizes = (s5_w, att_w, kv_w, kv_w, ret_w, ret_w, ret_w, ret_w)
    assert sum(sizes) == w_in.shape[2] and s5_w + att_w + ret_w == w_out.shape[1]
    assert n_ctx % TOKEN_BLOCK == 0 and t_len % TOKEN_BLOCK == 0 and t_len >= 3 * ATT_BLOCK

    pad = (-(bsz + 1)) % 8
    cvec = jnp.concatenate([jnp.zeros((pad, d), F32), c_ctx[None].astype(F32)], axis=0)
    cvec = jnp.concatenate([c.astype(F32), cvec], axis=0)
    mod_all = _modulation(cvec, w_mod.astype(F32), b_mod.astype(F32)).reshape(depth, bsz + pad + 1, 6, d)

    tabs = _rope_tables(t_len, n_ctx)
    xz, ctx_in = x.astype(F32), ctx.astype(F32)
    for l in range(depth):
        need_ctx = l < depth - 1
        mod = mod_all[l]
        u, qa, va, qr, vr, gr, ka, kr = _inproj(xz, ctx_in, mod, w_in[l].astype(BF16), tabs, sizes, n_ctx)
        s5w = _s5_weights(s5_lam_re[l], s5_lam_im[l], s5_log_step[l], s5_b_re[l], s5_b_im[l],
                          s5_c_re[l], s5_c_im[l], S5_CHUNK)
        y_s5 = _s5_scan(u, s5w, n_ctx)
        o_att = _attention(qa, ka, va, attn_sink[l], n_ctx, need_ctx)
        o_ret = _retention(qr, kr, vr, gr, ret_log_gamma[l], n_ctx)
        i = l // 2
        router = None if l % 2 == 0 else moe_router[i]
        outs = _outproj(y_s5, u, o_att, o_ret, xz, ctx_in, mod, s5_d[l], s5_w_glu[l], s5_b_glu[l], w_out[l],
                        ln1_g[l], ln1_b[l], router, n_ctx, need_ctx, alpha)
        ctx_in = None
        ctx_tokens = n_ctx if need_ctx else 0
        if l % 2 == 0:
            x1, fx = outs
            xz = _ffn(fx, x1, mod, ffn_w1[i].astype(BF16), ffn_w3[i].astype(BF16), ffn_w2[i].astype(BF16),
                      ln2_g[l], ln2_b[l], ctx_tokens, alpha)
        else:
            x1, fx, route = outs
            xz = _moe(fx, x1, route, mod, moe_w1[i].astype(BF16), moe_w3[i].astype(BF16),
                      moe_w2[i].astype(BF16), ln2_g[l], ln2_b[l], ctx_tokens, alpha)
    return xz if xz.shape[1] == t_len else xz[:, n_ctx:]
```
